```python
import math
import jax, jax.numpy as jnp
from jax import lax
import numpy as np

D_MODEL = 2048
BATCH = 8
SEQ = 4096
DEPTH = 2

D_MIX = D_MODEL
D_A = D_MIX // 2
D_B = D_MIX - D_A
A_GROUPS = 16
B_HEADS = 16
B_HEAD_DIM = D_B // B_HEADS
CONV_A_WIDTH = 3
CONV_B_WIDTH = 4
LRU_C = 8.0
D_IN_EVEN = 3 * D_A + 2 * D_B
SB_HEADS = 16
SB_HEAD_DIM = D_MODEL // SB_HEADS
Q_BLOCK = 128
D_FF = 4 * D_MODEL
NORM_EPS = 1e-6
N_EVEN = (DEPTH + 1) // 2
N_ODD = DEPTH // 2

kernel_name = "hybrid_conv_rglru_stickbreak_block"


def rms_norm(x, g):
    xf = x.astype(jnp.float32)
    y = xf * lax.rsqrt(jnp.mean(xf * xf, axis=-1, keepdims=True) + NORM_EPS)
    return (y * g.astype(jnp.float32)).astype(x.dtype)


def causal_dwconv(x, w, bias=None):
    k_width = w.shape[0]
    s = x.shape[1]
    xp = jnp.pad(x, ((0, 0), (k_width - 1, 0), (0, 0)))
    y = w[k_width - 1] * x
    for k in range(k_width - 1):
        y = y + w[k] * xp[:, k:k + s]
    if bias is not None:
        y = y + bias
    return y


def rg_lru(x, w_a, b_a, w_x, b_x, lam):
    bsz, s, _ = x.shape
    xf = x.astype(jnp.float32)
    xh = xf.reshape(bsz, s, B_HEADS, B_HEAD_DIM)
    r = jax.nn.sigmoid(jnp.einsum('bshi,hij->bshj', xh, w_a.astype(jnp.float32)).reshape(bsz, s, D_B)
                       + b_a.astype(jnp.float32))
    i = jax.nn.sigmoid(jnp.einsum('bshi,hij->bshj', xh, w_x.astype(jnp.float32)).reshape(bsz, s, D_B)
                       + b_x.astype(jnp.float32))
    log_a = LRU_C * r * jax.nn.log_sigmoid(lam.astype(jnp.float32))
    a = jnp.exp(log_a)
    mult = jnp.sqrt(-jnp.expm1(2.0 * log_a))
    b = mult * (i * xf)

    def combine(e1, e2):
        a1, b1 = e1
        a2, b2 = e2
        return a1 * a2, a2 * b1 + b2

    _, h = lax.associative_scan(combine, (a, b), axis=1)
    return h.astype(x.dtype)


def conv_lru_mixer(h, w_in, conv_a, conv_b, conv_b_bias, rg_w_a, rg_b_a, rg_w_x, rg_b_x, rg_lambda, w_out):
    proj = h @ w_in
    a_bgate, a_cgate, a_x, b_gate, b_x = jnp.split(
        proj, [D_A, 2 * D_A, 3 * D_A, 3 * D_A + D_B], axis=-1)
    y_a = a_bgate * causal_dwconv(a_cgate * a_x, conv_a)
    xr = causal_dwconv(b_x, conv_b, conv_b_bias)
    y_b = rg_lru(xr, rg_w_a, rg_b_a, rg_w_x, rg_b_x, rg_lambda) * jax.nn.gelu(b_gate, approximate=True)
    return jnp.concatenate([y_a, y_b], axis=-1) @ w_out


def stick_breaking_attention(h, w_qkv, w_o):
    bsz, s, _ = h.shape
    qkv = h @ w_qkv
    q, k, v = jnp.split(qkv, 3, axis=-1)
    to_heads = lambda t: t.reshape(bsz, s, SB_HEADS, SB_HEAD_DIM).transpose(0, 2, 1, 3)
    q, k, v = to_heads(q), to_heads(k), to_heads(v)
    n_blocks = s // Q_BLOCK
    q_blocks = q.reshape(bsz, SB_HEADS, n_blocks, Q_BLOCK, SB_HEAD_DIM).transpose(2, 0, 1, 3, 4)
    starts = jnp.arange(n_blocks, dtype=jnp.int32) * Q_BLOCK
    scale = 1.0 / math.sqrt(SB_HEAD_DIM)
    kf = k.astype(jnp.float32)
    vf = v.astype(jnp.float32)
    key_pos = jnp.arange(s, dtype=jnp.int32)[None, :]

    def block(args):
        q_blk, start = args
        z = jnp.einsum('bhqd,bhkd->bhqk', q_blk.astype(jnp.float32), kf) * scale
        q_pos = start + jnp.arange(Q_BLOCK, dtype=jnp.int32)[:, None]
        causal = key_pos < q_pos
        log_not = jnp.where(causal, jax.nn.log_sigmoid(-z), 0.0)
        suffix = lax.cumsum(log_not, axis=3, reverse=True) - log_not
        w = jnp.where(causal, jnp.exp(jax.nn.log_sigmoid(z) + suffix), 0.0)
        return jnp.einsum('bhqk,bhkd->bhqd', w, vf)

    out = lax.map(block, (q_blocks, starts))
    out = out.transpose(1, 0, 3, 2, 4).reshape(bsz, s, D_MODEL).astype(h.dtype)
    return out @ w_o


def sq_relu_mlp(h, w_up, w_down):
    u = jax.nn.relu(h @ w_up)
    return (u * u) @ w_down


def _fwd_setup_inputs(seed: int = 0) -> dict:
    key = jax.random.key(seed)
    ks = jax.random.split(key, 20)
    nrm = lambda k, shape, fan_in: jax.random.normal(k, shape, jnp.float32) * (fan_in ** -0.5)
    x = jax.random.normal(ks[0], (BATCH, SEQ, D_MODEL), jnp.float32)
    norm_gains = 1.0 + 0.05 * jax.random.normal(ks[1], (DEPTH, 4, D_MODEL), jnp.float32)
    hyb_w_in = nrm(ks[2], (N_EVEN, D_MODEL, D_IN_EVEN), D_MODEL)
    hyb_conv_a = nrm(ks[3], (N_EVEN, CONV_A_WIDTH, D_A), CONV_A_WIDTH)
    hyb_conv_b = nrm(ks[4], (N_EVEN, CONV_B_WIDTH, D_B), CONV_B_WIDTH)
    hyb_conv_b_bias = 0.02 * jax.random.normal(ks[5], (N_EVEN, D_B), jnp.float32)
    hyb_rg_w_a = nrm(ks[6], (N_EVEN, B_HEADS, B_HEAD_DIM, B_HEAD_DIM), B_HEAD_DIM)
    hyb_rg_b_a = 0.02 * jax.random.normal(ks[7], (N_EVEN, D_B), jnp.float32)
    hyb_rg_w_x = nrm(ks[8], (N_EVEN, B_HEADS, B_HEAD_DIM, B_HEAD_DIM), B_HEAD_DIM)
    hyb_rg_b_x = 0.02 * jax.random.normal(ks[9], (N_EVEN, D_B), jnp.float32)
    u = jax.random.uniform(ks[10], (N_EVEN, D_B), jnp.float32, 0.9, 0.999)
    sig = u ** (1.0 / LRU_C)
    hyb_rg_lambda = jnp.log(sig) - jnp.log1p(-sig)
    hyb_w_out = nrm(ks[11], (N_EVEN, D_MIX, D_MODEL), D_MIX)
    sb_w_qkv = nrm(ks[12], (N_ODD, D_MODEL, 3 * D_MODEL), D_MODEL)
    sb_w_o = nrm(ks[13], (N_ODD, D_MODEL, D_MODEL), D_MODEL)
    mlp_w_up = nrm(ks[14], (DEPTH, D_MODEL, D_FF), D_MODEL)
    mlp_w_down = nrm(ks[15], (DEPTH, D_FF, D_MODEL), D_FF)
    return {"x": x, "norm_gains": norm_gains, "hyb_w_in": hyb_w_in, "hyb_conv_a": hyb_conv_a,
            "hyb_conv_b": hyb_conv_b, "hyb_conv_b_bias": hyb_conv_b_bias, "hyb_rg_w_a": hyb_rg_w_a,
            "hyb_rg_b_a": hyb_rg_b_a, "hyb_rg_w_x": hyb_rg_w_x, "hyb_rg_b_x": hyb_rg_b_x,
            "hyb_rg_lambda": hyb_rg_lambda, "hyb_w_out": hyb_w_out, "sb_w_qkv": sb_w_qkv,
            "sb_w_o": sb_w_o, "mlp_w_up": mlp_w_up, "mlp_w_down": mlp_w_down}


def _fwd_reference(x, norm_gains, hyb_w_in, hyb_conv_a, hyb_conv_b, hyb_conv_b_bias, hyb_rg_w_a, hyb_rg_b_a,
              hyb_rg_w_x, hyb_rg_b_x, hyb_rg_lambda, hyb_w_out, sb_w_qkv, sb_w_o, mlp_w_up, mlp_w_down):
    for layer in range(DEPTH):
        g = norm_gains[layer]
        h = rms_norm(x, g[0])
        if layer % 2 == 0:
            e = layer // 2
            mix = conv_lru_mixer(h, hyb_w_in[e], hyb_conv_a[e], hyb_conv_b[e], hyb_conv_b_bias[e],
                                 hyb_rg_w_a[e], hyb_rg_b_a[e], hyb_rg_w_x[e], hyb_rg_b_x[e],
                                 hyb_rg_lambda[e], hyb_w_out[e])
        else:
            o = layer // 2
            mix = stick_breaking_attention(h, sb_w_qkv[o], sb_w_o[o])
        x = x + rms_norm(mix, g[1])
        h = rms_norm(x, g[2])
        x = x + rms_norm(sq_relu_mlp(h, mlp_w_up[layer], mlp_w_down[layer]), g[3])
    return x


import jax as _jax
import jax.numpy as _jnp

TWIN_FORMAT = 'train_step'
FWD_PARAMS = ['x', 'norm_gains', 'hyb_w_in', 'hyb_conv_a', 'hyb_conv_b', 'hyb_conv_b_bias', 'hyb_rg_w_a', 'hyb_rg_b_a', 'hyb_rg_w_x', 'hyb_rg_b_x', 'hyb_rg_lambda', 'hyb_w_out', 'sb_w_qkv', 'sb_w_o', 'mlp_w_up', 'mlp_w_down']
TWIN_WEIGHTS = ['norm_gains', 'hyb_w_in', 'hyb_conv_a', 'hyb_conv_b', 'hyb_conv_b_bias', 'hyb_rg_w_a', 'hyb_rg_b_a', 'hyb_rg_w_x', 'hyb_rg_b_x', 'hyb_rg_lambda', 'hyb_w_out', 'sb_w_qkv', 'sb_w_o', 'mlp_w_up', 'mlp_w_down']
TWIN_DIFF_INPUT = 'x'
TWIN_INPUTS = ['x', 'norm_gains', 'hyb_w_in', 'hyb_conv_a', 'hyb_conv_b', 'hyb_conv_b_bias', 'hyb_rg_w_a', 'hyb_rg_b_a', 'hyb_rg_w_x', 'hyb_rg_b_x', 'hyb_rg_lambda', 'hyb_w_out', 'sb_w_qkv', 'sb_w_o', 'mlp_w_up', 'mlp_w_down', 'loss_target', 'm_norm_gains', 'm_hyb_w_in', 'm_hyb_conv_a', 'm_hyb_conv_b', 'm_hyb_conv_b_bias', 'm_hyb_rg_w_a', 'm_hyb_rg_b_a', 'm_hyb_rg_w_x', 'm_hyb_rg_b_x', 'm_hyb_rg_lambda', 'm_hyb_w_out', 'm_sb_w_qkv', 'm_sb_w_o', 'm_mlp_w_up', 'm_mlp_w_down', 'v_norm_gains', 'v_hyb_w_in', 'v_hyb_conv_a', 'v_hyb_conv_b', 'v_hyb_conv_b_bias', 'v_hyb_rg_w_a', 'v_hyb_rg_b_a', 'v_hyb_rg_w_x', 'v_hyb_rg_b_x', 'v_hyb_rg_lambda', 'v_hyb_w_out', 'v_sb_w_qkv', 'v_sb_w_o', 'v_mlp_w_up', 'v_mlp_w_down']
TWIN_OUTPUTS = ['loss', 'grad_x', 'grad_norm_gains', 'grad_hyb_w_in', 'grad_hyb_conv_a', 'grad_hyb_conv_b', 'grad_hyb_conv_b_bias', 'grad_hyb_rg_w_a', 'grad_hyb_rg_b_a', 'grad_hyb_rg_w_x', 'grad_hyb_rg_b_x', 'grad_hyb_rg_lambda', 'grad_hyb_w_out', 'grad_sb_w_qkv', 'grad_sb_w_o', 'grad_mlp_w_up', 'grad_mlp_w_down', 'delta_norm_gains', 'delta_hyb_w_in', 'delta_hyb_conv_a', 'delta_hyb_conv_b', 'delta_hyb_conv_b_bias', 'delta_hyb_rg_w_a', 'delta_hyb_rg_b_a', 'delta_hyb_rg_w_x', 'delta_hyb_rg_b_x', 'delta_hyb_rg_lambda', 'delta_hyb_w_out', 'delta_sb_w_qkv', 'delta_sb_w_o', 'delta_mlp_w_up', 'delta_mlp_w_down', 'new_m_norm_gains', 'new_m_hyb_w_in', 'new_m_hyb_conv_a', 'new_m_hyb_conv_b', 'new_m_hyb_conv_b_bias', 'new_m_hyb_rg_w_a', 'new_m_hyb_rg_b_a', 'new_m_hyb_rg_w_x', 'new_m_hyb_rg_b_x', 'new_m_hyb_rg_lambda', 'new_m_hyb_w_out', 'new_m_sb_w_qkv', 'new_m_sb_w_o', 'new_m_mlp_w_up', 'new_m_mlp_w_down', 'new_v_norm_gains', 'new_v_hyb_w_in', 'new_v_hyb_conv_a', 'new_v_hyb_conv_b', 'new_v_hyb_conv_b_bias', 'new_v_hyb_rg_w_a', 'new_v_hyb_rg_b_a', 'new_v_hyb_rg_w_x', 'new_v_hyb_rg_b_x', 'new_v_hyb_rg_lambda', 'new_v_hyb_w_out', 'new_v_sb_w_qkv', 'new_v_sb_w_o', 'new_v_mlp_w_up', 'new_v_mlp_w_down']
TWIN_LEAF_KINDS = {'loss': 'loss', 'grad_x': 'grad_x', 'grad_norm_gains': 'grad_w', 'grad_hyb_w_in': 'grad_w', 'grad_hyb_conv_a': 'grad_w', 'grad_hyb_conv_b': 'grad_w', 'grad_hyb_conv_b_bias': 'grad_w', 'grad_hyb_rg_w_a': 'grad_w', 'grad_hyb_rg_b_a': 'grad_w', 'grad_hyb_rg_w_x': 'grad_w', 'grad_hyb_rg_b_x': 'grad_w', 'grad_hyb_rg_lambda': 'grad_w', 'grad_hyb_w_out': 'grad_w', 'grad_sb_w_qkv': 'grad_w', 'grad_sb_w_o': 'grad_w', 'grad_mlp_w_up': 'grad_w', 'grad_mlp_w_down': 'grad_w', 'delta_norm_gains': 'delta_w', 'delta_hyb_w_in': 'delta_w', 'delta_hyb_conv_a': 'delta_w', 'delta_hyb_conv_b': 'delta_w', 'delta_hyb_conv_b_bias': 'delta_w', 'delta_hyb_rg_w_a': 'delta_w', 'delta_hyb_rg_b_a': 'delta_w', 'delta_hyb_rg_w_x': 'delta_w', 'delta_hyb_rg_b_x': 'delta_w', 'delta_hyb_rg_lambda': 'delta_w', 'delta_hyb_w_out': 'delta_w', 'delta_sb_w_qkv': 'delta_w', 'delta_sb_w_o': 'delta_w', 'delta_mlp_w_up': 'delta_w', 'delta_mlp_w_down': 'delta_w', 'new_m_norm_gains': 'new_m', 'new_m_hyb_w_in': 'new_m', 'new_m_hyb_conv_a': 'new_m', 'new_m_hyb_conv_b': 'new_m', 'new_m_hyb_conv_b_bias': 'new_m', 'new_m_hyb_rg_w_a': 'new_m', 'new_m_hyb_rg_b_a': 'new_m', 'new_m_hyb_rg_w_x': 'new_m', 'new_m_hyb_rg_b_x': 'new_m', 'new_m_hyb_rg_lambda': 'new_m', 'new_m_hyb_w_out': 'new_m', 'new_m_sb_w_qkv': 'new_m', 'new_m_sb_w_o': 'new_m', 'new_m_mlp_w_up': 'new_m', 'new_m_mlp_w_down': 'new_m', 'new_v_norm_gains': 'new_v', 'new_v_hyb_w_in': 'new_v', 'new_v_hyb_conv_a': 'new_v', 'new_v_hyb_conv_b': 'new_v', 'new_v_hyb_conv_b_bias': 'new_v', 'new_v_hyb_rg_w_a': 'new_v', 'new_v_hyb_rg_b_a': 'new_v', 'new_v_hyb_rg_w_x': 'new_v', 'new_v_hyb_rg_b_x': 'new_v', 'new_v_hyb_rg_lambda': 'new_v', 'new_v_hyb_w_out': 'new_v', 'new_v_sb_w_qkv': 'new_v', 'new_v_sb_w_o': 'new_v', 'new_v_mlp_w_up': 'new_v', 'new_v_mlp_w_down': 'new_v'}


def _forward(args):
    return _fwd_reference(*[args[k] for k in FWD_PARAMS])


def _output_shape():
    out = _jax.eval_shape(lambda: _forward(_fwd_setup_inputs(0)))
    return out.shape, out.dtype

N_MICROBATCH = 1
ADAM_LR = 0.001
ADAM_B1 = 0.9
ADAM_B2 = 0.999
ADAM_EPS = 1e-08
ADAM_WD = 0.01
ADAM_STEP = 10
PER_EXAMPLE_BATCH_AXIS = {'x': 0, 'loss_target': 0}
SHARED_INPUTS = []
_WEIGHT_DTYPES = {'norm_gains': _jnp.float32, 'hyb_w_in': _jnp.float32, 'hyb_conv_a': _jnp.float32, 'hyb_conv_b': _jnp.float32, 'hyb_conv_b_bias': _jnp.float32, 'hyb_rg_w_a': _jnp.float32, 'hyb_rg_b_a': _jnp.float32, 'hyb_rg_w_x': _jnp.float32, 'hyb_rg_b_x': _jnp.float32, 'hyb_rg_lambda': _jnp.float32, 'hyb_w_out': _jnp.float32, 'sb_w_qkv': _jnp.float32, 'sb_w_o': _jnp.float32, 'mlp_w_up': _jnp.float32, 'mlp_w_down': _jnp.float32}
MOMENT_SCALE = {'norm_gains': 1.185088e+01, 'hyb_w_in': 4.016809e-01, 'hyb_conv_a': 5.036566e-01, 'hyb_conv_b': 1.787169e+00, 'hyb_conv_b_bias': 2.287449e+01, 'hyb_rg_w_a': 7.935479e-01, 'hyb_rg_b_a': 4.615457e-01, 'hyb_rg_w_x': 1.483653e+00, 'hyb_rg_b_x': 5.349719e-01, 'hyb_rg_lambda': 6.605971e-01, 'hyb_w_out': 1.446277e+00, 'sb_w_qkv': 2.099876e+00, 'sb_w_o': 3.623414e+00, 'mlp_w_up': 5.567755e-01, 'mlp_w_down': 4.728486e+00}


def _to_microbatches(a, axis):
    t = _jnp.moveaxis(a, axis, 0)
    t = t.reshape((N_MICROBATCH, t.shape[0] // N_MICROBATCH) + t.shape[1:])
    return _jnp.moveaxis(t, 1, axis + 1)


def setup_inputs(seed: int = 0) -> dict:
    inp = _fwd_setup_inputs(seed)
    key = _jax.random.fold_in(_jax.random.key(seed), 7919)
    shape, _ = _output_shape()
    out = dict(inp)
    out["loss_target"] = _jax.random.normal(_jax.random.fold_in(key, 0), shape, _jnp.float32)
    for i, name in enumerate(TWIN_WEIGHTS):
        w = inp[name].astype(_jnp.float32)
        if MOMENT_SCALE is None:
            s = _jnp.sqrt(_jnp.mean(_jnp.square(w)) + 1e-30)
        else:
            s = MOMENT_SCALE[name]
        km, kv = _jax.random.split(_jax.random.fold_in(key, i + 1))
        out[name] = w
        out["m_" + name] = s * _jax.random.normal(km, w.shape, _jnp.float32)
        out["v_" + name] = (s * s) * _jax.random.uniform(kv, w.shape, _jnp.float32, 0.5, 1.5)
    if N_MICROBATCH > 1:
        for name, axis in PER_EXAMPLE_BATCH_AXIS.items():
            out[name] = _to_microbatches(out[name], axis)
    return {'x': out['x'], 'norm_gains': out['norm_gains'], 'hyb_w_in': out['hyb_w_in'], 'hyb_conv_a': out['hyb_conv_a'], 'hyb_conv_b': out['hyb_conv_b'], 'hyb_conv_b_bias': out['hyb_conv_b_bias'], 'hyb_rg_w_a': out['hyb_rg_w_a'], 'hyb_rg_b_a': out['hyb_rg_b_a'], 'hyb_rg_w_x': out['hyb_rg_w_x'], 'hyb_rg_b_x': out['hyb_rg_b_x'], 'hyb_rg_lambda': out['hyb_rg_lambda'], 'hyb_w_out': out['hyb_w_out'], 'sb_w_qkv': out['sb_w_qkv'], 'sb_w_o': out['sb_w_o'], 'mlp_w_up': out['mlp_w_up'], 'mlp_w_down': out['mlp_w_down'], 'loss_target': out['loss_target'], 'm_norm_gains': out['m_norm_gains'], 'm_hyb_w_in': out['m_hyb_w_in'], 'm_hyb_conv_a': out['m_hyb_conv_a'], 'm_hyb_conv_b': out['m_hyb_conv_b'], 'm_hyb_conv_b_bias': out['m_hyb_conv_b_bias'], 'm_hyb_rg_w_a': out['m_hyb_rg_w_a'], 'm_hyb_rg_b_a': out['m_hyb_rg_b_a'], 'm_hyb_rg_w_x': out['m_hyb_rg_w_x'], 'm_hyb_rg_b_x': out['m_hyb_rg_b_x'], 'm_hyb_rg_lambda': out['m_hyb_rg_lambda'], 'm_hyb_w_out': out['m_hyb_w_out'], 'm_sb_w_qkv': out['m_sb_w_qkv'], 'm_sb_w_o': out['m_sb_w_o'], 'm_mlp_w_up': out['m_mlp_w_up'], 'm_mlp_w_down': out['m_mlp_w_down'], 'v_norm_gains': out['v_norm_gains'], 'v_hyb_w_in': out['v_hyb_w_in'], 'v_hyb_conv_a': out['v_hyb_conv_a'], 'v_hyb_conv_b': out['v_hyb_conv_b'], 'v_hyb_conv_b_bias': out['v_hyb_conv_b_bias'], 'v_hyb_rg_w_a': out['v_hyb_rg_w_a'], 'v_hyb_rg_b_a': out['v_hyb_rg_b_a'], 'v_hyb_rg_w_x': out['v_hyb_rg_w_x'], 'v_hyb_rg_b_x': out['v_hyb_rg_b_x'], 'v_hyb_rg_lambda': out['v_hyb_rg_lambda'], 'v_hyb_w_out': out['v_hyb_w_out'], 'v_sb_w_qkv': out['v_sb_w_qkv'], 'v_sb_w_o': out['v_sb_w_o'], 'v_mlp_w_up': out['v_mlp_w_up'], 'v_mlp_w_down': out['v_mlp_w_down']}


def _loss(weights, diff, rest, loss_target):
    with _jax.named_scope("forward"):
        args = {**rest, TWIN_DIFF_INPUT: diff, **{k: w.astype(_WEIGHT_DTYPES[k]) for k, w in weights.items()}}
        y = _forward(args)
    with _jax.named_scope("loss_head"):
        err = _jnp.square(y.astype(_jnp.float32) - loss_target)
        return 0.5 * _jnp.sum(_jnp.mean(err, axis=-1)) if err.ndim else 0.5 * err


def _adamw(w, g, m, v):
    m = ADAM_B1 * m + (1.0 - ADAM_B1) * g
    v = ADAM_B2 * v + (1.0 - ADAM_B2) * _jnp.square(g)
    m_hat = m / (1.0 - ADAM_B1 ** ADAM_STEP)
    v_hat = v / (1.0 - ADAM_B2 ** ADAM_STEP)
    delta = -ADAM_LR * (m_hat / (_jnp.sqrt(v_hat) + ADAM_EPS) + ADAM_WD * w)
    return delta, m, v


def reference(x, norm_gains, hyb_w_in, hyb_conv_a, hyb_conv_b, hyb_conv_b_bias, hyb_rg_w_a, hyb_rg_b_a, hyb_rg_w_x, hyb_rg_b_x, hyb_rg_lambda, hyb_w_out, sb_w_qkv, sb_w_o, mlp_w_up, mlp_w_down, loss_target, m_norm_gains, m_hyb_w_in, m_hyb_conv_a, m_hyb_conv_b, m_hyb_conv_b_bias, m_hyb_rg_w_a, m_hyb_rg_b_a, m_hyb_rg_w_x, m_hyb_rg_b_x, m_hyb_rg_lambda, m_hyb_w_out, m_sb_w_qkv, m_sb_w_o, m_mlp_w_up, m_mlp_w_down, v_norm_gains, v_hyb_w_in, v_hyb_conv_a, v_hyb_conv_b, v_hyb_conv_b_bias, v_hyb_rg_w_a, v_hyb_rg_b_a, v_hyb_rg_w_x, v_hyb_rg_b_x, v_hyb_rg_lambda, v_hyb_w_out, v_sb_w_qkv, v_sb_w_o, v_mlp_w_up, v_mlp_w_down):
    given = dict(x=x, norm_gains=norm_gains, hyb_w_in=hyb_w_in, hyb_conv_a=hyb_conv_a, hyb_conv_b=hyb_conv_b, hyb_conv_b_bias=hyb_conv_b_bias, hyb_rg_w_a=hyb_rg_w_a, hyb_rg_b_a=hyb_rg_b_a, hyb_rg_w_x=hyb_rg_w_x, hyb_rg_b_x=hyb_rg_b_x, hyb_rg_lambda=hyb_rg_lambda, hyb_w_out=hyb_w_out, sb_w_qkv=sb_w_qkv, sb_w_o=sb_w_o, mlp_w_up=mlp_w_up, mlp_w_down=mlp_w_down, loss_target=loss_target, m_norm_gains=m_norm_gains, m_hyb_w_in=m_hyb_w_in, m_hyb_conv_a=m_hyb_conv_a, m_hyb_conv_b=m_hyb_conv_b, m_hyb_conv_b_bias=m_hyb_conv_b_bias, m_hyb_rg_w_a=m_hyb_rg_w_a, m_hyb_rg_b_a=m_hyb_rg_b_a, m_hyb_rg_w_x=m_hyb_rg_w_x, m_hyb_rg_b_x=m_hyb_rg_b_x, m_hyb_rg_lambda=m_hyb_rg_lambda, m_hyb_w_out=m_hyb_w_out, m_sb_w_qkv=m_sb_w_qkv, m_sb_w_o=m_sb_w_o, m_mlp_w_up=m_mlp_w_up, m_mlp_w_down=m_mlp_w_down, v_norm_gains=v_norm_gains, v_hyb_w_in=v_hyb_w_in, v_hyb_conv_a=v_hyb_conv_a, v_hyb_conv_b=v_hyb_conv_b, v_hyb_conv_b_bias=v_hyb_conv_b_bias, v_hyb_rg_w_a=v_hyb_rg_w_a, v_hyb_rg_b_a=v_hyb_rg_b_a, v_hyb_rg_w_x=v_hyb_rg_w_x, v_hyb_rg_b_x=v_hyb_rg_b_x, v_hyb_rg_lambda=v_hyb_rg_lambda, v_hyb_w_out=v_hyb_w_out, v_sb_w_qkv=v_sb_w_qkv, v_sb_w_o=v_sb_w_o, v_mlp_w_up=v_mlp_w_up, v_mlp_w_down=v_mlp_w_down)
    weights = {n: given[n] for n in TWIN_WEIGHTS}
    shared = {n: given[n] for n in SHARED_INPUTS}
    per_example = {n: given[n] for n in ['x']}
    grad_fn = _jax.value_and_grad(_loss, argnums=(0, 1))

    def one_microbatch(ex, loss_target):
        ex = dict(ex)
        diff = ex.pop(TWIN_DIFF_INPUT)
        return grad_fn(weights, diff, {**shared, **ex}, loss_target)

    if N_MICROBATCH == 1:
        loss, (grad_w, grad_x) = one_microbatch(per_example, given["loss_target"])
    else:
        def body(carry, xs):
            loss_sum, grad_sum = carry
            l_k, (gw_k, gx_k) = one_microbatch(xs[0], xs[1])
            with _jax.named_scope("update"):
                return (loss_sum + l_k, _jax.tree.map(_jnp.add, grad_sum, gw_k)), gx_k

        init = (_jnp.zeros((), _jnp.float32), _jax.tree.map(_jnp.zeros_like, weights))
        (loss, grad_w), grad_x = _jax.lax.scan(body, init, (per_example, given["loss_target"]))
    with _jax.named_scope("update"):
        delta_w, new_m, new_v = {}, {}, {}
        for n in TWIN_WEIGHTS:
            delta_w[n], new_m[n], new_v[n] = _adamw(weights[n], grad_w[n], given["m_" + n], given["v_" + n])
    return (loss, grad_x, *[grad_w[n] for n in TWIN_WEIGHTS], *[delta_w[n] for n in TWIN_WEIGHTS],
            *[new_m[n] for n in TWIN_WEIGHTS], *[new_v[n] for n in TWIN_WEIGHTS])
```

```python
import functools
import math

import jax
import jax.numpy as jnp
from jax import lax
from jax.experimental import pallas as pl
from jax.experimental.pallas import tpu as pltpu

F32 = jnp.float32
BF16 = jnp.bfloat16
MESH = pl.DeviceIdType.MESH

SB_HEADS = 16
NORM_EPS = 1e-6
LRU_C = 8.0
ADAM_LR = 0.001
ADAM_B1 = 0.9
ADAM_B2 = 0.999
ADAM_EPS = 1e-08
ADAM_WD = 0.01
ADAM_STEP = 10

LANES = 128
SUBLANES = 8
VMEM_LIMIT = 48 * 1024 * 1024
MM_TILE = 1024
ROW_TILE = 256
N_CHIPS = 4
N_DEV = 8

_DIMS = {
    "nn": (((1,), (0,)), ((), ())),
    "nt": (((1,), (1,)), ((), ())),
    "tn": (((0,), (0,)), ((), ())),
}


def _cp(sem=None, vmem=VMEM_LIMIT):
    return pltpu.CompilerParams(dimension_semantics=sem, vmem_limit_bytes=vmem)


def _pick(dim, pref):
    t = min(dim, pref)
    while dim % t:
        t -= LANES
    return t


def _whole(shape):
    nd = len(shape)
    return pl.BlockSpec(tuple(shape), lambda *_: (0,) * nd)


def _sigmoid(z):
    return 1.0 / (1.0 + jnp.exp(-z))


def _log_sigmoid(z):
    return jnp.minimum(z, 0.0) - jnp.log(1.0 + jnp.exp(-jnp.abs(z)))


def _expm1(z):
    series = z * (1.0 + z * (0.5 + z * (1.0 / 6.0 + z * (1.0 / 24.0))))
    return jnp.where(jnp.abs(z) < 0.05, series, jnp.exp(z) - 1.0)


_GELU_C = math.sqrt(2.0 / math.pi)


def _gelu_and_grad(g):
    inner = _GELU_C * (g + 0.044715 * g * g * g)
    t = jnp.tanh(inner)
    val = 0.5 * g * (1.0 + t)
    grad = 0.5 * (1.0 + t) + 0.5 * g * (1.0 - t * t) * _GELU_C * (1.0 + 3.0 * 0.044715 * g * g)
    return val, grad


def _shift_down(cur, prev8, k, rows):
    n = cur.shape[0]
    rolled = pltpu.roll(cur, k, 0)
    head = jnp.tile(pltpu.roll(prev8, k, 0), (n // SUBLANES, 1))
    return jnp.where(rows < k, head, rolled)


def _shift_up(cur, next8, k, rows):
    n = cur.shape[0]
    rolled = pltpu.roll(cur, n - k, 0)
    tail = jnp.tile(pltpu.roll(next8, SUBLANES - k, 0), (n // SUBLANES, 1))
    return jnp.where(rows >= n - k, tail, rolled)


def _colsum(v):
    return jnp.sum(v, axis=0, keepdims=True)


def _matmul(name, mode, grid, operands, in_specs, out_shapes, out_specs, acc_shape, epilogue=None):
    nk = grid[2]
    n_in = len(operands)
    dims = _DIMS[mode]

    def body(*refs):
        a_ref, b_ref = refs[0], refs[1]
        extra = refs[2:n_in]
        outs = refs[n_in:-1]
        acc_ref = refs[-1]
        k = pl.program_id(2)

        @pl.when(k == 0)
        def _():
            acc_ref[...] = jnp.zeros_like(acc_ref)

        acc_ref[...] += lax.dot_general(
            a_ref[...].astype(BF16), b_ref[...].astype(BF16), dims, preferred_element_type=F32)

        @pl.when(k == nk - 1)
        def _():
            acc = acc_ref[...]
            res = epilogue(acc, *[e[...] for e in extra]) if epilogue is not None else (acc,)
            for o_ref, o in zip(outs, res):
                o_ref[...] = o.astype(o_ref.dtype)

    return pl.pallas_call(
        body, name=name, grid=grid, in_specs=in_specs, out_specs=out_specs, out_shape=out_shapes,
        scratch_shapes=[pltpu.VMEM(acc_shape, F32)],
        compiler_params=_cp(("parallel", "parallel", "arbitrary")),
    )(*operands)


def _mm_fwd_col(name, a, wfull, out_dtypes=(F32,), epilogue=None):
    s, kdim = a.shape
    _, _, cs = wfull.shape
    tm, tk, tn = _pick(s, MM_TILE), _pick(kdim, MM_TILE), _pick(cs, MM_TILE)
    nbj = cs // tn
    grid = (s // tm, N_CHIPS * nbj, kdim // tk)
    out_shapes = [jax.ShapeDtypeStruct((s, N_CHIPS * cs), dt) for dt in out_dtypes]
    out_specs = [pl.BlockSpec((tm, tn), lambda i, n, k: (i, n)) for _ in out_dtypes]
    return _matmul(
        name, "nn", grid, [a, wfull],
        [pl.BlockSpec((tm, tk), lambda i, n, k: (i, k)),
         pl.BlockSpec((None, tk, tn), lambda i, n, k: (n // nbj, k, n % nbj))],
        out_shapes, out_specs, (tm, tn), epilogue)


def _mm_fwd_row(name, a, w2d, out_dtype=F32):
    s, kdim = a.shape
    _, n_out = w2d.shape
    tm, tk, tn = _pick(s, MM_TILE), _pick(kdim, MM_TILE), _pick(n_out, MM_TILE)
    grid = (s // tm, n_out // tn, kdim // tk)
    return _matmul(
        name, "nn", grid, [a, w2d],
        [pl.BlockSpec((tm, tk), lambda i, n, k: (i, k)),
         pl.BlockSpec((tk, tn), lambda i, n, k: (k, n))],
        [jax.ShapeDtypeStruct((s, n_out), out_dtype)],
        [pl.BlockSpec((tm, tn), lambda i, n, k: (i, n))], (tm, tn))[0]


def _mm_bwd_col(name, dy, wfull, out_dtype=F32):
    s, _ = dy.shape
    _, kdim, cs = wfull.shape
    tm, tn, tk = _pick(s, MM_TILE), _pick(kdim, MM_TILE), _pick(cs, MM_TILE)
    nbj = cs // tk
    grid = (s // tm, kdim // tn, N_CHIPS * nbj)
    return _matmul(
        name, "nt", grid, [dy, wfull],
        [pl.BlockSpec((tm, tk), lambda i, n, k: (i, k)),
         pl.BlockSpec((None, tn, tk), lambda i, n, k: (k // nbj, n, k % nbj))],
        [jax.ShapeDtypeStruct((s, kdim), out_dtype)],
        [pl.BlockSpec((tm, tn), lambda i, n, k: (i, n))], (tm, tn))[0]


def _mm_bwd_row(name, dy, w2d, out_dtypes=(F32,), extra=None, epilogue=None):
    s, n_in = dy.shape
    kdim, _ = w2d.shape
    tm, tn, tk = _pick(s, MM_TILE), _pick(kdim, MM_TILE), _pick(n_in, MM_TILE)
    grid = (s // tm, kdim // tn, n_in // tk)
    operands = [dy, w2d]
    in_specs = [pl.BlockSpec((tm, tk), lambda i, n, k: (i, k)),
                pl.BlockSpec((tn, tk), lambda i, n, k: (n, k))]
    if extra is not None:
        operands.append(extra)
        in_specs.append(pl.BlockSpec((tm, tn), lambda i, n, k: (i, n)))
    return _matmul(
        name, "nt", grid, operands, in_specs,
        [jax.ShapeDtypeStruct((s, kdim), dt) for dt in out_dtypes],
        [pl.BlockSpec((tm, tn), lambda i, n, k: (i, n)) for _ in out_dtypes], (tm, tn), epilogue)


def _mm_wgrad_col(name, a, dy, cs):
    s, kdim = a.shape
    tm, tn, ts = _pick(kdim, MM_TILE), _pick(cs, MM_TILE), _pick(s, MM_TILE)
    nbj = cs // tn
    grid = (kdim // tm, N_CHIPS * nbj, s // ts)
    return _matmul(
        name, "tn", grid, [a, dy],
        [pl.BlockSpec((ts, tm), lambda i, n, k: (k, i)),
         pl.BlockSpec((ts, tn), lambda i, n, k: (k, n))],
        [jax.ShapeDtypeStruct((N_CHIPS, kdim, cs), BF16)],
        [pl.BlockSpec((None, tm, tn), lambda i, n, k: (n // nbj, i, n % nbj))], (tm, tn))[0]


def _mm_wgrad_row(name, a, dy):
    s, kdim = a.shape
    _, n_out = dy.shape
    tm, tn, ts = _pick(kdim, MM_TILE), _pick(n_out, MM_TILE), _pick(s, MM_TILE)
    grid = (kdim // tm, n_out // tn, s // ts)
    return _matmul(
        name, "tn", grid, [a, dy],
        [pl.BlockSpec((ts, tm), lambda i, n, k: (k, i)),
         pl.BlockSpec((ts, tn), lambda i, n, k: (k, n))],
        [jax.ShapeDtypeStruct((kdim, n_out), BF16)],
        [pl.BlockSpec((tm, tn), lambda i, n, k: (i, n))], (tm, tn))[0]


def _mm_wgrad_diag(name, a, dy):
    s, width = a.shape
    nb = width // LANES
    ts = _pick(s, MM_TILE)
    grid = (nb, 1, s // ts)
    return _matmul(
        name, "tn", grid, [a, dy],
        [pl.BlockSpec((ts, LANES), lambda i, n, k: (k, i)),
         pl.BlockSpec((ts, LANES), lambda i, n, k: (k, i))],
        [jax.ShapeDtypeStruct((nb, LANES, LANES), F32)],
        [pl.BlockSpec((None, LANES, LANES), lambda i, n, k: (i, 0, 0))], (LANES, LANES))[0]


def _rowspec(tr, d):
    return pl.BlockSpec((tr, d), lambda i: (i, 0))


def _vecspec(d):
    return pl.BlockSpec((1, d), lambda i: (0, 0))


def _rms(x, g):
    return x * lax.rsqrt(jnp.mean(x * x, axis=-1, keepdims=True) + NORM_EPS) * g


def _cast_bf16(name, w):
    r, c = w.shape
    tr = _pick(r, ROW_TILE)

    def body(w_ref, o_ref):
        o_ref[...] = w_ref[...].astype(BF16)

    return pl.pallas_call(
        body, name=name, grid=(r // tr,), in_specs=[_rowspec(tr, c)], out_specs=_rowspec(tr, c),
        out_shape=jax.ShapeDtypeStruct((r, c), BF16), compiler_params=_cp(("parallel",)))(w)


def _rms_fwd(name, x, g):
    s, d = x.shape
    tr = _pick(s, ROW_TILE)

    def body(x_ref, g_ref, h_ref):
        h_ref[...] = _rms(x_ref[...], g_ref[...]).astype(BF16)

    return pl.pallas_call(
        body, name=name, grid=(s // tr,), in_specs=[_rowspec(tr, d), _vecspec(d)],
        out_specs=_rowspec(tr, d), out_shape=jax.ShapeDtypeStruct((s, d), BF16),
        compiler_params=_cp(("parallel",)))(x, g)


def _rms_post(name, y, g_post, res, g_next=None):
    s, d = y.shape
    tr = _pick(s, ROW_TILE)
    with_next = g_next is not None

    def body(*refs):
        if with_next:
            y_ref, gp_ref, r_ref, gn_ref, x_ref, h_ref = refs
        else:
            y_ref, gp_ref, r_ref, x_ref = refs
        xn = r_ref[...] + _rms(y_ref[...], gp_ref[...])
        x_ref[...] = xn
        if with_next:
            h_ref[...] = _rms(xn, gn_ref[...]).astype(BF16)

    operands = [y, g_post, res] + ([g_next] if with_next else [])
    in_specs = [_rowspec(tr, d), _vecspec(d), _rowspec(tr, d)] + ([_vecspec(d)] if with_next else [])
    out_shape = [jax.ShapeDtypeStruct((s, d), F32)] + ([jax.ShapeDtypeStruct((s, d), BF16)] if with_next else [])
    out_specs = [_rowspec(tr, d)] + ([_rowspec(tr, d)] if with_next else [])
    return pl.pallas_call(
        body, name=name, grid=(s // tr,), in_specs=in_specs, out_specs=out_specs, out_shape=out_shape,
        compiler_params=_cp(("parallel",)))(*operands)


def _rms_bwd(name, x, g, dy, res=None, out_dtype=F32):
    s, d = x.shape
    tr = _pick(s, ROW_TILE)
    nsteps = s // tr
    with_res = res is not None

    def body(*refs):
        if with_res:
            x_ref, g_ref, dy_ref, r_ref, dx_ref, dg_ref, acc_ref = refs
        else:
            x_ref, g_ref, dy_ref, dx_ref, dg_ref, acc_ref = refs
        i = pl.program_id(0)

        @pl.when(i == 0)
        def _():
            acc_ref[...] = jnp.zeros_like(acc_ref)

        xv = x_ref[...]
        dyv = dy_ref[...].astype(F32)
        r = lax.rsqrt(jnp.mean(xv * xv, axis=-1, keepdims=True) + NORM_EPS)
        xhat = xv * r
        gy = dyv * g_ref[...]
        dx = r * (gy - xhat * jnp.mean(gy * xhat, axis=-1, keepdims=True))
        if with_res:
            dx = dx + r_ref[...]
        dx_ref[...] = dx.astype(dx_ref.dtype)
        acc_ref[...] += jnp.sum((dyv * xhat).reshape(tr // SUBLANES, SUBLANES, d), axis=0)

        @pl.when(i == nsteps - 1)
        def _():
            dg_ref[...] = jnp.broadcast_to(_colsum(acc_ref[...]), (SUBLANES, d))

    operands = [x, g, dy] + ([res] if with_res else [])
    in_specs = [_rowspec(tr, d), _vecspec(d), _rowspec(tr, d)] + ([_rowspec(tr, d)] if with_res else [])
    dx, dg = pl.pallas_call(
        body, name=name, grid=(nsteps,), in_specs=in_specs,
        out_specs=[_rowspec(tr, d), pl.BlockSpec((SUBLANES, d), lambda i: (0, 0))],
        out_shape=[jax.ShapeDtypeStruct((s, d), out_dtype), jax.ShapeDtypeStruct((SUBLANES, d), F32)],
        scratch_shapes=[pltpu.VMEM((SUBLANES, d), F32)],
        compiler_params=_cp(("arbitrary",)))(*operands)
    return dx, dg[0:1]


def _loss_head(name, y, target):
    s, d = y.shape
    tr = _pick(s, ROW_TILE)
    nsteps = s // tr

    def body(y_ref, t_ref, dy_ref, l_ref, acc_ref):
        i = pl.program_id(0)

        @pl.when(i == 0)
        def _():
            acc_ref[...] = jnp.zeros_like(acc_ref)

        err = y_ref[...] - t_ref[...]
        dy_ref[...] = err * (1.0 / d)
        acc_ref[...] += jnp.sum((err * err).reshape(tr // SUBLANES, SUBLANES, d), axis=0)

        @pl.when(i == nsteps - 1)
        def _():
            l_ref[...] = jnp.full((SUBLANES, LANES), (0.5 / d) * jnp.sum(acc_ref[...]), F32)

    dy, l = pl.pallas_call(
        body, name=name, grid=(nsteps,), in_specs=[_rowspec(tr, d), _rowspec(tr, d)],
        out_specs=[_rowspec(tr, d), pl.BlockSpec((SUBLANES, LANES), lambda i: (0, 0))],
        out_shape=[jax.ShapeDtypeStruct((s, d), F32), jax.ShapeDtypeStruct((SUBLANES, LANES), F32)],
        scratch_shapes=[pltpu.VMEM((SUBLANES, d), F32)],
        compiler_params=_cp(("arbitrary",)))(y, target)
    return dy, l[0, 0]


def _gates(xr, wa, ba, wx, bx, lam):
    xb = xr.astype(BF16)
    r = _sigmoid(jnp.dot(xb, wa, preferred_element_type=F32) + ba)
    i = _sigmoid(jnp.dot(xb, wx, preferred_element_type=F32) + bx)
    log_a = LRU_C * r * _log_sigmoid(lam)
    a = jnp.exp(log_a)
    m = jnp.sqrt(-_expm1(2.0 * log_a))
    return r, i, a, m


def _mixer_fwd(proj, conv_a, conv_b, bias, wa_blk, ba, wx_blk, bx, lam):
    s, w5 = proj.shape
    w = w5 // 5
    nch = w // LANES
    ts = _pick(s, ROW_TILE)
    nt = s // ts

    def body(p_ref, pp_ref, ca_ref, cb_ref, bias_ref, wa_ref, ba_ref, wx_ref, bx_ref, lam_ref,
             y_ref, h_ref, a_scr, b_scr, hc_scr):
        t = pl.program_id(0)
        first = t == 0
        rows = lax.broadcasted_iota(jnp.int32, (ts, LANES), 0)

        @pl.when(first)
        def _():
            hc_scr[...] = jnp.zeros_like(hc_scr)

        def cur(comp, c):
            return p_ref[:, comp * w + c * LANES:comp * w + (c + 1) * LANES]

        def prev(comp, c):
            v = pp_ref[:, comp * w + c * LANES:comp * w + (c + 1) * LANES]
            return jnp.where(first, 0.0, v)

        for c in range(nch):
            sl = slice(c * LANES, (c + 1) * LANES)
            cx = cur(1, c) * cur(2, c)
            cxp = prev(1, c) * prev(2, c)
            wa3 = ca_ref[:, sl]
            conv = (wa3[2:3] * cx + wa3[1:2] * _shift_down(cx, cxp, 1, rows)
                    + wa3[0:1] * _shift_down(cx, cxp, 2, rows))
            y_ref[:, sl] = (cur(0, c) * conv).astype(BF16)

        for c in range(nch):
            sl = slice(c * LANES, (c + 1) * LANES)
            xb, xbp = cur(4, c), prev(4, c)
            wb4 = cb_ref[:, sl]
            xr = (wb4[3:4] * xb + wb4[2:3] * _shift_down(xb, xbp, 1, rows)
                  + wb4[1:2] * _shift_down(xb, xbp, 2, rows)
                  + wb4[0:1] * _shift_down(xb, xbp, 3, rows) + bias_ref[:, sl])
            _, i, a, m = _gates(xr, wa_ref[c], ba_ref[:, sl], wx_ref[c], bx_ref[:, sl], lam_ref[:, sl])
            a_scr[:, sl] = a
            b_scr[:, sl] = m * i * xr

        def step(r, h):
            h = a_scr[pl.ds(r, 1), :] * h + b_scr[pl.ds(r, 1), :]
            h_ref[pl.ds(r, 1), :] = h
            return h

        hc_scr[0:1, :] = lax.fori_loop(0, ts, step, hc_scr[0:1, :], unroll=8)

        for c in range(nch):
            sl = slice(c * LANES, (c + 1) * LANES)
            gel, _ = _gelu_and_grad(cur(3, c))
            y_ref[:, w + c * LANES:w + (c + 1) * LANES] = (h_ref[:, sl] * gel).astype(BF16)

    vec = lambda n: _whole((n, w))
    return pl.pallas_call(
        body, name="mixer_fwd", grid=(nt,),
        in_specs=[pl.BlockSpec((ts, w5), lambda t: (t, 0)),
                  pl.BlockSpec((SUBLANES, w5), lambda t: (jnp.maximum(t * (ts // SUBLANES) - 1, 0), 0)),
                  vec(3), vec(4), vec(1), _whole(wa_blk.shape), vec(1), _whole(wx_blk.shape), vec(1), vec(1)],
        out_specs=[pl.BlockSpec((ts, 2 * w), lambda t: (t, 0)), pl.BlockSpec((ts, w), lambda t: (t, 0))],
        out_shape=[jax.ShapeDtypeStruct((s, 2 * w), BF16), jax.ShapeDtypeStruct((s, w), F32)],
        scratch_shapes=[pltpu.VMEM((ts, w), F32), pltpu.VMEM((ts, w), F32), pltpu.VMEM((SUBLANES, w), F32)],
        compiler_params=_cp(("arbitrary",)),
    )(proj, proj, conv_a, conv_b, bias, wa_blk, ba, wx_blk, bx, lam)


_SG_CONV_A, _SG_CONV_B, _SG_BIAS, _SG_BA, _SG_BX, _SG_LAM, _SG_ROWS = 0, 3, 7, 8, 9, 10, 16


def _mixer_bwd(proj, hseq, dy, conv_a, conv_b, bias, wa_blk, ba, wx_blk, bx, lam):
    s, w5 = proj.shape
    w = w5 // 5
    nch = w // LANES
    ts = _pick(s, ROW_TILE)
    nt = s // ts
    tpb = ts // SUBLANES

    def body(p_ref, pp_ref, h_ref, hp_ref, dy_ref, ca_ref, cb_ref, bias_ref, wa_ref, ba_ref, wx_ref, bx_ref,
             lam_ref, dp_ref, xr_ref, dpa_ref, dpx_ref, sg_ref,
             a_scr, g_scr, l_scr, x_scr, r_scr, i_scr, m_scr, cl_scr, cdc_scr, cdx_scr):
        pid = pl.program_id(0)
        last = pid == 0
        first = pid == nt - 1
        rows = lax.broadcasted_iota(jnp.int32, (ts, LANES), 0)

        @pl.when(last)
        def _():
            sg_ref[...] = jnp.zeros_like(sg_ref)
            cl_scr[...] = jnp.zeros_like(cl_scr)
            cdc_scr[...] = jnp.zeros_like(cdc_scr)
            cdx_scr[...] = jnp.zeros_like(cdx_scr)

        def cur(comp, c):
            return p_ref[:, comp * w + c * LANES:comp * w + (c + 1) * LANES]

        def prev(comp, c):
            v = pp_ref[:, comp * w + c * LANES:comp * w + (c + 1) * LANES]
            return jnp.where(first, 0.0, v)

        def put(comp, c, v):
            dp_ref[:, comp * w + c * LANES:comp * w + (c + 1) * LANES] = v

        def acc(row, sl, v):
            sg_ref[row:row + 1, sl] += _colsum(v)

        for c in range(nch):
            sl = slice(c * LANES, (c + 1) * LANES)
            bg, cg, ax = cur(0, c), cur(1, c), cur(2, c)
            cx = cg * ax
            cxp = prev(1, c) * prev(2, c)
            cx1 = _shift_down(cx, cxp, 1, rows)
            cx2 = _shift_down(cx, cxp, 2, rows)
            wa3 = ca_ref[:, sl]
            conv = wa3[2:3] * cx + wa3[1:2] * cx1 + wa3[0:1] * cx2
            dya = dy_ref[:, sl]
            put(0, c, dya * conv)
            dconv = dya * bg
            nxt = cdc_scr[:, sl]
            dcx = (wa3[2:3] * dconv + wa3[1:2] * _shift_up(dconv, nxt, 1, rows)
                   + wa3[0:1] * _shift_up(dconv, nxt, 2, rows))
            cdc_scr[:, sl] = dconv[0:SUBLANES]
            put(1, c, dcx * ax)
            put(2, c, dcx * cg)
            acc(_SG_CONV_A + 2, sl, dconv * cx)
            acc(_SG_CONV_A + 1, sl, dconv * cx1)
            acc(_SG_CONV_A + 0, sl, dconv * cx2)

        for c in range(nch):
            sl = slice(c * LANES, (c + 1) * LANES)
            xb, xbp = cur(4, c), prev(4, c)
            wb4 = cb_ref[:, sl]
            xr = (wb4[3:4] * xb + wb4[2:3] * _shift_down(xb, xbp, 1, rows)
                  + wb4[1:2] * _shift_down(xb, xbp, 2, rows)
                  + wb4[0:1] * _shift_down(xb, xbp, 3, rows) + bias_ref[:, sl])
            r, i, a, m = _gates(xr, wa_ref[c], ba_ref[:, sl], wx_ref[c], bx_ref[:, sl], lam_ref[:, sl])
            gel, dgel = _gelu_and_grad(cur(3, c))
            dyb = dy_ref[:, w + c * LANES:w + (c + 1) * LANES]
            put(3, c, dyb * h_ref[:, sl] * dgel)
            g_scr[:, sl] = dyb * gel
            a_scr[:, sl] = a
            x_scr[:, sl] = xr
            r_scr[:, sl] = r
            i_scr[:, sl] = i
            m_scr[:, sl] = m

        def step(j, carry):
            r = ts - 1 - j
            lam_t = g_scr[pl.ds(r, 1), :] + carry
            l_scr[pl.ds(r, 1), :] = lam_t
            return a_scr[pl.ds(r, 1), :] * lam_t

        cl_scr[0:1, :] = lax.fori_loop(0, ts, step, cl_scr[0:1, :], unroll=8)

        for c in range(nch):
            sl = slice(c * LANES, (c + 1) * LANES)
            lam_t = l_scr[:, sl]
            hprev = _shift_down(h_ref[:, sl], jnp.where(first, 0.0, hp_ref[:, sl]), 1, rows)
            xr, r, i, m, a = x_scr[:, sl], r_scr[:, sl], i_scr[:, sl], m_scr[:, sl], a_scr[:, sl]
            da = lam_t * hprev
            dm = lam_t * i * xr
            di = lam_t * m * xr
            dxr = lam_t * m * i
            dlog_a = da * a - dm * a * a / m
            lam_p = lam_ref[:, sl]
            dr = dlog_a * (LRU_C * _log_sigmoid(lam_p))
            acc(_SG_LAM, sl, dlog_a * r * (LRU_C * _sigmoid(-lam_p)))
            dpa = dr * r * (1.0 - r)
            dpx = di * i * (1.0 - i)
            dpa_b, dpx_b = dpa.astype(BF16), dpx.astype(BF16)
            dxr = (dxr + lax.dot_general(dpa_b, wa_ref[c], _DIMS["nt"], preferred_element_type=F32)
                   + lax.dot_general(dpx_b, wx_ref[c], _DIMS["nt"], preferred_element_type=F32))
            xr_ref[:, sl] = xr.astype(BF16)
            dpa_ref[:, sl] = dpa_b
            dpx_ref[:, sl] = dpx_b
            acc(_SG_BA, sl, dpa)
            acc(_SG_BX, sl, dpx)
            acc(_SG_BIAS, sl, dxr)
            nxt = cdx_scr[:, sl]
            wb4 = cb_ref[:, sl]
            put(4, c, wb4[3:4] * dxr + wb4[2:3] * _shift_up(dxr, nxt, 1, rows)
                + wb4[1:2] * _shift_up(dxr, nxt, 2, rows) + wb4[0:1] * _shift_up(dxr, nxt, 3, rows))
            cdx_scr[:, sl] = dxr[0:SUBLANES]
            xb, xbp = cur(4, c), prev(4, c)
            acc(_SG_CONV_B + 3, sl, dxr * xb)
            acc(_SG_CONV_B + 2, sl, dxr * _shift_down(xb, xbp, 1, rows))
            acc(_SG_CONV_B + 1, sl, dxr * _shift_down(xb, xbp, 2, rows))
            acc(_SG_CONV_B + 0, sl, dxr * _shift_down(xb, xbp, 3, rows))

    blk = lambda width: pl.BlockSpec((ts, width), lambda p: (nt - 1 - p, 0))
    pre = lambda width: pl.BlockSpec(
        (SUBLANES, width), lambda p: (jnp.maximum((nt - 1 - p) * tpb - 1, 0), 0))
    vec = lambda n: _whole((n, w))
    big = lambda: pltpu.VMEM((ts, w), F32)
    small = lambda: pltpu.VMEM((SUBLANES, w), F32)
    return pl.pallas_call(
        body, name="mixer_bwd", grid=(nt,),
        in_specs=[blk(w5), pre(w5), blk(w), pre(w), blk(2 * w),
                  vec(3), vec(4), vec(1), _whole(wa_blk.shape), vec(1), _whole(wx_blk.shape), vec(1), vec(1)],
        out_specs=[blk(w5), blk(w), blk(w), blk(w), _whole((_SG_ROWS, w))],
        out_shape=[jax.ShapeDtypeStruct((s, w5), F32), jax.ShapeDtypeStruct((s, w), BF16),
                   jax.ShapeDtypeStruct((s, w), BF16), jax.ShapeDtypeStruct((s, w), BF16),
                   jax.ShapeDtypeStruct((_SG_ROWS, w), F32)],
        scratch_shapes=[big(), big(), big(), big(), big(), big(), big(), small(), small(), small()],
        compiler_params=_cp(("arbitrary",)),
    )(proj, proj, hseq, hseq, dy, conv_a, conv_b, bias, wa_blk, ba, wx_blk, bx, lam)


def _split_dot(v, tri):
    hi = v.astype(BF16)
    lo = (v - hi.astype(F32)).astype(BF16)
    return (jnp.dot(hi, tri, preferred_element_type=F32) + jnp.dot(lo, tri, preferred_element_type=F32))


def _tri(cmp):
    r = lax.broadcasted_iota(jnp.int32, (LANES, LANES), 0)
    c = lax.broadcasted_iota(jnp.int32, (LANES, LANES), 1)
    return cmp(r, c).astype(BF16)


def _attn_fwd(qkv, heads):
    s = qkv.shape[0]
    dh = LANES
    tq = _pick(s, ROW_TILE)
    nq = s // tq
    scale = 1.0 / math.sqrt(dh)
    kpb = tq // LANES

    def body(q_ref, k_ref, v_ref, o_ref, tot_ref, acc_scr, car_scr):
        qi = pl.program_id(1)
        q = q_ref[...]
        acc_scr[...] = jnp.zeros_like(acc_scr)
        car_scr[...] = jnp.zeros_like(car_scr)
        nkb = (qi + 1) * kpb
        tri = _tri(lambda r, c: r > c)
        qpos = qi * tq + lax.broadcasted_iota(jnp.int32, (tq, LANES), 0)
        lane = lax.broadcasted_iota(jnp.int32, (tq, LANES), 1)

        def step(j, carry):
            k0 = pl.multiple_of((nkb - 1 - j) * LANES, LANES)
            kb = k_ref[pl.ds(k0, LANES), :]
            vb = v_ref[pl.ds(k0, LANES), :]
            z = lax.dot_general(q, kb, _DIMS["nt"], preferred_element_type=F32) * scale
            l1 = jnp.log(1.0 + jnp.exp(-jnp.abs(z)))
            ls = jnp.minimum(z, 0.0) - l1
            valid = (k0 + lane) < qpos
            ln = jnp.where(valid, ls - z, 0.0)
            sfx = _split_dot(ln, tri)
            car = car_scr[...]
            wgt = jnp.where(valid, jnp.exp(ls + sfx + car), 0.0)
            acc_scr[...] += jnp.dot(wgt.astype(BF16), vb, preferred_element_type=F32)
            car_scr[...] = car + (sfx[:, 0:1] + ln[:, 0:1])
            return carry

        lax.fori_loop(0, nkb, step, 0)
        o_ref[...] = acc_scr[...].astype(BF16)
        tot_ref[...] = car_scr[...]

    return pl.pallas_call(
        body, name="attn_fwd", grid=(heads, nq),
        in_specs=[pl.BlockSpec((tq, dh), lambda h, i: (i, h)),
                  pl.BlockSpec((s, dh), lambda h, i: (0, heads + h)),
                  pl.BlockSpec((s, dh), lambda h, i: (0, 2 * heads + h))],
        out_specs=[pl.BlockSpec((tq, dh), lambda h, i: (i, h)), pl.BlockSpec((tq, dh), lambda h, i: (i, h))],
        out_shape=[jax.ShapeDtypeStruct((s, heads * dh), BF16), jax.ShapeDtypeStruct((s, heads * dh), F32)],
        scratch_shapes=[pltpu.VMEM((tq, dh), F32), pltpu.VMEM((tq, dh), F32)],
        compiler_params=_cp(("parallel", "arbitrary")),
    )(qkv, qkv, qkv)


def _attn_bwd(qkv, tot, do, heads):
    s = qkv.shape[0]
    dh = LANES
    tq = _pick(s, ROW_TILE)
    nq = s // tq
    scale = 1.0 / math.sqrt(dh)
    kpb = tq // LANES

    def body(q_ref, k_ref, v_ref, tot_ref, do_ref, dq_ref, dk_ref, dv_ref,
             dq_scr, dk_scr, dv_scr, cl_scr, cg_scr):
        qi = pl.program_id(1)

        @pl.when(qi == 0)
        def _():
            dk_scr[...] = jnp.zeros_like(dk_scr)
            dv_scr[...] = jnp.zeros_like(dv_scr)

        q = q_ref[...]
        dob = do_ref[...]
        dq_scr[...] = jnp.zeros_like(dq_scr)
        cl_scr[...] = jnp.zeros_like(cl_scr)
        cg_scr[...] = jnp.zeros_like(cg_scr)
        nkb = (qi + 1) * kpb
        tri_le = _tri(lambda r, c: r <= c)
        tri_lt = _tri(lambda r, c: r < c)
        qpos = qi * tq + lax.broadcasted_iota(jnp.int32, (tq, LANES), 0)
        lane = lax.broadcasted_iota(jnp.int32, (tq, LANES), 1)

        def step(j, carry):
            k0 = pl.multiple_of(j * LANES, LANES)
            kb = k_ref[pl.ds(k0, LANES), :]
            vb = v_ref[pl.ds(k0, LANES), :]
            z = lax.dot_general(q, kb, _DIMS["nt"], preferred_element_type=F32) * scale
            l1 = jnp.log(1.0 + jnp.exp(-jnp.abs(z)))
            ls = jnp.minimum(z, 0.0) - l1
            valid = (k0 + lane) < qpos
            ln = jnp.where(valid, ls - z, 0.0)
            pin = cl_scr[...] + _split_dot(ln, tri_le)
            wgt = jnp.where(valid, jnp.exp(ls + (tot_ref[...] - pin)), 0.0)
            dw = lax.dot_general(dob, vb, _DIMS["nt"], preferred_element_type=F32)
            g = wgt * dw
            pex = cg_scr[...] + _split_dot(g, tri_lt)
            beta = jnp.exp(ls)
            dz = jnp.where(valid, g - beta * (g + pex), 0.0).astype(BF16)
            dq_scr[...] += jnp.dot(dz, kb, preferred_element_type=F32)
            dk_scr[pl.ds(k0, LANES), :] += lax.dot_general(dz, q, _DIMS["tn"], preferred_element_type=F32)
            dv_scr[pl.ds(k0, LANES), :] += lax.dot_general(
                wgt.astype(BF16), dob, _DIMS["tn"], preferred_element_type=F32)
            cl_scr[...] = jnp.broadcast_to(pin[:, LANES - 1:LANES], (tq, LANES))
            cg_scr[...] = jnp.broadcast_to(pex[:, LANES - 1:LANES] + g[:, LANES - 1:LANES], (tq, LANES))
            return carry

        lax.fori_loop(0, nkb, step, 0)
        dq_ref[...] = (dq_scr[...] * scale).astype(BF16)

        @pl.when(qi == nq - 1)
        def _():
            dk_ref[...] = (dk_scr[...] * scale).astype(BF16)
            dv_ref[...] = dv_scr[...].astype(BF16)

    qblk = pl.BlockSpec((tq, dh), lambda h, i: (i, h))
    hblk = pl.BlockSpec((s, dh), lambda h, i: (0, h))
    out = jax.ShapeDtypeStruct((s, heads * dh), BF16)
    return pl.pallas_call(
        body, name="attn_bwd", grid=(heads, nq),
        in_specs=[qblk, pl.BlockSpec((s, dh), lambda h, i: (0, heads + h)),
                  pl.BlockSpec((s, dh), lambda h, i: (0, 2 * heads + h)), qblk, qblk],
        out_specs=[qblk, hblk, hblk], out_shape=[out, out, out],
        scratch_shapes=[pltpu.VMEM((tq, dh), F32), pltpu.VMEM((s, dh), F32), pltpu.VMEM((s, dh), F32),
                        pltpu.VMEM((tq, dh), F32), pltpu.VMEM((tq, dh), F32)],
        compiler_params=_cp(("parallel", "arbitrary")),
    )(qkv, qkv, qkv, tot, do)


def _place():
    x, y, c = lax.axis_index("x"), lax.axis_index("y"), lax.axis_index("c")
    chips = [(1 - x, y), (x, 1 - y), (1 - x, 1 - y)]
    return x, y, c, chips


def _hbm_specs(n):
    return [pl.BlockSpec(memory_space=pl.ANY) for _ in range(n)]


def _remote(src, dst, send_sem, recv_sem, dev):
    return pltpu.make_async_remote_copy(
        src_ref=src, dst_ref=dst, send_sem=send_sem, recv_sem=recv_sem, device_id=dev, device_id_type=MESH)


def _allgather_weights(name, shards):
    n = len(shards)

    def body(*refs):
        ins, outs = refs[:n], refs[n:2 * n]
        send_sems, recv_sems, fsend_sems, frecv_sems, local_sems = refs[2 * n:]
        x, y, c, chips = _place()
        me = 2 * x + y
        sibling = (x, y, 1 - c)
        firsts, locals_ = [], []
        for w in range(n):
            hr = ins[w].shape[0] // 2
            mine = pl.ds(c * hr, hr)
            lc = pltpu.make_async_copy(ins[w], outs[w].at[me], local_sems.at[w])
            lc.start()
            locals_.append(lc)
            for k, (px, py) in enumerate(chips):
                cp = _remote(ins[w].at[mine], outs[w].at[me, mine], send_sems.at[w, k], recv_sems.at[w, k],
                             (px, py, c))
                cp.start()
                firsts.append(cp)
        passed = []
        for w in range(n):
            hr = ins[w].shape[0] // 2
            mine = pl.ds(c * hr, hr)
            for k, (px, py) in enumerate(chips):
                slot = outs[w].at[2 * px + py, mine]
                _remote(slot, slot, send_sems.at[w, k], recv_sems.at[w, k], (px, py, c)).wait_recv()
                cp = _remote(slot, slot, fsend_sems.at[w, k], frecv_sems.at[w, k], sibling)
                cp.start()
                passed.append(cp)
        for w in range(n):
            hr = ins[w].shape[0] // 2
            other = pl.ds((1 - c) * hr, hr)
            for k, (px, py) in enumerate(chips):
                slot = outs[w].at[2 * px + py, other]
                _remote(slot, slot, fsend_sems.at[w, k], frecv_sems.at[w, k], sibling).wait_recv()
        for cp in firsts + passed:
            cp.wait_send()
        for lc in locals_:
            lc.wait()

    sem = lambda: pltpu.SemaphoreType.DMA((n, 3))
    return pl.pallas_call(
        body, name=name, in_specs=_hbm_specs(n), out_specs=_hbm_specs(n),
        out_shape=[jax.ShapeDtypeStruct((N_CHIPS,) + s.shape, s.dtype) for s in shards],
        scratch_shapes=[sem(), sem(), sem(), sem(), pltpu.SemaphoreType.DMA((n,))],
    )(*shards)


def _exchange_sibling_halves(name, slabs):
    n = len(slabs)

    def body(*refs):
        ins, outs = refs[:n], refs[n:2 * n]
        send_sems, recv_sems = refs[2 * n:]
        x, y, c, _ = _place()
        cps = []
        for w in range(n):
            hr = ins[w].shape[1] // 2
            cp = _remote(ins[w].at[:, pl.ds((1 - c) * hr, hr), :], outs[w], send_sems.at[w], recv_sems.at[w],
                         (x, y, 1 - c))
            cp.start()
            cps.append(cp)
        for cp in cps:
            cp.wait()

    return pl.pallas_call(
        body, name=name, in_specs=_hbm_specs(n), out_specs=_hbm_specs(n),
        out_shape=[jax.ShapeDtypeStruct((N_CHIPS, s.shape[1] // 2, s.shape[2]), s.dtype) for s in slabs],
        scratch_shapes=[pltpu.SemaphoreType.DMA((n,)), pltpu.SemaphoreType.DMA((n,))],
    )(*slabs)


def _exchange_chips(name, parts):
    n = len(parts)

    def body(*refs):
        ins, outs = refs[:n], refs[n:2 * n]
        send_sems, recv_sems, local_sems = refs[2 * n:]
        x, y, c, chips = _place()
        me = 2 * x + y
        cps, locals_ = [], []
        for w in range(n):
            lc = pltpu.make_async_copy(ins[w].at[me], outs[w].at[me], local_sems.at[w])
            lc.start()
            locals_.append(lc)
            for k, (px, py) in enumerate(chips):
                cp = _remote(ins[w].at[2 * px + py], outs[w].at[me], send_sems.at[w, k], recv_sems.at[w, k],
                             (px, py, c))
                cp.start()
                cps.append(cp)
        for w in range(n):
            for k, (px, py) in enumerate(chips):
                slot = outs[w].at[2 * px + py]
                _remote(slot, slot, send_sems.at[w, k], recv_sems.at[w, k], (px, py, c)).wait_recv()
        for cp in cps:
            cp.wait_send()
        for lc in locals_:
            lc.wait()

    return pl.pallas_call(
        body, name=name, in_specs=_hbm_specs(n), out_specs=_hbm_specs(n),
        out_shape=[jax.ShapeDtypeStruct(s.shape, s.dtype) for s in parts],
        scratch_shapes=[pltpu.SemaphoreType.DMA((n, 3)), pltpu.SemaphoreType.DMA((n, 3)),
                        pltpu.SemaphoreType.DMA((n,))],
    )(*parts)


def _join_sibling_halves(name, halves):
    n = len(halves)

    def body(*refs):
        ins, outs = refs[:n], refs[n:2 * n]
        send_sems, recv_sems, local_sems = refs[2 * n:]
        x, y, c, _ = _place()
        cps, locals_ = [], []
        for w in range(n):
            hr = ins[w].shape[0]
            mine = outs[w].at[pl.ds(c * hr, hr)]
            lc = pltpu.make_async_copy(ins[w], mine, local_sems.at[w])
            lc.start()
            locals_.append(lc)
            cp = _remote(ins[w], mine, send_sems.at[w], recv_sems.at[w], (x, y, 1 - c))
            cp.start()
            cps.append(cp)
        for w in range(n):
            hr = ins[w].shape[0]
            other = outs[w].at[pl.ds((1 - c) * hr, hr)]
            _remote(other, other, send_sems.at[w], recv_sems.at[w], (x, y, 1 - c)).wait_recv()
        for cp in cps:
            cp.wait_send()
        for lc in locals_:
            lc.wait()

    return pl.pallas_call(
        body, name=name, in_specs=_hbm_specs(n), out_specs=_hbm_specs(n),
        out_shape=[jax.ShapeDtypeStruct((2 * s.shape[0], s.shape[1]), s.dtype) for s in halves],
        scratch_shapes=[pltpu.SemaphoreType.DMA((n,)), pltpu.SemaphoreType.DMA((n,)),
                        pltpu.SemaphoreType.DMA((n,))],
    )(*halves)


def _allgather_chips_small(name, v):
    r = v.shape[0]

    def body(v_ref, o_ref, send_sems, recv_sems):
        x, y, c, chips = _place()
        me = 2 * x + y
        o_ref[me] = v_ref[...]
        cps = []
        for k, (px, py) in enumerate(chips):
            cp = _remote(v_ref, o_ref.at[me], send_sems.at[k], recv_sems.at[k], (px, py, c))
            cp.start()
            cps.append(cp)
        for k, (px, py) in enumerate(chips):
            slot = o_ref.at[2 * px + py]
            _remote(slot, slot, send_sems.at[k], recv_sems.at[k], (px, py, c)).wait_recv()
        for cp in cps:
            cp.wait_send()

    return pl.pallas_call(
        body, name=name, in_specs=[pl.BlockSpec(memory_space=pltpu.VMEM)],
        out_specs=pl.BlockSpec(memory_space=pltpu.VMEM),
        out_shape=jax.ShapeDtypeStruct((N_CHIPS, r, LANES), F32),
        scratch_shapes=[pltpu.SemaphoreType.DMA((3,)), pltpu.SemaphoreType.DMA((3,))],
    )(v)


def _allreduce_small(name, v):
    r = v.shape[0]

    def body(v_ref, o_ref, all_ref, send_sems, recv_sems):
        x, y, c, _ = _place()
        me = 4 * x + 2 * y + c
        all_ref[me] = v_ref[...]
        peers = [(1 - x if k & 4 else x, 1 - y if k & 2 else y, 1 - c if k & 1 else c)
                 for k in range(1, N_DEV)]
        cps = []
        for k, dev in enumerate(peers):
            cp = _remote(v_ref, all_ref.at[me], send_sems.at[k], recv_sems.at[k], dev)
            cp.start()
            cps.append(cp)
        for k, (px, py, pc) in enumerate(peers):
            slot = all_ref.at[4 * px + 2 * py + pc]
            _remote(slot, slot, send_sems.at[k], recv_sems.at[k], (px, py, pc)).wait_recv()
        for cp in cps:
            cp.wait_send()
        total = all_ref[0]
        for d in range(1, N_DEV):
            total = total + all_ref[d]
        o_ref[...] = total

    return pl.pallas_call(
        body, name=name, in_specs=[pl.BlockSpec(memory_space=pltpu.VMEM)],
        out_specs=pl.BlockSpec(memory_space=pltpu.VMEM),
        out_shape=jax.ShapeDtypeStruct((r, LANES), F32),
        scratch_shapes=[pltpu.VMEM((N_DEV, r, LANES), F32), pltpu.SemaphoreType.DMA((N_DEV - 1,)),
                        pltpu.SemaphoreType.DMA((N_DEV - 1,))],
    )(v)


def _add_sibling(name, slabs, recv, c):
    _, r, cols = slabs.shape
    hr = r // 2
    tr = _pick(hr, ROW_TILE)
    nb = hr // tr

    def body(c_ref, a_ref, b_ref, o_ref):
        o_ref[...] = (a_ref[...].astype(F32) + b_ref[...].astype(F32)).astype(BF16)

    grid_spec = pltpu.PrefetchScalarGridSpec(
        num_scalar_prefetch=1, grid=(N_CHIPS, nb),
        in_specs=[pl.BlockSpec((None, tr, cols), lambda j, i, c_ref: (j, c_ref[0] * nb + i, 0)),
                  pl.BlockSpec((None, tr, cols), lambda j, i, c_ref: (j, i, 0))],
        out_specs=pl.BlockSpec((None, tr, cols), lambda j, i, c_ref: (j, i, 0)))
    return pl.pallas_call(
        body, name=name, grid_spec=grid_spec,
        out_shape=jax.ShapeDtypeStruct((N_CHIPS, hr, cols), BF16),
        compiler_params=_cp(("parallel", "parallel")))(jnp.reshape(c, (1,)).astype(jnp.int32), slabs, recv)


def _sum_chips(name, parts):
    _, r, cols = parts.shape
    tr = _pick(r, ROW_TILE)

    def body(p_ref, o_ref):
        total = p_ref[0].astype(F32)
        for j in range(1, N_CHIPS):
            total = total + p_ref[j].astype(F32)
        o_ref[...] = total

    return pl.pallas_call(
        body, name=name, grid=(r // tr,),
        in_specs=[pl.BlockSpec((N_CHIPS, tr, cols), lambda i: (0, i, 0))],
        out_specs=pl.BlockSpec((tr, cols), lambda i: (i, 0)),
        out_shape=jax.ShapeDtypeStruct((r, cols), F32),
        compiler_params=_cp(("parallel",)))(parts)


def _adamw_math(w, g, m, v):
    m = ADAM_B1 * m + (1.0 - ADAM_B1) * g
    v = ADAM_B2 * v + (1.0 - ADAM_B2) * (g * g)
    m_hat = m / (1.0 - ADAM_B1 ** ADAM_STEP)
    v_hat = v / (1.0 - ADAM_B2 ** ADAM_STEP)
    delta = -ADAM_LR * (m_hat / (jnp.sqrt(v_hat) + ADAM_EPS) + ADAM_WD * w)
    return delta, m, v


def _adamw(name, w, gs, m, v):
    nl, r, cols = w.shape
    tr = _pick(r, LANES)

    def body(*refs):
        w_ref, m_ref, v_ref = refs[0:3]
        g_refs = refs[3:3 + nl]
        go_ref, d_ref, nm_ref, nv_ref = refs[3 + nl:]
        layer = pl.program_id(0)
        g = g_refs[0][...]
        for j in range(1, nl):
            g = jnp.where(layer == j, g_refs[j][...], g)
        d, nm, nv = _adamw_math(w_ref[...], g, m_ref[...], v_ref[...])
        go_ref[...] = g
        d_ref[...] = d
        nm_ref[...] = nm
        nv_ref[...] = nv

    spec3 = pl.BlockSpec((None, tr, cols), lambda l, i: (l, i, 0))
    gspec = pl.BlockSpec((tr, cols), lambda l, i: (i, 0))
    out = jax.ShapeDtypeStruct((nl, r, cols), F32)
    return pl.pallas_call(
        body, name=name, grid=(nl, r // tr), in_specs=[spec3] * 3 + [gspec] * nl, out_specs=[spec3] * 4,
        out_shape=[out] * 4, compiler_params=_cp(("parallel", "parallel")))(w, m, v, *gs)


def _adamw_small(name, groups):
    n = len(groups)
    flat = [a for grp in groups for a in grp]

    def body(*refs):
        ins, outs = refs[:4 * n], refs[4 * n:]
        for p in range(n):
            w_ref, g_ref, m_ref, v_ref = ins[4 * p:4 * p + 4]
            d, nm, nv = _adamw_math(w_ref[...], g_ref[...], m_ref[...], v_ref[...])
            outs[3 * p][...] = d
            outs[3 * p + 1][...] = nm
            outs[3 * p + 2][...] = nv

    vm = pl.BlockSpec(memory_space=pltpu.VMEM)
    out_shape = [jax.ShapeDtypeStruct(grp[0].shape, F32) for grp in groups for _ in range(3)]
    res = pl.pallas_call(
        body, name=name, in_specs=[vm] * (4 * n), out_specs=[vm] * (3 * n), out_shape=out_shape)(*flat)
    return [tuple(res[3 * p:3 * p + 3]) for p in range(n)]


def _block_diag_pairs(w):
    h, d, _ = w.shape
    z = jnp.zeros((h // 2, d, d), w.dtype)
    top = jnp.concatenate([w[0::2], z], axis=2)
    bot = jnp.concatenate([z, w[1::2]], axis=2)
    return jnp.concatenate([top, bot], axis=1).astype(BF16)


def _diag_pairs_to_heads(g, d):
    a = g[:, :d, :d]
    b = g[:, d:, d:]
    return jnp.stack([a, b], axis=1).reshape(-1, d, d)


def _rows128(a):
    flat = a.reshape(-1, LANES)
    pad = (-flat.shape[0]) % SUBLANES
    if pad:
        flat = jnp.concatenate([flat, jnp.zeros((pad, LANES), flat.dtype)], axis=0)
    return flat


def _unshard_last(g4, shape):
    g4 = g4.reshape((N_CHIPS,) + tuple(shape))
    return jnp.concatenate([g4[j] for j in range(N_CHIPS)], axis=-1)


def kernel(x, norm_gains, hyb_w_in, hyb_conv_a, hyb_conv_b, hyb_conv_b_bias, hyb_rg_w_a, hyb_rg_b_a, hyb_rg_w_x, hyb_rg_b_x, hyb_rg_lambda, hyb_w_out, sb_w_qkv, sb_w_o, mlp_w_up, mlp_w_down, loss_target, m_norm_gains, m_hyb_w_in, m_hyb_conv_a, m_hyb_conv_b, m_hyb_conv_b_bias, m_hyb_rg_w_a, m_hyb_rg_b_a, m_hyb_rg_w_x, m_hyb_rg_b_x, m_hyb_rg_lambda, m_hyb_w_out, m_sb_w_qkv, m_sb_w_o, m_mlp_w_up, m_mlp_w_down, v_norm_gains, v_hyb_w_in, v_hyb_conv_a, v_hyb_conv_b, v_hyb_conv_b_bias, v_hyb_rg_w_a, v_hyb_rg_b_a, v_hyb_rg_w_x, v_hyb_rg_b_x, v_hyb_rg_lambda, v_hyb_w_out, v_sb_w_qkv, v_sb_w_o, v_mlp_w_up, v_mlp_w_down):
    cx_ = lax.axis_index("x")
    cy_ = lax.axis_index("y")
    cc_ = lax.axis_index("c")
    chip = 2 * cx_ + cy_

    x0 = x[0]
    target = loss_target[0]
    s, d = x0.shape
    heads = SB_HEADS
    assert d // heads == LANES
    n_rg, hd = hyb_rg_w_a.shape[1], hyb_rg_w_a.shape[2]
    wmix = n_rg * hd
    assert 2 * hd == LANES

    big = {
        "hyb_w_in": hyb_w_in[0], "hyb_w_out": hyb_w_out[0], "sb_w_qkv": sb_w_qkv[0], "sb_w_o": sb_w_o[0],
        "mlp_w_up0": mlp_w_up[0], "mlp_w_down0": mlp_w_down[0],
        "mlp_w_up1": mlp_w_up[1], "mlp_w_down1": mlp_w_down[1],
    }
    names = list(big)
    shards_bf = [_cast_bf16("cast_" + k, big[k]) for k in names]
    full = dict(zip(names, _allgather_weights("allgather_weights", shards_bf)))
    rowsharded = lambda k: full[k].reshape(-1, full[k].shape[2])

    ng_s, ca_s, cb_s = norm_gains.reshape(-1, norm_gains.shape[2]), hyb_conv_a[0], hyb_conv_b[0]
    packed = jnp.concatenate([_rows128(ng_s), _rows128(ca_s), _rows128(cb_s)], axis=0)
    gathered = _allgather_chips_small("allgather_small", packed)
    n0 = ng_s.size // LANES
    n1 = n0 + (-n0) % SUBLANES
    m0 = ca_s.size // LANES
    m1 = m0 + (-m0) % SUBLANES
    k0 = cb_s.size // LANES
    gains = _unshard_last(gathered[:, 0:n0], ng_s.shape).reshape(2, 4, 1, d)
    conv_a = _unshard_last(gathered[:, n1:n1 + m0], ca_s.shape)
    conv_b = _unshard_last(gathered[:, n1 + m1:n1 + m1 + k0], cb_s.shape)
    bias, b_a, b_x, lam = hyb_conv_b_bias, hyb_rg_b_a, hyb_rg_b_x, hyb_rg_lambda
    wa_blk = _block_diag_pairs(hyb_rg_w_a[0])
    wx_blk = _block_diag_pairs(hyb_rg_w_x[0])

    relu_sq = lambda acc: (jnp.maximum(acc, 0.0), jnp.square(jnp.maximum(acc, 0.0)))

    h1 = _rms_fwd("rms_pre0", x0, gains[0, 0])
    proj = _mm_fwd_col("proj_in", h1, full["hyb_w_in"])[0]
    ycat, hseq = _mixer_fwd(proj, conv_a, conv_b, bias, wa_blk, b_a, wx_blk, b_x, lam)
    mix0 = _mm_fwd_row("proj_out", ycat, rowsharded("hyb_w_out"))
    x1, h2 = _rms_post("rms_mix0", mix0, gains[0, 1], x0, gains[0, 2])
    u0, a0 = _mm_fwd_col("mlp_up0", h2, full["mlp_w_up0"], (BF16, BF16), relu_sq)
    mlp0 = _mm_fwd_row("mlp_down0", a0, rowsharded("mlp_w_down0"))
    x2, h3 = _rms_post("rms_mlp0", mlp0, gains[0, 3], x1, gains[1, 0])

    qkv = _mm_fwd_col("qkv", h3, full["sb_w_qkv"], (BF16,))[0]
    att, tot = _attn_fwd(qkv, heads)
    mix1 = _mm_fwd_row("attn_out", att, rowsharded("sb_w_o"))
    x3, h4 = _rms_post("rms_mix1", mix1, gains[1, 1], x2, gains[1, 2])
    u1, a1 = _mm_fwd_col("mlp_up1", h4, full["mlp_w_up1"], (BF16, BF16), relu_sq)
    mlp1 = _mm_fwd_row("mlp_down1", a1, rowsharded("mlp_w_down1"))
    (x4,) = _rms_post("rms_mlp1", mlp1, gains[1, 3], x3)

    dy, loss_local = _loss_head("loss_head", x4, target)
    loss = lax.psum(loss_local, ("x", "y", "c"))

    grads_big = {}
    dgain = [[None] * 4 for _ in range(2)]
    drelu = lambda acc, u: (acc * (2.0 * u.astype(F32)),)

    def mlp_bwd(layer, dxo, mlp_out, xin, hin, u, a):
        dmlp, dgain[layer][3] = _rms_bwd(f"rms_mlp{layer}_bwd", mlp_out, gains[layer, 3], dxo, out_dtype=BF16)
        wd, wu = rowsharded(f"mlp_w_down{layer}"), full[f"mlp_w_up{layer}"]
        grads_big[f"mlp_w_down{layer}"] = _mm_wgrad_row(f"mlp_down{layer}_wgrad", a, dmlp).reshape(
            N_CHIPS, -1, d)
        du = _mm_bwd_row(f"mlp_down{layer}_bwd", dmlp, wd, (BF16,), u, drelu)[0]
        grads_big[f"mlp_w_up{layer}"] = _mm_wgrad_col(f"mlp_up{layer}_wgrad", hin, du, wu.shape[2])
        dh = _mm_bwd_col(f"mlp_up{layer}_bwd", du, wu)
        dxm, dgain[layer][2] = _rms_bwd(f"rms_premlp{layer}_bwd", xin, gains[layer, 2], dh, res=dxo)
        return dxm

    dx3 = mlp_bwd(1, dy, mlp1, x3, h4, u1, a1)
    dmix1, dgain[1][1] = _rms_bwd("rms_mix1_bwd", mix1, gains[1, 1], dx3, out_dtype=BF16)
    grads_big["sb_w_o"] = _mm_wgrad_row("attn_out_wgrad", att, dmix1).reshape(N_CHIPS, -1, d)
    datt = _mm_bwd_row("attn_out_bwd", dmix1, rowsharded("sb_w_o"), (BF16,))[0]
    dq, dk, dv = _attn_bwd(qkv, tot, datt, heads)
    dqkv = jnp.concatenate([dq, dk, dv], axis=1)
    grads_big["sb_w_qkv"] = _mm_wgrad_col("qkv_wgrad", h3, dqkv, full["sb_w_qkv"].shape[2])
    dh3 = _mm_bwd_col("qkv_bwd", dqkv, full["sb_w_qkv"])
    dx2, dgain[1][0] = _rms_bwd("rms_pre1_bwd", x2, gains[1, 0], dh3, res=dx3)

    dx1 = mlp_bwd(0, dx2, mlp0, x1, h2, u0, a0)
    dmix0, dgain[0][1] = _rms_bwd("rms_mix0_bwd", mix0, gains[0, 1], dx1, out_dtype=BF16)
    grads_big["hyb_w_out"] = _mm_wgrad_row("proj_out_wgrad", ycat, dmix0).reshape(N_CHIPS, -1, d)
    dycat = _mm_bwd_row("proj_out_bwd", dmix0, rowsharded("hyb_w_out"))[0]
    dproj, xr_b, dpa_b, dpx_b, sg = _mixer_bwd(
        proj, hseq, dycat, conv_a, conv_b, bias, wa_blk, b_a, wx_blk, b_x, lam)
    grads_big["hyb_w_in"] = _mm_wgrad_col("proj_in_wgrad", h1, dproj, full["hyb_w_in"].shape[2])
    dh1 = _mm_bwd_col("proj_in_bwd", dproj, full["hyb_w_in"])
    dx0, dgain[0][0] = _rms_bwd("rms_pre0_bwd", x0, gains[0, 0], dh1, res=dx1)
    dwa = _diag_pairs_to_heads(_mm_wgrad_diag("rg_w_a_wgrad", xr_b, dpa_b), hd)
    dwx = _diag_pairs_to_heads(_mm_wgrad_diag("rg_w_x_wgrad", xr_b, dpx_b), hd)

    dgains = jnp.concatenate([dgain[l][k] for l in range(2) for k in range(4)], axis=0)
    small_parts = [dgains, sg[_SG_CONV_A:_SG_CONV_A + 3], sg[_SG_CONV_B:_SG_CONV_B + 4], sg[_SG_BIAS:_SG_BIAS + 1],
                   dwa, sg[_SG_BA:_SG_BA + 1], dwx, sg[_SG_BX:_SG_BX + 1], sg[_SG_LAM:_SG_LAM + 1]]
    small_rows = [_rows128(p) for p in small_parts]
    reduced = _allreduce_small("allreduce_small", jnp.concatenate(small_rows, axis=0))
    small_full, off = [], 0
    for p, rws in zip(small_parts, small_rows):
        small_full.append(reduced[off:off + p.size // LANES].reshape(p.shape))
        off += rws.shape[0]
    g_gains, g_ca, g_cb, g_bias, g_wa, g_ba, g_wx, g_bx, g_lam = small_full

    def my_cols(g, width):
        return lax.dynamic_slice_in_dim(g, chip * width, width, axis=g.ndim - 1)

    small = [
        ("norm_gains", norm_gains, my_cols(g_gains, norm_gains.shape[2]).reshape(norm_gains.shape),
         m_norm_gains, v_norm_gains),
        ("hyb_conv_a", hyb_conv_a, my_cols(g_ca, hyb_conv_a.shape[2])[None], m_hyb_conv_a, v_hyb_conv_a),
        ("hyb_conv_b", hyb_conv_b, my_cols(g_cb, hyb_conv_b.shape[2])[None], m_hyb_conv_b, v_hyb_conv_b),
        ("hyb_conv_b_bias", hyb_conv_b_bias, g_bias, m_hyb_conv_b_bias, v_hyb_conv_b_bias),
        ("hyb_rg_w_a", hyb_rg_w_a, g_wa[None], m_hyb_rg_w_a, v_hyb_rg_w_a),
        ("hyb_rg_b_a", hyb_rg_b_a, g_ba, m_hyb_rg_b_a, v_hyb_rg_b_a),
        ("hyb_rg_w_x", hyb_rg_w_x, g_wx[None], m_hyb_rg_w_x, v_hyb_rg_w_x),
        ("hyb_rg_b_x", hyb_rg_b_x, g_bx, m_hyb_rg_b_x, v_hyb_rg_b_x),
        ("hyb_rg_lambda", hyb_rg_lambda, g_lam, m_hyb_rg_lambda, v_hyb_rg_lambda),
    ]
    to2d = lambda a: a.reshape(-1, a.shape[-1])
    small_res = _adamw_small("adamw_small", [tuple(to2d(a) for a in (w, g, m, v)) for _, w, g, m, v in small])
    out = {}
    for (nm, w, g, _, _), (dl, nmom, nvar) in zip(small, small_res):
        out[nm] = (g, dl.reshape(w.shape), nmom.reshape(w.shape), nvar.reshape(w.shape))

    slabs = [grads_big[k] for k in names]
    recv_sib = _exchange_sibling_halves("grads_to_sibling", slabs)
    chip_part = [_add_sibling("grads_add_" + k, sl, rv, cc_) for k, sl, rv in zip(names, slabs, recv_sib)]
    recv_chip = _exchange_chips("grads_to_chips", chip_part)
    halves = [_sum_chips("grads_sum_" + k, rc) for k, rc in zip(names, recv_chip)]
    gfull = dict(zip(names, _join_sibling_halves("grads_join", halves)))

    stacked = {
        "hyb_w_in": (hyb_w_in, m_hyb_w_in, v_hyb_w_in, ["hyb_w_in"]),
        "hyb_w_out": (hyb_w_out, m_hyb_w_out, v_hyb_w_out, ["hyb_w_out"]),
        "sb_w_qkv": (sb_w_qkv, m_sb_w_qkv, v_sb_w_qkv, ["sb_w_qkv"]),
        "sb_w_o": (sb_w_o, m_sb_w_o, v_sb_w_o, ["sb_w_o"]),
        "mlp_w_up": (mlp_w_up, m_mlp_w_up, v_mlp_w_up, ["mlp_w_up0", "mlp_w_up1"]),
        "mlp_w_down": (mlp_w_down, m_mlp_w_down, v_mlp_w_down, ["mlp_w_down0", "mlp_w_down1"]),
    }
    for k, (w, m, v, parts) in stacked.items():
        out[k] = tuple(_adamw("adamw_" + k, w, [gfull[p] for p in parts], m, v))

    order = ["norm_gains", "hyb_w_in", "hyb_conv_a", "hyb_conv_b", "hyb_conv_b_bias", "hyb_rg_w_a", "hyb_rg_b_a",
             "hyb_rg_w_x", "hyb_rg_b_x", "hyb_rg_lambda", "hyb_w_out", "sb_w_qkv", "sb_w_o", "mlp_w_up",
             "mlp_w_down"]
    return (loss, dx0[None], *[out[k][0] for k in order], *[out[k][1] for k in order],
            *[out[k][2] for k in order], *[out[k][3] for k in order])
```

```python
import functools
import math

import jax
import jax.numpy as jnp
from jax import lax
from jax.experimental import pallas as pl
from jax.experimental.pallas import tpu as pltpu

F32 = jnp.float32
BF16 = jnp.bfloat16
MESH = pl.DeviceIdType.MESH

SB_HEADS = 16
NORM_EPS = 1e-6
LRU_C = 8.0
ADAM_LR = 0.001
ADAM_B1 = 0.9
ADAM_B2 = 0.999
ADAM_EPS = 1e-08
ADAM_WD = 0.01
ADAM_STEP = 10

LANES = 128
SUBLANES = 8
VMEM_LIMIT = 48 * 1024 * 1024
MM_TILE = 1024
MM_TILE_K = 2048
ROW_TILE = 256
ATT_TILE = 512
N_CHIPS = 4
N_DEV = 8

_DIMS = {
    "nn": (((1,), (0,)), ((), ())),
    "nt": (((1,), (1,)), ((), ())),
    "tn": (((0,), (0,)), ((), ())),
}


def _cp(sem=None, vmem=VMEM_LIMIT):
    return pltpu.CompilerParams(dimension_semantics=sem, vmem_limit_bytes=vmem)


def _pick(dim, pref):
    t = min(dim, pref)
    while dim % t:
        t -= LANES
    return t


def _whole(shape):
    nd = len(shape)
    return pl.BlockSpec(tuple(shape), lambda *_: (0,) * nd)


def _sigmoid(z):
    return 1.0 / (1.0 + jnp.exp(-z))


def _log_sigmoid(z):
    return jnp.minimum(z, 0.0) - jnp.log(1.0 + jnp.exp(-jnp.abs(z)))


def _expm1(z):
    series = z * (1.0 + z * (0.5 + z * (1.0 / 6.0 + z * (1.0 / 24.0))))
    return jnp.where(jnp.abs(z) < 0.05, series, jnp.exp(z) - 1.0)


_GELU_C = math.sqrt(2.0 / math.pi)


def _gelu_and_grad(g):
    inner = _GELU_C * (g + 0.044715 * g * g * g)
    t = jnp.tanh(inner)
    val = 0.5 * g * (1.0 + t)
    grad = 0.5 * (1.0 + t) + 0.5 * g * (1.0 - t * t) * _GELU_C * (1.0 + 3.0 * 0.044715 * g * g)
    return val, grad


def _shift_down(cur, prev8, k, rows):
    n = cur.shape[0]
    rolled = pltpu.roll(cur, k, 0)
    head = jnp.tile(pltpu.roll(prev8, k, 0), (n // SUBLANES, 1))
    return jnp.where(rows < k, head, rolled)


def _shift_up(cur, next8, k, rows):
    n = cur.shape[0]
    rolled = pltpu.roll(cur, n - k, 0)
    tail = jnp.tile(pltpu.roll(next8, SUBLANES - k, 0), (n // SUBLANES, 1))
    return jnp.where(rows >= n - k, tail, rolled)


def _colsum(v):
    return jnp.sum(v, axis=0, keepdims=True)


def _matmul(name, mode, grid, operands, in_specs, out_shapes, out_specs, acc_shape, epilogue=None):
    nk = grid[2]
    n_in = len(operands)
    dims = _DIMS[mode]

    def finish(acc, extra, outs):
        res = epilogue(acc, *[e[...] for e in extra]) if epilogue is not None else (acc,)
        for o_ref, o in zip(outs, res):
            o_ref[...] = o.astype(o_ref.dtype)

    def product(a_ref, b_ref):
        return lax.dot_general(a_ref[...].astype(BF16), b_ref[...].astype(BF16), dims, preferred_element_type=F32)

    def body_single(*refs):
        finish(product(refs[0], refs[1]), refs[2:n_in], refs[n_in:])

    def body(*refs):
        extra = refs[2:n_in]
        outs = refs[n_in:-1]
        acc_ref = refs[-1]
        k = pl.program_id(2)

        @pl.when(k == 0)
        def _():
            acc_ref[...] = product(refs[0], refs[1])

        @pl.when(k > 0)
        def _():
            acc_ref[...] += product(refs[0], refs[1])

        @pl.when(k == nk - 1)
        def _():
            finish(acc_ref[...], extra, outs)

    return pl.pallas_call(
        body_single if nk == 1 else body, name=name, grid=grid, in_specs=in_specs, out_specs=out_specs,
        out_shape=out_shapes, scratch_shapes=[] if nk == 1 else [pltpu.VMEM(acc_shape, F32)],
        compiler_params=_cp(("parallel", "parallel", "arbitrary")),
    )(*operands)


def _mm_fwd_col(name, a, wfull, out_dtypes=(F32,), epilogue=None):
    s, kdim = a.shape
    _, _, cs = wfull.shape
    tm, tk, tn = _pick(s, MM_TILE), _pick(kdim, MM_TILE_K), _pick(cs, MM_TILE)
    nbj = cs // tn
    grid = (s // tm, N_CHIPS * nbj, kdim // tk)
    out_shapes = [jax.ShapeDtypeStruct((s, N_CHIPS * cs), dt) for dt in out_dtypes]
    out_specs = [pl.BlockSpec((tm, tn), lambda i, n, k: (i, n)) for _ in out_dtypes]
    return _matmul(
        name, "nn", grid, [a, wfull],
        [pl.BlockSpec((tm, tk), lambda i, n, k: (i, k)),
         pl.BlockSpec((None, tk, tn), lambda i, n, k: (n // nbj, k, n % nbj))],
        out_shapes, out_specs, (tm, tn), epilogue)


def _mm_fwd_row(name, a, w2d, out_dtype=F32):
    s, kdim = a.shape
    _, n_out = w2d.shape
    tm, tk, tn = _pick(s, MM_TILE), _pick(kdim, MM_TILE_K), _pick(n_out, MM_TILE)
    grid = (s // tm, n_out // tn, kdim // tk)
    return _matmul(
        name, "nn", grid, [a, w2d],
        [pl.BlockSpec((tm, tk), lambda i, n, k: (i, k)),
         pl.BlockSpec((tk, tn), lambda i, n, k: (k, n))],
        [jax.ShapeDtypeStruct((s, n_out), out_dtype)],
        [pl.BlockSpec((tm, tn), lambda i, n, k: (i, n))], (tm, tn))[0]


def _mm_bwd_col(name, dy, wfull, out_dtype=F32):
    s, _ = dy.shape
    _, kdim, cs = wfull.shape
    tm, tn, tk = _pick(s, MM_TILE), _pick(kdim, MM_TILE), _pick(cs, MM_TILE_K)
    nbj = cs // tk
    grid = (s // tm, kdim // tn, N_CHIPS * nbj)
    return _matmul(
        name, "nt", grid, [dy, wfull],
        [pl.BlockSpec((tm, tk), lambda i, n, k: (i, k)),
         pl.BlockSpec((None, tn, tk), lambda i, n, k: (k // nbj, n, k % nbj))],
        [jax.ShapeDtypeStruct((s, kdim), out_dtype)],
        [pl.BlockSpec((tm, tn), lambda i, n, k: (i, n))], (tm, tn))[0]


def _mm_bwd_row(name, dy, w2d, out_dtypes=(F32,), extra=None, epilogue=None):
    s, n_in = dy.shape
    kdim, _ = w2d.shape
    tm, tn, tk = _pick(s, MM_TILE), _pick(kdim, MM_TILE), _pick(n_in, MM_TILE_K)
    grid = (s // tm, kdim // tn, n_in // tk)
    operands = [dy, w2d]
    in_specs = [pl.BlockSpec((tm, tk), lambda i, n, k: (i, k)),
                pl.BlockSpec((tn, tk), lambda i, n, k: (n, k))]
    if extra is not None:
        operands.append(extra)
        in_specs.append(pl.BlockSpec((tm, tn), lambda i, n, k: (i, n)))
    return _matmul(
        name, "nt", grid, operands, in_specs,
        [jax.ShapeDtypeStruct((s, kdim), dt) for dt in out_dtypes],
        [pl.BlockSpec((tm, tn), lambda i, n, k: (i, n)) for _ in out_dtypes], (tm, tn), epilogue)


def _mm_wgrad_col(name, a, dy, cs):
    s, kdim = a.shape
    tm, tn, ts = _pick(kdim, MM_TILE), _pick(cs, MM_TILE), _pick(s, MM_TILE_K)
    nbj = cs // tn
    grid = (kdim // tm, N_CHIPS * nbj, s // ts)
    return _matmul(
        name, "tn", grid, [a, dy],
        [pl.BlockSpec((ts, tm), lambda i, n, k: (k, i)),
         pl.BlockSpec((ts, tn), lambda i, n, k: (k, n))],
        [jax.ShapeDtypeStruct((N_CHIPS, kdim, cs), BF16)],
        [pl.BlockSpec((None, tm, tn), lambda i, n, k: (n // nbj, i, n % nbj))], (tm, tn))[0]


def _mm_wgrad_row(name, a, dy):
    s, kdim = a.shape
    _, n_out = dy.shape
    tm, tn, ts = _pick(kdim, MM_TILE), _pick(n_out, MM_TILE), _pick(s, MM_TILE_K)
    grid = (kdim // tm, n_out // tn, s // ts)
    return _matmul(
        name, "tn", grid, [a, dy],
        [pl.BlockSpec((ts, tm), lambda i, n, k: (k, i)),
         pl.BlockSpec((ts, tn), lambda i, n, k: (k, n))],
        [jax.ShapeDtypeStruct((kdim, n_out), BF16)],
        [pl.BlockSpec((tm, tn), lambda i, n, k: (i, n))], (tm, tn))[0]


def _mm_wgrad_diag(name, a, dy):
    s, width = a.shape
    nb = width // LANES
    ts = _pick(s, MM_TILE)
    grid = (nb, 1, s // ts)
    return _matmul(
        name, "tn", grid, [a, dy],
        [pl.BlockSpec((ts, LANES), lambda i, n, k: (k, i)),
         pl.BlockSpec((ts, LANES), lambda i, n, k: (k, i))],
        [jax.ShapeDtypeStruct((nb, LANES, LANES), F32)],
        [pl.BlockSpec((None, LANES, LANES), lambda i, n, k: (i, 0, 0))], (LANES, LANES))[0]


def _rowspec(tr, d):
    return pl.BlockSpec((tr, d), lambda i: (i, 0))


def _vecspec(d):
    return pl.BlockSpec((1, d), lambda i: (0, 0))


def _rms(x, g):
    return x * lax.rsqrt(jnp.mean(x * x, axis=-1, keepdims=True) + NORM_EPS) * g


def _cast_into_slot(name, w, chip):
    r, c = w.shape
    tr = _pick(r, ROW_TILE)

    def body(chip_ref, w_ref, o_ref):
        o_ref[...] = w_ref[...].astype(BF16)

    grid_spec = pltpu.PrefetchScalarGridSpec(
        num_scalar_prefetch=1, grid=(r // tr,),
        in_specs=[pl.BlockSpec((tr, c), lambda i, chip_ref: (i, 0))],
        out_specs=pl.BlockSpec((None, tr, c), lambda i, chip_ref: (chip_ref[0], i, 0)))
    return pl.pallas_call(
        body, name=name, grid_spec=grid_spec, out_shape=jax.ShapeDtypeStruct((N_CHIPS, r, c), BF16),
        compiler_params=_cp(("parallel",)))(jnp.reshape(chip, (1,)).astype(jnp.int32), w)


def _rms_fwd(name, x, g):
    s, d = x.shape
    tr = _pick(s, ROW_TILE)

    def body(x_ref, g_ref, h_ref):
        h_ref[...] = _rms(x_ref[...], g_ref[...]).astype(BF16)

    return pl.pallas_call(
        body, name=name, grid=(s // tr,), in_specs=[_rowspec(tr, d), _vecspec(d)],
        out_specs=_rowspec(tr, d), out_shape=jax.ShapeDtypeStruct((s, d), BF16),
        compiler_params=_cp(("parallel",)))(x, g)


def _rms_post(name, y, g_post, res, g_next=None):
    s, d = y.shape
    tr = _pick(s, ROW_TILE)
    with_next = g_next is not None

    def body(*refs):
        if with_next:
            y_ref, gp_ref, r_ref, gn_ref, x_ref, h_ref = refs
        else:
            y_ref, gp_ref, r_ref, x_ref = refs
        xn = r_ref[...] + _rms(y_ref[...], gp_ref[...])
        x_ref[...] = xn
        if with_next:
            h_ref[...] = _rms(xn, gn_ref[...]).astype(BF16)

    operands = [y, g_post, res] + ([g_next] if with_next else [])
    in_specs = [_rowspec(tr, d), _vecspec(d), _rowspec(tr, d)] + ([_vecspec(d)] if with_next else [])
    out_shape = [jax.ShapeDtypeStruct((s, d), F32)] + ([jax.ShapeDtypeStruct((s, d), BF16)] if with_next else [])
    out_specs = [_rowspec(tr, d)] + ([_rowspec(tr, d)] if with_next else [])
    return pl.pallas_call(
        body, name=name, grid=(s // tr,), in_specs=in_specs, out_specs=out_specs, out_shape=out_shape,
        compiler_params=_cp(("parallel",)))(*operands)


def _rms_bwd(name, x, g, dy, res=None, out_dtype=F32):
    s, d = x.shape
    tr = _pick(s, ROW_TILE)
    nsteps = s // tr
    with_res = res is not None

    def body(*refs):
        if with_res:
            x_ref, g_ref, dy_ref, r_ref, dx_ref, dg_ref, acc_ref = refs
        else:
            x_ref, g_ref, dy_ref, dx_ref, dg_ref, acc_ref = refs
        i = pl.program_id(0)

        @pl.when(i == 0)
        def _():
            acc_ref[...] = jnp.zeros_like(acc_ref)

        xv = x_ref[...]
        dyv = dy_ref[...].astype(F32)
        r = lax.rsqrt(jnp.mean(xv * xv, axis=-1, keepdims=True) + NORM_EPS)
        xhat = xv * r
        gy = dyv * g_ref[...]
        dx = r * (gy - xhat * jnp.mean(gy * xhat, axis=-1, keepdims=True))
        if with_res:
            dx = dx + r_ref[...]
        dx_ref[...] = dx.astype(dx_ref.dtype)
        acc_ref[...] += jnp.sum((dyv * xhat).reshape(tr // SUBLANES, SUBLANES, d), axis=0)

        @pl.when(i == nsteps - 1)
        def _():
            dg_ref[...] = jnp.broadcast_to(_colsum(acc_ref[...]), (SUBLANES, d))

    operands = [x, g, dy] + ([res] if with_res else [])
    in_specs = [_rowspec(tr, d), _vecspec(d), _rowspec(tr, d)] + ([_rowspec(tr, d)] if with_res else [])
    dx, dg = pl.pallas_call(
        body, name=name, grid=(nsteps,), in_specs=in_specs,
        out_specs=[_rowspec(tr, d), pl.BlockSpec((SUBLANES, d), lambda i: (0, 0))],
        out_shape=[jax.ShapeDtypeStruct((s, d), out_dtype), jax.ShapeDtypeStruct((SUBLANES, d), F32)],
        scratch_shapes=[pltpu.VMEM((SUBLANES, d), F32)],
        compiler_params=_cp(("arbitrary",)))(*operands)
    return dx, dg[0:1]


def _loss_head(name, y, target):
    s, d = y.shape
    tr = _pick(s, ROW_TILE)
    nsteps = s // tr

    def body(y_ref, t_ref, dy_ref, l_ref, acc_ref):
        i = pl.program_id(0)

        @pl.when(i == 0)
        def _():
            acc_ref[...] = jnp.zeros_like(acc_ref)

        err = y_ref[...] - t_ref[...]
        dy_ref[...] = err * (1.0 / d)
        acc_ref[...] += jnp.sum((err * err).reshape(tr // SUBLANES, SUBLANES, d), axis=0)

        @pl.when(i == nsteps - 1)
        def _():
            l_ref[...] = jnp.full((SUBLANES, LANES), (0.5 / d) * jnp.sum(acc_ref[...]), F32)

    dy, l = pl.pallas_call(
        body, name=name, grid=(nsteps,), in_specs=[_rowspec(tr, d), _rowspec(tr, d)],
        out_specs=[_rowspec(tr, d), pl.BlockSpec((SUBLANES, LANES), lambda i: (0, 0))],
        out_shape=[jax.ShapeDtypeStruct((s, d), F32), jax.ShapeDtypeStruct((SUBLANES, LANES), F32)],
        scratch_shapes=[pltpu.VMEM((SUBLANES, d), F32)],
        compiler_params=_cp(("arbitrary",)))(y, target)
    return dy, l[0, 0]


def _gates(xr, wa, ba, wx, bx, lam):
    xb = xr.astype(BF16)
    r = _sigmoid(jnp.dot(xb, wa, preferred_element_type=F32) + ba)
    i = _sigmoid(jnp.dot(xb, wx, preferred_element_type=F32) + bx)
    log_a = LRU_C * r * _log_sigmoid(lam)
    a = jnp.exp(log_a)
    m = jnp.sqrt(-_expm1(2.0 * log_a))
    return r, i, a, m


def _mixer_fwd(proj, conv_a, conv_b, bias, wa_blk, ba, wx_blk, bx, lam):
    s, w5 = proj.shape
    w = w5 // 5
    nch = w // LANES
    ts = _pick(s, ROW_TILE)
    nt = s // ts

    def body(p_ref, pp_ref, ca_ref, cb_ref, bias_ref, wa_ref, ba_ref, wx_ref, bx_ref, lam_ref,
             y_ref, h_ref, a_scr, b_scr, hc_scr):
        t = pl.program_id(0)
        first = t == 0
        rows = lax.broadcasted_iota(jnp.int32, (ts, LANES), 0)

        @pl.when(first)
        def _():
            hc_scr[...] = jnp.zeros_like(hc_scr)

        def cur(comp, c):
            return p_ref[:, comp * w + c * LANES:comp * w + (c + 1) * LANES]

        def prev(comp, c):
            v = pp_ref[:, comp * w + c * LANES:comp * w + (c + 1) * LANES]
            return jnp.where(first, 0.0, v)

        for c in range(nch):
            sl = slice(c * LANES, (c + 1) * LANES)
            cx = cur(1, c) * cur(2, c)
            cxp = prev(1, c) * prev(2, c)
            wa3 = ca_ref[:, sl]
            conv = (wa3[2:3] * cx + wa3[1:2] * _shift_down(cx, cxp, 1, rows)
                    + wa3[0:1] * _shift_down(cx, cxp, 2, rows))
            y_ref[:, sl] = (cur(0, c) * conv).astype(BF16)

        for c in range(nch):
            sl = slice(c * LANES, (c + 1) * LANES)
            xb, xbp = cur(4, c), prev(4, c)
            wb4 = cb_ref[:, sl]
            xr = (wb4[3:4] * xb + wb4[2:3] * _shift_down(xb, xbp, 1, rows)
                  + wb4[1:2] * _shift_down(xb, xbp, 2, rows)
                  + wb4[0:1] * _shift_down(xb, xbp, 3, rows) + bias_ref[:, sl])
            _, i, a, m = _gates(xr, wa_ref[c], ba_ref[:, sl], wx_ref[c], bx_ref[:, sl], lam_ref[:, sl])
            a_scr[:, sl] = a
            b_scr[:, sl] = m * i * xr

        def step(r, h):
            h = a_scr[pl.ds(r, 1), :] * h + b_scr[pl.ds(r, 1), :]
            h_ref[pl.ds(r, 1), :] = h
            return h

        hc_scr[0:1, :] = lax.fori_loop(0, ts, step, hc_scr[0:1, :], unroll=8)

        for c in range(nch):
            sl = slice(c * LANES, (c + 1) * LANES)
            gel, _ = _gelu_and_grad(cur(3, c))
            y_ref[:, w + c * LANES:w + (c + 1) * LANES] = (h_ref[:, sl] * gel).astype(BF16)

    vec = lambda n: _whole((n, w))
    return pl.pallas_call(
        body, name="mixer_fwd", grid=(nt,),
        in_specs=[pl.BlockSpec((ts, w5), lambda t: (t, 0)),
                  pl.BlockSpec((SUBLANES, w5), lambda t: (jnp.maximum(t * (ts // SUBLANES) - 1, 0), 0)),
                  vec(3), vec(4), vec(1), _whole(wa_blk.shape), vec(1), _whole(wx_blk.shape), vec(1), vec(1)],
        out_specs=[pl.BlockSpec((ts, 2 * w), lambda t: (t, 0)), pl.BlockSpec((ts, w), lambda t: (t, 0))],
        out_shape=[jax.ShapeDtypeStruct((s, 2 * w), BF16), jax.ShapeDtypeStruct((s, w), F32)],
        scratch_shapes=[pltpu.VMEM((ts, w), F32), pltpu.VMEM((ts, w), F32), pltpu.VMEM((SUBLANES, w), F32)],
        compiler_params=_cp(("arbitrary",)),
    )(proj, proj, conv_a, conv_b, bias, wa_blk, ba, wx_blk, bx, lam)


_SG_CONV_A, _SG_CONV_B, _SG_BIAS, _SG_BA, _SG_BX, _SG_LAM, _SG_ROWS = 0, 3, 7, 8, 9, 10, 16


def _mixer_bwd(proj, hseq, dy, conv_a, conv_b, bias, wa_blk, ba, wx_blk, bx, lam):
    s, w5 = proj.shape
    w = w5 // 5
    nch = w // LANES
    ts = _pick(s, ROW_TILE)
    nt = s // ts
    tpb = ts // SUBLANES

    def body(p_ref, pp_ref, h_ref, hp_ref, dy_ref, ca_ref, cb_ref, bias_ref, wa_ref, ba_ref, wx_ref, bx_ref,
             lam_ref, dp_ref, xr_ref, dpa_ref, dpx_ref, sg_ref,
             a_scr, g_scr, l_scr, x_scr, r_scr, i_scr, m_scr, cl_scr, cdc_scr, cdx_scr):
        pid = pl.program_id(0)
        last = pid == 0
        first = pid == nt - 1
        rows = lax.broadcasted_iota(jnp.int32, (ts, LANES), 0)

        @pl.when(last)
        def _():
            sg_ref[...] = jnp.zeros_like(sg_ref)
            cl_scr[...] = jnp.zeros_like(cl_scr)
            cdc_scr[...] = jnp.zeros_like(cdc_scr)
            cdx_scr[...] = jnp.zeros_like(cdx_scr)

        def cur(comp, c):
            return p_ref[:, comp * w + c * LANES:comp * w + (c + 1) * LANES]

        def prev(comp, c):
            v = pp_ref[:, comp * w + c * LANES:comp * w + (c + 1) * LANES]
            return jnp.where(first, 0.0, v)

        def put(comp, c, v):
            dp_ref[:, comp * w + c * LANES:comp * w + (c + 1) * LANES] = v

        def acc(row, sl, v):
            sg_ref[row:row + 1, sl] += _colsum(v)

        for c in range(nch):
            sl = slice(c * LANES, (c + 1) * LANES)
            bg, cg, ax = cur(0, c), cur(1, c), cur(2, c)
            cx = cg * ax
            cxp = prev(1, c) * prev(2, c)
            cx1 = _shift_down(cx, cxp, 1, rows)
            cx2 = _shift_down(cx, cxp, 2, rows)
            wa3 = ca_ref[:, sl]
            conv = wa3[2:3] * cx + wa3[1:2] * cx1 + wa3[0:1] * cx2
            dya = dy_ref[:, sl]
            put(0, c, dya * conv)
            dconv = dya * bg
            nxt = cdc_scr[:, sl]
            dcx = (wa3[2:3] * dconv + wa3[1:2] * _shift_up(dconv, nxt, 1, rows)
                   + wa3[0:1] * _shift_up(dconv, nxt, 2, rows))
            cdc_scr[:, sl] = dconv[0:SUBLANES]
            put(1, c, dcx * ax)
            put(2, c, dcx * cg)
            acc(_SG_CONV_A + 2, sl, dconv * cx)
            acc(_SG_CONV_A + 1, sl, dconv * cx1)
            acc(_SG_CONV_A + 0, sl, dconv * cx2)

        for c in range(nch):
            sl = slice(c * LANES, (c + 1) * LANES)
            xb, xbp = cur(4, c), prev(4, c)
            wb4 = cb_ref[:, sl]
            xr = (wb4[3:4] * xb + wb4[2:3] * _shift_down(xb, xbp, 1, rows)
                  + wb4[1:2] * _shift_down(xb, xbp, 2, rows)
                  + wb4[0:1] * _shift_down(xb, xbp, 3, rows) + bias_ref[:, sl])
            r, i, a, m = _gates(xr, wa_ref[c], ba_ref[:, sl], wx_ref[c], bx_ref[:, sl], lam_ref[:, sl])
            gel, dgel = _gelu_and_grad(cur(3, c))
            dyb = dy_ref[:, w + c * LANES:w + (c + 1) * LANES]
            put(3, c, dyb * h_ref[:, sl] * dgel)
            g_scr[:, sl] = dyb * gel
            a_scr[:, sl] = a
            x_scr[:, sl] = xr
            r_scr[:, sl] = r
            i_scr[:, sl] = i
            m_scr[:, sl] = m

        def step(j, carry):
            r = ts - 1 - j
            lam_t = g_scr[pl.ds(r, 1), :] + carry
            l_scr[pl.ds(r, 1), :] = lam_t
            return a_scr[pl.ds(r, 1), :] * lam_t

        cl_scr[0:1, :] = lax.fori_loop(0, ts, step, cl_scr[0:1, :], unroll=8)

        for c in range(nch):
            sl = slice(c * LANES, (c + 1) * LANES)
            lam_t = l_scr[:, sl]
            hprev = _shift_down(h_ref[:, sl], jnp.where(first, 0.0, hp_ref[:, sl]), 1, rows)
            xr, r, i, m, a = x_scr[:, sl], r_scr[:, sl], i_scr[:, sl], m_scr[:, sl], a_scr[:, sl]
            da = lam_t * hprev
            dm = lam_t * i * xr
            di = lam_t * m * xr
            dxr = lam_t * m * i
            dlog_a = da * a - dm * a * a / m
            lam_p = lam_ref[:, sl]
            dr = dlog_a * (LRU_C * _log_sigmoid(lam_p))
            acc(_SG_LAM, sl, dlog_a * r * (LRU_C * _sigmoid(-lam_p)))
            dpa = dr * r * (1.0 - r)
            dpx = di * i * (1.0 - i)
            dpa_b, dpx_b = dpa.astype(BF16), dpx.astype(BF16)
            dxr = (dxr + lax.dot_general(dpa_b, wa_ref[c], _DIMS["nt"], preferred_element_type=F32)
                   + lax.dot_general(dpx_b, wx_ref[c], _DIMS["nt"], preferred_element_type=F32))
            xr_ref[:, sl] = xr.astype(BF16)
            dpa_ref[:, sl] = dpa_b
            dpx_ref[:, sl] = dpx_b
            acc(_SG_BA, sl, dpa)
            acc(_SG_BX, sl, dpx)
            acc(_SG_BIAS, sl, dxr)
            nxt = cdx_scr[:, sl]
            wb4 = cb_ref[:, sl]
            put(4, c, wb4[3:4] * dxr + wb4[2:3] * _shift_up(dxr, nxt, 1, rows)
                + wb4[1:2] * _shift_up(dxr, nxt, 2, rows) + wb4[0:1] * _shift_up(dxr, nxt, 3, rows))
            cdx_scr[:, sl] = dxr[0:SUBLANES]
            xb, xbp = cur(4, c), prev(4, c)
            acc(_SG_CONV_B + 3, sl, dxr * xb)
            acc(_SG_CONV_B + 2, sl, dxr * _shift_down(xb, xbp, 1, rows))
            acc(_SG_CONV_B + 1, sl, dxr * _shift_down(xb, xbp, 2, rows))
            acc(_SG_CONV_B + 0, sl, dxr * _shift_down(xb, xbp, 3, rows))

    blk = lambda width: pl.BlockSpec((ts, width), lambda p: (nt - 1 - p, 0))
    pre = lambda width: pl.BlockSpec(
        (SUBLANES, width), lambda p: (jnp.maximum((nt - 1 - p) * tpb - 1, 0), 0))
    vec = lambda n: _whole((n, w))
    big = lambda: pltpu.VMEM((ts, w), F32)
    small = lambda: pltpu.VMEM((SUBLANES, w), F32)
    return pl.pallas_call(
        body, name="mixer_bwd", grid=(nt,),
        in_specs=[blk(w5), pre(w5), blk(w), pre(w), blk(2 * w),
                  vec(3), vec(4), vec(1), _whole(wa_blk.shape), vec(1), _whole(wx_blk.shape), vec(1), vec(1)],
        out_specs=[blk(w5), blk(w), blk(w), blk(w), _whole((_SG_ROWS, w))],
        out_shape=[jax.ShapeDtypeStruct((s, w5), F32), jax.ShapeDtypeStruct((s, w), BF16),
                   jax.ShapeDtypeStruct((s, w), BF16), jax.ShapeDtypeStruct((s, w), BF16),
                   jax.ShapeDtypeStruct((_SG_ROWS, w), F32)],
        scratch_shapes=[big(), big(), big(), big(), big(), big(), big(), small(), small(), small()],
        compiler_params=_cp(("arbitrary",)),
    )(proj, proj, hseq, hseq, dy, conv_a, conv_b, bias, wa_blk, ba, wx_blk, bx, lam)


def _split_dot(v, tri):
    hi = v.astype(BF16)
    lo = (v - hi.astype(F32)).astype(BF16)
    return (jnp.dot(hi, tri, preferred_element_type=F32) + jnp.dot(lo, tri, preferred_element_type=F32))


def _tri(cmp):
    r = lax.broadcasted_iota(jnp.int32, (LANES, LANES), 0)
    c = lax.broadcasted_iota(jnp.int32, (LANES, LANES), 1)
    return cmp(r, c).astype(BF16)


def _lane_blocks(v):
    return [v[:, b * LANES:(b + 1) * LANES] for b in range(v.shape[1] // LANES)]


def _last_lane(v):
    return jnp.broadcast_to(v[:, LANES - 1:LANES], v.shape)


def _log_gates(q, kb, scale, diagonal):
    z = lax.dot_general(q, kb, _DIMS["nt"], preferred_element_type=F32) * scale
    ls = jnp.minimum(z, 0.0) - jnp.log(1.0 + jnp.exp(-jnp.abs(z)))
    ln = ls - z
    valid = None
    if diagonal:
        valid = (lax.broadcasted_iota(jnp.int32, z.shape, 1) < lax.broadcasted_iota(jnp.int32, z.shape, 0))
        ln = jnp.where(valid, ln, 0.0)
    return ls, ln, valid


def _attn_fwd(qkv, heads):
    s = qkv.shape[0]
    dh = LANES
    tq = _pick(s, ATT_TILE)
    nq = s // tq
    nb = tq // LANES
    scale = 1.0 / math.sqrt(dh)

    def body(q_ref, k_ref, v_ref, o_ref, tot_ref, acc_scr, car_scr):
        qi = pl.program_id(1)
        q = q_ref[...]
        acc_scr[...] = jnp.zeros_like(acc_scr)
        car_scr[...] = jnp.zeros_like(car_scr)
        tri = _tri(lambda r, c: r > c)

        def tile(kt, diagonal):
            k0 = pl.multiple_of(kt * tq, tq)
            kb = k_ref[pl.ds(k0, tq), :]
            vb = v_ref[pl.ds(k0, tq), :]
            ls, ln, valid = _log_gates(q, kb, scale, diagonal)
            blocks = _lane_blocks(ln)
            sfx = _split_dot(jnp.concatenate(blocks, axis=0), tri)
            car = car_scr[...]
            parts = [None] * nb
            for b in reversed(range(nb)):
                sb = sfx[b * tq:(b + 1) * tq]
                parts[b] = sb + car
                car = car + (sb[:, 0:1] + blocks[b][:, 0:1])
            car_scr[...] = car
            wgt = jnp.exp(ls + jnp.concatenate(parts, axis=1))
            if diagonal:
                wgt = jnp.where(valid, wgt, 0.0)
            acc_scr[...] += jnp.dot(wgt.astype(BF16), vb, preferred_element_type=F32)

        tile(qi, True)

        def step(j, carry):
            tile(qi - 1 - j, False)
            return carry

        lax.fori_loop(0, qi, step, 0)
        o_ref[...] = acc_scr[...].astype(BF16)
        tot_ref[...] = car_scr[...]

    return pl.pallas_call(
        body, name="attn_fwd", grid=(heads, nq),
        in_specs=[pl.BlockSpec((tq, dh), lambda h, i: (i, h)),
                  pl.BlockSpec((s, dh), lambda h, i: (0, heads + h)),
                  pl.BlockSpec((s, dh), lambda h, i: (0, 2 * heads + h))],
        out_specs=[pl.BlockSpec((tq, dh), lambda h, i: (i, h)), pl.BlockSpec((tq, dh), lambda h, i: (i, h))],
        out_shape=[jax.ShapeDtypeStruct((s, heads * dh), BF16), jax.ShapeDtypeStruct((s, heads * dh), F32)],
        scratch_shapes=[pltpu.VMEM((tq, dh), F32), pltpu.VMEM((tq, dh), F32)],
        compiler_params=_cp(("parallel", "arbitrary")),
    )(qkv, qkv, qkv)


def _attn_bwd(qkv, tot, do, heads):
    s = qkv.shape[0]
    dh = LANES
    tq = _pick(s, ATT_TILE)
    nq = s // tq
    nb = tq // LANES
    scale = 1.0 / math.sqrt(dh)

    def body(q_ref, k_ref, v_ref, tot_ref, do_ref, dq_ref, dk_ref, dv_ref,
             dq_scr, dk_scr, dv_scr, cl_scr, cg_scr):
        qi = pl.program_id(1)

        @pl.when(qi == 0)
        def _():
            dk_scr[...] = jnp.zeros_like(dk_scr)
            dv_scr[...] = jnp.zeros_like(dv_scr)

        q = q_ref[...]
        dob = do_ref[...]
        dq_scr[...] = jnp.zeros_like(dq_scr)
        cl_scr[...] = jnp.zeros_like(cl_scr)
        cg_scr[...] = jnp.zeros_like(cg_scr)
        tri_le = _tri(lambda r, c: r <= c)
        tri_lt = _tri(lambda r, c: r < c)

        def tile(kt, diagonal):
            k0 = pl.multiple_of(kt * tq, tq)
            kb = k_ref[pl.ds(k0, tq), :]
            vb = v_ref[pl.ds(k0, tq), :]
            ls, ln, valid = _log_gates(q, kb, scale, diagonal)
            pin = _split_dot(jnp.concatenate(_lane_blocks(ln), axis=0), tri_le)
            total = tot_ref[...]
            cl = cl_scr[...]
            parts = []
            for b in range(nb):
                pb = pin[b * tq:(b + 1) * tq] + cl
                parts.append(total - pb)
                cl = _last_lane(pb)
            cl_scr[...] = cl
            wgt = jnp.exp(ls + jnp.concatenate(parts, axis=1))
            if diagonal:
                wgt = jnp.where(valid, wgt, 0.0)
            g = wgt * lax.dot_general(dob, vb, _DIMS["nt"], preferred_element_type=F32)
            gblocks = _lane_blocks(g)
            pex = jnp.dot(jnp.concatenate(gblocks, axis=0).astype(BF16), tri_lt, preferred_element_type=F32)
            cg = cg_scr[...]
            parts = []
            for b in range(nb):
                pb = pex[b * tq:(b + 1) * tq] + cg
                parts.append(pb)
                cg = _last_lane(pb + gblocks[b])
            cg_scr[...] = cg
            dz = g - jnp.exp(ls) * (g + jnp.concatenate(parts, axis=1))
            if diagonal:
                dz = jnp.where(valid, dz, 0.0)
            dz = dz.astype(BF16)
            dq_scr[...] += jnp.dot(dz, kb, preferred_element_type=F32)
            dk_scr[pl.ds(k0, tq), :] += lax.dot_general(dz, q, _DIMS["tn"], preferred_element_type=F32)
            dv_scr[pl.ds(k0, tq), :] += lax.dot_general(
                wgt.astype(BF16), dob, _DIMS["tn"], preferred_element_type=F32)

        def step(j, carry):
            tile(j, False)
            return carry

        lax.fori_loop(0, qi, step, 0)
        tile(qi, True)
        dq_ref[...] = (dq_scr[...] * scale).astype(BF16)

        @pl.when(qi == nq - 1)
        def _():
            dk_ref[...] = (dk_scr[...] * scale).astype(BF16)
            dv_ref[...] = dv_scr[...].astype(BF16)

    qblk = pl.BlockSpec((tq, dh), lambda h, i: (i, h))
    hblk = pl.BlockSpec((s, dh), lambda h, i: (0, h))
    out = jax.ShapeDtypeStruct((s, heads * dh), BF16)
    return pl.pallas_call(
        body, name="attn_bwd", grid=(heads, nq),
        in_specs=[qblk, pl.BlockSpec((s, dh), lambda h, i: (0, heads + h)),
                  pl.BlockSpec((s, dh), lambda h, i: (0, 2 * heads + h)), qblk, qblk],
        out_specs=[qblk, hblk, hblk], out_shape=[out, out, out],
        scratch_shapes=[pltpu.VMEM((tq, dh), F32), pltpu.VMEM((s, dh), F32), pltpu.VMEM((s, dh), F32),
                        pltpu.VMEM((tq, dh), F32), pltpu.VMEM((tq, dh), F32)],
        compiler_params=_cp(("parallel", "arbitrary")),
    )(qkv, qkv, qkv, tot, do)


def _place():
    x, y, c = lax.axis_index("x"), lax.axis_index("y"), lax.axis_index("c")
    chips = [(1 - x, y), (x, 1 - y), (1 - x, 1 - y)]
    return x, y, c, chips


def _hbm_specs(n):
    return [pl.BlockSpec(memory_space=pl.ANY) for _ in range(n)]


def _remote(src, dst, send_sem, recv_sem, dev):
    return pltpu.make_async_remote_copy(
        src_ref=src, dst_ref=dst, send_sem=send_sem, recv_sem=recv_sem, device_id=dev, device_id_type=MESH)


def _allgather_weights(name, fulls):
    n = len(fulls)

    def body(*refs):
        bufs = refs[n:2 * n]
        send_sems, recv_sems, fsend_sems, frecv_sems = refs[2 * n:]
        x, y, c, chips = _place()
        me = 2 * x + y
        sibling = (x, y, 1 - c)
        firsts = []
        for w in range(n):
            hr = bufs[w].shape[1] // 2
            mine = bufs[w].at[me, pl.ds(c * hr, hr)]
            for k, (px, py) in enumerate(chips):
                cp = _remote(mine, mine, send_sems.at[w, k], recv_sems.at[w, k], (px, py, c))
                cp.start()
                firsts.append(cp)
        passed = []
        for w in range(n):
            hr = bufs[w].shape[1] // 2
            for k, (px, py) in enumerate(chips):
                slot = bufs[w].at[2 * px + py, pl.ds(c * hr, hr)]
                _remote(slot, slot, send_sems.at[w, k], recv_sems.at[w, k], (px, py, c)).wait_recv()
                cp = _remote(slot, slot, fsend_sems.at[w, k], frecv_sems.at[w, k], sibling)
                cp.start()
                passed.append(cp)
        for w in range(n):
            hr = bufs[w].shape[1] // 2
            for k, (px, py) in enumerate(chips):
                slot = bufs[w].at[2 * px + py, pl.ds((1 - c) * hr, hr)]
                _remote(slot, slot, fsend_sems.at[w, k], frecv_sems.at[w, k], sibling).wait_recv()
        for cp in firsts + passed:
            cp.wait_send()

    sem = lambda: pltpu.SemaphoreType.DMA((n, 3))
    return pl.pallas_call(
        body, name=name, in_specs=_hbm_specs(n), out_specs=_hbm_specs(n),
        out_shape=[jax.ShapeDtypeStruct(f.shape, f.dtype) for f in fulls],
        input_output_aliases={w: w for w in range(n)},
        scratch_shapes=[sem(), sem(), sem(), sem()],
    )(*fulls)


def _exchange_sibling_halves(name, slabs):
    n = len(slabs)

    def body(*refs):
        ins, outs = refs[:n], refs[n:2 * n]
        send_sems, recv_sems = refs[2 * n:]
        x, y, c, _ = _place()
        cps = []
        for w in range(n):
            hr = ins[w].shape[1] // 2
            cp = _remote(ins[w].at[:, pl.ds((1 - c) * hr, hr), :], outs[w], send_sems.at[w], recv_sems.at[w],
                         (x, y, 1 - c))
            cp.start()
            cps.append(cp)
        for cp in cps:
            cp.wait()

    return pl.pallas_call(
        body, name=name, in_specs=_hbm_specs(n), out_specs=_hbm_specs(n),
        out_shape=[jax.ShapeDtypeStruct((N_CHIPS, s.shape[1] // 2, s.shape[2]), s.dtype) for s in slabs],
        scratch_shapes=[pltpu.SemaphoreType.DMA((n,)), pltpu.SemaphoreType.DMA((n,))],
    )(*slabs)


def _exchange_chips(name, parts):
    n = len(parts)

    def body(*refs):
        ins, outs = refs[:n], refs[n:2 * n]
        send_sems, recv_sems = refs[2 * n:]
        x, y, c, chips = _place()
        cps = []
        for w in range(n):
            for k, (px, py) in enumerate(chips):
                cp = _remote(ins[w].at[2 * px + py], outs[w].at[k], send_sems.at[w, k], recv_sems.at[w, k],
                             (px, py, c))
                cp.start()
                cps.append(cp)
        for cp in cps:
            cp.wait()

    return pl.pallas_call(
        body, name=name, in_specs=_hbm_specs(n), out_specs=_hbm_specs(n),
        out_shape=[jax.ShapeDtypeStruct((3,) + s.shape[1:], s.dtype) for s in parts],
        scratch_shapes=[pltpu.SemaphoreType.DMA((n, 3)), pltpu.SemaphoreType.DMA((n, 3))],
    )(*parts)


def _join_sibling_halves(name, bufs):
    n = len(bufs)

    def body(*refs):
        outs = refs[n:2 * n]
        send_sems, recv_sems = refs[2 * n:]
        x, y, c, _ = _place()
        cps = []
        for w in range(n):
            hr = outs[w].shape[0] // 2
            mine = outs[w].at[pl.ds(c * hr, hr)]
            cp = _remote(mine, mine, send_sems.at[w], recv_sems.at[w], (x, y, 1 - c))
            cp.start()
            cps.append(cp)
        for w in range(n):
            hr = outs[w].shape[0] // 2
            other = outs[w].at[pl.ds((1 - c) * hr, hr)]
            _remote(other, other, send_sems.at[w], recv_sems.at[w], (x, y, 1 - c)).wait_recv()
        for cp in cps:
            cp.wait_send()

    return pl.pallas_call(
        body, name=name, in_specs=_hbm_specs(n), out_specs=_hbm_specs(n),
        out_shape=[jax.ShapeDtypeStruct(b.shape, b.dtype) for b in bufs],
        input_output_aliases={w: w for w in range(n)},
        scratch_shapes=[pltpu.SemaphoreType.DMA((n,)), pltpu.SemaphoreType.DMA((n,))],
    )(*bufs)


def _allgather_chips_small(name, v):
    r = v.shape[0]

    def body(v_ref, o_ref, send_sems, recv_sems):
        x, y, c, chips = _place()
        me = 2 * x + y
        o_ref[me] = v_ref[...]
        cps = []
        for k, (px, py) in enumerate(chips):
            cp = _remote(v_ref, o_ref.at[me], send_sems.at[k], recv_sems.at[k], (px, py, c))
            cp.start()
            cps.append(cp)
        for k, (px, py) in enumerate(chips):
            slot = o_ref.at[2 * px + py]
            _remote(slot, slot, send_sems.at[k], recv_sems.at[k], (px, py, c)).wait_recv()
        for cp in cps:
            cp.wait_send()

    return pl.pallas_call(
        body, name=name, in_specs=[pl.BlockSpec(memory_space=pltpu.VMEM)],
        out_specs=pl.BlockSpec(memory_space=pltpu.VMEM),
        out_shape=jax.ShapeDtypeStruct((N_CHIPS, r, LANES), F32),
        scratch_shapes=[pltpu.SemaphoreType.DMA((3,)), pltpu.SemaphoreType.DMA((3,))],
    )(v)


def _allreduce_small(name, v):
    r = v.shape[0]

    def body(v_ref, o_ref, all_ref, send_sems, recv_sems):
        x, y, c, _ = _place()
        me = 4 * x + 2 * y + c
        all_ref[me] = v_ref[...]
        peers = [(1 - x if k & 4 else x, 1 - y if k & 2 else y, 1 - c if k & 1 else c)
                 for k in range(1, N_DEV)]
        cps = []
        for k, dev in enumerate(peers):
            cp = _remote(v_ref, all_ref.at[me], send_sems.at[k], recv_sems.at[k], dev)
            cp.start()
            cps.append(cp)
        for k, (px, py, pc) in enumerate(peers):
            slot = all_ref.at[4 * px + 2 * py + pc]
            _remote(slot, slot, send_sems.at[k], recv_sems.at[k], (px, py, pc)).wait_recv()
        for cp in cps:
            cp.wait_send()
        total = all_ref[0]
        for d in range(1, N_DEV):
            total = total + all_ref[d]
        o_ref[...] = total

    return pl.pallas_call(
        body, name=name, in_specs=[pl.BlockSpec(memory_space=pltpu.VMEM)],
        out_specs=pl.BlockSpec(memory_space=pltpu.VMEM),
        out_shape=jax.ShapeDtypeStruct((r, LANES), F32),
        scratch_shapes=[pltpu.VMEM((N_DEV, r, LANES), F32), pltpu.SemaphoreType.DMA((N_DEV - 1,)),
                        pltpu.SemaphoreType.DMA((N_DEV - 1,))],
    )(v)


def _add_sibling(name, slabs, recv, c):
    _, r, cols = slabs.shape
    hr = r // 2
    tr = _pick(hr, ROW_TILE)
    nb = hr // tr

    def body(c_ref, a_ref, b_ref, o_ref):
        o_ref[...] = (a_ref[...].astype(F32) + b_ref[...].astype(F32)).astype(BF16)

    grid_spec = pltpu.PrefetchScalarGridSpec(
        num_scalar_prefetch=1, grid=(N_CHIPS, nb),
        in_specs=[pl.BlockSpec((None, tr, cols), lambda j, i, c_ref: (j, c_ref[0] * nb + i, 0)),
                  pl.BlockSpec((None, tr, cols), lambda j, i, c_ref: (j, i, 0))],
        out_specs=pl.BlockSpec((None, tr, cols), lambda j, i, c_ref: (j, i, 0)))
    return pl.pallas_call(
        body, name=name, grid_spec=grid_spec,
        out_shape=jax.ShapeDtypeStruct((N_CHIPS, hr, cols), BF16),
        compiler_params=_cp(("parallel", "parallel")))(jnp.reshape(c, (1,)).astype(jnp.int32), slabs, recv)


def _sum_chips(name, own, recv, chip, c):
    _, hr, cols = recv.shape
    tr = _pick(hr, ROW_TILE)
    nb = hr // tr

    def body(sc_ref, own_ref, recv_ref, o_ref):
        total = own_ref[...].astype(F32)
        for k in range(3):
            total = total + recv_ref[k].astype(F32)
        o_ref[...] = total

    grid_spec = pltpu.PrefetchScalarGridSpec(
        num_scalar_prefetch=1, grid=(nb,),
        in_specs=[pl.BlockSpec((None, tr, cols), lambda i, sc: (sc[0], i, 0)),
                  pl.BlockSpec((3, tr, cols), lambda i, sc: (0, i, 0))],
        out_specs=pl.BlockSpec((tr, cols), lambda i, sc: (sc[1] * nb + i, 0)))
    return pl.pallas_call(
        body, name=name, grid_spec=grid_spec, out_shape=jax.ShapeDtypeStruct((2 * hr, cols), F32),
        compiler_params=_cp(("parallel",)))(jnp.stack([chip, c]).astype(jnp.int32), own, recv)


def _adamw_math(w, g, m, v):
    m = ADAM_B1 * m + (1.0 - ADAM_B1) * g
    v = ADAM_B2 * v + (1.0 - ADAM_B2) * (g * g)
    m_hat = m / (1.0 - ADAM_B1 ** ADAM_STEP)
    v_hat = v / (1.0 - ADAM_B2 ** ADAM_STEP)
    delta = -ADAM_LR * (m_hat / (jnp.sqrt(v_hat) + ADAM_EPS) + ADAM_WD * w)
    return delta, m, v


def _adamw(name, w, gs, m, v):
    nl, r, cols = w.shape
    tr = _pick(r, LANES)

    def body(*refs):
        w_ref, m_ref, v_ref = refs[0:3]
        g_refs = refs[3:3 + nl]
        go_ref, d_ref, nm_ref, nv_ref = refs[3 + nl:]
        layer = pl.program_id(0)
        g = g_refs[0][...]
        for j in range(1, nl):
            g = jnp.where(layer == j, g_refs[j][...], g)
        d, nm, nv = _adamw_math(w_ref[...], g, m_ref[...], v_ref[...])
        go_ref[...] = g
        d_ref[...] = d
        nm_ref[...] = nm
        nv_ref[...] = nv

    spec3 = pl.BlockSpec((None, tr, cols), lambda l, i: (l, i, 0))
    gspec = pl.BlockSpec((tr, cols), lambda l, i: (i, 0))
    out = jax.ShapeDtypeStruct((nl, r, cols), F32)
    return pl.pallas_call(
        body, name=name, grid=(nl, r // tr), in_specs=[spec3] * 3 + [gspec] * nl, out_specs=[spec3] * 4,
        out_shape=[out] * 4, compiler_params=_cp(("parallel", "parallel")))(w, m, v, *gs)


def _adamw_small(name, groups):
    n = len(groups)
    flat = [a for grp in groups for a in grp]

    def body(*refs):
        ins, outs = refs[:4 * n], refs[4 * n:]
        for p in range(n):
            w_ref, g_ref, m_ref, v_ref = ins[4 * p:4 * p + 4]
            d, nm, nv = _adamw_math(w_ref[...], g_ref[...], m_ref[...], v_ref[...])
            outs[3 * p][...] = d
            outs[3 * p + 1][...] = nm
            outs[3 * p + 2][...] = nv

    vm = pl.BlockSpec(memory_space=pltpu.VMEM)
    out_shape = [jax.ShapeDtypeStruct(grp[0].shape, F32) for grp in groups for _ in range(3)]
    res = pl.pallas_call(
        body, name=name, in_specs=[vm] * (4 * n), out_specs=[vm] * (3 * n), out_shape=out_shape)(*flat)
    return [tuple(res[3 * p:3 * p + 3]) for p in range(n)]


def _block_diag_pairs(w):
    h, d, _ = w.shape
    z = jnp.zeros((h // 2, d, d), w.dtype)
    top = jnp.concatenate([w[0::2], z], axis=2)
    bot = jnp.concatenate([z, w[1::2]], axis=2)
    return jnp.concatenate([top, bot], axis=1).astype(BF16)


def _diag_pairs_to_heads(g, d):
    a = g[:, :d, :d]
    b = g[:, d:, d:]
    return jnp.stack([a, b], axis=1).reshape(-1, d, d)


def _rows128(a):
    flat = a.reshape(-1, LANES)
    pad = (-flat.shape[0]) % SUBLANES
    if pad:
        flat = jnp.concatenate([flat, jnp.zeros((pad, LANES), flat.dtype)], axis=0)
    return flat


def _unshard_last(g4, shape):
    g4 = g4.reshape((N_CHIPS,) + tuple(shape))
    return jnp.concatenate([g4[j] for j in range(N_CHIPS)], axis=-1)


def kernel(x, norm_gains, hyb_w_in, hyb_conv_a, hyb_conv_b, hyb_conv_b_bias, hyb_rg_w_a, hyb_rg_b_a, hyb_rg_w_x, hyb_rg_b_x, hyb_rg_lambda, hyb_w_out, sb_w_qkv, sb_w_o, mlp_w_up, mlp_w_down, loss_target, m_norm_gains, m_hyb_w_in, m_hyb_conv_a, m_hyb_conv_b, m_hyb_conv_b_bias, m_hyb_rg_w_a, m_hyb_rg_b_a, m_hyb_rg_w_x, m_hyb_rg_b_x, m_hyb_rg_lambda, m_hyb_w_out, m_sb_w_qkv, m_sb_w_o, m_mlp_w_up, m_mlp_w_down, v_norm_gains, v_hyb_w_in, v_hyb_conv_a, v_hyb_conv_b, v_hyb_conv_b_bias, v_hyb_rg_w_a, v_hyb_rg_b_a, v_hyb_rg_w_x, v_hyb_rg_b_x, v_hyb_rg_lambda, v_hyb_w_out, v_sb_w_qkv, v_sb_w_o, v_mlp_w_up, v_mlp_w_down):
    cx_ = lax.axis_index("x")
    cy_ = lax.axis_index("y")
    cc_ = lax.axis_index("c")
    chip = 2 * cx_ + cy_

    x0 = x[0]
    target = loss_target[0]
    s, d = x0.shape
    heads = SB_HEADS
    assert d // heads == LANES
    n_rg, hd = hyb_rg_w_a.shape[1], hyb_rg_w_a.shape[2]
    wmix = n_rg * hd
    assert 2 * hd == LANES

    big = {
        "hyb_w_in": hyb_w_in[0], "hyb_w_out": hyb_w_out[0], "sb_w_qkv": sb_w_qkv[0], "sb_w_o": sb_w_o[0],
        "mlp_w_up0": mlp_w_up[0], "mlp_w_down0": mlp_w_down[0],
        "mlp_w_up1": mlp_w_up[1], "mlp_w_down1": mlp_w_down[1],
    }
    names = list(big)
    slots = [_cast_into_slot("cast_" + k, big[k], chip) for k in names]
    full = dict(zip(names, _allgather_weights("allgather_weights", slots)))
    rowsharded = lambda k: full[k].reshape(-1, full[k].shape[2])

    ng_s, ca_s, cb_s = norm_gains.reshape(-1, norm_gains.shape[2]), hyb_conv_a[0], hyb_conv_b[0]
    packed = jnp.concatenate([_rows128(ng_s), _rows128(ca_s), _rows128(cb_s)], axis=0)
    gathered = _allgather_chips_small("allgather_small", packed)
    n0 = ng_s.size // LANES
    n1 = n0 + (-n0) % SUBLANES
    m0 = ca_s.size // LANES
    m1 = m0 + (-m0) % SUBLANES
    k0 = cb_s.size // LANES
    gains = _unshard_last(gathered[:, 0:n0], ng_s.shape).reshape(2, 4, 1, d)
    conv_a = _unshard_last(gathered[:, n1:n1 + m0], ca_s.shape)
    conv_b = _unshard_last(gathered[:, n1 + m1:n1 + m1 + k0], cb_s.shape)
    bias, b_a, b_x, lam = hyb_conv_b_bias, hyb_rg_b_a, hyb_rg_b_x, hyb_rg_lambda
    wa_blk = _block_diag_pairs(hyb_rg_w_a[0])
    wx_blk = _block_diag_pairs(hyb_rg_w_x[0])

    relu_sq = lambda acc: (jnp.maximum(acc, 0.0), jnp.square(jnp.maximum(acc, 0.0)))

    h1 = _rms_fwd("rms_pre0", x0, gains[0, 0])
    proj = _mm_fwd_col("proj_in", h1, full["hyb_w_in"])[0]
    ycat, hseq = _mixer_fwd(proj, conv_a, conv_b, bias, wa_blk, b_a, wx_blk, b_x, lam)
    mix0 = _mm_fwd_row("proj_out", ycat, rowsharded("hyb_w_out"))
    x1, h2 = _rms_post("rms_mix0", mix0, gains[0, 1], x0, gains[0, 2])
    u0, a0 = _mm_fwd_col("mlp_up0", h2, full["mlp_w_up0"], (BF16, BF16), relu_sq)
    mlp0 = _mm_fwd_row("mlp_down0", a0, rowsharded("mlp_w_down0"))
    x2, h3 = _rms_post("rms_mlp0", mlp0, gains[0, 3], x1, gains[1, 0])

    qkv = _mm_fwd_col("qkv", h3, full["sb_w_qkv"], (BF16,))[0]
    att, tot = _attn_fwd(qkv, heads)
    mix1 = _mm_fwd_row("attn_out", att, rowsharded("sb_w_o"))
    x3, h4 = _rms_post("rms_mix1", mix1, gains[1, 1], x2, gains[1, 2])
    u1, a1 = _mm_fwd_col("mlp_up1", h4, full["mlp_w_up1"], (BF16, BF16), relu_sq)
    mlp1 = _mm_fwd_row("mlp_down1", a1, rowsharded("mlp_w_down1"))
    (x4,) = _rms_post("rms_mlp1", mlp1, gains[1, 3], x3)

    dy, loss_local = _loss_head("loss_head", x4, target)
    loss = lax.psum(loss_local, ("x", "y", "c"))

    grads_big = {}
    dgain = [[None] * 4 for _ in range(2)]
    drelu = lambda acc, u: (acc * (2.0 * u.astype(F32)),)

    def mlp_bwd(layer, dxo, mlp_out, xin, hin, u, a):
        dmlp, dgain[layer][3] = _rms_bwd(f"rms_mlp{layer}_bwd", mlp_out, gains[layer, 3], dxo, out_dtype=BF16)
        wd, wu = rowsharded(f"mlp_w_down{layer}"), full[f"mlp_w_up{layer}"]
        grads_big[f"mlp_w_down{layer}"] = _mm_wgrad_row(f"mlp_down{layer}_wgrad", a, dmlp).reshape(
            N_CHIPS, -1, d)
        du = _mm_bwd_row(f"mlp_down{layer}_bwd", dmlp, wd, (BF16,), u, drelu)[0]
        grads_big[f"mlp_w_up{layer}"] = _mm_wgrad_col(f"mlp_up{layer}_wgrad", hin, du, wu.shape[2])
        dh = _mm_bwd_col(f"mlp_up{layer}_bwd", du, wu)
        dxm, dgain[layer][2] = _rms_bwd(f"rms_premlp{layer}_bwd", xin, gains[layer, 2], dh, res=dxo)
        return dxm

    dx3 = mlp_bwd(1, dy, mlp1, x3, h4, u1, a1)
    dmix1, dgain[1][1] = _rms_bwd("rms_mix1_bwd", mix1, gains[1, 1], dx3, out_dtype=BF16)
    grads_big["sb_w_o"] = _mm_wgrad_row("attn_out_wgrad", att, dmix1).reshape(N_CHIPS, -1, d)
    datt = _mm_bwd_row("attn_out_bwd", dmix1, rowsharded("sb_w_o"), (BF16,))[0]
    dq, dk, dv = _attn_bwd(qkv, tot, datt, heads)
    dqkv = jnp.concatenate([dq, dk, dv], axis=1)
    grads_big["sb_w_qkv"] = _mm_wgrad_col("qkv_wgrad", h3, dqkv, full["sb_w_qkv"].shape[2])
    dh3 = _mm_bwd_col("qkv_bwd", dqkv, full["sb_w_qkv"])
    dx2, dgain[1][0] = _rms_bwd("rms_pre1_bwd", x2, gains[1, 0], dh3, res=dx3)

    dx1 = mlp_bwd(0, dx2, mlp0, x1, h2, u0, a0)
    dmix0, dgain[0][1] = _rms_bwd("rms_mix0_bwd", mix0, gains[0, 1], dx1, out_dtype=BF16)
    grads_big["hyb_w_out"] = _mm_wgrad_row("proj_out_wgrad", ycat, dmix0).reshape(N_CHIPS, -1, d)
    dycat = _mm_bwd_row("proj_out_bwd", dmix0, rowsharded("hyb_w_out"))[0]
    dproj, xr_b, dpa_b, dpx_b, sg = _mixer_bwd(
        proj, hseq, dycat, conv_a, conv_b, bias, wa_blk, b_a, wx_blk, b_x, lam)
    grads_big["hyb_w_in"] = _mm_wgrad_col("proj_in_wgrad", h1, dproj, full["hyb_w_in"].shape[2])
    dh1 = _mm_bwd_col("proj_in_bwd", dproj, full["hyb_w_in"])
    dx0, dgain[0][0] = _rms_bwd("rms_pre0_bwd", x0, gains[0, 0], dh1, res=dx1)
    dwa = _diag_pairs_to_heads(_mm_wgrad_diag("rg_w_a_wgrad", xr_b, dpa_b), hd)
    dwx = _diag_pairs_to_heads(_mm_wgrad_diag("rg_w_x_wgrad", xr_b, dpx_b), hd)

    dgains = jnp.concatenate([dgain[l][k] for l in range(2) for k in range(4)], axis=0)
    small_parts = [dgains, sg[_SG_CONV_A:_SG_CONV_A + 3], sg[_SG_CONV_B:_SG_CONV_B + 4], sg[_SG_BIAS:_SG_BIAS + 1],
                   dwa, sg[_SG_BA:_SG_BA + 1], dwx, sg[_SG_BX:_SG_BX + 1], sg[_SG_LAM:_SG_LAM + 1]]
    small_rows = [_rows128(p) for p in small_parts]
    reduced = _allreduce_small("allreduce_small", jnp.concatenate(small_rows, axis=0))
    small_full, off = [], 0
    for p, rws in zip(small_parts, small_rows):
        small_full.append(reduced[off:off + p.size // LANES].reshape(p.shape))
        off += rws.shape[0]
    g_gains, g_ca, g_cb, g_bias, g_wa, g_ba, g_wx, g_bx, g_lam = small_full

    def my_cols(g, width):
        return lax.dynamic_slice_in_dim(g, chip * width, width, axis=g.ndim - 1)

    small = [
        ("norm_gains", norm_gains, my_cols(g_gains, norm_gains.shape[2]).reshape(norm_gains.shape),
         m_norm_gains, v_norm_gains),
        ("hyb_conv_a", hyb_conv_a, my_cols(g_ca, hyb_conv_a.shape[2])[None], m_hyb_conv_a, v_hyb_conv_a),
        ("hyb_conv_b", hyb_conv_b, my_cols(g_cb, hyb_conv_b.shape[2])[None], m_hyb_conv_b, v_hyb_conv_b),
        ("hyb_conv_b_bias", hyb_conv_b_bias, g_bias, m_hyb_conv_b_bias, v_hyb_conv_b_bias),
        ("hyb_rg_w_a", hyb_rg_w_a, g_wa[None], m_hyb_rg_w_a, v_hyb_rg_w_a),
        ("hyb_rg_b_a", hyb_rg_b_a, g_ba, m_hyb_rg_b_a, v_hyb_rg_b_a),
        ("hyb_rg_w_x", hyb_rg_w_x, g_wx[None], m_hyb_rg_w_x, v_hyb_rg_w_x),
        ("hyb_rg_b_x", hyb_rg_b_x, g_bx, m_hyb_rg_b_x, v_hyb_rg_b_x),
        ("hyb_rg_lambda", hyb_rg_lambda, g_lam, m_hyb_rg_lambda, v_hyb_rg_lambda),
    ]
    to2d = lambda a: a.reshape(-1, a.shape[-1])
    small_res = _adamw_small("adamw_small", [tuple(to2d(a) for a in (w, g, m, v)) for _, w, g, m, v in small])
    out = {}
    for (nm, w, g, _, _), (dl, nmom, nvar) in zip(small, small_res):
        out[nm] = (g, dl.reshape(w.shape), nmom.reshape(w.shape), nvar.reshape(w.shape))

    slabs = [grads_big[k] for k in names]
    recv_sib = _exchange_sibling_halves("grads_to_sibling", slabs)
    chip_part = [_add_sibling("grads_add_" + k, sl, rv, cc_) for k, sl, rv in zip(names, slabs, recv_sib)]
    recv_chip = _exchange_chips("grads_to_chips", chip_part)
    halves = [_sum_chips("grads_sum_" + k, own, rc, chip, cc_) for k, own, rc in zip(names, chip_part, recv_chip)]
    gfull = dict(zip(names, _join_sibling_halves("grads_join", halves)))

    stacked = {
        "hyb_w_in": (hyb_w_in, m_hyb_w_in, v_hyb_w_in, ["hyb_w_in"]),
        "hyb_w_out": (hyb_w_out, m_hyb_w_out, v_hyb_w_out, ["hyb_w_out"]),
        "sb_w_qkv": (sb_w_qkv, m_sb_w_qkv, v_sb_w_qkv, ["sb_w_qkv"]),
        "sb_w_o": (sb_w_o, m_sb_w_o, v_sb_w_o, ["sb_w_o"]),
        "mlp_w_up": (mlp_w_up, m_mlp_w_up, v_mlp_w_up, ["mlp_w_up0", "mlp_w_up1"]),
        "mlp_w_down": (mlp_w_down, m_mlp_w_down, v_mlp_w_down, ["mlp_w_down0", "mlp_w_down1"]),
    }
    for k, (w, m, v, parts) in stacked.items():
        out[k] = tuple(_adamw("adamw_" + k, w, [gfull[p] for p in parts], m, v))

    order = ["norm_gains", "hyb_w_in", "hyb_conv_a", "hyb_conv_b", "hyb_conv_b_bias", "hyb_rg_w_a", "hyb_rg_b_a",
             "hyb_rg_w_x", "hyb_rg_b_x", "hyb_rg_lambda", "hyb_w_out", "sb_w_qkv", "sb_w_o", "mlp_w_up",
             "mlp_w_down"]
    return (loss, dx0[None], *[out[k][0] for k in order], *[out[k][1] for k in order],
            *[out[k][2] for k in order], *[out[k][3] for k in order])
```

```python
import functools
import math

import jax
import jax.numpy as jnp
from jax import lax
from jax.experimental import pallas as pl
from jax.experimental.pallas import tpu as pltpu

F32 = jnp.float32
BF16 = jnp.bfloat16
MESH = pl.DeviceIdType.MESH

SB_HEADS = 16
NORM_EPS = 1e-6
LRU_C = 8.0
ADAM_LR = 0.001
ADAM_B1 = 0.9
ADAM_B2 = 0.999
ADAM_EPS = 1e-08
ADAM_WD = 0.01
ADAM_STEP = 10

LANES = 128
SUBLANES = 8
VMEM_LIMIT = 48 * 1024 * 1024
MM_TILE = 1024
MM_TILE_K = 2048
ROW_TILE = 256
ATT_TILE = 512
ATT_HEADS_PER_STEP = 2
N_CHIPS = 4
N_DEV = 8

_DIMS = {
    "nn": (((1,), (0,)), ((), ())),
    "nt": (((1,), (1,)), ((), ())),
    "tn": (((0,), (0,)), ((), ())),
}


def _cp(sem=None, vmem=VMEM_LIMIT):
    return pltpu.CompilerParams(dimension_semantics=sem, vmem_limit_bytes=vmem)


def _pick(dim, pref):
    t = min(dim, pref)
    while dim % t:
        t -= LANES
    return t


def _whole(shape):
    nd = len(shape)
    return pl.BlockSpec(tuple(shape), lambda *_: (0,) * nd)


def _sigmoid(z):
    return 1.0 / (1.0 + jnp.exp(-z))


def _log_sigmoid(z):
    return jnp.minimum(z, 0.0) - jnp.log(1.0 + jnp.exp(-jnp.abs(z)))


def _expm1(z):
    series = z * (1.0 + z * (0.5 + z * (1.0 / 6.0 + z * (1.0 / 24.0))))
    return jnp.where(jnp.abs(z) < 0.05, series, jnp.exp(z) - 1.0)


_GELU_C = math.sqrt(2.0 / math.pi)


def _gelu_and_grad(g):
    inner = _GELU_C * (g + 0.044715 * g * g * g)
    t = jnp.tanh(inner)
    val = 0.5 * g * (1.0 + t)
    grad = 0.5 * (1.0 + t) + 0.5 * g * (1.0 - t * t) * _GELU_C * (1.0 + 3.0 * 0.044715 * g * g)
    return val, grad


def _shift_down(cur, prev8, k, rows):
    n = cur.shape[0]
    rolled = pltpu.roll(cur, k, 0)
    head = jnp.tile(pltpu.roll(prev8, k, 0), (n // SUBLANES, 1))
    return jnp.where(rows < k, head, rolled)


def _shift_up(cur, next8, k, rows):
    n = cur.shape[0]
    rolled = pltpu.roll(cur, n - k, 0)
    tail = jnp.tile(pltpu.roll(next8, SUBLANES - k, 0), (n // SUBLANES, 1))
    return jnp.where(rows >= n - k, tail, rolled)


def _colsum(v):
    return jnp.sum(v, axis=0, keepdims=True)


def _matmul(name, mode, grid, operands, in_specs, out_shapes, out_specs, acc_shape, epilogue=None):
    nk = grid[2]
    n_in = len(operands)
    dims = _DIMS[mode]

    def finish(acc, extra, outs):
        res = epilogue(acc, *[e[...] for e in extra]) if epilogue is not None else (acc,)
        for o_ref, o in zip(outs, res):
            o_ref[...] = o.astype(o_ref.dtype)

    def product(a_ref, b_ref):
        return lax.dot_general(a_ref[...].astype(BF16), b_ref[...].astype(BF16), dims, preferred_element_type=F32)

    def body_single(*refs):
        finish(product(refs[0], refs[1]), refs[2:n_in], refs[n_in:])

    def body(*refs):
        extra = refs[2:n_in]
        outs = refs[n_in:-1]
        acc_ref = refs[-1]
        k = pl.program_id(2)

        @pl.when(k == 0)
        def _():
            acc_ref[...] = product(refs[0], refs[1])

        @pl.when(k > 0)
        def _():
            acc_ref[...] += product(refs[0], refs[1])

        @pl.when(k == nk - 1)
        def _():
            finish(acc_ref[...], extra, outs)

    return pl.pallas_call(
        body_single if nk == 1 else body, name=name, grid=grid, in_specs=in_specs, out_specs=out_specs,
        out_shape=out_shapes, scratch_shapes=[] if nk == 1 else [pltpu.VMEM(acc_shape, F32)],
        compiler_params=_cp(("parallel", "parallel", "arbitrary")),
    )(*operands)


def _mm_fwd_col(name, a, wfull, out_dtypes=(F32,), epilogue=None):
    s, kdim = a.shape
    _, _, cs = wfull.shape
    tm, tk, tn = _pick(s, MM_TILE), _pick(kdim, MM_TILE_K), _pick(cs, MM_TILE)
    nbj = cs // tn
    grid = (s // tm, N_CHIPS * nbj, kdim // tk)
    out_shapes = [jax.ShapeDtypeStruct((s, N_CHIPS * cs), dt) for dt in out_dtypes]
    out_specs = [pl.BlockSpec((tm, tn), lambda i, n, k: (i, n)) for _ in out_dtypes]
    return _matmul(
        name, "nn", grid, [a, wfull],
        [pl.BlockSpec((tm, tk), lambda i, n, k: (i, k)),
         pl.BlockSpec((None, tk, tn), lambda i, n, k: (n // nbj, k, n % nbj))],
        out_shapes, out_specs, (tm, tn), epilogue)


def _mm_fwd_row(name, a, w2d, out_dtype=F32):
    s, kdim = a.shape
    _, n_out = w2d.shape
    tm, tk, tn = _pick(s, MM_TILE), _pick(kdim, MM_TILE_K), _pick(n_out, MM_TILE)
    grid = (s // tm, n_out // tn, kdim // tk)
    return _matmul(
        name, "nn", grid, [a, w2d],
        [pl.BlockSpec((tm, tk), lambda i, n, k: (i, k)),
         pl.BlockSpec((tk, tn), lambda i, n, k: (k, n))],
        [jax.ShapeDtypeStruct((s, n_out), out_dtype)],
        [pl.BlockSpec((tm, tn), lambda i, n, k: (i, n))], (tm, tn))[0]


def _mm_bwd_col(name, dy, wfull, out_dtype=F32):
    s, _ = dy.shape
    _, kdim, cs = wfull.shape
    tm, tn, tk = _pick(s, MM_TILE), _pick(kdim, MM_TILE), _pick(cs, MM_TILE_K)
    nbj = cs // tk
    grid = (s // tm, kdim // tn, N_CHIPS * nbj)
    return _matmul(
        name, "nt", grid, [dy, wfull],
        [pl.BlockSpec((tm, tk), lambda i, n, k: (i, k)),
         pl.BlockSpec((None, tn, tk), lambda i, n, k: (k // nbj, n, k % nbj))],
        [jax.ShapeDtypeStruct((s, kdim), out_dtype)],
        [pl.BlockSpec((tm, tn), lambda i, n, k: (i, n))], (tm, tn))[0]


def _mm_bwd_row(name, dy, w2d, out_dtypes=(F32,), extra=None, epilogue=None):
    s, n_in = dy.shape
    kdim, _ = w2d.shape
    tm, tn, tk = _pick(s, MM_TILE), _pick(kdim, MM_TILE), _pick(n_in, MM_TILE_K)
    grid = (s // tm, kdim // tn, n_in // tk)
    operands = [dy, w2d]
    in_specs = [pl.BlockSpec((tm, tk), lambda i, n, k: (i, k)),
                pl.BlockSpec((tn, tk), lambda i, n, k: (n, k))]
    if extra is not None:
        operands.append(extra)
        in_specs.append(pl.BlockSpec((tm, tn), lambda i, n, k: (i, n)))
    return _matmul(
        name, "nt", grid, operands, in_specs,
        [jax.ShapeDtypeStruct((s, kdim), dt) for dt in out_dtypes],
        [pl.BlockSpec((tm, tn), lambda i, n, k: (i, n)) for _ in out_dtypes], (tm, tn), epilogue)


def _mm_wgrad_col(name, a, dy, cs):
    s, kdim = a.shape
    tm, tn, ts = _pick(kdim, MM_TILE), _pick(cs, MM_TILE), _pick(s, MM_TILE_K)
    nbj = cs // tn
    grid = (kdim // tm, N_CHIPS * nbj, s // ts)
    return _matmul(
        name, "tn", grid, [a, dy],
        [pl.BlockSpec((ts, tm), lambda i, n, k: (k, i)),
         pl.BlockSpec((ts, tn), lambda i, n, k: (k, n))],
        [jax.ShapeDtypeStruct((N_CHIPS, kdim, cs), BF16)],
        [pl.BlockSpec((None, tm, tn), lambda i, n, k: (n // nbj, i, n % nbj))], (tm, tn))[0]


def _mm_wgrad_row(name, a, dy):
    s, kdim = a.shape
    _, n_out = dy.shape
    tm, tn, ts = _pick(kdim, MM_TILE), _pick(n_out, MM_TILE), _pick(s, MM_TILE_K)
    grid = (kdim // tm, n_out // tn, s // ts)
    return _matmul(
        name, "tn", grid, [a, dy],
        [pl.BlockSpec((ts, tm), lambda i, n, k: (k, i)),
         pl.BlockSpec((ts, tn), lambda i, n, k: (k, n))],
        [jax.ShapeDtypeStruct((kdim, n_out), BF16)],
        [pl.BlockSpec((tm, tn), lambda i, n, k: (i, n))], (tm, tn))[0]


def _mm_wgrad_diag(name, a, dy):
    s, width = a.shape
    nb = width // LANES
    ts = _pick(s, MM_TILE)
    grid = (nb, 1, s // ts)
    return _matmul(
        name, "tn", grid, [a, dy],
        [pl.BlockSpec((ts, LANES), lambda i, n, k: (k, i)),
         pl.BlockSpec((ts, LANES), lambda i, n, k: (k, i))],
        [jax.ShapeDtypeStruct((nb, LANES, LANES), F32)],
        [pl.BlockSpec((None, LANES, LANES), lambda i, n, k: (i, 0, 0))], (LANES, LANES))[0]


def _rowspec(tr, d):
    return pl.BlockSpec((tr, d), lambda i: (i, 0))


def _vecspec(d):
    return pl.BlockSpec((1, d), lambda i: (0, 0))


def _rms(x, g):
    return x * lax.rsqrt(jnp.mean(x * x, axis=-1, keepdims=True) + NORM_EPS) * g


def _cast_into_slot(name, w, chip):
    r, c = w.shape
    tr = _pick(r, ROW_TILE)

    def body(chip_ref, w_ref, o_ref):
        o_ref[...] = w_ref[...].astype(BF16)

    grid_spec = pltpu.PrefetchScalarGridSpec(
        num_scalar_prefetch=1, grid=(r // tr,),
        in_specs=[pl.BlockSpec((tr, c), lambda i, chip_ref: (i, 0))],
        out_specs=pl.BlockSpec((None, tr, c), lambda i, chip_ref: (chip_ref[0], i, 0)))
    return pl.pallas_call(
        body, name=name, grid_spec=grid_spec, out_shape=jax.ShapeDtypeStruct((N_CHIPS, r, c), BF16),
        compiler_params=_cp(("parallel",)))(jnp.reshape(chip, (1,)).astype(jnp.int32), w)


def _rms_fwd(name, x, g):
    s, d = x.shape
    tr = _pick(s, ROW_TILE)

    def body(x_ref, g_ref, h_ref):
        h_ref[...] = _rms(x_ref[...], g_ref[...]).astype(BF16)

    return pl.pallas_call(
        body, name=name, grid=(s // tr,), in_specs=[_rowspec(tr, d), _vecspec(d)],
        out_specs=_rowspec(tr, d), out_shape=jax.ShapeDtypeStruct((s, d), BF16),
        compiler_params=_cp(("parallel",)))(x, g)


def _rms_post(name, y, g_post, res, g_next=None):
    s, d = y.shape
    tr = _pick(s, ROW_TILE)
    with_next = g_next is not None

    def body(*refs):
        if with_next:
            y_ref, gp_ref, r_ref, gn_ref, x_ref, h_ref = refs
        else:
            y_ref, gp_ref, r_ref, x_ref = refs
        xn = r_ref[...] + _rms(y_ref[...], gp_ref[...])
        x_ref[...] = xn
        if with_next:
            h_ref[...] = _rms(xn, gn_ref[...]).astype(BF16)

    operands = [y, g_post, res] + ([g_next] if with_next else [])
    in_specs = [_rowspec(tr, d), _vecspec(d), _rowspec(tr, d)] + ([_vecspec(d)] if with_next else [])
    out_shape = [jax.ShapeDtypeStruct((s, d), F32)] + ([jax.ShapeDtypeStruct((s, d), BF16)] if with_next else [])
    out_specs = [_rowspec(tr, d)] + ([_rowspec(tr, d)] if with_next else [])
    return pl.pallas_call(
        body, name=name, grid=(s // tr,), in_specs=in_specs, out_specs=out_specs, out_shape=out_shape,
        compiler_params=_cp(("parallel",)))(*operands)


def _rms_bwd(name, x, g, dy, res=None, out_dtype=F32):
    s, d = x.shape
    tr = _pick(s, ROW_TILE)
    nsteps = s // tr
    with_res = res is not None

    def body(*refs):
        if with_res:
            x_ref, g_ref, dy_ref, r_ref, dx_ref, dg_ref, acc_ref = refs
        else:
            x_ref, g_ref, dy_ref, dx_ref, dg_ref, acc_ref = refs
        i = pl.program_id(0)

        @pl.when(i == 0)
        def _():
            acc_ref[...] = jnp.zeros_like(acc_ref)

        xv = x_ref[...]
        dyv = dy_ref[...].astype(F32)
        r = lax.rsqrt(jnp.mean(xv * xv, axis=-1, keepdims=True) + NORM_EPS)
        xhat = xv * r
        gy = dyv * g_ref[...]
        dx = r * (gy - xhat * jnp.mean(gy * xhat, axis=-1, keepdims=True))
        if with_res:
            dx = dx + r_ref[...]
        dx_ref[...] = dx.astype(dx_ref.dtype)
        acc_ref[...] += jnp.sum((dyv * xhat).reshape(tr // SUBLANES, SUBLANES, d), axis=0)

        @pl.when(i == nsteps - 1)
        def _():
            dg_ref[...] = jnp.broadcast_to(_colsum(acc_ref[...]), (SUBLANES, d))

    operands = [x, g, dy] + ([res] if with_res else [])
    in_specs = [_rowspec(tr, d), _vecspec(d), _rowspec(tr, d)] + ([_rowspec(tr, d)] if with_res else [])
    dx, dg = pl.pallas_call(
        body, name=name, grid=(nsteps,), in_specs=in_specs,
        out_specs=[_rowspec(tr, d), pl.BlockSpec((SUBLANES, d), lambda i: (0, 0))],
        out_shape=[jax.ShapeDtypeStruct((s, d), out_dtype), jax.ShapeDtypeStruct((SUBLANES, d), F32)],
        scratch_shapes=[pltpu.VMEM((SUBLANES, d), F32)],
        compiler_params=_cp(("arbitrary",)))(*operands)
    return dx, dg[0:1]


def _loss_head(name, y, target):
    s, d = y.shape
    tr = _pick(s, ROW_TILE)
    nsteps = s // tr

    def body(y_ref, t_ref, dy_ref, l_ref, acc_ref):
        i = pl.program_id(0)

        @pl.when(i == 0)
        def _():
            acc_ref[...] = jnp.zeros_like(acc_ref)

        err = y_ref[...] - t_ref[...]
        dy_ref[...] = err * (1.0 / d)
        acc_ref[...] += jnp.sum((err * err).reshape(tr // SUBLANES, SUBLANES, d), axis=0)

        @pl.when(i == nsteps - 1)
        def _():
            l_ref[...] = jnp.full((SUBLANES, LANES), (0.5 / d) * jnp.sum(acc_ref[...]), F32)

    dy, l = pl.pallas_call(
        body, name=name, grid=(nsteps,), in_specs=[_rowspec(tr, d), _rowspec(tr, d)],
        out_specs=[_rowspec(tr, d), pl.BlockSpec((SUBLANES, LANES), lambda i: (0, 0))],
        out_shape=[jax.ShapeDtypeStruct((s, d), F32), jax.ShapeDtypeStruct((SUBLANES, LANES), F32)],
        scratch_shapes=[pltpu.VMEM((SUBLANES, d), F32)],
        compiler_params=_cp(("arbitrary",)))(y, target)
    return dy, l[0, 0]


def _gates(xr, wa, ba, wx, bx, lam):
    xb = xr.astype(BF16)
    r = _sigmoid(jnp.dot(xb, wa, preferred_element_type=F32) + ba)
    i = _sigmoid(jnp.dot(xb, wx, preferred_element_type=F32) + bx)
    log_a = LRU_C * r * _log_sigmoid(lam)
    a = jnp.exp(log_a)
    m = jnp.sqrt(-_expm1(2.0 * log_a))
    return r, i, a, m


def _mixer_fwd(proj, conv_a, conv_b, bias, wa_blk, ba, wx_blk, bx, lam):
    s, w5 = proj.shape
    w = w5 // 5
    nch = w // LANES
    ts = _pick(s, ROW_TILE)
    nt = s // ts

    def body(p_ref, pp_ref, ca_ref, cb_ref, bias_ref, wa_ref, ba_ref, wx_ref, bx_ref, lam_ref,
             y_ref, h_ref, a_scr, b_scr, hc_scr):
        t = pl.program_id(0)
        first = t == 0
        rows = lax.broadcasted_iota(jnp.int32, (ts, LANES), 0)

        @pl.when(first)
        def _():
            hc_scr[...] = jnp.zeros_like(hc_scr)

        def cur(comp, c):
            return p_ref[:, comp * w + c * LANES:comp * w + (c + 1) * LANES]

        def prev(comp, c):
            v = pp_ref[:, comp * w + c * LANES:comp * w + (c + 1) * LANES]
            return jnp.where(first, 0.0, v)

        for c in range(nch):
            sl = slice(c * LANES, (c + 1) * LANES)
            cx = cur(1, c) * cur(2, c)
            cxp = prev(1, c) * prev(2, c)
            wa3 = ca_ref[:, sl]
            conv = (wa3[2:3] * cx + wa3[1:2] * _shift_down(cx, cxp, 1, rows)
                    + wa3[0:1] * _shift_down(cx, cxp, 2, rows))
            y_ref[:, sl] = (cur(0, c) * conv).astype(BF16)

        for c in range(nch):
            sl = slice(c * LANES, (c + 1) * LANES)
            xb, xbp = cur(4, c), prev(4, c)
            wb4 = cb_ref[:, sl]
            xr = (wb4[3:4] * xb + wb4[2:3] * _shift_down(xb, xbp, 1, rows)
                  + wb4[1:2] * _shift_down(xb, xbp, 2, rows)
                  + wb4[0:1] * _shift_down(xb, xbp, 3, rows) + bias_ref[:, sl])
            _, i, a, m = _gates(xr, wa_ref[c], ba_ref[:, sl], wx_ref[c], bx_ref[:, sl], lam_ref[:, sl])
            a_scr[:, sl] = a
            b_scr[:, sl] = m * i * xr

        def step(r, h):
            h = a_scr[pl.ds(r, 1), :] * h + b_scr[pl.ds(r, 1), :]
            h_ref[pl.ds(r, 1), :] = h
            return h

        hc_scr[0:1, :] = lax.fori_loop(0, ts, step, hc_scr[0:1, :], unroll=8)

        for c in range(nch):
            sl = slice(c * LANES, (c + 1) * LANES)
            gel, _ = _gelu_and_grad(cur(3, c))
            y_ref[:, w + c * LANES:w + (c + 1) * LANES] = (h_ref[:, sl] * gel).astype(BF16)

    vec = lambda n: _whole((n, w))
    return pl.pallas_call(
        body, name="mixer_fwd", grid=(nt,),
        in_specs=[pl.BlockSpec((ts, w5), lambda t: (t, 0)),
                  pl.BlockSpec((SUBLANES, w5), lambda t: (jnp.maximum(t * (ts // SUBLANES) - 1, 0), 0)),
                  vec(3), vec(4), vec(1), _whole(wa_blk.shape), vec(1), _whole(wx_blk.shape), vec(1), vec(1)],
        out_specs=[pl.BlockSpec((ts, 2 * w), lambda t: (t, 0)), pl.BlockSpec((ts, w), lambda t: (t, 0))],
        out_shape=[jax.ShapeDtypeStruct((s, 2 * w), BF16), jax.ShapeDtypeStruct((s, w), F32)],
        scratch_shapes=[pltpu.VMEM((ts, w), F32), pltpu.VMEM((ts, w), F32), pltpu.VMEM((SUBLANES, w), F32)],
        compiler_params=_cp(("arbitrary",)),
    )(proj, proj, conv_a, conv_b, bias, wa_blk, ba, wx_blk, bx, lam)


_SG_CONV_A, _SG_CONV_B, _SG_BIAS, _SG_BA, _SG_BX, _SG_LAM, _SG_ROWS = 0, 3, 7, 8, 9, 10, 16


def _mixer_bwd(proj, hseq, dy, conv_a, conv_b, bias, wa_blk, ba, wx_blk, bx, lam):
    s, w5 = proj.shape
    w = w5 // 5
    nch = w // LANES
    ts = _pick(s, ROW_TILE)
    nt = s // ts
    tpb = ts // SUBLANES

    def body(p_ref, pp_ref, h_ref, hp_ref, dy_ref, ca_ref, cb_ref, bias_ref, wa_ref, ba_ref, wx_ref, bx_ref,
             lam_ref, dp_ref, xr_ref, dpa_ref, dpx_ref, sg_ref,
             a_scr, g_scr, l_scr, x_scr, r_scr, i_scr, m_scr, cl_scr, cdc_scr, cdx_scr):
        pid = pl.program_id(0)
        last = pid == 0
        first = pid == nt - 1
        rows = lax.broadcasted_iota(jnp.int32, (ts, LANES), 0)

        @pl.when(last)
        def _():
            sg_ref[...] = jnp.zeros_like(sg_ref)
            cl_scr[...] = jnp.zeros_like(cl_scr)
            cdc_scr[...] = jnp.zeros_like(cdc_scr)
            cdx_scr[...] = jnp.zeros_like(cdx_scr)

        def cur(comp, c):
            return p_ref[:, comp * w + c * LANES:comp * w + (c + 1) * LANES]

        def prev(comp, c):
            v = pp_ref[:, comp * w + c * LANES:comp * w + (c + 1) * LANES]
            return jnp.where(first, 0.0, v)

        def put(comp, c, v):
            dp_ref[:, comp * w + c * LANES:comp * w + (c + 1) * LANES] = v

        def acc(row, sl, v):
            sg_ref[row:row + 1, sl] += _colsum(v)

        for c in range(nch):
            sl = slice(c * LANES, (c + 1) * LANES)
            bg, cg, ax = cur(0, c), cur(1, c), cur(2, c)
            cx = cg * ax
            cxp = prev(1, c) * prev(2, c)
            cx1 = _shift_down(cx, cxp, 1, rows)
            cx2 = _shift_down(cx, cxp, 2, rows)
            wa3 = ca_ref[:, sl]
            conv = wa3[2:3] * cx + wa3[1:2] * cx1 + wa3[0:1] * cx2
            dya = dy_ref[:, sl]
            put(0, c, dya * conv)
            dconv = dya * bg
            nxt = cdc_scr[:, sl]
            dcx = (wa3[2:3] * dconv + wa3[1:2] * _shift_up(dconv, nxt, 1, rows)
                   + wa3[0:1] * _shift_up(dconv, nxt, 2, rows))
            cdc_scr[:, sl] = dconv[0:SUBLANES]
            put(1, c, dcx * ax)
            put(2, c, dcx * cg)
            acc(_SG_CONV_A + 2, sl, dconv * cx)
            acc(_SG_CONV_A + 1, sl, dconv * cx1)
            acc(_SG_CONV_A + 0, sl, dconv * cx2)

        for c in range(nch):
            sl = slice(c * LANES, (c + 1) * LANES)
            xb, xbp = cur(4, c), prev(4, c)
            wb4 = cb_ref[:, sl]
            xr = (wb4[3:4] * xb + wb4[2:3] * _shift_down(xb, xbp, 1, rows)
                  + wb4[1:2] * _shift_down(xb, xbp, 2, rows)
                  + wb4[0:1] * _shift_down(xb, xbp, 3, rows) + bias_ref[:, sl])
            r, i, a, m = _gates(xr, wa_ref[c], ba_ref[:, sl], wx_ref[c], bx_ref[:, sl], lam_ref[:, sl])
            gel, dgel = _gelu_and_grad(cur(3, c))
            dyb = dy_ref[:, w + c * LANES:w + (c + 1) * LANES]
            put(3, c, dyb * h_ref[:, sl] * dgel)
            g_scr[:, sl] = dyb * gel
            a_scr[:, sl] = a
            x_scr[:, sl] = xr
            r_scr[:, sl] = r
            i_scr[:, sl] = i
            m_scr[:, sl] = m

        def step(j, carry):
            r = ts - 1 - j
            lam_t = g_scr[pl.ds(r, 1), :] + carry
            l_scr[pl.ds(r, 1), :] = lam_t
            return a_scr[pl.ds(r, 1), :] * lam_t

        cl_scr[0:1, :] = lax.fori_loop(0, ts, step, cl_scr[0:1, :], unroll=8)

        for c in range(nch):
            sl = slice(c * LANES, (c + 1) * LANES)
            lam_t = l_scr[:, sl]
            hprev = _shift_down(h_ref[:, sl], jnp.where(first, 0.0, hp_ref[:, sl]), 1, rows)
            xr, r, i, m, a = x_scr[:, sl], r_scr[:, sl], i_scr[:, sl], m_scr[:, sl], a_scr[:, sl]
            da = lam_t * hprev
            dm = lam_t * i * xr
            di = lam_t * m * xr
            dxr = lam_t * m * i
            dlog_a = da * a - dm * a * a / m
            lam_p = lam_ref[:, sl]
            dr = dlog_a * (LRU_C * _log_sigmoid(lam_p))
            acc(_SG_LAM, sl, dlog_a * r * (LRU_C * _sigmoid(-lam_p)))
            dpa = dr * r * (1.0 - r)
            dpx = di * i * (1.0 - i)
            dpa_b, dpx_b = dpa.astype(BF16), dpx.astype(BF16)
            dxr = (dxr + lax.dot_general(dpa_b, wa_ref[c], _DIMS["nt"], preferred_element_type=F32)
                   + lax.dot_general(dpx_b, wx_ref[c], _DIMS["nt"], preferred_element_type=F32))
            xr_ref[:, sl] = xr.astype(BF16)
            dpa_ref[:, sl] = dpa_b
            dpx_ref[:, sl] = dpx_b
            acc(_SG_BA, sl, dpa)
            acc(_SG_BX, sl, dpx)
            acc(_SG_BIAS, sl, dxr)
            nxt = cdx_scr[:, sl]
            wb4 = cb_ref[:, sl]
            put(4, c, wb4[3:4] * dxr + wb4[2:3] * _shift_up(dxr, nxt, 1, rows)
                + wb4[1:2] * _shift_up(dxr, nxt, 2, rows) + wb4[0:1] * _shift_up(dxr, nxt, 3, rows))
            cdx_scr[:, sl] = dxr[0:SUBLANES]
            xb, xbp = cur(4, c), prev(4, c)
            acc(_SG_CONV_B + 3, sl, dxr * xb)
            acc(_SG_CONV_B + 2, sl, dxr * _shift_down(xb, xbp, 1, rows))
            acc(_SG_CONV_B + 1, sl, dxr * _shift_down(xb, xbp, 2, rows))
            acc(_SG_CONV_B + 0, sl, dxr * _shift_down(xb, xbp, 3, rows))

    blk = lambda width: pl.BlockSpec((ts, width), lambda p: (nt - 1 - p, 0))
    pre = lambda width: pl.BlockSpec(
        (SUBLANES, width), lambda p: (jnp.maximum((nt - 1 - p) * tpb - 1, 0), 0))
    vec = lambda n: _whole((n, w))
    big = lambda: pltpu.VMEM((ts, w), F32)
    small = lambda: pltpu.VMEM((SUBLANES, w), F32)
    return pl.pallas_call(
        body, name="mixer_bwd", grid=(nt,),
        in_specs=[blk(w5), pre(w5), blk(w), pre(w), blk(2 * w),
                  vec(3), vec(4), vec(1), _whole(wa_blk.shape), vec(1), _whole(wx_blk.shape), vec(1), vec(1)],
        out_specs=[blk(w5), blk(w), blk(w), blk(w), _whole((_SG_ROWS, w))],
        out_shape=[jax.ShapeDtypeStruct((s, w5), F32), jax.ShapeDtypeStruct((s, w), BF16),
                   jax.ShapeDtypeStruct((s, w), BF16), jax.ShapeDtypeStruct((s, w), BF16),
                   jax.ShapeDtypeStruct((_SG_ROWS, w), F32)],
        scratch_shapes=[big(), big(), big(), big(), big(), big(), big(), small(), small(), small()],
        compiler_params=_cp(("arbitrary",)),
    )(proj, proj, hseq, hseq, dy, conv_a, conv_b, bias, wa_blk, ba, wx_blk, bx, lam)


def _split_dot(v, tri):
    hi = v.astype(BF16)
    lo = (v - hi.astype(F32)).astype(BF16)
    return (jnp.dot(hi, tri, preferred_element_type=F32) + jnp.dot(lo, tri, preferred_element_type=F32))


def _tri(cmp):
    r = lax.broadcasted_iota(jnp.int32, (LANES, LANES), 0)
    c = lax.broadcasted_iota(jnp.int32, (LANES, LANES), 1)
    return cmp(r, c).astype(BF16)


def _lane_blocks(v):
    return [v[:, b * LANES:(b + 1) * LANES] for b in range(v.shape[1] // LANES)]


def _last_lane(v):
    return jnp.broadcast_to(v[:, LANES - 1:LANES], v.shape)


def _scores(q, kb, scale):
    return lax.dot_general(q, kb, _DIMS["nt"], preferred_element_type=F32) * scale


def _log_gates(z, diagonal):
    ls = jnp.minimum(z, 0.0) - jnp.log(1.0 + jnp.exp(-jnp.abs(z)))
    ln = ls - z
    valid = None
    if diagonal:
        valid = (lax.broadcasted_iota(jnp.int32, z.shape, 1) < lax.broadcasted_iota(jnp.int32, z.shape, 0))
        ln = jnp.where(valid, ln, 0.0)
    return ls, ln, valid


def _attn_fwd(qkv, heads):
    s = qkv.shape[0]
    dh = LANES
    tq = _pick(s, ATT_TILE)
    nq = s // tq
    nb = tq // LANES
    scale = 1.0 / math.sqrt(dh)

    hp = ATT_HEADS_PER_STEP
    groups = heads // hp
    wid = hp * dh

    def body(q_ref, k_ref, v_ref, o_ref, tot_ref, acc_scr, car_scr):
        qi = pl.program_id(1)
        acc_scr[...] = jnp.zeros_like(acc_scr)
        car_scr[...] = jnp.zeros_like(car_scr)
        tri = _tri(lambda r, c: r > c)

        def tile(kt, diagonal):
            k0 = pl.multiple_of(kt * tq, tq)
            heads_cols = [slice(hh * dh, (hh + 1) * dh) for hh in range(hp)]
            zs = [_scores(q_ref[:, cols], k_ref[pl.ds(k0, tq), cols], scale) for cols in heads_cols]
            gates = [_log_gates(z, diagonal) for z in zs]
            sfxs = [_split_dot(jnp.concatenate(_lane_blocks(ln), axis=0), tri) for _, ln, _ in gates]
            for cols, (ls, ln, valid), sfx in zip(heads_cols, gates, sfxs):
                blocks = _lane_blocks(ln)
                car = car_scr[:, cols]
                parts = [None] * nb
                for b in reversed(range(nb)):
                    sb = sfx[b * tq:(b + 1) * tq]
                    parts[b] = sb + car
                    car = car + (sb[:, 0:1] + blocks[b][:, 0:1])
                car_scr[:, cols] = car
                wgt = jnp.exp(ls + jnp.concatenate(parts, axis=1))
                if diagonal:
                    wgt = jnp.where(valid, wgt, 0.0)
                acc_scr[:, cols] += jnp.dot(
                    wgt.astype(BF16), v_ref[pl.ds(k0, tq), cols], preferred_element_type=F32)

        tile(qi, True)

        def step(j, carry):
            tile(qi - 1 - j, False)
            return carry

        lax.fori_loop(0, qi, step, 0)
        o_ref[...] = acc_scr[...].astype(BF16)
        tot_ref[...] = car_scr[...]

    return pl.pallas_call(
        body, name="attn_fwd", grid=(groups, nq),
        in_specs=[pl.BlockSpec((tq, wid), lambda h, i: (i, h)),
                  pl.BlockSpec((s, wid), lambda h, i: (0, groups + h)),
                  pl.BlockSpec((s, wid), lambda h, i: (0, 2 * groups + h))],
        out_specs=[pl.BlockSpec((tq, wid), lambda h, i: (i, h)), pl.BlockSpec((tq, wid), lambda h, i: (i, h))],
        out_shape=[jax.ShapeDtypeStruct((s, heads * dh), BF16), jax.ShapeDtypeStruct((s, heads * dh), F32)],
        scratch_shapes=[pltpu.VMEM((tq, wid), F32), pltpu.VMEM((tq, wid), F32)],
        compiler_params=_cp(("parallel", "arbitrary")),
    )(qkv, qkv, qkv)


def _attn_bwd(qkv, tot, do, heads):
    s = qkv.shape[0]
    dh = LANES
    tq = _pick(s, ATT_TILE)
    nq = s // tq
    nb = tq // LANES
    scale = 1.0 / math.sqrt(dh)

    hp = ATT_HEADS_PER_STEP
    groups = heads // hp
    wid = hp * dh

    def body(q_ref, k_ref, v_ref, tot_ref, do_ref, dq_ref, dk_ref, dv_ref,
             dq_scr, dk_scr, dv_scr, cl_scr, cg_scr):
        qi = pl.program_id(1)

        @pl.when(qi == 0)
        def _():
            dk_scr[...] = jnp.zeros_like(dk_scr)
            dv_scr[...] = jnp.zeros_like(dv_scr)

        dq_scr[...] = jnp.zeros_like(dq_scr)
        cl_scr[...] = jnp.zeros_like(cl_scr)
        cg_scr[...] = jnp.zeros_like(cg_scr)
        tri_le = _tri(lambda r, c: r <= c)
        tri_lt = _tri(lambda r, c: r < c)

        def tile(kt, diagonal):
            k0 = pl.multiple_of(kt * tq, tq)
            heads_cols = [slice(hh * dh, (hh + 1) * dh) for hh in range(hp)]
            keys = pl.ds(k0, tq)
            zs = [_scores(q_ref[:, cols], k_ref[keys, cols], scale) for cols in heads_cols]
            dws = [lax.dot_general(do_ref[:, cols], v_ref[keys, cols], _DIMS["nt"], preferred_element_type=F32)
                   for cols in heads_cols]
            gates = [_log_gates(z, diagonal) for z in zs]
            pins = [_split_dot(jnp.concatenate(_lane_blocks(ln), axis=0), tri_le) for _, ln, _ in gates]
            wgts, gs = [], []
            for cols, (ls, _, valid), pin, dw in zip(heads_cols, gates, pins, dws):
                total = tot_ref[:, cols]
                cl = cl_scr[:, cols]
                parts = []
                for b in range(nb):
                    pb = pin[b * tq:(b + 1) * tq] + cl
                    parts.append(total - pb)
                    cl = _last_lane(pb)
                cl_scr[:, cols] = cl
                wgt = jnp.exp(ls + jnp.concatenate(parts, axis=1))
                if diagonal:
                    wgt = jnp.where(valid, wgt, 0.0)
                wgts.append(wgt)
                gs.append(wgt * dw)
            pexs = [jnp.dot(jnp.concatenate(_lane_blocks(g), axis=0).astype(BF16), tri_lt,
                            preferred_element_type=F32) for g in gs]
            for cols, wgt in zip(heads_cols, wgts):
                dv_scr[keys, cols] += lax.dot_general(
                    wgt.astype(BF16), do_ref[:, cols], _DIMS["tn"], preferred_element_type=F32)
            for cols, (ls, _, valid), g, pex in zip(heads_cols, gates, gs, pexs):
                gblocks = _lane_blocks(g)
                cg = cg_scr[:, cols]
                parts = []
                for b in range(nb):
                    pb = pex[b * tq:(b + 1) * tq] + cg
                    parts.append(pb)
                    cg = _last_lane(pb + gblocks[b])
                cg_scr[:, cols] = cg
                dz = g - jnp.exp(ls) * (g + jnp.concatenate(parts, axis=1))
                if diagonal:
                    dz = jnp.where(valid, dz, 0.0)
                dz = dz.astype(BF16)
                dq_scr[:, cols] += jnp.dot(dz, k_ref[keys, cols], preferred_element_type=F32)
                dk_scr[keys, cols] += lax.dot_general(
                    dz, q_ref[:, cols], _DIMS["tn"], preferred_element_type=F32)

        def step(j, carry):
            tile(j, False)
            return carry

        lax.fori_loop(0, qi, step, 0)
        tile(qi, True)
        dq_ref[...] = (dq_scr[...] * scale).astype(BF16)

        @pl.when(qi == nq - 1)
        def _():
            dk_ref[...] = (dk_scr[...] * scale).astype(BF16)
            dv_ref[...] = dv_scr[...].astype(BF16)

    qblk = pl.BlockSpec((tq, wid), lambda h, i: (i, h))
    hblk = pl.BlockSpec((s, wid), lambda h, i: (0, h))
    out = jax.ShapeDtypeStruct((s, heads * dh), BF16)
    return pl.pallas_call(
        body, name="attn_bwd", grid=(groups, nq),
        in_specs=[qblk, pl.BlockSpec((s, wid), lambda h, i: (0, groups + h)),
                  pl.BlockSpec((s, wid), lambda h, i: (0, 2 * groups + h)), qblk, qblk],
        out_specs=[qblk, hblk, hblk], out_shape=[out, out, out],
        scratch_shapes=[pltpu.VMEM((tq, wid), F32), pltpu.VMEM((s, wid), F32), pltpu.VMEM((s, wid), F32),
                        pltpu.VMEM((tq, wid), F32), pltpu.VMEM((tq, wid), F32)],
        compiler_params=_cp(("parallel", "arbitrary")),
    )(qkv, qkv, qkv, tot, do)


def _place():
    x, y, c = lax.axis_index("x"), lax.axis_index("y"), lax.axis_index("c")
    chips = [(1 - x, y), (x, 1 - y), (1 - x, 1 - y)]
    return x, y, c, chips


def _hbm_specs(n):
    return [pl.BlockSpec(memory_space=pl.ANY) for _ in range(n)]


def _remote(src, dst, send_sem, recv_sem, dev):
    return pltpu.make_async_remote_copy(
        src_ref=src, dst_ref=dst, send_sem=send_sem, recv_sem=recv_sem, device_id=dev, device_id_type=MESH)


def _allgather_weights(name, fulls):
    n = len(fulls)

    def body(*refs):
        bufs = refs[n:2 * n]
        send_sems, recv_sems, fsend_sems, frecv_sems = refs[2 * n:]
        x, y, c, chips = _place()
        me = 2 * x + y
        sibling = (x, y, 1 - c)
        firsts = []
        for w in range(n):
            hr = bufs[w].shape[1] // 2
            mine = bufs[w].at[me, pl.ds(c * hr, hr)]
            for k, (px, py) in enumerate(chips):
                cp = _remote(mine, mine, send_sems.at[w, k], recv_sems.at[w, k], (px, py, c))
                cp.start()
                firsts.append(cp)
        passed = []
        for w in range(n):
            hr = bufs[w].shape[1] // 2
            for k, (px, py) in enumerate(chips):
                slot = bufs[w].at[2 * px + py, pl.ds(c * hr, hr)]
                _remote(slot, slot, send_sems.at[w, k], recv_sems.at[w, k], (px, py, c)).wait_recv()
                cp = _remote(slot, slot, fsend_sems.at[w, k], frecv_sems.at[w, k], sibling)
                cp.start()
                passed.append(cp)
        for w in range(n):
            hr = bufs[w].shape[1] // 2
            for k, (px, py) in enumerate(chips):
                slot = bufs[w].at[2 * px + py, pl.ds((1 - c) * hr, hr)]
                _remote(slot, slot, fsend_sems.at[w, k], frecv_sems.at[w, k], sibling).wait_recv()
        for cp in firsts + passed:
            cp.wait_send()

    sem = lambda: pltpu.SemaphoreType.DMA((n, 3))
    return pl.pallas_call(
        body, name=name, in_specs=_hbm_specs(n), out_specs=_hbm_specs(n),
        out_shape=[jax.ShapeDtypeStruct(f.shape, f.dtype) for f in fulls],
        input_output_aliases={w: w for w in range(n)},
        scratch_shapes=[sem(), sem(), sem(), sem()],
    )(*fulls)


def _exchange_sibling_halves(name, slabs):
    n = len(slabs)

    def body(*refs):
        ins, outs = refs[:n], refs[n:2 * n]
        send_sems, recv_sems = refs[2 * n:]
        x, y, c, _ = _place()
        cps = []
        for w in range(n):
            hr = ins[w].shape[1] // 2
            cp = _remote(ins[w].at[:, pl.ds((1 - c) * hr, hr), :], outs[w], send_sems.at[w], recv_sems.at[w],
                         (x, y, 1 - c))
            cp.start()
            cps.append(cp)
        for cp in cps:
            cp.wait()

    return pl.pallas_call(
        body, name=name, in_specs=_hbm_specs(n), out_specs=_hbm_specs(n),
        out_shape=[jax.ShapeDtypeStruct((N_CHIPS, s.shape[1] // 2, s.shape[2]), s.dtype) for s in slabs],
        scratch_shapes=[pltpu.SemaphoreType.DMA((n,)), pltpu.SemaphoreType.DMA((n,))],
    )(*slabs)


def _exchange_chips(name, parts):
    n = len(parts)

    def body(*refs):
        ins, outs = refs[:n], refs[n:2 * n]
        send_sems, recv_sems = refs[2 * n:]
        x, y, c, chips = _place()
        cps = []
        for w in range(n):
            for k, (px, py) in enumerate(chips):
                cp = _remote(ins[w].at[2 * px + py], outs[w].at[k], send_sems.at[w, k], recv_sems.at[w, k],
                             (px, py, c))
                cp.start()
                cps.append(cp)
        for cp in cps:
            cp.wait()

    return pl.pallas_call(
        body, name=name, in_specs=_hbm_specs(n), out_specs=_hbm_specs(n),
        out_shape=[jax.ShapeDtypeStruct((3,) + s.shape[1:], s.dtype) for s in parts],
        scratch_shapes=[pltpu.SemaphoreType.DMA((n, 3)), pltpu.SemaphoreType.DMA((n, 3))],
    )(*parts)


def _join_sibling_halves(name, bufs):
    n = len(bufs)

    def body(*refs):
        outs = refs[n:2 * n]
        send_sems, recv_sems = refs[2 * n:]
        x, y, c, _ = _place()
        cps = []
        for w in range(n):
            hr = outs[w].shape[0] // 2
            mine = outs[w].at[pl.ds(c * hr, hr)]
            cp = _remote(mine, mine, send_sems.at[w], recv_sems.at[w], (x, y, 1 - c))
            cp.start()
            cps.append(cp)
        for w in range(n):
            hr = outs[w].shape[0] // 2
            other = outs[w].at[pl.ds((1 - c) * hr, hr)]
            _remote(other, other, send_sems.at[w], recv_sems.at[w], (x, y, 1 - c)).wait_recv()
        for cp in cps:
            cp.wait_send()

    return pl.pallas_call(
        body, name=name, in_specs=_hbm_specs(n), out_specs=_hbm_specs(n),
        out_shape=[jax.ShapeDtypeStruct(b.shape, b.dtype) for b in bufs],
        input_output_aliases={w: w for w in range(n)},
        scratch_shapes=[pltpu.SemaphoreType.DMA((n,)), pltpu.SemaphoreType.DMA((n,))],
    )(*bufs)


def _allgather_chips_small(name, v):
    r = v.shape[0]

    def body(v_ref, o_ref, send_sems, recv_sems):
        x, y, c, chips = _place()
        me = 2 * x + y
        o_ref[me] = v_ref[...]
        cps = []
        for k, (px, py) in enumerate(chips):
            cp = _remote(v_ref, o_ref.at[me], send_sems.at[k], recv_sems.at[k], (px, py, c))
            cp.start()
            cps.append(cp)
        for k, (px, py) in enumerate(chips):
            slot = o_ref.at[2 * px + py]
            _remote(slot, slot, send_sems.at[k], recv_sems.at[k], (px, py, c)).wait_recv()
        for cp in cps:
            cp.wait_send()

    return pl.pallas_call(
        body, name=name, in_specs=[pl.BlockSpec(memory_space=pltpu.VMEM)],
        out_specs=pl.BlockSpec(memory_space=pltpu.VMEM),
        out_shape=jax.ShapeDtypeStruct((N_CHIPS, r, LANES), F32),
        scratch_shapes=[pltpu.SemaphoreType.DMA((3,)), pltpu.SemaphoreType.DMA((3,))],
    )(v)


def _allreduce_small(name, v):
    r = v.shape[0]

    def body(v_ref, o_ref, all_ref, send_sems, recv_sems):
        x, y, c, _ = _place()
        me = 4 * x + 2 * y + c
        all_ref[me] = v_ref[...]
        peers = [(1 - x if k & 4 else x, 1 - y if k & 2 else y, 1 - c if k & 1 else c)
                 for k in range(1, N_DEV)]
        cps = []
        for k, dev in enumerate(peers):
            cp = _remote(v_ref, all_ref.at[me], send_sems.at[k], recv_sems.at[k], dev)
            cp.start()
            cps.append(cp)
        for k, (px, py, pc) in enumerate(peers):
            slot = all_ref.at[4 * px + 2 * py + pc]
            _remote(slot, slot, send_sems.at[k], recv_sems.at[k], (px, py, pc)).wait_recv()
        for cp in cps:
            cp.wait_send()
        total = all_ref[0]
        for d in range(1, N_DEV):
            total = total + all_ref[d]
        o_ref[...] = total

    return pl.pallas_call(
        body, name=name, in_specs=[pl.BlockSpec(memory_space=pltpu.VMEM)],
        out_specs=pl.BlockSpec(memory_space=pltpu.VMEM),
        out_shape=jax.ShapeDtypeStruct((r, LANES), F32),
        scratch_shapes=[pltpu.VMEM((N_DEV, r, LANES), F32), pltpu.SemaphoreType.DMA((N_DEV - 1,)),
                        pltpu.SemaphoreType.DMA((N_DEV - 1,))],
    )(v)


def _add_sibling(name, slabs, recv, c):
    _, r, cols = slabs.shape
    hr = r // 2
    tr = _pick(hr, ROW_TILE)
    nb = hr // tr

    def body(c_ref, a_ref, b_ref, o_ref):
        o_ref[...] = (a_ref[...].astype(F32) + b_ref[...].astype(F32)).astype(BF16)

    grid_spec = pltpu.PrefetchScalarGridSpec(
        num_scalar_prefetch=1, grid=(N_CHIPS, nb),
        in_specs=[pl.BlockSpec((None, tr, cols), lambda j, i, c_ref: (j, c_ref[0] * nb + i, 0)),
                  pl.BlockSpec((None, tr, cols), lambda j, i, c_ref: (j, i, 0))],
        out_specs=pl.BlockSpec((None, tr, cols), lambda j, i, c_ref: (j, i, 0)))
    return pl.pallas_call(
        body, name=name, grid_spec=grid_spec,
        out_shape=jax.ShapeDtypeStruct((N_CHIPS, hr, cols), BF16),
        compiler_params=_cp(("parallel", "parallel")))(jnp.reshape(c, (1,)).astype(jnp.int32), slabs, recv)


def _sum_chips(name, own, recv, chip, c):
    _, hr, cols = recv.shape
    tr = _pick(hr, ROW_TILE)
    nb = hr // tr

    def body(sc_ref, own_ref, recv_ref, o_ref):
        total = own_ref[...].astype(F32)
        for k in range(3):
            total = total + recv_ref[k].astype(F32)
        o_ref[...] = total

    grid_spec = pltpu.PrefetchScalarGridSpec(
        num_scalar_prefetch=1, grid=(nb,),
        in_specs=[pl.BlockSpec((None, tr, cols), lambda i, sc: (sc[0], i, 0)),
                  pl.BlockSpec((3, tr, cols), lambda i, sc: (0, i, 0))],
        out_specs=pl.BlockSpec((tr, cols), lambda i, sc: (sc[1] * nb + i, 0)))
    return pl.pallas_call(
        body, name=name, grid_spec=grid_spec, out_shape=jax.ShapeDtypeStruct((2 * hr, cols), F32),
        compiler_params=_cp(("parallel",)))(jnp.stack([chip, c]).astype(jnp.int32), own, recv)


def _adamw_math(w, g, m, v):
    m = ADAM_B1 * m + (1.0 - ADAM_B1) * g
    v = ADAM_B2 * v + (1.0 - ADAM_B2) * (g * g)
    m_hat = m / (1.0 - ADAM_B1 ** ADAM_STEP)
    v_hat = v / (1.0 - ADAM_B2 ** ADAM_STEP)
    delta = -ADAM_LR * (m_hat / (jnp.sqrt(v_hat) + ADAM_EPS) + ADAM_WD * w)
    return delta, m, v


def _adamw(name, w, gs, m, v):
    nl, r, cols = w.shape
    tr = _pick(r, LANES)

    def body(*refs):
        w_ref, m_ref, v_ref = refs[0:3]
        g_refs = refs[3:3 + nl]
        go_ref, d_ref, nm_ref, nv_ref = refs[3 + nl:]
        layer = pl.program_id(0)
        g = g_refs[0][...]
        for j in range(1, nl):
            g = jnp.where(layer == j, g_refs[j][...], g)
        d, nm, nv = _adamw_math(w_ref[...], g, m_ref[...], v_ref[...])
        go_ref[...] = g
        d_ref[...] = d
        nm_ref[...] = nm
        nv_ref[...] = nv

    spec3 = pl.BlockSpec((None, tr, cols), lambda l, i: (l, i, 0))
    gspec = pl.BlockSpec((tr, cols), lambda l, i: (i, 0))
    out = jax.ShapeDtypeStruct((nl, r, cols), F32)
    return pl.pallas_call(
        body, name=name, grid=(nl, r // tr), in_specs=[spec3] * 3 + [gspec] * nl, out_specs=[spec3] * 4,
        out_shape=[out] * 4, compiler_params=_cp(("parallel", "parallel")))(w, m, v, *gs)


def _adamw_small(name, groups):
    n = len(groups)
    flat = [a for grp in groups for a in grp]

    def body(*refs):
        ins, outs = refs[:4 * n], refs[4 * n:]
        for p in range(n):
            w_ref, g_ref, m_ref, v_ref = ins[4 * p:4 * p + 4]
            d, nm, nv = _adamw_math(w_ref[...], g_ref[...], m_ref[...], v_ref[...])
            outs[3 * p][...] = d
            outs[3 * p + 1][...] = nm
            outs[3 * p + 2][...] = nv

    vm = pl.BlockSpec(memory_space=pltpu.VMEM)
    out_shape = [jax.ShapeDtypeStruct(grp[0].shape, F32) for grp in groups for _ in range(3)]
    res = pl.pallas_call(
        body, name=name, in_specs=[vm] * (4 * n), out_specs=[vm] * (3 * n), out_shape=out_shape)(*flat)
    return [tuple(res[3 * p:3 * p + 3]) for p in range(n)]


def _block_diag_pairs(w):
    h, d, _ = w.shape
    z = jnp.zeros((h // 2, d, d), w.dtype)
    top = jnp.concatenate([w[0::2], z], axis=2)
    bot = jnp.concatenate([z, w[1::2]], axis=2)
    return jnp.concatenate([top, bot], axis=1).astype(BF16)


def _diag_pairs_to_heads(g, d):
    a = g[:, :d, :d]
    b = g[:, d:, d:]
    return jnp.stack([a, b], axis=1).reshape(-1, d, d)


def _rows128(a):
    flat = a.reshape(-1, LANES)
    pad = (-flat.shape[0]) % SUBLANES
    if pad:
        flat = jnp.concatenate([flat, jnp.zeros((pad, LANES), flat.dtype)], axis=0)
    return flat


def _unshard_last(g4, shape):
    g4 = g4.reshape((N_CHIPS,) + tuple(shape))
    return jnp.concatenate([g4[j] for j in range(N_CHIPS)], axis=-1)


def kernel(x, norm_gains, hyb_w_in, hyb_conv_a, hyb_conv_b, hyb_conv_b_bias, hyb_rg_w_a, hyb_rg_b_a, hyb_rg_w_x, hyb_rg_b_x, hyb_rg_lambda, hyb_w_out, sb_w_qkv, sb_w_o, mlp_w_up, mlp_w_down, loss_target, m_norm_gains, m_hyb_w_in, m_hyb_conv_a, m_hyb_conv_b, m_hyb_conv_b_bias, m_hyb_rg_w_a, m_hyb_rg_b_a, m_hyb_rg_w_x, m_hyb_rg_b_x, m_hyb_rg_lambda, m_hyb_w_out, m_sb_w_qkv, m_sb_w_o, m_mlp_w_up, m_mlp_w_down, v_norm_gains, v_hyb_w_in, v_hyb_conv_a, v_hyb_conv_b, v_hyb_conv_b_bias, v_hyb_rg_w_a, v_hyb_rg_b_a, v_hyb_rg_w_x, v_hyb_rg_b_x, v_hyb_rg_lambda, v_hyb_w_out, v_sb_w_qkv, v_sb_w_o, v_mlp_w_up, v_mlp_w_down):
    cx_ = lax.axis_index("x")
    cy_ = lax.axis_index("y")
    cc_ = lax.axis_index("c")
    chip = 2 * cx_ + cy_

    x0 = x[0]
    target = loss_target[0]
    s, d = x0.shape
    heads = SB_HEADS
    assert d // heads == LANES
    n_rg, hd = hyb_rg_w_a.shape[1], hyb_rg_w_a.shape[2]
    wmix = n_rg * hd
    assert 2 * hd == LANES

    big = {
        "hyb_w_in": hyb_w_in[0], "hyb_w_out": hyb_w_out[0], "sb_w_qkv": sb_w_qkv[0], "sb_w_o": sb_w_o[0],
        "mlp_w_up0": mlp_w_up[0], "mlp_w_down0": mlp_w_down[0],
        "mlp_w_up1": mlp_w_up[1], "mlp_w_down1": mlp_w_down[1],
    }
    names = list(big)
    slots = [_cast_into_slot("cast_" + k, big[k], chip) for k in names]
    full = dict(zip(names, _allgather_weights("allgather_weights", slots)))
    rowsharded = lambda k: full[k].reshape(-1, full[k].shape[2])

    ng_s, ca_s, cb_s = norm_gains.reshape(-1, norm_gains.shape[2]), hyb_conv_a[0], hyb_conv_b[0]
    packed = jnp.concatenate([_rows128(ng_s), _rows128(ca_s), _rows128(cb_s)], axis=0)
    gathered = _allgather_chips_small("allgather_small", packed)
    n0 = ng_s.size // LANES
    n1 = n0 + (-n0) % SUBLANES
    m0 = ca_s.size // LANES
    m1 = m0 + (-m0) % SUBLANES
    k0 = cb_s.size // LANES
    gains = _unshard_last(gathered[:, 0:n0], ng_s.shape).reshape(2, 4, 1, d)
    conv_a = _unshard_last(gathered[:, n1:n1 + m0], ca_s.shape)
    conv_b = _unshard_last(gathered[:, n1 + m1:n1 + m1 + k0], cb_s.shape)
    bias, b_a, b_x, lam = hyb_conv_b_bias, hyb_rg_b_a, hyb_rg_b_x, hyb_rg_lambda
    wa_blk = _block_diag_pairs(hyb_rg_w_a[0])
    wx_blk = _block_diag_pairs(hyb_rg_w_x[0])

    relu_sq = lambda acc: (jnp.maximum(acc, 0.0), jnp.square(jnp.maximum(acc, 0.0)))

    h1 = _rms_fwd("rms_pre0", x0, gains[0, 0])
    proj = _mm_fwd_col("proj_in", h1, full["hyb_w_in"])[0]
    ycat, hseq = _mixer_fwd(proj, conv_a, conv_b, bias, wa_blk, b_a, wx_blk, b_x, lam)
    mix0 = _mm_fwd_row("proj_out", ycat, rowsharded("hyb_w_out"))
    x1, h2 = _rms_post("rms_mix0", mix0, gains[0, 1], x0, gains[0, 2])
    u0, a0 = _mm_fwd_col("mlp_up0", h2, full["mlp_w_up0"], (BF16, BF16), relu_sq)
    mlp0 = _mm_fwd_row("mlp_down0", a0, rowsharded("mlp_w_down0"))
    x2, h3 = _rms_post("rms_mlp0", mlp0, gains[0, 3], x1, gains[1, 0])

    qkv = _mm_fwd_col("qkv", h3, full["sb_w_qkv"], (BF16,))[0]
    att, tot = _attn_fwd(qkv, heads)
    mix1 = _mm_fwd_row("attn_out", att, rowsharded("sb_w_o"))
    x3, h4 = _rms_post("rms_mix1", mix1, gains[1, 1], x2, gains[1, 2])
    u1, a1 = _mm_fwd_col("mlp_up1", h4, full["mlp_w_up1"], (BF16, BF16), relu_sq)
    mlp1 = _mm_fwd_row("mlp_down1", a1, rowsharded("mlp_w_down1"))
    (x4,) = _rms_post("rms_mlp1", mlp1, gains[1, 3], x3)

    dy, loss_local = _loss_head("loss_head", x4, target)
    loss = lax.psum(loss_local, ("x", "y", "c"))

    grads_big = {}
    dgain = [[None] * 4 for _ in range(2)]
    drelu = lambda acc, u: (acc * (2.0 * u.astype(F32)),)

    def mlp_bwd(layer, dxo, mlp_out, xin, hin, u, a):
        dmlp, dgain[layer][3] = _rms_bwd(f"rms_mlp{layer}_bwd", mlp_out, gains[layer, 3], dxo, out_dtype=BF16)
        wd, wu = rowsharded(f"mlp_w_down{layer}"), full[f"mlp_w_up{layer}"]
        grads_big[f"mlp_w_down{layer}"] = _mm_wgrad_row(f"mlp_down{layer}_wgrad", a, dmlp).reshape(
            N_CHIPS, -1, d)
        du = _mm_bwd_row(f"mlp_down{layer}_bwd", dmlp, wd, (BF16,), u, drelu)[0]
        grads_big[f"mlp_w_up{layer}"] = _mm_wgrad_col(f"mlp_up{layer}_wgrad", hin, du, wu.shape[2])
        dh = _mm_bwd_col(f"mlp_up{layer}_bwd", du, wu)
        dxm, dgain[layer][2] = _rms_bwd(f"rms_premlp{layer}_bwd", xin, gains[layer, 2], dh, res=dxo)
        return dxm

    dx3 = mlp_bwd(1, dy, mlp1, x3, h4, u1, a1)
    dmix1, dgain[1][1] = _rms_bwd("rms_mix1_bwd", mix1, gains[1, 1], dx3, out_dtype=BF16)
    grads_big["sb_w_o"] = _mm_wgrad_row("attn_out_wgrad", att, dmix1).reshape(N_CHIPS, -1, d)
    datt = _mm_bwd_row("attn_out_bwd", dmix1, rowsharded("sb_w_o"), (BF16,))[0]
    dq, dk, dv = _attn_bwd(qkv, tot, datt, heads)
    dqkv = jnp.concatenate([dq, dk, dv], axis=1)
    grads_big["sb_w_qkv"] = _mm_wgrad_col("qkv_wgrad", h3, dqkv, full["sb_w_qkv"].shape[2])
    dh3 = _mm_bwd_col("qkv_bwd", dqkv, full["sb_w_qkv"])
    dx2, dgain[1][0] = _rms_bwd("rms_pre1_bwd", x2, gains[1, 0], dh3, res=dx3)

    dx1 = mlp_bwd(0, dx2, mlp0, x1, h2, u0, a0)
    dmix0, dgain[0][1] = _rms_bwd("rms_mix0_bwd", mix0, gains[0, 1], dx1, out_dtype=BF16)
    grads_big["hyb_w_out"] = _mm_wgrad_row("proj_out_wgrad", ycat, dmix0).reshape(N_CHIPS, -1, d)
    dycat = _mm_bwd_row("proj_out_bwd", dmix0, rowsharded("hyb_w_out"))[0]
    dproj, xr_b, dpa_b, dpx_b, sg = _mixer_bwd(
        proj, hseq, dycat, conv_a, conv_b, bias, wa_blk, b_a, wx_blk, b_x, lam)
    grads_big["hyb_w_in"] = _mm_wgrad_col("proj_in_wgrad", h1, dproj, full["hyb_w_in"].shape[2])
    dh1 = _mm_bwd_col("proj_in_bwd", dproj, full["hyb_w_in"])
    dx0, dgain[0][0] = _rms_bwd("rms_pre0_bwd", x0, gains[0, 0], dh1, res=dx1)
    dwa = _diag_pairs_to_heads(_mm_wgrad_diag("rg_w_a_wgrad", xr_b, dpa_b), hd)
    dwx = _diag_pairs_to_heads(_mm_wgrad_diag("rg_w_x_wgrad", xr_b, dpx_b), hd)

    dgains = jnp.concatenate([dgain[l][k] for l in range(2) for k in range(4)], axis=0)
    small_parts = [dgains, sg[_SG_CONV_A:_SG_CONV_A + 3], sg[_SG_CONV_B:_SG_CONV_B + 4], sg[_SG_BIAS:_SG_BIAS + 1],
                   dwa, sg[_SG_BA:_SG_BA + 1], dwx, sg[_SG_BX:_SG_BX + 1], sg[_SG_LAM:_SG_LAM + 1]]
    small_rows = [_rows128(p) for p in small_parts]
    reduced = _allreduce_small("allreduce_small", jnp.concatenate(small_rows, axis=0))
    small_full, off = [], 0
    for p, rws in zip(small_parts, small_rows):
        small_full.append(reduced[off:off + p.size // LANES].reshape(p.shape))
        off += rws.shape[0]
    g_gains, g_ca, g_cb, g_bias, g_wa, g_ba, g_wx, g_bx, g_lam = small_full

    def my_cols(g, width):
        return lax.dynamic_slice_in_dim(g, chip * width, width, axis=g.ndim - 1)

    small = [
        ("norm_gains", norm_gains, my_cols(g_gains, norm_gains.shape[2]).reshape(norm_gains.shape),
         m_norm_gains, v_norm_gains),
        ("hyb_conv_a", hyb_conv_a, my_cols(g_ca, hyb_conv_a.shape[2])[None], m_hyb_conv_a, v_hyb_conv_a),
        ("hyb_conv_b", hyb_conv_b, my_cols(g_cb, hyb_conv_b.shape[2])[None], m_hyb_conv_b, v_hyb_conv_b),
        ("hyb_conv_b_bias", hyb_conv_b_bias, g_bias, m_hyb_conv_b_bias, v_hyb_conv_b_bias),
        ("hyb_rg_w_a", hyb_rg_w_a, g_wa[None], m_hyb_rg_w_a, v_hyb_rg_w_a),
        ("hyb_rg_b_a", hyb_rg_b_a, g_ba, m_hyb_rg_b_a, v_hyb_rg_b_a),
        ("hyb_rg_w_x", hyb_rg_w_x, g_wx[None], m_hyb_rg_w_x, v_hyb_rg_w_x),
        ("hyb_rg_b_x", hyb_rg_b_x, g_bx, m_hyb_rg_b_x, v_hyb_rg_b_x),
        ("hyb_rg_lambda", hyb_rg_lambda, g_lam, m_hyb_rg_lambda, v_hyb_rg_lambda),
    ]
    to2d = lambda a: a.reshape(-1, a.shape[-1])
    small_res = _adamw_small("adamw_small", [tuple(to2d(a) for a in (w, g, m, v)) for _, w, g, m, v in small])
    out = {}
    for (nm, w, g, _, _), (dl, nmom, nvar) in zip(small, small_res):
        out[nm] = (g, dl.reshape(w.shape), nmom.reshape(w.shape), nvar.reshape(w.shape))

    slabs = [grads_big[k] for k in names]
    recv_sib = _exchange_sibling_halves("grads_to_sibling", slabs)
    chip_part = [_add_sibling("grads_add_" + k, sl, rv, cc_) for k, sl, rv in zip(names, slabs, recv_sib)]
    recv_chip = _exchange_chips("grads_to_chips", chip_part)
    halves = [_sum_chips("grads_sum_" + k, own, rc, chip, cc_) for k, own, rc in zip(names, chip_part, recv_chip)]
    gfull = dict(zip(names, _join_sibling_halves("grads_join", halves)))

    stacked = {
        "hyb_w_in": (hyb_w_in, m_hyb_w_in, v_hyb_w_in, ["hyb_w_in"]),
        "hyb_w_out": (hyb_w_out, m_hyb_w_out, v_hyb_w_out, ["hyb_w_out"]),
        "sb_w_qkv": (sb_w_qkv, m_sb_w_qkv, v_sb_w_qkv, ["sb_w_qkv"]),
        "sb_w_o": (sb_w_o, m_sb_w_o, v_sb_w_o, ["sb_w_o"]),
        "mlp_w_up": (mlp_w_up, m_mlp_w_up, v_mlp_w_up, ["mlp_w_up0", "mlp_w_up1"]),
        "mlp_w_down": (mlp_w_down, m_mlp_w_down, v_mlp_w_down, ["mlp_w_down0", "mlp_w_down1"]),
    }
    for k, (w, m, v, parts) in stacked.items():
        out[k] = tuple(_adamw("adamw_" + k, w, [gfull[p] for p in parts], m, v))

    order = ["norm_gains", "hyb_w_in", "hyb_conv_a", "hyb_conv_b", "hyb_conv_b_bias", "hyb_rg_w_a", "hyb_rg_b_a",
             "hyb_rg_w_x", "hyb_rg_b_x", "hyb_rg_lambda", "hyb_w_out", "sb_w_qkv", "sb_w_o", "mlp_w_up",
             "mlp_w_down"]
    return (loss, dx0[None], *[out[k][0] for k in order], *[out[k][1] for k in order],
            *[out[k][2] for k in order], *[out[k][3] for k in order])
```

```python
import functools
import math

import jax
import jax.numpy as jnp
from jax import lax
from jax.experimental import pallas as pl
from jax.experimental.pallas import tpu as pltpu
from jax.experimental.pallas import tpu_sc as plsc

F32 = jnp.float32
BF16 = jnp.bfloat16
MESH = pl.DeviceIdType.MESH

SB_HEADS = 16
NORM_EPS = 1e-6
LRU_C = 8.0
ADAM_LR = 0.001
ADAM_B1 = 0.9
ADAM_B2 = 0.999
ADAM_EPS = 1e-08
ADAM_WD = 0.01
ADAM_STEP = 10

LANES = 128
SUBLANES = 8
VMEM_LIMIT = 48 * 1024 * 1024
MM_TILE = 1024
MM_TILE_K = 2048
ROW_TILE = 256
ATT_TILE = 512
ATT_HEADS_PER_STEP = 2
N_CHIPS = 4
N_DEV = 8

_DIMS = {
    "nn": (((1,), (0,)), ((), ())),
    "nt": (((1,), (1,)), ((), ())),
    "tn": (((0,), (0,)), ((), ())),
}


def _cp(sem=None, vmem=VMEM_LIMIT):
    return pltpu.CompilerParams(dimension_semantics=sem, vmem_limit_bytes=vmem)


def _pick(dim, pref):
    t = min(dim, pref)
    while dim % t:
        t -= LANES
    return t


def _whole(shape):
    nd = len(shape)
    return pl.BlockSpec(tuple(shape), lambda *_: (0,) * nd)


def _sigmoid(z):
    return 1.0 / (1.0 + jnp.exp(-z))


def _log_sigmoid(z):
    return jnp.minimum(z, 0.0) - jnp.log(1.0 + jnp.exp(-jnp.abs(z)))


def _expm1(z):
    series = z * (1.0 + z * (0.5 + z * (1.0 / 6.0 + z * (1.0 / 24.0))))
    return jnp.where(jnp.abs(z) < 0.05, series, jnp.exp(z) - 1.0)


_GELU_C = math.sqrt(2.0 / math.pi)


def _gelu_and_grad(g):
    inner = _GELU_C * (g + 0.044715 * g * g * g)
    t = jnp.tanh(inner)
    val = 0.5 * g * (1.0 + t)
    grad = 0.5 * (1.0 + t) + 0.5 * g * (1.0 - t * t) * _GELU_C * (1.0 + 3.0 * 0.044715 * g * g)
    return val, grad


def _shift_down(cur, prev8, k, rows):
    n = cur.shape[0]
    rolled = pltpu.roll(cur, k, 0)
    head = jnp.tile(pltpu.roll(prev8, k, 0), (n // SUBLANES, 1))
    return jnp.where(rows < k, head, rolled)


def _shift_up(cur, next8, k, rows):
    n = cur.shape[0]
    rolled = pltpu.roll(cur, n - k, 0)
    tail = jnp.tile(pltpu.roll(next8, SUBLANES - k, 0), (n // SUBLANES, 1))
    return jnp.where(rows >= n - k, tail, rolled)


def _colsum(v):
    return jnp.sum(v, axis=0, keepdims=True)


def _matmul(name, mode, grid, operands, in_specs, out_shapes, out_specs, acc_shape, epilogue=None):
    nk = grid[2]
    n_in = len(operands)
    dims = _DIMS[mode]

    def finish(acc, extra, outs):
        res = epilogue(acc, *[e[...] for e in extra]) if epilogue is not None else (acc,)
        for o_ref, o in zip(outs, res):
            o_ref[...] = o.astype(o_ref.dtype)

    def product(a_ref, b_ref):
        return lax.dot_general(a_ref[...].astype(BF16), b_ref[...].astype(BF16), dims, preferred_element_type=F32)

    def body_single(*refs):
        finish(product(refs[0], refs[1]), refs[2:n_in], refs[n_in:])

    def body(*refs):
        extra = refs[2:n_in]
        outs = refs[n_in:-1]
        acc_ref = refs[-1]
        k = pl.program_id(2)

        @pl.when(k == 0)
        def _():
            acc_ref[...] = product(refs[0], refs[1])

        @pl.when(k > 0)
        def _():
            acc_ref[...] += product(refs[0], refs[1])

        @pl.when(k == nk - 1)
        def _():
            finish(acc_ref[...], extra, outs)

    return pl.pallas_call(
        body_single if nk == 1 else body, name=name, grid=grid, in_specs=in_specs, out_specs=out_specs,
        out_shape=out_shapes, scratch_shapes=[] if nk == 1 else [pltpu.VMEM(acc_shape, F32)],
        compiler_params=_cp(("parallel", "parallel", "arbitrary")),
    )(*operands)


def _mm_fwd_col(name, a, wfull, out_dtypes=(F32,), epilogue=None):
    s, kdim = a.shape
    _, _, cs = wfull.shape
    tm, tk, tn = _pick(s, MM_TILE), _pick(kdim, MM_TILE_K), _pick(cs, MM_TILE)
    nbj = cs // tn
    grid = (s // tm, N_CHIPS * nbj, kdim // tk)
    out_shapes = [jax.ShapeDtypeStruct((s, N_CHIPS * cs), dt) for dt in out_dtypes]
    out_specs = [pl.BlockSpec((tm, tn), lambda i, n, k: (i, n)) for _ in out_dtypes]
    return _matmul(
        name, "nn", grid, [a, wfull],
        [pl.BlockSpec((tm, tk), lambda i, n, k: (i, k)),
         pl.BlockSpec((None, tk, tn), lambda i, n, k: (n // nbj, k, n % nbj))],
        out_shapes, out_specs, (tm, tn), epilogue)


def _mm_fwd_row(name, a, w2d, out_dtype=F32):
    s, kdim = a.shape
    _, n_out = w2d.shape
    tm, tk, tn = _pick(s, MM_TILE), _pick(kdim, MM_TILE_K), _pick(n_out, MM_TILE)
    grid = (s // tm, n_out // tn, kdim // tk)
    return _matmul(
        name, "nn", grid, [a, w2d],
        [pl.BlockSpec((tm, tk), lambda i, n, k: (i, k)),
         pl.BlockSpec((tk, tn), lambda i, n, k: (k, n))],
        [jax.ShapeDtypeStruct((s, n_out), out_dtype)],
        [pl.BlockSpec((tm, tn), lambda i, n, k: (i, n))], (tm, tn))[0]


def _mm_bwd_col(name, dy, wfull, out_dtype=F32):
    s, _ = dy.shape
    _, kdim, cs = wfull.shape
    tm, tn, tk = _pick(s, MM_TILE), _pick(kdim, MM_TILE), _pick(cs, MM_TILE_K)
    nbj = cs // tk
    grid = (s // tm, kdim // tn, N_CHIPS * nbj)
    return _matmul(
        name, "nt", grid, [dy, wfull],
        [pl.BlockSpec((tm, tk), lambda i, n, k: (i, k)),
         pl.BlockSpec((None, tn, tk), lambda i, n, k: (k // nbj, n, k % nbj))],
        [jax.ShapeDtypeStruct((s, kdim), out_dtype)],
        [pl.BlockSpec((tm, tn), lambda i, n, k: (i, n))], (tm, tn))[0]


def _mm_bwd_row(name, dy, w2d, out_dtypes=(F32,), extra=None, epilogue=None):
    s, n_in = dy.shape
    kdim, _ = w2d.shape
    tm, tn, tk = _pick(s, MM_TILE), _pick(kdim, MM_TILE), _pick(n_in, MM_TILE_K)
    grid = (s // tm, kdim // tn, n_in // tk)
    operands = [dy, w2d]
    in_specs = [pl.BlockSpec((tm, tk), lambda i, n, k: (i, k)),
                pl.BlockSpec((tn, tk), lambda i, n, k: (n, k))]
    if extra is not None:
        operands.append(extra)
        in_specs.append(pl.BlockSpec((tm, tn), lambda i, n, k: (i, n)))
    return _matmul(
        name, "nt", grid, operands, in_specs,
        [jax.ShapeDtypeStruct((s, kdim), dt) for dt in out_dtypes],
        [pl.BlockSpec((tm, tn), lambda i, n, k: (i, n)) for _ in out_dtypes], (tm, tn), epilogue)


def _mm_wgrad_col(name, a, dy, cs):
    s, kdim = a.shape
    tm, tn, ts = _pick(kdim, MM_TILE), _pick(cs, MM_TILE), _pick(s, MM_TILE_K)
    nbj = cs // tn
    grid = (kdim // tm, N_CHIPS * nbj, s // ts)
    return _matmul(
        name, "tn", grid, [a, dy],
        [pl.BlockSpec((ts, tm), lambda i, n, k: (k, i)),
         pl.BlockSpec((ts, tn), lambda i, n, k: (k, n))],
        [jax.ShapeDtypeStruct((N_CHIPS, kdim, cs), BF16)],
        [pl.BlockSpec((None, tm, tn), lambda i, n, k: (n // nbj, i, n % nbj))], (tm, tn))[0]


def _mm_wgrad_row(name, a, dy):
    s, kdim = a.shape
    _, n_out = dy.shape
    tm, tn, ts = _pick(kdim, MM_TILE), _pick(n_out, MM_TILE), _pick(s, MM_TILE_K)
    grid = (kdim // tm, n_out // tn, s // ts)
    return _matmul(
        name, "tn", grid, [a, dy],
        [pl.BlockSpec((ts, tm), lambda i, n, k: (k, i)),
         pl.BlockSpec((ts, tn), lambda i, n, k: (k, n))],
        [jax.ShapeDtypeStruct((kdim, n_out), BF16)],
        [pl.BlockSpec((tm, tn), lambda i, n, k: (i, n))], (tm, tn))[0]


def _mm_wgrad_diag(name, a, dy):
    s, width = a.shape
    nb = width // LANES
    ts = _pick(s, MM_TILE)
    grid = (nb, 1, s // ts)
    return _matmul(
        name, "tn", grid, [a, dy],
        [pl.BlockSpec((ts, LANES), lambda i, n, k: (k, i)),
         pl.BlockSpec((ts, LANES), lambda i, n, k: (k, i))],
        [jax.ShapeDtypeStruct((nb, LANES, LANES), F32)],
        [pl.BlockSpec((None, LANES, LANES), lambda i, n, k: (i, 0, 0))], (LANES, LANES))[0]


def _rowspec(tr, d):
    return pl.BlockSpec((tr, d), lambda i: (i, 0))


def _vecspec(d):
    return pl.BlockSpec((1, d), lambda i: (0, 0))


def _rms(x, g):
    return x * lax.rsqrt(jnp.mean(x * x, axis=-1, keepdims=True) + NORM_EPS) * g


def _cast_into_slot(name, w, chip):
    r, c = w.shape
    tr = _pick(r, ROW_TILE)

    def body(chip_ref, w_ref, o_ref):
        o_ref[...] = w_ref[...].astype(BF16)

    grid_spec = pltpu.PrefetchScalarGridSpec(
        num_scalar_prefetch=1, grid=(r // tr,),
        in_specs=[pl.BlockSpec((tr, c), lambda i, chip_ref: (i, 0))],
        out_specs=pl.BlockSpec((None, tr, c), lambda i, chip_ref: (chip_ref[0], i, 0)))
    return pl.pallas_call(
        body, name=name, grid_spec=grid_spec, out_shape=jax.ShapeDtypeStruct((N_CHIPS, r, c), BF16),
        compiler_params=_cp(("parallel",)))(jnp.reshape(chip, (1,)).astype(jnp.int32), w)


def _rms_fwd(name, x, g):
    s, d = x.shape
    tr = _pick(s, ROW_TILE)

    def body(x_ref, g_ref, h_ref):
        h_ref[...] = _rms(x_ref[...], g_ref[...]).astype(BF16)

    return pl.pallas_call(
        body, name=name, grid=(s // tr,), in_specs=[_rowspec(tr, d), _vecspec(d)],
        out_specs=_rowspec(tr, d), out_shape=jax.ShapeDtypeStruct((s, d), BF16),
        compiler_params=_cp(("parallel",)))(x, g)


def _rms_post(name, y, g_post, res, g_next=None):
    s, d = y.shape
    tr = _pick(s, ROW_TILE)
    with_next = g_next is not None

    def body(*refs):
        if with_next:
            y_ref, gp_ref, r_ref, gn_ref, x_ref, h_ref = refs
        else:
            y_ref, gp_ref, r_ref, x_ref = refs
        xn = r_ref[...] + _rms(y_ref[...], gp_ref[...])
        x_ref[...] = xn
        if with_next:
            h_ref[...] = _rms(xn, gn_ref[...]).astype(BF16)

    operands = [y, g_post, res] + ([g_next] if with_next else [])
    in_specs = [_rowspec(tr, d), _vecspec(d), _rowspec(tr, d)] + ([_vecspec(d)] if with_next else [])
    out_shape = [jax.ShapeDtypeStruct((s, d), F32)] + ([jax.ShapeDtypeStruct((s, d), BF16)] if with_next else [])
    out_specs = [_rowspec(tr, d)] + ([_rowspec(tr, d)] if with_next else [])
    return pl.pallas_call(
        body, name=name, grid=(s // tr,), in_specs=in_specs, out_specs=out_specs, out_shape=out_shape,
        compiler_params=_cp(("parallel",)))(*operands)


def _rms_bwd(name, x, g, dy, res=None, out_dtype=F32):
    s, d = x.shape
    tr = _pick(s, ROW_TILE)
    nsteps = s // tr
    with_res = res is not None

    def body(*refs):
        if with_res:
            x_ref, g_ref, dy_ref, r_ref, dx_ref, dg_ref, acc_ref = refs
        else:
            x_ref, g_ref, dy_ref, dx_ref, dg_ref, acc_ref = refs
        i = pl.program_id(0)

        @pl.when(i == 0)
        def _():
            acc_ref[...] = jnp.zeros_like(acc_ref)

        xv = x_ref[...]
        dyv = dy_ref[...].astype(F32)
        r = lax.rsqrt(jnp.mean(xv * xv, axis=-1, keepdims=True) + NORM_EPS)
        xhat = xv * r
        gy = dyv * g_ref[...]
        dx = r * (gy - xhat * jnp.mean(gy * xhat, axis=-1, keepdims=True))
        if with_res:
            dx = dx + r_ref[...]
        dx_ref[...] = dx.astype(dx_ref.dtype)
        acc_ref[...] += jnp.sum((dyv * xhat).reshape(tr // SUBLANES, SUBLANES, d), axis=0)

        @pl.when(i == nsteps - 1)
        def _():
            dg_ref[...] = jnp.broadcast_to(_colsum(acc_ref[...]), (SUBLANES, d))

    operands = [x, g, dy] + ([res] if with_res else [])
    in_specs = [_rowspec(tr, d), _vecspec(d), _rowspec(tr, d)] + ([_rowspec(tr, d)] if with_res else [])
    dx, dg = pl.pallas_call(
        body, name=name, grid=(nsteps,), in_specs=in_specs,
        out_specs=[_rowspec(tr, d), pl.BlockSpec((SUBLANES, d), lambda i: (0, 0))],
        out_shape=[jax.ShapeDtypeStruct((s, d), out_dtype), jax.ShapeDtypeStruct((SUBLANES, d), F32)],
        scratch_shapes=[pltpu.VMEM((SUBLANES, d), F32)],
        compiler_params=_cp(("arbitrary",)))(*operands)
    return dx, dg[0:1]


def _loss_head(name, y, target):
    s, d = y.shape
    tr = _pick(s, ROW_TILE)
    nsteps = s // tr

    def body(y_ref, t_ref, dy_ref, l_ref, acc_ref):
        i = pl.program_id(0)

        @pl.when(i == 0)
        def _():
            acc_ref[...] = jnp.zeros_like(acc_ref)

        err = y_ref[...] - t_ref[...]
        dy_ref[...] = err * (1.0 / d)
        acc_ref[...] += jnp.sum((err * err).reshape(tr // SUBLANES, SUBLANES, d), axis=0)

        @pl.when(i == nsteps - 1)
        def _():
            l_ref[...] = jnp.full((SUBLANES, LANES), (0.5 / d) * jnp.sum(acc_ref[...]), F32)

    dy, l = pl.pallas_call(
        body, name=name, grid=(nsteps,), in_specs=[_rowspec(tr, d), _rowspec(tr, d)],
        out_specs=[_rowspec(tr, d), pl.BlockSpec((SUBLANES, LANES), lambda i: (0, 0))],
        out_shape=[jax.ShapeDtypeStruct((s, d), F32), jax.ShapeDtypeStruct((SUBLANES, LANES), F32)],
        scratch_shapes=[pltpu.VMEM((SUBLANES, d), F32)],
        compiler_params=_cp(("arbitrary",)))(y, target)
    return dy, l[0, 0]


def _gates(xr, wa, ba, wx, bx, lam):
    xb = xr.astype(BF16)
    r = _sigmoid(jnp.dot(xb, wa, preferred_element_type=F32) + ba)
    i = _sigmoid(jnp.dot(xb, wx, preferred_element_type=F32) + bx)
    log_a = LRU_C * r * _log_sigmoid(lam)
    a = jnp.exp(log_a)
    m = jnp.sqrt(-_expm1(2.0 * log_a))
    return r, i, a, m


def _mixer_fwd(proj, conv_a, conv_b, bias, wa_blk, ba, wx_blk, bx, lam):
    s, w5 = proj.shape
    w = w5 // 5
    nch = w // LANES
    ts = _pick(s, ROW_TILE)
    nt = s // ts

    def body(p_ref, pp_ref, ca_ref, cb_ref, bias_ref, wa_ref, ba_ref, wx_ref, bx_ref, lam_ref,
             y_ref, h_ref, a_scr, b_scr, hc_scr):
        t = pl.program_id(0)
        first = t == 0
        rows = lax.broadcasted_iota(jnp.int32, (ts, LANES), 0)

        @pl.when(first)
        def _():
            hc_scr[...] = jnp.zeros_like(hc_scr)

        def cur(comp, c):
            return p_ref[:, comp * w + c * LANES:comp * w + (c + 1) * LANES]

        def prev(comp, c):
            v = pp_ref[:, comp * w + c * LANES:comp * w + (c + 1) * LANES]
            return jnp.where(first, 0.0, v)

        for c in range(nch):
            sl = slice(c * LANES, (c + 1) * LANES)
            cx = cur(1, c) * cur(2, c)
            cxp = prev(1, c) * prev(2, c)
            wa3 = ca_ref[:, sl]
            conv = (wa3[2:3] * cx + wa3[1:2] * _shift_down(cx, cxp, 1, rows)
                    + wa3[0:1] * _shift_down(cx, cxp, 2, rows))
            y_ref[:, sl] = (cur(0, c) * conv).astype(BF16)

        for c in range(nch):
            sl = slice(c * LANES, (c + 1) * LANES)
            xb, xbp = cur(4, c), prev(4, c)
            wb4 = cb_ref[:, sl]
            xr = (wb4[3:4] * xb + wb4[2:3] * _shift_down(xb, xbp, 1, rows)
                  + wb4[1:2] * _shift_down(xb, xbp, 2, rows)
                  + wb4[0:1] * _shift_down(xb, xbp, 3, rows) + bias_ref[:, sl])
            _, i, a, m = _gates(xr, wa_ref[c], ba_ref[:, sl], wx_ref[c], bx_ref[:, sl], lam_ref[:, sl])
            a_scr[:, sl] = a
            b_scr[:, sl] = m * i * xr

        def step(r, h):
            h = a_scr[pl.ds(r, 1), :] * h + b_scr[pl.ds(r, 1), :]
            h_ref[pl.ds(r, 1), :] = h
            return h

        hc_scr[0:1, :] = lax.fori_loop(0, ts, step, hc_scr[0:1, :], unroll=8)

        for c in range(nch):
            sl = slice(c * LANES, (c + 1) * LANES)
            gel, _ = _gelu_and_grad(cur(3, c))
            y_ref[:, w + c * LANES:w + (c + 1) * LANES] = (h_ref[:, sl] * gel).astype(BF16)

    vec = lambda n: _whole((n, w))
    return pl.pallas_call(
        body, name="mixer_fwd", grid=(nt,),
        in_specs=[pl.BlockSpec((ts, w5), lambda t: (t, 0)),
                  pl.BlockSpec((SUBLANES, w5), lambda t: (jnp.maximum(t * (ts // SUBLANES) - 1, 0), 0)),
                  vec(3), vec(4), vec(1), _whole(wa_blk.shape), vec(1), _whole(wx_blk.shape), vec(1), vec(1)],
        out_specs=[pl.BlockSpec((ts, 2 * w), lambda t: (t, 0)), pl.BlockSpec((ts, w), lambda t: (t, 0))],
        out_shape=[jax.ShapeDtypeStruct((s, 2 * w), BF16), jax.ShapeDtypeStruct((s, w), F32)],
        scratch_shapes=[pltpu.VMEM((ts, w), F32), pltpu.VMEM((ts, w), F32), pltpu.VMEM((SUBLANES, w), F32)],
        compiler_params=_cp(("arbitrary",)),
    )(proj, proj, conv_a, conv_b, bias, wa_blk, ba, wx_blk, bx, lam)


_SG_CONV_A, _SG_CONV_B, _SG_BIAS, _SG_BA, _SG_BX, _SG_LAM, _SG_ROWS = 0, 3, 7, 8, 9, 10, 16


def _mixer_bwd(proj, hseq, dy, conv_a, conv_b, bias, wa_blk, ba, wx_blk, bx, lam):
    s, w5 = proj.shape
    w = w5 // 5
    nch = w // LANES
    ts = _pick(s, ROW_TILE)
    nt = s // ts
    tpb = ts // SUBLANES

    def body(p_ref, pp_ref, h_ref, hp_ref, dy_ref, ca_ref, cb_ref, bias_ref, wa_ref, ba_ref, wx_ref, bx_ref,
             lam_ref, dp_ref, xr_ref, dpa_ref, dpx_ref, sg_ref,
             a_scr, g_scr, l_scr, x_scr, r_scr, i_scr, m_scr, cl_scr, cdc_scr, cdx_scr):
        pid = pl.program_id(0)
        last = pid == 0
        first = pid == nt - 1
        rows = lax.broadcasted_iota(jnp.int32, (ts, LANES), 0)

        @pl.when(last)
        def _():
            sg_ref[...] = jnp.zeros_like(sg_ref)
            cl_scr[...] = jnp.zeros_like(cl_scr)
            cdc_scr[...] = jnp.zeros_like(cdc_scr)
            cdx_scr[...] = jnp.zeros_like(cdx_scr)

        def cur(comp, c):
            return p_ref[:, comp * w + c * LANES:comp * w + (c + 1) * LANES]

        def prev(comp, c):
            v = pp_ref[:, comp * w + c * LANES:comp * w + (c + 1) * LANES]
            return jnp.where(first, 0.0, v)

        def put(comp, c, v):
            dp_ref[:, comp * w + c * LANES:comp * w + (c + 1) * LANES] = v

        def acc(row, sl, v):
            sg_ref[row:row + 1, sl] += _colsum(v)

        for c in range(nch):
            sl = slice(c * LANES, (c + 1) * LANES)
            bg, cg, ax = cur(0, c), cur(1, c), cur(2, c)
            cx = cg * ax
            cxp = prev(1, c) * prev(2, c)
            cx1 = _shift_down(cx, cxp, 1, rows)
            cx2 = _shift_down(cx, cxp, 2, rows)
            wa3 = ca_ref[:, sl]
            conv = wa3[2:3] * cx + wa3[1:2] * cx1 + wa3[0:1] * cx2
            dya = dy_ref[:, sl]
            put(0, c, dya * conv)
            dconv = dya * bg
            nxt = cdc_scr[:, sl]
            dcx = (wa3[2:3] * dconv + wa3[1:2] * _shift_up(dconv, nxt, 1, rows)
                   + wa3[0:1] * _shift_up(dconv, nxt, 2, rows))
            cdc_scr[:, sl] = dconv[0:SUBLANES]
            put(1, c, dcx * ax)
            put(2, c, dcx * cg)
            acc(_SG_CONV_A + 2, sl, dconv * cx)
            acc(_SG_CONV_A + 1, sl, dconv * cx1)
            acc(_SG_CONV_A + 0, sl, dconv * cx2)

        for c in range(nch):
            sl = slice(c * LANES, (c + 1) * LANES)
            xb, xbp = cur(4, c), prev(4, c)
            wb4 = cb_ref[:, sl]
            xr = (wb4[3:4] * xb + wb4[2:3] * _shift_down(xb, xbp, 1, rows)
                  + wb4[1:2] * _shift_down(xb, xbp, 2, rows)
                  + wb4[0:1] * _shift_down(xb, xbp, 3, rows) + bias_ref[:, sl])
            r, i, a, m = _gates(xr, wa_ref[c], ba_ref[:, sl], wx_ref[c], bx_ref[:, sl], lam_ref[:, sl])
            gel, dgel = _gelu_and_grad(cur(3, c))
            dyb = dy_ref[:, w + c * LANES:w + (c + 1) * LANES]
            put(3, c, dyb * h_ref[:, sl] * dgel)
            g_scr[:, sl] = dyb * gel
            a_scr[:, sl] = a
            x_scr[:, sl] = xr
            r_scr[:, sl] = r
            i_scr[:, sl] = i
            m_scr[:, sl] = m

        def step(j, carry):
            r = ts - 1 - j
            lam_t = g_scr[pl.ds(r, 1), :] + carry
            l_scr[pl.ds(r, 1), :] = lam_t
            return a_scr[pl.ds(r, 1), :] * lam_t

        cl_scr[0:1, :] = lax.fori_loop(0, ts, step, cl_scr[0:1, :], unroll=8)

        for c in range(nch):
            sl = slice(c * LANES, (c + 1) * LANES)
            lam_t = l_scr[:, sl]
            hprev = _shift_down(h_ref[:, sl], jnp.where(first, 0.0, hp_ref[:, sl]), 1, rows)
            xr, r, i, m, a = x_scr[:, sl], r_scr[:, sl], i_scr[:, sl], m_scr[:, sl], a_scr[:, sl]
            da = lam_t * hprev
            dm = lam_t * i * xr
            di = lam_t * m * xr
            dxr = lam_t * m * i
            dlog_a = da * a - dm * a * a / m
            lam_p = lam_ref[:, sl]
            dr = dlog_a * (LRU_C * _log_sigmoid(lam_p))
            acc(_SG_LAM, sl, dlog_a * r * (LRU_C * _sigmoid(-lam_p)))
            dpa = dr * r * (1.0 - r)
            dpx = di * i * (1.0 - i)
            dpa_b, dpx_b = dpa.astype(BF16), dpx.astype(BF16)
            dxr = (dxr + lax.dot_general(dpa_b, wa_ref[c], _DIMS["nt"], preferred_element_type=F32)
                   + lax.dot_general(dpx_b, wx_ref[c], _DIMS["nt"], preferred_element_type=F32))
            xr_ref[:, sl] = xr.astype(BF16)
            dpa_ref[:, sl] = dpa_b
            dpx_ref[:, sl] = dpx_b
            acc(_SG_BA, sl, dpa)
            acc(_SG_BX, sl, dpx)
            acc(_SG_BIAS, sl, dxr)
            nxt = cdx_scr[:, sl]
            wb4 = cb_ref[:, sl]
            put(4, c, wb4[3:4] * dxr + wb4[2:3] * _shift_up(dxr, nxt, 1, rows)
                + wb4[1:2] * _shift_up(dxr, nxt, 2, rows) + wb4[0:1] * _shift_up(dxr, nxt, 3, rows))
            cdx_scr[:, sl] = dxr[0:SUBLANES]
            xb, xbp = cur(4, c), prev(4, c)
            acc(_SG_CONV_B + 3, sl, dxr * xb)
            acc(_SG_CONV_B + 2, sl, dxr * _shift_down(xb, xbp, 1, rows))
            acc(_SG_CONV_B + 1, sl, dxr * _shift_down(xb, xbp, 2, rows))
            acc(_SG_CONV_B + 0, sl, dxr * _shift_down(xb, xbp, 3, rows))

    blk = lambda width: pl.BlockSpec((ts, width), lambda p: (nt - 1 - p, 0))
    pre = lambda width: pl.BlockSpec(
        (SUBLANES, width), lambda p: (jnp.maximum((nt - 1 - p) * tpb - 1, 0), 0))
    vec = lambda n: _whole((n, w))
    big = lambda: pltpu.VMEM((ts, w), F32)
    small = lambda: pltpu.VMEM((SUBLANES, w), F32)
    return pl.pallas_call(
        body, name="mixer_bwd", grid=(nt,),
        in_specs=[blk(w5), pre(w5), blk(w), pre(w), blk(2 * w),
                  vec(3), vec(4), vec(1), _whole(wa_blk.shape), vec(1), _whole(wx_blk.shape), vec(1), vec(1)],
        out_specs=[blk(w5), blk(w), blk(w), blk(w), _whole((_SG_ROWS, w))],
        out_shape=[jax.ShapeDtypeStruct((s, w5), F32), jax.ShapeDtypeStruct((s, w), BF16),
                   jax.ShapeDtypeStruct((s, w), BF16), jax.ShapeDtypeStruct((s, w), BF16),
                   jax.ShapeDtypeStruct((_SG_ROWS, w), F32)],
        scratch_shapes=[big(), big(), big(), big(), big(), big(), big(), small(), small(), small()],
        compiler_params=_cp(("arbitrary",)),
    )(proj, proj, hseq, hseq, dy, conv_a, conv_b, bias, wa_blk, ba, wx_blk, bx, lam)


def _split_dot(v, tri):
    hi = v.astype(BF16)
    lo = (v - hi.astype(F32)).astype(BF16)
    return (jnp.dot(hi, tri, preferred_element_type=F32) + jnp.dot(lo, tri, preferred_element_type=F32))


def _tri(cmp):
    r = lax.broadcasted_iota(jnp.int32, (LANES, LANES), 0)
    c = lax.broadcasted_iota(jnp.int32, (LANES, LANES), 1)
    return cmp(r, c).astype(BF16)


def _lane_blocks(v):
    return [v[:, b * LANES:(b + 1) * LANES] for b in range(v.shape[1] // LANES)]


def _last_lane(v):
    return jnp.broadcast_to(v[:, LANES - 1:LANES], v.shape)


def _scores(q, kb, scale):
    return lax.dot_general(q, kb, _DIMS["nt"], preferred_element_type=F32) * scale


def _log_gates(z, diagonal):
    ls = jnp.minimum(z, 0.0) - jnp.log(1.0 + jnp.exp(-jnp.abs(z)))
    ln = ls - z
    valid = None
    if diagonal:
        valid = (lax.broadcasted_iota(jnp.int32, z.shape, 1) < lax.broadcasted_iota(jnp.int32, z.shape, 0))
        ln = jnp.where(valid, ln, 0.0)
    return ls, ln, valid


def _attn_fwd(qkv, heads):
    s = qkv.shape[0]
    dh = LANES
    tq = _pick(s, ATT_TILE)
    nq = s // tq
    nb = tq // LANES
    scale = 1.0 / math.sqrt(dh)

    hp = ATT_HEADS_PER_STEP
    groups = heads // hp
    wid = hp * dh

    def body(q_ref, k_ref, v_ref, o_ref, tot_ref, acc_scr, car_scr):
        qi = pl.program_id(1)
        acc_scr[...] = jnp.zeros_like(acc_scr)
        car_scr[...] = jnp.zeros_like(car_scr)
        tri = _tri(lambda r, c: r > c)

        def tile(kt, diagonal):
            k0 = pl.multiple_of(kt * tq, tq)
            heads_cols = [slice(hh * dh, (hh + 1) * dh) for hh in range(hp)]
            zs = [_scores(q_ref[:, cols], k_ref[pl.ds(k0, tq), cols], scale) for cols in heads_cols]
            gates = [_log_gates(z, diagonal) for z in zs]
            sfxs = [_split_dot(jnp.concatenate(_lane_blocks(ln), axis=0), tri) for _, ln, _ in gates]
            for cols, (ls, ln, valid), sfx in zip(heads_cols, gates, sfxs):
                blocks = _lane_blocks(ln)
                car = car_scr[:, cols]
                parts = [None] * nb
                for b in reversed(range(nb)):
                    sb = sfx[b * tq:(b + 1) * tq]
                    parts[b] = sb + car
                    car = car + (sb[:, 0:1] + blocks[b][:, 0:1])
                car_scr[:, cols] = car
                wgt = jnp.exp(ls + jnp.concatenate(parts, axis=1))
                if diagonal:
                    wgt = jnp.where(valid, wgt, 0.0)
                acc_scr[:, cols] += jnp.dot(
                    wgt.astype(BF16), v_ref[pl.ds(k0, tq), cols], preferred_element_type=F32)

        tile(qi, True)

        def step(j, carry):
            tile(qi - 1 - j, False)
            return carry

        lax.fori_loop(0, qi, step, 0)
        o_ref[...] = acc_scr[...].astype(BF16)
        tot_ref[...] = car_scr[...]

    return pl.pallas_call(
        body, name="attn_fwd", grid=(groups, nq),
        in_specs=[pl.BlockSpec((tq, wid), lambda h, i: (i, h)),
                  pl.BlockSpec((s, wid), lambda h, i: (0, groups + h)),
                  pl.BlockSpec((s, wid), lambda h, i: (0, 2 * groups + h))],
        out_specs=[pl.BlockSpec((tq, wid), lambda h, i: (i, h)), pl.BlockSpec((tq, wid), lambda h, i: (i, h))],
        out_shape=[jax.ShapeDtypeStruct((s, heads * dh), BF16), jax.ShapeDtypeStruct((s, heads * dh), F32)],
        scratch_shapes=[pltpu.VMEM((tq, wid), F32), pltpu.VMEM((tq, wid), F32)],
        compiler_params=_cp(("parallel", "arbitrary")),
    )(qkv, qkv, qkv)


def _attn_bwd(qkv, tot, do, heads):
    s = qkv.shape[0]
    dh = LANES
    tq = _pick(s, ATT_TILE)
    nq = s // tq
    nb = tq // LANES
    scale = 1.0 / math.sqrt(dh)

    hp = ATT_HEADS_PER_STEP
    groups = heads // hp
    wid = hp * dh

    def body(q_ref, k_ref, v_ref, tot_ref, do_ref, dq_ref, dk_ref, dv_ref,
             dq_scr, dk_scr, dv_scr, cl_scr, cg_scr):
        qi = pl.program_id(1)

        @pl.when(qi == 0)
        def _():
            dk_scr[...] = jnp.zeros_like(dk_scr)
            dv_scr[...] = jnp.zeros_like(dv_scr)

        dq_scr[...] = jnp.zeros_like(dq_scr)
        cl_scr[...] = jnp.zeros_like(cl_scr)
        cg_scr[...] = jnp.zeros_like(cg_scr)
        tri_le = _tri(lambda r, c: r <= c)
        tri_lt = _tri(lambda r, c: r < c)

        def tile(kt, diagonal):
            k0 = pl.multiple_of(kt * tq, tq)
            heads_cols = [slice(hh * dh, (hh + 1) * dh) for hh in range(hp)]
            keys = pl.ds(k0, tq)
            zs = [_scores(q_ref[:, cols], k_ref[keys, cols], scale) for cols in heads_cols]
            dws = [lax.dot_general(do_ref[:, cols], v_ref[keys, cols], _DIMS["nt"], preferred_element_type=F32)
                   for cols in heads_cols]
            gates = [_log_gates(z, diagonal) for z in zs]
            pins = [_split_dot(jnp.concatenate(_lane_blocks(ln), axis=0), tri_le) for _, ln, _ in gates]
            wgts, gs = [], []
            for cols, (ls, _, valid), pin, dw in zip(heads_cols, gates, pins, dws):
                total = tot_ref[:, cols]
                cl = cl_scr[:, cols]
                parts = []
                for b in range(nb):
                    pb = pin[b * tq:(b + 1) * tq] + cl
                    parts.append(total - pb)
                    cl = _last_lane(pb)
                cl_scr[:, cols] = cl
                wgt = jnp.exp(ls + jnp.concatenate(parts, axis=1))
                if diagonal:
                    wgt = jnp.where(valid, wgt, 0.0)
                wgts.append(wgt)
                gs.append(wgt * dw)
            pexs = [jnp.dot(jnp.concatenate(_lane_blocks(g), axis=0).astype(BF16), tri_lt,
                            preferred_element_type=F32) for g in gs]
            for cols, wgt in zip(heads_cols, wgts):
                dv_scr[keys, cols] += lax.dot_general(
                    wgt.astype(BF16), do_ref[:, cols], _DIMS["tn"], preferred_element_type=F32)
            for cols, (ls, _, valid), g, pex in zip(heads_cols, gates, gs, pexs):
                gblocks = _lane_blocks(g)
                cg = cg_scr[:, cols]
                parts = []
                for b in range(nb):
                    pb = pex[b * tq:(b + 1) * tq] + cg
                    parts.append(pb)
                    cg = _last_lane(pb + gblocks[b])
                cg_scr[:, cols] = cg
                dz = g - jnp.exp(ls) * (g + jnp.concatenate(parts, axis=1))
                if diagonal:
                    dz = jnp.where(valid, dz, 0.0)
                dz = dz.astype(BF16)
                dq_scr[:, cols] += jnp.dot(dz, k_ref[keys, cols], preferred_element_type=F32)
                dk_scr[keys, cols] += lax.dot_general(
                    dz, q_ref[:, cols], _DIMS["tn"], preferred_element_type=F32)

        def step(j, carry):
            tile(j, False)
            return carry

        lax.fori_loop(0, qi, step, 0)
        tile(qi, True)
        dq_ref[...] = (dq_scr[...] * scale).astype(BF16)

        @pl.when(qi == nq - 1)
        def _():
            dk_ref[...] = (dk_scr[...] * scale).astype(BF16)
            dv_ref[...] = dv_scr[...].astype(BF16)

    qblk = pl.BlockSpec((tq, wid), lambda h, i: (i, h))
    hblk = pl.BlockSpec((s, wid), lambda h, i: (0, h))
    out = jax.ShapeDtypeStruct((s, heads * dh), BF16)
    return pl.pallas_call(
        body, name="attn_bwd", grid=(groups, nq),
        in_specs=[qblk, pl.BlockSpec((s, wid), lambda h, i: (0, groups + h)),
                  pl.BlockSpec((s, wid), lambda h, i: (0, 2 * groups + h)), qblk, qblk],
        out_specs=[qblk, hblk, hblk], out_shape=[out, out, out],
        scratch_shapes=[pltpu.VMEM((tq, wid), F32), pltpu.VMEM((s, wid), F32), pltpu.VMEM((s, wid), F32),
                        pltpu.VMEM((tq, wid), F32), pltpu.VMEM((tq, wid), F32)],
        compiler_params=_cp(("parallel", "arbitrary")),
    )(qkv, qkv, qkv, tot, do)


def _place():
    x, y, c = lax.axis_index("x"), lax.axis_index("y"), lax.axis_index("c")
    chips = [(1 - x, y), (x, 1 - y), (1 - x, 1 - y)]
    return x, y, c, chips


def _hbm_specs(n):
    return [pl.BlockSpec(memory_space=pl.ANY) for _ in range(n)]


def _remote(src, dst, send_sem, recv_sem, dev):
    return pltpu.make_async_remote_copy(
        src_ref=src, dst_ref=dst, send_sem=send_sem, recv_sem=recv_sem, device_id=dev, device_id_type=MESH)


def _allgather_weights(name, fulls):
    n = len(fulls)

    def body(*refs):
        bufs = refs[n:2 * n]
        send_sems, recv_sems, fsend_sems, frecv_sems = refs[2 * n:]
        x, y, c, chips = _place()
        me = 2 * x + y
        sibling = (x, y, 1 - c)
        firsts = []
        for w in range(n):
            hr = bufs[w].shape[1] // 2
            mine = bufs[w].at[me, pl.ds(c * hr, hr)]
            for k, (px, py) in enumerate(chips):
                cp = _remote(mine, mine, send_sems.at[w, k], recv_sems.at[w, k], (px, py, c))
                cp.start()
                firsts.append(cp)
        passed = []
        for w in range(n):
            hr = bufs[w].shape[1] // 2
            for k, (px, py) in enumerate(chips):
                slot = bufs[w].at[2 * px + py, pl.ds(c * hr, hr)]
                _remote(slot, slot, send_sems.at[w, k], recv_sems.at[w, k], (px, py, c)).wait_recv()
                cp = _remote(slot, slot, fsend_sems.at[w, k], frecv_sems.at[w, k], sibling)
                cp.start()
                passed.append(cp)
        for w in range(n):
            hr = bufs[w].shape[1] // 2
            for k, (px, py) in enumerate(chips):
                slot = bufs[w].at[2 * px + py, pl.ds((1 - c) * hr, hr)]
                _remote(slot, slot, fsend_sems.at[w, k], frecv_sems.at[w, k], sibling).wait_recv()
        for cp in firsts + passed:
            cp.wait_send()

    sem = lambda: pltpu.SemaphoreType.DMA((n, 3))
    return pl.pallas_call(
        body, name=name, in_specs=_hbm_specs(n), out_specs=_hbm_specs(n),
        out_shape=[jax.ShapeDtypeStruct(f.shape, f.dtype) for f in fulls],
        input_output_aliases={w: w for w in range(n)},
        scratch_shapes=[sem(), sem(), sem(), sem()],
    )(*fulls)


def _handshake(peers):
    barrier = pltpu.get_barrier_semaphore()
    for dev in peers:
        pl.semaphore_signal(barrier, inc=1, device_id=dev, device_id_type=MESH)
    pl.semaphore_wait(barrier, len(peers))


def _allgather_async(name, slot_buf, collective_id):
    buf = jax.new_ref(slot_buf, memory_space=pltpu.MemorySpace.HBM)
    hr = slot_buf.shape[1] // 2
    dma = pltpu.SemaphoreType.DMA

    @pl.kernel(mesh=plsc.ScalarSubcoreMesh(axis_name="seq", num_cores=1), name=name,
               scratch_types=(dma,) * 12, compiler_params=pltpu.CompilerParams(collective_id=collective_id))
    def launch(*sems):
        send_sems, recv_sems, fsend_sems, frecv_sems = sems[0:3], sems[3:6], sems[6:9], sems[9:12]
        x, y, c, chips = _place()
        me = 2 * x + y
        sibling = (x, y, 1 - c)
        _handshake([(px, py, c) for px, py in chips] + [sibling])
        mine = buf.at[me, pl.ds(c * hr, hr)]
        firsts = []
        for k, (px, py) in enumerate(chips):
            cp = _remote(mine, mine, send_sems[k], recv_sems[k], (px, py, c))
            cp.start()
            firsts.append(cp)
        passed = []
        for k, (px, py) in enumerate(chips):
            slot = buf.at[2 * px + py, pl.ds(c * hr, hr)]
            _remote(slot, slot, send_sems[k], recv_sems[k], (px, py, c)).wait_recv()
            cp = _remote(slot, slot, fsend_sems[k], frecv_sems[k], sibling)
            cp.start()
            passed.append(cp)
        for k, (px, py) in enumerate(chips):
            slot = buf.at[2 * px + py, pl.ds((1 - c) * hr, hr)]
            _remote(slot, slot, fsend_sems[k], frecv_sems[k], sibling).wait_recv()
        for cp in firsts + passed:
            cp.wait_send()

    launch()
    return buf[...]


def _exchange_sibling_halves(name, slabs):
    n = len(slabs)

    def body(*refs):
        ins, outs = refs[:n], refs[n:2 * n]
        send_sems, recv_sems = refs[2 * n:]
        x, y, c, _ = _place()
        cps = []
        for w in range(n):
            hr = ins[w].shape[1] // 2
            cp = _remote(ins[w].at[:, pl.ds((1 - c) * hr, hr), :], outs[w], send_sems.at[w], recv_sems.at[w],
                         (x, y, 1 - c))
            cp.start()
            cps.append(cp)
        for cp in cps:
            cp.wait()

    return pl.pallas_call(
        body, name=name, in_specs=_hbm_specs(n), out_specs=_hbm_specs(n),
        out_shape=[jax.ShapeDtypeStruct((N_CHIPS, s.shape[1] // 2, s.shape[2]), s.dtype) for s in slabs],
        scratch_shapes=[pltpu.SemaphoreType.DMA((n,)), pltpu.SemaphoreType.DMA((n,))],
    )(*slabs)


def _exchange_chips(name, parts):
    n = len(parts)

    def body(*refs):
        ins, outs = refs[:n], refs[n:2 * n]
        send_sems, recv_sems = refs[2 * n:]
        x, y, c, chips = _place()
        cps = []
        for w in range(n):
            for k, (px, py) in enumerate(chips):
                cp = _remote(ins[w].at[2 * px + py], outs[w].at[k], send_sems.at[w, k], recv_sems.at[w, k],
                             (px, py, c))
                cp.start()
                cps.append(cp)
        for cp in cps:
            cp.wait()

    return pl.pallas_call(
        body, name=name, in_specs=_hbm_specs(n), out_specs=_hbm_specs(n),
        out_shape=[jax.ShapeDtypeStruct((3,) + s.shape[1:], s.dtype) for s in parts],
        scratch_shapes=[pltpu.SemaphoreType.DMA((n, 3)), pltpu.SemaphoreType.DMA((n, 3))],
    )(*parts)


def _join_sibling_halves(name, bufs):
    n = len(bufs)

    def body(*refs):
        outs = refs[n:2 * n]
        send_sems, recv_sems = refs[2 * n:]
        x, y, c, _ = _place()
        cps = []
        for w in range(n):
            hr = outs[w].shape[0] // 2
            mine = outs[w].at[pl.ds(c * hr, hr)]
            cp = _remote(mine, mine, send_sems.at[w], recv_sems.at[w], (x, y, 1 - c))
            cp.start()
            cps.append(cp)
        for w in range(n):
            hr = outs[w].shape[0] // 2
            other = outs[w].at[pl.ds((1 - c) * hr, hr)]
            _remote(other, other, send_sems.at[w], recv_sems.at[w], (x, y, 1 - c)).wait_recv()
        for cp in cps:
            cp.wait_send()

    return pl.pallas_call(
        body, name=name, in_specs=_hbm_specs(n), out_specs=_hbm_specs(n),
        out_shape=[jax.ShapeDtypeStruct(b.shape, b.dtype) for b in bufs],
        input_output_aliases={w: w for w in range(n)},
        scratch_shapes=[pltpu.SemaphoreType.DMA((n,)), pltpu.SemaphoreType.DMA((n,))],
    )(*bufs)


def _allgather_chips_small(name, v):
    r = v.shape[0]

    def body(v_ref, o_ref, send_sems, recv_sems):
        x, y, c, chips = _place()
        me = 2 * x + y
        o_ref[me] = v_ref[...]
        cps = []
        for k, (px, py) in enumerate(chips):
            cp = _remote(v_ref, o_ref.at[me], send_sems.at[k], recv_sems.at[k], (px, py, c))
            cp.start()
            cps.append(cp)
        for k, (px, py) in enumerate(chips):
            slot = o_ref.at[2 * px + py]
            _remote(slot, slot, send_sems.at[k], recv_sems.at[k], (px, py, c)).wait_recv()
        for cp in cps:
            cp.wait_send()

    return pl.pallas_call(
        body, name=name, in_specs=[pl.BlockSpec(memory_space=pltpu.VMEM)],
        out_specs=pl.BlockSpec(memory_space=pltpu.VMEM),
        out_shape=jax.ShapeDtypeStruct((N_CHIPS, r, LANES), F32),
        scratch_shapes=[pltpu.SemaphoreType.DMA((3,)), pltpu.SemaphoreType.DMA((3,))],
    )(v)


def _allreduce_small(name, v):
    r = v.shape[0]

    def body(v_ref, o_ref, all_ref, send_sems, recv_sems):
        x, y, c, _ = _place()
        me = 4 * x + 2 * y + c
        all_ref[me] = v_ref[...]
        peers = [(1 - x if k & 4 else x, 1 - y if k & 2 else y, 1 - c if k & 1 else c)
                 for k in range(1, N_DEV)]
        cps = []
        for k, dev in enumerate(peers):
            cp = _remote(v_ref, all_ref.at[me], send_sems.at[k], recv_sems.at[k], dev)
            cp.start()
            cps.append(cp)
        for k, (px, py, pc) in enumerate(peers):
            slot = all_ref.at[4 * px + 2 * py + pc]
            _remote(slot, slot, send_sems.at[k], recv_sems.at[k], (px, py, pc)).wait_recv()
        for cp in cps:
            cp.wait_send()
        total = all_ref[0]
        for d in range(1, N_DEV):
            total = total + all_ref[d]
        o_ref[...] = total

    return pl.pallas_call(
        body, name=name, in_specs=[pl.BlockSpec(memory_space=pltpu.VMEM)],
        out_specs=pl.BlockSpec(memory_space=pltpu.VMEM),
        out_shape=jax.ShapeDtypeStruct((r, LANES), F32),
        scratch_shapes=[pltpu.VMEM((N_DEV, r, LANES), F32), pltpu.SemaphoreType.DMA((N_DEV - 1,)),
                        pltpu.SemaphoreType.DMA((N_DEV - 1,))],
    )(v)


def _add_sibling(name, slabs, recv, c):
    _, r, cols = slabs.shape
    hr = r // 2
    tr = _pick(hr, ROW_TILE)
    nb = hr // tr

    def body(c_ref, a_ref, b_ref, o_ref):
        o_ref[...] = (a_ref[...].astype(F32) + b_ref[...].astype(F32)).astype(BF16)

    grid_spec = pltpu.PrefetchScalarGridSpec(
        num_scalar_prefetch=1, grid=(N_CHIPS, nb),
        in_specs=[pl.BlockSpec((None, tr, cols), lambda j, i, c_ref: (j, c_ref[0] * nb + i, 0)),
                  pl.BlockSpec((None, tr, cols), lambda j, i, c_ref: (j, i, 0))],
        out_specs=pl.BlockSpec((None, tr, cols), lambda j, i, c_ref: (j, i, 0)))
    return pl.pallas_call(
        body, name=name, grid_spec=grid_spec,
        out_shape=jax.ShapeDtypeStruct((N_CHIPS, hr, cols), BF16),
        compiler_params=_cp(("parallel", "parallel")))(jnp.reshape(c, (1,)).astype(jnp.int32), slabs, recv)


def _sum_chips(name, own, recv, chip, c):
    _, hr, cols = recv.shape
    tr = _pick(hr, ROW_TILE)
    nb = hr // tr

    def body(sc_ref, own_ref, recv_ref, o_ref):
        total = own_ref[...].astype(F32)
        for k in range(3):
            total = total + recv_ref[k].astype(F32)
        o_ref[...] = total

    grid_spec = pltpu.PrefetchScalarGridSpec(
        num_scalar_prefetch=1, grid=(nb,),
        in_specs=[pl.BlockSpec((None, tr, cols), lambda i, sc: (sc[0], i, 0)),
                  pl.BlockSpec((3, tr, cols), lambda i, sc: (0, i, 0))],
        out_specs=pl.BlockSpec((tr, cols), lambda i, sc: (sc[1] * nb + i, 0)))
    return pl.pallas_call(
        body, name=name, grid_spec=grid_spec, out_shape=jax.ShapeDtypeStruct((2 * hr, cols), F32),
        compiler_params=_cp(("parallel",)))(jnp.stack([chip, c]).astype(jnp.int32), own, recv)


def _adamw_math(w, g, m, v):
    m = ADAM_B1 * m + (1.0 - ADAM_B1) * g
    v = ADAM_B2 * v + (1.0 - ADAM_B2) * (g * g)
    m_hat = m / (1.0 - ADAM_B1 ** ADAM_STEP)
    v_hat = v / (1.0 - ADAM_B2 ** ADAM_STEP)
    delta = -ADAM_LR * (m_hat / (jnp.sqrt(v_hat) + ADAM_EPS) + ADAM_WD * w)
    return delta, m, v


def _adamw(name, w, gs, m, v):
    nl, r, cols = w.shape
    tr = _pick(r, LANES)

    def body(*refs):
        w_ref, m_ref, v_ref = refs[0:3]
        g_refs = refs[3:3 + nl]
        go_ref, d_ref, nm_ref, nv_ref = refs[3 + nl:]
        layer = pl.program_id(0)
        g = g_refs[0][...]
        for j in range(1, nl):
            g = jnp.where(layer == j, g_refs[j][...], g)
        d, nm, nv = _adamw_math(w_ref[...], g, m_ref[...], v_ref[...])
        go_ref[...] = g
        d_ref[...] = d
        nm_ref[...] = nm
        nv_ref[...] = nv

    spec3 = pl.BlockSpec((None, tr, cols), lambda l, i: (l, i, 0))
    gspec = pl.BlockSpec((tr, cols), lambda l, i: (i, 0))
    out = jax.ShapeDtypeStruct((nl, r, cols), F32)
    return pl.pallas_call(
        body, name=name, grid=(nl, r // tr), in_specs=[spec3] * 3 + [gspec] * nl, out_specs=[spec3] * 4,
        out_shape=[out] * 4, compiler_params=_cp(("parallel", "parallel")))(w, m, v, *gs)


def _adamw_small(name, groups):
    n = len(groups)
    flat = [a for grp in groups for a in grp]

    def body(*refs):
        ins, outs = refs[:4 * n], refs[4 * n:]
        for p in range(n):
            w_ref, g_ref, m_ref, v_ref = ins[4 * p:4 * p + 4]
            d, nm, nv = _adamw_math(w_ref[...], g_ref[...], m_ref[...], v_ref[...])
            outs[3 * p][...] = d
            outs[3 * p + 1][...] = nm
            outs[3 * p + 2][...] = nv

    vm = pl.BlockSpec(memory_space=pltpu.VMEM)
    out_shape = [jax.ShapeDtypeStruct(grp[0].shape, F32) for grp in groups for _ in range(3)]
    res = pl.pallas_call(
        body, name=name, in_specs=[vm] * (4 * n), out_specs=[vm] * (3 * n), out_shape=out_shape)(*flat)
    return [tuple(res[3 * p:3 * p + 3]) for p in range(n)]


def _block_diag_pairs(w):
    h, d, _ = w.shape
    z = jnp.zeros((h // 2, d, d), w.dtype)
    top = jnp.concatenate([w[0::2], z], axis=2)
    bot = jnp.concatenate([z, w[1::2]], axis=2)
    return jnp.concatenate([top, bot], axis=1).astype(BF16)


def _diag_pairs_to_heads(g, d):
    a = g[:, :d, :d]
    b = g[:, d:, d:]
    return jnp.stack([a, b], axis=1).reshape(-1, d, d)


def _rows128(a):
    flat = a.reshape(-1, LANES)
    pad = (-flat.shape[0]) % SUBLANES
    if pad:
        flat = jnp.concatenate([flat, jnp.zeros((pad, LANES), flat.dtype)], axis=0)
    return flat


def _unshard_last(g4, shape):
    g4 = g4.reshape((N_CHIPS,) + tuple(shape))
    return jnp.concatenate([g4[j] for j in range(N_CHIPS)], axis=-1)


def kernel(x, norm_gains, hyb_w_in, hyb_conv_a, hyb_conv_b, hyb_conv_b_bias, hyb_rg_w_a, hyb_rg_b_a, hyb_rg_w_x, hyb_rg_b_x, hyb_rg_lambda, hyb_w_out, sb_w_qkv, sb_w_o, mlp_w_up, mlp_w_down, loss_target, m_norm_gains, m_hyb_w_in, m_hyb_conv_a, m_hyb_conv_b, m_hyb_conv_b_bias, m_hyb_rg_w_a, m_hyb_rg_b_a, m_hyb_rg_w_x, m_hyb_rg_b_x, m_hyb_rg_lambda, m_hyb_w_out, m_sb_w_qkv, m_sb_w_o, m_mlp_w_up, m_mlp_w_down, v_norm_gains, v_hyb_w_in, v_hyb_conv_a, v_hyb_conv_b, v_hyb_conv_b_bias, v_hyb_rg_w_a, v_hyb_rg_b_a, v_hyb_rg_w_x, v_hyb_rg_b_x, v_hyb_rg_lambda, v_hyb_w_out, v_sb_w_qkv, v_sb_w_o, v_mlp_w_up, v_mlp_w_down):
    cx_ = lax.axis_index("x")
    cy_ = lax.axis_index("y")
    cc_ = lax.axis_index("c")
    chip = 2 * cx_ + cy_

    x0 = x[0]
    target = loss_target[0]
    s, d = x0.shape
    heads = SB_HEADS
    assert d // heads == LANES
    n_rg, hd = hyb_rg_w_a.shape[1], hyb_rg_w_a.shape[2]
    wmix = n_rg * hd
    assert 2 * hd == LANES

    big = {
        "hyb_w_in": hyb_w_in[0], "hyb_w_out": hyb_w_out[0], "mlp_w_up0": mlp_w_up[0], "mlp_w_down0": mlp_w_down[0],
        "sb_w_qkv": sb_w_qkv[0], "sb_w_o": sb_w_o[0], "mlp_w_up1": mlp_w_up[1], "mlp_w_down1": mlp_w_down[1],
    }
    names = list(big)
    slots = [_cast_into_slot("cast_" + k, big[k], chip) for k in names]
    full = {k: _allgather_async("allgather_" + k, slot, cid) for cid, (k, slot) in enumerate(zip(names, slots))}
    rowsharded = lambda k: full[k].reshape(-1, full[k].shape[2])

    ng_s, ca_s, cb_s = norm_gains.reshape(-1, norm_gains.shape[2]), hyb_conv_a[0], hyb_conv_b[0]
    packed = jnp.concatenate([_rows128(ng_s), _rows128(ca_s), _rows128(cb_s)], axis=0)
    gathered = _allgather_chips_small("allgather_small", packed)
    n0 = ng_s.size // LANES
    n1 = n0 + (-n0) % SUBLANES
    m0 = ca_s.size // LANES
    m1 = m0 + (-m0) % SUBLANES
    k0 = cb_s.size // LANES
    gains = _unshard_last(gathered[:, 0:n0], ng_s.shape).reshape(2, 4, 1, d)
    conv_a = _unshard_last(gathered[:, n1:n1 + m0], ca_s.shape)
    conv_b = _unshard_last(gathered[:, n1 + m1:n1 + m1 + k0], cb_s.shape)
    bias, b_a, b_x, lam = hyb_conv_b_bias, hyb_rg_b_a, hyb_rg_b_x, hyb_rg_lambda
    wa_blk = _block_diag_pairs(hyb_rg_w_a[0])
    wx_blk = _block_diag_pairs(hyb_rg_w_x[0])

    relu_sq = lambda acc: (jnp.maximum(acc, 0.0), jnp.square(jnp.maximum(acc, 0.0)))

    h1 = _rms_fwd("rms_pre0", x0, gains[0, 0])
    proj = _mm_fwd_col("proj_in", h1, full["hyb_w_in"])[0]
    ycat, hseq = _mixer_fwd(proj, conv_a, conv_b, bias, wa_blk, b_a, wx_blk, b_x, lam)
    mix0 = _mm_fwd_row("proj_out", ycat, rowsharded("hyb_w_out"))
    x1, h2 = _rms_post("rms_mix0", mix0, gains[0, 1], x0, gains[0, 2])
    u0, a0 = _mm_fwd_col("mlp_up0", h2, full["mlp_w_up0"], (BF16, BF16), relu_sq)
    mlp0 = _mm_fwd_row("mlp_down0", a0, rowsharded("mlp_w_down0"))
    x2, h3 = _rms_post("rms_mlp0", mlp0, gains[0, 3], x1, gains[1, 0])

    qkv = _mm_fwd_col("qkv", h3, full["sb_w_qkv"], (BF16,))[0]
    att, tot = _attn_fwd(qkv, heads)
    mix1 = _mm_fwd_row("attn_out", att, rowsharded("sb_w_o"))
    x3, h4 = _rms_post("rms_mix1", mix1, gains[1, 1], x2, gains[1, 2])
    u1, a1 = _mm_fwd_col("mlp_up1", h4, full["mlp_w_up1"], (BF16, BF16), relu_sq)
    mlp1 = _mm_fwd_row("mlp_down1", a1, rowsharded("mlp_w_down1"))
    (x4,) = _rms_post("rms_mlp1", mlp1, gains[1, 3], x3)

    dy, loss_local = _loss_head("loss_head", x4, target)
    loss = lax.psum(loss_local, ("x", "y", "c"))

    grads_big = {}
    dgain = [[None] * 4 for _ in range(2)]
    drelu = lambda acc, u: (acc * (2.0 * u.astype(F32)),)

    def mlp_bwd(layer, dxo, mlp_out, xin, hin, u, a):
        dmlp, dgain[layer][3] = _rms_bwd(f"rms_mlp{layer}_bwd", mlp_out, gains[layer, 3], dxo, out_dtype=BF16)
        wd, wu = rowsharded(f"mlp_w_down{layer}"), full[f"mlp_w_up{layer}"]
        grads_big[f"mlp_w_down{layer}"] = _mm_wgrad_row(f"mlp_down{layer}_wgrad", a, dmlp).reshape(
            N_CHIPS, -1, d)
        du = _mm_bwd_row(f"mlp_down{layer}_bwd", dmlp, wd, (BF16,), u, drelu)[0]
        grads_big[f"mlp_w_up{layer}"] = _mm_wgrad_col(f"mlp_up{layer}_wgrad", hin, du, wu.shape[2])
        dh = _mm_bwd_col(f"mlp_up{layer}_bwd", du, wu)
        dxm, dgain[layer][2] = _rms_bwd(f"rms_premlp{layer}_bwd", xin, gains[layer, 2], dh, res=dxo)
        return dxm

    dx3 = mlp_bwd(1, dy, mlp1, x3, h4, u1, a1)
    dmix1, dgain[1][1] = _rms_bwd("rms_mix1_bwd", mix1, gains[1, 1], dx3, out_dtype=BF16)
    grads_big["sb_w_o"] = _mm_wgrad_row("attn_out_wgrad", att, dmix1).reshape(N_CHIPS, -1, d)
    datt = _mm_bwd_row("attn_out_bwd", dmix1, rowsharded("sb_w_o"), (BF16,))[0]
    dq, dk, dv = _attn_bwd(qkv, tot, datt, heads)
    dqkv = jnp.concatenate([dq, dk, dv], axis=1)
    grads_big["sb_w_qkv"] = _mm_wgrad_col("qkv_wgrad", h3, dqkv, full["sb_w_qkv"].shape[2])
    dh3 = _mm_bwd_col("qkv_bwd", dqkv, full["sb_w_qkv"])
    dx2, dgain[1][0] = _rms_bwd("rms_pre1_bwd", x2, gains[1, 0], dh3, res=dx3)

    dx1 = mlp_bwd(0, dx2, mlp0, x1, h2, u0, a0)
    dmix0, dgain[0][1] = _rms_bwd("rms_mix0_bwd", mix0, gains[0, 1], dx1, out_dtype=BF16)
    grads_big["hyb_w_out"] = _mm_wgrad_row("proj_out_wgrad", ycat, dmix0).reshape(N_CHIPS, -1, d)
    dycat = _mm_bwd_row("proj_out_bwd", dmix0, rowsharded("hyb_w_out"))[0]
    dproj, xr_b, dpa_b, dpx_b, sg = _mixer_bwd(
        proj, hseq, dycat, conv_a, conv_b, bias, wa_blk, b_a, wx_blk, b_x, lam)
    grads_big["hyb_w_in"] = _mm_wgrad_col("proj_in_wgrad", h1, dproj, full["hyb_w_in"].shape[2])
    dh1 = _mm_bwd_col("proj_in_bwd", dproj, full["hyb_w_in"])
    dx0, dgain[0][0] = _rms_bwd("rms_pre0_bwd", x0, gains[0, 0], dh1, res=dx1)
    dwa = _diag_pairs_to_heads(_mm_wgrad_diag("rg_w_a_wgrad", xr_b, dpa_b), hd)
    dwx = _diag_pairs_to_heads(_mm_wgrad_diag("rg_w_x_wgrad", xr_b, dpx_b), hd)

    dgains = jnp.concatenate([dgain[l][k] for l in range(2) for k in range(4)], axis=0)
    small_parts = [dgains, sg[_SG_CONV_A:_SG_CONV_A + 3], sg[_SG_CONV_B:_SG_CONV_B + 4], sg[_SG_BIAS:_SG_BIAS + 1],
                   dwa, sg[_SG_BA:_SG_BA + 1], dwx, sg[_SG_BX:_SG_BX + 1], sg[_SG_LAM:_SG_LAM + 1]]
    small_rows = [_rows128(p) for p in small_parts]
    reduced = _allreduce_small("allreduce_small", jnp.concatenate(small_rows, axis=0))
    small_full, off = [], 0
    for p, rws in zip(small_parts, small_rows):
        small_full.append(reduced[off:off + p.size // LANES].reshape(p.shape))
        off += rws.shape[0]
    g_gains, g_ca, g_cb, g_bias, g_wa, g_ba, g_wx, g_bx, g_lam = small_full

    def my_cols(g, width):
        return lax.dynamic_slice_in_dim(g, chip * width, width, axis=g.ndim - 1)

    small = [
        ("norm_gains", norm_gains, my_cols(g_gains, norm_gains.shape[2]).reshape(norm_gains.shape),
         m_norm_gains, v_norm_gains),
        ("hyb_conv_a", hyb_conv_a, my_cols(g_ca, hyb_conv_a.shape[2])[None], m_hyb_conv_a, v_hyb_conv_a),
        ("hyb_conv_b", hyb_conv_b, my_cols(g_cb, hyb_conv_b.shape[2])[None], m_hyb_conv_b, v_hyb_conv_b),
        ("hyb_conv_b_bias", hyb_conv_b_bias, g_bias, m_hyb_conv_b_bias, v_hyb_conv_b_bias),
        ("hyb_rg_w_a", hyb_rg_w_a, g_wa[None], m_hyb_rg_w_a, v_hyb_rg_w_a),
        ("hyb_rg_b_a", hyb_rg_b_a, g_ba, m_hyb_rg_b_a, v_hyb_rg_b_a),
        ("hyb_rg_w_x", hyb_rg_w_x, g_wx[None], m_hyb_rg_w_x, v_hyb_rg_w_x),
        ("hyb_rg_b_x", hyb_rg_b_x, g_bx, m_hyb_rg_b_x, v_hyb_rg_b_x),
        ("hyb_rg_lambda", hyb_rg_lambda, g_lam, m_hyb_rg_lambda, v_hyb_rg_lambda),
    ]
    to2d = lambda a: a.reshape(-1, a.shape[-1])
    small_res = _adamw_small("adamw_small", [tuple(to2d(a) for a in (w, g, m, v)) for _, w, g, m, v in small])
    out = {}
    for (nm, w, g, _, _), (dl, nmom, nvar) in zip(small, small_res):
        out[nm] = (g, dl.reshape(w.shape), nmom.reshape(w.shape), nvar.reshape(w.shape))

    slabs = [grads_big[k] for k in names]
    recv_sib = _exchange_sibling_halves("grads_to_sibling", slabs)
    chip_part = [_add_sibling("grads_add_" + k, sl, rv, cc_) for k, sl, rv in zip(names, slabs, recv_sib)]
    recv_chip = _exchange_chips("grads_to_chips", chip_part)
    halves = [_sum_chips("grads_sum_" + k, own, rc, chip, cc_) for k, own, rc in zip(names, chip_part, recv_chip)]
    gfull = dict(zip(names, _join_sibling_halves("grads_join", halves)))

    stacked = {
        "hyb_w_in": (hyb_w_in, m_hyb_w_in, v_hyb_w_in, ["hyb_w_in"]),
        "hyb_w_out": (hyb_w_out, m_hyb_w_out, v_hyb_w_out, ["hyb_w_out"]),
        "sb_w_qkv": (sb_w_qkv, m_sb_w_qkv, v_sb_w_qkv, ["sb_w_qkv"]),
        "sb_w_o": (sb_w_o, m_sb_w_o, v_sb_w_o, ["sb_w_o"]),
        "mlp_w_up": (mlp_w_up, m_mlp_w_up, v_mlp_w_up, ["mlp_w_up0", "mlp_w_up1"]),
        "mlp_w_down": (mlp_w_down, m_mlp_w_down, v_mlp_w_down, ["mlp_w_down0", "mlp_w_down1"]),
    }
    for k, (w, m, v, parts) in stacked.items():
        out[k] = tuple(_adamw("adamw_" + k, w, [gfull[p] for p in parts], m, v))

    order = ["norm_gains", "hyb_w_in", "hyb_conv_a", "hyb_conv_b", "hyb_conv_b_bias", "hyb_rg_w_a", "hyb_rg_b_a",
             "hyb_rg_w_x", "hyb_rg_b_x", "hyb_rg_lambda", "hyb_w_out", "sb_w_qkv", "sb_w_o", "mlp_w_up",
             "mlp_w_down"]
    return (loss, dx0[None], *[out[k][0] for k in order], *[out[k][1] for k in order],
            *[out[k][2] for k in order], *[out[k][3] for k in order])
```

```python
import functools
import math

import jax
import jax.numpy as jnp
from jax import lax
from jax.experimental import pallas as pl
from jax.experimental.pallas import tpu as pltpu
from jax.experimental.pallas import tpu_sc as plsc

F32 = jnp.float32
BF16 = jnp.bfloat16
MESH = pl.DeviceIdType.MESH

SB_HEADS = 16
NORM_EPS = 1e-6
LRU_C = 8.0
ADAM_LR = 0.001
ADAM_B1 = 0.9
ADAM_B2 = 0.999
ADAM_EPS = 1e-08
ADAM_WD = 0.01
ADAM_STEP = 10

LANES = 128
SUBLANES = 8
VMEM_LIMIT = 48 * 1024 * 1024
MM_TILE = 1024
MM_TILE_K = 2048
ROW_TILE = 256
ATT_TILE = 512
ATT_HEADS_PER_STEP = 2
N_CHIPS = 4
N_DEV = 8
COLLECTIVE_SIBLING = 8
COLLECTIVE_CHIPS = 9

_DIMS = {
    "nn": (((1,), (0,)), ((), ())),
    "nt": (((1,), (1,)), ((), ())),
    "tn": (((0,), (0,)), ((), ())),
}


def _cp(sem=None, vmem=VMEM_LIMIT):
    return pltpu.CompilerParams(dimension_semantics=sem, vmem_limit_bytes=vmem)


def _pick(dim, pref):
    t = min(dim, pref)
    while dim % t:
        t -= LANES
    return t


def _whole(shape):
    nd = len(shape)
    return pl.BlockSpec(tuple(shape), lambda *_: (0,) * nd)


def _sigmoid(z):
    return 1.0 / (1.0 + jnp.exp(-z))


def _log_sigmoid(z):
    return jnp.minimum(z, 0.0) - jnp.log(1.0 + jnp.exp(-jnp.abs(z)))


def _expm1(z):
    series = z * (1.0 + z * (0.5 + z * (1.0 / 6.0 + z * (1.0 / 24.0))))
    return jnp.where(jnp.abs(z) < 0.05, series, jnp.exp(z) - 1.0)


_GELU_C = math.sqrt(2.0 / math.pi)


def _gelu_and_grad(g):
    inner = _GELU_C * (g + 0.044715 * g * g * g)
    t = jnp.tanh(inner)
    val = 0.5 * g * (1.0 + t)
    grad = 0.5 * (1.0 + t) + 0.5 * g * (1.0 - t * t) * _GELU_C * (1.0 + 3.0 * 0.044715 * g * g)
    return val, grad


def _shift_down(cur, prev8, k, rows):
    n = cur.shape[0]
    rolled = pltpu.roll(cur, k, 0)
    head = jnp.tile(pltpu.roll(prev8, k, 0), (n // SUBLANES, 1))
    return jnp.where(rows < k, head, rolled)


def _shift_up(cur, next8, k, rows):
    n = cur.shape[0]
    rolled = pltpu.roll(cur, n - k, 0)
    tail = jnp.tile(pltpu.roll(next8, SUBLANES - k, 0), (n // SUBLANES, 1))
    return jnp.where(rows >= n - k, tail, rolled)


def _colsum(v):
    return jnp.sum(v, axis=0, keepdims=True)


def _matmul(name, mode, grid, operands, in_specs, out_shapes, out_specs, acc_shape, epilogue=None):
    nk = grid[2]
    n_in = len(operands)
    dims = _DIMS[mode]

    def finish(acc, extra, outs):
        res = epilogue(acc, *[e[...] for e in extra]) if epilogue is not None else (acc,)
        for o_ref, o in zip(outs, res):
            o_ref[...] = o.astype(o_ref.dtype)

    def product(a_ref, b_ref):
        return lax.dot_general(a_ref[...].astype(BF16), b_ref[...].astype(BF16), dims, preferred_element_type=F32)

    def body_single(*refs):
        finish(product(refs[0], refs[1]), refs[2:n_in], refs[n_in:])

    def body(*refs):
        extra = refs[2:n_in]
        outs = refs[n_in:-1]
        acc_ref = refs[-1]
        k = pl.program_id(2)

        @pl.when(k == 0)
        def _():
            acc_ref[...] = product(refs[0], refs[1])

        @pl.when(k > 0)
        def _():
            acc_ref[...] += product(refs[0], refs[1])

        @pl.when(k == nk - 1)
        def _():
            finish(acc_ref[...], extra, outs)

    return pl.pallas_call(
        body_single if nk == 1 else body, name=name, grid=grid, in_specs=in_specs, out_specs=out_specs,
        out_shape=out_shapes, scratch_shapes=[] if nk == 1 else [pltpu.VMEM(acc_shape, F32)],
        compiler_params=_cp(("parallel", "parallel", "arbitrary")),
    )(*operands)


def _mm_fwd_col(name, a, wfull, out_dtypes=(F32,), epilogue=None):
    s, kdim = a.shape
    _, _, cs = wfull.shape
    tm, tk, tn = _pick(s, MM_TILE), _pick(kdim, MM_TILE_K), _pick(cs, MM_TILE)
    nbj = cs // tn
    grid = (s // tm, N_CHIPS * nbj, kdim // tk)
    out_shapes = [jax.ShapeDtypeStruct((s, N_CHIPS * cs), dt) for dt in out_dtypes]
    out_specs = [pl.BlockSpec((tm, tn), lambda i, n, k: (i, n)) for _ in out_dtypes]
    return _matmul(
        name, "nn", grid, [a, wfull],
        [pl.BlockSpec((tm, tk), lambda i, n, k: (i, k)),
         pl.BlockSpec((None, tk, tn), lambda i, n, k: (n // nbj, k, n % nbj))],
        out_shapes, out_specs, (tm, tn), epilogue)


def _mm_fwd_row(name, a, w2d, out_dtype=F32):
    s, kdim = a.shape
    _, n_out = w2d.shape
    tm, tk, tn = _pick(s, MM_TILE), _pick(kdim, MM_TILE_K), _pick(n_out, MM_TILE)
    grid = (s // tm, n_out // tn, kdim // tk)
    return _matmul(
        name, "nn", grid, [a, w2d],
        [pl.BlockSpec((tm, tk), lambda i, n, k: (i, k)),
         pl.BlockSpec((tk, tn), lambda i, n, k: (k, n))],
        [jax.ShapeDtypeStruct((s, n_out), out_dtype)],
        [pl.BlockSpec((tm, tn), lambda i, n, k: (i, n))], (tm, tn))[0]


def _mm_bwd_col(name, dy, wfull, out_dtype=F32):
    s, _ = dy.shape
    _, kdim, cs = wfull.shape
    tm, tn, tk = _pick(s, MM_TILE), _pick(kdim, MM_TILE), _pick(cs, MM_TILE_K)
    nbj = cs // tk
    grid = (s // tm, kdim // tn, N_CHIPS * nbj)
    return _matmul(
        name, "nt", grid, [dy, wfull],
        [pl.BlockSpec((tm, tk), lambda i, n, k: (i, k)),
         pl.BlockSpec((None, tn, tk), lambda i, n, k: (k // nbj, n, k % nbj))],
        [jax.ShapeDtypeStruct((s, kdim), out_dtype)],
        [pl.BlockSpec((tm, tn), lambda i, n, k: (i, n))], (tm, tn))[0]


def _mm_bwd_row(name, dy, w2d, out_dtypes=(F32,), extra=None, epilogue=None):
    s, n_in = dy.shape
    kdim, _ = w2d.shape
    tm, tn, tk = _pick(s, MM_TILE), _pick(kdim, MM_TILE), _pick(n_in, MM_TILE_K)
    grid = (s // tm, kdim // tn, n_in // tk)
    operands = [dy, w2d]
    in_specs = [pl.BlockSpec((tm, tk), lambda i, n, k: (i, k)),
                pl.BlockSpec((tn, tk), lambda i, n, k: (n, k))]
    if extra is not None:
        operands.append(extra)
        in_specs.append(pl.BlockSpec((tm, tn), lambda i, n, k: (i, n)))
    return _matmul(
        name, "nt", grid, operands, in_specs,
        [jax.ShapeDtypeStruct((s, kdim), dt) for dt in out_dtypes],
        [pl.BlockSpec((tm, tn), lambda i, n, k: (i, n)) for _ in out_dtypes], (tm, tn), epilogue)


def _mm_wgrad_col(name, a, dy, cs):
    s, kdim = a.shape
    tm, tn, ts = _pick(kdim, MM_TILE), _pick(cs, MM_TILE), _pick(s, MM_TILE_K)
    nbj = cs // tn
    grid = (kdim // tm, N_CHIPS * nbj, s // ts)
    return _matmul(
        name, "tn", grid, [a, dy],
        [pl.BlockSpec((ts, tm), lambda i, n, k: (k, i)),
         pl.BlockSpec((ts, tn), lambda i, n, k: (k, n))],
        [jax.ShapeDtypeStruct((N_CHIPS, kdim, cs), BF16)],
        [pl.BlockSpec((None, tm, tn), lambda i, n, k: (n // nbj, i, n % nbj))], (tm, tn))[0]


def _mm_wgrad_row(name, a, dy):
    s, kdim = a.shape
    _, n_out = dy.shape
    tm, tn, ts = _pick(kdim, MM_TILE), _pick(n_out, MM_TILE), _pick(s, MM_TILE_K)
    grid = (kdim // tm, n_out // tn, s // ts)
    return _matmul(
        name, "tn", grid, [a, dy],
        [pl.BlockSpec((ts, tm), lambda i, n, k: (k, i)),
         pl.BlockSpec((ts, tn), lambda i, n, k: (k, n))],
        [jax.ShapeDtypeStruct((kdim, n_out), BF16)],
        [pl.BlockSpec((tm, tn), lambda i, n, k: (i, n))], (tm, tn))[0]


def _mm_wgrad_diag(name, a, dy):
    s, width = a.shape
    nb = width // LANES
    ts = _pick(s, MM_TILE)
    grid = (nb, 1, s // ts)
    return _matmul(
        name, "tn", grid, [a, dy],
        [pl.BlockSpec((ts, LANES), lambda i, n, k: (k, i)),
         pl.BlockSpec((ts, LANES), lambda i, n, k: (k, i))],
        [jax.ShapeDtypeStruct((nb, LANES, LANES), F32)],
        [pl.BlockSpec((None, LANES, LANES), lambda i, n, k: (i, 0, 0))], (LANES, LANES))[0]


def _rowspec(tr, d):
    return pl.BlockSpec((tr, d), lambda i: (i, 0))


def _vecspec(d):
    return pl.BlockSpec((1, d), lambda i: (0, 0))


def _rms(x, g):
    return x * lax.rsqrt(jnp.mean(x * x, axis=-1, keepdims=True) + NORM_EPS) * g


def _cast_into_slot(name, w, chip):
    r, c = w.shape
    tr = _pick(r, ROW_TILE)

    def body(chip_ref, w_ref, o_ref):
        o_ref[...] = w_ref[...].astype(BF16)

    grid_spec = pltpu.PrefetchScalarGridSpec(
        num_scalar_prefetch=1, grid=(r // tr,),
        in_specs=[pl.BlockSpec((tr, c), lambda i, chip_ref: (i, 0))],
        out_specs=pl.BlockSpec((None, tr, c), lambda i, chip_ref: (chip_ref[0], i, 0)))
    return pl.pallas_call(
        body, name=name, grid_spec=grid_spec, out_shape=jax.ShapeDtypeStruct((N_CHIPS, r, c), BF16),
        compiler_params=_cp(("parallel",)))(jnp.reshape(chip, (1,)).astype(jnp.int32), w)


def _rms_fwd(name, x, g):
    s, d = x.shape
    tr = _pick(s, ROW_TILE)

    def body(x_ref, g_ref, h_ref):
        h_ref[...] = _rms(x_ref[...], g_ref[...]).astype(BF16)

    return pl.pallas_call(
        body, name=name, grid=(s // tr,), in_specs=[_rowspec(tr, d), _vecspec(d)],
        out_specs=_rowspec(tr, d), out_shape=jax.ShapeDtypeStruct((s, d), BF16),
        compiler_params=_cp(("parallel",)))(x, g)


def _rms_post(name, y, g_post, res, g_next=None):
    s, d = y.shape
    tr = _pick(s, ROW_TILE)
    with_next = g_next is not None

    def body(*refs):
        if with_next:
            y_ref, gp_ref, r_ref, gn_ref, x_ref, h_ref = refs
        else:
            y_ref, gp_ref, r_ref, x_ref = refs
        xn = r_ref[...] + _rms(y_ref[...], gp_ref[...])
        x_ref[...] = xn
        if with_next:
            h_ref[...] = _rms(xn, gn_ref[...]).astype(BF16)

    operands = [y, g_post, res] + ([g_next] if with_next else [])
    in_specs = [_rowspec(tr, d), _vecspec(d), _rowspec(tr, d)] + ([_vecspec(d)] if with_next else [])
    out_shape = [jax.ShapeDtypeStruct((s, d), F32)] + ([jax.ShapeDtypeStruct((s, d), BF16)] if with_next else [])
    out_specs = [_rowspec(tr, d)] + ([_rowspec(tr, d)] if with_next else [])
    return pl.pallas_call(
        body, name=name, grid=(s // tr,), in_specs=in_specs, out_specs=out_specs, out_shape=out_shape,
        compiler_params=_cp(("parallel",)))(*operands)


def _rms_bwd(name, x, g, dy, res=None, out_dtype=F32):
    s, d = x.shape
    tr = _pick(s, ROW_TILE)
    nsteps = s // tr
    with_res = res is not None

    def body(*refs):
        if with_res:
            x_ref, g_ref, dy_ref, r_ref, dx_ref, dg_ref, acc_ref = refs
        else:
            x_ref, g_ref, dy_ref, dx_ref, dg_ref, acc_ref = refs
        i = pl.program_id(0)

        @pl.when(i == 0)
        def _():
            acc_ref[...] = jnp.zeros_like(acc_ref)

        xv = x_ref[...]
        dyv = dy_ref[...].astype(F32)
        r = lax.rsqrt(jnp.mean(xv * xv, axis=-1, keepdims=True) + NORM_EPS)
        xhat = xv * r
        gy = dyv * g_ref[...]
        dx = r * (gy - xhat * jnp.mean(gy * xhat, axis=-1, keepdims=True))
        if with_res:
            dx = dx + r_ref[...]
        dx_ref[...] = dx.astype(dx_ref.dtype)
        acc_ref[...] += jnp.sum((dyv * xhat).reshape(tr // SUBLANES, SUBLANES, d), axis=0)

        @pl.when(i == nsteps - 1)
        def _():
            dg_ref[...] = jnp.broadcast_to(_colsum(acc_ref[...]), (SUBLANES, d))

    operands = [x, g, dy] + ([res] if with_res else [])
    in_specs = [_rowspec(tr, d), _vecspec(d), _rowspec(tr, d)] + ([_rowspec(tr, d)] if with_res else [])
    dx, dg = pl.pallas_call(
        body, name=name, grid=(nsteps,), in_specs=in_specs,
        out_specs=[_rowspec(tr, d), pl.BlockSpec((SUBLANES, d), lambda i: (0, 0))],
        out_shape=[jax.ShapeDtypeStruct((s, d), out_dtype), jax.ShapeDtypeStruct((SUBLANES, d), F32)],
        scratch_shapes=[pltpu.VMEM((SUBLANES, d), F32)],
        compiler_params=_cp(("arbitrary",)))(*operands)
    return dx, dg[0:1]


def _loss_head(name, y, target):
    s, d = y.shape
    tr = _pick(s, ROW_TILE)
    nsteps = s // tr

    def body(y_ref, t_ref, dy_ref, l_ref, acc_ref):
        i = pl.program_id(0)

        @pl.when(i == 0)
        def _():
            acc_ref[...] = jnp.zeros_like(acc_ref)

        err = y_ref[...] - t_ref[...]
        dy_ref[...] = err * (1.0 / d)
        acc_ref[...] += jnp.sum((err * err).reshape(tr // SUBLANES, SUBLANES, d), axis=0)

        @pl.when(i == nsteps - 1)
        def _():
            l_ref[...] = jnp.full((SUBLANES, LANES), (0.5 / d) * jnp.sum(acc_ref[...]), F32)

    dy, l = pl.pallas_call(
        body, name=name, grid=(nsteps,), in_specs=[_rowspec(tr, d), _rowspec(tr, d)],
        out_specs=[_rowspec(tr, d), pl.BlockSpec((SUBLANES, LANES), lambda i: (0, 0))],
        out_shape=[jax.ShapeDtypeStruct((s, d), F32), jax.ShapeDtypeStruct((SUBLANES, LANES), F32)],
        scratch_shapes=[pltpu.VMEM((SUBLANES, d), F32)],
        compiler_params=_cp(("arbitrary",)))(y, target)
    return dy, l[0, 0]


def _gates(xr, wa, ba, wx, bx, lam):
    xb = xr.astype(BF16)
    r = _sigmoid(jnp.dot(xb, wa, preferred_element_type=F32) + ba)
    i = _sigmoid(jnp.dot(xb, wx, preferred_element_type=F32) + bx)
    log_a = LRU_C * r * _log_sigmoid(lam)
    a = jnp.exp(log_a)
    m = jnp.sqrt(-_expm1(2.0 * log_a))
    return r, i, a, m


def _mixer_fwd(proj, conv_a, conv_b, bias, wa_blk, ba, wx_blk, bx, lam):
    s, w5 = proj.shape
    w = w5 // 5
    nch = w // LANES
    ts = _pick(s, ROW_TILE)
    nt = s // ts

    def body(p_ref, pp_ref, ca_ref, cb_ref, bias_ref, wa_ref, ba_ref, wx_ref, bx_ref, lam_ref,
             y_ref, h_ref, a_scr, b_scr, hc_scr):
        t = pl.program_id(0)
        first = t == 0
        rows = lax.broadcasted_iota(jnp.int32, (ts, LANES), 0)

        @pl.when(first)
        def _():
            hc_scr[...] = jnp.zeros_like(hc_scr)

        def cur(comp, c):
            return p_ref[:, comp * w + c * LANES:comp * w + (c + 1) * LANES]

        def prev(comp, c):
            v = pp_ref[:, comp * w + c * LANES:comp * w + (c + 1) * LANES]
            return jnp.where(first, 0.0, v)

        for c in range(nch):
            sl = slice(c * LANES, (c + 1) * LANES)
            cx = cur(1, c) * cur(2, c)
            cxp = prev(1, c) * prev(2, c)
            wa3 = ca_ref[:, sl]
            conv = (wa3[2:3] * cx + wa3[1:2] * _shift_down(cx, cxp, 1, rows)
                    + wa3[0:1] * _shift_down(cx, cxp, 2, rows))
            y_ref[:, sl] = (cur(0, c) * conv).astype(BF16)

        for c in range(nch):
            sl = slice(c * LANES, (c + 1) * LANES)
            xb, xbp = cur(4, c), prev(4, c)
            wb4 = cb_ref[:, sl]
            xr = (wb4[3:4] * xb + wb4[2:3] * _shift_down(xb, xbp, 1, rows)
                  + wb4[1:2] * _shift_down(xb, xbp, 2, rows)
                  + wb4[0:1] * _shift_down(xb, xbp, 3, rows) + bias_ref[:, sl])
            _, i, a, m = _gates(xr, wa_ref[c], ba_ref[:, sl], wx_ref[c], bx_ref[:, sl], lam_ref[:, sl])
            a_scr[:, sl] = a
            b_scr[:, sl] = m * i * xr

        def step(r, h):
            h = a_scr[pl.ds(r, 1), :] * h + b_scr[pl.ds(r, 1), :]
            h_ref[pl.ds(r, 1), :] = h
            return h

        hc_scr[0:1, :] = lax.fori_loop(0, ts, step, hc_scr[0:1, :], unroll=8)

        for c in range(nch):
            sl = slice(c * LANES, (c + 1) * LANES)
            gel, _ = _gelu_and_grad(cur(3, c))
            y_ref[:, w + c * LANES:w + (c + 1) * LANES] = (h_ref[:, sl] * gel).astype(BF16)

    vec = lambda n: _whole((n, w))
    return pl.pallas_call(
        body, name="mixer_fwd", grid=(nt,),
        in_specs=[pl.BlockSpec((ts, w5), lambda t: (t, 0)),
                  pl.BlockSpec((SUBLANES, w5), lambda t: (jnp.maximum(t * (ts // SUBLANES) - 1, 0), 0)),
                  vec(3), vec(4), vec(1), _whole(wa_blk.shape), vec(1), _whole(wx_blk.shape), vec(1), vec(1)],
        out_specs=[pl.BlockSpec((ts, 2 * w), lambda t: (t, 0)), pl.BlockSpec((ts, w), lambda t: (t, 0))],
        out_shape=[jax.ShapeDtypeStruct((s, 2 * w), BF16), jax.ShapeDtypeStruct((s, w), F32)],
        scratch_shapes=[pltpu.VMEM((ts, w), F32), pltpu.VMEM((ts, w), F32), pltpu.VMEM((SUBLANES, w), F32)],
        compiler_params=_cp(("arbitrary",)),
    )(proj, proj, conv_a, conv_b, bias, wa_blk, ba, wx_blk, bx, lam)


_SG_CONV_A, _SG_CONV_B, _SG_BIAS, _SG_BA, _SG_BX, _SG_LAM, _SG_ROWS = 0, 3, 7, 8, 9, 10, 16


def _mixer_bwd(proj, hseq, dy, conv_a, conv_b, bias, wa_blk, ba, wx_blk, bx, lam):
    s, w5 = proj.shape
    w = w5 // 5
    nch = w // LANES
    ts = _pick(s, ROW_TILE)
    nt = s // ts
    tpb = ts // SUBLANES

    def body(p_ref, pp_ref, h_ref, hp_ref, dy_ref, ca_ref, cb_ref, bias_ref, wa_ref, ba_ref, wx_ref, bx_ref,
             lam_ref, dp_ref, xr_ref, dpa_ref, dpx_ref, sg_ref,
             a_scr, g_scr, l_scr, x_scr, r_scr, i_scr, m_scr, cl_scr, cdc_scr, cdx_scr):
        pid = pl.program_id(0)
        last = pid == 0
        first = pid == nt - 1
        rows = lax.broadcasted_iota(jnp.int32, (ts, LANES), 0)

        @pl.when(last)
        def _():
            sg_ref[...] = jnp.zeros_like(sg_ref)
            cl_scr[...] = jnp.zeros_like(cl_scr)
            cdc_scr[...] = jnp.zeros_like(cdc_scr)
            cdx_scr[...] = jnp.zeros_like(cdx_scr)

        def cur(comp, c):
            return p_ref[:, comp * w + c * LANES:comp * w + (c + 1) * LANES]

        def prev(comp, c):
            v = pp_ref[:, comp * w + c * LANES:comp * w + (c + 1) * LANES]
            return jnp.where(first, 0.0, v)

        def put(comp, c, v):
            dp_ref[:, comp * w + c * LANES:comp * w + (c + 1) * LANES] = v

        def acc(row, sl, v):
            sg_ref[row:row + 1, sl] += _colsum(v)

        for c in range(nch):
            sl = slice(c * LANES, (c + 1) * LANES)
            bg, cg, ax = cur(0, c), cur(1, c), cur(2, c)
            cx = cg * ax
            cxp = prev(1, c) * prev(2, c)
            cx1 = _shift_down(cx, cxp, 1, rows)
            cx2 = _shift_down(cx, cxp, 2, rows)
            wa3 = ca_ref[:, sl]
            conv = wa3[2:3] * cx + wa3[1:2] * cx1 + wa3[0:1] * cx2
            dya = dy_ref[:, sl]
            put(0, c, dya * conv)
            dconv = dya * bg
            nxt = cdc_scr[:, sl]
            dcx = (wa3[2:3] * dconv + wa3[1:2] * _shift_up(dconv, nxt, 1, rows)
                   + wa3[0:1] * _shift_up(dconv, nxt, 2, rows))
            cdc_scr[:, sl] = dconv[0:SUBLANES]
            put(1, c, dcx * ax)
            put(2, c, dcx * cg)
            acc(_SG_CONV_A + 2, sl, dconv * cx)
            acc(_SG_CONV_A + 1, sl, dconv * cx1)
            acc(_SG_CONV_A + 0, sl, dconv * cx2)

        for c in range(nch):
            sl = slice(c * LANES, (c + 1) * LANES)
            xb, xbp = cur(4, c), prev(4, c)
            wb4 = cb_ref[:, sl]
            xr = (wb4[3:4] * xb + wb4[2:3] * _shift_down(xb, xbp, 1, rows)
                  + wb4[1:2] * _shift_down(xb, xbp, 2, rows)
                  + wb4[0:1] * _shift_down(xb, xbp, 3, rows) + bias_ref[:, sl])
            r, i, a, m = _gates(xr, wa_ref[c], ba_ref[:, sl], wx_ref[c], bx_ref[:, sl], lam_ref[:, sl])
            gel, dgel = _gelu_and_grad(cur(3, c))
            dyb = dy_ref[:, w + c * LANES:w + (c + 1) * LANES]
            put(3, c, dyb * h_ref[:, sl] * dgel)
            g_scr[:, sl] = dyb * gel
            a_scr[:, sl] = a
            x_scr[:, sl] = xr
            r_scr[:, sl] = r
            i_scr[:, sl] = i
            m_scr[:, sl] = m

        def step(j, carry):
            r = ts - 1 - j
            lam_t = g_scr[pl.ds(r, 1), :] + carry
            l_scr[pl.ds(r, 1), :] = lam_t
            return a_scr[pl.ds(r, 1), :] * lam_t

        cl_scr[0:1, :] = lax.fori_loop(0, ts, step, cl_scr[0:1, :], unroll=8)

        for c in range(nch):
            sl = slice(c * LANES, (c + 1) * LANES)
            lam_t = l_scr[:, sl]
            hprev = _shift_down(h_ref[:, sl], jnp.where(first, 0.0, hp_ref[:, sl]), 1, rows)
            xr, r, i, m, a = x_scr[:, sl], r_scr[:, sl], i_scr[:, sl], m_scr[:, sl], a_scr[:, sl]
            da = lam_t * hprev
            dm = lam_t * i * xr
            di = lam_t * m * xr
            dxr = lam_t * m * i
            dlog_a = da * a - dm * a * a / m
            lam_p = lam_ref[:, sl]
            dr = dlog_a * (LRU_C * _log_sigmoid(lam_p))
            acc(_SG_LAM, sl, dlog_a * r * (LRU_C * _sigmoid(-lam_p)))
            dpa = dr * r * (1.0 - r)
            dpx = di * i * (1.0 - i)
            dpa_b, dpx_b = dpa.astype(BF16), dpx.astype(BF16)
            dxr = (dxr + lax.dot_general(dpa_b, wa_ref[c], _DIMS["nt"], preferred_element_type=F32)
                   + lax.dot_general(dpx_b, wx_ref[c], _DIMS["nt"], preferred_element_type=F32))
            xr_ref[:, sl] = xr.astype(BF16)
            dpa_ref[:, sl] = dpa_b
            dpx_ref[:, sl] = dpx_b
            acc(_SG_BA, sl, dpa)
            acc(_SG_BX, sl, dpx)
            acc(_SG_BIAS, sl, dxr)
            nxt = cdx_scr[:, sl]
            wb4 = cb_ref[:, sl]
            put(4, c, wb4[3:4] * dxr + wb4[2:3] * _shift_up(dxr, nxt, 1, rows)
                + wb4[1:2] * _shift_up(dxr, nxt, 2, rows) + wb4[0:1] * _shift_up(dxr, nxt, 3, rows))
            cdx_scr[:, sl] = dxr[0:SUBLANES]
            xb, xbp = cur(4, c), prev(4, c)
            acc(_SG_CONV_B + 3, sl, dxr * xb)
            acc(_SG_CONV_B + 2, sl, dxr * _shift_down(xb, xbp, 1, rows))
            acc(_SG_CONV_B + 1, sl, dxr * _shift_down(xb, xbp, 2, rows))
            acc(_SG_CONV_B + 0, sl, dxr * _shift_down(xb, xbp, 3, rows))

    blk = lambda width: pl.BlockSpec((ts, width), lambda p: (nt - 1 - p, 0))
    pre = lambda width: pl.BlockSpec(
        (SUBLANES, width), lambda p: (jnp.maximum((nt - 1 - p) * tpb - 1, 0), 0))
    vec = lambda n: _whole((n, w))
    big = lambda: pltpu.VMEM((ts, w), F32)
    small = lambda: pltpu.VMEM((SUBLANES, w), F32)
    return pl.pallas_call(
        body, name="mixer_bwd", grid=(nt,),
        in_specs=[blk(w5), pre(w5), blk(w), pre(w), blk(2 * w),
                  vec(3), vec(4), vec(1), _whole(wa_blk.shape), vec(1), _whole(wx_blk.shape), vec(1), vec(1)],
        out_specs=[blk(w5), blk(w), blk(w), blk(w), _whole((_SG_ROWS, w))],
        out_shape=[jax.ShapeDtypeStruct((s, w5), F32), jax.ShapeDtypeStruct((s, w), BF16),
                   jax.ShapeDtypeStruct((s, w), BF16), jax.ShapeDtypeStruct((s, w), BF16),
                   jax.ShapeDtypeStruct((_SG_ROWS, w), F32)],
        scratch_shapes=[big(), big(), big(), big(), big(), big(), big(), small(), small(), small()],
        compiler_params=_cp(("arbitrary",)),
    )(proj, proj, hseq, hseq, dy, conv_a, conv_b, bias, wa_blk, ba, wx_blk, bx, lam)


def _split_dot(v, tri):
    hi = v.astype(BF16)
    lo = (v - hi.astype(F32)).astype(BF16)
    return (jnp.dot(hi, tri, preferred_element_type=F32) + jnp.dot(lo, tri, preferred_element_type=F32))


def _tri(cmp):
    r = lax.broadcasted_iota(jnp.int32, (LANES, LANES), 0)
    c = lax.broadcasted_iota(jnp.int32, (LANES, LANES), 1)
    return cmp(r, c).astype(BF16)


def _lane_blocks(v):
    return [v[:, b * LANES:(b + 1) * LANES] for b in range(v.shape[1] // LANES)]


def _last_lane(v):
    return jnp.broadcast_to(v[:, LANES - 1:LANES], v.shape)


def _scores(q, kb, scale):
    return lax.dot_general(q, kb, _DIMS["nt"], preferred_element_type=F32) * scale


def _log_gates(z, diagonal):
    ls = jnp.minimum(z, 0.0) - jnp.log(1.0 + jnp.exp(-jnp.abs(z)))
    ln = ls - z
    valid = None
    if diagonal:
        valid = (lax.broadcasted_iota(jnp.int32, z.shape, 1) < lax.broadcasted_iota(jnp.int32, z.shape, 0))
        ln = jnp.where(valid, ln, 0.0)
    return ls, ln, valid


def _attn_fwd(qkv, heads):
    s = qkv.shape[0]
    dh = LANES
    tq = _pick(s, ATT_TILE)
    nq = s // tq
    nb = tq // LANES
    scale = 1.0 / math.sqrt(dh)

    hp = ATT_HEADS_PER_STEP
    groups = heads // hp
    wid = hp * dh

    def body(q_ref, k_ref, v_ref, o_ref, tot_ref, acc_scr, car_scr):
        qi = pl.program_id(1)
        acc_scr[...] = jnp.zeros_like(acc_scr)
        car_scr[...] = jnp.zeros_like(car_scr)
        tri = _tri(lambda r, c: r > c)

        def tile(kt, diagonal):
            k0 = pl.multiple_of(kt * tq, tq)
            heads_cols = [slice(hh * dh, (hh + 1) * dh) for hh in range(hp)]
            zs = [_scores(q_ref[:, cols], k_ref[pl.ds(k0, tq), cols], scale) for cols in heads_cols]
            gates = [_log_gates(z, diagonal) for z in zs]
            sfxs = [_split_dot(jnp.concatenate(_lane_blocks(ln), axis=0), tri) for _, ln, _ in gates]
            for cols, (ls, ln, valid), sfx in zip(heads_cols, gates, sfxs):
                blocks = _lane_blocks(ln)
                car = car_scr[:, cols]
                parts = [None] * nb
                for b in reversed(range(nb)):
                    sb = sfx[b * tq:(b + 1) * tq]
                    parts[b] = sb + car
                    car = car + (sb[:, 0:1] + blocks[b][:, 0:1])
                car_scr[:, cols] = car
                wgt = jnp.exp(ls + jnp.concatenate(parts, axis=1))
                if diagonal:
                    wgt = jnp.where(valid, wgt, 0.0)
                acc_scr[:, cols] += jnp.dot(
                    wgt.astype(BF16), v_ref[pl.ds(k0, tq), cols], preferred_element_type=F32)

        tile(qi, True)

        def step(j, carry):
            tile(qi - 1 - j, False)
            return carry

        lax.fori_loop(0, qi, step, 0)
        o_ref[...] = acc_scr[...].astype(BF16)
        tot_ref[...] = car_scr[...]

    return pl.pallas_call(
        body, name="attn_fwd", grid=(groups, nq),
        in_specs=[pl.BlockSpec((tq, wid), lambda h, i: (i, h)),
                  pl.BlockSpec((s, wid), lambda h, i: (0, groups + h)),
                  pl.BlockSpec((s, wid), lambda h, i: (0, 2 * groups + h))],
        out_specs=[pl.BlockSpec((tq, wid), lambda h, i: (i, h)), pl.BlockSpec((tq, wid), lambda h, i: (i, h))],
        out_shape=[jax.ShapeDtypeStruct((s, heads * dh), BF16), jax.ShapeDtypeStruct((s, heads * dh), F32)],
        scratch_shapes=[pltpu.VMEM((tq, wid), F32), pltpu.VMEM((tq, wid), F32)],
        compiler_params=_cp(("parallel", "arbitrary")),
    )(qkv, qkv, qkv)


def _attn_bwd(qkv, tot, do, heads):
    s = qkv.shape[0]
    dh = LANES
    tq = _pick(s, ATT_TILE)
    nq = s // tq
    nb = tq // LANES
    scale = 1.0 / math.sqrt(dh)

    hp = ATT_HEADS_PER_STEP
    groups = heads // hp
    wid = hp * dh

    def body(q_ref, k_ref, v_ref, tot_ref, do_ref, dq_ref, dk_ref, dv_ref,
             dq_scr, dk_scr, dv_scr, cl_scr, cg_scr):
        qi = pl.program_id(1)

        @pl.when(qi == 0)
        def _():
            dk_scr[...] = jnp.zeros_like(dk_scr)
            dv_scr[...] = jnp.zeros_like(dv_scr)

        dq_scr[...] = jnp.zeros_like(dq_scr)
        cl_scr[...] = jnp.zeros_like(cl_scr)
        cg_scr[...] = jnp.zeros_like(cg_scr)
        tri_le = _tri(lambda r, c: r <= c)
        tri_lt = _tri(lambda r, c: r < c)

        def tile(kt, diagonal):
            k0 = pl.multiple_of(kt * tq, tq)
            heads_cols = [slice(hh * dh, (hh + 1) * dh) for hh in range(hp)]
            keys = pl.ds(k0, tq)
            zs = [_scores(q_ref[:, cols], k_ref[keys, cols], scale) for cols in heads_cols]
            dws = [lax.dot_general(do_ref[:, cols], v_ref[keys, cols], _DIMS["nt"], preferred_element_type=F32)
                   for cols in heads_cols]
            gates = [_log_gates(z, diagonal) for z in zs]
            pins = [_split_dot(jnp.concatenate(_lane_blocks(ln), axis=0), tri_le) for _, ln, _ in gates]
            wgts, gs = [], []
            for cols, (ls, _, valid), pin, dw in zip(heads_cols, gates, pins, dws):
                total = tot_ref[:, cols]
                cl = cl_scr[:, cols]
                parts = []
                for b in range(nb):
                    pb = pin[b * tq:(b + 1) * tq] + cl
                    parts.append(total - pb)
                    cl = _last_lane(pb)
                cl_scr[:, cols] = cl
                wgt = jnp.exp(ls + jnp.concatenate(parts, axis=1))
                if diagonal:
                    wgt = jnp.where(valid, wgt, 0.0)
                wgts.append(wgt)
                gs.append(wgt * dw)
            pexs = [jnp.dot(jnp.concatenate(_lane_blocks(g), axis=0).astype(BF16), tri_lt,
                            preferred_element_type=F32) for g in gs]
            for cols, wgt in zip(heads_cols, wgts):
                dv_scr[keys, cols] += lax.dot_general(
                    wgt.astype(BF16), do_ref[:, cols], _DIMS["tn"], preferred_element_type=F32)
            for cols, (ls, _, valid), g, pex in zip(heads_cols, gates, gs, pexs):
                gblocks = _lane_blocks(g)
                cg = cg_scr[:, cols]
                parts = []
                for b in range(nb):
                    pb = pex[b * tq:(b + 1) * tq] + cg
                    parts.append(pb)
                    cg = _last_lane(pb + gblocks[b])
                cg_scr[:, cols] = cg
                dz = g - jnp.exp(ls) * (g + jnp.concatenate(parts, axis=1))
                if diagonal:
                    dz = jnp.where(valid, dz, 0.0)
                dz = dz.astype(BF16)
                dq_scr[:, cols] += jnp.dot(dz, k_ref[keys, cols], preferred_element_type=F32)
                dk_scr[keys, cols] += lax.dot_general(
                    dz, q_ref[:, cols], _DIMS["tn"], preferred_element_type=F32)

        def step(j, carry):
            tile(j, False)
            return carry

        lax.fori_loop(0, qi, step, 0)
        tile(qi, True)
        dq_ref[...] = (dq_scr[...] * scale).astype(BF16)

        @pl.when(qi == nq - 1)
        def _():
            dk_ref[...] = (dk_scr[...] * scale).astype(BF16)
            dv_ref[...] = dv_scr[...].astype(BF16)

    qblk = pl.BlockSpec((tq, wid), lambda h, i: (i, h))
    hblk = pl.BlockSpec((s, wid), lambda h, i: (0, h))
    out = jax.ShapeDtypeStruct((s, heads * dh), BF16)
    return pl.pallas_call(
        body, name="attn_bwd", grid=(groups, nq),
        in_specs=[qblk, pl.BlockSpec((s, wid), lambda h, i: (0, groups + h)),
                  pl.BlockSpec((s, wid), lambda h, i: (0, 2 * groups + h)), qblk, qblk],
        out_specs=[qblk, hblk, hblk], out_shape=[out, out, out],
        scratch_shapes=[pltpu.VMEM((tq, wid), F32), pltpu.VMEM((s, wid), F32), pltpu.VMEM((s, wid), F32),
                        pltpu.VMEM((tq, wid), F32), pltpu.VMEM((tq, wid), F32)],
        compiler_params=_cp(("parallel", "arbitrary")),
    )(qkv, qkv, qkv, tot, do)


def _place():
    x, y, c = lax.axis_index("x"), lax.axis_index("y"), lax.axis_index("c")
    chips = [(1 - x, y), (x, 1 - y), (1 - x, 1 - y)]
    return x, y, c, chips


def _hbm_specs(n):
    return [pl.BlockSpec(memory_space=pl.ANY) for _ in range(n)]


def _remote(src, dst, send_sem, recv_sem, dev):
    return pltpu.make_async_remote_copy(
        src_ref=src, dst_ref=dst, send_sem=send_sem, recv_sem=recv_sem, device_id=dev, device_id_type=MESH)


def _allgather_weights(name, fulls):
    n = len(fulls)

    def body(*refs):
        bufs = refs[n:2 * n]
        send_sems, recv_sems, fsend_sems, frecv_sems = refs[2 * n:]
        x, y, c, chips = _place()
        me = 2 * x + y
        sibling = (x, y, 1 - c)
        firsts = []
        for w in range(n):
            hr = bufs[w].shape[1] // 2
            mine = bufs[w].at[me, pl.ds(c * hr, hr)]
            for k, (px, py) in enumerate(chips):
                cp = _remote(mine, mine, send_sems.at[w, k], recv_sems.at[w, k], (px, py, c))
                cp.start()
                firsts.append(cp)
        passed = []
        for w in range(n):
            hr = bufs[w].shape[1] // 2
            for k, (px, py) in enumerate(chips):
                slot = bufs[w].at[2 * px + py, pl.ds(c * hr, hr)]
                _remote(slot, slot, send_sems.at[w, k], recv_sems.at[w, k], (px, py, c)).wait_recv()
                cp = _remote(slot, slot, fsend_sems.at[w, k], frecv_sems.at[w, k], sibling)
                cp.start()
                passed.append(cp)
        for w in range(n):
            hr = bufs[w].shape[1] // 2
            for k, (px, py) in enumerate(chips):
                slot = bufs[w].at[2 * px + py, pl.ds((1 - c) * hr, hr)]
                _remote(slot, slot, fsend_sems.at[w, k], frecv_sems.at[w, k], sibling).wait_recv()
        for cp in firsts + passed:
            cp.wait_send()

    sem = lambda: pltpu.SemaphoreType.DMA((n, 3))
    return pl.pallas_call(
        body, name=name, in_specs=_hbm_specs(n), out_specs=_hbm_specs(n),
        out_shape=[jax.ShapeDtypeStruct(f.shape, f.dtype) for f in fulls],
        input_output_aliases={w: w for w in range(n)},
        scratch_shapes=[sem(), sem(), sem(), sem()],
    )(*fulls)


def _handshake(peers):
    barrier = pltpu.get_barrier_semaphore()
    for dev in peers:
        pl.semaphore_signal(barrier, inc=1, device_id=dev, device_id_type=MESH)
    pl.semaphore_wait(barrier, len(peers))


def _allgather_async(name, slot_buf, collective_id):
    buf = jax.new_ref(slot_buf, memory_space=pltpu.MemorySpace.HBM)
    hr = slot_buf.shape[1] // 2
    dma = pltpu.SemaphoreType.DMA

    @pl.kernel(mesh=plsc.ScalarSubcoreMesh(axis_name="seq", num_cores=1), name=name,
               scratch_types=(dma,) * 12, compiler_params=pltpu.CompilerParams(collective_id=collective_id))
    def launch(*sems):
        send_sems, recv_sems, fsend_sems, frecv_sems = sems[0:3], sems[3:6], sems[6:9], sems[9:12]
        x, y, c, chips = _place()
        me = 2 * x + y
        sibling = (x, y, 1 - c)
        _handshake([(px, py, c) for px, py in chips] + [sibling])
        mine = buf.at[me, pl.ds(c * hr, hr)]
        firsts = []
        for k, (px, py) in enumerate(chips):
            cp = _remote(mine, mine, send_sems[k], recv_sems[k], (px, py, c))
            cp.start()
            firsts.append(cp)
        passed = []
        for k, (px, py) in enumerate(chips):
            slot = buf.at[2 * px + py, pl.ds(c * hr, hr)]
            _remote(slot, slot, send_sems[k], recv_sems[k], (px, py, c)).wait_recv()
            cp = _remote(slot, slot, fsend_sems[k], frecv_sems[k], sibling)
            cp.start()
            passed.append(cp)
        for k, (px, py) in enumerate(chips):
            slot = buf.at[2 * px + py, pl.ds((1 - c) * hr, hr)]
            _remote(slot, slot, fsend_sems[k], frecv_sems[k], sibling).wait_recv()
        for cp in firsts + passed:
            cp.wait_send()

    launch()
    return buf[...]


def _sequencer_kernel(name, n_sems, collective_id):
    return functools.partial(
        pl.kernel, mesh=plsc.ScalarSubcoreMesh(axis_name="seq", num_cores=1), name=name,
        scratch_types=(pltpu.SemaphoreType.DMA,) * n_sems,
        compiler_params=pltpu.CompilerParams(collective_id=collective_id))


def _to_sibling_async(name, slab):
    src = jax.new_ref(slab, memory_space=pltpu.MemorySpace.HBM)
    hr = slab.shape[1] // 2
    got = jax.empty_ref(jax.ShapeDtypeStruct((N_CHIPS, hr, slab.shape[2]), slab.dtype),
                        memory_space=pltpu.MemorySpace.HBM)

    @_sequencer_kernel(name, 2, COLLECTIVE_SIBLING)
    def launch(send_sem, recv_sem):
        x, y, c, _ = _place()
        _handshake([(x, y, 1 - c)])
        _remote(src.at[:, pl.ds((1 - c) * hr, hr), :], got, send_sem, recv_sem, (x, y, 1 - c)).start()
        _remote(got, got, send_sem, recv_sem, (x, y, 1 - c)).wait()

    launch()
    return src[...], got[...]


def _to_chips_async(name, part):
    src = jax.new_ref(part, memory_space=pltpu.MemorySpace.HBM)
    got = jax.empty_ref(jax.ShapeDtypeStruct((3,) + part.shape[1:], part.dtype), memory_space=pltpu.MemorySpace.HBM)

    @_sequencer_kernel(name, 6, COLLECTIVE_CHIPS)
    def launch(*sems):
        send_sems, recv_sems = sems[0:3], sems[3:6]
        x, y, c, chips = _place()
        _handshake([(px, py, c) for px, py in chips])
        cps = []
        for k, (px, py) in enumerate(chips):
            cp = _remote(src.at[2 * px + py], got.at[k], send_sems[k], recv_sems[k], (px, py, c))
            cp.start()
            cps.append(cp)
        for cp in cps:
            cp.wait()

    launch()
    return src[...], got[...]


def _join_sibling_async(name, half_filled):
    buf = jax.new_ref(half_filled, memory_space=pltpu.MemorySpace.HBM)
    hr = half_filled.shape[0] // 2

    @_sequencer_kernel(name, 2, COLLECTIVE_SIBLING)
    def launch(send_sem, recv_sem):
        x, y, c, _ = _place()
        _handshake([(x, y, 1 - c)])
        mine = buf.at[pl.ds(c * hr, hr)]
        other = buf.at[pl.ds((1 - c) * hr, hr)]
        cp = _remote(mine, mine, send_sem, recv_sem, (x, y, 1 - c))
        cp.start()
        _remote(other, other, send_sem, recv_sem, (x, y, 1 - c)).wait_recv()
        cp.wait_send()

    launch()
    return buf[...]


def _exchange_sibling_halves(name, slabs):
    n = len(slabs)

    def body(*refs):
        ins, outs = refs[:n], refs[n:2 * n]
        send_sems, recv_sems = refs[2 * n:]
        x, y, c, _ = _place()
        cps = []
        for w in range(n):
            hr = ins[w].shape[1] // 2
            cp = _remote(ins[w].at[:, pl.ds((1 - c) * hr, hr), :], outs[w], send_sems.at[w], recv_sems.at[w],
                         (x, y, 1 - c))
            cp.start()
            cps.append(cp)
        for cp in cps:
            cp.wait()

    return pl.pallas_call(
        body, name=name, in_specs=_hbm_specs(n), out_specs=_hbm_specs(n),
        out_shape=[jax.ShapeDtypeStruct((N_CHIPS, s.shape[1] // 2, s.shape[2]), s.dtype) for s in slabs],
        scratch_shapes=[pltpu.SemaphoreType.DMA((n,)), pltpu.SemaphoreType.DMA((n,))],
    )(*slabs)


def _exchange_chips(name, parts):
    n = len(parts)

    def body(*refs):
        ins, outs = refs[:n], refs[n:2 * n]
        send_sems, recv_sems = refs[2 * n:]
        x, y, c, chips = _place()
        cps = []
        for w in range(n):
            for k, (px, py) in enumerate(chips):
                cp = _remote(ins[w].at[2 * px + py], outs[w].at[k], send_sems.at[w, k], recv_sems.at[w, k],
                             (px, py, c))
                cp.start()
                cps.append(cp)
        for cp in cps:
            cp.wait()

    return pl.pallas_call(
        body, name=name, in_specs=_hbm_specs(n), out_specs=_hbm_specs(n),
        out_shape=[jax.ShapeDtypeStruct((3,) + s.shape[1:], s.dtype) for s in parts],
        scratch_shapes=[pltpu.SemaphoreType.DMA((n, 3)), pltpu.SemaphoreType.DMA((n, 3))],
    )(*parts)


def _join_sibling_halves(name, bufs):
    n = len(bufs)

    def body(*refs):
        outs = refs[n:2 * n]
        send_sems, recv_sems = refs[2 * n:]
        x, y, c, _ = _place()
        cps = []
        for w in range(n):
            hr = outs[w].shape[0] // 2
            mine = outs[w].at[pl.ds(c * hr, hr)]
            cp = _remote(mine, mine, send_sems.at[w], recv_sems.at[w], (x, y, 1 - c))
            cp.start()
            cps.append(cp)
        for w in range(n):
            hr = outs[w].shape[0] // 2
            other = outs[w].at[pl.ds((1 - c) * hr, hr)]
            _remote(other, other, send_sems.at[w], recv_sems.at[w], (x, y, 1 - c)).wait_recv()
        for cp in cps:
            cp.wait_send()

    return pl.pallas_call(
        body, name=name, in_specs=_hbm_specs(n), out_specs=_hbm_specs(n),
        out_shape=[jax.ShapeDtypeStruct(b.shape, b.dtype) for b in bufs],
        input_output_aliases={w: w for w in range(n)},
        scratch_shapes=[pltpu.SemaphoreType.DMA((n,)), pltpu.SemaphoreType.DMA((n,))],
    )(*bufs)


def _allgather_chips_small(name, v):
    r = v.shape[0]

    def body(v_ref, o_ref, send_sems, recv_sems):
        x, y, c, chips = _place()
        me = 2 * x + y
        o_ref[me] = v_ref[...]
        cps = []
        for k, (px, py) in enumerate(chips):
            cp = _remote(v_ref, o_ref.at[me], send_sems.at[k], recv_sems.at[k], (px, py, c))
            cp.start()
            cps.append(cp)
        for k, (px, py) in enumerate(chips):
            slot = o_ref.at[2 * px + py]
            _remote(slot, slot, send_sems.at[k], recv_sems.at[k], (px, py, c)).wait_recv()
        for cp in cps:
            cp.wait_send()

    return pl.pallas_call(
        body, name=name, in_specs=[pl.BlockSpec(memory_space=pltpu.VMEM)],
        out_specs=pl.BlockSpec(memory_space=pltpu.VMEM),
        out_shape=jax.ShapeDtypeStruct((N_CHIPS, r, LANES), F32),
        scratch_shapes=[pltpu.SemaphoreType.DMA((3,)), pltpu.SemaphoreType.DMA((3,))],
    )(v)


def _allreduce_small(name, v):
    r = v.shape[0]

    def body(v_ref, o_ref, all_ref, send_sems, recv_sems):
        x, y, c, _ = _place()
        me = 4 * x + 2 * y + c
        all_ref[me] = v_ref[...]
        peers = [(1 - x if k & 4 else x, 1 - y if k & 2 else y, 1 - c if k & 1 else c)
                 for k in range(1, N_DEV)]
        cps = []
        for k, dev in enumerate(peers):
            cp = _remote(v_ref, all_ref.at[me], send_sems.at[k], recv_sems.at[k], dev)
            cp.start()
            cps.append(cp)
        for k, (px, py, pc) in enumerate(peers):
            slot = all_ref.at[4 * px + 2 * py + pc]
            _remote(slot, slot, send_sems.at[k], recv_sems.at[k], (px, py, pc)).wait_recv()
        for cp in cps:
            cp.wait_send()
        total = all_ref[0]
        for d in range(1, N_DEV):
            total = total + all_ref[d]
        o_ref[...] = total

    return pl.pallas_call(
        body, name=name, in_specs=[pl.BlockSpec(memory_space=pltpu.VMEM)],
        out_specs=pl.BlockSpec(memory_space=pltpu.VMEM),
        out_shape=jax.ShapeDtypeStruct((r, LANES), F32),
        scratch_shapes=[pltpu.VMEM((N_DEV, r, LANES), F32), pltpu.SemaphoreType.DMA((N_DEV - 1,)),
                        pltpu.SemaphoreType.DMA((N_DEV - 1,))],
    )(v)


def _add_sibling(name, slabs, recv, c):
    _, r, cols = slabs.shape
    hr = r // 2
    tr = _pick(hr, ROW_TILE)
    nb = hr // tr

    def body(c_ref, a_ref, b_ref, o_ref):
        o_ref[...] = (a_ref[...].astype(F32) + b_ref[...].astype(F32)).astype(BF16)

    grid_spec = pltpu.PrefetchScalarGridSpec(
        num_scalar_prefetch=1, grid=(N_CHIPS, nb),
        in_specs=[pl.BlockSpec((None, tr, cols), lambda j, i, c_ref: (j, c_ref[0] * nb + i, 0)),
                  pl.BlockSpec((None, tr, cols), lambda j, i, c_ref: (j, i, 0))],
        out_specs=pl.BlockSpec((None, tr, cols), lambda j, i, c_ref: (j, i, 0)))
    return pl.pallas_call(
        body, name=name, grid_spec=grid_spec,
        out_shape=jax.ShapeDtypeStruct((N_CHIPS, hr, cols), BF16),
        compiler_params=_cp(("parallel", "parallel")))(jnp.reshape(c, (1,)).astype(jnp.int32), slabs, recv)


def _sum_chips(name, own, recv, chip, c):
    _, hr, cols = recv.shape
    tr = _pick(hr, ROW_TILE)
    nb = hr // tr

    def body(sc_ref, own_ref, recv_ref, o_ref):
        total = own_ref[...].astype(F32)
        for k in range(3):
            total = total + recv_ref[k].astype(F32)
        o_ref[...] = total

    grid_spec = pltpu.PrefetchScalarGridSpec(
        num_scalar_prefetch=1, grid=(nb,),
        in_specs=[pl.BlockSpec((None, tr, cols), lambda i, sc: (sc[0], i, 0)),
                  pl.BlockSpec((3, tr, cols), lambda i, sc: (0, i, 0))],
        out_specs=pl.BlockSpec((tr, cols), lambda i, sc: (sc[1] * nb + i, 0)))
    return pl.pallas_call(
        body, name=name, grid_spec=grid_spec, out_shape=jax.ShapeDtypeStruct((2 * hr, cols), F32),
        compiler_params=_cp(("parallel",)))(jnp.stack([chip, c]).astype(jnp.int32), own, recv)


def _adamw_math(w, g, m, v):
    m = ADAM_B1 * m + (1.0 - ADAM_B1) * g
    v = ADAM_B2 * v + (1.0 - ADAM_B2) * (g * g)
    m_hat = m / (1.0 - ADAM_B1 ** ADAM_STEP)
    v_hat = v / (1.0 - ADAM_B2 ** ADAM_STEP)
    delta = -ADAM_LR * (m_hat / (jnp.sqrt(v_hat) + ADAM_EPS) + ADAM_WD * w)
    return delta, m, v


def _adamw(name, w, gs, m, v):
    nl, r, cols = w.shape
    tr = _pick(r, LANES)

    def body(*refs):
        w_ref, m_ref, v_ref = refs[0:3]
        g_refs = refs[3:3 + nl]
        go_ref, d_ref, nm_ref, nv_ref = refs[3 + nl:]
        layer = pl.program_id(0)
        g = g_refs[0][...]
        for j in range(1, nl):
            g = jnp.where(layer == j, g_refs[j][...], g)
        d, nm, nv = _adamw_math(w_ref[...], g, m_ref[...], v_ref[...])
        go_ref[...] = g
        d_ref[...] = d
        nm_ref[...] = nm
        nv_ref[...] = nv

    spec3 = pl.BlockSpec((None, tr, cols), lambda l, i: (l, i, 0))
    gspec = pl.BlockSpec((tr, cols), lambda l, i: (i, 0))
    out = jax.ShapeDtypeStruct((nl, r, cols), F32)
    return pl.pallas_call(
        body, name=name, grid=(nl, r // tr), in_specs=[spec3] * 3 + [gspec] * nl, out_specs=[spec3] * 4,
        out_shape=[out] * 4, compiler_params=_cp(("parallel", "parallel")))(w, m, v, *gs)


def _adamw_small(name, groups):
    n = len(groups)
    flat = [a for grp in groups for a in grp]

    def body(*refs):
        ins, outs = refs[:4 * n], refs[4 * n:]
        for p in range(n):
            w_ref, g_ref, m_ref, v_ref = ins[4 * p:4 * p + 4]
            d, nm, nv = _adamw_math(w_ref[...], g_ref[...], m_ref[...], v_ref[...])
            outs[3 * p][...] = d
            outs[3 * p + 1][...] = nm
            outs[3 * p + 2][...] = nv

    vm = pl.BlockSpec(memory_space=pltpu.VMEM)
    out_shape = [jax.ShapeDtypeStruct(grp[0].shape, F32) for grp in groups for _ in range(3)]
    res = pl.pallas_call(
        body, name=name, in_specs=[vm] * (4 * n), out_specs=[vm] * (3 * n), out_shape=out_shape)(*flat)
    return [tuple(res[3 * p:3 * p + 3]) for p in range(n)]


def _block_diag_pairs(w):
    h, d, _ = w.shape
    z = jnp.zeros((h // 2, d, d), w.dtype)
    top = jnp.concatenate([w[0::2], z], axis=2)
    bot = jnp.concatenate([z, w[1::2]], axis=2)
    return jnp.concatenate([top, bot], axis=1).astype(BF16)


def _diag_pairs_to_heads(g, d):
    a = g[:, :d, :d]
    b = g[:, d:, d:]
    return jnp.stack([a, b], axis=1).reshape(-1, d, d)


def _rows128(a):
    flat = a.reshape(-1, LANES)
    pad = (-flat.shape[0]) % SUBLANES
    if pad:
        flat = jnp.concatenate([flat, jnp.zeros((pad, LANES), flat.dtype)], axis=0)
    return flat


def _unshard_last(g4, shape):
    g4 = g4.reshape((N_CHIPS,) + tuple(shape))
    return jnp.concatenate([g4[j] for j in range(N_CHIPS)], axis=-1)


def kernel(x, norm_gains, hyb_w_in, hyb_conv_a, hyb_conv_b, hyb_conv_b_bias, hyb_rg_w_a, hyb_rg_b_a, hyb_rg_w_x, hyb_rg_b_x, hyb_rg_lambda, hyb_w_out, sb_w_qkv, sb_w_o, mlp_w_up, mlp_w_down, loss_target, m_norm_gains, m_hyb_w_in, m_hyb_conv_a, m_hyb_conv_b, m_hyb_conv_b_bias, m_hyb_rg_w_a, m_hyb_rg_b_a, m_hyb_rg_w_x, m_hyb_rg_b_x, m_hyb_rg_lambda, m_hyb_w_out, m_sb_w_qkv, m_sb_w_o, m_mlp_w_up, m_mlp_w_down, v_norm_gains, v_hyb_w_in, v_hyb_conv_a, v_hyb_conv_b, v_hyb_conv_b_bias, v_hyb_rg_w_a, v_hyb_rg_b_a, v_hyb_rg_w_x, v_hyb_rg_b_x, v_hyb_rg_lambda, v_hyb_w_out, v_sb_w_qkv, v_sb_w_o, v_mlp_w_up, v_mlp_w_down):
    cx_ = lax.axis_index("x")
    cy_ = lax.axis_index("y")
    cc_ = lax.axis_index("c")
    chip = 2 * cx_ + cy_

    x0 = x[0]
    target = loss_target[0]
    s, d = x0.shape
    heads = SB_HEADS
    assert d // heads == LANES
    n_rg, hd = hyb_rg_w_a.shape[1], hyb_rg_w_a.shape[2]
    wmix = n_rg * hd
    assert 2 * hd == LANES

    big = {
        "hyb_w_in": hyb_w_in[0], "hyb_w_out": hyb_w_out[0], "mlp_w_up0": mlp_w_up[0], "mlp_w_down0": mlp_w_down[0],
        "sb_w_qkv": sb_w_qkv[0], "sb_w_o": sb_w_o[0], "mlp_w_up1": mlp_w_up[1], "mlp_w_down1": mlp_w_down[1],
    }
    names = list(big)
    slots = [_cast_into_slot("cast_" + k, big[k], chip) for k in names]
    full = {k: _allgather_async("allgather_" + k, slot, cid) for cid, (k, slot) in enumerate(zip(names, slots))}
    rowsharded = lambda k: full[k].reshape(-1, full[k].shape[2])

    ng_s, ca_s, cb_s = norm_gains.reshape(-1, norm_gains.shape[2]), hyb_conv_a[0], hyb_conv_b[0]
    packed = jnp.concatenate([_rows128(ng_s), _rows128(ca_s), _rows128(cb_s)], axis=0)
    gathered = _allgather_chips_small("allgather_small", packed)
    n0 = ng_s.size // LANES
    n1 = n0 + (-n0) % SUBLANES
    m0 = ca_s.size // LANES
    m1 = m0 + (-m0) % SUBLANES
    k0 = cb_s.size // LANES
    gains = _unshard_last(gathered[:, 0:n0], ng_s.shape).reshape(2, 4, 1, d)
    conv_a = _unshard_last(gathered[:, n1:n1 + m0], ca_s.shape)
    conv_b = _unshard_last(gathered[:, n1 + m1:n1 + m1 + k0], cb_s.shape)
    bias, b_a, b_x, lam = hyb_conv_b_bias, hyb_rg_b_a, hyb_rg_b_x, hyb_rg_lambda
    wa_blk = _block_diag_pairs(hyb_rg_w_a[0])
    wx_blk = _block_diag_pairs(hyb_rg_w_x[0])

    relu_sq = lambda acc: (jnp.maximum(acc, 0.0), jnp.square(jnp.maximum(acc, 0.0)))

    h1 = _rms_fwd("rms_pre0", x0, gains[0, 0])
    proj = _mm_fwd_col("proj_in", h1, full["hyb_w_in"])[0]
    ycat, hseq = _mixer_fwd(proj, conv_a, conv_b, bias, wa_blk, b_a, wx_blk, b_x, lam)
    mix0 = _mm_fwd_row("proj_out", ycat, rowsharded("hyb_w_out"))
    x1, h2 = _rms_post("rms_mix0", mix0, gains[0, 1], x0, gains[0, 2])
    u0, a0 = _mm_fwd_col("mlp_up0", h2, full["mlp_w_up0"], (BF16, BF16), relu_sq)
    mlp0 = _mm_fwd_row("mlp_down0", a0, rowsharded("mlp_w_down0"))
    x2, h3 = _rms_post("rms_mlp0", mlp0, gains[0, 3], x1, gains[1, 0])

    qkv = _mm_fwd_col("qkv", h3, full["sb_w_qkv"], (BF16,))[0]
    att, tot = _attn_fwd(qkv, heads)
    mix1 = _mm_fwd_row("attn_out", att, rowsharded("sb_w_o"))
    x3, h4 = _rms_post("rms_mix1", mix1, gains[1, 1], x2, gains[1, 2])
    u1, a1 = _mm_fwd_col("mlp_up1", h4, full["mlp_w_up1"], (BF16, BF16), relu_sq)
    mlp1 = _mm_fwd_row("mlp_down1", a1, rowsharded("mlp_w_down1"))
    (x4,) = _rms_post("rms_mlp1", mlp1, gains[1, 3], x3)

    dy, loss_local = _loss_head("loss_head", x4, target)
    loss = lax.psum(loss_local, ("x", "y", "c"))

    dgain = [[None] * 4 for _ in range(2)]
    drelu = lambda acc, u: (acc * (2.0 * u.astype(F32)),)
    stage_a, stage_b, gfull = {}, {}, {}

    def tie(main, side):
        return lax.optimization_barrier((main, side))

    def reduce_start(k, slab, main):
        main, slab = tie(main, slab)
        stage_a[k] = _to_sibling_async("grads_to_sibling_" + k, slab)
        return main

    def reduce_to_chips(k, main):
        slab, from_sibling = stage_a.pop(k)
        main, part = tie(main, _add_sibling("grads_add_" + k, slab, from_sibling, cc_))
        stage_b[k] = _to_chips_async("grads_to_chips_" + k, part)
        return main

    def after(value, token):
        return tie(value, token)[0]

    def reduce_finish(k, main):
        own, from_chips = stage_b.pop(k)
        main, half = tie(main, _sum_chips("grads_sum_" + k, after(own, main), from_chips, chip, cc_))
        gfull[k] = _join_sibling_async("grads_join_" + k, half)
        return main

    def mlp_bwd(layer, dxo, mlp_out, xin, hin, u, a):
        down, up = f"mlp_w_down{layer}", f"mlp_w_up{layer}"
        dmlp, dgain[layer][3] = _rms_bwd(f"rms_mlp{layer}_bwd", mlp_out, gains[layer, 3], dxo, out_dtype=BF16)
        wd, wu = rowsharded(down), full[up]
        dmlp = reduce_start(down, _mm_wgrad_row(f"mlp_down{layer}_wgrad", a, dmlp).reshape(N_CHIPS, -1, d), dmlp)
        du = _mm_bwd_row(f"mlp_down{layer}_bwd", dmlp, wd, (BF16,), u, drelu)[0]
        du = reduce_start(up, _mm_wgrad_col(f"mlp_up{layer}_wgrad", hin, du, wu.shape[2]), du)
        du = reduce_to_chips(down, du)
        dh = _mm_bwd_col(f"mlp_up{layer}_bwd", du, wu)
        dh = reduce_to_chips(up, dh)
        dxm, dgain[layer][2] = _rms_bwd(f"rms_premlp{layer}_bwd", xin, gains[layer, 2], dh, res=dxo)
        return dxm

    dx3 = mlp_bwd(1, dy, mlp1, x3, h4, u1, a1)
    dmix1, dgain[1][1] = _rms_bwd("rms_mix1_bwd", mix1, gains[1, 1], dx3, out_dtype=BF16)
    dmix1 = reduce_start("sb_w_o", _mm_wgrad_row("attn_out_wgrad", att, dmix1).reshape(N_CHIPS, -1, d), dmix1)
    datt = _mm_bwd_row("attn_out_bwd", dmix1, rowsharded("sb_w_o"), (BF16,))[0]
    dq, dk, dv = _attn_bwd(qkv, tot, datt, heads)
    dqkv = jnp.concatenate([dq, dk, dv], axis=1)
    dqkv = reduce_to_chips("sb_w_o", dqkv)
    dqkv = reduce_finish("mlp_w_down1", dqkv)
    dqkv = reduce_finish("mlp_w_up1", dqkv)
    dqkv = reduce_start("sb_w_qkv", _mm_wgrad_col("qkv_wgrad", h3, dqkv, full["sb_w_qkv"].shape[2]), dqkv)
    dh3 = _mm_bwd_col("qkv_bwd", dqkv, full["sb_w_qkv"])
    dh3 = reduce_to_chips("sb_w_qkv", dh3)
    dx2, dgain[1][0] = _rms_bwd("rms_pre1_bwd", x2, gains[1, 0], dh3, res=dx3)

    dx1 = mlp_bwd(0, dx2, mlp0, x1, h2, u0, a0)
    dx1 = reduce_finish("sb_w_o", dx1)
    dx1 = reduce_finish("sb_w_qkv", dx1)
    dmix0, dgain[0][1] = _rms_bwd("rms_mix0_bwd", mix0, gains[0, 1], dx1, out_dtype=BF16)
    dmix0 = reduce_start("hyb_w_out", _mm_wgrad_row("proj_out_wgrad", ycat, dmix0).reshape(N_CHIPS, -1, d), dmix0)
    dycat = _mm_bwd_row("proj_out_bwd", dmix0, rowsharded("hyb_w_out"))[0]
    dproj, xr_b, dpa_b, dpx_b, sg = _mixer_bwd(
        proj, hseq, dycat, conv_a, conv_b, bias, wa_blk, b_a, wx_blk, b_x, lam)
    dproj = reduce_to_chips("hyb_w_out", dproj)
    dproj = reduce_start("hyb_w_in", _mm_wgrad_col("proj_in_wgrad", h1, dproj, full["hyb_w_in"].shape[2]), dproj)
    dh1 = _mm_bwd_col("proj_in_bwd", dproj, full["hyb_w_in"])
    dh1 = reduce_to_chips("hyb_w_in", dh1)
    dx0, dgain[0][0] = _rms_bwd("rms_pre0_bwd", x0, gains[0, 0], dh1, res=dx1)
    dwa = _diag_pairs_to_heads(_mm_wgrad_diag("rg_w_a_wgrad", xr_b, dpa_b), hd)
    dwx = _diag_pairs_to_heads(_mm_wgrad_diag("rg_w_x_wgrad", xr_b, dpx_b), hd)

    dgains = jnp.concatenate([dgain[l][k] for l in range(2) for k in range(4)], axis=0)
    small_parts = [dgains, sg[_SG_CONV_A:_SG_CONV_A + 3], sg[_SG_CONV_B:_SG_CONV_B + 4], sg[_SG_BIAS:_SG_BIAS + 1],
                   dwa, sg[_SG_BA:_SG_BA + 1], dwx, sg[_SG_BX:_SG_BX + 1], sg[_SG_LAM:_SG_LAM + 1]]
    small_rows = [_rows128(p) for p in small_parts]
    reduced = _allreduce_small("allreduce_small", jnp.concatenate(small_rows, axis=0))
    small_full, off = [], 0
    for p, rws in zip(small_parts, small_rows):
        small_full.append(reduced[off:off + p.size // LANES].reshape(p.shape))
        off += rws.shape[0]
    g_gains, g_ca, g_cb, g_bias, g_wa, g_ba, g_wx, g_bx, g_lam = small_full

    def my_cols(g, width):
        return lax.dynamic_slice_in_dim(g, chip * width, width, axis=g.ndim - 1)

    small = [
        ("norm_gains", norm_gains, my_cols(g_gains, norm_gains.shape[2]).reshape(norm_gains.shape),
         m_norm_gains, v_norm_gains),
        ("hyb_conv_a", hyb_conv_a, my_cols(g_ca, hyb_conv_a.shape[2])[None], m_hyb_conv_a, v_hyb_conv_a),
        ("hyb_conv_b", hyb_conv_b, my_cols(g_cb, hyb_conv_b.shape[2])[None], m_hyb_conv_b, v_hyb_conv_b),
        ("hyb_conv_b_bias", hyb_conv_b_bias, g_bias, m_hyb_conv_b_bias, v_hyb_conv_b_bias),
        ("hyb_rg_w_a", hyb_rg_w_a, g_wa[None], m_hyb_rg_w_a, v_hyb_rg_w_a),
        ("hyb_rg_b_a", hyb_rg_b_a, g_ba, m_hyb_rg_b_a, v_hyb_rg_b_a),
        ("hyb_rg_w_x", hyb_rg_w_x, g_wx[None], m_hyb_rg_w_x, v_hyb_rg_w_x),
        ("hyb_rg_b_x", hyb_rg_b_x, g_bx, m_hyb_rg_b_x, v_hyb_rg_b_x),
        ("hyb_rg_lambda", hyb_rg_lambda, g_lam, m_hyb_rg_lambda, v_hyb_rg_lambda),
    ]
    to2d = lambda a: a.reshape(-1, a.shape[-1])
    small_res = _adamw_small("adamw_small", [tuple(to2d(a) for a in (w, g, m, v)) for _, w, g, m, v in small])
    out = {}
    for (nm, w, g, _, _), (dl, nmom, nvar) in zip(small, small_res):
        out[nm] = (g, dl.reshape(w.shape), nmom.reshape(w.shape), nvar.reshape(w.shape))

    stacked = {
        "mlp_w_down": (mlp_w_down, m_mlp_w_down, v_mlp_w_down, ["mlp_w_down0", "mlp_w_down1"]),
        "mlp_w_up": (mlp_w_up, m_mlp_w_up, v_mlp_w_up, ["mlp_w_up0", "mlp_w_up1"]),
        "sb_w_o": (sb_w_o, m_sb_w_o, v_sb_w_o, ["sb_w_o"]),
        "sb_w_qkv": (sb_w_qkv, m_sb_w_qkv, v_sb_w_qkv, ["sb_w_qkv"]),
        "hyb_w_out": (hyb_w_out, m_hyb_w_out, v_hyb_w_out, ["hyb_w_out"]),
        "hyb_w_in": (hyb_w_in, m_hyb_w_in, v_hyb_w_in, ["hyb_w_in"]),
    }

    def update(k, token):
        w, m, v, parts = stacked[k]
        out[k] = tuple(_adamw("adamw_" + k, w, [after(gfull[p], token) for p in parts], m, v))
        return out[k][1]

    token = small_res[0][0]
    token = update("sb_w_qkv", token)
    token = reduce_finish("mlp_w_down0", token)
    token = update("sb_w_o", token)
    token = reduce_finish("mlp_w_up0", token)
    token = update("mlp_w_down", token)
    token = reduce_finish("hyb_w_out", token)
    token = update("mlp_w_up", token)
    token = reduce_finish("hyb_w_in", token)
    token = update("hyb_w_out", token)
    update("hyb_w_in", token)

    order = ["norm_gains", "hyb_w_in", "hyb_conv_a", "hyb_conv_b", "hyb_conv_b_bias", "hyb_rg_w_a", "hyb_rg_b_a",
             "hyb_rg_w_x", "hyb_rg_b_x", "hyb_rg_lambda", "hyb_w_out", "sb_w_qkv", "sb_w_o", "mlp_w_up",
             "mlp_w_down"]
    return (loss, dx0[None], *[out[k][0] for k in order], *[out[k][1] for k in order],
            *[out[k][2] for k in order], *[out[k][3] for k in order])
```

```python
import functools
import math

import jax
import jax.numpy as jnp
from jax import lax
from jax.experimental import pallas as pl
from jax.experimental.pallas import tpu as pltpu
from jax.experimental.pallas import tpu_sc as plsc

F32 = jnp.float32
BF16 = jnp.bfloat16
MESH = pl.DeviceIdType.MESH

SB_HEADS = 16
NORM_EPS = 1e-6
LRU_C = 8.0
ADAM_LR = 0.001
ADAM_B1 = 0.9
ADAM_B2 = 0.999
ADAM_EPS = 1e-08
ADAM_WD = 0.01
ADAM_STEP = 10

LANES = 128
SUBLANES = 8
VMEM_LIMIT = 48 * 1024 * 1024
MM_TILE = 1024
MM_TILE_N = 1280
MM_TILE_K = 2048
ROW_TILE = 256
ATT_TILE = 512
ATT_HEADS_PER_STEP = 2
N_CHIPS = 4
N_DEV = 8
COLLECTIVE_SIBLING = 8
COLLECTIVE_CHIPS = 9

_DIMS = {
    "nn": (((1,), (0,)), ((), ())),
    "nt": (((1,), (1,)), ((), ())),
    "tn": (((0,), (0,)), ((), ())),
}


def _cp(sem=None, vmem=VMEM_LIMIT):
    return pltpu.CompilerParams(dimension_semantics=sem, vmem_limit_bytes=vmem)


def _pick(dim, pref):
    t = min(dim, pref)
    while dim % t:
        t -= LANES
    return t


def _whole(shape):
    nd = len(shape)
    return pl.BlockSpec(tuple(shape), lambda *_: (0,) * nd)


def _sigmoid(z):
    return 1.0 / (1.0 + jnp.exp(-z))


def _log_sigmoid(z):
    return jnp.minimum(z, 0.0) - jnp.log(1.0 + jnp.exp(-jnp.abs(z)))


def _expm1(z):
    series = z * (1.0 + z * (0.5 + z * (1.0 / 6.0 + z * (1.0 / 24.0))))
    return jnp.where(jnp.abs(z) < 0.05, series, jnp.exp(z) - 1.0)


_GELU_C = math.sqrt(2.0 / math.pi)


def _gelu_and_grad(g):
    inner = _GELU_C * (g + 0.044715 * g * g * g)
    t = jnp.tanh(inner)
    val = 0.5 * g * (1.0 + t)
    grad = 0.5 * (1.0 + t) + 0.5 * g * (1.0 - t * t) * _GELU_C * (1.0 + 3.0 * 0.044715 * g * g)
    return val, grad


def _shift_down(cur, prev8, k, rows):
    n = cur.shape[0]
    rolled = pltpu.roll(cur, k, 0)
    head = jnp.tile(pltpu.roll(prev8, k, 0), (n // SUBLANES, 1))
    return jnp.where(rows < k, head, rolled)


def _shift_up(cur, next8, k, rows):
    n = cur.shape[0]
    rolled = pltpu.roll(cur, n - k, 0)
    tail = jnp.tile(pltpu.roll(next8, SUBLANES - k, 0), (n // SUBLANES, 1))
    return jnp.where(rows >= n - k, tail, rolled)


def _colsum(v):
    return jnp.sum(v, axis=0, keepdims=True)


def _matmul(name, mode, grid, operands, in_specs, out_shapes, out_specs, acc_shape, epilogue=None):
    nk = grid[2]
    n_in = len(operands)
    dims = _DIMS[mode]

    def finish(acc, extra, outs):
        res = epilogue(acc, *[e[...] for e in extra]) if epilogue is not None else (acc,)
        for o_ref, o in zip(outs, res):
            o_ref[...] = o.astype(o_ref.dtype)

    def product(a_ref, b_ref):
        return lax.dot_general(a_ref[...].astype(BF16), b_ref[...].astype(BF16), dims, preferred_element_type=F32)

    def body_single(*refs):
        finish(product(refs[0], refs[1]), refs[2:n_in], refs[n_in:])

    def body(*refs):
        extra = refs[2:n_in]
        outs = refs[n_in:-1]
        acc_ref = refs[-1]
        k = pl.program_id(2)

        @pl.when(k == 0)
        def _():
            acc_ref[...] = product(refs[0], refs[1])

        @pl.when(k > 0)
        def _():
            acc_ref[...] += product(refs[0], refs[1])

        @pl.when(k == nk - 1)
        def _():
            finish(acc_ref[...], extra, outs)

    return pl.pallas_call(
        body_single if nk == 1 else body, name=name, grid=grid, in_specs=in_specs, out_specs=out_specs,
        out_shape=out_shapes, scratch_shapes=[] if nk == 1 else [pltpu.VMEM(acc_shape, F32)],
        compiler_params=_cp(("parallel", "parallel", "arbitrary")),
    )(*operands)


def _mm_fwd_col(name, a, wfull, out_dtypes=(F32,), epilogue=None):
    s, kdim = a.shape
    _, _, cs = wfull.shape
    tm, tk, tn = _pick(s, MM_TILE), _pick(kdim, MM_TILE_K), _pick(cs, MM_TILE_N)
    nbj = cs // tn
    grid = (s // tm, N_CHIPS * nbj, kdim // tk)
    out_shapes = [jax.ShapeDtypeStruct((s, N_CHIPS * cs), dt) for dt in out_dtypes]
    out_specs = [pl.BlockSpec((tm, tn), lambda i, n, k: (i, n)) for _ in out_dtypes]
    return _matmul(
        name, "nn", grid, [a, wfull],
        [pl.BlockSpec((tm, tk), lambda i, n, k: (i, k)),
         pl.BlockSpec((None, tk, tn), lambda i, n, k: (n // nbj, k, n % nbj))],
        out_shapes, out_specs, (tm, tn), epilogue)


def _mm_fwd_row(name, a, w2d, out_dtype=F32):
    s, kdim = a.shape
    _, n_out = w2d.shape
    tm, tk, tn = _pick(s, MM_TILE), _pick(kdim, MM_TILE_K), _pick(n_out, MM_TILE)
    grid = (s // tm, n_out // tn, kdim // tk)
    return _matmul(
        name, "nn", grid, [a, w2d],
        [pl.BlockSpec((tm, tk), lambda i, n, k: (i, k)),
         pl.BlockSpec((tk, tn), lambda i, n, k: (k, n))],
        [jax.ShapeDtypeStruct((s, n_out), out_dtype)],
        [pl.BlockSpec((tm, tn), lambda i, n, k: (i, n))], (tm, tn))[0]


def _mm_bwd_col(name, dy, wfull, out_dtype=F32):
    s, _ = dy.shape
    _, kdim, cs = wfull.shape
    tm, tn, tk = _pick(s, MM_TILE), _pick(kdim, MM_TILE), _pick(cs, MM_TILE_K)
    nbj = cs // tk
    grid = (s // tm, kdim // tn, N_CHIPS * nbj)
    return _matmul(
        name, "nt", grid, [dy, wfull],
        [pl.BlockSpec((tm, tk), lambda i, n, k: (i, k)),
         pl.BlockSpec((None, tn, tk), lambda i, n, k: (k // nbj, n, k % nbj))],
        [jax.ShapeDtypeStruct((s, kdim), out_dtype)],
        [pl.BlockSpec((tm, tn), lambda i, n, k: (i, n))], (tm, tn))[0]


def _mm_bwd_row(name, dy, w2d, out_dtypes=(F32,), extra=None, epilogue=None):
    s, n_in = dy.shape
    kdim, _ = w2d.shape
    tm, tn, tk = _pick(s, MM_TILE), _pick(kdim, MM_TILE), _pick(n_in, MM_TILE_K)
    grid = (s // tm, kdim // tn, n_in // tk)
    operands = [dy, w2d]
    in_specs = [pl.BlockSpec((tm, tk), lambda i, n, k: (i, k)),
                pl.BlockSpec((tn, tk), lambda i, n, k: (n, k))]
    if extra is not None:
        operands.append(extra)
        in_specs.append(pl.BlockSpec((tm, tn), lambda i, n, k: (i, n)))
    return _matmul(
        name, "nt", grid, operands, in_specs,
        [jax.ShapeDtypeStruct((s, kdim), dt) for dt in out_dtypes],
        [pl.BlockSpec((tm, tn), lambda i, n, k: (i, n)) for _ in out_dtypes], (tm, tn), epilogue)


def _mm_wgrad_col(name, a, dy, cs):
    s, kdim = a.shape
    tm, tn, ts = _pick(kdim, MM_TILE), _pick(cs, MM_TILE_N), _pick(s, MM_TILE_K)
    nbj = cs // tn
    grid = (kdim // tm, N_CHIPS * nbj, s // ts)
    return _matmul(
        name, "tn", grid, [a, dy],
        [pl.BlockSpec((ts, tm), lambda i, n, k: (k, i)),
         pl.BlockSpec((ts, tn), lambda i, n, k: (k, n))],
        [jax.ShapeDtypeStruct((N_CHIPS, kdim, cs), BF16)],
        [pl.BlockSpec((None, tm, tn), lambda i, n, k: (n // nbj, i, n % nbj))], (tm, tn))[0]


def _mm_wgrad_row(name, a, dy):
    s, kdim = a.shape
    _, n_out = dy.shape
    tm, tn, ts = _pick(kdim, MM_TILE), _pick(n_out, MM_TILE), _pick(s, MM_TILE_K)
    grid = (kdim // tm, n_out // tn, s // ts)
    return _matmul(
        name, "tn", grid, [a, dy],
        [pl.BlockSpec((ts, tm), lambda i, n, k: (k, i)),
         pl.BlockSpec((ts, tn), lambda i, n, k: (k, n))],
        [jax.ShapeDtypeStruct((kdim, n_out), BF16)],
        [pl.BlockSpec((tm, tn), lambda i, n, k: (i, n))], (tm, tn))[0]


def _mm_wgrad_diag(name, a, dy):
    s, width = a.shape
    nb = width // LANES
    ts = _pick(s, MM_TILE)
    grid = (nb, 1, s // ts)
    return _matmul(
        name, "tn", grid, [a, dy],
        [pl.BlockSpec((ts, LANES), lambda i, n, k: (k, i)),
         pl.BlockSpec((ts, LANES), lambda i, n, k: (k, i))],
        [jax.ShapeDtypeStruct((nb, LANES, LANES), F32)],
        [pl.BlockSpec((None, LANES, LANES), lambda i, n, k: (i, 0, 0))], (LANES, LANES))[0]


def _rowspec(tr, d):
    return pl.BlockSpec((tr, d), lambda i: (i, 0))


def _vecspec(d):
    return pl.BlockSpec((1, d), lambda i: (0, 0))


def _rms(x, g):
    return x * lax.rsqrt(jnp.mean(x * x, axis=-1, keepdims=True) + NORM_EPS) * g


def _cast_into_slot(name, w, layer, chip):
    _, r, c = w.shape
    tr = _pick(r, ROW_TILE)

    def body(chip_ref, w_ref, o_ref):
        o_ref[...] = w_ref[...].astype(BF16)

    grid_spec = pltpu.PrefetchScalarGridSpec(
        num_scalar_prefetch=1, grid=(r // tr,),
        in_specs=[pl.BlockSpec((None, tr, c), lambda i, chip_ref: (layer, i, 0))],
        out_specs=pl.BlockSpec((None, tr, c), lambda i, chip_ref: (chip_ref[0], i, 0)))
    return pl.pallas_call(
        body, name=name, grid_spec=grid_spec, out_shape=jax.ShapeDtypeStruct((N_CHIPS, r, c), BF16),
        compiler_params=_cp(("parallel",)))(jnp.reshape(chip, (1,)).astype(jnp.int32), w)


def _rms_fwd(name, x, g):
    s, d = x.shape
    tr = _pick(s, ROW_TILE)

    def body(x_ref, g_ref, h_ref):
        h_ref[...] = _rms(x_ref[...], g_ref[...]).astype(BF16)

    return pl.pallas_call(
        body, name=name, grid=(s // tr,), in_specs=[_rowspec(tr, d), _vecspec(d)],
        out_specs=_rowspec(tr, d), out_shape=jax.ShapeDtypeStruct((s, d), BF16),
        compiler_params=_cp(("parallel",)))(x, g)


def _rms_post(name, y, g_post, res, g_next=None):
    s, d = y.shape
    tr = _pick(s, ROW_TILE)
    with_next = g_next is not None

    def body(*refs):
        if with_next:
            y_ref, gp_ref, r_ref, gn_ref, x_ref, h_ref = refs
        else:
            y_ref, gp_ref, r_ref, x_ref = refs
        xn = r_ref[...] + _rms(y_ref[...], gp_ref[...])
        x_ref[...] = xn
        if with_next:
            h_ref[...] = _rms(xn, gn_ref[...]).astype(BF16)

    operands = [y, g_post, res] + ([g_next] if with_next else [])
    in_specs = [_rowspec(tr, d), _vecspec(d), _rowspec(tr, d)] + ([_vecspec(d)] if with_next else [])
    out_shape = [jax.ShapeDtypeStruct((s, d), F32)] + ([jax.ShapeDtypeStruct((s, d), BF16)] if with_next else [])
    out_specs = [_rowspec(tr, d)] + ([_rowspec(tr, d)] if with_next else [])
    return pl.pallas_call(
        body, name=name, grid=(s // tr,), in_specs=in_specs, out_specs=out_specs, out_shape=out_shape,
        compiler_params=_cp(("parallel",)))(*operands)


def _rms_bwd(name, x, g, dy, res=None, out_dtype=F32):
    s, d = x.shape
    tr = _pick(s, ROW_TILE)
    nsteps = s // tr
    with_res = res is not None

    def body(*refs):
        if with_res:
            x_ref, g_ref, dy_ref, r_ref, dx_ref, dg_ref, acc_ref = refs
        else:
            x_ref, g_ref, dy_ref, dx_ref, dg_ref, acc_ref = refs
        i = pl.program_id(0)

        @pl.when(i == 0)
        def _():
            acc_ref[...] = jnp.zeros_like(acc_ref)

        xv = x_ref[...]
        dyv = dy_ref[...].astype(F32)
        r = lax.rsqrt(jnp.mean(xv * xv, axis=-1, keepdims=True) + NORM_EPS)
        xhat = xv * r
        gy = dyv * g_ref[...]
        dx = r * (gy - xhat * jnp.mean(gy * xhat, axis=-1, keepdims=True))
        if with_res:
            dx = dx + r_ref[...]
        dx_ref[...] = dx.astype(dx_ref.dtype)
        acc_ref[...] += jnp.sum((dyv * xhat).reshape(tr // SUBLANES, SUBLANES, d), axis=0)

        @pl.when(i == nsteps - 1)
        def _():
            dg_ref[...] = jnp.broadcast_to(_colsum(acc_ref[...]), (SUBLANES, d))

    operands = [x, g, dy] + ([res] if with_res else [])
    in_specs = [_rowspec(tr, d), _vecspec(d), _rowspec(tr, d)] + ([_rowspec(tr, d)] if with_res else [])
    dx, dg = pl.pallas_call(
        body, name=name, grid=(nsteps,), in_specs=in_specs,
        out_specs=[_rowspec(tr, d), pl.BlockSpec((SUBLANES, d), lambda i: (0, 0))],
        out_shape=[jax.ShapeDtypeStruct((s, d), out_dtype), jax.ShapeDtypeStruct((SUBLANES, d), F32)],
        scratch_shapes=[pltpu.VMEM((SUBLANES, d), F32)],
        compiler_params=_cp(("arbitrary",)))(*operands)
    return dx, dg[0:1]


def _loss_head(name, y, target):
    s, d = y.shape
    tr = _pick(s, ROW_TILE)
    nsteps = s // tr

    def body(y_ref, t_ref, dy_ref, l_ref, acc_ref):
        i = pl.program_id(0)

        @pl.when(i == 0)
        def _():
            acc_ref[...] = jnp.zeros_like(acc_ref)

        err = y_ref[...] - t_ref[...]
        dy_ref[...] = err * (1.0 / d)
        acc_ref[...] += jnp.sum((err * err).reshape(tr // SUBLANES, SUBLANES, d), axis=0)

        @pl.when(i == nsteps - 1)
        def _():
            l_ref[...] = jnp.full((SUBLANES, LANES), (0.5 / d) * jnp.sum(acc_ref[...]), F32)

    dy, l = pl.pallas_call(
        body, name=name, grid=(nsteps,), in_specs=[_rowspec(tr, d), _rowspec(tr, d)],
        out_specs=[_rowspec(tr, d), pl.BlockSpec((SUBLANES, LANES), lambda i: (0, 0))],
        out_shape=[jax.ShapeDtypeStruct((s, d), F32), jax.ShapeDtypeStruct((SUBLANES, LANES), F32)],
        scratch_shapes=[pltpu.VMEM((SUBLANES, d), F32)],
        compiler_params=_cp(("arbitrary",)))(y, target)
    return dy, l[0, 0]


def _gates(xr, wa, ba, wx, bx, lam):
    xb = xr.astype(BF16)
    r = _sigmoid(jnp.dot(xb, wa, preferred_element_type=F32) + ba)
    i = _sigmoid(jnp.dot(xb, wx, preferred_element_type=F32) + bx)
    log_a = LRU_C * r * _log_sigmoid(lam)
    a = jnp.exp(log_a)
    m = jnp.sqrt(-_expm1(2.0 * log_a))
    return r, i, a, m


def _mixer_fwd(proj, conv_a, conv_b, bias, wa_blk, ba, wx_blk, bx, lam):
    s, w5 = proj.shape
    w = w5 // 5
    nch = w // LANES
    ts = _pick(s, ROW_TILE)
    nt = s // ts

    def body(p_ref, pp_ref, ca_ref, cb_ref, bias_ref, wa_ref, ba_ref, wx_ref, bx_ref, lam_ref,
             y_ref, h_ref, a_scr, b_scr, hc_scr):
        t = pl.program_id(0)
        first = t == 0
        rows = lax.broadcasted_iota(jnp.int32, (ts, LANES), 0)

        @pl.when(first)
        def _():
            hc_scr[...] = jnp.zeros_like(hc_scr)

        def cur(comp, c):
            return p_ref[:, comp * w + c * LANES:comp * w + (c + 1) * LANES]

        def prev(comp, c):
            v = pp_ref[:, comp * w + c * LANES:comp * w + (c + 1) * LANES]
            return jnp.where(first, 0.0, v)

        for c in range(nch):
            sl = slice(c * LANES, (c + 1) * LANES)
            cx = cur(1, c) * cur(2, c)
            cxp = prev(1, c) * prev(2, c)
            wa3 = ca_ref[:, sl]
            conv = (wa3[2:3] * cx + wa3[1:2] * _shift_down(cx, cxp, 1, rows)
                    + wa3[0:1] * _shift_down(cx, cxp, 2, rows))
            y_ref[:, sl] = (cur(0, c) * conv).astype(BF16)

        for c in range(nch):
            sl = slice(c * LANES, (c + 1) * LANES)
            xb, xbp = cur(4, c), prev(4, c)
            wb4 = cb_ref[:, sl]
            xr = (wb4[3:4] * xb + wb4[2:3] * _shift_down(xb, xbp, 1, rows)
                  + wb4[1:2] * _shift_down(xb, xbp, 2, rows)
                  + wb4[0:1] * _shift_down(xb, xbp, 3, rows) + bias_ref[:, sl])
            _, i, a, m = _gates(xr, wa_ref[c], ba_ref[:, sl], wx_ref[c], bx_ref[:, sl], lam_ref[:, sl])
            a_scr[:, sl] = a
            b_scr[:, sl] = m * i * xr

        def step(r, h):
            h = a_scr[pl.ds(r, 1), :] * h + b_scr[pl.ds(r, 1), :]
            h_ref[pl.ds(r, 1), :] = h
            return h

        hc_scr[0:1, :] = lax.fori_loop(0, ts, step, hc_scr[0:1, :], unroll=8)

        for c in range(nch):
            sl = slice(c * LANES, (c + 1) * LANES)
            gel, _ = _gelu_and_grad(cur(3, c))
            y_ref[:, w + c * LANES:w + (c + 1) * LANES] = (h_ref[:, sl] * gel).astype(BF16)

    vec = lambda n: _whole((n, w))
    return pl.pallas_call(
        body, name="mixer_fwd", grid=(nt,),
        in_specs=[pl.BlockSpec((ts, w5), lambda t: (t, 0)),
                  pl.BlockSpec((SUBLANES, w5), lambda t: (jnp.maximum(t * (ts // SUBLANES) - 1, 0), 0)),
                  vec(3), vec(4), vec(1), _whole(wa_blk.shape), vec(1), _whole(wx_blk.shape), vec(1), vec(1)],
        out_specs=[pl.BlockSpec((ts, 2 * w), lambda t: (t, 0)), pl.BlockSpec((ts, w), lambda t: (t, 0))],
        out_shape=[jax.ShapeDtypeStruct((s, 2 * w), BF16), jax.ShapeDtypeStruct((s, w), F32)],
        scratch_shapes=[pltpu.VMEM((ts, w), F32), pltpu.VMEM((ts, w), F32), pltpu.VMEM((SUBLANES, w), F32)],
        compiler_params=_cp(("arbitrary",)),
    )(proj, proj, conv_a, conv_b, bias, wa_blk, ba, wx_blk, bx, lam)


_SG_CONV_A, _SG_CONV_B, _SG_BIAS, _SG_BA, _SG_BX, _SG_LAM, _SG_ROWS = 0, 3, 7, 8, 9, 10, 16


def _mixer_bwd(proj, hseq, dy, conv_a, conv_b, bias, wa_blk, ba, wx_blk, bx, lam):
    s, w5 = proj.shape
    w = w5 // 5
    nch = w // LANES
    ts = _pick(s, ROW_TILE)
    nt = s // ts
    tpb = ts // SUBLANES

    def body(p_ref, pp_ref, h_ref, hp_ref, dy_ref, ca_ref, cb_ref, bias_ref, wa_ref, ba_ref, wx_ref, bx_ref,
             lam_ref, dp_ref, xr_ref, dpa_ref, dpx_ref, sg_ref,
             a_scr, g_scr, l_scr, x_scr, r_scr, i_scr, m_scr, cl_scr, cdc_scr, cdx_scr):
        pid = pl.program_id(0)
        last = pid == 0
        first = pid == nt - 1
        rows = lax.broadcasted_iota(jnp.int32, (ts, LANES), 0)

        @pl.when(last)
        def _():
            sg_ref[...] = jnp.zeros_like(sg_ref)
            cl_scr[...] = jnp.zeros_like(cl_scr)
            cdc_scr[...] = jnp.zeros_like(cdc_scr)
            cdx_scr[...] = jnp.zeros_like(cdx_scr)

        def cur(comp, c):
            return p_ref[:, comp * w + c * LANES:comp * w + (c + 1) * LANES]

        def prev(comp, c):
            v = pp_ref[:, comp * w + c * LANES:comp * w + (c + 1) * LANES]
            return jnp.where(first, 0.0, v)

        def put(comp, c, v):
            dp_ref[:, comp * w + c * LANES:comp * w + (c + 1) * LANES] = v.astype(dp_ref.dtype)

        def acc(row, sl, v):
            sg_ref[row:row + 1, sl] += _colsum(v)

        for c in range(nch):
            sl = slice(c * LANES, (c + 1) * LANES)
            bg, cg, ax = cur(0, c), cur(1, c), cur(2, c)
            cx = cg * ax
            cxp = prev(1, c) * prev(2, c)
            cx1 = _shift_down(cx, cxp, 1, rows)
            cx2 = _shift_down(cx, cxp, 2, rows)
            wa3 = ca_ref[:, sl]
            conv = wa3[2:3] * cx + wa3[1:2] * cx1 + wa3[0:1] * cx2
            dya = dy_ref[:, sl]
            put(0, c, dya * conv)
            dconv = dya * bg
            nxt = cdc_scr[:, sl]
            dcx = (wa3[2:3] * dconv + wa3[1:2] * _shift_up(dconv, nxt, 1, rows)
                   + wa3[0:1] * _shift_up(dconv, nxt, 2, rows))
            cdc_scr[:, sl] = dconv[0:SUBLANES]
            put(1, c, dcx * ax)
            put(2, c, dcx * cg)
            acc(_SG_CONV_A + 2, sl, dconv * cx)
            acc(_SG_CONV_A + 1, sl, dconv * cx1)
            acc(_SG_CONV_A + 0, sl, dconv * cx2)

        for c in range(nch):
            sl = slice(c * LANES, (c + 1) * LANES)
            xb, xbp = cur(4, c), prev(4, c)
            wb4 = cb_ref[:, sl]
            xr = (wb4[3:4] * xb + wb4[2:3] * _shift_down(xb, xbp, 1, rows)
                  + wb4[1:2] * _shift_down(xb, xbp, 2, rows)
                  + wb4[0:1] * _shift_down(xb, xbp, 3, rows) + bias_ref[:, sl])
            r, i, a, m = _gates(xr, wa_ref[c], ba_ref[:, sl], wx_ref[c], bx_ref[:, sl], lam_ref[:, sl])
            gel, dgel = _gelu_and_grad(cur(3, c))
            dyb = dy_ref[:, w + c * LANES:w + (c + 1) * LANES]
            put(3, c, dyb * h_ref[:, sl] * dgel)
            g_scr[:, sl] = dyb * gel
            a_scr[:, sl] = a
            x_scr[:, sl] = xr
            r_scr[:, sl] = r
            i_scr[:, sl] = i
            m_scr[:, sl] = m

        def step(j, carry):
            r = ts - 1 - j
            lam_t = g_scr[pl.ds(r, 1), :] + carry
            l_scr[pl.ds(r, 1), :] = lam_t
            return a_scr[pl.ds(r, 1), :] * lam_t

        cl_scr[0:1, :] = lax.fori_loop(0, ts, step, cl_scr[0:1, :], unroll=8)

        for c in range(nch):
            sl = slice(c * LANES, (c + 1) * LANES)
            lam_t = l_scr[:, sl]
            hprev = _shift_down(h_ref[:, sl], jnp.where(first, 0.0, hp_ref[:, sl]), 1, rows)
            xr, r, i, m, a = x_scr[:, sl], r_scr[:, sl], i_scr[:, sl], m_scr[:, sl], a_scr[:, sl]
            da = lam_t * hprev
            dm = lam_t * i * xr
            di = lam_t * m * xr
            dxr = lam_t * m * i
            dlog_a = da * a - dm * a * a / m
            lam_p = lam_ref[:, sl]
            dr = dlog_a * (LRU_C * _log_sigmoid(lam_p))
            acc(_SG_LAM, sl, dlog_a * r * (LRU_C * _sigmoid(-lam_p)))
            dpa = dr * r * (1.0 - r)
            dpx = di * i * (1.0 - i)
            dpa_b, dpx_b = dpa.astype(BF16), dpx.astype(BF16)
            dxr = (dxr + lax.dot_general(dpa_b, wa_ref[c], _DIMS["nt"], preferred_element_type=F32)
                   + lax.dot_general(dpx_b, wx_ref[c], _DIMS["nt"], preferred_element_type=F32))
            xr_ref[:, sl] = xr.astype(BF16)
            dpa_ref[:, sl] = dpa_b
            dpx_ref[:, sl] = dpx_b
            acc(_SG_BA, sl, dpa)
            acc(_SG_BX, sl, dpx)
            acc(_SG_BIAS, sl, dxr)
            nxt = cdx_scr[:, sl]
            wb4 = cb_ref[:, sl]
            put(4, c, wb4[3:4] * dxr + wb4[2:3] * _shift_up(dxr, nxt, 1, rows)
                + wb4[1:2] * _shift_up(dxr, nxt, 2, rows) + wb4[0:1] * _shift_up(dxr, nxt, 3, rows))
            cdx_scr[:, sl] = dxr[0:SUBLANES]
            xb, xbp = cur(4, c), prev(4, c)
            acc(_SG_CONV_B + 3, sl, dxr * xb)
            acc(_SG_CONV_B + 2, sl, dxr * _shift_down(xb, xbp, 1, rows))
            acc(_SG_CONV_B + 1, sl, dxr * _shift_down(xb, xbp, 2, rows))
            acc(_SG_CONV_B + 0, sl, dxr * _shift_down(xb, xbp, 3, rows))

    blk = lambda width: pl.BlockSpec((ts, width), lambda p: (nt - 1 - p, 0))
    pre = lambda width: pl.BlockSpec(
        (SUBLANES, width), lambda p: (jnp.maximum((nt - 1 - p) * tpb - 1, 0), 0))
    vec = lambda n: _whole((n, w))
    big = lambda: pltpu.VMEM((ts, w), F32)
    small = lambda: pltpu.VMEM((SUBLANES, w), F32)
    return pl.pallas_call(
        body, name="mixer_bwd", grid=(nt,),
        in_specs=[blk(w5), pre(w5), blk(w), pre(w), blk(2 * w),
                  vec(3), vec(4), vec(1), _whole(wa_blk.shape), vec(1), _whole(wx_blk.shape), vec(1), vec(1)],
        out_specs=[blk(w5), blk(w), blk(w), blk(w), _whole((_SG_ROWS, w))],
        out_shape=[jax.ShapeDtypeStruct((s, w5), BF16), jax.ShapeDtypeStruct((s, w), BF16),
                   jax.ShapeDtypeStruct((s, w), BF16), jax.ShapeDtypeStruct((s, w), BF16),
                   jax.ShapeDtypeStruct((_SG_ROWS, w), F32)],
        scratch_shapes=[big(), big(), big(), big(), big(), big(), big(), small(), small(), small()],
        compiler_params=_cp(("arbitrary",)),
    )(proj, proj, hseq, hseq, dy, conv_a, conv_b, bias, wa_blk, ba, wx_blk, bx, lam)


def _split_dot(v, tri2):
    hi = v.astype(BF16)
    lo = (v - hi.astype(F32)).astype(BF16)
    return jnp.dot(jnp.concatenate([hi, lo], axis=1), tri2, preferred_element_type=F32)


def _tri(cmp):
    r = lax.broadcasted_iota(jnp.int32, (LANES, LANES), 0)
    c = lax.broadcasted_iota(jnp.int32, (LANES, LANES), 1)
    return cmp(r, c).astype(BF16)


def _lane_blocks(v):
    return [v[:, b * LANES:(b + 1) * LANES] for b in range(v.shape[1] // LANES)]


def _last_lane(v):
    return jnp.broadcast_to(v[:, LANES - 1:LANES], v.shape)


def _scores(q, kb, scale):
    return lax.dot_general(q, kb, _DIMS["nt"], preferred_element_type=F32) * scale


def _log_gates(z, diagonal):
    ls = jnp.minimum(z, 0.0) - jnp.log(1.0 + jnp.exp(-jnp.abs(z)))
    ln = ls - z
    valid = None
    if diagonal:
        valid = (lax.broadcasted_iota(jnp.int32, z.shape, 1) < lax.broadcasted_iota(jnp.int32, z.shape, 0))
        ln = jnp.where(valid, ln, 0.0)
    return ls, ln, valid


def _attn_fwd(qkv, heads):
    s = qkv.shape[0]
    dh = LANES
    tq = _pick(s, ATT_TILE)
    nq = s // tq
    nb = tq // LANES
    scale = 1.0 / math.sqrt(dh)

    hp = ATT_HEADS_PER_STEP
    groups = heads // hp
    wid = hp * dh

    def body(q_ref, k_ref, v_ref, o_ref, tot_ref, acc_scr, car_scr):
        qi = pl.program_id(1)
        acc_scr[...] = jnp.zeros_like(acc_scr)
        car_scr[...] = jnp.zeros_like(car_scr)
        tri = _tri(lambda r, c: r > c)
        tri = jnp.concatenate([tri, tri], axis=0)

        def tile(kt, diagonal):
            k0 = pl.multiple_of(kt * tq, tq)
            heads_cols = [slice(hh * dh, (hh + 1) * dh) for hh in range(hp)]
            zs = [_scores(q_ref[:, cols], k_ref[pl.ds(k0, tq), cols], scale) for cols in heads_cols]
            gates = [_log_gates(z, diagonal) for z in zs]
            sfxs = [_split_dot(jnp.concatenate(_lane_blocks(ln), axis=0), tri) for _, ln, _ in gates]
            for cols, (ls, ln, valid), sfx in zip(heads_cols, gates, sfxs):
                blocks = _lane_blocks(ln)
                car = car_scr[:, cols]
                parts = [None] * nb
                for b in reversed(range(nb)):
                    sb = sfx[b * tq:(b + 1) * tq]
                    parts[b] = sb + car
                    car = car + (sb[:, 0:1] + blocks[b][:, 0:1])
                car_scr[:, cols] = car
                wgt = jnp.exp(ls + jnp.concatenate(parts, axis=1))
                if diagonal:
                    wgt = jnp.where(valid, wgt, 0.0)
                acc_scr[:, cols] += jnp.dot(
                    wgt.astype(BF16), v_ref[pl.ds(k0, tq), cols], preferred_element_type=F32)

        tile(qi, True)

        def step(j, carry):
            tile(qi - 1 - j, False)
            return carry

        lax.fori_loop(0, qi, step, 0)
        o_ref[...] = acc_scr[...].astype(BF16)
        tot_ref[...] = car_scr[...]

    return pl.pallas_call(
        body, name="attn_fwd", grid=(groups, nq),
        in_specs=[pl.BlockSpec((tq, wid), lambda h, i: (i, h)),
                  pl.BlockSpec((s, wid), lambda h, i: (0, groups + h)),
                  pl.BlockSpec((s, wid), lambda h, i: (0, 2 * groups + h))],
        out_specs=[pl.BlockSpec((tq, wid), lambda h, i: (i, h)), pl.BlockSpec((tq, wid), lambda h, i: (i, h))],
        out_shape=[jax.ShapeDtypeStruct((s, heads * dh), BF16), jax.ShapeDtypeStruct((s, heads * dh), F32)],
        scratch_shapes=[pltpu.VMEM((tq, wid), F32), pltpu.VMEM((tq, wid), F32)],
        compiler_params=_cp(("parallel", "arbitrary")),
    )(qkv, qkv, qkv)


def _attn_bwd(qkv, tot, do, heads):
    s = qkv.shape[0]
    dh = LANES
    tq = _pick(s, ATT_TILE)
    nq = s // tq
    nb = tq // LANES
    scale = 1.0 / math.sqrt(dh)

    hp = ATT_HEADS_PER_STEP
    groups = heads // hp
    wid = hp * dh

    def body(q_ref, k_ref, v_ref, tot_ref, do_ref, dq_ref, dk_ref, dv_ref,
             dq_scr, dk_scr, dv_scr, cl_scr, cg_scr):
        qi = pl.program_id(1)

        @pl.when(qi == 0)
        def _():
            dk_scr[...] = jnp.zeros_like(dk_scr)
            dv_scr[...] = jnp.zeros_like(dv_scr)

        dq_scr[...] = jnp.zeros_like(dq_scr)
        cl_scr[...] = jnp.zeros_like(cl_scr)
        cg_scr[...] = jnp.zeros_like(cg_scr)
        tri_le = _tri(lambda r, c: r <= c)
        tri_le = jnp.concatenate([tri_le, tri_le], axis=0)
        tri_lt = _tri(lambda r, c: r < c)

        def tile(kt, diagonal):
            k0 = pl.multiple_of(kt * tq, tq)
            heads_cols = [slice(hh * dh, (hh + 1) * dh) for hh in range(hp)]
            keys = pl.ds(k0, tq)
            zs = [_scores(q_ref[:, cols], k_ref[keys, cols], scale) for cols in heads_cols]
            dws = [lax.dot_general(do_ref[:, cols], v_ref[keys, cols], _DIMS["nt"], preferred_element_type=F32)
                   for cols in heads_cols]
            gates = [_log_gates(z, diagonal) for z in zs]
            pins = [_split_dot(jnp.concatenate(_lane_blocks(ln), axis=0), tri_le) for _, ln, _ in gates]
            wgts, gs = [], []
            for cols, (ls, _, valid), pin, dw in zip(heads_cols, gates, pins, dws):
                total = tot_ref[:, cols]
                cl = cl_scr[:, cols]
                parts = []
                for b in range(nb):
                    pb = pin[b * tq:(b + 1) * tq] + cl
                    parts.append(total - pb)
                    cl = _last_lane(pb)
                cl_scr[:, cols] = cl
                wgt = jnp.exp(ls + jnp.concatenate(parts, axis=1))
                if diagonal:
                    wgt = jnp.where(valid, wgt, 0.0)
                wgts.append(wgt)
                gs.append(wgt * dw)
            pexs = [jnp.dot(jnp.concatenate(_lane_blocks(g), axis=0).astype(BF16), tri_lt,
                            preferred_element_type=F32) for g in gs]
            for cols, wgt in zip(heads_cols, wgts):
                dv_scr[keys, cols] += lax.dot_general(
                    wgt.astype(BF16), do_ref[:, cols], _DIMS["tn"], preferred_element_type=F32)
            for cols, (ls, _, valid), g, pex in zip(heads_cols, gates, gs, pexs):
                gblocks = _lane_blocks(g)
                cg = cg_scr[:, cols]
                parts = []
                for b in range(nb):
                    pb = pex[b * tq:(b + 1) * tq] + cg
                    parts.append(pb)
                    cg = _last_lane(pb + gblocks[b])
                cg_scr[:, cols] = cg
                dz = g - jnp.exp(ls) * (g + jnp.concatenate(parts, axis=1))
                if diagonal:
                    dz = jnp.where(valid, dz, 0.0)
                dz = dz.astype(BF16)
                dq_scr[:, cols] += jnp.dot(dz, k_ref[keys, cols], preferred_element_type=F32)
                dk_scr[keys, cols] += lax.dot_general(
                    dz, q_ref[:, cols], _DIMS["tn"], preferred_element_type=F32)

        def step(j, carry):
            tile(j, False)
            return carry

        lax.fori_loop(0, qi, step, 0)
        tile(qi, True)
        dq_ref[...] = (dq_scr[...] * scale).astype(BF16)

        @pl.when(qi == nq - 1)
        def _():
            dk_ref[...] = (dk_scr[...] * scale).astype(BF16)
            dv_ref[...] = dv_scr[...].astype(BF16)

    qblk = pl.BlockSpec((tq, wid), lambda h, i: (i, h))
    hblk = pl.BlockSpec((s, wid), lambda h, i: (0, h))
    out = jax.ShapeDtypeStruct((s, heads * dh), BF16)
    return pl.pallas_call(
        body, name="attn_bwd", grid=(groups, nq),
        in_specs=[qblk, pl.BlockSpec((s, wid), lambda h, i: (0, groups + h)),
                  pl.BlockSpec((s, wid), lambda h, i: (0, 2 * groups + h)), qblk, qblk],
        out_specs=[qblk, hblk, hblk], out_shape=[out, out, out],
        scratch_shapes=[pltpu.VMEM((tq, wid), F32), pltpu.VMEM((s, wid), F32), pltpu.VMEM((s, wid), F32),
                        pltpu.VMEM((tq, wid), F32), pltpu.VMEM((tq, wid), F32)],
        compiler_params=_cp(("parallel", "arbitrary")),
    )(qkv, qkv, qkv, tot, do)


def _place():
    x, y, c = lax.axis_index("x"), lax.axis_index("y"), lax.axis_index("c")
    chips = [(1 - x, y), (x, 1 - y), (1 - x, 1 - y)]
    return x, y, c, chips


def _hbm_specs(n):
    return [pl.BlockSpec(memory_space=pl.ANY) for _ in range(n)]


def _remote(src, dst, send_sem, recv_sem, dev):
    return pltpu.make_async_remote_copy(
        src_ref=src, dst_ref=dst, send_sem=send_sem, recv_sem=recv_sem, device_id=dev, device_id_type=MESH)


def _allgather_weights(name, fulls):
    n = len(fulls)

    def body(*refs):
        bufs = refs[n:2 * n]
        send_sems, recv_sems, fsend_sems, frecv_sems = refs[2 * n:]
        x, y, c, chips = _place()
        me = 2 * x + y
        sibling = (x, y, 1 - c)
        firsts = []
        for w in range(n):
            hr = bufs[w].shape[1] // 2
            mine = bufs[w].at[me, pl.ds(c * hr, hr)]
            for k, (px, py) in enumerate(chips):
                cp = _remote(mine, mine, send_sems.at[w, k], recv_sems.at[w, k], (px, py, c))
                cp.start()
                firsts.append(cp)
        passed = []
        for w in range(n):
            hr = bufs[w].shape[1] // 2
            for k, (px, py) in enumerate(chips):
                slot = bufs[w].at[2 * px + py, pl.ds(c * hr, hr)]
                _remote(slot, slot, send_sems.at[w, k], recv_sems.at[w, k], (px, py, c)).wait_recv()
                cp = _remote(slot, slot, fsend_sems.at[w, k], frecv_sems.at[w, k], sibling)
                cp.start()
                passed.append(cp)
        for w in range(n):
            hr = bufs[w].shape[1] // 2
            for k, (px, py) in enumerate(chips):
                slot = bufs[w].at[2 * px + py, pl.ds((1 - c) * hr, hr)]
                _remote(slot, slot, fsend_sems.at[w, k], frecv_sems.at[w, k], sibling).wait_recv()
        for cp in firsts + passed:
            cp.wait_send()

    sem = lambda: pltpu.SemaphoreType.DMA((n, 3))
    return pl.pallas_call(
        body, name=name, in_specs=_hbm_specs(n), out_specs=_hbm_specs(n),
        out_shape=[jax.ShapeDtypeStruct(f.shape, f.dtype) for f in fulls],
        input_output_aliases={w: w for w in range(n)},
        scratch_shapes=[sem(), sem(), sem(), sem()],
    )(*fulls)


def _handshake(peers):
    barrier = pltpu.get_barrier_semaphore()
    for dev in peers:
        pl.semaphore_signal(barrier, inc=1, device_id=dev, device_id_type=MESH)
    pl.semaphore_wait(barrier, len(peers))


def _allgather_async(name, slot_buf, collective_id):
    buf = jax.new_ref(slot_buf, memory_space=pltpu.MemorySpace.HBM)
    hr = slot_buf.shape[1] // 2
    dma = pltpu.SemaphoreType.DMA

    @pl.kernel(mesh=plsc.ScalarSubcoreMesh(axis_name="seq", num_cores=1), name=name,
               scratch_types=(dma,) * 12, compiler_params=pltpu.CompilerParams(collective_id=collective_id))
    def launch(*sems):
        send_sems, recv_sems, fsend_sems, frecv_sems = sems[0:3], sems[3:6], sems[6:9], sems[9:12]
        x, y, c, chips = _place()
        me = 2 * x + y
        sibling = (x, y, 1 - c)
        _handshake([(px, py, c) for px, py in chips] + [sibling])
        mine = buf.at[me, pl.ds(c * hr, hr)]
        firsts = []
        for k, (px, py) in enumerate(chips):
            cp = _remote(mine, mine, send_sems[k], recv_sems[k], (px, py, c))
            cp.start()
            firsts.append(cp)
        passed = []
        for k, (px, py) in enumerate(chips):
            slot = buf.at[2 * px + py, pl.ds(c * hr, hr)]
            _remote(slot, slot, send_sems[k], recv_sems[k], (px, py, c)).wait_recv()
            cp = _remote(slot, slot, fsend_sems[k], frecv_sems[k], sibling)
            cp.start()
            passed.append(cp)
        for k, (px, py) in enumerate(chips):
            slot = buf.at[2 * px + py, pl.ds((1 - c) * hr, hr)]
            _remote(slot, slot, fsend_sems[k], frecv_sems[k], sibling).wait_recv()
        for cp in firsts + passed:
            cp.wait_send()

    launch()
    return buf[...]


def _sequencer_kernel(name, n_sems, collective_id):
    return functools.partial(
        pl.kernel, mesh=plsc.ScalarSubcoreMesh(axis_name="seq", num_cores=1), name=name,
        scratch_types=(pltpu.SemaphoreType.DMA,) * n_sems,
        compiler_params=pltpu.CompilerParams(collective_id=collective_id))


def _to_sibling_async(name, slab):
    src = jax.new_ref(slab, memory_space=pltpu.MemorySpace.HBM)
    hr = slab.shape[1] // 2
    got = jax.empty_ref(jax.ShapeDtypeStruct((N_CHIPS, hr, slab.shape[2]), slab.dtype),
                        memory_space=pltpu.MemorySpace.HBM)

    @_sequencer_kernel(name, 2, COLLECTIVE_SIBLING)
    def launch(send_sem, recv_sem):
        x, y, c, _ = _place()
        _handshake([(x, y, 1 - c)])
        _remote(src.at[:, pl.ds((1 - c) * hr, hr), :], got, send_sem, recv_sem, (x, y, 1 - c)).start()
        _remote(got, got, send_sem, recv_sem, (x, y, 1 - c)).wait()

    launch()
    return src[...], got[...]


def _to_chips_async(name, part):
    src = jax.new_ref(part, memory_space=pltpu.MemorySpace.HBM)
    got = jax.empty_ref(jax.ShapeDtypeStruct((3,) + part.shape[1:], part.dtype), memory_space=pltpu.MemorySpace.HBM)

    @_sequencer_kernel(name, 6, COLLECTIVE_CHIPS)
    def launch(*sems):
        send_sems, recv_sems = sems[0:3], sems[3:6]
        x, y, c, chips = _place()
        _handshake([(px, py, c) for px, py in chips])
        cps = []
        for k, (px, py) in enumerate(chips):
            cp = _remote(src.at[2 * px + py], got.at[k], send_sems[k], recv_sems[k], (px, py, c))
            cp.start()
            cps.append(cp)
        for cp in cps:
            cp.wait()

    launch()
    return src[...], got[...]


def _join_sibling_async(name, half_filled):
    buf = jax.new_ref(half_filled, memory_space=pltpu.MemorySpace.HBM)
    hr = half_filled.shape[0] // 2

    @_sequencer_kernel(name, 2, COLLECTIVE_SIBLING)
    def launch(send_sem, recv_sem):
        x, y, c, _ = _place()
        _handshake([(x, y, 1 - c)])
        mine = buf.at[pl.ds(c * hr, hr)]
        other = buf.at[pl.ds((1 - c) * hr, hr)]
        cp = _remote(mine, mine, send_sem, recv_sem, (x, y, 1 - c))
        cp.start()
        _remote(other, other, send_sem, recv_sem, (x, y, 1 - c)).wait_recv()
        cp.wait_send()

    launch()
    return buf[...]


def _exchange_sibling_halves(name, slabs):
    n = len(slabs)

    def body(*refs):
        ins, outs = refs[:n], refs[n:2 * n]
        send_sems, recv_sems = refs[2 * n:]
        x, y, c, _ = _place()
        cps = []
        for w in range(n):
            hr = ins[w].shape[1] // 2
            cp = _remote(ins[w].at[:, pl.ds((1 - c) * hr, hr), :], outs[w], send_sems.at[w], recv_sems.at[w],
                         (x, y, 1 - c))
            cp.start()
            cps.append(cp)
        for cp in cps:
            cp.wait()

    return pl.pallas_call(
        body, name=name, in_specs=_hbm_specs(n), out_specs=_hbm_specs(n),
        out_shape=[jax.ShapeDtypeStruct((N_CHIPS, s.shape[1] // 2, s.shape[2]), s.dtype) for s in slabs],
        scratch_shapes=[pltpu.SemaphoreType.DMA((n,)), pltpu.SemaphoreType.DMA((n,))],
    )(*slabs)


def _exchange_chips(name, parts):
    n = len(parts)

    def body(*refs):
        ins, outs = refs[:n], refs[n:2 * n]
        send_sems, recv_sems = refs[2 * n:]
        x, y, c, chips = _place()
        cps = []
        for w in range(n):
            for k, (px, py) in enumerate(chips):
                cp = _remote(ins[w].at[2 * px + py], outs[w].at[k], send_sems.at[w, k], recv_sems.at[w, k],
                             (px, py, c))
                cp.start()
                cps.append(cp)
        for cp in cps:
            cp.wait()

    return pl.pallas_call(
        body, name=name, in_specs=_hbm_specs(n), out_specs=_hbm_specs(n),
        out_shape=[jax.ShapeDtypeStruct((3,) + s.shape[1:], s.dtype) for s in parts],
        scratch_shapes=[pltpu.SemaphoreType.DMA((n, 3)), pltpu.SemaphoreType.DMA((n, 3))],
    )(*parts)


def _join_sibling_halves(name, bufs):
    n = len(bufs)

    def body(*refs):
        outs = refs[n:2 * n]
        send_sems, recv_sems = refs[2 * n:]
        x, y, c, _ = _place()
        cps = []
        for w in range(n):
            hr = outs[w].shape[0] // 2
            mine = outs[w].at[pl.ds(c * hr, hr)]
            cp = _remote(mine, mine, send_sems.at[w], recv_sems.at[w], (x, y, 1 - c))
            cp.start()
            cps.append(cp)
        for w in range(n):
            hr = outs[w].shape[0] // 2
            other = outs[w].at[pl.ds((1 - c) * hr, hr)]
            _remote(other, other, send_sems.at[w], recv_sems.at[w], (x, y, 1 - c)).wait_recv()
        for cp in cps:
            cp.wait_send()

    return pl.pallas_call(
        body, name=name, in_specs=_hbm_specs(n), out_specs=_hbm_specs(n),
        out_shape=[jax.ShapeDtypeStruct(b.shape, b.dtype) for b in bufs],
        input_output_aliases={w: w for w in range(n)},
        scratch_shapes=[pltpu.SemaphoreType.DMA((n,)), pltpu.SemaphoreType.DMA((n,))],
    )(*bufs)


def _allgather_chips_small(name, v):
    r = v.shape[0]

    def body(v_ref, o_ref, send_sems, recv_sems):
        x, y, c, chips = _place()
        me = 2 * x + y
        o_ref[me] = v_ref[...]
        cps = []
        for k, (px, py) in enumerate(chips):
            cp = _remote(v_ref, o_ref.at[me], send_sems.at[k], recv_sems.at[k], (px, py, c))
            cp.start()
            cps.append(cp)
        for k, (px, py) in enumerate(chips):
            slot = o_ref.at[2 * px + py]
            _remote(slot, slot, send_sems.at[k], recv_sems.at[k], (px, py, c)).wait_recv()
        for cp in cps:
            cp.wait_send()

    return pl.pallas_call(
        body, name=name, in_specs=[pl.BlockSpec(memory_space=pltpu.VMEM)],
        out_specs=pl.BlockSpec(memory_space=pltpu.VMEM),
        out_shape=jax.ShapeDtypeStruct((N_CHIPS, r, LANES), F32),
        scratch_shapes=[pltpu.SemaphoreType.DMA((3,)), pltpu.SemaphoreType.DMA((3,))],
    )(v)


def _allreduce_small(name, v):
    r = v.shape[0]
    hr = r // 2
    assert hr % SUBLANES == 0

    def body(v_ref, o_ref, sib_ref, chips_ref, send_sems, recv_sems):
        x, y, c, chips = _place()
        me = 2 * x + y
        sibling = (x, y, 1 - c)
        first = _remote(v_ref, sib_ref, send_sems.at[0], recv_sems.at[0], sibling)
        first.start()
        first.wait()
        mine = pl.ds(pl.multiple_of(c * hr, SUBLANES), hr)
        chips_ref[me] = v_ref[mine, :] + sib_ref[mine, :]
        cps = []
        for k, (px, py) in enumerate(chips):
            cp = _remote(chips_ref.at[me], chips_ref.at[me], send_sems.at[1 + k], recv_sems.at[1 + k], (px, py, c))
            cp.start()
            cps.append(cp)
        for k, (px, py) in enumerate(chips):
            slot = chips_ref.at[2 * px + py]
            _remote(slot, slot, send_sems.at[1 + k], recv_sems.at[1 + k], (px, py, c)).wait_recv()
        total = chips_ref[0]
        for j in range(1, N_CHIPS):
            total = total + chips_ref[j]
        o_ref[mine, :] = total
        last = _remote(o_ref.at[mine], o_ref.at[mine], send_sems.at[4], recv_sems.at[4], sibling)
        last.start()
        other = o_ref.at[pl.ds(pl.multiple_of((1 - c) * hr, SUBLANES), hr)]
        _remote(other, other, send_sems.at[4], recv_sems.at[4], sibling).wait_recv()
        last.wait_send()
        for cp in cps:
            cp.wait_send()

    return pl.pallas_call(
        body, name=name, in_specs=[pl.BlockSpec(memory_space=pltpu.VMEM)],
        out_specs=pl.BlockSpec(memory_space=pltpu.VMEM),
        out_shape=jax.ShapeDtypeStruct((r, LANES), F32),
        scratch_shapes=[pltpu.VMEM((r, LANES), F32), pltpu.VMEM((N_CHIPS, hr, LANES), F32),
                        pltpu.SemaphoreType.DMA((5,)), pltpu.SemaphoreType.DMA((5,))],
    )(v)


def _add_sibling(name, slabs, recv, c):
    _, r, cols = slabs.shape
    hr = r // 2
    tr = _pick(hr, ROW_TILE)
    nb = hr // tr

    def body(c_ref, a_ref, b_ref, o_ref):
        o_ref[...] = (a_ref[...].astype(F32) + b_ref[...].astype(F32)).astype(BF16)

    grid_spec = pltpu.PrefetchScalarGridSpec(
        num_scalar_prefetch=1, grid=(N_CHIPS, nb),
        in_specs=[pl.BlockSpec((None, tr, cols), lambda j, i, c_ref: (j, c_ref[0] * nb + i, 0)),
                  pl.BlockSpec((None, tr, cols), lambda j, i, c_ref: (j, i, 0))],
        out_specs=pl.BlockSpec((None, tr, cols), lambda j, i, c_ref: (j, i, 0)))
    return pl.pallas_call(
        body, name=name, grid_spec=grid_spec,
        out_shape=jax.ShapeDtypeStruct((N_CHIPS, hr, cols), BF16),
        compiler_params=_cp(("parallel", "parallel")))(jnp.reshape(c, (1,)).astype(jnp.int32), slabs, recv)


def _sum_chips(name, own, recv, chip, c):
    _, hr, cols = recv.shape
    tr = _pick(hr, ROW_TILE)
    nb = hr // tr

    def body(sc_ref, own_ref, recv_ref, o_ref):
        total = own_ref[...].astype(F32)
        for k in range(3):
            total = total + recv_ref[k].astype(F32)
        o_ref[...] = total

    grid_spec = pltpu.PrefetchScalarGridSpec(
        num_scalar_prefetch=1, grid=(nb,),
        in_specs=[pl.BlockSpec((None, tr, cols), lambda i, sc: (sc[0], i, 0)),
                  pl.BlockSpec((3, tr, cols), lambda i, sc: (0, i, 0))],
        out_specs=pl.BlockSpec((tr, cols), lambda i, sc: (sc[1] * nb + i, 0)))
    return pl.pallas_call(
        body, name=name, grid_spec=grid_spec, out_shape=jax.ShapeDtypeStruct((2 * hr, cols), F32),
        compiler_params=_cp(("parallel",)))(jnp.stack([chip, c]).astype(jnp.int32), own, recv)


def _adamw_math(w, g, m, v):
    m = ADAM_B1 * m + (1.0 - ADAM_B1) * g
    v = ADAM_B2 * v + (1.0 - ADAM_B2) * (g * g)
    m_hat = m / (1.0 - ADAM_B1 ** ADAM_STEP)
    v_hat = v / (1.0 - ADAM_B2 ** ADAM_STEP)
    delta = -ADAM_LR * (m_hat / (jnp.sqrt(v_hat) + ADAM_EPS) + ADAM_WD * w)
    return delta, m, v


def _adamw(name, w, gs, m, v):
    nl, r, cols = w.shape
    tr = _pick(r, LANES)

    def body(*refs):
        w_ref, m_ref, v_ref = refs[0:3]
        g_refs = refs[3:3 + nl]
        go_ref, d_ref, nm_ref, nv_ref = refs[3 + nl:]
        layer = pl.program_id(0)
        g = g_refs[0][...]
        for j in range(1, nl):
            g = jnp.where(layer == j, g_refs[j][...], g)
        d, nm, nv = _adamw_math(w_ref[...], g, m_ref[...], v_ref[...])
        go_ref[...] = g
        d_ref[...] = d
        nm_ref[...] = nm
        nv_ref[...] = nv

    spec3 = pl.BlockSpec((None, tr, cols), lambda l, i: (l, i, 0))
    gspec = pl.BlockSpec((tr, cols), lambda l, i: (i, 0))
    out = jax.ShapeDtypeStruct((nl, r, cols), F32)
    return pl.pallas_call(
        body, name=name, grid=(nl, r // tr), in_specs=[spec3] * 3 + [gspec] * nl, out_specs=[spec3] * 4,
        out_shape=[out] * 4, compiler_params=_cp(("parallel", "parallel")))(w, m, v, *gs)


def _adamw_small(name, groups):
    n = len(groups)
    flat = [a for grp in groups for a in grp]

    def body(*refs):
        ins, outs = refs[:4 * n], refs[4 * n:]
        for p in range(n):
            w_ref, g_ref, m_ref, v_ref = ins[4 * p:4 * p + 4]
            d, nm, nv = _adamw_math(w_ref[...], g_ref[...], m_ref[...], v_ref[...])
            outs[3 * p][...] = d
            outs[3 * p + 1][...] = nm
            outs[3 * p + 2][...] = nv

    vm = pl.BlockSpec(memory_space=pltpu.VMEM)
    out_shape = [jax.ShapeDtypeStruct(grp[0].shape, F32) for grp in groups for _ in range(3)]
    res = pl.pallas_call(
        body, name=name, in_specs=[vm] * (4 * n), out_specs=[vm] * (3 * n), out_shape=out_shape)(*flat)
    return [tuple(res[3 * p:3 * p + 3]) for p in range(n)]


def _block_diag_pairs(w):
    h, d, _ = w.shape
    z = jnp.zeros((h // 2, d, d), w.dtype)
    top = jnp.concatenate([w[0::2], z], axis=2)
    bot = jnp.concatenate([z, w[1::2]], axis=2)
    return jnp.concatenate([top, bot], axis=1).astype(BF16)


def _diag_pairs_to_heads(g, d):
    a = g[:, :d, :d]
    b = g[:, d:, d:]
    return jnp.stack([a, b], axis=1).reshape(-1, d, d)


def _rows128(a):
    flat = a.reshape(-1, LANES)
    pad = (-flat.shape[0]) % SUBLANES
    if pad:
        flat = jnp.concatenate([flat, jnp.zeros((pad, LANES), flat.dtype)], axis=0)
    return flat


def _unshard_last(g4, shape):
    g4 = g4.reshape((N_CHIPS,) + tuple(shape))
    return jnp.concatenate([g4[j] for j in range(N_CHIPS)], axis=-1)


def kernel(x, norm_gains, hyb_w_in, hyb_conv_a, hyb_conv_b, hyb_conv_b_bias, hyb_rg_w_a, hyb_rg_b_a, hyb_rg_w_x, hyb_rg_b_x, hyb_rg_lambda, hyb_w_out, sb_w_qkv, sb_w_o, mlp_w_up, mlp_w_down, loss_target, m_norm_gains, m_hyb_w_in, m_hyb_conv_a, m_hyb_conv_b, m_hyb_conv_b_bias, m_hyb_rg_w_a, m_hyb_rg_b_a, m_hyb_rg_w_x, m_hyb_rg_b_x, m_hyb_rg_lambda, m_hyb_w_out, m_sb_w_qkv, m_sb_w_o, m_mlp_w_up, m_mlp_w_down, v_norm_gains, v_hyb_w_in, v_hyb_conv_a, v_hyb_conv_b, v_hyb_conv_b_bias, v_hyb_rg_w_a, v_hyb_rg_b_a, v_hyb_rg_w_x, v_hyb_rg_b_x, v_hyb_rg_lambda, v_hyb_w_out, v_sb_w_qkv, v_sb_w_o, v_mlp_w_up, v_mlp_w_down):
    cx_ = lax.axis_index("x")
    cy_ = lax.axis_index("y")
    cc_ = lax.axis_index("c")
    chip = 2 * cx_ + cy_

    x0 = x[0]
    target = loss_target[0]
    s, d = x0.shape
    heads = SB_HEADS
    assert d // heads == LANES
    n_rg, hd = hyb_rg_w_a.shape[1], hyb_rg_w_a.shape[2]
    wmix = n_rg * hd
    assert 2 * hd == LANES

    big = {
        "hyb_w_in": (hyb_w_in, 0), "hyb_w_out": (hyb_w_out, 0), "mlp_w_up0": (mlp_w_up, 0),
        "mlp_w_down0": (mlp_w_down, 0), "sb_w_qkv": (sb_w_qkv, 0), "sb_w_o": (sb_w_o, 0),
        "mlp_w_up1": (mlp_w_up, 1), "mlp_w_down1": (mlp_w_down, 1),
    }
    names = list(big)
    slots = [_cast_into_slot("cast_" + k, big[k][0], big[k][1], chip) for k in names]
    full = {k: _allgather_async("allgather_" + k, slot, cid) for cid, (k, slot) in enumerate(zip(names, slots))}
    rowsharded = lambda k: full[k].reshape(-1, full[k].shape[2])

    ng_s, ca_s, cb_s = norm_gains.reshape(-1, norm_gains.shape[2]), hyb_conv_a[0], hyb_conv_b[0]
    packed = jnp.concatenate([_rows128(ng_s), _rows128(ca_s), _rows128(cb_s)], axis=0)
    gathered = _allgather_chips_small("allgather_small", packed)
    n0 = ng_s.size // LANES
    n1 = n0 + (-n0) % SUBLANES
    m0 = ca_s.size // LANES
    m1 = m0 + (-m0) % SUBLANES
    k0 = cb_s.size // LANES
    gains = _unshard_last(gathered[:, 0:n0], ng_s.shape).reshape(2, 4, 1, d)
    conv_a = _unshard_last(gathered[:, n1:n1 + m0], ca_s.shape)
    conv_b = _unshard_last(gathered[:, n1 + m1:n1 + m1 + k0], cb_s.shape)
    bias, b_a, b_x, lam = hyb_conv_b_bias, hyb_rg_b_a, hyb_rg_b_x, hyb_rg_lambda
    wa_blk = _block_diag_pairs(hyb_rg_w_a[0])
    wx_blk = _block_diag_pairs(hyb_rg_w_x[0])

    relu_sq = lambda acc: (jnp.maximum(acc, 0.0), jnp.square(jnp.maximum(acc, 0.0)))

    h1 = _rms_fwd("rms_pre0", x0, gains[0, 0])
    proj = _mm_fwd_col("proj_in", h1, full["hyb_w_in"])[0]
    ycat, hseq = _mixer_fwd(proj, conv_a, conv_b, bias, wa_blk, b_a, wx_blk, b_x, lam)
    mix0 = _mm_fwd_row("proj_out", ycat, rowsharded("hyb_w_out"))
    x1, h2 = _rms_post("rms_mix0", mix0, gains[0, 1], x0, gains[0, 2])
    u0, a0 = _mm_fwd_col("mlp_up0", h2, full["mlp_w_up0"], (BF16, BF16), relu_sq)
    mlp0 = _mm_fwd_row("mlp_down0", a0, rowsharded("mlp_w_down0"))
    x2, h3 = _rms_post("rms_mlp0", mlp0, gains[0, 3], x1, gains[1, 0])

    qkv = _mm_fwd_col("qkv", h3, full["sb_w_qkv"], (BF16,))[0]
    att, tot = _attn_fwd(qkv, heads)
    mix1 = _mm_fwd_row("attn_out", att, rowsharded("sb_w_o"))
    x3, h4 = _rms_post("rms_mix1", mix1, gains[1, 1], x2, gains[1, 2])
    u1, a1 = _mm_fwd_col("mlp_up1", h4, full["mlp_w_up1"], (BF16, BF16), relu_sq)
    mlp1 = _mm_fwd_row("mlp_down1", a1, rowsharded("mlp_w_down1"))
    (x4,) = _rms_post("rms_mlp1", mlp1, gains[1, 3], x3)

    dy, loss_local = _loss_head("loss_head", x4, target)
    loss = lax.psum(loss_local, ("x", "y", "c"))

    dgain = [[None] * 4 for _ in range(2)]
    drelu = lambda acc, u: (acc * (2.0 * u.astype(F32)),)
    stage_a, stage_b, gfull = {}, {}, {}

    def tie(main, side):
        return lax.optimization_barrier((main, side))

    def reduce_start(k, slab, main):
        main, slab = tie(main, slab)
        stage_a[k] = _to_sibling_async("grads_to_sibling_" + k, slab)
        return main

    def reduce_to_chips(k, main):
        slab, from_sibling = stage_a.pop(k)
        main, part = tie(main, _add_sibling("grads_add_" + k, slab, from_sibling, cc_))
        stage_b[k] = _to_chips_async("grads_to_chips_" + k, part)
        return main

    def after(value, token):
        return tie(value, token)[0]

    def reduce_finish(k, main):
        own, from_chips = stage_b.pop(k)
        main, half = tie(main, _sum_chips("grads_sum_" + k, after(own, main), from_chips, chip, cc_))
        gfull[k] = _join_sibling_async("grads_join_" + k, half)
        return main

    def mlp_bwd(layer, dxo, mlp_out, xin, hin, u, a):
        down, up = f"mlp_w_down{layer}", f"mlp_w_up{layer}"
        dmlp, dgain[layer][3] = _rms_bwd(f"rms_mlp{layer}_bwd", mlp_out, gains[layer, 3], dxo, out_dtype=BF16)
        wd, wu = rowsharded(down), full[up]
        dmlp = reduce_start(down, _mm_wgrad_row(f"mlp_down{layer}_wgrad", a, dmlp).reshape(N_CHIPS, -1, d), dmlp)
        du = _mm_bwd_row(f"mlp_down{layer}_bwd", dmlp, wd, (BF16,), u, drelu)[0]
        du = reduce_start(up, _mm_wgrad_col(f"mlp_up{layer}_wgrad", hin, du, wu.shape[2]), du)
        du = reduce_to_chips(down, du)
        dh = _mm_bwd_col(f"mlp_up{layer}_bwd", du, wu)
        dh = reduce_to_chips(up, dh)
        dxm, dgain[layer][2] = _rms_bwd(f"rms_premlp{layer}_bwd", xin, gains[layer, 2], dh, res=dxo)
        return dxm

    dx3 = mlp_bwd(1, dy, mlp1, x3, h4, u1, a1)
    dmix1, dgain[1][1] = _rms_bwd("rms_mix1_bwd", mix1, gains[1, 1], dx3, out_dtype=BF16)
    dmix1 = reduce_start("sb_w_o", _mm_wgrad_row("attn_out_wgrad", att, dmix1).reshape(N_CHIPS, -1, d), dmix1)
    datt = _mm_bwd_row("attn_out_bwd", dmix1, rowsharded("sb_w_o"), (BF16,))[0]
    dq, dk, dv = _attn_bwd(qkv, tot, datt, heads)
    dqkv = jnp.concatenate([dq, dk, dv], axis=1)
    dqkv = reduce_to_chips("sb_w_o", dqkv)
    dqkv = reduce_finish("mlp_w_down1", dqkv)
    dqkv = reduce_finish("mlp_w_up1", dqkv)
    dqkv = reduce_start("sb_w_qkv", _mm_wgrad_col("qkv_wgrad", h3, dqkv, full["sb_w_qkv"].shape[2]), dqkv)
    dh3 = _mm_bwd_col("qkv_bwd", dqkv, full["sb_w_qkv"])
    dh3 = reduce_to_chips("sb_w_qkv", dh3)
    dx2, dgain[1][0] = _rms_bwd("rms_pre1_bwd", x2, gains[1, 0], dh3, res=dx3)

    dx1 = mlp_bwd(0, dx2, mlp0, x1, h2, u0, a0)
    dx1 = reduce_finish("sb_w_o", dx1)
    dx1 = reduce_finish("sb_w_qkv", dx1)
    dmix0, dgain[0][1] = _rms_bwd("rms_mix0_bwd", mix0, gains[0, 1], dx1, out_dtype=BF16)
    dmix0 = reduce_start("hyb_w_out", _mm_wgrad_row("proj_out_wgrad", ycat, dmix0).reshape(N_CHIPS, -1, d), dmix0)
    dycat = _mm_bwd_row("proj_out_bwd", dmix0, rowsharded("hyb_w_out"))[0]
    dproj, xr_b, dpa_b, dpx_b, sg = _mixer_bwd(
        proj, hseq, dycat, conv_a, conv_b, bias, wa_blk, b_a, wx_blk, b_x, lam)
    dproj = reduce_to_chips("hyb_w_out", dproj)
    dproj = reduce_start("hyb_w_in", _mm_wgrad_col("proj_in_wgrad", h1, dproj, full["hyb_w_in"].shape[2]), dproj)
    dh1 = _mm_bwd_col("proj_in_bwd", dproj, full["hyb_w_in"])
    dh1 = reduce_to_chips("hyb_w_in", dh1)
    dx0, dgain[0][0] = _rms_bwd("rms_pre0_bwd", x0, gains[0, 0], dh1, res=dx1)
    dwa = _diag_pairs_to_heads(_mm_wgrad_diag("rg_w_a_wgrad", xr_b, dpa_b), hd)
    dwx = _diag_pairs_to_heads(_mm_wgrad_diag("rg_w_x_wgrad", xr_b, dpx_b), hd)

    dgains = jnp.concatenate([dgain[l][k] for l in range(2) for k in range(4)], axis=0)
    small_parts = [dgains, sg[_SG_CONV_A:_SG_CONV_A + 3], sg[_SG_CONV_B:_SG_CONV_B + 4], sg[_SG_BIAS:_SG_BIAS + 1],
                   dwa, sg[_SG_BA:_SG_BA + 1], dwx, sg[_SG_BX:_SG_BX + 1], sg[_SG_LAM:_SG_LAM + 1]]
    small_rows = [_rows128(p) for p in small_parts]
    n_small = sum(rws.shape[0] for rws in small_rows)
    tail_pad = [jnp.zeros(((-n_small) % (2 * SUBLANES), LANES), F32)] if n_small % (2 * SUBLANES) else []
    reduced = _allreduce_small("allreduce_small", jnp.concatenate(small_rows + tail_pad, axis=0))
    small_full, off = [], 0
    for p, rws in zip(small_parts, small_rows):
        small_full.append(reduced[off:off + p.size // LANES].reshape(p.shape))
        off += rws.shape[0]
    g_gains, g_ca, g_cb, g_bias, g_wa, g_ba, g_wx, g_bx, g_lam = small_full

    def my_cols(g, width):
        return lax.dynamic_slice_in_dim(g, chip * width, width, axis=g.ndim - 1)

    small = [
        ("norm_gains", norm_gains, my_cols(g_gains, norm_gains.shape[2]).reshape(norm_gains.shape),
         m_norm_gains, v_norm_gains),
        ("hyb_conv_a", hyb_conv_a, my_cols(g_ca, hyb_conv_a.shape[2])[None], m_hyb_conv_a, v_hyb_conv_a),
        ("hyb_conv_b", hyb_conv_b, my_cols(g_cb, hyb_conv_b.shape[2])[None], m_hyb_conv_b, v_hyb_conv_b),
        ("hyb_conv_b_bias", hyb_conv_b_bias, g_bias, m_hyb_conv_b_bias, v_hyb_conv_b_bias),
        ("hyb_rg_w_a", hyb_rg_w_a, g_wa[None], m_hyb_rg_w_a, v_hyb_rg_w_a),
        ("hyb_rg_b_a", hyb_rg_b_a, g_ba, m_hyb_rg_b_a, v_hyb_rg_b_a),
        ("hyb_rg_w_x", hyb_rg_w_x, g_wx[None], m_hyb_rg_w_x, v_hyb_rg_w_x),
        ("hyb_rg_b_x", hyb_rg_b_x, g_bx, m_hyb_rg_b_x, v_hyb_rg_b_x),
        ("hyb_rg_lambda", hyb_rg_lambda, g_lam, m_hyb_rg_lambda, v_hyb_rg_lambda),
    ]
    to2d = lambda a: a.reshape(-1, a.shape[-1])
    small_res = _adamw_small("adamw_small", [tuple(to2d(a) for a in (w, g, m, v)) for _, w, g, m, v in small])
    out = {}
    for (nm, w, g, _, _), (dl, nmom, nvar) in zip(small, small_res):
        out[nm] = (g, dl.reshape(w.shape), nmom.reshape(w.shape), nvar.reshape(w.shape))

    stacked = {
        "mlp_w_down": (mlp_w_down, m_mlp_w_down, v_mlp_w_down, ["mlp_w_down0", "mlp_w_down1"]),
        "mlp_w_up": (mlp_w_up, m_mlp_w_up, v_mlp_w_up, ["mlp_w_up0", "mlp_w_up1"]),
        "sb_w_o": (sb_w_o, m_sb_w_o, v_sb_w_o, ["sb_w_o"]),
        "sb_w_qkv": (sb_w_qkv, m_sb_w_qkv, v_sb_w_qkv, ["sb_w_qkv"]),
        "hyb_w_out": (hyb_w_out, m_hyb_w_out, v_hyb_w_out, ["hyb_w_out"]),
        "hyb_w_in": (hyb_w_in, m_hyb_w_in, v_hyb_w_in, ["hyb_w_in"]),
    }

    def update(k, token):
        w, m, v, parts = stacked[k]
        out[k] = tuple(_adamw("adamw_" + k, w, [after(gfull[p], token) for p in parts], m, v))
        return out[k][1]

    token = small_res[0][0]
    token = update("sb_w_qkv", token)
    token = reduce_finish("mlp_w_down0", token)
    token = update("sb_w_o", token)
    token = reduce_finish("mlp_w_up0", token)
    token = update("mlp_w_down", token)
    token = reduce_finish("hyb_w_out", token)
    token = update("mlp_w_up", token)
    token = reduce_finish("hyb_w_in", token)
    token = update("hyb_w_out", token)
    update("hyb_w_in", token)

    order = ["norm_gains", "hyb_w_in", "hyb_conv_a", "hyb_conv_b", "hyb_conv_b_bias", "hyb_rg_w_a", "hyb_rg_b_a",
             "hyb_rg_w_x", "hyb_rg_b_x", "hyb_rg_lambda", "hyb_w_out", "sb_w_qkv", "sb_w_o", "mlp_w_up",
             "mlp_w_down"]
    return (loss, dx0[None], *[out[k][0] for k in order], *[out[k][1] for k in order],
            *[out[k][2] for k in order], *[out[k][3] for k in order])
```

```python
import functools
import math

import jax
import jax.numpy as jnp
from jax import lax
from jax.experimental import pallas as pl
from jax.experimental.pallas import tpu as pltpu
from jax.experimental.pallas import tpu_sc as plsc

F32 = jnp.float32
BF16 = jnp.bfloat16
MESH = pl.DeviceIdType.MESH

SB_HEADS = 16
NORM_EPS = 1e-6
LRU_C = 8.0
ADAM_LR = 0.001
ADAM_B1 = 0.9
ADAM_B2 = 0.999
ADAM_EPS = 1e-08
ADAM_WD = 0.01
ADAM_STEP = 10

LANES = 128
SUBLANES = 8
VMEM_LIMIT = 48 * 1024 * 1024
MM_TILE = 1024
MM_TILE_N = 1280
MM_TILE_K = 2048
ROW_TILE = 256
ATT_TILE = 512
ATT_HEADS_PER_STEP = 2
N_CHIPS = 4
N_DEV = 8
COLLECTIVE_SIBLING = 8
COLLECTIVE_CHIPS = 9

_DIMS = {
    "nn": (((1,), (0,)), ((), ())),
    "nt": (((1,), (1,)), ((), ())),
    "tn": (((0,), (0,)), ((), ())),
}


def _cp(sem=None, vmem=VMEM_LIMIT):
    return pltpu.CompilerParams(dimension_semantics=sem, vmem_limit_bytes=vmem)


def _pick(dim, pref):
    t = min(dim, pref)
    while dim % t:
        t -= LANES
    return t


def _whole(shape):
    nd = len(shape)
    return pl.BlockSpec(tuple(shape), lambda *_: (0,) * nd)


def _sigmoid(z):
    return 1.0 / (1.0 + jnp.exp(-z))


def _log_sigmoid(z):
    return jnp.minimum(z, 0.0) - jnp.log(1.0 + jnp.exp(-jnp.abs(z)))


def _expm1(z):
    series = z * (1.0 + z * (0.5 + z * (1.0 / 6.0 + z * (1.0 / 24.0))))
    return jnp.where(jnp.abs(z) < 0.05, series, jnp.exp(z) - 1.0)


_GELU_C = math.sqrt(2.0 / math.pi)


def _gelu_and_grad(g):
    inner = _GELU_C * (g + 0.044715 * g * g * g)
    t = jnp.tanh(inner)
    val = 0.5 * g * (1.0 + t)
    grad = 0.5 * (1.0 + t) + 0.5 * g * (1.0 - t * t) * _GELU_C * (1.0 + 3.0 * 0.044715 * g * g)
    return val, grad


def _shift_down(cur, prev8, k, rows):
    n = cur.shape[0]
    rolled = pltpu.roll(cur, k, 0)
    head = jnp.tile(pltpu.roll(prev8, k, 0), (n // SUBLANES, 1))
    return jnp.where(rows < k, head, rolled)


def _shift_up(cur, next8, k, rows):
    n = cur.shape[0]
    rolled = pltpu.roll(cur, n - k, 0)
    tail = jnp.tile(pltpu.roll(next8, SUBLANES - k, 0), (n // SUBLANES, 1))
    return jnp.where(rows >= n - k, tail, rolled)


def _colsum(v):
    return jnp.sum(v, axis=0, keepdims=True)


def _matmul(name, mode, grid, operands, in_specs, out_shapes, out_specs, acc_shape, epilogue=None):
    nk = grid[2]
    n_in = len(operands)
    dims = _DIMS[mode]

    def finish(acc, extra, outs):
        res = epilogue(acc, *[e[...] for e in extra]) if epilogue is not None else (acc,)
        for o_ref, o in zip(outs, res):
            o_ref[...] = o.astype(o_ref.dtype)

    def product(a_ref, b_ref):
        return lax.dot_general(a_ref[...].astype(BF16), b_ref[...].astype(BF16), dims, preferred_element_type=F32)

    def body_single(*refs):
        finish(product(refs[0], refs[1]), refs[2:n_in], refs[n_in:])

    def body(*refs):
        extra = refs[2:n_in]
        outs = refs[n_in:-1]
        acc_ref = refs[-1]
        k = pl.program_id(2)

        @pl.when(k == 0)
        def _():
            acc_ref[...] = product(refs[0], refs[1])

        @pl.when(k > 0)
        def _():
            acc_ref[...] += product(refs[0], refs[1])

        @pl.when(k == nk - 1)
        def _():
            finish(acc_ref[...], extra, outs)

    return pl.pallas_call(
        body_single if nk == 1 else body, name=name, grid=grid, in_specs=in_specs, out_specs=out_specs,
        out_shape=out_shapes, scratch_shapes=[] if nk == 1 else [pltpu.VMEM(acc_shape, F32)],
        compiler_params=_cp(("parallel", "parallel", "arbitrary")),
    )(*operands)


def _mm_fwd_col(name, a, wfull, out_dtypes=(F32,), epilogue=None):
    s, kdim = a.shape
    _, _, cs = wfull.shape
    tm, tk, tn = _pick(s, MM_TILE), _pick(kdim, MM_TILE_K), _pick(cs, MM_TILE_N)
    nbj = cs // tn
    grid = (s // tm, N_CHIPS * nbj, kdim // tk)
    out_shapes = [jax.ShapeDtypeStruct((s, N_CHIPS * cs), dt) for dt in out_dtypes]
    out_specs = [pl.BlockSpec((tm, tn), lambda i, n, k: (i, n)) for _ in out_dtypes]
    return _matmul(
        name, "nn", grid, [a, wfull],
        [pl.BlockSpec((tm, tk), lambda i, n, k: (i, k)),
         pl.BlockSpec((None, tk, tn), lambda i, n, k: (n // nbj, k, n % nbj))],
        out_shapes, out_specs, (tm, tn), epilogue)


def _mm_fwd_row(name, a, w2d, out_dtype=F32):
    s, kdim = a.shape
    _, n_out = w2d.shape
    tm, tk, tn = _pick(s, MM_TILE), _pick(kdim, MM_TILE_K), _pick(n_out, MM_TILE)
    grid = (s // tm, n_out // tn, kdim // tk)
    return _matmul(
        name, "nn", grid, [a, w2d],
        [pl.BlockSpec((tm, tk), lambda i, n, k: (i, k)),
         pl.BlockSpec((tk, tn), lambda i, n, k: (k, n))],
        [jax.ShapeDtypeStruct((s, n_out), out_dtype)],
        [pl.BlockSpec((tm, tn), lambda i, n, k: (i, n))], (tm, tn))[0]


def _mm_bwd_col(name, dy, wfull, out_dtype=F32):
    s, _ = dy.shape
    _, kdim, cs = wfull.shape
    tm, tn, tk = _pick(s, MM_TILE), _pick(kdim, MM_TILE), _pick(cs, MM_TILE_K)
    nbj = cs // tk
    grid = (s // tm, kdim // tn, N_CHIPS * nbj)
    return _matmul(
        name, "nt", grid, [dy, wfull],
        [pl.BlockSpec((tm, tk), lambda i, n, k: (i, k)),
         pl.BlockSpec((None, tn, tk), lambda i, n, k: (k // nbj, n, k % nbj))],
        [jax.ShapeDtypeStruct((s, kdim), out_dtype)],
        [pl.BlockSpec((tm, tn), lambda i, n, k: (i, n))], (tm, tn))[0]


def _mm_bwd_row(name, dy, w2d, out_dtypes=(F32,), extra=None, epilogue=None):
    s, n_in = dy.shape
    kdim, _ = w2d.shape
    tm, tn, tk = _pick(s, MM_TILE), _pick(kdim, MM_TILE), _pick(n_in, MM_TILE_K)
    grid = (s // tm, kdim // tn, n_in // tk)
    operands = [dy, w2d]
    in_specs = [pl.BlockSpec((tm, tk), lambda i, n, k: (i, k)),
                pl.BlockSpec((tn, tk), lambda i, n, k: (n, k))]
    if extra is not None:
        operands.append(extra)
        in_specs.append(pl.BlockSpec((tm, tn), lambda i, n, k: (i, n)))
    return _matmul(
        name, "nt", grid, operands, in_specs,
        [jax.ShapeDtypeStruct((s, kdim), dt) for dt in out_dtypes],
        [pl.BlockSpec((tm, tn), lambda i, n, k: (i, n)) for _ in out_dtypes], (tm, tn), epilogue)


def _mm_wgrad_col(name, a, dy, cs):
    s, kdim = a.shape
    tm, tn, ts = _pick(kdim, MM_TILE), _pick(cs, MM_TILE_N), _pick(s, MM_TILE_K)
    nbj = cs // tn
    grid = (kdim // tm, N_CHIPS * nbj, s // ts)
    return _matmul(
        name, "tn", grid, [a, dy],
        [pl.BlockSpec((ts, tm), lambda i, n, k: (k, i)),
         pl.BlockSpec((ts, tn), lambda i, n, k: (k, n))],
        [jax.ShapeDtypeStruct((N_CHIPS, kdim, cs), BF16)],
        [pl.BlockSpec((None, tm, tn), lambda i, n, k: (n // nbj, i, n % nbj))], (tm, tn))[0]


def _mm_wgrad_row(name, a, dy):
    s, kdim = a.shape
    _, n_out = dy.shape
    tm, tn, ts = _pick(kdim, MM_TILE), _pick(n_out, MM_TILE), _pick(s, MM_TILE_K)
    grid = (kdim // tm, n_out // tn, s // ts)
    return _matmul(
        name, "tn", grid, [a, dy],
        [pl.BlockSpec((ts, tm), lambda i, n, k: (k, i)),
         pl.BlockSpec((ts, tn), lambda i, n, k: (k, n))],
        [jax.ShapeDtypeStruct((kdim, n_out), BF16)],
        [pl.BlockSpec((tm, tn), lambda i, n, k: (i, n))], (tm, tn))[0]


def _mm_wgrad_diag(name, a, dy):
    s, width = a.shape
    nb = width // LANES
    ts = _pick(s, MM_TILE)
    grid = (nb, 1, s // ts)
    return _matmul(
        name, "tn", grid, [a, dy],
        [pl.BlockSpec((ts, LANES), lambda i, n, k: (k, i)),
         pl.BlockSpec((ts, LANES), lambda i, n, k: (k, i))],
        [jax.ShapeDtypeStruct((nb, LANES, LANES), F32)],
        [pl.BlockSpec((None, LANES, LANES), lambda i, n, k: (i, 0, 0))], (LANES, LANES))[0]


def _rowspec(tr, d):
    return pl.BlockSpec((tr, d), lambda i: (i, 0))


def _vecspec(d):
    return pl.BlockSpec((1, d), lambda i: (0, 0))


def _rms(x, g):
    return x * lax.rsqrt(jnp.mean(x * x, axis=-1, keepdims=True) + NORM_EPS) * g


def _cast_into_slot(name, w, layer, chip):
    _, r, c = w.shape
    tr = _pick(r, ROW_TILE)

    def body(chip_ref, w_ref, o_ref):
        o_ref[...] = w_ref[...].astype(BF16)

    grid_spec = pltpu.PrefetchScalarGridSpec(
        num_scalar_prefetch=1, grid=(r // tr,),
        in_specs=[pl.BlockSpec((None, tr, c), lambda i, chip_ref: (layer, i, 0))],
        out_specs=pl.BlockSpec((None, tr, c), lambda i, chip_ref: (chip_ref[0], i, 0)))
    return pl.pallas_call(
        body, name=name, grid_spec=grid_spec, out_shape=jax.ShapeDtypeStruct((N_CHIPS, r, c), BF16),
        compiler_params=_cp(("parallel",)))(jnp.reshape(chip, (1,)).astype(jnp.int32), w)


def _rms_fwd(name, x, g):
    s, d = x.shape
    tr = _pick(s, ROW_TILE)

    def body(x_ref, g_ref, h_ref):
        h_ref[...] = _rms(x_ref[...], g_ref[...]).astype(BF16)

    return pl.pallas_call(
        body, name=name, grid=(s // tr,), in_specs=[_rowspec(tr, d), _vecspec(d)],
        out_specs=_rowspec(tr, d), out_shape=jax.ShapeDtypeStruct((s, d), BF16),
        compiler_params=_cp(("parallel",)))(x, g)


def _rms_post(name, y, g_post, res, g_next=None):
    s, d = y.shape
    tr = _pick(s, ROW_TILE)
    with_next = g_next is not None

    def body(*refs):
        if with_next:
            y_ref, gp_ref, r_ref, gn_ref, x_ref, h_ref = refs
        else:
            y_ref, gp_ref, r_ref, x_ref = refs
        xn = r_ref[...] + _rms(y_ref[...], gp_ref[...])
        x_ref[...] = xn
        if with_next:
            h_ref[...] = _rms(xn, gn_ref[...]).astype(BF16)

    operands = [y, g_post, res] + ([g_next] if with_next else [])
    in_specs = [_rowspec(tr, d), _vecspec(d), _rowspec(tr, d)] + ([_vecspec(d)] if with_next else [])
    out_shape = [jax.ShapeDtypeStruct((s, d), F32)] + ([jax.ShapeDtypeStruct((s, d), BF16)] if with_next else [])
    out_specs = [_rowspec(tr, d)] + ([_rowspec(tr, d)] if with_next else [])
    return pl.pallas_call(
        body, name=name, grid=(s // tr,), in_specs=in_specs, out_specs=out_specs, out_shape=out_shape,
        compiler_params=_cp(("parallel",)))(*operands)


def _rms_bwd(name, x, g, dy, res=None, out_dtype=F32):
    s, d = x.shape
    tr = _pick(s, ROW_TILE)
    nsteps = s // tr
    with_res = res is not None

    def body(*refs):
        if with_res:
            x_ref, g_ref, dy_ref, r_ref, dx_ref, dg_ref, acc_ref = refs
        else:
            x_ref, g_ref, dy_ref, dx_ref, dg_ref, acc_ref = refs
        i = pl.program_id(0)

        @pl.when(i == 0)
        def _():
            acc_ref[...] = jnp.zeros_like(acc_ref)

        xv = x_ref[...]
        dyv = dy_ref[...].astype(F32)
        r = lax.rsqrt(jnp.mean(xv * xv, axis=-1, keepdims=True) + NORM_EPS)
        xhat = xv * r
        gy = dyv * g_ref[...]
        dx = r * (gy - xhat * jnp.mean(gy * xhat, axis=-1, keepdims=True))
        if with_res:
            dx = dx + r_ref[...]
        dx_ref[...] = dx.astype(dx_ref.dtype)
        acc_ref[...] += jnp.sum((dyv * xhat).reshape(tr // SUBLANES, SUBLANES, d), axis=0)

        @pl.when(i == nsteps - 1)
        def _():
            dg_ref[...] = jnp.broadcast_to(_colsum(acc_ref[...]), (SUBLANES, d))

    operands = [x, g, dy] + ([res] if with_res else [])
    in_specs = [_rowspec(tr, d), _vecspec(d), _rowspec(tr, d)] + ([_rowspec(tr, d)] if with_res else [])
    dx, dg = pl.pallas_call(
        body, name=name, grid=(nsteps,), in_specs=in_specs,
        out_specs=[_rowspec(tr, d), pl.BlockSpec((SUBLANES, d), lambda i: (0, 0))],
        out_shape=[jax.ShapeDtypeStruct((s, d), out_dtype), jax.ShapeDtypeStruct((SUBLANES, d), F32)],
        scratch_shapes=[pltpu.VMEM((SUBLANES, d), F32)],
        compiler_params=_cp(("arbitrary",)))(*operands)
    return dx, dg[0:1]


def _last_norm_and_loss(name, y, g, res, target):
    s, d = y.shape
    tr = _pick(s, ROW_TILE)
    nsteps = s // tr

    def body(y_ref, g_ref, r_ref, t_ref, dx_ref, dy_ref, dg_ref, l_ref, acc_ref, lacc_ref):
        i = pl.program_id(0)

        @pl.when(i == 0)
        def _():
            acc_ref[...] = jnp.zeros_like(acc_ref)
            lacc_ref[...] = jnp.zeros_like(lacc_ref)

        yv = y_ref[...]
        gv = g_ref[...]
        r = lax.rsqrt(jnp.mean(yv * yv, axis=-1, keepdims=True) + NORM_EPS)
        yhat = yv * r
        err = r_ref[...] + yhat * gv - t_ref[...]
        dx = err * (1.0 / d)
        dx_ref[...] = dx
        lacc_ref[...] += jnp.sum((err * err).reshape(tr // SUBLANES, SUBLANES, d), axis=0)
        gy = dx * gv
        dy_ref[...] = (r * (gy - yhat * jnp.mean(gy * yhat, axis=-1, keepdims=True))).astype(dy_ref.dtype)
        acc_ref[...] += jnp.sum((dx * yhat).reshape(tr // SUBLANES, SUBLANES, d), axis=0)

        @pl.when(i == nsteps - 1)
        def _():
            dg_ref[...] = jnp.broadcast_to(_colsum(acc_ref[...]), (SUBLANES, d))
            l_ref[...] = jnp.full((SUBLANES, LANES), (0.5 / d) * jnp.sum(lacc_ref[...]), F32)

    dx, dy, dg, l = pl.pallas_call(
        body, name=name, grid=(nsteps,),
        in_specs=[_rowspec(tr, d), _vecspec(d), _rowspec(tr, d), _rowspec(tr, d)],
        out_specs=[_rowspec(tr, d), _rowspec(tr, d), pl.BlockSpec((SUBLANES, d), lambda i: (0, 0)),
                   pl.BlockSpec((SUBLANES, LANES), lambda i: (0, 0))],
        out_shape=[jax.ShapeDtypeStruct((s, d), F32), jax.ShapeDtypeStruct((s, d), BF16),
                   jax.ShapeDtypeStruct((SUBLANES, d), F32), jax.ShapeDtypeStruct((SUBLANES, LANES), F32)],
        scratch_shapes=[pltpu.VMEM((SUBLANES, d), F32), pltpu.VMEM((SUBLANES, d), F32)],
        compiler_params=_cp(("arbitrary",)))(y, g, res, target)
    return dx, dy, dg[0:1], l[0, 0]


def _loss_head(name, y, target):
    s, d = y.shape
    tr = _pick(s, ROW_TILE)
    nsteps = s // tr

    def body(y_ref, t_ref, dy_ref, l_ref, acc_ref):
        i = pl.program_id(0)

        @pl.when(i == 0)
        def _():
            acc_ref[...] = jnp.zeros_like(acc_ref)

        err = y_ref[...] - t_ref[...]
        dy_ref[...] = err * (1.0 / d)
        acc_ref[...] += jnp.sum((err * err).reshape(tr // SUBLANES, SUBLANES, d), axis=0)

        @pl.when(i == nsteps - 1)
        def _():
            l_ref[...] = jnp.full((SUBLANES, LANES), (0.5 / d) * jnp.sum(acc_ref[...]), F32)

    dy, l = pl.pallas_call(
        body, name=name, grid=(nsteps,), in_specs=[_rowspec(tr, d), _rowspec(tr, d)],
        out_specs=[_rowspec(tr, d), pl.BlockSpec((SUBLANES, LANES), lambda i: (0, 0))],
        out_shape=[jax.ShapeDtypeStruct((s, d), F32), jax.ShapeDtypeStruct((SUBLANES, LANES), F32)],
        scratch_shapes=[pltpu.VMEM((SUBLANES, d), F32)],
        compiler_params=_cp(("arbitrary",)))(y, target)
    return dy, l[0, 0]


def _gates(xr, wa, ba, wx, bx, lam):
    xb = xr.astype(BF16)
    r = _sigmoid(jnp.dot(xb, wa, preferred_element_type=F32) + ba)
    i = _sigmoid(jnp.dot(xb, wx, preferred_element_type=F32) + bx)
    log_a = LRU_C * r * _log_sigmoid(lam)
    a = jnp.exp(log_a)
    m = jnp.sqrt(-_expm1(2.0 * log_a))
    return r, i, a, m


def _mixer_fwd(proj, conv_a, conv_b, bias, wa_blk, ba, wx_blk, bx, lam):
    s, w5 = proj.shape
    w = w5 // 5
    nch = w // LANES
    ts = _pick(s, ROW_TILE)
    nt = s // ts

    def body(p_ref, pp_ref, ca_ref, cb_ref, bias_ref, wa_ref, ba_ref, wx_ref, bx_ref, lam_ref,
             y_ref, h_ref, a_scr, b_scr, hc_scr):
        t = pl.program_id(0)
        first = t == 0
        rows = lax.broadcasted_iota(jnp.int32, (ts, LANES), 0)

        @pl.when(first)
        def _():
            hc_scr[...] = jnp.zeros_like(hc_scr)

        def cur(comp, c):
            return p_ref[:, comp * w + c * LANES:comp * w + (c + 1) * LANES]

        def prev(comp, c):
            v = pp_ref[:, comp * w + c * LANES:comp * w + (c + 1) * LANES]
            return jnp.where(first, 0.0, v)

        for c in range(nch):
            sl = slice(c * LANES, (c + 1) * LANES)
            cx = cur(1, c) * cur(2, c)
            cxp = prev(1, c) * prev(2, c)
            wa3 = ca_ref[:, sl]
            conv = (wa3[2:3] * cx + wa3[1:2] * _shift_down(cx, cxp, 1, rows)
                    + wa3[0:1] * _shift_down(cx, cxp, 2, rows))
            y_ref[:, sl] = (cur(0, c) * conv).astype(BF16)

        for c in range(nch):
            sl = slice(c * LANES, (c + 1) * LANES)
            xb, xbp = cur(4, c), prev(4, c)
            wb4 = cb_ref[:, sl]
            xr = (wb4[3:4] * xb + wb4[2:3] * _shift_down(xb, xbp, 1, rows)
                  + wb4[1:2] * _shift_down(xb, xbp, 2, rows)
                  + wb4[0:1] * _shift_down(xb, xbp, 3, rows) + bias_ref[:, sl])
            _, i, a, m = _gates(xr, wa_ref[c], ba_ref[:, sl], wx_ref[c], bx_ref[:, sl], lam_ref[:, sl])
            a_scr[:, sl] = a
            b_scr[:, sl] = m * i * xr

        def step(r, h):
            h = a_scr[pl.ds(r, 1), :] * h + b_scr[pl.ds(r, 1), :]
            h_ref[pl.ds(r, 1), :] = h
            return h

        hc_scr[0:1, :] = lax.fori_loop(0, ts, step, hc_scr[0:1, :], unroll=8)

        for c in range(nch):
            sl = slice(c * LANES, (c + 1) * LANES)
            gel, _ = _gelu_and_grad(cur(3, c))
            y_ref[:, w + c * LANES:w + (c + 1) * LANES] = (h_ref[:, sl] * gel).astype(BF16)

    vec = lambda n: _whole((n, w))
    return pl.pallas_call(
        body, name="mixer_fwd", grid=(nt,),
        in_specs=[pl.BlockSpec((ts, w5), lambda t: (t, 0)),
                  pl.BlockSpec((SUBLANES, w5), lambda t: (jnp.maximum(t * (ts // SUBLANES) - 1, 0), 0)),
                  vec(3), vec(4), vec(1), _whole(wa_blk.shape), vec(1), _whole(wx_blk.shape), vec(1), vec(1)],
        out_specs=[pl.BlockSpec((ts, 2 * w), lambda t: (t, 0)), pl.BlockSpec((ts, w), lambda t: (t, 0))],
        out_shape=[jax.ShapeDtypeStruct((s, 2 * w), BF16), jax.ShapeDtypeStruct((s, w), F32)],
        scratch_shapes=[pltpu.VMEM((ts, w), F32), pltpu.VMEM((ts, w), F32), pltpu.VMEM((SUBLANES, w), F32)],
        compiler_params=_cp(("arbitrary",)),
    )(proj, proj, conv_a, conv_b, bias, wa_blk, ba, wx_blk, bx, lam)


_SG_CONV_A, _SG_CONV_B, _SG_BIAS, _SG_BA, _SG_BX, _SG_LAM, _SG_ROWS = 0, 3, 7, 8, 9, 10, 16


def _mixer_bwd(proj, hseq, dy, conv_a, conv_b, bias, wa_blk, ba, wx_blk, bx, lam):
    s, w5 = proj.shape
    w = w5 // 5
    nch = w // LANES
    ts = _pick(s, ROW_TILE)
    nt = s // ts
    tpb = ts // SUBLANES

    def body(p_ref, pp_ref, h_ref, hp_ref, dy_ref, ca_ref, cb_ref, bias_ref, wa_ref, ba_ref, wx_ref, bx_ref,
             lam_ref, dp_ref, xr_ref, dpa_ref, dpx_ref, sg_ref,
             a_scr, g_scr, l_scr, x_scr, r_scr, i_scr, m_scr, cl_scr, cdc_scr, cdx_scr):
        pid = pl.program_id(0)
        last = pid == 0
        first = pid == nt - 1
        rows = lax.broadcasted_iota(jnp.int32, (ts, LANES), 0)

        @pl.when(last)
        def _():
            sg_ref[...] = jnp.zeros_like(sg_ref)
            cl_scr[...] = jnp.zeros_like(cl_scr)
            cdc_scr[...] = jnp.zeros_like(cdc_scr)
            cdx_scr[...] = jnp.zeros_like(cdx_scr)

        def cur(comp, c):
            return p_ref[:, comp * w + c * LANES:comp * w + (c + 1) * LANES]

        def prev(comp, c):
            v = pp_ref[:, comp * w + c * LANES:comp * w + (c + 1) * LANES]
            return jnp.where(first, 0.0, v)

        def put(comp, c, v):
            dp_ref[:, comp * w + c * LANES:comp * w + (c + 1) * LANES] = v.astype(dp_ref.dtype)

        def acc(row, sl, v):
            sg_ref[row:row + 1, sl] += _colsum(v)

        for c in range(nch):
            sl = slice(c * LANES, (c + 1) * LANES)
            bg, cg, ax = cur(0, c), cur(1, c), cur(2, c)
            cx = cg * ax
            cxp = prev(1, c) * prev(2, c)
            cx1 = _shift_down(cx, cxp, 1, rows)
            cx2 = _shift_down(cx, cxp, 2, rows)
            wa3 = ca_ref[:, sl]
            conv = wa3[2:3] * cx + wa3[1:2] * cx1 + wa3[0:1] * cx2
            dya = dy_ref[:, sl]
            put(0, c, dya * conv)
            dconv = dya * bg
            nxt = cdc_scr[:, sl]
            dcx = (wa3[2:3] * dconv + wa3[1:2] * _shift_up(dconv, nxt, 1, rows)
                   + wa3[0:1] * _shift_up(dconv, nxt, 2, rows))
            cdc_scr[:, sl] = dconv[0:SUBLANES]
            put(1, c, dcx * ax)
            put(2, c, dcx * cg)
            acc(_SG_CONV_A + 2, sl, dconv * cx)
            acc(_SG_CONV_A + 1, sl, dconv * cx1)
            acc(_SG_CONV_A + 0, sl, dconv * cx2)

        for c in range(nch):
            sl = slice(c * LANES, (c + 1) * LANES)
            xb, xbp = cur(4, c), prev(4, c)
            wb4 = cb_ref[:, sl]
            xr = (wb4[3:4] * xb + wb4[2:3] * _shift_down(xb, xbp, 1, rows)
                  + wb4[1:2] * _shift_down(xb, xbp, 2, rows)
                  + wb4[0:1] * _shift_down(xb, xbp, 3, rows) + bias_ref[:, sl])
            r, i, a, m = _gates(xr, wa_ref[c], ba_ref[:, sl], wx_ref[c], bx_ref[:, sl], lam_ref[:, sl])
            gel, dgel = _gelu_and_grad(cur(3, c))
            dyb = dy_ref[:, w + c * LANES:w + (c + 1) * LANES]
            put(3, c, dyb * h_ref[:, sl] * dgel)
            g_scr[:, sl] = dyb * gel
            a_scr[:, sl] = a
            x_scr[:, sl] = xr
            r_scr[:, sl] = r
            i_scr[:, sl] = i
            m_scr[:, sl] = m

        def step(j, carry):
            r = ts - 1 - j
            lam_t = g_scr[pl.ds(r, 1), :] + carry
            l_scr[pl.ds(r, 1), :] = lam_t
            return a_scr[pl.ds(r, 1), :] * lam_t

        cl_scr[0:1, :] = lax.fori_loop(0, ts, step, cl_scr[0:1, :], unroll=8)

        for c in range(nch):
            sl = slice(c * LANES, (c + 1) * LANES)
            lam_t = l_scr[:, sl]
            hprev = _shift_down(h_ref[:, sl], jnp.where(first, 0.0, hp_ref[:, sl]), 1, rows)
            xr, r, i, m, a = x_scr[:, sl], r_scr[:, sl], i_scr[:, sl], m_scr[:, sl], a_scr[:, sl]
            da = lam_t * hprev
            dm = lam_t * i * xr
            di = lam_t * m * xr
            dxr = lam_t * m * i
            dlog_a = da * a - dm * a * a / m
            lam_p = lam_ref[:, sl]
            dr = dlog_a * (LRU_C * _log_sigmoid(lam_p))
            acc(_SG_LAM, sl, dlog_a * r * (LRU_C * _sigmoid(-lam_p)))
            dpa = dr * r * (1.0 - r)
            dpx = di * i * (1.0 - i)
            dpa_b, dpx_b = dpa.astype(BF16), dpx.astype(BF16)
            dxr = (dxr + lax.dot_general(dpa_b, wa_ref[c], _DIMS["nt"], preferred_element_type=F32)
                   + lax.dot_general(dpx_b, wx_ref[c], _DIMS["nt"], preferred_element_type=F32))
            xr_ref[:, sl] = xr.astype(BF16)
            dpa_ref[:, sl] = dpa_b
            dpx_ref[:, sl] = dpx_b
            acc(_SG_BA, sl, dpa)
            acc(_SG_BX, sl, dpx)
            acc(_SG_BIAS, sl, dxr)
            nxt = cdx_scr[:, sl]
            wb4 = cb_ref[:, sl]
            put(4, c, wb4[3:4] * dxr + wb4[2:3] * _shift_up(dxr, nxt, 1, rows)
                + wb4[1:2] * _shift_up(dxr, nxt, 2, rows) + wb4[0:1] * _shift_up(dxr, nxt, 3, rows))
            cdx_scr[:, sl] = dxr[0:SUBLANES]
            xb, xbp = cur(4, c), prev(4, c)
            acc(_SG_CONV_B + 3, sl, dxr * xb)
            acc(_SG_CONV_B + 2, sl, dxr * _shift_down(xb, xbp, 1, rows))
            acc(_SG_CONV_B + 1, sl, dxr * _shift_down(xb, xbp, 2, rows))
            acc(_SG_CONV_B + 0, sl, dxr * _shift_down(xb, xbp, 3, rows))

    blk = lambda width: pl.BlockSpec((ts, width), lambda p: (nt - 1 - p, 0))
    pre = lambda width: pl.BlockSpec(
        (SUBLANES, width), lambda p: (jnp.maximum((nt - 1 - p) * tpb - 1, 0), 0))
    vec = lambda n: _whole((n, w))
    big = lambda: pltpu.VMEM((ts, w), F32)
    small = lambda: pltpu.VMEM((SUBLANES, w), F32)
    return pl.pallas_call(
        body, name="mixer_bwd", grid=(nt,),
        in_specs=[blk(w5), pre(w5), blk(w), pre(w), blk(2 * w),
                  vec(3), vec(4), vec(1), _whole(wa_blk.shape), vec(1), _whole(wx_blk.shape), vec(1), vec(1)],
        out_specs=[blk(w5), blk(w), blk(w), blk(w), _whole((_SG_ROWS, w))],
        out_shape=[jax.ShapeDtypeStruct((s, w5), BF16), jax.ShapeDtypeStruct((s, w), BF16),
                   jax.ShapeDtypeStruct((s, w), BF16), jax.ShapeDtypeStruct((s, w), BF16),
                   jax.ShapeDtypeStruct((_SG_ROWS, w), F32)],
        scratch_shapes=[big(), big(), big(), big(), big(), big(), big(), small(), small(), small()],
        compiler_params=_cp(("arbitrary",)),
    )(proj, proj, hseq, hseq, dy, conv_a, conv_b, bias, wa_blk, ba, wx_blk, bx, lam)


def _split_dot(v, tri2):
    hi = v.astype(BF16)
    lo = (v - hi.astype(F32)).astype(BF16)
    return jnp.dot(jnp.concatenate([hi, lo], axis=1), tri2, preferred_element_type=F32)


def _tri(cmp):
    r = lax.broadcasted_iota(jnp.int32, (LANES, LANES), 0)
    c = lax.broadcasted_iota(jnp.int32, (LANES, LANES), 1)
    return cmp(r, c).astype(BF16)


def _lane_blocks(v):
    return [v[:, b * LANES:(b + 1) * LANES] for b in range(v.shape[1] // LANES)]


def _last_lane(v):
    return jnp.broadcast_to(v[:, LANES - 1:LANES], v.shape)


def _scores(q, kb, scale):
    return lax.dot_general(q, kb, _DIMS["nt"], preferred_element_type=F32) * scale


def _log_gates(z, diagonal):
    ls = jnp.minimum(z, 0.0) - jnp.log(1.0 + jnp.exp(-jnp.abs(z)))
    ln = ls - z
    valid = None
    if diagonal:
        valid = (lax.broadcasted_iota(jnp.int32, z.shape, 1) < lax.broadcasted_iota(jnp.int32, z.shape, 0))
        ln = jnp.where(valid, ln, 0.0)
    return ls, ln, valid


def _attn_fwd(qkv, heads):
    s = qkv.shape[0]
    dh = LANES
    tq = _pick(s, ATT_TILE)
    nq = s // tq
    nb = tq // LANES
    scale = 1.0 / math.sqrt(dh)

    hp = ATT_HEADS_PER_STEP
    groups = heads // hp
    wid = hp * dh

    def body(q_ref, k_ref, v_ref, o_ref, tot_ref, acc_scr, car_scr):
        qi = pl.program_id(1)
        acc_scr[...] = jnp.zeros_like(acc_scr)
        car_scr[...] = jnp.zeros_like(car_scr)
        tri = _tri(lambda r, c: r > c)
        tri = jnp.concatenate([tri, tri], axis=0)

        def tile(kt, diagonal):
            k0 = pl.multiple_of(kt * tq, tq)
            heads_cols = [slice(hh * dh, (hh + 1) * dh) for hh in range(hp)]
            zs = [_scores(q_ref[:, cols], k_ref[pl.ds(k0, tq), cols], scale) for cols in heads_cols]
            gates = [_log_gates(z, diagonal) for z in zs]
            sfxs = [_split_dot(jnp.concatenate(_lane_blocks(ln), axis=0), tri) for _, ln, _ in gates]
            for cols, (ls, ln, valid), sfx in zip(heads_cols, gates, sfxs):
                blocks = _lane_blocks(ln)
                car = car_scr[:, cols]
                parts = [None] * nb
                for b in reversed(range(nb)):
                    sb = sfx[b * tq:(b + 1) * tq]
                    parts[b] = sb + car
                    car = car + (sb[:, 0:1] + blocks[b][:, 0:1])
                car_scr[:, cols] = car
                wgt = jnp.exp(ls + jnp.concatenate(parts, axis=1))
                if diagonal:
                    wgt = jnp.where(valid, wgt, 0.0)
                acc_scr[:, cols] += jnp.dot(
                    wgt.astype(BF16), v_ref[pl.ds(k0, tq), cols], preferred_element_type=F32)

        tile(qi, True)

        def step(j, carry):
            tile(qi - 1 - j, False)
            return carry

        lax.fori_loop(0, qi, step, 0)
        o_ref[...] = acc_scr[...].astype(BF16)
        tot_ref[...] = car_scr[...]

    return pl.pallas_call(
        body, name="attn_fwd", grid=(groups, nq),
        in_specs=[pl.BlockSpec((tq, wid), lambda h, i: (i, h)),
                  pl.BlockSpec((s, wid), lambda h, i: (0, groups + h)),
                  pl.BlockSpec((s, wid), lambda h, i: (0, 2 * groups + h))],
        out_specs=[pl.BlockSpec((tq, wid), lambda h, i: (i, h)), pl.BlockSpec((tq, wid), lambda h, i: (i, h))],
        out_shape=[jax.ShapeDtypeStruct((s, heads * dh), BF16), jax.ShapeDtypeStruct((s, heads * dh), F32)],
        scratch_shapes=[pltpu.VMEM((tq, wid), F32), pltpu.VMEM((tq, wid), F32)],
        compiler_params=_cp(("parallel", "arbitrary")),
    )(qkv, qkv, qkv)


def _attn_bwd(qkv, tot, do, heads):
    s = qkv.shape[0]
    dh = LANES
    tq = _pick(s, ATT_TILE)
    nq = s // tq
    nb = tq // LANES
    scale = 1.0 / math.sqrt(dh)

    hp = ATT_HEADS_PER_STEP
    groups = heads // hp
    wid = hp * dh

    def body(q_ref, k_ref, v_ref, tot_ref, do_ref, dq_ref, dk_ref, dv_ref,
             dq_scr, dk_scr, dv_scr, cl_scr, cg_scr):
        qi = pl.program_id(1)

        @pl.when(qi == 0)
        def _():
            dk_scr[...] = jnp.zeros_like(dk_scr)
            dv_scr[...] = jnp.zeros_like(dv_scr)

        dq_scr[...] = jnp.zeros_like(dq_scr)
        cl_scr[...] = jnp.zeros_like(cl_scr)
        cg_scr[...] = jnp.zeros_like(cg_scr)
        tri_le = _tri(lambda r, c: r <= c)
        tri_le = jnp.concatenate([tri_le, tri_le], axis=0)
        tri_lt = _tri(lambda r, c: r < c)

        def tile(kt, diagonal):
            k0 = pl.multiple_of(kt * tq, tq)
            heads_cols = [slice(hh * dh, (hh + 1) * dh) for hh in range(hp)]
            keys = pl.ds(k0, tq)
            zs = [_scores(q_ref[:, cols], k_ref[keys, cols], scale) for cols in heads_cols]
            dws = [lax.dot_general(do_ref[:, cols], v_ref[keys, cols], _DIMS["nt"], preferred_element_type=F32)
                   for cols in heads_cols]
            gates = [_log_gates(z, diagonal) for z in zs]
            pins = [_split_dot(jnp.concatenate(_lane_blocks(ln), axis=0), tri_le) for _, ln, _ in gates]
            wgts, gs = [], []
            for cols, (ls, _, valid), pin, dw in zip(heads_cols, gates, pins, dws):
                total = tot_ref[:, cols]
                cl = cl_scr[:, cols]
                parts = []
                for b in range(nb):
                    pb = pin[b * tq:(b + 1) * tq] + cl
                    parts.append(total - pb)
                    cl = _last_lane(pb)
                cl_scr[:, cols] = cl
                wgt = jnp.exp(ls + jnp.concatenate(parts, axis=1))
                if diagonal:
                    wgt = jnp.where(valid, wgt, 0.0)
                wgts.append(wgt)
                gs.append(wgt * dw)
            pexs = [jnp.dot(jnp.concatenate(_lane_blocks(g), axis=0).astype(BF16), tri_lt,
                            preferred_element_type=F32) for g in gs]
            for cols, wgt in zip(heads_cols, wgts):
                dv_scr[keys, cols] += lax.dot_general(
                    wgt.astype(BF16), do_ref[:, cols], _DIMS["tn"], preferred_element_type=F32)
            for cols, (ls, _, valid), g, pex in zip(heads_cols, gates, gs, pexs):
                gblocks = _lane_blocks(g)
                cg = cg_scr[:, cols]
                parts = []
                for b in range(nb):
                    pb = pex[b * tq:(b + 1) * tq] + cg
                    parts.append(pb)
                    cg = _last_lane(pb + gblocks[b])
                cg_scr[:, cols] = cg
                dz = g - jnp.exp(ls) * (g + jnp.concatenate(parts, axis=1))
                if diagonal:
                    dz = jnp.where(valid, dz, 0.0)
                dz = dz.astype(BF16)
                dq_scr[:, cols] += jnp.dot(dz, k_ref[keys, cols], preferred_element_type=F32)
                dk_scr[keys, cols] += lax.dot_general(
                    dz, q_ref[:, cols], _DIMS["tn"], preferred_element_type=F32)

        def step(j, carry):
            tile(j, False)
            return carry

        lax.fori_loop(0, qi, step, 0)
        tile(qi, True)
        dq_ref[...] = (dq_scr[...] * scale).astype(BF16)

        @pl.when(qi == nq - 1)
        def _():
            dk_ref[...] = (dk_scr[...] * scale).astype(BF16)
            dv_ref[...] = dv_scr[...].astype(BF16)

    qblk = pl.BlockSpec((tq, wid), lambda h, i: (i, h))
    hblk = pl.BlockSpec((s, wid), lambda h, i: (0, h))
    out = jax.ShapeDtypeStruct((s, heads * dh), BF16)
    return pl.pallas_call(
        body, name="attn_bwd", grid=(groups, nq),
        in_specs=[qblk, pl.BlockSpec((s, wid), lambda h, i: (0, groups + h)),
                  pl.BlockSpec((s, wid), lambda h, i: (0, 2 * groups + h)), qblk, qblk],
        out_specs=[qblk, hblk, hblk], out_shape=[out, out, out],
        scratch_shapes=[pltpu.VMEM((tq, wid), F32), pltpu.VMEM((s, wid), F32), pltpu.VMEM((s, wid), F32),
                        pltpu.VMEM((tq, wid), F32), pltpu.VMEM((tq, wid), F32)],
        compiler_params=_cp(("parallel", "arbitrary")),
    )(qkv, qkv, qkv, tot, do)


def _place():
    x, y, c = lax.axis_index("x"), lax.axis_index("y"), lax.axis_index("c")
    chips = [(1 - x, y), (x, 1 - y), (1 - x, 1 - y)]
    return x, y, c, chips


def _hbm_specs(n):
    return [pl.BlockSpec(memory_space=pl.ANY) for _ in range(n)]


def _remote(src, dst, send_sem, recv_sem, dev):
    return pltpu.make_async_remote_copy(
        src_ref=src, dst_ref=dst, send_sem=send_sem, recv_sem=recv_sem, device_id=dev, device_id_type=MESH)


def _allgather_weights(name, fulls):
    n = len(fulls)

    def body(*refs):
        bufs = refs[n:2 * n]
        send_sems, recv_sems, fsend_sems, frecv_sems = refs[2 * n:]
        x, y, c, chips = _place()
        me = 2 * x + y
        sibling = (x, y, 1 - c)
        firsts = []
        for w in range(n):
            hr = bufs[w].shape[1] // 2
            mine = bufs[w].at[me, pl.ds(c * hr, hr)]
            for k, (px, py) in enumerate(chips):
                cp = _remote(mine, mine, send_sems.at[w, k], recv_sems.at[w, k], (px, py, c))
                cp.start()
                firsts.append(cp)
        passed = []
        for w in range(n):
            hr = bufs[w].shape[1] // 2
            for k, (px, py) in enumerate(chips):
                slot = bufs[w].at[2 * px + py, pl.ds(c * hr, hr)]
                _remote(slot, slot, send_sems.at[w, k], recv_sems.at[w, k], (px, py, c)).wait_recv()
                cp = _remote(slot, slot, fsend_sems.at[w, k], frecv_sems.at[w, k], sibling)
                cp.start()
                passed.append(cp)
        for w in range(n):
            hr = bufs[w].shape[1] // 2
            for k, (px, py) in enumerate(chips):
                slot = bufs[w].at[2 * px + py, pl.ds((1 - c) * hr, hr)]
                _remote(slot, slot, fsend_sems.at[w, k], frecv_sems.at[w, k], sibling).wait_recv()
        for cp in firsts + passed:
            cp.wait_send()

    sem = lambda: pltpu.SemaphoreType.DMA((n, 3))
    return pl.pallas_call(
        body, name=name, in_specs=_hbm_specs(n), out_specs=_hbm_specs(n),
        out_shape=[jax.ShapeDtypeStruct(f.shape, f.dtype) for f in fulls],
        input_output_aliases={w: w for w in range(n)},
        scratch_shapes=[sem(), sem(), sem(), sem()],
    )(*fulls)


def _handshake(peers):
    barrier = pltpu.get_barrier_semaphore()
    for dev in peers:
        pl.semaphore_signal(barrier, inc=1, device_id=dev, device_id_type=MESH)
    pl.semaphore_wait(barrier, len(peers))


def _allgather_async(name, slot_buf, collective_id):
    buf = jax.new_ref(slot_buf, memory_space=pltpu.MemorySpace.HBM)
    hr = slot_buf.shape[1] // 2
    dma = pltpu.SemaphoreType.DMA

    @pl.kernel(mesh=plsc.ScalarSubcoreMesh(axis_name="seq", num_cores=1), name=name,
               scratch_types=(dma,) * 12, compiler_params=pltpu.CompilerParams(collective_id=collective_id))
    def launch(*sems):
        send_sems, recv_sems, fsend_sems, frecv_sems = sems[0:3], sems[3:6], sems[6:9], sems[9:12]
        x, y, c, chips = _place()
        me = 2 * x + y
        sibling = (x, y, 1 - c)
        _handshake([(px, py, c) for px, py in chips] + [sibling])
        mine = buf.at[me, pl.ds(c * hr, hr)]
        firsts = []
        for k, (px, py) in enumerate(chips):
            cp = _remote(mine, mine, send_sems[k], recv_sems[k], (px, py, c))
            cp.start()
            firsts.append(cp)
        passed = []
        for k, (px, py) in enumerate(chips):
            slot = buf.at[2 * px + py, pl.ds(c * hr, hr)]
            _remote(slot, slot, send_sems[k], recv_sems[k], (px, py, c)).wait_recv()
            cp = _remote(slot, slot, fsend_sems[k], frecv_sems[k], sibling)
            cp.start()
            passed.append(cp)
        for k, (px, py) in enumerate(chips):
            slot = buf.at[2 * px + py, pl.ds((1 - c) * hr, hr)]
            _remote(slot, slot, fsend_sems[k], frecv_sems[k], sibling).wait_recv()
        for cp in firsts + passed:
            cp.wait_send()

    launch()
    return buf[...]


def _sequencer_kernel(name, n_sems, collective_id):
    return functools.partial(
        pl.kernel, mesh=plsc.ScalarSubcoreMesh(axis_name="seq", num_cores=1), name=name,
        scratch_types=(pltpu.SemaphoreType.DMA,) * n_sems,
        compiler_params=pltpu.CompilerParams(collective_id=collective_id))


def _to_sibling_async(name, slab):
    src = jax.new_ref(slab, memory_space=pltpu.MemorySpace.HBM)
    hr = slab.shape[1] // 2
    got = jax.empty_ref(jax.ShapeDtypeStruct((N_CHIPS, hr, slab.shape[2]), slab.dtype),
                        memory_space=pltpu.MemorySpace.HBM)

    @_sequencer_kernel(name, 2, COLLECTIVE_SIBLING)
    def launch(send_sem, recv_sem):
        x, y, c, _ = _place()
        _handshake([(x, y, 1 - c)])
        _remote(src.at[:, pl.ds((1 - c) * hr, hr), :], got, send_sem, recv_sem, (x, y, 1 - c)).start()
        _remote(got, got, send_sem, recv_sem, (x, y, 1 - c)).wait()

    launch()
    return src[...], got[...]


def _to_chips_async(name, part):
    src = jax.new_ref(part, memory_space=pltpu.MemorySpace.HBM)
    got = jax.empty_ref(jax.ShapeDtypeStruct((3,) + part.shape[1:], part.dtype), memory_space=pltpu.MemorySpace.HBM)

    @_sequencer_kernel(name, 6, COLLECTIVE_CHIPS)
    def launch(*sems):
        send_sems, recv_sems = sems[0:3], sems[3:6]
        x, y, c, chips = _place()
        _handshake([(px, py, c) for px, py in chips])
        cps = []
        for k, (px, py) in enumerate(chips):
            cp = _remote(src.at[2 * px + py], got.at[k], send_sems[k], recv_sems[k], (px, py, c))
            cp.start()
            cps.append(cp)
        for cp in cps:
            cp.wait()

    launch()
    return src[...], got[...]


def _join_sibling_async(name, half_filled):
    buf = jax.new_ref(half_filled, memory_space=pltpu.MemorySpace.HBM)
    hr = half_filled.shape[0] // 2

    @_sequencer_kernel(name, 2, COLLECTIVE_SIBLING)
    def launch(send_sem, recv_sem):
        x, y, c, _ = _place()
        _handshake([(x, y, 1 - c)])
        mine = buf.at[pl.ds(c * hr, hr)]
        other = buf.at[pl.ds((1 - c) * hr, hr)]
        cp = _remote(mine, mine, send_sem, recv_sem, (x, y, 1 - c))
        cp.start()
        _remote(other, other, send_sem, recv_sem, (x, y, 1 - c)).wait_recv()
        cp.wait_send()

    launch()
    return buf[...]


def _exchange_sibling_halves(name, slabs):
    n = len(slabs)

    def body(*refs):
        ins, outs = refs[:n], refs[n:2 * n]
        send_sems, recv_sems = refs[2 * n:]
        x, y, c, _ = _place()
        cps = []
        for w in range(n):
            hr = ins[w].shape[1] // 2
            cp = _remote(ins[w].at[:, pl.ds((1 - c) * hr, hr), :], outs[w], send_sems.at[w], recv_sems.at[w],
                         (x, y, 1 - c))
            cp.start()
            cps.append(cp)
        for cp in cps:
            cp.wait()

    return pl.pallas_call(
        body, name=name, in_specs=_hbm_specs(n), out_specs=_hbm_specs(n),
        out_shape=[jax.ShapeDtypeStruct((N_CHIPS, s.shape[1] // 2, s.shape[2]), s.dtype) for s in slabs],
        scratch_shapes=[pltpu.SemaphoreType.DMA((n,)), pltpu.SemaphoreType.DMA((n,))],
    )(*slabs)


def _exchange_chips(name, parts):
    n = len(parts)

    def body(*refs):
        ins, outs = refs[:n], refs[n:2 * n]
        send_sems, recv_sems = refs[2 * n:]
        x, y, c, chips = _place()
        cps = []
        for w in range(n):
            for k, (px, py) in enumerate(chips):
                cp = _remote(ins[w].at[2 * px + py], outs[w].at[k], send_sems.at[w, k], recv_sems.at[w, k],
                             (px, py, c))
                cp.start()
                cps.append(cp)
        for cp in cps:
            cp.wait()

    return pl.pallas_call(
        body, name=name, in_specs=_hbm_specs(n), out_specs=_hbm_specs(n),
        out_shape=[jax.ShapeDtypeStruct((3,) + s.shape[1:], s.dtype) for s in parts],
        scratch_shapes=[pltpu.SemaphoreType.DMA((n, 3)), pltpu.SemaphoreType.DMA((n, 3))],
    )(*parts)


def _join_sibling_halves(name, bufs):
    n = len(bufs)

    def body(*refs):
        outs = refs[n:2 * n]
        send_sems, recv_sems = refs[2 * n:]
        x, y, c, _ = _place()
        cps = []
        for w in range(n):
            hr = outs[w].shape[0] // 2
            mine = outs[w].at[pl.ds(c * hr, hr)]
            cp = _remote(mine, mine, send_sems.at[w], recv_sems.at[w], (x, y, 1 - c))
            cp.start()
            cps.append(cp)
        for w in range(n):
            hr = outs[w].shape[0] // 2
            other = outs[w].at[pl.ds((1 - c) * hr, hr)]
            _remote(other, other, send_sems.at[w], recv_sems.at[w], (x, y, 1 - c)).wait_recv()
        for cp in cps:
            cp.wait_send()

    return pl.pallas_call(
        body, name=name, in_specs=_hbm_specs(n), out_specs=_hbm_specs(n),
        out_shape=[jax.ShapeDtypeStruct(b.shape, b.dtype) for b in bufs],
        input_output_aliases={w: w for w in range(n)},
        scratch_shapes=[pltpu.SemaphoreType.DMA((n,)), pltpu.SemaphoreType.DMA((n,))],
    )(*bufs)


def _allgather_chips_small(name, v):
    r = v.shape[0]

    def body(v_ref, o_ref, send_sems, recv_sems):
        x, y, c, chips = _place()
        me = 2 * x + y
        o_ref[me] = v_ref[...]
        cps = []
        for k, (px, py) in enumerate(chips):
            cp = _remote(v_ref, o_ref.at[me], send_sems.at[k], recv_sems.at[k], (px, py, c))
            cp.start()
            cps.append(cp)
        for k, (px, py) in enumerate(chips):
            slot = o_ref.at[2 * px + py]
            _remote(slot, slot, send_sems.at[k], recv_sems.at[k], (px, py, c)).wait_recv()
        for cp in cps:
            cp.wait_send()

    return pl.pallas_call(
        body, name=name, in_specs=[pl.BlockSpec(memory_space=pltpu.VMEM)],
        out_specs=pl.BlockSpec(memory_space=pltpu.VMEM),
        out_shape=jax.ShapeDtypeStruct((N_CHIPS, r, LANES), F32),
        scratch_shapes=[pltpu.SemaphoreType.DMA((3,)), pltpu.SemaphoreType.DMA((3,))],
    )(v)


def _allreduce_small(name, v):
    r = v.shape[0]
    hr = r // 2
    assert hr % SUBLANES == 0

    def body(v_ref, o_ref, sib_ref, chips_ref, send_sems, recv_sems):
        x, y, c, chips = _place()
        me = 2 * x + y
        sibling = (x, y, 1 - c)
        first = _remote(v_ref, sib_ref, send_sems.at[0], recv_sems.at[0], sibling)
        first.start()
        first.wait()
        mine = pl.ds(pl.multiple_of(c * hr, SUBLANES), hr)
        chips_ref[me] = v_ref[mine, :] + sib_ref[mine, :]
        cps = []
        for k, (px, py) in enumerate(chips):
            cp = _remote(chips_ref.at[me], chips_ref.at[me], send_sems.at[1 + k], recv_sems.at[1 + k], (px, py, c))
            cp.start()
            cps.append(cp)
        for k, (px, py) in enumerate(chips):
            slot = chips_ref.at[2 * px + py]
            _remote(slot, slot, send_sems.at[1 + k], recv_sems.at[1 + k], (px, py, c)).wait_recv()
        total = chips_ref[0]
        for j in range(1, N_CHIPS):
            total = total + chips_ref[j]
        o_ref[mine, :] = total
        last = _remote(o_ref.at[mine], o_ref.at[mine], send_sems.at[4], recv_sems.at[4], sibling)
        last.start()
        other = o_ref.at[pl.ds(pl.multiple_of((1 - c) * hr, SUBLANES), hr)]
        _remote(other, other, send_sems.at[4], recv_sems.at[4], sibling).wait_recv()
        last.wait_send()
        for cp in cps:
            cp.wait_send()

    return pl.pallas_call(
        body, name=name, in_specs=[pl.BlockSpec(memory_space=pltpu.VMEM)],
        out_specs=pl.BlockSpec(memory_space=pltpu.VMEM),
        out_shape=jax.ShapeDtypeStruct((r, LANES), F32),
        scratch_shapes=[pltpu.VMEM((r, LANES), F32), pltpu.VMEM((N_CHIPS, hr, LANES), F32),
                        pltpu.SemaphoreType.DMA((5,)), pltpu.SemaphoreType.DMA((5,))],
    )(v)


def _add_sibling(name, slabs, recv, c):
    _, r, cols = slabs.shape
    hr = r // 2
    tr = _pick(hr, ROW_TILE)
    nb = hr // tr

    def body(c_ref, a_ref, b_ref, o_ref):
        o_ref[...] = (a_ref[...].astype(F32) + b_ref[...].astype(F32)).astype(BF16)

    grid_spec = pltpu.PrefetchScalarGridSpec(
        num_scalar_prefetch=1, grid=(N_CHIPS, nb),
        in_specs=[pl.BlockSpec((None, tr, cols), lambda j, i, c_ref: (j, c_ref[0] * nb + i, 0)),
                  pl.BlockSpec((None, tr, cols), lambda j, i, c_ref: (j, i, 0))],
        out_specs=pl.BlockSpec((None, tr, cols), lambda j, i, c_ref: (j, i, 0)))
    return pl.pallas_call(
        body, name=name, grid_spec=grid_spec,
        out_shape=jax.ShapeDtypeStruct((N_CHIPS, hr, cols), BF16),
        compiler_params=_cp(("parallel", "parallel")))(jnp.reshape(c, (1,)).astype(jnp.int32), slabs, recv)


def _sum_chips(name, own, recv, chip, c):
    _, hr, cols = recv.shape
    tr = _pick(hr, ROW_TILE)
    nb = hr // tr

    def body(sc_ref, own_ref, recv_ref, o_ref):
        total = own_ref[...].astype(F32)
        for k in range(3):
            total = total + recv_ref[k].astype(F32)
        o_ref[...] = total

    grid_spec = pltpu.PrefetchScalarGridSpec(
        num_scalar_prefetch=1, grid=(nb,),
        in_specs=[pl.BlockSpec((None, tr, cols), lambda i, sc: (sc[0], i, 0)),
                  pl.BlockSpec((3, tr, cols), lambda i, sc: (0, i, 0))],
        out_specs=pl.BlockSpec((tr, cols), lambda i, sc: (sc[1] * nb + i, 0)))
    return pl.pallas_call(
        body, name=name, grid_spec=grid_spec, out_shape=jax.ShapeDtypeStruct((2 * hr, cols), F32),
        compiler_params=_cp(("parallel",)))(jnp.stack([chip, c]).astype(jnp.int32), own, recv)


def _adamw_math(w, g, m, v):
    m = ADAM_B1 * m + (1.0 - ADAM_B1) * g
    v = ADAM_B2 * v + (1.0 - ADAM_B2) * (g * g)
    m_hat = m / (1.0 - ADAM_B1 ** ADAM_STEP)
    v_hat = v / (1.0 - ADAM_B2 ** ADAM_STEP)
    delta = -ADAM_LR * (m_hat / (jnp.sqrt(v_hat) + ADAM_EPS) + ADAM_WD * w)
    return delta, m, v


def _adamw(name, w, gs, m, v):
    nl, r, cols = w.shape
    tr = _pick(r, LANES)

    def body(*refs):
        w_ref, m_ref, v_ref = refs[0:3]
        g_refs = refs[3:3 + nl]
        go_ref, d_ref, nm_ref, nv_ref = refs[3 + nl:]
        layer = pl.program_id(0)
        g = g_refs[0][...]
        for j in range(1, nl):
            g = jnp.where(layer == j, g_refs[j][...], g)
        d, nm, nv = _adamw_math(w_ref[...], g, m_ref[...], v_ref[...])
        go_ref[...] = g
        d_ref[...] = d
        nm_ref[...] = nm
        nv_ref[...] = nv

    spec3 = pl.BlockSpec((None, tr, cols), lambda l, i: (l, i, 0))
    gspec = pl.BlockSpec((tr, cols), lambda l, i: (i, 0))
    out = jax.ShapeDtypeStruct((nl, r, cols), F32)
    return pl.pallas_call(
        body, name=name, grid=(nl, r // tr), in_specs=[spec3] * 3 + [gspec] * nl, out_specs=[spec3] * 4,
        out_shape=[out] * 4, compiler_params=_cp(("parallel", "parallel")))(w, m, v, *gs)


def _adamw_small(name, groups):
    n = len(groups)
    flat = [a for grp in groups for a in grp]

    def body(*refs):
        ins, outs = refs[:4 * n], refs[4 * n:]
        for p in range(n):
            w_ref, g_ref, m_ref, v_ref = ins[4 * p:4 * p + 4]
            d, nm, nv = _adamw_math(w_ref[...], g_ref[...], m_ref[...], v_ref[...])
            outs[3 * p][...] = d
            outs[3 * p + 1][...] = nm
            outs[3 * p + 2][...] = nv

    vm = pl.BlockSpec(memory_space=pltpu.VMEM)
    out_shape = [jax.ShapeDtypeStruct(grp[0].shape, F32) for grp in groups for _ in range(3)]
    res = pl.pallas_call(
        body, name=name, in_specs=[vm] * (4 * n), out_specs=[vm] * (3 * n), out_shape=out_shape)(*flat)
    return [tuple(res[3 * p:3 * p + 3]) for p in range(n)]


def _block_diag_pairs(w):
    h, d, _ = w.shape
    z = jnp.zeros((h // 2, d, d), w.dtype)
    top = jnp.concatenate([w[0::2], z], axis=2)
    bot = jnp.concatenate([z, w[1::2]], axis=2)
    return jnp.concatenate([top, bot], axis=1).astype(BF16)


def _diag_pairs_to_heads(g, d):
    a = g[:, :d, :d]
    b = g[:, d:, d:]
    return jnp.stack([a, b], axis=1).reshape(-1, d, d)


def _rows128(a):
    flat = a.reshape(-1, LANES)
    pad = (-flat.shape[0]) % SUBLANES
    if pad:
        flat = jnp.concatenate([flat, jnp.zeros((pad, LANES), flat.dtype)], axis=0)
    return flat


def _unshard_last(g4, shape):
    g4 = g4.reshape((N_CHIPS,) + tuple(shape))
    return jnp.concatenate([g4[j] for j in range(N_CHIPS)], axis=-1)


def kernel(x, norm_gains, hyb_w_in, hyb_conv_a, hyb_conv_b, hyb_conv_b_bias, hyb_rg_w_a, hyb_rg_b_a, hyb_rg_w_x, hyb_rg_b_x, hyb_rg_lambda, hyb_w_out, sb_w_qkv, sb_w_o, mlp_w_up, mlp_w_down, loss_target, m_norm_gains, m_hyb_w_in, m_hyb_conv_a, m_hyb_conv_b, m_hyb_conv_b_bias, m_hyb_rg_w_a, m_hyb_rg_b_a, m_hyb_rg_w_x, m_hyb_rg_b_x, m_hyb_rg_lambda, m_hyb_w_out, m_sb_w_qkv, m_sb_w_o, m_mlp_w_up, m_mlp_w_down, v_norm_gains, v_hyb_w_in, v_hyb_conv_a, v_hyb_conv_b, v_hyb_conv_b_bias, v_hyb_rg_w_a, v_hyb_rg_b_a, v_hyb_rg_w_x, v_hyb_rg_b_x, v_hyb_rg_lambda, v_hyb_w_out, v_sb_w_qkv, v_sb_w_o, v_mlp_w_up, v_mlp_w_down):
    cx_ = lax.axis_index("x")
    cy_ = lax.axis_index("y")
    cc_ = lax.axis_index("c")
    chip = 2 * cx_ + cy_

    x0 = x[0]
    target = loss_target[0]
    s, d = x0.shape
    heads = SB_HEADS
    assert d // heads == LANES
    n_rg, hd = hyb_rg_w_a.shape[1], hyb_rg_w_a.shape[2]
    wmix = n_rg * hd
    assert 2 * hd == LANES

    big = {
        "hyb_w_in": (hyb_w_in, 0), "hyb_w_out": (hyb_w_out, 0), "mlp_w_up0": (mlp_w_up, 0),
        "mlp_w_down0": (mlp_w_down, 0), "sb_w_qkv": (sb_w_qkv, 0), "sb_w_o": (sb_w_o, 0),
        "mlp_w_up1": (mlp_w_up, 1), "mlp_w_down1": (mlp_w_down, 1),
    }
    names = list(big)
    slots = [_cast_into_slot("cast_" + k, big[k][0], big[k][1], chip) for k in names]
    full = {k: _allgather_async("allgather_" + k, slot, cid) for cid, (k, slot) in enumerate(zip(names, slots))}
    rowsharded = lambda k: full[k].reshape(-1, full[k].shape[2])

    ng_s, ca_s, cb_s = norm_gains.reshape(-1, norm_gains.shape[2]), hyb_conv_a[0], hyb_conv_b[0]
    packed = jnp.concatenate([_rows128(ng_s), _rows128(ca_s), _rows128(cb_s)], axis=0)
    gathered = _allgather_chips_small("allgather_small", packed)
    n0 = ng_s.size // LANES
    n1 = n0 + (-n0) % SUBLANES
    m0 = ca_s.size // LANES
    m1 = m0 + (-m0) % SUBLANES
    k0 = cb_s.size // LANES
    gains = _unshard_last(gathered[:, 0:n0], ng_s.shape).reshape(2, 4, 1, d)
    conv_a = _unshard_last(gathered[:, n1:n1 + m0], ca_s.shape)
    conv_b = _unshard_last(gathered[:, n1 + m1:n1 + m1 + k0], cb_s.shape)
    bias, b_a, b_x, lam = hyb_conv_b_bias, hyb_rg_b_a, hyb_rg_b_x, hyb_rg_lambda
    wa_blk = _block_diag_pairs(hyb_rg_w_a[0])
    wx_blk = _block_diag_pairs(hyb_rg_w_x[0])

    relu_sq = lambda acc: (jnp.maximum(acc, 0.0), jnp.square(jnp.maximum(acc, 0.0)))

    h1 = _rms_fwd("rms_pre0", x0, gains[0, 0])
    proj = _mm_fwd_col("proj_in", h1, full["hyb_w_in"])[0]
    ycat, hseq = _mixer_fwd(proj, conv_a, conv_b, bias, wa_blk, b_a, wx_blk, b_x, lam)
    mix0 = _mm_fwd_row("proj_out", ycat, rowsharded("hyb_w_out"))
    x1, h2 = _rms_post("rms_mix0", mix0, gains[0, 1], x0, gains[0, 2])
    u0, a0 = _mm_fwd_col("mlp_up0", h2, full["mlp_w_up0"], (BF16, BF16), relu_sq)
    mlp0 = _mm_fwd_row("mlp_down0", a0, rowsharded("mlp_w_down0"))
    x2, h3 = _rms_post("rms_mlp0", mlp0, gains[0, 3], x1, gains[1, 0])

    qkv = _mm_fwd_col("qkv", h3, full["sb_w_qkv"], (BF16,))[0]
    att, tot = _attn_fwd(qkv, heads)
    mix1 = _mm_fwd_row("attn_out", att, rowsharded("sb_w_o"))
    x3, h4 = _rms_post("rms_mix1", mix1, gains[1, 1], x2, gains[1, 2])
    u1, a1 = _mm_fwd_col("mlp_up1", h4, full["mlp_w_up1"], (BF16, BF16), relu_sq)
    mlp1 = _mm_fwd_row("mlp_down1", a1, rowsharded("mlp_w_down1"))
    dy, dmlp1, dgain_mlp1, loss_local = _last_norm_and_loss("last_norm_loss", mlp1, gains[1, 3], x3, target)
    loss = lax.psum(loss_local, ("x", "y", "c"))

    dgain = [[None] * 4 for _ in range(2)]
    drelu = lambda acc, u: (acc * (2.0 * u.astype(F32)),)
    stage_a, stage_b, gfull = {}, {}, {}

    def tie(main, side):
        return lax.optimization_barrier((main, side))

    def reduce_start(k, slab, main):
        main, slab = tie(main, slab)
        stage_a[k] = _to_sibling_async("grads_to_sibling_" + k, slab)
        return main

    def reduce_to_chips(k, main):
        slab, from_sibling = stage_a.pop(k)
        main, part = tie(main, _add_sibling("grads_add_" + k, slab, from_sibling, cc_))
        stage_b[k] = _to_chips_async("grads_to_chips_" + k, part)
        return main

    def after(value, token):
        return tie(value, token)[0]

    def reduce_finish(k, main):
        own, from_chips = stage_b.pop(k)
        main, half = tie(main, _sum_chips("grads_sum_" + k, after(own, main), from_chips, chip, cc_))
        gfull[k] = _join_sibling_async("grads_join_" + k, half)
        return main

    def mlp_bwd(layer, dxo, mlp_out, xin, hin, u, a, through_norm=None):
        down, up = f"mlp_w_down{layer}", f"mlp_w_up{layer}"
        if through_norm is None:
            through_norm = _rms_bwd(f"rms_mlp{layer}_bwd", mlp_out, gains[layer, 3], dxo, out_dtype=BF16)
        dmlp, dgain[layer][3] = through_norm
        wd, wu = rowsharded(down), full[up]
        dmlp = reduce_start(down, _mm_wgrad_row(f"mlp_down{layer}_wgrad", a, dmlp).reshape(N_CHIPS, -1, d), dmlp)
        du = _mm_bwd_row(f"mlp_down{layer}_bwd", dmlp, wd, (BF16,), u, drelu)[0]
        du = reduce_start(up, _mm_wgrad_col(f"mlp_up{layer}_wgrad", hin, du, wu.shape[2]), du)
        du = reduce_to_chips(down, du)
        dh = _mm_bwd_col(f"mlp_up{layer}_bwd", du, wu)
        dh = reduce_to_chips(up, dh)
        dxm, dgain[layer][2] = _rms_bwd(f"rms_premlp{layer}_bwd", xin, gains[layer, 2], dh, res=dxo)
        return dxm

    dx3 = mlp_bwd(1, dy, mlp1, x3, h4, u1, a1, through_norm=(dmlp1, dgain_mlp1))
    dmix1, dgain[1][1] = _rms_bwd("rms_mix1_bwd", mix1, gains[1, 1], dx3, out_dtype=BF16)
    dmix1 = reduce_start("sb_w_o", _mm_wgrad_row("attn_out_wgrad", att, dmix1).reshape(N_CHIPS, -1, d), dmix1)
    datt = _mm_bwd_row("attn_out_bwd", dmix1, rowsharded("sb_w_o"), (BF16,))[0]
    dq, dk, dv = _attn_bwd(qkv, tot, datt, heads)
    dqkv = jnp.concatenate([dq, dk, dv], axis=1)
    dqkv = reduce_to_chips("sb_w_o", dqkv)
    dqkv = reduce_finish("mlp_w_down1", dqkv)
    dqkv = reduce_finish("mlp_w_up1", dqkv)
    dqkv = reduce_start("sb_w_qkv", _mm_wgrad_col("qkv_wgrad", h3, dqkv, full["sb_w_qkv"].shape[2]), dqkv)
    dh3 = _mm_bwd_col("qkv_bwd", dqkv, full["sb_w_qkv"])
    dh3 = reduce_to_chips("sb_w_qkv", dh3)
    dx2, dgain[1][0] = _rms_bwd("rms_pre1_bwd", x2, gains[1, 0], dh3, res=dx3)

    dx1 = mlp_bwd(0, dx2, mlp0, x1, h2, u0, a0)
    dx1 = reduce_finish("sb_w_o", dx1)
    dx1 = reduce_finish("sb_w_qkv", dx1)
    dmix0, dgain[0][1] = _rms_bwd("rms_mix0_bwd", mix0, gains[0, 1], dx1, out_dtype=BF16)
    dmix0 = reduce_finish("mlp_w_down0", dmix0)
    dmix0 = reduce_start("hyb_w_out", _mm_wgrad_row("proj_out_wgrad", ycat, dmix0).reshape(N_CHIPS, -1, d), dmix0)
    dycat = _mm_bwd_row("proj_out_bwd", dmix0, rowsharded("hyb_w_out"))[0]
    dproj, xr_b, dpa_b, dpx_b, sg = _mixer_bwd(
        proj, hseq, dycat, conv_a, conv_b, bias, wa_blk, b_a, wx_blk, b_x, lam)
    dproj = reduce_finish("mlp_w_up0", dproj)
    dproj = reduce_to_chips("hyb_w_out", dproj)
    dproj = reduce_start("hyb_w_in", _mm_wgrad_col("proj_in_wgrad", h1, dproj, full["hyb_w_in"].shape[2]), dproj)
    dh1 = _mm_bwd_col("proj_in_bwd", dproj, full["hyb_w_in"])
    dh1 = reduce_to_chips("hyb_w_in", dh1)
    dx0, dgain[0][0] = _rms_bwd("rms_pre0_bwd", x0, gains[0, 0], dh1, res=dx1)
    dwa = _diag_pairs_to_heads(_mm_wgrad_diag("rg_w_a_wgrad", xr_b, dpa_b), hd)
    dwx = _diag_pairs_to_heads(_mm_wgrad_diag("rg_w_x_wgrad", xr_b, dpx_b), hd)

    dgains = jnp.concatenate([dgain[l][k] for l in range(2) for k in range(4)], axis=0)
    small_parts = [dgains, sg[_SG_CONV_A:_SG_CONV_A + 3], sg[_SG_CONV_B:_SG_CONV_B + 4], sg[_SG_BIAS:_SG_BIAS + 1],
                   dwa, sg[_SG_BA:_SG_BA + 1], dwx, sg[_SG_BX:_SG_BX + 1], sg[_SG_LAM:_SG_LAM + 1]]
    small_rows = [_rows128(p) for p in small_parts]
    n_small = sum(rws.shape[0] for rws in small_rows)
    tail_pad = [jnp.zeros(((-n_small) % (2 * SUBLANES), LANES), F32)] if n_small % (2 * SUBLANES) else []
    reduced = _allreduce_small("allreduce_small", jnp.concatenate(small_rows + tail_pad, axis=0))
    small_full, off = [], 0
    for p, rws in zip(small_parts, small_rows):
        small_full.append(reduced[off:off + p.size // LANES].reshape(p.shape))
        off += rws.shape[0]
    g_gains, g_ca, g_cb, g_bias, g_wa, g_ba, g_wx, g_bx, g_lam = small_full

    def my_cols(g, width):
        return lax.dynamic_slice_in_dim(g, chip * width, width, axis=g.ndim - 1)

    small = [
        ("norm_gains", norm_gains, my_cols(g_gains, norm_gains.shape[2]).reshape(norm_gains.shape),
         m_norm_gains, v_norm_gains),
        ("hyb_conv_a", hyb_conv_a, my_cols(g_ca, hyb_conv_a.shape[2])[None], m_hyb_conv_a, v_hyb_conv_a),
        ("hyb_conv_b", hyb_conv_b, my_cols(g_cb, hyb_conv_b.shape[2])[None], m_hyb_conv_b, v_hyb_conv_b),
        ("hyb_conv_b_bias", hyb_conv_b_bias, g_bias, m_hyb_conv_b_bias, v_hyb_conv_b_bias),
        ("hyb_rg_w_a", hyb_rg_w_a, g_wa[None], m_hyb_rg_w_a, v_hyb_rg_w_a),
        ("hyb_rg_b_a", hyb_rg_b_a, g_ba, m_hyb_rg_b_a, v_hyb_rg_b_a),
        ("hyb_rg_w_x", hyb_rg_w_x, g_wx[None], m_hyb_rg_w_x, v_hyb_rg_w_x),
        ("hyb_rg_b_x", hyb_rg_b_x, g_bx, m_hyb_rg_b_x, v_hyb_rg_b_x),
        ("hyb_rg_lambda", hyb_rg_lambda, g_lam, m_hyb_rg_lambda, v_hyb_rg_lambda),
    ]
    to2d = lambda a: a.reshape(-1, a.shape[-1])
    small_res = _adamw_small("adamw_small", [tuple(to2d(a) for a in (w, g, m, v)) for _, w, g, m, v in small])
    out = {}
    for (nm, w, g, _, _), (dl, nmom, nvar) in zip(small, small_res):
        out[nm] = (g, dl.reshape(w.shape), nmom.reshape(w.shape), nvar.reshape(w.shape))

    stacked = {
        "mlp_w_down": (mlp_w_down, m_mlp_w_down, v_mlp_w_down, ["mlp_w_down0", "mlp_w_down1"]),
        "mlp_w_up": (mlp_w_up, m_mlp_w_up, v_mlp_w_up, ["mlp_w_up0", "mlp_w_up1"]),
        "sb_w_o": (sb_w_o, m_sb_w_o, v_sb_w_o, ["sb_w_o"]),
        "sb_w_qkv": (sb_w_qkv, m_sb_w_qkv, v_sb_w_qkv, ["sb_w_qkv"]),
        "hyb_w_out": (hyb_w_out, m_hyb_w_out, v_hyb_w_out, ["hyb_w_out"]),
        "hyb_w_in": (hyb_w_in, m_hyb_w_in, v_hyb_w_in, ["hyb_w_in"]),
    }

    def update(k, token):
        w, m, v, parts = stacked[k]
        out[k] = tuple(_adamw("adamw_" + k, w, [after(gfull[p], token) for p in parts], m, v))
        return out[k][1]

    token = small_res[0][0]
    token = update("sb_w_qkv", token)
    token = update("sb_w_o", token)
    token = update("mlp_w_down", token)
    token = reduce_finish("hyb_w_out", token)
    token = update("mlp_w_up", token)
    token = reduce_finish("hyb_w_in", token)
    token = update("hyb_w_out", token)
    update("hyb_w_in", token)

    order = ["norm_gains", "hyb_w_in", "hyb_conv_a", "hyb_conv_b", "hyb_conv_b_bias", "hyb_rg_w_a", "hyb_rg_b_a",
             "hyb_rg_w_x", "hyb_rg_b_x", "hyb_rg_lambda", "hyb_w_out", "sb_w_qkv", "sb_w_o", "mlp_w_up",
             "mlp_w_down"]
    return (loss, dx0[None], *[out[k][0] for k in order], *[out[k][1] for k in order],
            *[out[k][2] for k in order], *[out[k][3] for k in order])
```

```python
import functools
import math

import jax
import jax.numpy as jnp
from jax import lax
from jax.experimental import pallas as pl
from jax.experimental.pallas import tpu as pltpu
from jax.experimental.pallas import tpu_sc as plsc

F32 = jnp.float32
BF16 = jnp.bfloat16
MESH = pl.DeviceIdType.MESH

SB_HEADS = 16
NORM_EPS = 1e-6
LRU_C = 8.0
ADAM_LR = 0.001
ADAM_B1 = 0.9
ADAM_B2 = 0.999
ADAM_EPS = 1e-08
ADAM_WD = 0.01
ADAM_STEP = 10

LANES = 128
SUBLANES = 8
VMEM_LIMIT = 48 * 1024 * 1024
MM_TILE = 1024
MM_VMEM_BUDGET = 40 * 1024 * 1024
MM_TILE_N = 1280
MM_TILE_K = 2048
ROW_TILE = 256
ATT_TILE = 512
ATT_HEADS_PER_STEP = 2
N_CHIPS = 4
COLLECTIVE_SIBLING = 8
COLLECTIVE_CHIPS = 9

_DIMS = {
    "nn": (((1,), (0,)), ((), ())),
    "nt": (((1,), (1,)), ((), ())),
    "tn": (((0,), (0,)), ((), ())),
}


def _cp(sem=None, vmem=VMEM_LIMIT):
    return pltpu.CompilerParams(dimension_semantics=sem, vmem_limit_bytes=vmem)


def _pick(dim, pref):
    t = min(dim, pref)
    while dim % t:
        t -= LANES
    return t


def _whole(shape):
    nd = len(shape)
    return pl.BlockSpec(tuple(shape), lambda *_: (0,) * nd)


def _sigmoid(z):
    return 1.0 / (1.0 + jnp.exp(-z))


def _log_sigmoid(z):
    return jnp.minimum(z, 0.0) - jnp.log(1.0 + jnp.exp(-jnp.abs(z)))


def _expm1(z):
    series = z * (1.0 + z * (0.5 + z * (1.0 / 6.0 + z * (1.0 / 24.0))))
    return jnp.where(jnp.abs(z) < 0.05, series, jnp.exp(z) - 1.0)


_GELU_C = math.sqrt(2.0 / math.pi)


def _gelu_and_grad(g):
    inner = _GELU_C * (g + 0.044715 * g * g * g)
    t = jnp.tanh(inner)
    val = 0.5 * g * (1.0 + t)
    grad = 0.5 * (1.0 + t) + 0.5 * g * (1.0 - t * t) * _GELU_C * (1.0 + 3.0 * 0.044715 * g * g)
    return val, grad


def _shift_down(cur, prev8, k, rows):
    n = cur.shape[0]
    rolled = pltpu.roll(cur, k, 0)
    head = jnp.tile(pltpu.roll(prev8, k, 0), (n // SUBLANES, 1))
    return jnp.where(rows < k, head, rolled)


def _shift_up(cur, next8, k, rows):
    n = cur.shape[0]
    rolled = pltpu.roll(cur, n - k, 0)
    tail = jnp.tile(pltpu.roll(next8, SUBLANES - k, 0), (n // SUBLANES, 1))
    return jnp.where(rows >= n - k, tail, rolled)


def _colsum(v):
    return jnp.sum(v, axis=0, keepdims=True)


def _matmul(name, mode, grid, operands, in_specs, out_shapes, out_specs, acc_shape, epilogue=None):
    nk = grid[2]
    n_in = len(operands)
    dims = _DIMS[mode]

    def finish(acc, extra, outs):
        res = epilogue(acc, *[e[...] for e in extra]) if epilogue is not None else (acc,)
        for o_ref, o in zip(outs, res):
            o_ref[...] = o.astype(o_ref.dtype)

    def product(a_ref, b_ref):
        return lax.dot_general(a_ref[...].astype(BF16), b_ref[...].astype(BF16), dims, preferred_element_type=F32)

    def body_single(*refs):
        finish(product(refs[0], refs[1]), refs[2:n_in], refs[n_in:])

    def body(*refs):
        extra = refs[2:n_in]
        outs = refs[n_in:-1]
        acc_ref = refs[-1]
        k = pl.program_id(2)

        @pl.when(k == 0)
        def _():
            acc_ref[...] = product(refs[0], refs[1])

        @pl.when(k > 0)
        def _():
            acc_ref[...] += product(refs[0], refs[1])

        @pl.when(k == nk - 1)
        def _():
            finish(acc_ref[...], extra, outs)

    return pl.pallas_call(
        body_single if nk == 1 else body, name=name, grid=grid, in_specs=in_specs, out_specs=out_specs,
        out_shape=out_shapes, scratch_shapes=[] if nk == 1 else [pltpu.VMEM(acc_shape, F32)],
        compiler_params=_cp(("parallel", "parallel", "arbitrary")),
    )(*operands)


def _pick_m(m, tk, tn, a_dtype, b_dtype, out_dtypes, extra_dtypes=()):
    size = lambda dt: jnp.dtype(dt).itemsize
    per_row = 2 * tk * size(a_dtype) + tn * (2 * sum(size(dt) for dt in tuple(out_dtypes) + tuple(extra_dtypes)) + 4)
    fixed = 2 * tk * tn * size(b_dtype)
    tm = _pick(m, MM_TILE)
    while tm > LANES and tm * per_row + fixed > MM_VMEM_BUDGET:
        tm = _pick(m, tm // 2)
    return tm


def _mm_fwd_col(name, a, wfull, out_dtypes=(F32,), epilogue=None):
    s, kdim = a.shape
    _, _, cs = wfull.shape
    tk, tn = _pick(kdim, MM_TILE_K), _pick(cs, MM_TILE_N)
    tm = _pick_m(s, tk, tn, a.dtype, wfull.dtype, out_dtypes)
    nbj = cs // tn
    grid = (s // tm, N_CHIPS * nbj, kdim // tk)
    out_shapes = [jax.ShapeDtypeStruct((s, N_CHIPS * cs), dt) for dt in out_dtypes]
    out_specs = [pl.BlockSpec((tm, tn), lambda i, n, k: (i, n)) for _ in out_dtypes]
    return _matmul(
        name, "nn", grid, [a, wfull],
        [pl.BlockSpec((tm, tk), lambda i, n, k: (i, k)),
         pl.BlockSpec((None, tk, tn), lambda i, n, k: (n // nbj, k, n % nbj))],
        out_shapes, out_specs, (tm, tn), epilogue)


def _mm_fwd_row(name, a, w2d, out_dtype=F32):
    s, kdim = a.shape
    _, n_out = w2d.shape
    tk, tn = _pick(kdim, MM_TILE_K), _pick(n_out, MM_TILE)
    tm = _pick_m(s, tk, tn, a.dtype, w2d.dtype, (out_dtype,))
    grid = (s // tm, n_out // tn, kdim // tk)
    return _matmul(
        name, "nn", grid, [a, w2d],
        [pl.BlockSpec((tm, tk), lambda i, n, k: (i, k)),
         pl.BlockSpec((tk, tn), lambda i, n, k: (k, n))],
        [jax.ShapeDtypeStruct((s, n_out), out_dtype)],
        [pl.BlockSpec((tm, tn), lambda i, n, k: (i, n))], (tm, tn))[0]


def _mm_bwd_col(name, dy, wfull, out_dtype=F32):
    s, _ = dy.shape
    _, kdim, cs = wfull.shape
    tn, tk = _pick(kdim, MM_TILE), _pick(cs, MM_TILE_K)
    tm = _pick_m(s, tk, tn, dy.dtype, wfull.dtype, (out_dtype,))
    nbj = cs // tk
    grid = (s // tm, kdim // tn, N_CHIPS * nbj)
    return _matmul(
        name, "nt", grid, [dy, wfull],
        [pl.BlockSpec((tm, tk), lambda i, n, k: (i, k)),
         pl.BlockSpec((None, tn, tk), lambda i, n, k: (k // nbj, n, k % nbj))],
        [jax.ShapeDtypeStruct((s, kdim), out_dtype)],
        [pl.BlockSpec((tm, tn), lambda i, n, k: (i, n))], (tm, tn))[0]


def _mm_bwd_row(name, dy, w2d, out_dtypes=(F32,), extra=None, epilogue=None):
    s, n_in = dy.shape
    kdim, _ = w2d.shape
    tn, tk = _pick(kdim, MM_TILE), _pick(n_in, MM_TILE_K)
    tm = _pick_m(s, tk, tn, dy.dtype, w2d.dtype, out_dtypes, () if extra is None else (extra.dtype,))
    grid = (s // tm, kdim // tn, n_in // tk)
    operands = [dy, w2d]
    in_specs = [pl.BlockSpec((tm, tk), lambda i, n, k: (i, k)),
                pl.BlockSpec((tn, tk), lambda i, n, k: (n, k))]
    if extra is not None:
        operands.append(extra)
        in_specs.append(pl.BlockSpec((tm, tn), lambda i, n, k: (i, n)))
    return _matmul(
        name, "nt", grid, operands, in_specs,
        [jax.ShapeDtypeStruct((s, kdim), dt) for dt in out_dtypes],
        [pl.BlockSpec((tm, tn), lambda i, n, k: (i, n)) for _ in out_dtypes], (tm, tn), epilogue)


def _mm_wgrad_col(name, a, dy, cs):
    s, kdim = a.shape
    tn, ts = _pick(cs, MM_TILE_N), _pick(s, MM_TILE_K)
    tm = _pick_m(kdim, ts, tn, a.dtype, dy.dtype, (BF16,))
    nbj = cs // tn
    grid = (kdim // tm, N_CHIPS * nbj, s // ts)
    return _matmul(
        name, "tn", grid, [a, dy],
        [pl.BlockSpec((ts, tm), lambda i, n, k: (k, i)),
         pl.BlockSpec((ts, tn), lambda i, n, k: (k, n))],
        [jax.ShapeDtypeStruct((N_CHIPS, kdim, cs), BF16)],
        [pl.BlockSpec((None, tm, tn), lambda i, n, k: (n // nbj, i, n % nbj))], (tm, tn))[0]


def _mm_wgrad_row(name, a, dy):
    s, kdim = a.shape
    _, n_out = dy.shape
    tn, ts = _pick(n_out, MM_TILE), _pick(s, MM_TILE_K)
    tm = _pick_m(kdim, ts, tn, a.dtype, dy.dtype, (BF16,))
    grid = (kdim // tm, n_out // tn, s // ts)
    return _matmul(
        name, "tn", grid, [a, dy],
        [pl.BlockSpec((ts, tm), lambda i, n, k: (k, i)),
         pl.BlockSpec((ts, tn), lambda i, n, k: (k, n))],
        [jax.ShapeDtypeStruct((kdim, n_out), BF16)],
        [pl.BlockSpec((tm, tn), lambda i, n, k: (i, n))], (tm, tn))[0]


def _mm_wgrad_diag(name, a, dy):
    s, width = a.shape
    nb = width // LANES
    ts = _pick(s, MM_TILE)
    grid = (nb, 1, s // ts)
    return _matmul(
        name, "tn", grid, [a, dy],
        [pl.BlockSpec((ts, LANES), lambda i, n, k: (k, i)),
         pl.BlockSpec((ts, LANES), lambda i, n, k: (k, i))],
        [jax.ShapeDtypeStruct((nb, LANES, LANES), F32)],
        [pl.BlockSpec((None, LANES, LANES), lambda i, n, k: (i, 0, 0))], (LANES, LANES))[0]


def _rowspec(tr, d):
    return pl.BlockSpec((tr, d), lambda i: (i, 0))


def _vecspec(d):
    return pl.BlockSpec((1, d), lambda i: (0, 0))


def _rms(x, g):
    return x * lax.rsqrt(jnp.mean(x * x, axis=-1, keepdims=True) + NORM_EPS) * g


def _cast_into_slot(name, w, layer, chip):
    _, r, c = w.shape
    tr = _pick(r, ROW_TILE)

    def body(chip_ref, w_ref, o_ref):
        o_ref[...] = w_ref[...].astype(BF16)

    grid_spec = pltpu.PrefetchScalarGridSpec(
        num_scalar_prefetch=1, grid=(r // tr,),
        in_specs=[pl.BlockSpec((None, tr, c), lambda i, chip_ref: (layer, i, 0))],
        out_specs=pl.BlockSpec((None, tr, c), lambda i, chip_ref: (chip_ref[0], i, 0)))
    return pl.pallas_call(
        body, name=name, grid_spec=grid_spec, out_shape=jax.ShapeDtypeStruct((N_CHIPS, r, c), BF16),
        compiler_params=_cp(("parallel",)))(jnp.reshape(chip, (1,)).astype(jnp.int32), w)


def _rms_fwd(name, x, g):
    s, d = x.shape
    tr = _pick(s, ROW_TILE)

    def body(x_ref, g_ref, h_ref):
        h_ref[...] = _rms(x_ref[...], g_ref[...]).astype(BF16)

    return pl.pallas_call(
        body, name=name, grid=(s // tr,), in_specs=[_rowspec(tr, d), _vecspec(d)],
        out_specs=_rowspec(tr, d), out_shape=jax.ShapeDtypeStruct((s, d), BF16),
        compiler_params=_cp(("parallel",)))(x, g)


def _rms_post(name, y, g_post, res, g_next=None):
    s, d = y.shape
    tr = _pick(s, ROW_TILE)
    with_next = g_next is not None

    def body(*refs):
        if with_next:
            y_ref, gp_ref, r_ref, gn_ref, x_ref, h_ref = refs
        else:
            y_ref, gp_ref, r_ref, x_ref = refs
        xn = r_ref[...] + _rms(y_ref[...], gp_ref[...])
        x_ref[...] = xn
        if with_next:
            h_ref[...] = _rms(xn, gn_ref[...]).astype(BF16)

    operands = [y, g_post, res] + ([g_next] if with_next else [])
    in_specs = [_rowspec(tr, d), _vecspec(d), _rowspec(tr, d)] + ([_vecspec(d)] if with_next else [])
    out_shape = [jax.ShapeDtypeStruct((s, d), F32)] + ([jax.ShapeDtypeStruct((s, d), BF16)] if with_next else [])
    out_specs = [_rowspec(tr, d)] + ([_rowspec(tr, d)] if with_next else [])
    return pl.pallas_call(
        body, name=name, grid=(s // tr,), in_specs=in_specs, out_specs=out_specs, out_shape=out_shape,
        compiler_params=_cp(("parallel",)))(*operands)


def _rms_bwd(name, x, g, dy, res=None, out_dtype=F32):
    s, d = x.shape
    tr = _pick(s, ROW_TILE)
    nsteps = s // tr
    with_res = res is not None

    def body(*refs):
        if with_res:
            x_ref, g_ref, dy_ref, r_ref, dx_ref, dg_ref, acc_ref = refs
        else:
            x_ref, g_ref, dy_ref, dx_ref, dg_ref, acc_ref = refs
        i = pl.program_id(0)

        @pl.when(i == 0)
        def _():
            acc_ref[...] = jnp.zeros_like(acc_ref)

        xv = x_ref[...]
        dyv = dy_ref[...].astype(F32)
        r = lax.rsqrt(jnp.mean(xv * xv, axis=-1, keepdims=True) + NORM_EPS)
        xhat = xv * r
        gy = dyv * g_ref[...]
        dx = r * (gy - xhat * jnp.mean(gy * xhat, axis=-1, keepdims=True))
        if with_res:
            dx = dx + r_ref[...]
        dx_ref[...] = dx.astype(dx_ref.dtype)
        acc_ref[...] += jnp.sum((dyv * xhat).reshape(tr // SUBLANES, SUBLANES, d), axis=0)

        @pl.when(i == nsteps - 1)
        def _():
            dg_ref[...] = jnp.broadcast_to(_colsum(acc_ref[...]), (SUBLANES, d))

    operands = [x, g, dy] + ([res] if with_res else [])
    in_specs = [_rowspec(tr, d), _vecspec(d), _rowspec(tr, d)] + ([_rowspec(tr, d)] if with_res else [])
    dx, dg = pl.pallas_call(
        body, name=name, grid=(nsteps,), in_specs=in_specs,
        out_specs=[_rowspec(tr, d), pl.BlockSpec((SUBLANES, d), lambda i: (0, 0))],
        out_shape=[jax.ShapeDtypeStruct((s, d), out_dtype), jax.ShapeDtypeStruct((SUBLANES, d), F32)],
        scratch_shapes=[pltpu.VMEM((SUBLANES, d), F32)],
        compiler_params=_cp(("arbitrary",)))(*operands)
    return dx, dg[0:1]


def _rms_bwd_pair(name, x, g, dy, res, y2, g2):
    s, d = x.shape
    tr = _pick(s, ROW_TILE)
    nsteps = s // tr

    def through(xv, gv, dyv):
        r = lax.rsqrt(jnp.mean(xv * xv, axis=-1, keepdims=True) + NORM_EPS)
        xhat = xv * r
        gy = dyv * gv
        dx = r * (gy - xhat * jnp.mean(gy * xhat, axis=-1, keepdims=True))
        return dx, jnp.sum((dyv * xhat).reshape(tr // SUBLANES, SUBLANES, d), axis=0)

    def body(x_ref, g_ref, dy_ref, r_ref, y2_ref, g2_ref, dx_ref, d2_ref, dg_ref, dg2_ref, acc_ref, acc2_ref):
        i = pl.program_id(0)

        @pl.when(i == 0)
        def _():
            acc_ref[...] = jnp.zeros_like(acc_ref)
            acc2_ref[...] = jnp.zeros_like(acc2_ref)

        dx, part = through(x_ref[...], g_ref[...], dy_ref[...].astype(F32))
        dx = dx + r_ref[...]
        dx_ref[...] = dx
        acc_ref[...] += part
        d2, part2 = through(y2_ref[...], g2_ref[...], dx)
        d2_ref[...] = d2.astype(d2_ref.dtype)
        acc2_ref[...] += part2

        @pl.when(i == nsteps - 1)
        def _():
            dg_ref[...] = jnp.broadcast_to(_colsum(acc_ref[...]), (SUBLANES, d))
            dg2_ref[...] = jnp.broadcast_to(_colsum(acc2_ref[...]), (SUBLANES, d))

    row, vec = _rowspec(tr, d), _vecspec(d)
    gspec = pl.BlockSpec((SUBLANES, d), lambda i: (0, 0))
    dx, d2, dg, dg2 = pl.pallas_call(
        body, name=name, grid=(nsteps,), in_specs=[row, vec, row, row, row, vec],
        out_specs=[row, row, gspec, gspec],
        out_shape=[jax.ShapeDtypeStruct((s, d), F32), jax.ShapeDtypeStruct((s, d), BF16),
                   jax.ShapeDtypeStruct((SUBLANES, d), F32), jax.ShapeDtypeStruct((SUBLANES, d), F32)],
        scratch_shapes=[pltpu.VMEM((SUBLANES, d), F32), pltpu.VMEM((SUBLANES, d), F32)],
        compiler_params=_cp(("arbitrary",)))(x, g, dy, res, y2, g2)
    return dx, dg[0:1], d2, dg2[0:1]


def _last_norm_and_loss(name, y, g, res, target):
    s, d = y.shape
    tr = _pick(s, ROW_TILE)
    nsteps = s // tr

    def body(y_ref, g_ref, r_ref, t_ref, dx_ref, dy_ref, dg_ref, l_ref, acc_ref, lacc_ref):
        i = pl.program_id(0)

        @pl.when(i == 0)
        def _():
            acc_ref[...] = jnp.zeros_like(acc_ref)
            lacc_ref[...] = jnp.zeros_like(lacc_ref)

        yv = y_ref[...]
        gv = g_ref[...]
        r = lax.rsqrt(jnp.mean(yv * yv, axis=-1, keepdims=True) + NORM_EPS)
        yhat = yv * r
        err = r_ref[...] + yhat * gv - t_ref[...]
        dx = err * (1.0 / d)
        dx_ref[...] = dx
        lacc_ref[...] += jnp.sum((err * err).reshape(tr // SUBLANES, SUBLANES, d), axis=0)
        gy = dx * gv
        dy_ref[...] = (r * (gy - yhat * jnp.mean(gy * yhat, axis=-1, keepdims=True))).astype(dy_ref.dtype)
        acc_ref[...] += jnp.sum((dx * yhat).reshape(tr // SUBLANES, SUBLANES, d), axis=0)

        @pl.when(i == nsteps - 1)
        def _():
            dg_ref[...] = jnp.broadcast_to(_colsum(acc_ref[...]), (SUBLANES, d))
            l_ref[...] = jnp.full((SUBLANES, LANES), (0.5 / d) * jnp.sum(lacc_ref[...]), F32)

    dx, dy, dg, l = pl.pallas_call(
        body, name=name, grid=(nsteps,),
        in_specs=[_rowspec(tr, d), _vecspec(d), _rowspec(tr, d), _rowspec(tr, d)],
        out_specs=[_rowspec(tr, d), _rowspec(tr, d), pl.BlockSpec((SUBLANES, d), lambda i: (0, 0)),
                   pl.BlockSpec((SUBLANES, LANES), lambda i: (0, 0))],
        out_shape=[jax.ShapeDtypeStruct((s, d), F32), jax.ShapeDtypeStruct((s, d), BF16),
                   jax.ShapeDtypeStruct((SUBLANES, d), F32), jax.ShapeDtypeStruct((SUBLANES, LANES), F32)],
        scratch_shapes=[pltpu.VMEM((SUBLANES, d), F32), pltpu.VMEM((SUBLANES, d), F32)],
        compiler_params=_cp(("arbitrary",)))(y, g, res, target)
    return dx, dy, dg[0:1], l[0, 0]


def _gates(xr, wa, ba, wx, bx, lam):
    xb = xr.astype(BF16)
    r = _sigmoid(jnp.dot(xb, wa, preferred_element_type=F32) + ba)
    i = _sigmoid(jnp.dot(xb, wx, preferred_element_type=F32) + bx)
    log_a = LRU_C * r * _log_sigmoid(lam)
    a = jnp.exp(log_a)
    m = jnp.sqrt(-_expm1(2.0 * log_a))
    return r, i, a, m


def _mixer_fwd(proj, conv_a, conv_b, bias, wa_blk, ba, wx_blk, bx, lam):
    s, w5 = proj.shape
    w = w5 // 5
    nch = w // LANES
    ts = _pick(s, ROW_TILE)
    nt = s // ts

    def body(p_ref, pp_ref, ca_ref, cb_ref, bias_ref, wa_ref, ba_ref, wx_ref, bx_ref, lam_ref,
             y_ref, h_ref, a_scr, b_scr, hc_scr):
        t = pl.program_id(0)
        first = t == 0
        rows = lax.broadcasted_iota(jnp.int32, (ts, LANES), 0)

        @pl.when(first)
        def _():
            hc_scr[...] = jnp.zeros_like(hc_scr)

        def cur(comp, c):
            return p_ref[:, comp * w + c * LANES:comp * w + (c + 1) * LANES]

        def prev(comp, c):
            v = pp_ref[:, comp * w + c * LANES:comp * w + (c + 1) * LANES]
            return jnp.where(first, 0.0, v)

        for c in range(nch):
            sl = slice(c * LANES, (c + 1) * LANES)
            cx = cur(1, c) * cur(2, c)
            cxp = prev(1, c) * prev(2, c)
            wa3 = ca_ref[:, sl]
            conv = (wa3[2:3] * cx + wa3[1:2] * _shift_down(cx, cxp, 1, rows)
                    + wa3[0:1] * _shift_down(cx, cxp, 2, rows))
            y_ref[:, sl] = (cur(0, c) * conv).astype(BF16)

        for c in range(nch):
            sl = slice(c * LANES, (c + 1) * LANES)
            xb, xbp = cur(4, c), prev(4, c)
            wb4 = cb_ref[:, sl]
            xr = (wb4[3:4] * xb + wb4[2:3] * _shift_down(xb, xbp, 1, rows)
                  + wb4[1:2] * _shift_down(xb, xbp, 2, rows)
                  + wb4[0:1] * _shift_down(xb, xbp, 3, rows) + bias_ref[:, sl])
            _, i, a, m = _gates(xr, wa_ref[c], ba_ref[:, sl], wx_ref[c], bx_ref[:, sl], lam_ref[:, sl])
            a_scr[:, sl] = a
            b_scr[:, sl] = m * i * xr

        def step(r, h):
            h = a_scr[pl.ds(r, 1), :] * h + b_scr[pl.ds(r, 1), :]
            h_ref[pl.ds(r, 1), :] = h
            return h

        hc_scr[0:1, :] = lax.fori_loop(0, ts, step, hc_scr[0:1, :], unroll=8)

        for c in range(nch):
            sl = slice(c * LANES, (c + 1) * LANES)
            gel, _ = _gelu_and_grad(cur(3, c))
            y_ref[:, w + c * LANES:w + (c + 1) * LANES] = (h_ref[:, sl] * gel).astype(BF16)

    vec = lambda n: _whole((n, w))
    return pl.pallas_call(
        body, name="mixer_fwd", grid=(nt,),
        in_specs=[pl.BlockSpec((ts, w5), lambda t: (t, 0)),
                  pl.BlockSpec((SUBLANES, w5), lambda t: (jnp.maximum(t * (ts // SUBLANES) - 1, 0), 0)),
                  vec(3), vec(4), vec(1), _whole(wa_blk.shape), vec(1), _whole(wx_blk.shape), vec(1), vec(1)],
        out_specs=[pl.BlockSpec((ts, 2 * w), lambda t: (t, 0)), pl.BlockSpec((ts, w), lambda t: (t, 0))],
        out_shape=[jax.ShapeDtypeStruct((s, 2 * w), BF16), jax.ShapeDtypeStruct((s, w), F32)],
        scratch_shapes=[pltpu.VMEM((ts, w), F32), pltpu.VMEM((ts, w), F32), pltpu.VMEM((SUBLANES, w), F32)],
        compiler_params=_cp(("arbitrary",)),
    )(proj, proj, conv_a, conv_b, bias, wa_blk, ba, wx_blk, bx, lam)


_SG_CONV_A, _SG_CONV_B, _SG_BIAS, _SG_BA, _SG_BX, _SG_LAM, _SG_ROWS = 0, 3, 7, 8, 9, 10, 16


def _mixer_bwd(proj, hseq, dy, conv_a, conv_b, bias, wa_blk, ba, wx_blk, bx, lam):
    s, w5 = proj.shape
    w = w5 // 5
    nch = w // LANES
    ts = _pick(s, ROW_TILE)
    nt = s // ts
    tpb = ts // SUBLANES

    def body(p_ref, pp_ref, h_ref, hp_ref, dy_ref, ca_ref, cb_ref, bias_ref, wa_ref, ba_ref, wx_ref, bx_ref,
             lam_ref, dp_ref, xr_ref, dpa_ref, dpx_ref, sg_ref,
             a_scr, g_scr, l_scr, x_scr, r_scr, i_scr, m_scr, cl_scr, cdc_scr, cdx_scr):
        pid = pl.program_id(0)
        last = pid == 0
        first = pid == nt - 1
        rows = lax.broadcasted_iota(jnp.int32, (ts, LANES), 0)

        @pl.when(last)
        def _():
            sg_ref[...] = jnp.zeros_like(sg_ref)
            cl_scr[...] = jnp.zeros_like(cl_scr)
            cdc_scr[...] = jnp.zeros_like(cdc_scr)
            cdx_scr[...] = jnp.zeros_like(cdx_scr)

        def cur(comp, c):
            return p_ref[:, comp * w + c * LANES:comp * w + (c + 1) * LANES]

        def prev(comp, c):
            v = pp_ref[:, comp * w + c * LANES:comp * w + (c + 1) * LANES]
            return jnp.where(first, 0.0, v)

        def put(comp, c, v):
            dp_ref[:, comp * w + c * LANES:comp * w + (c + 1) * LANES] = v.astype(dp_ref.dtype)

        def acc(row, sl, v):
            sg_ref[row:row + 1, sl] += _colsum(v)

        for c in range(nch):
            sl = slice(c * LANES, (c + 1) * LANES)
            bg, cg, ax = cur(0, c), cur(1, c), cur(2, c)
            cx = cg * ax
            cxp = prev(1, c) * prev(2, c)
            cx1 = _shift_down(cx, cxp, 1, rows)
            cx2 = _shift_down(cx, cxp, 2, rows)
            wa3 = ca_ref[:, sl]
            conv = wa3[2:3] * cx + wa3[1:2] * cx1 + wa3[0:1] * cx2
            dya = dy_ref[:, sl]
            put(0, c, dya * conv)
            dconv = dya * bg
            nxt = cdc_scr[:, sl]
            dcx = (wa3[2:3] * dconv + wa3[1:2] * _shift_up(dconv, nxt, 1, rows)
                   + wa3[0:1] * _shift_up(dconv, nxt, 2, rows))
            cdc_scr[:, sl] = dconv[0:SUBLANES]
            put(1, c, dcx * ax)
            put(2, c, dcx * cg)
            acc(_SG_CONV_A + 2, sl, dconv * cx)
            acc(_SG_CONV_A + 1, sl, dconv * cx1)
            acc(_SG_CONV_A + 0, sl, dconv * cx2)

        for c in range(nch):
            sl = slice(c * LANES, (c + 1) * LANES)
            xb, xbp = cur(4, c), prev(4, c)
            wb4 = cb_ref[:, sl]
            xr = (wb4[3:4] * xb + wb4[2:3] * _shift_down(xb, xbp, 1, rows)
                  + wb4[1:2] * _shift_down(xb, xbp, 2, rows)
                  + wb4[0:1] * _shift_down(xb, xbp, 3, rows) + bias_ref[:, sl])
            r, i, a, m = _gates(xr, wa_ref[c], ba_ref[:, sl], wx_ref[c], bx_ref[:, sl], lam_ref[:, sl])
            gel, dgel = _gelu_and_grad(cur(3, c))
            dyb = dy_ref[:, w + c * LANES:w + (c + 1) * LANES]
            put(3, c, dyb * h_ref[:, sl] * dgel)
            g_scr[:, sl] = dyb * gel
            a_scr[:, sl] = a
            x_scr[:, sl] = xr
            r_scr[:, sl] = r
            i_scr[:, sl] = i
            m_scr[:, sl] = m

        def step(j, carry):
            r = ts - 1 - j
            lam_t = g_scr[pl.ds(r, 1), :] + carry
            l_scr[pl.ds(r, 1), :] = lam_t
            return a_scr[pl.ds(r, 1), :] * lam_t

        cl_scr[0:1, :] = lax.fori_loop(0, ts, step, cl_scr[0:1, :], unroll=8)

        for c in range(nch):
            sl = slice(c * LANES, (c + 1) * LANES)
            lam_t = l_scr[:, sl]
            hprev = _shift_down(h_ref[:, sl], jnp.where(first, 0.0, hp_ref[:, sl]), 1, rows)
            xr, r, i, m, a = x_scr[:, sl], r_scr[:, sl], i_scr[:, sl], m_scr[:, sl], a_scr[:, sl]
            da = lam_t * hprev
            dm = lam_t * i * xr
            di = lam_t * m * xr
            dxr = lam_t * m * i
            dlog_a = da * a - dm * a * a / m
            lam_p = lam_ref[:, sl]
            dr = dlog_a * (LRU_C * _log_sigmoid(lam_p))
            acc(_SG_LAM, sl, dlog_a * r * (LRU_C * _sigmoid(-lam_p)))
            dpa = dr * r * (1.0 - r)
            dpx = di * i * (1.0 - i)
            dpa_b, dpx_b = dpa.astype(BF16), dpx.astype(BF16)
            dxr = (dxr + lax.dot_general(dpa_b, wa_ref[c], _DIMS["nt"], preferred_element_type=F32)
                   + lax.dot_general(dpx_b, wx_ref[c], _DIMS["nt"], preferred_element_type=F32))
            xr_ref[:, sl] = xr.astype(BF16)
            dpa_ref[:, sl] = dpa_b
            dpx_ref[:, sl] = dpx_b
            acc(_SG_BA, sl, dpa)
            acc(_SG_BX, sl, dpx)
            acc(_SG_BIAS, sl, dxr)
            nxt = cdx_scr[:, sl]
            wb4 = cb_ref[:, sl]
            put(4, c, wb4[3:4] * dxr + wb4[2:3] * _shift_up(dxr, nxt, 1, rows)
                + wb4[1:2] * _shift_up(dxr, nxt, 2, rows) + wb4[0:1] * _shift_up(dxr, nxt, 3, rows))
            cdx_scr[:, sl] = dxr[0:SUBLANES]
            xb, xbp = cur(4, c), prev(4, c)
            acc(_SG_CONV_B + 3, sl, dxr * xb)
            acc(_SG_CONV_B + 2, sl, dxr * _shift_down(xb, xbp, 1, rows))
            acc(_SG_CONV_B + 1, sl, dxr * _shift_down(xb, xbp, 2, rows))
            acc(_SG_CONV_B + 0, sl, dxr * _shift_down(xb, xbp, 3, rows))

    blk = lambda width: pl.BlockSpec((ts, width), lambda p: (nt - 1 - p, 0))
    pre = lambda width: pl.BlockSpec(
        (SUBLANES, width), lambda p: (jnp.maximum((nt - 1 - p) * tpb - 1, 0), 0))
    vec = lambda n: _whole((n, w))
    big = lambda: pltpu.VMEM((ts, w), F32)
    small = lambda: pltpu.VMEM((SUBLANES, w), F32)
    return pl.pallas_call(
        body, name="mixer_bwd", grid=(nt,),
        in_specs=[blk(w5), pre(w5), blk(w), pre(w), blk(2 * w),
                  vec(3), vec(4), vec(1), _whole(wa_blk.shape), vec(1), _whole(wx_blk.shape), vec(1), vec(1)],
        out_specs=[blk(w5), blk(w), blk(w), blk(w), _whole((_SG_ROWS, w))],
        out_shape=[jax.ShapeDtypeStruct((s, w5), BF16), jax.ShapeDtypeStruct((s, w), BF16),
                   jax.ShapeDtypeStruct((s, w), BF16), jax.ShapeDtypeStruct((s, w), BF16),
                   jax.ShapeDtypeStruct((_SG_ROWS, w), F32)],
        scratch_shapes=[big(), big(), big(), big(), big(), big(), big(), small(), small(), small()],
        compiler_params=_cp(("arbitrary",)),
    )(proj, proj, hseq, hseq, dy, conv_a, conv_b, bias, wa_blk, ba, wx_blk, bx, lam)


def _split_dot(v, tri2):
    hi = v.astype(BF16)
    lo = (v - hi.astype(F32)).astype(BF16)
    return jnp.dot(jnp.concatenate([hi, lo], axis=1), tri2, preferred_element_type=F32)


def _tri(cmp):
    r = lax.broadcasted_iota(jnp.int32, (LANES, LANES), 0)
    c = lax.broadcasted_iota(jnp.int32, (LANES, LANES), 1)
    return cmp(r, c).astype(BF16)


def _lane_blocks(v):
    return [v[:, b * LANES:(b + 1) * LANES] for b in range(v.shape[1] // LANES)]


def _last_lane(v):
    return jnp.broadcast_to(v[:, LANES - 1:LANES], v.shape)


def _scores(q, kb, scale):
    return lax.dot_general(q, kb, _DIMS["nt"], preferred_element_type=F32) * scale


def _log_gates(z, diagonal):
    ls = jnp.minimum(z, 0.0) - jnp.log(1.0 + jnp.exp(-jnp.abs(z)))
    ln = ls - z
    valid = None
    if diagonal:
        valid = (lax.broadcasted_iota(jnp.int32, z.shape, 1) < lax.broadcasted_iota(jnp.int32, z.shape, 0))
        ln = jnp.where(valid, ln, 0.0)
    return ls, ln, valid


def _attn_fwd(qkv, heads):
    s = qkv.shape[0]
    dh = LANES
    tq = _pick(s, ATT_TILE)
    nq = s // tq
    nb = tq // LANES
    scale = 1.0 / math.sqrt(dh)

    hp = ATT_HEADS_PER_STEP
    groups = heads // hp
    wid = hp * dh

    def body(q_ref, k_ref, v_ref, o_ref, tot_ref, acc_scr, car_scr):
        qi = pl.program_id(1)
        acc_scr[...] = jnp.zeros_like(acc_scr)
        car_scr[...] = jnp.zeros_like(car_scr)
        tri = _tri(lambda r, c: r > c)
        tri = jnp.concatenate([tri, tri], axis=0)

        def tile(kt, diagonal):
            k0 = pl.multiple_of(kt * tq, tq)
            heads_cols = [slice(hh * dh, (hh + 1) * dh) for hh in range(hp)]
            zs = [_scores(q_ref[:, cols], k_ref[pl.ds(k0, tq), cols], scale) for cols in heads_cols]
            gates = [_log_gates(z, diagonal) for z in zs]
            sfxs = [_split_dot(jnp.concatenate(_lane_blocks(ln), axis=0), tri) for _, ln, _ in gates]
            for cols, (ls, ln, valid), sfx in zip(heads_cols, gates, sfxs):
                blocks = _lane_blocks(ln)
                car = car_scr[:, cols]
                parts = [None] * nb
                for b in reversed(range(nb)):
                    sb = sfx[b * tq:(b + 1) * tq]
                    parts[b] = sb + car
                    car = car + (sb[:, 0:1] + blocks[b][:, 0:1])
                car_scr[:, cols] = car
                wgt = jnp.exp(ls + jnp.concatenate(parts, axis=1))
                if diagonal:
                    wgt = jnp.where(valid, wgt, 0.0)
                acc_scr[:, cols] += jnp.dot(
                    wgt.astype(BF16), v_ref[pl.ds(k0, tq), cols], preferred_element_type=F32)

        tile(qi, True)

        def step(j, carry):
            tile(qi - 1 - j, False)
            return carry

        lax.fori_loop(0, qi, step, 0)
        o_ref[...] = acc_scr[...].astype(BF16)
        tot_ref[...] = car_scr[...]

    return pl.pallas_call(
        body, name="attn_fwd", grid=(groups, nq),
        in_specs=[pl.BlockSpec((tq, wid), lambda h, i: (i, h)),
                  pl.BlockSpec((s, wid), lambda h, i: (0, groups + h)),
                  pl.BlockSpec((s, wid), lambda h, i: (0, 2 * groups + h))],
        out_specs=[pl.BlockSpec((tq, wid), lambda h, i: (i, h)), pl.BlockSpec((tq, wid), lambda h, i: (i, h))],
        out_shape=[jax.ShapeDtypeStruct((s, heads * dh), BF16), jax.ShapeDtypeStruct((s, heads * dh), F32)],
        scratch_shapes=[pltpu.VMEM((tq, wid), F32), pltpu.VMEM((tq, wid), F32)],
        compiler_params=_cp(("parallel", "arbitrary")),
    )(qkv, qkv, qkv)


def _attn_bwd(qkv, tot, do, heads):
    s = qkv.shape[0]
    dh = LANES
    tq = _pick(s, ATT_TILE)
    nq = s // tq
    nb = tq // LANES
    scale = 1.0 / math.sqrt(dh)

    hp = ATT_HEADS_PER_STEP
    groups = heads // hp
    wid = hp * dh

    def body(q_ref, k_ref, v_ref, tot_ref, do_ref, dq_ref, dk_ref, dv_ref,
             dq_scr, dk_scr, dv_scr, cl_scr, cg_scr):
        qi = pl.program_id(1)

        @pl.when(qi == 0)
        def _():
            dk_scr[...] = jnp.zeros_like(dk_scr)
            dv_scr[...] = jnp.zeros_like(dv_scr)

        dq_scr[...] = jnp.zeros_like(dq_scr)
        cl_scr[...] = jnp.zeros_like(cl_scr)
        cg_scr[...] = jnp.zeros_like(cg_scr)
        tri_le = _tri(lambda r, c: r <= c)
        tri_le = jnp.concatenate([tri_le, tri_le], axis=0)
        tri_lt = _tri(lambda r, c: r < c)

        def tile(kt, diagonal):
            k0 = pl.multiple_of(kt * tq, tq)
            heads_cols = [slice(hh * dh, (hh + 1) * dh) for hh in range(hp)]
            keys = pl.ds(k0, tq)
            zs = [_scores(q_ref[:, cols], k_ref[keys, cols], scale) for cols in heads_cols]
            dws = [lax.dot_general(do_ref[:, cols], v_ref[keys, cols], _DIMS["nt"], preferred_element_type=F32)
                   for cols in heads_cols]
            gates = [_log_gates(z, diagonal) for z in zs]
            pins = [_split_dot(jnp.concatenate(_lane_blocks(ln), axis=0), tri_le) for _, ln, _ in gates]
            wgts, gs = [], []
            for cols, (ls, _, valid), pin, dw in zip(heads_cols, gates, pins, dws):
                total = tot_ref[:, cols]
                cl = cl_scr[:, cols]
                parts = []
                for b in range(nb):
                    pb = pin[b * tq:(b + 1) * tq] + cl
                    parts.append(total - pb)
                    cl = _last_lane(pb)
                cl_scr[:, cols] = cl
                wgt = jnp.exp(ls + jnp.concatenate(parts, axis=1))
                if diagonal:
                    wgt = jnp.where(valid, wgt, 0.0)
                wgts.append(wgt)
                gs.append(wgt * dw)
            pexs = [jnp.dot(jnp.concatenate(_lane_blocks(g), axis=0).astype(BF16), tri_lt,
                            preferred_element_type=F32) for g in gs]
            for cols, wgt in zip(heads_cols, wgts):
                dv_scr[keys, cols] += lax.dot_general(
                    wgt.astype(BF16), do_ref[:, cols], _DIMS["tn"], preferred_element_type=F32)
            for cols, (ls, _, valid), g, pex in zip(heads_cols, gates, gs, pexs):
                gblocks = _lane_blocks(g)
                cg = cg_scr[:, cols]
                parts = []
                for b in range(nb):
                    pb = pex[b * tq:(b + 1) * tq] + cg
                    parts.append(pb)
                    cg = _last_lane(pb + gblocks[b])
                cg_scr[:, cols] = cg
                dz = g - jnp.exp(ls) * (g + jnp.concatenate(parts, axis=1))
                if diagonal:
                    dz = jnp.where(valid, dz, 0.0)
                dz = dz.astype(BF16)
                dq_scr[:, cols] += jnp.dot(dz, k_ref[keys, cols], preferred_element_type=F32)
                dk_scr[keys, cols] += lax.dot_general(
                    dz, q_ref[:, cols], _DIMS["tn"], preferred_element_type=F32)

        def step(j, carry):
            tile(j, False)
            return carry

        lax.fori_loop(0, qi, step, 0)
        tile(qi, True)
        dq_ref[...] = (dq_scr[...] * scale).astype(BF16)

        @pl.when(qi == nq - 1)
        def _():
            dk_ref[...] = (dk_scr[...] * scale).astype(BF16)
            dv_ref[...] = dv_scr[...].astype(BF16)

    qblk = pl.BlockSpec((tq, wid), lambda h, i: (i, h))
    hblk = pl.BlockSpec((s, wid), lambda h, i: (0, h))
    out = jax.ShapeDtypeStruct((s, heads * dh), BF16)
    return pl.pallas_call(
        body, name="attn_bwd", grid=(groups, nq),
        in_specs=[qblk, pl.BlockSpec((s, wid), lambda h, i: (0, groups + h)),
                  pl.BlockSpec((s, wid), lambda h, i: (0, 2 * groups + h)), qblk, qblk],
        out_specs=[qblk, hblk, hblk], out_shape=[out, out, out],
        scratch_shapes=[pltpu.VMEM((tq, wid), F32), pltpu.VMEM((s, wid), F32), pltpu.VMEM((s, wid), F32),
                        pltpu.VMEM((tq, wid), F32), pltpu.VMEM((tq, wid), F32)],
        compiler_params=_cp(("parallel", "arbitrary")),
    )(qkv, qkv, qkv, tot, do)


def _place():
    x, y, c = lax.axis_index("x"), lax.axis_index("y"), lax.axis_index("c")
    chips = [(1 - x, y), (x, 1 - y), (1 - x, 1 - y)]
    return x, y, c, chips


def _remote(src, dst, send_sem, recv_sem, dev):
    return pltpu.make_async_remote_copy(
        src_ref=src, dst_ref=dst, send_sem=send_sem, recv_sem=recv_sem, device_id=dev, device_id_type=MESH)


def _handshake(peers):
    barrier = pltpu.get_barrier_semaphore()
    for dev in peers:
        pl.semaphore_signal(barrier, inc=1, device_id=dev, device_id_type=MESH)
    pl.semaphore_wait(barrier, len(peers))


def _sequencer_kernel(name, n_sems, collective_id):
    return functools.partial(
        pl.kernel, mesh=plsc.ScalarSubcoreMesh(axis_name="seq", num_cores=1), name=name,
        scratch_types=(pltpu.SemaphoreType.DMA,) * n_sems,
        compiler_params=pltpu.CompilerParams(collective_id=collective_id))


def _allgather_async(name, slot_buf, collective_id):
    buf = jax.new_ref(slot_buf, memory_space=pltpu.MemorySpace.HBM)
    hr = slot_buf.shape[1] // 2

    @_sequencer_kernel(name, 12, collective_id)
    def launch(*sems):
        send_sems, recv_sems, fsend_sems, frecv_sems = sems[0:3], sems[3:6], sems[6:9], sems[9:12]
        x, y, c, chips = _place()
        me = 2 * x + y
        sibling = (x, y, 1 - c)
        _handshake([(px, py, c) for px, py in chips] + [sibling])
        mine = buf.at[me, pl.ds(c * hr, hr)]
        firsts = []
        for k, (px, py) in enumerate(chips):
            cp = _remote(mine, mine, send_sems[k], recv_sems[k], (px, py, c))
            cp.start()
            firsts.append(cp)
        passed = []
        for k, (px, py) in enumerate(chips):
            slot = buf.at[2 * px + py, pl.ds(c * hr, hr)]
            _remote(slot, slot, send_sems[k], recv_sems[k], (px, py, c)).wait_recv()
            cp = _remote(slot, slot, fsend_sems[k], frecv_sems[k], sibling)
            cp.start()
            passed.append(cp)
        for k, (px, py) in enumerate(chips):
            slot = buf.at[2 * px + py, pl.ds((1 - c) * hr, hr)]
            _remote(slot, slot, fsend_sems[k], frecv_sems[k], sibling).wait_recv()
        for cp in firsts + passed:
            cp.wait_send()

    launch()
    return buf[...]


def _to_sibling_async(name, slab):
    src = jax.new_ref(slab, memory_space=pltpu.MemorySpace.HBM)
    hr = slab.shape[1] // 2
    got = jax.empty_ref(jax.ShapeDtypeStruct((N_CHIPS, hr, slab.shape[2]), slab.dtype),
                        memory_space=pltpu.MemorySpace.HBM)

    @_sequencer_kernel(name, 2, COLLECTIVE_SIBLING)
    def launch(send_sem, recv_sem):
        x, y, c, _ = _place()
        _handshake([(x, y, 1 - c)])
        _remote(src.at[:, pl.ds((1 - c) * hr, hr), :], got, send_sem, recv_sem, (x, y, 1 - c)).start()
        _remote(got, got, send_sem, recv_sem, (x, y, 1 - c)).wait()

    launch()
    return src[...], got[...]


def _to_chips_async(name, part):
    src = jax.new_ref(part, memory_space=pltpu.MemorySpace.HBM)
    got = jax.empty_ref(jax.ShapeDtypeStruct((3,) + part.shape[1:], part.dtype), memory_space=pltpu.MemorySpace.HBM)

    @_sequencer_kernel(name, 6, COLLECTIVE_CHIPS)
    def launch(*sems):
        send_sems, recv_sems = sems[0:3], sems[3:6]
        x, y, c, chips = _place()
        _handshake([(px, py, c) for px, py in chips])
        cps = []
        for k, (px, py) in enumerate(chips):
            cp = _remote(src.at[2 * px + py], got.at[k], send_sems[k], recv_sems[k], (px, py, c))
            cp.start()
            cps.append(cp)
        for cp in cps:
            cp.wait()

    launch()
    return src[...], got[...]


def _join_sibling_async(name, half_filled):
    buf = jax.new_ref(half_filled, memory_space=pltpu.MemorySpace.HBM)
    hr = half_filled.shape[0] // 2

    @_sequencer_kernel(name, 2, COLLECTIVE_SIBLING)
    def launch(send_sem, recv_sem):
        x, y, c, _ = _place()
        _handshake([(x, y, 1 - c)])
        mine = buf.at[pl.ds(c * hr, hr)]
        other = buf.at[pl.ds((1 - c) * hr, hr)]
        cp = _remote(mine, mine, send_sem, recv_sem, (x, y, 1 - c))
        cp.start()
        _remote(other, other, send_sem, recv_sem, (x, y, 1 - c)).wait_recv()
        cp.wait_send()

    launch()
    return buf[...]


def _allgather_chips_small(name, v):
    r = v.shape[0]

    def body(v_ref, o_ref, send_sems, recv_sems):
        x, y, c, chips = _place()
        me = 2 * x + y
        o_ref[me] = v_ref[...]
        cps = []
        for k, (px, py) in enumerate(chips):
            cp = _remote(v_ref, o_ref.at[me], send_sems.at[k], recv_sems.at[k], (px, py, c))
            cp.start()
            cps.append(cp)
        for k, (px, py) in enumerate(chips):
            slot = o_ref.at[2 * px + py]
            _remote(slot, slot, send_sems.at[k], recv_sems.at[k], (px, py, c)).wait_recv()
        for cp in cps:
            cp.wait_send()

    return pl.pallas_call(
        body, name=name, in_specs=[pl.BlockSpec(memory_space=pltpu.VMEM)],
        out_specs=pl.BlockSpec(memory_space=pltpu.VMEM),
        out_shape=jax.ShapeDtypeStruct((N_CHIPS, r, LANES), F32),
        scratch_shapes=[pltpu.SemaphoreType.DMA((3,)), pltpu.SemaphoreType.DMA((3,))],
    )(v)


def _allreduce_small(name, v):
    r = v.shape[0]
    hr = r // 2
    assert hr % SUBLANES == 0

    def body(v_ref, o_ref, sib_ref, chips_ref, send_sems, recv_sems):
        x, y, c, chips = _place()
        me = 2 * x + y
        sibling = (x, y, 1 - c)
        first = _remote(v_ref, sib_ref, send_sems.at[0], recv_sems.at[0], sibling)
        first.start()
        first.wait()
        mine = pl.ds(pl.multiple_of(c * hr, SUBLANES), hr)
        chips_ref[me] = v_ref[mine, :] + sib_ref[mine, :]
        cps = []
        for k, (px, py) in enumerate(chips):
            cp = _remote(chips_ref.at[me], chips_ref.at[me], send_sems.at[1 + k], recv_sems.at[1 + k], (px, py, c))
            cp.start()
            cps.append(cp)
        for k, (px, py) in enumerate(chips):
            slot = chips_ref.at[2 * px + py]
            _remote(slot, slot, send_sems.at[1 + k], recv_sems.at[1 + k], (px, py, c)).wait_recv()
        total = chips_ref[0]
        for j in range(1, N_CHIPS):
            total = total + chips_ref[j]
        o_ref[mine, :] = total
        last = _remote(o_ref.at[mine], o_ref.at[mine], send_sems.at[4], recv_sems.at[4], sibling)
        last.start()
        other = o_ref.at[pl.ds(pl.multiple_of((1 - c) * hr, SUBLANES), hr)]
        _remote(other, other, send_sems.at[4], recv_sems.at[4], sibling).wait_recv()
        last.wait_send()
        for cp in cps:
            cp.wait_send()

    return pl.pallas_call(
        body, name=name, in_specs=[pl.BlockSpec(memory_space=pltpu.VMEM)],
        out_specs=pl.BlockSpec(memory_space=pltpu.VMEM),
        out_shape=jax.ShapeDtypeStruct((r, LANES), F32),
        scratch_shapes=[pltpu.VMEM((r, LANES), F32), pltpu.VMEM((N_CHIPS, hr, LANES), F32),
                        pltpu.SemaphoreType.DMA((5,)), pltpu.SemaphoreType.DMA((5,))],
    )(v)


def _add_sibling(name, slabs, recv, c):
    _, r, cols = slabs.shape
    hr = r // 2
    tr = _pick(hr, ROW_TILE)
    nb = hr // tr

    def body(c_ref, a_ref, b_ref, o_ref):
        o_ref[...] = (a_ref[...].astype(F32) + b_ref[...].astype(F32)).astype(BF16)

    grid_spec = pltpu.PrefetchScalarGridSpec(
        num_scalar_prefetch=1, grid=(N_CHIPS, nb),
        in_specs=[pl.BlockSpec((None, tr, cols), lambda j, i, c_ref: (j, c_ref[0] * nb + i, 0)),
                  pl.BlockSpec((None, tr, cols), lambda j, i, c_ref: (j, i, 0))],
        out_specs=pl.BlockSpec((None, tr, cols), lambda j, i, c_ref: (j, i, 0)))
    return pl.pallas_call(
        body, name=name, grid_spec=grid_spec,
        out_shape=jax.ShapeDtypeStruct((N_CHIPS, hr, cols), BF16),
        compiler_params=_cp(("parallel", "parallel")))(jnp.reshape(c, (1,)).astype(jnp.int32), slabs, recv)


def _sum_chips(name, own, recv, chip, c):
    _, hr, cols = recv.shape
    tr = _pick(hr, ROW_TILE)
    nb = hr // tr

    def body(sc_ref, own_ref, recv_ref, o_ref):
        total = own_ref[...].astype(F32)
        for k in range(3):
            total = total + recv_ref[k].astype(F32)
        o_ref[...] = total

    grid_spec = pltpu.PrefetchScalarGridSpec(
        num_scalar_prefetch=1, grid=(nb,),
        in_specs=[pl.BlockSpec((None, tr, cols), lambda i, sc: (sc[0], i, 0)),
                  pl.BlockSpec((3, tr, cols), lambda i, sc: (0, i, 0))],
        out_specs=pl.BlockSpec((tr, cols), lambda i, sc: (sc[1] * nb + i, 0)))
    return pl.pallas_call(
        body, name=name, grid_spec=grid_spec, out_shape=jax.ShapeDtypeStruct((2 * hr, cols), F32),
        compiler_params=_cp(("parallel",)))(jnp.stack([chip, c]).astype(jnp.int32), own, recv)


def _adamw_math(w, g, m, v):
    m = ADAM_B1 * m + (1.0 - ADAM_B1) * g
    v = ADAM_B2 * v + (1.0 - ADAM_B2) * (g * g)
    m_hat = m / (1.0 - ADAM_B1 ** ADAM_STEP)
    v_hat = v / (1.0 - ADAM_B2 ** ADAM_STEP)
    delta = -ADAM_LR * (m_hat / (jnp.sqrt(v_hat) + ADAM_EPS) + ADAM_WD * w)
    return delta, m, v


def _adamw(name, w, gs, m, v):
    nl, r, cols = w.shape
    tr = _pick(r, LANES)

    def body(*refs):
        w_ref, m_ref, v_ref = refs[0:3]
        g_refs = refs[3:3 + nl]
        go_ref, d_ref, nm_ref, nv_ref = refs[3 + nl:]
        layer = pl.program_id(0)
        g = g_refs[0][...]
        for j in range(1, nl):
            g = jnp.where(layer == j, g_refs[j][...], g)
        d, nm, nv = _adamw_math(w_ref[...], g, m_ref[...], v_ref[...])
        go_ref[...] = g
        d_ref[...] = d
        nm_ref[...] = nm
        nv_ref[...] = nv

    spec3 = pl.BlockSpec((None, tr, cols), lambda l, i: (l, i, 0))
    gspec = pl.BlockSpec((tr, cols), lambda l, i: (i, 0))
    out = jax.ShapeDtypeStruct((nl, r, cols), F32)
    return pl.pallas_call(
        body, name=name, grid=(nl, r // tr), in_specs=[spec3] * 3 + [gspec] * nl, out_specs=[spec3] * 4,
        out_shape=[out] * 4, compiler_params=_cp(("parallel", "parallel")))(w, m, v, *gs)


def _adamw_small(name, groups):
    n = len(groups)
    flat = [a for grp in groups for a in grp]

    def body(*refs):
        ins, outs = refs[:4 * n], refs[4 * n:]
        for p in range(n):
            w_ref, g_ref, m_ref, v_ref = ins[4 * p:4 * p + 4]
            d, nm, nv = _adamw_math(w_ref[...], g_ref[...], m_ref[...], v_ref[...])
            outs[3 * p][...] = d
            outs[3 * p + 1][...] = nm
            outs[3 * p + 2][...] = nv

    vm = pl.BlockSpec(memory_space=pltpu.VMEM)
    out_shape = [jax.ShapeDtypeStruct(grp[0].shape, F32) for grp in groups for _ in range(3)]
    res = pl.pallas_call(
        body, name=name, in_specs=[vm] * (4 * n), out_specs=[vm] * (3 * n), out_shape=out_shape)(*flat)
    return [tuple(res[3 * p:3 * p + 3]) for p in range(n)]


def _block_diag_pairs(w):
    h, d, _ = w.shape
    z = jnp.zeros((h // 2, d, d), w.dtype)
    top = jnp.concatenate([w[0::2], z], axis=2)
    bot = jnp.concatenate([z, w[1::2]], axis=2)
    return jnp.concatenate([top, bot], axis=1).astype(BF16)


def _diag_pairs_to_heads(g, d):
    a = g[:, :d, :d]
    b = g[:, d:, d:]
    return jnp.stack([a, b], axis=1).reshape(-1, d, d)


def _rows128(a):
    flat = a.reshape(-1, LANES)
    pad = (-flat.shape[0]) % SUBLANES
    if pad:
        flat = jnp.concatenate([flat, jnp.zeros((pad, LANES), flat.dtype)], axis=0)
    return flat


def _unshard_last(g4, shape):
    g4 = g4.reshape((N_CHIPS,) + tuple(shape))
    return jnp.concatenate([g4[j] for j in range(N_CHIPS)], axis=-1)


def kernel(x, norm_gains, hyb_w_in, hyb_conv_a, hyb_conv_b, hyb_conv_b_bias, hyb_rg_w_a, hyb_rg_b_a, hyb_rg_w_x, hyb_rg_b_x, hyb_rg_lambda, hyb_w_out, sb_w_qkv, sb_w_o, mlp_w_up, mlp_w_down, loss_target, m_norm_gains, m_hyb_w_in, m_hyb_conv_a, m_hyb_conv_b, m_hyb_conv_b_bias, m_hyb_rg_w_a, m_hyb_rg_b_a, m_hyb_rg_w_x, m_hyb_rg_b_x, m_hyb_rg_lambda, m_hyb_w_out, m_sb_w_qkv, m_sb_w_o, m_mlp_w_up, m_mlp_w_down, v_norm_gains, v_hyb_w_in, v_hyb_conv_a, v_hyb_conv_b, v_hyb_conv_b_bias, v_hyb_rg_w_a, v_hyb_rg_b_a, v_hyb_rg_w_x, v_hyb_rg_b_x, v_hyb_rg_lambda, v_hyb_w_out, v_sb_w_qkv, v_sb_w_o, v_mlp_w_up, v_mlp_w_down):
    cx_ = lax.axis_index("x")
    cy_ = lax.axis_index("y")
    cc_ = lax.axis_index("c")
    chip = 2 * cx_ + cy_

    x0 = x[0]
    target = loss_target[0]
    s, d = x0.shape
    heads = SB_HEADS
    assert d // heads == LANES
    n_rg, hd = hyb_rg_w_a.shape[1], hyb_rg_w_a.shape[2]
    wmix = n_rg * hd
    assert 2 * hd == LANES

    big = {
        "hyb_w_in": (hyb_w_in, 0), "hyb_w_out": (hyb_w_out, 0), "mlp_w_up0": (mlp_w_up, 0),
        "mlp_w_down0": (mlp_w_down, 0), "sb_w_qkv": (sb_w_qkv, 0), "sb_w_o": (sb_w_o, 0),
        "mlp_w_up1": (mlp_w_up, 1), "mlp_w_down1": (mlp_w_down, 1),
    }
    names = list(big)
    slots = [_cast_into_slot("cast_" + k, big[k][0], big[k][1], chip) for k in names]
    full = {k: _allgather_async("allgather_" + k, slot, cid) for cid, (k, slot) in enumerate(zip(names, slots))}
    rowsharded = lambda k: full[k].reshape(-1, full[k].shape[2])

    ng_s, ca_s, cb_s = norm_gains.reshape(-1, norm_gains.shape[2]), hyb_conv_a[0], hyb_conv_b[0]
    packed = jnp.concatenate([_rows128(ng_s), _rows128(ca_s), _rows128(cb_s)], axis=0)
    gathered = _allgather_chips_small("allgather_small", packed)
    n0 = ng_s.size // LANES
    n1 = n0 + (-n0) % SUBLANES
    m0 = ca_s.size // LANES
    m1 = m0 + (-m0) % SUBLANES
    k0 = cb_s.size // LANES
    gains = _unshard_last(gathered[:, 0:n0], ng_s.shape).reshape(2, 4, 1, d)
    conv_a = _unshard_last(gathered[:, n1:n1 + m0], ca_s.shape)
    conv_b = _unshard_last(gathered[:, n1 + m1:n1 + m1 + k0], cb_s.shape)
    bias, b_a, b_x, lam = hyb_conv_b_bias, hyb_rg_b_a, hyb_rg_b_x, hyb_rg_lambda
    wa_blk = _block_diag_pairs(hyb_rg_w_a[0])
    wx_blk = _block_diag_pairs(hyb_rg_w_x[0])

    relu_sq = lambda acc: (jnp.maximum(acc, 0.0), jnp.square(jnp.maximum(acc, 0.0)))

    h1 = _rms_fwd("rms_pre0", x0, gains[0, 0])
    proj = _mm_fwd_col("proj_in", h1, full["hyb_w_in"])[0]
    ycat, hseq = _mixer_fwd(proj, conv_a, conv_b, bias, wa_blk, b_a, wx_blk, b_x, lam)
    mix0 = _mm_fwd_row("proj_out", ycat, rowsharded("hyb_w_out"))
    x1, h2 = _rms_post("rms_mix0", mix0, gains[0, 1], x0, gains[0, 2])
    u0, a0 = _mm_fwd_col("mlp_up0", h2, full["mlp_w_up0"], (BF16, BF16), relu_sq)
    mlp0 = _mm_fwd_row("mlp_down0", a0, rowsharded("mlp_w_down0"))
    x2, h3 = _rms_post("rms_mlp0", mlp0, gains[0, 3], x1, gains[1, 0])

    qkv = _mm_fwd_col("qkv", h3, full["sb_w_qkv"], (BF16,))[0]
    att, tot = _attn_fwd(qkv, heads)
    mix1 = _mm_fwd_row("attn_out", att, rowsharded("sb_w_o"))
    x3, h4 = _rms_post("rms_mix1", mix1, gains[1, 1], x2, gains[1, 2])
    u1, a1 = _mm_fwd_col("mlp_up1", h4, full["mlp_w_up1"], (BF16, BF16), relu_sq)
    mlp1 = _mm_fwd_row("mlp_down1", a1, rowsharded("mlp_w_down1"))
    dy, dmlp1, dgain_mlp1, loss_local = _last_norm_and_loss("last_norm_loss", mlp1, gains[1, 3], x3, target)
    loss = lax.psum(loss_local, ("x", "y", "c"))

    dgain = [[None] * 4 for _ in range(2)]
    drelu = lambda acc, u: (acc * (2.0 * u.astype(F32)),)
    stage_a, stage_b, gfull = {}, {}, {}

    def tie(main, side):
        return lax.optimization_barrier((main, side))

    def reduce_start(k, slab, main):
        main, slab = tie(main, slab)
        stage_a[k] = _to_sibling_async("grads_to_sibling_" + k, slab)
        return main

    def reduce_to_chips(k, main):
        slab, from_sibling = stage_a.pop(k)
        main, part = tie(main, _add_sibling("grads_add_" + k, slab, from_sibling, cc_))
        stage_b[k] = _to_chips_async("grads_to_chips_" + k, part)
        return main

    def after(value, token):
        return tie(value, token)[0]

    def reduce_finish(k, main):
        own, from_chips = stage_b.pop(k)
        main, half = tie(main, _sum_chips("grads_sum_" + k, after(own, main), from_chips, chip, cc_))
        gfull[k] = _join_sibling_async("grads_join_" + k, half)
        return main

    def mlp_bwd(layer, dxo, dmlp, xin, hin, u, a, mix):
        down, up = f"mlp_w_down{layer}", f"mlp_w_up{layer}"
        wd, wu = rowsharded(down), full[up]
        dmlp = reduce_start(down, _mm_wgrad_row(f"mlp_down{layer}_wgrad", a, dmlp).reshape(N_CHIPS, -1, d), dmlp)
        du = _mm_bwd_row(f"mlp_down{layer}_bwd", dmlp, wd, (BF16,), u, drelu)[0]
        du = reduce_start(up, _mm_wgrad_col(f"mlp_up{layer}_wgrad", hin, du, wu.shape[2]), du)
        du = reduce_to_chips(down, du)
        dh = _mm_bwd_col(f"mlp_up{layer}_bwd", du, wu)
        dh = reduce_to_chips(up, dh)
        dxm, dgain[layer][2], dmix, dgain[layer][1] = _rms_bwd_pair(
            f"rms_premlp{layer}_mix{layer}_bwd", xin, gains[layer, 2], dh, dxo, mix, gains[layer, 1])
        return dxm, dmix

    dgain[1][3] = dgain_mlp1
    dx3, dmix1 = mlp_bwd(1, dy, dmlp1, x3, h4, u1, a1, mix1)
    dmix1 = reduce_start("sb_w_o", _mm_wgrad_row("attn_out_wgrad", att, dmix1).reshape(N_CHIPS, -1, d), dmix1)
    datt = _mm_bwd_row("attn_out_bwd", dmix1, rowsharded("sb_w_o"), (BF16,))[0]
    dq, dk, dv = _attn_bwd(qkv, tot, datt, heads)
    dqkv = jnp.concatenate([dq, dk, dv], axis=1)
    dqkv = reduce_to_chips("sb_w_o", dqkv)
    dqkv = reduce_finish("mlp_w_down1", dqkv)
    dqkv = reduce_finish("mlp_w_up1", dqkv)
    dqkv = reduce_start("sb_w_qkv", _mm_wgrad_col("qkv_wgrad", h3, dqkv, full["sb_w_qkv"].shape[2]), dqkv)
    dh3 = _mm_bwd_col("qkv_bwd", dqkv, full["sb_w_qkv"])
    dh3 = reduce_to_chips("sb_w_qkv", dh3)
    dx2, dgain[1][0], dmlp0, dgain[0][3] = _rms_bwd_pair(
        "rms_pre1_mlp0_bwd", x2, gains[1, 0], dh3, dx3, mlp0, gains[0, 3])

    dx1, dmix0 = mlp_bwd(0, dx2, dmlp0, x1, h2, u0, a0, mix0)
    dmix0 = reduce_finish("sb_w_o", dmix0)
    dmix0 = reduce_finish("sb_w_qkv", dmix0)
    dmix0 = reduce_finish("mlp_w_down0", dmix0)
    dmix0 = reduce_start("hyb_w_out", _mm_wgrad_row("proj_out_wgrad", ycat, dmix0).reshape(N_CHIPS, -1, d), dmix0)
    dycat = _mm_bwd_row("proj_out_bwd", dmix0, rowsharded("hyb_w_out"))[0]
    dproj, xr_b, dpa_b, dpx_b, sg = _mixer_bwd(
        proj, hseq, dycat, conv_a, conv_b, bias, wa_blk, b_a, wx_blk, b_x, lam)
    dproj = reduce_finish("mlp_w_up0", dproj)
    dproj = reduce_to_chips("hyb_w_out", dproj)
    dproj = reduce_start("hyb_w_in", _mm_wgrad_col("proj_in_wgrad", h1, dproj, full["hyb_w_in"].shape[2]), dproj)
    dh1 = _mm_bwd_col("proj_in_bwd", dproj, full["hyb_w_in"])
    dh1 = reduce_to_chips("hyb_w_in", dh1)
    dx0, dgain[0][0] = _rms_bwd("rms_pre0_bwd", x0, gains[0, 0], dh1, res=dx1)
    dwa = _diag_pairs_to_heads(_mm_wgrad_diag("rg_w_a_wgrad", xr_b, dpa_b), hd)
    dwx = _diag_pairs_to_heads(_mm_wgrad_diag("rg_w_x_wgrad", xr_b, dpx_b), hd)

    dgains = jnp.concatenate([dgain[l][k] for l in range(2) for k in range(4)], axis=0)
    small_parts = [dgains, sg[_SG_CONV_A:_SG_CONV_A + 3], sg[_SG_CONV_B:_SG_CONV_B + 4], sg[_SG_BIAS:_SG_BIAS + 1],
                   dwa, sg[_SG_BA:_SG_BA + 1], dwx, sg[_SG_BX:_SG_BX + 1], sg[_SG_LAM:_SG_LAM + 1]]
    small_rows = [_rows128(p) for p in small_parts]
    n_small = sum(rws.shape[0] for rws in small_rows)
    tail_pad = [jnp.zeros(((-n_small) % (2 * SUBLANES), LANES), F32)] if n_small % (2 * SUBLANES) else []
    reduced = _allreduce_small("allreduce_small", jnp.concatenate(small_rows + tail_pad, axis=0))
    small_full, off = [], 0
    for p, rws in zip(small_parts, small_rows):
        small_full.append(reduced[off:off + p.size // LANES].reshape(p.shape))
        off += rws.shape[0]
    g_gains, g_ca, g_cb, g_bias, g_wa, g_ba, g_wx, g_bx, g_lam = small_full

    def my_cols(g, width):
        return lax.dynamic_slice_in_dim(g, chip * width, width, axis=g.ndim - 1)

    small = [
        ("norm_gains", norm_gains, my_cols(g_gains, norm_gains.shape[2]).reshape(norm_gains.shape),
         m_norm_gains, v_norm_gains),
        ("hyb_conv_a", hyb_conv_a, my_cols(g_ca, hyb_conv_a.shape[2])[None], m_hyb_conv_a, v_hyb_conv_a),
        ("hyb_conv_b", hyb_conv_b, my_cols(g_cb, hyb_conv_b.shape[2])[None], m_hyb_conv_b, v_hyb_conv_b),
        ("hyb_conv_b_bias", hyb_conv_b_bias, g_bias, m_hyb_conv_b_bias, v_hyb_conv_b_bias),
        ("hyb_rg_w_a", hyb_rg_w_a, g_wa[None], m_hyb_rg_w_a, v_hyb_rg_w_a),
        ("hyb_rg_b_a", hyb_rg_b_a, g_ba, m_hyb_rg_b_a, v_hyb_rg_b_a),
        ("hyb_rg_w_x", hyb_rg_w_x, g_wx[None], m_hyb_rg_w_x, v_hyb_rg_w_x),
        ("hyb_rg_b_x", hyb_rg_b_x, g_bx, m_hyb_rg_b_x, v_hyb_rg_b_x),
        ("hyb_rg_lambda", hyb_rg_lambda, g_lam, m_hyb_rg_lambda, v_hyb_rg_lambda),
    ]
    to2d = lambda a: a.reshape(-1, a.shape[-1])
    small_res = _adamw_small("adamw_small", [tuple(to2d(a) for a in (w, g, m, v)) for _, w, g, m, v in small])
    out = {}
    for (nm, w, g, _, _), (dl, nmom, nvar) in zip(small, small_res):
        out[nm] = (g, dl.reshape(w.shape), nmom.reshape(w.shape), nvar.reshape(w.shape))

    stacked = {
        "mlp_w_down": (mlp_w_down, m_mlp_w_down, v_mlp_w_down, ["mlp_w_down0", "mlp_w_down1"]),
        "mlp_w_up": (mlp_w_up, m_mlp_w_up, v_mlp_w_up, ["mlp_w_up0", "mlp_w_up1"]),
        "sb_w_o": (sb_w_o, m_sb_w_o, v_sb_w_o, ["sb_w_o"]),
        "sb_w_qkv": (sb_w_qkv, m_sb_w_qkv, v_sb_w_qkv, ["sb_w_qkv"]),
        "hyb_w_out": (hyb_w_out, m_hyb_w_out, v_hyb_w_out, ["hyb_w_out"]),
        "hyb_w_in": (hyb_w_in, m_hyb_w_in, v_hyb_w_in, ["hyb_w_in"]),
    }

    def update(k, token):
        w, m, v, parts = stacked[k]
        out[k] = tuple(_adamw("adamw_" + k, w, [after(gfull[p], token) for p in parts], m, v))
        return out[k][1]

    token = small_res[0][0]
    token = update("sb_w_qkv", token)
    token = update("sb_w_o", token)
    token = update("mlp_w_down", token)
    token = reduce_finish("hyb_w_out", token)
    token = update("mlp_w_up", token)
    token = reduce_finish("hyb_w_in", token)
    token = update("hyb_w_out", token)
    update("hyb_w_in", token)

    order = ["norm_gains", "hyb_w_in", "hyb_conv_a", "hyb_conv_b", "hyb_conv_b_bias", "hyb_rg_w_a", "hyb_rg_b_a",
             "hyb_rg_w_x", "hyb_rg_b_x", "hyb_rg_lambda", "hyb_w_out", "sb_w_qkv", "sb_w_o", "mlp_w_up",
             "mlp_w_down"]
    return (loss, dx0[None], *[out[k][0] for k in order], *[out[k][1] for k in order],
            *[out[k][2] for k in order], *[out[k][3] for k in order])
```

```python
import functools
import math

import jax
import jax.numpy as jnp
from jax import lax
from jax.experimental import pallas as pl
from jax.experimental.pallas import tpu as pltpu
from jax.experimental.pallas import tpu_sc as plsc

F32 = jnp.float32
BF16 = jnp.bfloat16
MESH = pl.DeviceIdType.MESH

SB_HEADS = 16
NORM_EPS = 1e-6
LRU_C = 8.0
ADAM_LR = 0.001
ADAM_B1 = 0.9
ADAM_B2 = 0.999
ADAM_EPS = 1e-08
ADAM_WD = 0.01
ADAM_STEP = 10

LANES = 128
SUBLANES = 8
VMEM_LIMIT = 48 * 1024 * 1024
MM_TILE = 1024
MM_VMEM_BUDGET = 40 * 1024 * 1024
MM_TILE_N = 1280
MM_TILE_K = 2048
ROW_TILE = 256
STREAM_TILE = 1024
ATT_TILE = 512
ATT_HEADS_PER_STEP = 2
N_CHIPS = 4
COLLECTIVE_SIBLING = 8
COLLECTIVE_CHIPS = 9

_DIMS = {
    "nn": (((1,), (0,)), ((), ())),
    "nt": (((1,), (1,)), ((), ())),
    "tn": (((0,), (0,)), ((), ())),
}


def _cp(sem=None, vmem=VMEM_LIMIT):
    return pltpu.CompilerParams(dimension_semantics=sem, vmem_limit_bytes=vmem)


def _pick(dim, pref):
    t = min(dim, pref)
    while dim % t:
        t -= LANES
    return t


def _whole(shape):
    nd = len(shape)
    return pl.BlockSpec(tuple(shape), lambda *_: (0,) * nd)


def _sigmoid(z):
    return 1.0 / (1.0 + jnp.exp(-z))


def _log_sigmoid(z):
    return jnp.minimum(z, 0.0) - jnp.log(1.0 + jnp.exp(-jnp.abs(z)))


def _expm1(z):
    series = z * (1.0 + z * (0.5 + z * (1.0 / 6.0 + z * (1.0 / 24.0))))
    return jnp.where(jnp.abs(z) < 0.05, series, jnp.exp(z) - 1.0)


_GELU_C = math.sqrt(2.0 / math.pi)


def _gelu_and_grad(g):
    inner = _GELU_C * (g + 0.044715 * g * g * g)
    t = jnp.tanh(inner)
    val = 0.5 * g * (1.0 + t)
    grad = 0.5 * (1.0 + t) + 0.5 * g * (1.0 - t * t) * _GELU_C * (1.0 + 3.0 * 0.044715 * g * g)
    return val, grad


def _shift_down(cur, prev8, k, rows):
    n = cur.shape[0]
    rolled = pltpu.roll(cur, k, 0)
    head = jnp.tile(pltpu.roll(prev8, k, 0), (n // SUBLANES, 1))
    return jnp.where(rows < k, head, rolled)


def _shift_up(cur, next8, k, rows):
    n = cur.shape[0]
    rolled = pltpu.roll(cur, n - k, 0)
    tail = jnp.tile(pltpu.roll(next8, SUBLANES - k, 0), (n // SUBLANES, 1))
    return jnp.where(rows >= n - k, tail, rolled)


def _colsum(v):
    return jnp.sum(v, axis=0, keepdims=True)


def _matmul(name, mode, grid, operands, in_specs, out_shapes, out_specs, acc_shape, epilogue=None):
    nk = grid[2]
    n_in = len(operands)
    dims = _DIMS[mode]

    def finish(acc, extra, outs):
        res = epilogue(acc, *[e[...] for e in extra]) if epilogue is not None else (acc,)
        for o_ref, o in zip(outs, res):
            o_ref[...] = o.astype(o_ref.dtype)

    def product(a_ref, b_ref):
        return lax.dot_general(a_ref[...].astype(BF16), b_ref[...].astype(BF16), dims, preferred_element_type=F32)

    def body_single(*refs):
        finish(product(refs[0], refs[1]), refs[2:n_in], refs[n_in:])

    def body(*refs):
        extra = refs[2:n_in]
        outs = refs[n_in:-1]
        acc_ref = refs[-1]
        k = pl.program_id(2)

        @pl.when(k == 0)
        def _():
            acc_ref[...] = product(refs[0], refs[1])

        @pl.when(k > 0)
        def _():
            acc_ref[...] += product(refs[0], refs[1])

        @pl.when(k == nk - 1)
        def _():
            finish(acc_ref[...], extra, outs)

    return pl.pallas_call(
        body_single if nk == 1 else body, name=name, grid=grid, in_specs=in_specs, out_specs=out_specs,
        out_shape=out_shapes, scratch_shapes=[] if nk == 1 else [pltpu.VMEM(acc_shape, F32)],
        compiler_params=_cp(("parallel", "parallel", "arbitrary")),
    )(*operands)


def _pick_m(m, tk, tn, a_dtype, b_dtype, out_dtypes, extra_dtypes=()):
    size = lambda dt: jnp.dtype(dt).itemsize
    per_row = 2 * tk * size(a_dtype) + tn * (2 * sum(size(dt) for dt in tuple(out_dtypes) + tuple(extra_dtypes)) + 4)
    fixed = 2 * tk * tn * size(b_dtype)
    tm = _pick(m, MM_TILE)
    while tm > LANES and tm * per_row + fixed > MM_VMEM_BUDGET:
        tm = _pick(m, tm // 2)
    return tm


def _mm_fwd_col(name, a, wfull, out_dtypes=(F32,), epilogue=None):
    s, kdim = a.shape
    _, _, cs = wfull.shape
    tk, tn = _pick(kdim, MM_TILE_K), _pick(cs, MM_TILE_N)
    tm = _pick_m(s, tk, tn, a.dtype, wfull.dtype, out_dtypes)
    nbj = cs // tn
    grid = (s // tm, N_CHIPS * nbj, kdim // tk)
    out_shapes = [jax.ShapeDtypeStruct((s, N_CHIPS * cs), dt) for dt in out_dtypes]
    out_specs = [pl.BlockSpec((tm, tn), lambda i, n, k: (i, n)) for _ in out_dtypes]
    return _matmul(
        name, "nn", grid, [a, wfull],
        [pl.BlockSpec((tm, tk), lambda i, n, k: (i, k)),
         pl.BlockSpec((None, tk, tn), lambda i, n, k: (n // nbj, k, n % nbj))],
        out_shapes, out_specs, (tm, tn), epilogue)


def _mm_fwd_row(name, a, w2d, out_dtype=F32):
    s, kdim = a.shape
    _, n_out = w2d.shape
    tk, tn = _pick(kdim, MM_TILE_K), _pick(n_out, MM_TILE)
    tm = _pick_m(s, tk, tn, a.dtype, w2d.dtype, (out_dtype,))
    grid = (s // tm, n_out // tn, kdim // tk)
    return _matmul(
        name, "nn", grid, [a, w2d],
        [pl.BlockSpec((tm, tk), lambda i, n, k: (i, k)),
         pl.BlockSpec((tk, tn), lambda i, n, k: (k, n))],
        [jax.ShapeDtypeStruct((s, n_out), out_dtype)],
        [pl.BlockSpec((tm, tn), lambda i, n, k: (i, n))], (tm, tn))[0]


def _mm_bwd_col(name, dy, wfull, out_dtype=F32):
    s, _ = dy.shape
    _, kdim, cs = wfull.shape
    tn, tk = _pick(kdim, MM_TILE), _pick(cs, MM_TILE_K)
    tm = _pick_m(s, tk, tn, dy.dtype, wfull.dtype, (out_dtype,))
    nbj = cs // tk
    grid = (s // tm, kdim // tn, N_CHIPS * nbj)
    return _matmul(
        name, "nt", grid, [dy, wfull],
        [pl.BlockSpec((tm, tk), lambda i, n, k: (i, k)),
         pl.BlockSpec((None, tn, tk), lambda i, n, k: (k // nbj, n, k % nbj))],
        [jax.ShapeDtypeStruct((s, kdim), out_dtype)],
        [pl.BlockSpec((tm, tn), lambda i, n, k: (i, n))], (tm, tn))[0]


def _mm_bwd_row(name, dy, w2d, out_dtypes=(F32,), extra=None, epilogue=None):
    s, n_in = dy.shape
    kdim, _ = w2d.shape
    tn, tk = _pick(kdim, MM_TILE), _pick(n_in, MM_TILE_K)
    tm = _pick_m(s, tk, tn, dy.dtype, w2d.dtype, out_dtypes, () if extra is None else (extra.dtype,))
    grid = (s // tm, kdim // tn, n_in // tk)
    operands = [dy, w2d]
    in_specs = [pl.BlockSpec((tm, tk), lambda i, n, k: (i, k)),
                pl.BlockSpec((tn, tk), lambda i, n, k: (n, k))]
    if extra is not None:
        operands.append(extra)
        in_specs.append(pl.BlockSpec((tm, tn), lambda i, n, k: (i, n)))
    return _matmul(
        name, "nt", grid, operands, in_specs,
        [jax.ShapeDtypeStruct((s, kdim), dt) for dt in out_dtypes],
        [pl.BlockSpec((tm, tn), lambda i, n, k: (i, n)) for _ in out_dtypes], (tm, tn), epilogue)


def _mm_wgrad_row(name, a, dy):
    s, kdim = a.shape
    _, n_out = dy.shape
    tn, ts = _pick(n_out, MM_TILE), _pick(s, MM_TILE_K)
    tm = _pick_m(kdim, ts, tn, a.dtype, dy.dtype, (BF16,))
    grid = (kdim // tm, n_out // tn, s // ts)
    return _matmul(
        name, "tn", grid, [a, dy],
        [pl.BlockSpec((ts, tm), lambda i, n, k: (k, i)),
         pl.BlockSpec((ts, tn), lambda i, n, k: (k, n))],
        [jax.ShapeDtypeStruct((kdim, n_out), BF16)],
        [pl.BlockSpec((tm, tn), lambda i, n, k: (i, n))], (tm, tn))[0]


def _mm_wgrad_half(name, a, dy, half, cs=None, init=None):
    s, kdim = a.shape
    ts = _pick(s, MM_TILE_K)
    nk = s // ts
    if cs is not None:
        hr, cols = kdim // 2, cs
        tn = _pick(cs, MM_TILE_N)
        tm = _pick_m(hr, ts, tn, a.dtype, dy.dtype, (BF16,), (BF16,))
        ni, nbj = hr // tm, cs // tn
        grid = (ni, N_CHIPS * nbj, nk)
        a_map = lambda i, n, k, h: (k, h[0] * ni + i)
        o_map = lambda i, n, k, h: (n // nbj, i, n % nbj)
    else:
        hr, cols = kdim // N_CHIPS // 2, dy.shape[1]
        tn = _pick(cols, MM_TILE)
        tm = _pick_m(hr, ts, tn, a.dtype, dy.dtype, (BF16,), (BF16,))
        ni = hr // tm
        grid = (N_CHIPS * ni, cols // tn, nk)
        a_map = lambda i, n, k, h: (k, (i // ni) * 2 * ni + h[0] * ni + i % ni)
        o_map = lambda i, n, k, h: (i // ni, i % ni, n)
    with_init = init is not None

    def body(*refs):
        a_ref, b_ref = refs[1], refs[2]
        init_ref = refs[3] if with_init else None
        o_ref, acc_ref = refs[-2], refs[-1]
        k = pl.program_id(2)
        prod = lax.dot_general(a_ref[...].astype(BF16), b_ref[...].astype(BF16), _DIMS["tn"],
                               preferred_element_type=F32)

        @pl.when(k == 0)
        def _():
            acc_ref[...] = prod + init_ref[...].astype(F32) if with_init else prod

        @pl.when(k > 0)
        def _():
            acc_ref[...] += prod

        @pl.when(k == nk - 1)
        def _():
            o_ref[...] = acc_ref[...].astype(BF16)

    oblk = pl.BlockSpec((None, tm, tn), o_map)
    grid_spec = pltpu.PrefetchScalarGridSpec(
        num_scalar_prefetch=1, grid=grid,
        in_specs=[pl.BlockSpec((ts, tm), a_map), pl.BlockSpec((ts, tn), lambda i, n, k, h: (k, n))]
        + ([oblk] if with_init else []),
        out_specs=oblk, scratch_shapes=[pltpu.VMEM((tm, tn), F32)])
    operands = [jnp.reshape(half, (1,)).astype(jnp.int32), a, dy] + ([init] if with_init else [])
    return pl.pallas_call(
        body, name=name, grid_spec=grid_spec, out_shape=jax.ShapeDtypeStruct((N_CHIPS, hr, cols), BF16),
        compiler_params=_cp(("parallel", "parallel", "arbitrary")))(*operands)


def _mm_wgrad_diag(name, a, dy):
    s, width = a.shape
    nb = width // LANES
    ts = _pick(s, MM_TILE)
    grid = (nb, 1, s // ts)
    return _matmul(
        name, "tn", grid, [a, dy],
        [pl.BlockSpec((ts, LANES), lambda i, n, k: (k, i)),
         pl.BlockSpec((ts, LANES), lambda i, n, k: (k, i))],
        [jax.ShapeDtypeStruct((nb, LANES, LANES), F32)],
        [pl.BlockSpec((None, LANES, LANES), lambda i, n, k: (i, 0, 0))], (LANES, LANES))[0]


def _rowspec(tr, d):
    return pl.BlockSpec((tr, d), lambda i: (i, 0))


def _vecspec(d):
    return pl.BlockSpec((1, d), lambda i: (0, 0))


def _rms(x, g):
    return x * lax.rsqrt(jnp.mean(x * x, axis=-1, keepdims=True) + NORM_EPS) * g


def _cast_into_slot(name, w, layer, chip):
    _, r, c = w.shape
    tr = _pick(r, STREAM_TILE)

    def body(chip_ref, w_ref, o_ref):
        o_ref[...] = w_ref[...].astype(BF16)

    grid_spec = pltpu.PrefetchScalarGridSpec(
        num_scalar_prefetch=1, grid=(r // tr,),
        in_specs=[pl.BlockSpec((None, tr, c), lambda i, chip_ref: (layer, i, 0))],
        out_specs=pl.BlockSpec((None, tr, c), lambda i, chip_ref: (chip_ref[0], i, 0)))
    return pl.pallas_call(
        body, name=name, grid_spec=grid_spec, out_shape=jax.ShapeDtypeStruct((N_CHIPS, r, c), BF16),
        compiler_params=_cp(("parallel",)))(jnp.reshape(chip, (1,)).astype(jnp.int32), w)


def _rms_fwd(name, x, g):
    s, d = x.shape
    tr = _pick(s, ROW_TILE)

    def body(x_ref, g_ref, h_ref):
        h_ref[...] = _rms(x_ref[...], g_ref[...]).astype(BF16)

    return pl.pallas_call(
        body, name=name, grid=(s // tr,), in_specs=[_rowspec(tr, d), _vecspec(d)],
        out_specs=_rowspec(tr, d), out_shape=jax.ShapeDtypeStruct((s, d), BF16),
        compiler_params=_cp(("parallel",)))(x, g)


def _rms_post(name, y, g_post, res, g_next=None):
    s, d = y.shape
    tr = _pick(s, ROW_TILE)
    with_next = g_next is not None

    def body(*refs):
        if with_next:
            y_ref, gp_ref, r_ref, gn_ref, x_ref, h_ref = refs
        else:
            y_ref, gp_ref, r_ref, x_ref = refs
        xn = r_ref[...] + _rms(y_ref[...], gp_ref[...])
        x_ref[...] = xn
        if with_next:
            h_ref[...] = _rms(xn, gn_ref[...]).astype(BF16)

    operands = [y, g_post, res] + ([g_next] if with_next else [])
    in_specs = [_rowspec(tr, d), _vecspec(d), _rowspec(tr, d)] + ([_vecspec(d)] if with_next else [])
    out_shape = [jax.ShapeDtypeStruct((s, d), F32)] + ([jax.ShapeDtypeStruct((s, d), BF16)] if with_next else [])
    out_specs = [_rowspec(tr, d)] + ([_rowspec(tr, d)] if with_next else [])
    return pl.pallas_call(
        body, name=name, grid=(s // tr,), in_specs=in_specs, out_specs=out_specs, out_shape=out_shape,
        compiler_params=_cp(("parallel",)))(*operands)


def _rms_bwd(name, x, g, dy, res=None, out_dtype=F32):
    s, d = x.shape
    tr = _pick(s, ROW_TILE)
    nsteps = s // tr
    with_res = res is not None

    def body(*refs):
        if with_res:
            x_ref, g_ref, dy_ref, r_ref, dx_ref, dg_ref, acc_ref = refs
        else:
            x_ref, g_ref, dy_ref, dx_ref, dg_ref, acc_ref = refs
        i = pl.program_id(0)

        @pl.when(i == 0)
        def _():
            acc_ref[...] = jnp.zeros_like(acc_ref)

        xv = x_ref[...]
        dyv = dy_ref[...].astype(F32)
        r = lax.rsqrt(jnp.mean(xv * xv, axis=-1, keepdims=True) + NORM_EPS)
        xhat = xv * r
        gy = dyv * g_ref[...]
        dx = r * (gy - xhat * jnp.mean(gy * xhat, axis=-1, keepdims=True))
        if with_res:
            dx = dx + r_ref[...]
        dx_ref[...] = dx.astype(dx_ref.dtype)
        acc_ref[...] += jnp.sum((dyv * xhat).reshape(tr // SUBLANES, SUBLANES, d), axis=0)

        @pl.when(i == nsteps - 1)
        def _():
            dg_ref[...] = jnp.broadcast_to(_colsum(acc_ref[...]), (SUBLANES, d))

    operands = [x, g, dy] + ([res] if with_res else [])
    in_specs = [_rowspec(tr, d), _vecspec(d), _rowspec(tr, d)] + ([_rowspec(tr, d)] if with_res else [])
    dx, dg = pl.pallas_call(
        body, name=name, grid=(nsteps,), in_specs=in_specs,
        out_specs=[_rowspec(tr, d), pl.BlockSpec((SUBLANES, d), lambda i: (0, 0))],
        out_shape=[jax.ShapeDtypeStruct((s, d), out_dtype), jax.ShapeDtypeStruct((SUBLANES, d), F32)],
        scratch_shapes=[pltpu.VMEM((SUBLANES, d), F32)],
        compiler_params=_cp(("arbitrary",)))(*operands)
    return dx, dg[0:1]


def _rms_bwd_pair(name, x, g, dy, res, y2, g2):
    s, d = x.shape
    tr = _pick(s, ROW_TILE)
    nsteps = s // tr

    def through(xv, gv, dyv):
        r = lax.rsqrt(jnp.mean(xv * xv, axis=-1, keepdims=True) + NORM_EPS)
        xhat = xv * r
        gy = dyv * gv
        dx = r * (gy - xhat * jnp.mean(gy * xhat, axis=-1, keepdims=True))
        return dx, jnp.sum((dyv * xhat).reshape(tr // SUBLANES, SUBLANES, d), axis=0)

    def body(x_ref, g_ref, dy_ref, r_ref, y2_ref, g2_ref, dx_ref, d2_ref, dg_ref, dg2_ref, acc_ref, acc2_ref):
        i = pl.program_id(0)

        @pl.when(i == 0)
        def _():
            acc_ref[...] = jnp.zeros_like(acc_ref)
            acc2_ref[...] = jnp.zeros_like(acc2_ref)

        dx, part = through(x_ref[...], g_ref[...], dy_ref[...].astype(F32))
        dx = dx + r_ref[...]
        dx_ref[...] = dx
        acc_ref[...] += part
        d2, part2 = through(y2_ref[...], g2_ref[...], dx)
        d2_ref[...] = d2.astype(d2_ref.dtype)
        acc2_ref[...] += part2

        @pl.when(i == nsteps - 1)
        def _():
            dg_ref[...] = jnp.broadcast_to(_colsum(acc_ref[...]), (SUBLANES, d))
            dg2_ref[...] = jnp.broadcast_to(_colsum(acc2_ref[...]), (SUBLANES, d))

    row, vec = _rowspec(tr, d), _vecspec(d)
    gspec = pl.BlockSpec((SUBLANES, d), lambda i: (0, 0))
    dx, d2, dg, dg2 = pl.pallas_call(
        body, name=name, grid=(nsteps,), in_specs=[row, vec, row, row, row, vec],
        out_specs=[row, row, gspec, gspec],
        out_shape=[jax.ShapeDtypeStruct((s, d), F32), jax.ShapeDtypeStruct((s, d), BF16),
                   jax.ShapeDtypeStruct((SUBLANES, d), F32), jax.ShapeDtypeStruct((SUBLANES, d), F32)],
        scratch_shapes=[pltpu.VMEM((SUBLANES, d), F32), pltpu.VMEM((SUBLANES, d), F32)],
        compiler_params=_cp(("arbitrary",)))(x, g, dy, res, y2, g2)
    return dx, dg[0:1], d2, dg2[0:1]


def _last_norm_and_loss(name, y, g, res, target):
    s, d = y.shape
    tr = _pick(s, ROW_TILE)
    nsteps = s // tr

    def body(y_ref, g_ref, r_ref, t_ref, dx_ref, dy_ref, dg_ref, l_ref, acc_ref, lacc_ref):
        i = pl.program_id(0)

        @pl.when(i == 0)
        def _():
            acc_ref[...] = jnp.zeros_like(acc_ref)
            lacc_ref[...] = jnp.zeros_like(lacc_ref)

        yv = y_ref[...]
        gv = g_ref[...]
        r = lax.rsqrt(jnp.mean(yv * yv, axis=-1, keepdims=True) + NORM_EPS)
        yhat = yv * r
        err = r_ref[...] + yhat * gv - t_ref[...]
        dx = err * (1.0 / d)
        dx_ref[...] = dx
        lacc_ref[...] += jnp.sum((err * err).reshape(tr // SUBLANES, SUBLANES, d), axis=0)
        gy = dx * gv
        dy_ref[...] = (r * (gy - yhat * jnp.mean(gy * yhat, axis=-1, keepdims=True))).astype(dy_ref.dtype)
        acc_ref[...] += jnp.sum((dx * yhat).reshape(tr // SUBLANES, SUBLANES, d), axis=0)

        @pl.when(i == nsteps - 1)
        def _():
            dg_ref[...] = jnp.broadcast_to(_colsum(acc_ref[...]), (SUBLANES, d))
            l_ref[...] = jnp.full((SUBLANES, LANES), (0.5 / d) * jnp.sum(lacc_ref[...]), F32)

    dx, dy, dg, l = pl.pallas_call(
        body, name=name, grid=(nsteps,),
        in_specs=[_rowspec(tr, d), _vecspec(d), _rowspec(tr, d), _rowspec(tr, d)],
        out_specs=[_rowspec(tr, d), _rowspec(tr, d), pl.BlockSpec((SUBLANES, d), lambda i: (0, 0)),
                   pl.BlockSpec((SUBLANES, LANES), lambda i: (0, 0))],
        out_shape=[jax.ShapeDtypeStruct((s, d), F32), jax.ShapeDtypeStruct((s, d), BF16),
                   jax.ShapeDtypeStruct((SUBLANES, d), F32), jax.ShapeDtypeStruct((SUBLANES, LANES), F32)],
        scratch_shapes=[pltpu.VMEM((SUBLANES, d), F32), pltpu.VMEM((SUBLANES, d), F32)],
        compiler_params=_cp(("arbitrary",)))(y, g, res, target)
    return dx, dy, dg[0:1], l[0, 0]


def _gates(xr, wa, ba, wx, bx, lam):
    xb = xr.astype(BF16)
    r = _sigmoid(jnp.dot(xb, wa, preferred_element_type=F32) + ba)
    i = _sigmoid(jnp.dot(xb, wx, preferred_element_type=F32) + bx)
    log_a = LRU_C * r * _log_sigmoid(lam)
    a = jnp.exp(log_a)
    m = jnp.sqrt(-_expm1(2.0 * log_a))
    return r, i, a, m


def _mixer_fwd(proj, conv_a, conv_b, bias, wa_blk, ba, wx_blk, bx, lam):
    s, w5 = proj.shape
    w = w5 // 5
    nch = w // LANES
    ts = _pick(s, ROW_TILE)
    nt = s // ts

    def body(p_ref, pp_ref, ca_ref, cb_ref, bias_ref, wa_ref, ba_ref, wx_ref, bx_ref, lam_ref,
             y_ref, h_ref, a_scr, b_scr, hc_scr):
        t = pl.program_id(0)
        first = t == 0
        rows = lax.broadcasted_iota(jnp.int32, (ts, LANES), 0)

        @pl.when(first)
        def _():
            hc_scr[...] = jnp.zeros_like(hc_scr)

        def cur(comp, c):
            return p_ref[:, comp * w + c * LANES:comp * w + (c + 1) * LANES]

        def prev(comp, c):
            v = pp_ref[:, comp * w + c * LANES:comp * w + (c + 1) * LANES]
            return jnp.where(first, 0.0, v)

        for c in range(nch):
            sl = slice(c * LANES, (c + 1) * LANES)
            cx = cur(1, c) * cur(2, c)
            cxp = prev(1, c) * prev(2, c)
            wa3 = ca_ref[:, sl]
            conv = (wa3[2:3] * cx + wa3[1:2] * _shift_down(cx, cxp, 1, rows)
                    + wa3[0:1] * _shift_down(cx, cxp, 2, rows))
            y_ref[:, sl] = (cur(0, c) * conv).astype(BF16)

        for c in range(nch):
            sl = slice(c * LANES, (c + 1) * LANES)
            xb, xbp = cur(4, c), prev(4, c)
            wb4 = cb_ref[:, sl]
            xr = (wb4[3:4] * xb + wb4[2:3] * _shift_down(xb, xbp, 1, rows)
                  + wb4[1:2] * _shift_down(xb, xbp, 2, rows)
                  + wb4[0:1] * _shift_down(xb, xbp, 3, rows) + bias_ref[:, sl])
            _, i, a, m = _gates(xr, wa_ref[c], ba_ref[:, sl], wx_ref[c], bx_ref[:, sl], lam_ref[:, sl])
            a_scr[:, sl] = a
            b_scr[:, sl] = m * i * xr

        def step(r, h):
            h = a_scr[pl.ds(r, 1), :] * h + b_scr[pl.ds(r, 1), :]
            h_ref[pl.ds(r, 1), :] = h
            return h

        hc_scr[0:1, :] = lax.fori_loop(0, ts, step, hc_scr[0:1, :], unroll=8)

        for c in range(nch):
            sl = slice(c * LANES, (c + 1) * LANES)
            gel, _ = _gelu_and_grad(cur(3, c))
            y_ref[:, w + c * LANES:w + (c + 1) * LANES] = (h_ref[:, sl] * gel).astype(BF16)

    vec = lambda n: _whole((n, w))
    return pl.pallas_call(
        body, name="mixer_fwd", grid=(nt,),
        in_specs=[pl.BlockSpec((ts, w5), lambda t: (t, 0)),
                  pl.BlockSpec((SUBLANES, w5), lambda t: (jnp.maximum(t * (ts // SUBLANES) - 1, 0), 0)),
                  vec(3), vec(4), vec(1), _whole(wa_blk.shape), vec(1), _whole(wx_blk.shape), vec(1), vec(1)],
        out_specs=[pl.BlockSpec((ts, 2 * w), lambda t: (t, 0)), pl.BlockSpec((ts, w), lambda t: (t, 0))],
        out_shape=[jax.ShapeDtypeStruct((s, 2 * w), BF16), jax.ShapeDtypeStruct((s, w), F32)],
        scratch_shapes=[pltpu.VMEM((ts, w), F32), pltpu.VMEM((ts, w), F32), pltpu.VMEM((SUBLANES, w), F32)],
        compiler_params=_cp(("arbitrary",)),
    )(proj, proj, conv_a, conv_b, bias, wa_blk, ba, wx_blk, bx, lam)


_SG_CONV_A, _SG_CONV_B, _SG_BIAS, _SG_BA, _SG_BX, _SG_LAM, _SG_ROWS = 0, 3, 7, 8, 9, 10, 16


def _mixer_bwd(proj, hseq, dy, conv_a, conv_b, bias, wa_blk, ba, wx_blk, bx, lam):
    s, w5 = proj.shape
    w = w5 // 5
    nch = w // LANES
    ts = _pick(s, ROW_TILE)
    nt = s // ts
    tpb = ts // SUBLANES

    def body(p_ref, pp_ref, h_ref, hp_ref, dy_ref, ca_ref, cb_ref, bias_ref, wa_ref, ba_ref, wx_ref, bx_ref,
             lam_ref, dp_ref, xr_ref, dpa_ref, dpx_ref, sg_ref,
             a_scr, g_scr, l_scr, x_scr, r_scr, i_scr, m_scr, cl_scr, cdc_scr, cdx_scr):
        pid = pl.program_id(0)
        last = pid == 0
        first = pid == nt - 1
        rows = lax.broadcasted_iota(jnp.int32, (ts, LANES), 0)

        @pl.when(last)
        def _():
            sg_ref[...] = jnp.zeros_like(sg_ref)
            cl_scr[...] = jnp.zeros_like(cl_scr)
            cdc_scr[...] = jnp.zeros_like(cdc_scr)
            cdx_scr[...] = jnp.zeros_like(cdx_scr)

        def cur(comp, c):
            return p_ref[:, comp * w + c * LANES:comp * w + (c + 1) * LANES]

        def prev(comp, c):
            v = pp_ref[:, comp * w + c * LANES:comp * w + (c + 1) * LANES]
            return jnp.where(first, 0.0, v)

        def put(comp, c, v):
            dp_ref[:, comp * w + c * LANES:comp * w + (c + 1) * LANES] = v.astype(dp_ref.dtype)

        def acc(row, sl, v):
            sg_ref[row:row + 1, sl] += _colsum(v)

        for c in range(nch):
            sl = slice(c * LANES, (c + 1) * LANES)
            bg, cg, ax = cur(0, c), cur(1, c), cur(2, c)
            cx = cg * ax
            cxp = prev(1, c) * prev(2, c)
            cx1 = _shift_down(cx, cxp, 1, rows)
            cx2 = _shift_down(cx, cxp, 2, rows)
            wa3 = ca_ref[:, sl]
            conv = wa3[2:3] * cx + wa3[1:2] * cx1 + wa3[0:1] * cx2
            dya = dy_ref[:, sl]
            put(0, c, dya * conv)
            dconv = dya * bg
            nxt = cdc_scr[:, sl]
            dcx = (wa3[2:3] * dconv + wa3[1:2] * _shift_up(dconv, nxt, 1, rows)
                   + wa3[0:1] * _shift_up(dconv, nxt, 2, rows))
            cdc_scr[:, sl] = dconv[0:SUBLANES]
            put(1, c, dcx * ax)
            put(2, c, dcx * cg)
            acc(_SG_CONV_A + 2, sl, dconv * cx)
            acc(_SG_CONV_A + 1, sl, dconv * cx1)
            acc(_SG_CONV_A + 0, sl, dconv * cx2)

        for c in range(nch):
            sl = slice(c * LANES, (c + 1) * LANES)
            xb, xbp = cur(4, c), prev(4, c)
            wb4 = cb_ref[:, sl]
            xr = (wb4[3:4] * xb + wb4[2:3] * _shift_down(xb, xbp, 1, rows)
                  + wb4[1:2] * _shift_down(xb, xbp, 2, rows)
                  + wb4[0:1] * _shift_down(xb, xbp, 3, rows) + bias_ref[:, sl])
            r, i, a, m = _gates(xr, wa_ref[c], ba_ref[:, sl], wx_ref[c], bx_ref[:, sl], lam_ref[:, sl])
            gel, dgel = _gelu_and_grad(cur(3, c))
            dyb = dy_ref[:, w + c * LANES:w + (c + 1) * LANES]
            put(3, c, dyb * h_ref[:, sl] * dgel)
            g_scr[:, sl] = dyb * gel
            a_scr[:, sl] = a
            x_scr[:, sl] = xr
            r_scr[:, sl] = r
            i_scr[:, sl] = i
            m_scr[:, sl] = m

        def step(j, carry):
            r = ts - 1 - j
            lam_t = g_scr[pl.ds(r, 1), :] + carry
            l_scr[pl.ds(r, 1), :] = lam_t
            return a_scr[pl.ds(r, 1), :] * lam_t

        cl_scr[0:1, :] = lax.fori_loop(0, ts, step, cl_scr[0:1, :], unroll=8)

        for c in range(nch):
            sl = slice(c * LANES, (c + 1) * LANES)
            lam_t = l_scr[:, sl]
            hprev = _shift_down(h_ref[:, sl], jnp.where(first, 0.0, hp_ref[:, sl]), 1, rows)
            xr, r, i, m, a = x_scr[:, sl], r_scr[:, sl], i_scr[:, sl], m_scr[:, sl], a_scr[:, sl]
            da = lam_t * hprev
            dm = lam_t * i * xr
            di = lam_t * m * xr
            dxr = lam_t * m * i
            dlog_a = da * a - dm * a * a / m
            lam_p = lam_ref[:, sl]
            dr = dlog_a * (LRU_C * _log_sigmoid(lam_p))
            acc(_SG_LAM, sl, dlog_a * r * (LRU_C * _sigmoid(-lam_p)))
            dpa = dr * r * (1.0 - r)
            dpx = di * i * (1.0 - i)
            dpa_b, dpx_b = dpa.astype(BF16), dpx.astype(BF16)
            dxr = (dxr + lax.dot_general(dpa_b, wa_ref[c], _DIMS["nt"], preferred_element_type=F32)
                   + lax.dot_general(dpx_b, wx_ref[c], _DIMS["nt"], preferred_element_type=F32))
            xr_ref[:, sl] = xr.astype(BF16)
            dpa_ref[:, sl] = dpa_b
            dpx_ref[:, sl] = dpx_b
            acc(_SG_BA, sl, dpa)
            acc(_SG_BX, sl, dpx)
            acc(_SG_BIAS, sl, dxr)
            nxt = cdx_scr[:, sl]
            wb4 = cb_ref[:, sl]
            put(4, c, wb4[3:4] * dxr + wb4[2:3] * _shift_up(dxr, nxt, 1, rows)
                + wb4[1:2] * _shift_up(dxr, nxt, 2, rows) + wb4[0:1] * _shift_up(dxr, nxt, 3, rows))
            cdx_scr[:, sl] = dxr[0:SUBLANES]
            xb, xbp = cur(4, c), prev(4, c)
            acc(_SG_CONV_B + 3, sl, dxr * xb)
            acc(_SG_CONV_B + 2, sl, dxr * _shift_down(xb, xbp, 1, rows))
            acc(_SG_CONV_B + 1, sl, dxr * _shift_down(xb, xbp, 2, rows))
            acc(_SG_CONV_B + 0, sl, dxr * _shift_down(xb, xbp, 3, rows))

    blk = lambda width: pl.BlockSpec((ts, width), lambda p: (nt - 1 - p, 0))
    pre = lambda width: pl.BlockSpec(
        (SUBLANES, width), lambda p: (jnp.maximum((nt - 1 - p) * tpb - 1, 0), 0))
    vec = lambda n: _whole((n, w))
    big = lambda: pltpu.VMEM((ts, w), F32)
    small = lambda: pltpu.VMEM((SUBLANES, w), F32)
    return pl.pallas_call(
        body, name="mixer_bwd", grid=(nt,),
        in_specs=[blk(w5), pre(w5), blk(w), pre(w), blk(2 * w),
                  vec(3), vec(4), vec(1), _whole(wa_blk.shape), vec(1), _whole(wx_blk.shape), vec(1), vec(1)],
        out_specs=[blk(w5), blk(w), blk(w), blk(w), _whole((_SG_ROWS, w))],
        out_shape=[jax.ShapeDtypeStruct((s, w5), BF16), jax.ShapeDtypeStruct((s, w), BF16),
                   jax.ShapeDtypeStruct((s, w), BF16), jax.ShapeDtypeStruct((s, w), BF16),
                   jax.ShapeDtypeStruct((_SG_ROWS, w), F32)],
        scratch_shapes=[big(), big(), big(), big(), big(), big(), big(), small(), small(), small()],
        compiler_params=_cp(("arbitrary",)),
    )(proj, proj, hseq, hseq, dy, conv_a, conv_b, bias, wa_blk, ba, wx_blk, bx, lam)


def _split_dot(v, tri2):
    hi = v.astype(BF16)
    lo = (v - hi.astype(F32)).astype(BF16)
    return jnp.dot(jnp.concatenate([hi, lo], axis=1), tri2, preferred_element_type=F32)


def _tri(cmp):
    r = lax.broadcasted_iota(jnp.int32, (LANES, LANES), 0)
    c = lax.broadcasted_iota(jnp.int32, (LANES, LANES), 1)
    return cmp(r, c).astype(BF16)


def _lane_blocks(v):
    return [v[:, b * LANES:(b + 1) * LANES] for b in range(v.shape[1] // LANES)]


def _last_lane(v):
    return jnp.broadcast_to(v[:, LANES - 1:LANES], v.shape)


def _scores(q, kb, scale):
    return lax.dot_general(q, kb, _DIMS["nt"], preferred_element_type=F32) * scale


def _log_gates(z, diagonal):
    ls = jnp.minimum(z, 0.0) - jnp.log(1.0 + jnp.exp(-jnp.abs(z)))
    ln = ls - z
    valid = None
    if diagonal:
        valid = (lax.broadcasted_iota(jnp.int32, z.shape, 1) < lax.broadcasted_iota(jnp.int32, z.shape, 0))
        ln = jnp.where(valid, ln, 0.0)
    return ls, ln, valid


def _attn_fwd(qkv, heads):
    s = qkv.shape[0]
    dh = LANES
    tq = _pick(s, ATT_TILE)
    nq = s // tq
    nb = tq // LANES
    scale = 1.0 / math.sqrt(dh)

    hp = ATT_HEADS_PER_STEP
    groups = heads // hp
    wid = hp * dh

    def body(q_ref, k_ref, v_ref, o_ref, tot_ref, acc_scr, car_scr):
        qi = pl.program_id(1)
        acc_scr[...] = jnp.zeros_like(acc_scr)
        car_scr[...] = jnp.zeros_like(car_scr)
        tri = _tri(lambda r, c: r > c)
        tri = jnp.concatenate([tri, tri], axis=0)

        def tile(kt, diagonal):
            k0 = pl.multiple_of(kt * tq, tq)
            heads_cols = [slice(hh * dh, (hh + 1) * dh) for hh in range(hp)]
            zs = [_scores(q_ref[:, cols], k_ref[pl.ds(k0, tq), cols], scale) for cols in heads_cols]
            gates = [_log_gates(z, diagonal) for z in zs]
            sfxs = [_split_dot(jnp.concatenate(_lane_blocks(ln), axis=0), tri) for _, ln, _ in gates]
            for cols, (ls, ln, valid), sfx in zip(heads_cols, gates, sfxs):
                blocks = _lane_blocks(ln)
                car = car_scr[:, cols]
                parts = [None] * nb
                for b in reversed(range(nb)):
                    sb = sfx[b * tq:(b + 1) * tq]
                    parts[b] = sb + car
                    car = car + (sb[:, 0:1] + blocks[b][:, 0:1])
                car_scr[:, cols] = car
                wgt = jnp.exp(ls + jnp.concatenate(parts, axis=1))
                if diagonal:
                    wgt = jnp.where(valid, wgt, 0.0)
                acc_scr[:, cols] += jnp.dot(
                    wgt.astype(BF16), v_ref[pl.ds(k0, tq), cols], preferred_element_type=F32)

        tile(qi, True)

        def step(j, carry):
            tile(qi - 1 - j, False)
            return carry

        lax.fori_loop(0, qi, step, 0)
        o_ref[...] = acc_scr[...].astype(BF16)
        tot_ref[...] = car_scr[...]

    return pl.pallas_call(
        body, name="attn_fwd", grid=(groups, nq),
        in_specs=[pl.BlockSpec((tq, wid), lambda h, i: (i, h)),
                  pl.BlockSpec((s, wid), lambda h, i: (0, groups + h)),
                  pl.BlockSpec((s, wid), lambda h, i: (0, 2 * groups + h))],
        out_specs=[pl.BlockSpec((tq, wid), lambda h, i: (i, h)), pl.BlockSpec((tq, wid), lambda h, i: (i, h))],
        out_shape=[jax.ShapeDtypeStruct((s, heads * dh), BF16), jax.ShapeDtypeStruct((s, heads * dh), F32)],
        scratch_shapes=[pltpu.VMEM((tq, wid), F32), pltpu.VMEM((tq, wid), F32)],
        compiler_params=_cp(("parallel", "arbitrary")),
    )(qkv, qkv, qkv)


def _attn_bwd(qkv, tot, do, heads):
    s = qkv.shape[0]
    dh = LANES
    tq = _pick(s, ATT_TILE)
    nq = s // tq
    nb = tq // LANES
    scale = 1.0 / math.sqrt(dh)

    hp = ATT_HEADS_PER_STEP
    groups = heads // hp
    wid = hp * dh

    def body(q_ref, k_ref, v_ref, tot_ref, do_ref, dq_ref, dk_ref, dv_ref,
             dq_scr, dk_scr, dv_scr, cl_scr, cg_scr):
        qi = pl.program_id(1)

        @pl.when(qi == 0)
        def _():
            dk_scr[...] = jnp.zeros_like(dk_scr)
            dv_scr[...] = jnp.zeros_like(dv_scr)

        dq_scr[...] = jnp.zeros_like(dq_scr)
        cl_scr[...] = jnp.zeros_like(cl_scr)
        cg_scr[...] = jnp.zeros_like(cg_scr)
        tri_le = _tri(lambda r, c: r <= c)
        tri_le = jnp.concatenate([tri_le, tri_le], axis=0)
        tri_lt = _tri(lambda r, c: r < c)

        def tile(kt, diagonal):
            k0 = pl.multiple_of(kt * tq, tq)
            heads_cols = [slice(hh * dh, (hh + 1) * dh) for hh in range(hp)]
            keys = pl.ds(k0, tq)
            zs = [_scores(q_ref[:, cols], k_ref[keys, cols], scale) for cols in heads_cols]
            dws = [lax.dot_general(do_ref[:, cols], v_ref[keys, cols], _DIMS["nt"], preferred_element_type=F32)
                   for cols in heads_cols]
            gates = [_log_gates(z, diagonal) for z in zs]
            pins = [_split_dot(jnp.concatenate(_lane_blocks(ln), axis=0), tri_le) for _, ln, _ in gates]
            wgts, gs = [], []
            for cols, (ls, _, valid), pin, dw in zip(heads_cols, gates, pins, dws):
                total = tot_ref[:, cols]
                cl = cl_scr[:, cols]
                parts = []
                for b in range(nb):
                    pb = pin[b * tq:(b + 1) * tq] + cl
                    parts.append(total - pb)
                    cl = _last_lane(pb)
                cl_scr[:, cols] = cl
                wgt = jnp.exp(ls + jnp.concatenate(parts, axis=1))
                if diagonal:
                    wgt = jnp.where(valid, wgt, 0.0)
                wgts.append(wgt)
                gs.append(wgt * dw)
            pexs = [jnp.dot(jnp.concatenate(_lane_blocks(g), axis=0).astype(BF16), tri_lt,
                            preferred_element_type=F32) for g in gs]
            for cols, wgt in zip(heads_cols, wgts):
                dv_scr[keys, cols] += lax.dot_general(
                    wgt.astype(BF16), do_ref[:, cols], _DIMS["tn"], preferred_element_type=F32)
            for cols, (ls, _, valid), g, pex in zip(heads_cols, gates, gs, pexs):
                gblocks = _lane_blocks(g)
                cg = cg_scr[:, cols]
                parts = []
                for b in range(nb):
                    pb = pex[b * tq:(b + 1) * tq] + cg
                    parts.append(pb)
                    cg = _last_lane(pb + gblocks[b])
                cg_scr[:, cols] = cg
                dz = g - jnp.exp(ls) * (g + jnp.concatenate(parts, axis=1))
                if diagonal:
                    dz = jnp.where(valid, dz, 0.0)
                dz = dz.astype(BF16)
                dq_scr[:, cols] += jnp.dot(dz, k_ref[keys, cols], preferred_element_type=F32)
                dk_scr[keys, cols] += lax.dot_general(
                    dz, q_ref[:, cols], _DIMS["tn"], preferred_element_type=F32)

        def step(j, carry):
            tile(j, False)
            return carry

        lax.fori_loop(0, qi, step, 0)
        tile(qi, True)
        dq_ref[...] = (dq_scr[...] * scale).astype(BF16)

        @pl.when(qi == nq - 1)
        def _():
            dk_ref[...] = (dk_scr[...] * scale).astype(BF16)
            dv_ref[...] = dv_scr[...].astype(BF16)

    qblk = pl.BlockSpec((tq, wid), lambda h, i: (i, h))
    hblk = pl.BlockSpec((s, wid), lambda h, i: (0, h))
    out = jax.ShapeDtypeStruct((s, heads * dh), BF16)
    return pl.pallas_call(
        body, name="attn_bwd", grid=(groups, nq),
        in_specs=[qblk, pl.BlockSpec((s, wid), lambda h, i: (0, groups + h)),
                  pl.BlockSpec((s, wid), lambda h, i: (0, 2 * groups + h)), qblk, qblk],
        out_specs=[qblk, hblk, hblk], out_shape=[out, out, out],
        scratch_shapes=[pltpu.VMEM((tq, wid), F32), pltpu.VMEM((s, wid), F32), pltpu.VMEM((s, wid), F32),
                        pltpu.VMEM((tq, wid), F32), pltpu.VMEM((tq, wid), F32)],
        compiler_params=_cp(("parallel", "arbitrary")),
    )(qkv, qkv, qkv, tot, do)


def _place():
    x, y, c = lax.axis_index("x"), lax.axis_index("y"), lax.axis_index("c")
    chips = [(1 - x, y), (x, 1 - y), (1 - x, 1 - y)]
    return x, y, c, chips


def _remote(src, dst, send_sem, recv_sem, dev):
    return pltpu.make_async_remote_copy(
        src_ref=src, dst_ref=dst, send_sem=send_sem, recv_sem=recv_sem, device_id=dev, device_id_type=MESH)


def _handshake(peers):
    barrier = pltpu.get_barrier_semaphore()
    for dev in peers:
        pl.semaphore_signal(barrier, inc=1, device_id=dev, device_id_type=MESH)
    pl.semaphore_wait(barrier, len(peers))


def _sequencer_kernel(name, n_sems, collective_id):
    return functools.partial(
        pl.kernel, mesh=plsc.ScalarSubcoreMesh(axis_name="seq", num_cores=1), name=name,
        scratch_types=(pltpu.SemaphoreType.DMA,) * n_sems,
        compiler_params=pltpu.CompilerParams(collective_id=collective_id))


def _allgather_async(name, slot_buf, collective_id):
    buf = jax.new_ref(slot_buf, memory_space=pltpu.MemorySpace.HBM)
    hr = slot_buf.shape[1] // 2

    @_sequencer_kernel(name, 12, collective_id)
    def launch(*sems):
        send_sems, recv_sems, fsend_sems, frecv_sems = sems[0:3], sems[3:6], sems[6:9], sems[9:12]
        x, y, c, chips = _place()
        me = 2 * x + y
        sibling = (x, y, 1 - c)
        _handshake([(px, py, c) for px, py in chips] + [sibling])
        mine = buf.at[me, pl.ds(c * hr, hr)]
        firsts = []
        for k, (px, py) in enumerate(chips):
            cp = _remote(mine, mine, send_sems[k], recv_sems[k], (px, py, c))
            cp.start()
            firsts.append(cp)
        passed = []
        for k, (px, py) in enumerate(chips):
            slot = buf.at[2 * px + py, pl.ds(c * hr, hr)]
            _remote(slot, slot, send_sems[k], recv_sems[k], (px, py, c)).wait_recv()
            cp = _remote(slot, slot, fsend_sems[k], frecv_sems[k], sibling)
            cp.start()
            passed.append(cp)
        for k, (px, py) in enumerate(chips):
            slot = buf.at[2 * px + py, pl.ds((1 - c) * hr, hr)]
            _remote(slot, slot, fsend_sems[k], frecv_sems[k], sibling).wait_recv()
        for cp in firsts + passed:
            cp.wait_send()

    launch()
    return buf[...]


def _to_sibling_async(name, slab):
    src = jax.new_ref(slab, memory_space=pltpu.MemorySpace.HBM)
    hr = slab.shape[1] // 2
    got = jax.empty_ref(jax.ShapeDtypeStruct((N_CHIPS, hr, slab.shape[2]), slab.dtype),
                        memory_space=pltpu.MemorySpace.HBM)

    @_sequencer_kernel(name, 2, COLLECTIVE_SIBLING)
    def launch(send_sem, recv_sem):
        x, y, c, _ = _place()
        _handshake([(x, y, 1 - c)])
        _remote(src.at[:, pl.ds((1 - c) * hr, hr), :], got, send_sem, recv_sem, (x, y, 1 - c)).start()
        _remote(got, got, send_sem, recv_sem, (x, y, 1 - c)).wait()

    launch()
    return src[...], got[...]


def _swap_with_sibling_async(name, part):
    src = jax.new_ref(part, memory_space=pltpu.MemorySpace.HBM)
    got = jax.empty_ref(jax.ShapeDtypeStruct(part.shape, part.dtype), memory_space=pltpu.MemorySpace.HBM)

    @_sequencer_kernel(name, 2, COLLECTIVE_SIBLING)
    def launch(send_sem, recv_sem):
        x, y, c, _ = _place()
        _handshake([(x, y, 1 - c)])
        cp = _remote(src, got, send_sem, recv_sem, (x, y, 1 - c))
        cp.start()
        cp.wait()

    launch()
    return got[...]


def _to_chips_async(name, part):
    src = jax.new_ref(part, memory_space=pltpu.MemorySpace.HBM)
    got = jax.empty_ref(jax.ShapeDtypeStruct((3,) + part.shape[1:], part.dtype), memory_space=pltpu.MemorySpace.HBM)

    @_sequencer_kernel(name, 6, COLLECTIVE_CHIPS)
    def launch(*sems):
        send_sems, recv_sems = sems[0:3], sems[3:6]
        x, y, c, chips = _place()
        _handshake([(px, py, c) for px, py in chips])
        cps = []
        for k, (px, py) in enumerate(chips):
            cp = _remote(src.at[2 * px + py], got.at[k], send_sems[k], recv_sems[k], (px, py, c))
            cp.start()
            cps.append(cp)
        for cp in cps:
            cp.wait()

    launch()
    return src[...], got[...]


def _join_sibling_async(name, half_filled):
    buf = jax.new_ref(half_filled, memory_space=pltpu.MemorySpace.HBM)
    hr = half_filled.shape[0] // 2

    @_sequencer_kernel(name, 2, COLLECTIVE_SIBLING)
    def launch(send_sem, recv_sem):
        x, y, c, _ = _place()
        _handshake([(x, y, 1 - c)])
        mine = buf.at[pl.ds(c * hr, hr)]
        other = buf.at[pl.ds((1 - c) * hr, hr)]
        cp = _remote(mine, mine, send_sem, recv_sem, (x, y, 1 - c))
        cp.start()
        _remote(other, other, send_sem, recv_sem, (x, y, 1 - c)).wait_recv()
        cp.wait_send()

    launch()
    return buf[...]


def _allgather_chips_small(name, v):
    r = v.shape[0]

    def body(v_ref, o_ref, send_sems, recv_sems):
        x, y, c, chips = _place()
        me = 2 * x + y
        o_ref[me] = v_ref[...]
        cps = []
        for k, (px, py) in enumerate(chips):
            cp = _remote(v_ref, o_ref.at[me], send_sems.at[k], recv_sems.at[k], (px, py, c))
            cp.start()
            cps.append(cp)
        for k, (px, py) in enumerate(chips):
            slot = o_ref.at[2 * px + py]
            _remote(slot, slot, send_sems.at[k], recv_sems.at[k], (px, py, c)).wait_recv()
        for cp in cps:
            cp.wait_send()

    return pl.pallas_call(
        body, name=name, in_specs=[pl.BlockSpec(memory_space=pltpu.VMEM)],
        out_specs=pl.BlockSpec(memory_space=pltpu.VMEM),
        out_shape=jax.ShapeDtypeStruct((N_CHIPS, r, LANES), F32),
        scratch_shapes=[pltpu.SemaphoreType.DMA((3,)), pltpu.SemaphoreType.DMA((3,))],
    )(v)


def _allreduce_small(name, v):
    r = v.shape[0]
    hr = r // 2
    assert hr % SUBLANES == 0

    def body(v_ref, o_ref, sib_ref, chips_ref, send_sems, recv_sems):
        x, y, c, chips = _place()
        me = 2 * x + y
        sibling = (x, y, 1 - c)
        first = _remote(v_ref, sib_ref, send_sems.at[0], recv_sems.at[0], sibling)
        first.start()
        first.wait()
        mine = pl.ds(pl.multiple_of(c * hr, SUBLANES), hr)
        chips_ref[me] = v_ref[mine, :] + sib_ref[mine, :]
        cps = []
        for k, (px, py) in enumerate(chips):
            cp = _remote(chips_ref.at[me], chips_ref.at[me], send_sems.at[1 + k], recv_sems.at[1 + k], (px, py, c))
            cp.start()
            cps.append(cp)
        for k, (px, py) in enumerate(chips):
            slot = chips_ref.at[2 * px + py]
            _remote(slot, slot, send_sems.at[1 + k], recv_sems.at[1 + k], (px, py, c)).wait_recv()
        total = chips_ref[0]
        for j in range(1, N_CHIPS):
            total = total + chips_ref[j]
        o_ref[mine, :] = total
        last = _remote(o_ref.at[mine], o_ref.at[mine], send_sems.at[4], recv_sems.at[4], sibling)
        last.start()
        other = o_ref.at[pl.ds(pl.multiple_of((1 - c) * hr, SUBLANES), hr)]
        _remote(other, other, send_sems.at[4], recv_sems.at[4], sibling).wait_recv()
        last.wait_send()
        for cp in cps:
            cp.wait_send()

    return pl.pallas_call(
        body, name=name, in_specs=[pl.BlockSpec(memory_space=pltpu.VMEM)],
        out_specs=pl.BlockSpec(memory_space=pltpu.VMEM),
        out_shape=jax.ShapeDtypeStruct((r, LANES), F32),
        scratch_shapes=[pltpu.VMEM((r, LANES), F32), pltpu.VMEM((N_CHIPS, hr, LANES), F32),
                        pltpu.SemaphoreType.DMA((5,)), pltpu.SemaphoreType.DMA((5,))],
    )(v)


def _add_sibling(name, slabs, recv, c):
    _, r, cols = slabs.shape
    hr = r // 2
    tr = _pick(hr, STREAM_TILE)
    nb = hr // tr

    def body(c_ref, a_ref, b_ref, o_ref):
        o_ref[...] = (a_ref[...].astype(F32) + b_ref[...].astype(F32)).astype(BF16)

    grid_spec = pltpu.PrefetchScalarGridSpec(
        num_scalar_prefetch=1, grid=(N_CHIPS, nb),
        in_specs=[pl.BlockSpec((None, tr, cols), lambda j, i, c_ref: (j, c_ref[0] * nb + i, 0)),
                  pl.BlockSpec((None, tr, cols), lambda j, i, c_ref: (j, i, 0))],
        out_specs=pl.BlockSpec((None, tr, cols), lambda j, i, c_ref: (j, i, 0)))
    return pl.pallas_call(
        body, name=name, grid_spec=grid_spec,
        out_shape=jax.ShapeDtypeStruct((N_CHIPS, hr, cols), BF16),
        compiler_params=_cp(("parallel", "parallel")))(jnp.reshape(c, (1,)).astype(jnp.int32), slabs, recv)


def _sum_chips(name, own, recv, chip, c):
    _, hr, cols = recv.shape
    tr = _pick(hr, STREAM_TILE // 2)
    nb = hr // tr

    def body(sc_ref, own_ref, recv_ref, o_ref):
        total = own_ref[...].astype(F32)
        for k in range(3):
            total = total + recv_ref[k].astype(F32)
        o_ref[...] = total

    grid_spec = pltpu.PrefetchScalarGridSpec(
        num_scalar_prefetch=1, grid=(nb,),
        in_specs=[pl.BlockSpec((None, tr, cols), lambda i, sc: (sc[0], i, 0)),
                  pl.BlockSpec((3, tr, cols), lambda i, sc: (0, i, 0))],
        out_specs=pl.BlockSpec((tr, cols), lambda i, sc: (sc[1] * nb + i, 0)))
    return pl.pallas_call(
        body, name=name, grid_spec=grid_spec, out_shape=jax.ShapeDtypeStruct((2 * hr, cols), F32),
        compiler_params=_cp(("parallel",)))(jnp.stack([chip, c]).astype(jnp.int32), own, recv)


def _adamw_math(w, g, m, v):
    m = ADAM_B1 * m + (1.0 - ADAM_B1) * g
    v = ADAM_B2 * v + (1.0 - ADAM_B2) * (g * g)
    m_hat = m / (1.0 - ADAM_B1 ** ADAM_STEP)
    v_hat = v / (1.0 - ADAM_B2 ** ADAM_STEP)
    delta = -ADAM_LR * (m_hat / (jnp.sqrt(v_hat) + ADAM_EPS) + ADAM_WD * w)
    return delta, m, v


def _adamw(name, w, gs, m, v):
    nl, r, cols = w.shape
    tr = _pick(r, ROW_TILE)

    def body(*refs):
        w_ref, m_ref, v_ref = refs[0:3]
        g_refs = refs[3:3 + nl]
        go_ref, d_ref, nm_ref, nv_ref = refs[3 + nl:]
        layer = pl.program_id(0)
        g = g_refs[0][...]
        for j in range(1, nl):
            g = jnp.where(layer == j, g_refs[j][...], g)
        d, nm, nv = _adamw_math(w_ref[...], g, m_ref[...], v_ref[...])
        go_ref[...] = g
        d_ref[...] = d
        nm_ref[...] = nm
        nv_ref[...] = nv

    spec3 = pl.BlockSpec((None, tr, cols), lambda l, i: (l, i, 0))
    gspec = pl.BlockSpec((tr, cols), lambda l, i: (i, 0))
    out = jax.ShapeDtypeStruct((nl, r, cols), F32)
    return pl.pallas_call(
        body, name=name, grid=(nl, r // tr), in_specs=[spec3] * 3 + [gspec] * nl, out_specs=[spec3] * 4,
        out_shape=[out] * 4, compiler_params=_cp(("parallel", "parallel")))(w, m, v, *gs)


def _adamw_small(name, groups):
    n = len(groups)
    flat = [a for grp in groups for a in grp]

    def body(*refs):
        ins, outs = refs[:4 * n], refs[4 * n:]
        for p in range(n):
            w_ref, g_ref, m_ref, v_ref = ins[4 * p:4 * p + 4]
            d, nm, nv = _adamw_math(w_ref[...], g_ref[...], m_ref[...], v_ref[...])
            outs[3 * p][...] = d
            outs[3 * p + 1][...] = nm
            outs[3 * p + 2][...] = nv

    vm = pl.BlockSpec(memory_space=pltpu.VMEM)
    out_shape = [jax.ShapeDtypeStruct(grp[0].shape, F32) for grp in groups for _ in range(3)]
    res = pl.pallas_call(
        body, name=name, in_specs=[vm] * (4 * n), out_specs=[vm] * (3 * n), out_shape=out_shape)(*flat)
    return [tuple(res[3 * p:3 * p + 3]) for p in range(n)]


def _block_diag_pairs(w):
    h, d, _ = w.shape
    z = jnp.zeros((h // 2, d, d), w.dtype)
    top = jnp.concatenate([w[0::2], z], axis=2)
    bot = jnp.concatenate([z, w[1::2]], axis=2)
    return jnp.concatenate([top, bot], axis=1).astype(BF16)


def _diag_pairs_to_heads(g, d):
    a = g[:, :d, :d]
    b = g[:, d:, d:]
    return jnp.stack([a, b], axis=1).reshape(-1, d, d)


def _rows128(a):
    flat = a.reshape(-1, LANES)
    pad = (-flat.shape[0]) % SUBLANES
    if pad:
        flat = jnp.concatenate([flat, jnp.zeros((pad, LANES), flat.dtype)], axis=0)
    return flat


def _unshard_last(g4, shape):
    g4 = g4.reshape((N_CHIPS,) + tuple(shape))
    return jnp.concatenate([g4[j] for j in range(N_CHIPS)], axis=-1)


def kernel(x, norm_gains, hyb_w_in, hyb_conv_a, hyb_conv_b, hyb_conv_b_bias, hyb_rg_w_a, hyb_rg_b_a, hyb_rg_w_x, hyb_rg_b_x, hyb_rg_lambda, hyb_w_out, sb_w_qkv, sb_w_o, mlp_w_up, mlp_w_down, loss_target, m_norm_gains, m_hyb_w_in, m_hyb_conv_a, m_hyb_conv_b, m_hyb_conv_b_bias, m_hyb_rg_w_a, m_hyb_rg_b_a, m_hyb_rg_w_x, m_hyb_rg_b_x, m_hyb_rg_lambda, m_hyb_w_out, m_sb_w_qkv, m_sb_w_o, m_mlp_w_up, m_mlp_w_down, v_norm_gains, v_hyb_w_in, v_hyb_conv_a, v_hyb_conv_b, v_hyb_conv_b_bias, v_hyb_rg_w_a, v_hyb_rg_b_a, v_hyb_rg_w_x, v_hyb_rg_b_x, v_hyb_rg_lambda, v_hyb_w_out, v_sb_w_qkv, v_sb_w_o, v_mlp_w_up, v_mlp_w_down):
    cx_ = lax.axis_index("x")
    cy_ = lax.axis_index("y")
    cc_ = lax.axis_index("c")
    chip = 2 * cx_ + cy_

    x0 = x[0]
    target = loss_target[0]
    s, d = x0.shape
    heads = SB_HEADS
    assert d // heads == LANES
    n_rg, hd = hyb_rg_w_a.shape[1], hyb_rg_w_a.shape[2]
    wmix = n_rg * hd
    assert 2 * hd == LANES

    big = {
        "hyb_w_in": (hyb_w_in, 0), "hyb_w_out": (hyb_w_out, 0), "mlp_w_up0": (mlp_w_up, 0),
        "mlp_w_down0": (mlp_w_down, 0), "sb_w_qkv": (sb_w_qkv, 0), "sb_w_o": (sb_w_o, 0),
        "mlp_w_up1": (mlp_w_up, 1), "mlp_w_down1": (mlp_w_down, 1),
    }
    names = list(big)
    slots = [_cast_into_slot("cast_" + k, big[k][0], big[k][1], chip) for k in names]
    full = {k: _allgather_async("allgather_" + k, slot, cid) for cid, (k, slot) in enumerate(zip(names, slots))}
    rowsharded = lambda k: full[k].reshape(-1, full[k].shape[2])

    ng_s, ca_s, cb_s = norm_gains.reshape(-1, norm_gains.shape[2]), hyb_conv_a[0], hyb_conv_b[0]
    packed = jnp.concatenate([_rows128(ng_s), _rows128(ca_s), _rows128(cb_s)], axis=0)
    gathered = _allgather_chips_small("allgather_small", packed)
    n0 = ng_s.size // LANES
    n1 = n0 + (-n0) % SUBLANES
    m0 = ca_s.size // LANES
    m1 = m0 + (-m0) % SUBLANES
    k0 = cb_s.size // LANES
    gains = _unshard_last(gathered[:, 0:n0], ng_s.shape).reshape(2, 4, 1, d)
    conv_a = _unshard_last(gathered[:, n1:n1 + m0], ca_s.shape)
    conv_b = _unshard_last(gathered[:, n1 + m1:n1 + m1 + k0], cb_s.shape)
    bias, b_a, b_x, lam = hyb_conv_b_bias, hyb_rg_b_a, hyb_rg_b_x, hyb_rg_lambda
    wa_blk = _block_diag_pairs(hyb_rg_w_a[0])
    wx_blk = _block_diag_pairs(hyb_rg_w_x[0])

    relu_sq = lambda acc: (jnp.maximum(acc, 0.0), jnp.square(jnp.maximum(acc, 0.0)))

    h1 = _rms_fwd("rms_pre0", x0, gains[0, 0])
    proj = _mm_fwd_col("proj_in", h1, full["hyb_w_in"])[0]
    ycat, hseq = _mixer_fwd(proj, conv_a, conv_b, bias, wa_blk, b_a, wx_blk, b_x, lam)
    mix0 = _mm_fwd_row("proj_out", ycat, rowsharded("hyb_w_out"))
    x1, h2 = _rms_post("rms_mix0", mix0, gains[0, 1], x0, gains[0, 2])
    u0, a0 = _mm_fwd_col("mlp_up0", h2, full["mlp_w_up0"], (BF16, BF16), relu_sq)
    mlp0 = _mm_fwd_row("mlp_down0", a0, rowsharded("mlp_w_down0"))
    x2, h3 = _rms_post("rms_mlp0", mlp0, gains[0, 3], x1, gains[1, 0])

    qkv = _mm_fwd_col("qkv", h3, full["sb_w_qkv"], (BF16,))[0]
    att, tot = _attn_fwd(qkv, heads)
    mix1 = _mm_fwd_row("attn_out", att, rowsharded("sb_w_o"))
    x3, h4 = _rms_post("rms_mix1", mix1, gains[1, 1], x2, gains[1, 2])
    u1, a1 = _mm_fwd_col("mlp_up1", h4, full["mlp_w_up1"], (BF16, BF16), relu_sq)
    mlp1 = _mm_fwd_row("mlp_down1", a1, rowsharded("mlp_w_down1"))
    dy, dmlp1, dgain_mlp1, loss_local = _last_norm_and_loss("last_norm_loss", mlp1, gains[1, 3], x3, target)
    loss = lax.psum(loss_local, ("x", "y", "c"))

    dgain = [[None] * 4 for _ in range(2)]
    drelu = lambda acc, u: (acc * (2.0 * u.astype(F32)),)
    stage_a, stage_b, gfull = {}, {}, {}

    def tie(main, side):
        return lax.optimization_barrier((main, side))

    def reduce_start(k, slab, main):
        main, slab = tie(main, slab)
        stage_a[k] = _to_sibling_async("grads_to_sibling_" + k, slab)
        return main

    def reduce_to_chips(k, main):
        slab, from_sibling = stage_a.pop(k)
        main, part = tie(main, _add_sibling("grads_add_" + k, slab, from_sibling, cc_))
        stage_b[k] = _to_chips_async("grads_to_chips_" + k, part)
        return main

    def reduce_split_start(k, act, dy, cs, main):
        main, other = tie(main, _mm_wgrad_half(k + "_wgrad_sibling_rows", act, dy, 1 - cc_, cs))
        stage_a[k] = (act, dy, cs, _swap_with_sibling_async("grads_to_sibling_" + k, other))
        return main

    def reduce_split_to_chips(k, main):
        act, dy, cs, from_sibling = stage_a.pop(k)
        main, part = tie(main, _mm_wgrad_half(k + "_wgrad_my_rows", act, dy, cc_, cs, init=from_sibling))
        stage_b[k] = _to_chips_async("grads_to_chips_" + k, part)
        return main

    def after(value, token):
        return tie(value, token)[0]

    def reduce_finish(k, main):
        own, from_chips = stage_b.pop(k)
        main, half = tie(main, _sum_chips("grads_sum_" + k, after(own, main), from_chips, chip, cc_))
        gfull[k] = _join_sibling_async("grads_join_" + k, half)
        return main

    def mlp_bwd(layer, dxo, dmlp, xin, hin, u, a, mix):
        down, up = f"mlp_w_down{layer}", f"mlp_w_up{layer}"
        wd, wu = rowsharded(down), full[up]
        dmlp = reduce_split_start(down, a, dmlp, None, dmlp)
        du = _mm_bwd_row(f"mlp_down{layer}_bwd", dmlp, wd, (BF16,), u, drelu)[0]
        du = reduce_split_start(up, hin, du, wu.shape[2], du)
        du = reduce_split_to_chips(down, du)
        dh = _mm_bwd_col(f"mlp_up{layer}_bwd", du, wu)
        dh = reduce_split_to_chips(up, dh)
        dxm, dgain[layer][2], dmix, dgain[layer][1] = _rms_bwd_pair(
            f"rms_premlp{layer}_mix{layer}_bwd", xin, gains[layer, 2], dh, dxo, mix, gains[layer, 1])
        return dxm, dmix

    dgain[1][3] = dgain_mlp1
    dx3, dmix1 = mlp_bwd(1, dy, dmlp1, x3, h4, u1, a1, mix1)
    dmix1 = reduce_start("sb_w_o", _mm_wgrad_row("attn_out_wgrad", att, dmix1).reshape(N_CHIPS, -1, d), dmix1)
    datt = _mm_bwd_row("attn_out_bwd", dmix1, rowsharded("sb_w_o"), (BF16,))[0]
    dq, dk, dv = _attn_bwd(qkv, tot, datt, heads)
    dqkv = jnp.concatenate([dq, dk, dv], axis=1)
    dqkv = reduce_to_chips("sb_w_o", dqkv)
    dqkv = reduce_finish("mlp_w_down1", dqkv)
    dqkv = reduce_finish("mlp_w_up1", dqkv)
    dqkv = reduce_split_start("sb_w_qkv", h3, dqkv, full["sb_w_qkv"].shape[2], dqkv)
    dh3 = _mm_bwd_col("qkv_bwd", dqkv, full["sb_w_qkv"])
    dh3 = reduce_split_to_chips("sb_w_qkv", dh3)
    dx2, dgain[1][0], dmlp0, dgain[0][3] = _rms_bwd_pair(
        "rms_pre1_mlp0_bwd", x2, gains[1, 0], dh3, dx3, mlp0, gains[0, 3])

    dx1, dmix0 = mlp_bwd(0, dx2, dmlp0, x1, h2, u0, a0, mix0)
    dmix0 = reduce_finish("sb_w_o", dmix0)
    dmix0 = reduce_finish("sb_w_qkv", dmix0)
    dmix0 = reduce_finish("mlp_w_down0", dmix0)
    dmix0 = reduce_start("hyb_w_out", _mm_wgrad_row("proj_out_wgrad", ycat, dmix0).reshape(N_CHIPS, -1, d), dmix0)
    dycat = _mm_bwd_row("proj_out_bwd", dmix0, rowsharded("hyb_w_out"))[0]
    dproj, xr_b, dpa_b, dpx_b, sg = _mixer_bwd(
        proj, hseq, dycat, conv_a, conv_b, bias, wa_blk, b_a, wx_blk, b_x, lam)
    dproj = reduce_finish("mlp_w_up0", dproj)
    dproj = reduce_to_chips("hyb_w_out", dproj)
    dproj = reduce_split_start("hyb_w_in", h1, dproj, full["hyb_w_in"].shape[2], dproj)
    dh1 = _mm_bwd_col("proj_in_bwd", dproj, full["hyb_w_in"])
    dh1 = reduce_split_to_chips("hyb_w_in", dh1)
    dx0, dgain[0][0] = _rms_bwd("rms_pre0_bwd", x0, gains[0, 0], dh1, res=dx1)
    dwa = _diag_pairs_to_heads(_mm_wgrad_diag("rg_w_a_wgrad", xr_b, dpa_b), hd)
    dwx = _diag_pairs_to_heads(_mm_wgrad_diag("rg_w_x_wgrad", xr_b, dpx_b), hd)

    dgains = jnp.concatenate([dgain[l][k] for l in range(2) for k in range(4)], axis=0)
    small_parts = [dgains, sg[_SG_CONV_A:_SG_CONV_A + 3], sg[_SG_CONV_B:_SG_CONV_B + 4], sg[_SG_BIAS:_SG_BIAS + 1],
                   dwa, sg[_SG_BA:_SG_BA + 1], dwx, sg[_SG_BX:_SG_BX + 1], sg[_SG_LAM:_SG_LAM + 1]]
    small_rows = [_rows128(p) for p in small_parts]
    n_small = sum(rws.shape[0] for rws in small_rows)
    tail_pad = [jnp.zeros(((-n_small) % (2 * SUBLANES), LANES), F32)] if n_small % (2 * SUBLANES) else []
    reduced = _allreduce_small("allreduce_small", jnp.concatenate(small_rows + tail_pad, axis=0))
    small_full, off = [], 0
    for p, rws in zip(small_parts, small_rows):
        small_full.append(reduced[off:off + p.size // LANES].reshape(p.shape))
        off += rws.shape[0]
    g_gains, g_ca, g_cb, g_bias, g_wa, g_ba, g_wx, g_bx, g_lam = small_full

    def my_cols(g, width):
        return lax.dynamic_slice_in_dim(g, chip * width, width, axis=g.ndim - 1)

    small = [
        ("norm_gains", norm_gains, my_cols(g_gains, norm_gains.shape[2]).reshape(norm_gains.shape),
         m_norm_gains, v_norm_gains),
        ("hyb_conv_a", hyb_conv_a, my_cols(g_ca, hyb_conv_a.shape[2])[None], m_hyb_conv_a, v_hyb_conv_a),
        ("hyb_conv_b", hyb_conv_b, my_cols(g_cb, hyb_conv_b.shape[2])[None], m_hyb_conv_b, v_hyb_conv_b),
        ("hyb_conv_b_bias", hyb_conv_b_bias, g_bias, m_hyb_conv_b_bias, v_hyb_conv_b_bias),
        ("hyb_rg_w_a", hyb_rg_w_a, g_wa[None], m_hyb_rg_w_a, v_hyb_rg_w_a),
        ("hyb_rg_b_a", hyb_rg_b_a, g_ba, m_hyb_rg_b_a, v_hyb_rg_b_a),
        ("hyb_rg_w_x", hyb_rg_w_x, g_wx[None], m_hyb_rg_w_x, v_hyb_rg_w_x),
        ("hyb_rg_b_x", hyb_rg_b_x, g_bx, m_hyb_rg_b_x, v_hyb_rg_b_x),
        ("hyb_rg_lambda", hyb_rg_lambda, g_lam, m_hyb_rg_lambda, v_hyb_rg_lambda),
    ]
    to2d = lambda a: a.reshape(-1, a.shape[-1])
    small_res = _adamw_small("adamw_small", [tuple(to2d(a) for a in (w, g, m, v)) for _, w, g, m, v in small])
    out = {}
    for (nm, w, g, _, _), (dl, nmom, nvar) in zip(small, small_res):
        out[nm] = (g, dl.reshape(w.shape), nmom.reshape(w.shape), nvar.reshape(w.shape))

    stacked = {
        "mlp_w_down": (mlp_w_down, m_mlp_w_down, v_mlp_w_down, ["mlp_w_down0", "mlp_w_down1"]),
        "mlp_w_up": (mlp_w_up, m_mlp_w_up, v_mlp_w_up, ["mlp_w_up0", "mlp_w_up1"]),
        "sb_w_o": (sb_w_o, m_sb_w_o, v_sb_w_o, ["sb_w_o"]),
        "sb_w_qkv": (sb_w_qkv, m_sb_w_qkv, v_sb_w_qkv, ["sb_w_qkv"]),
        "hyb_w_out": (hyb_w_out, m_hyb_w_out, v_hyb_w_out, ["hyb_w_out"]),
        "hyb_w_in": (hyb_w_in, m_hyb_w_in, v_hyb_w_in, ["hyb_w_in"]),
    }

    def update(k, token):
        w, m, v, parts = stacked[k]
        out[k] = tuple(_adamw("adamw_" + k, w, [after(gfull[p], token) for p in parts], m, v))
        return out[k][1]

    token = small_res[0][0]
    token = update("sb_w_qkv", token)
    token = update("sb_w_o", token)
    token = update("mlp_w_down", token)
    token = reduce_finish("hyb_w_out", token)
    token = update("mlp_w_up", token)
    token = reduce_finish("hyb_w_in", token)
    token = update("hyb_w_out", token)
    update("hyb_w_in", token)

    order = ["norm_gains", "hyb_w_in", "hyb_conv_a", "hyb_conv_b", "hyb_conv_b_bias", "hyb_rg_w_a", "hyb_rg_b_a",
             "hyb_rg_w_x", "hyb_rg_b_x", "hyb_rg_lambda", "hyb_w_out", "sb_w_qkv", "sb_w_o", "mlp_w_up",
             "mlp_w_down"]
    return (loss, dx0[None], *[out[k][0] for k in order], *[out[k][1] for k in order],
            *[out[k][2] for k in order], *[out[k][3] for k in order])
```

```python
import functools
import math

import jax
import jax.numpy as jnp
from jax import lax
from jax.experimental import pallas as pl
from jax.experimental.pallas import tpu as pltpu
from jax.experimental.pallas import tpu_sc as plsc

F32 = jnp.float32
BF16 = jnp.bfloat16
MESH = pl.DeviceIdType.MESH

SB_HEADS = 16
NORM_EPS = 1e-6
LRU_C = 8.0
ADAM_LR = 0.001
ADAM_B1 = 0.9
ADAM_B2 = 0.999
ADAM_EPS = 1e-08
ADAM_WD = 0.01
ADAM_STEP = 10

LANES = 128
SUBLANES = 8
VMEM_LIMIT = 48 * 1024 * 1024
MM_TILE = 1024
MM_VMEM_BUDGET = 40 * 1024 * 1024
MM_TILE_N = 1280
MM_TILE_K = 2048
ROW_TILE = 256
STREAM_TILE = 1024
ATT_TILE = 512
ATT_HEADS_PER_STEP = 2
ATT_FWD_HEADS_PER_STEP = 4
N_CHIPS = 4
COLLECTIVE_SIBLING = 8
COLLECTIVE_CHIPS = 9

_DIMS = {
    "nn": (((1,), (0,)), ((), ())),
    "nt": (((1,), (1,)), ((), ())),
    "tn": (((0,), (0,)), ((), ())),
}


def _cp(sem=None, vmem=VMEM_LIMIT):
    return pltpu.CompilerParams(dimension_semantics=sem, vmem_limit_bytes=vmem)


def _pick(dim, pref):
    t = min(dim, pref)
    while dim % t:
        t -= LANES
    return t


def _whole(shape):
    nd = len(shape)
    return pl.BlockSpec(tuple(shape), lambda *_: (0,) * nd)


def _sigmoid(z):
    return 1.0 / (1.0 + jnp.exp(-z))


def _log_sigmoid(z):
    return jnp.minimum(z, 0.0) - jnp.log(1.0 + jnp.exp(-jnp.abs(z)))


def _expm1(z):
    series = z * (1.0 + z * (0.5 + z * (1.0 / 6.0 + z * (1.0 / 24.0))))
    return jnp.where(jnp.abs(z) < 0.05, series, jnp.exp(z) - 1.0)


_GELU_C = math.sqrt(2.0 / math.pi)


def _gelu_and_grad(g):
    inner = _GELU_C * (g + 0.044715 * g * g * g)
    t = jnp.tanh(inner)
    val = 0.5 * g * (1.0 + t)
    grad = 0.5 * (1.0 + t) + 0.5 * g * (1.0 - t * t) * _GELU_C * (1.0 + 3.0 * 0.044715 * g * g)
    return val, grad


def _shift_down(cur, prev8, k, rows):
    n = cur.shape[0]
    rolled = pltpu.roll(cur, k, 0)
    head = jnp.tile(pltpu.roll(prev8, k, 0), (n // SUBLANES, 1))
    return jnp.where(rows < k, head, rolled)


def _shift_up(cur, next8, k, rows):
    n = cur.shape[0]
    rolled = pltpu.roll(cur, n - k, 0)
    tail = jnp.tile(pltpu.roll(next8, SUBLANES - k, 0), (n // SUBLANES, 1))
    return jnp.where(rows >= n - k, tail, rolled)


def _colsum(v):
    return jnp.sum(v, axis=0, keepdims=True)


def _matmul(name, mode, grid, operands, in_specs, out_shapes, out_specs, acc_shape, epilogue=None):
    nk = grid[2]
    n_in = len(operands)
    dims = _DIMS[mode]

    def finish(acc, extra, outs):
        res = epilogue(acc, *[e[...] for e in extra]) if epilogue is not None else (acc,)
        for o_ref, o in zip(outs, res):
            o_ref[...] = o.astype(o_ref.dtype)

    def product(a_ref, b_ref):
        return lax.dot_general(a_ref[...].astype(BF16), b_ref[...].astype(BF16), dims, preferred_element_type=F32)

    def body_single(*refs):
        finish(product(refs[0], refs[1]), refs[2:n_in], refs[n_in:])

    def body(*refs):
        extra = refs[2:n_in]
        outs = refs[n_in:-1]
        acc_ref = refs[-1]
        k = pl.program_id(2)

        @pl.when(k == 0)
        def _():
            acc_ref[...] = product(refs[0], refs[1])

        @pl.when(k > 0)
        def _():
            acc_ref[...] += product(refs[0], refs[1])

        @pl.when(k == nk - 1)
        def _():
            finish(acc_ref[...], extra, outs)

    return pl.pallas_call(
        body_single if nk == 1 else body, name=name, grid=grid, in_specs=in_specs, out_specs=out_specs,
        out_shape=out_shapes, scratch_shapes=[] if nk == 1 else [pltpu.VMEM(acc_shape, F32)],
        compiler_params=_cp(("parallel", "parallel", "arbitrary")),
    )(*operands)


def _pick_m(m, tk, tn, a_dtype, b_dtype, out_dtypes, extra_dtypes=()):
    size = lambda dt: jnp.dtype(dt).itemsize
    per_row = 2 * tk * size(a_dtype) + tn * (2 * sum(size(dt) for dt in tuple(out_dtypes) + tuple(extra_dtypes)) + 4)
    fixed = 2 * tk * tn * size(b_dtype)
    tm = _pick(m, MM_TILE)
    while tm > LANES and tm * per_row + fixed > MM_VMEM_BUDGET:
        tm = _pick(m, tm // 2)
    return tm


def _mm_fwd_col(name, a, wfull, out_dtypes=(F32,), epilogue=None):
    s, kdim = a.shape
    _, _, cs = wfull.shape
    tk, tn = _pick(kdim, MM_TILE_K), _pick(cs, MM_TILE_N)
    tm = _pick_m(s, tk, tn, a.dtype, wfull.dtype, out_dtypes)
    nbj = cs // tn
    grid = (s // tm, N_CHIPS * nbj, kdim // tk)
    out_shapes = [jax.ShapeDtypeStruct((s, N_CHIPS * cs), dt) for dt in out_dtypes]
    out_specs = [pl.BlockSpec((tm, tn), lambda i, n, k: (i, n)) for _ in out_dtypes]
    return _matmul(
        name, "nn", grid, [a, wfull],
        [pl.BlockSpec((tm, tk), lambda i, n, k: (i, k)),
         pl.BlockSpec((None, tk, tn), lambda i, n, k: (n // nbj, k, n % nbj))],
        out_shapes, out_specs, (tm, tn), epilogue)


def _mm_fwd_row(name, a, w2d, out_dtype=F32):
    s, kdim = a.shape
    _, n_out = w2d.shape
    tk, tn = _pick(kdim, MM_TILE_K), _pick(n_out, MM_TILE)
    tm = _pick_m(s, tk, tn, a.dtype, w2d.dtype, (out_dtype,))
    grid = (s // tm, n_out // tn, kdim // tk)
    return _matmul(
        name, "nn", grid, [a, w2d],
        [pl.BlockSpec((tm, tk), lambda i, n, k: (i, k)),
         pl.BlockSpec((tk, tn), lambda i, n, k: (k, n))],
        [jax.ShapeDtypeStruct((s, n_out), out_dtype)],
        [pl.BlockSpec((tm, tn), lambda i, n, k: (i, n))], (tm, tn))[0]


def _mm_bwd_col(name, dy, wfull, out_dtype=F32):
    s, _ = dy.shape
    _, kdim, cs = wfull.shape
    tn, tk = _pick(kdim, MM_TILE), _pick(cs, MM_TILE_K)
    tm = _pick_m(s, tk, tn, dy.dtype, wfull.dtype, (out_dtype,))
    nbj = cs // tk
    grid = (s // tm, kdim // tn, N_CHIPS * nbj)
    return _matmul(
        name, "nt", grid, [dy, wfull],
        [pl.BlockSpec((tm, tk), lambda i, n, k: (i, k)),
         pl.BlockSpec((None, tn, tk), lambda i, n, k: (k // nbj, n, k % nbj))],
        [jax.ShapeDtypeStruct((s, kdim), out_dtype)],
        [pl.BlockSpec((tm, tn), lambda i, n, k: (i, n))], (tm, tn))[0]


def _mm_bwd_row(name, dy, w2d, out_dtypes=(F32,), extra=None, epilogue=None):
    s, n_in = dy.shape
    kdim, _ = w2d.shape
    tn, tk = _pick(kdim, MM_TILE), _pick(n_in, MM_TILE_K)
    tm = _pick_m(s, tk, tn, dy.dtype, w2d.dtype, out_dtypes, () if extra is None else (extra.dtype,))
    grid = (s // tm, kdim // tn, n_in // tk)
    operands = [dy, w2d]
    in_specs = [pl.BlockSpec((tm, tk), lambda i, n, k: (i, k)),
                pl.BlockSpec((tn, tk), lambda i, n, k: (n, k))]
    if extra is not None:
        operands.append(extra)
        in_specs.append(pl.BlockSpec((tm, tn), lambda i, n, k: (i, n)))
    return _matmul(
        name, "nt", grid, operands, in_specs,
        [jax.ShapeDtypeStruct((s, kdim), dt) for dt in out_dtypes],
        [pl.BlockSpec((tm, tn), lambda i, n, k: (i, n)) for _ in out_dtypes], (tm, tn), epilogue)


def _mm_wgrad_row(name, a, dy):
    s, kdim = a.shape
    _, n_out = dy.shape
    tn, ts = _pick(n_out, MM_TILE), _pick(s, MM_TILE_K)
    tm = _pick_m(kdim, ts, tn, a.dtype, dy.dtype, (BF16,))
    grid = (kdim // tm, n_out // tn, s // ts)
    return _matmul(
        name, "tn", grid, [a, dy],
        [pl.BlockSpec((ts, tm), lambda i, n, k: (k, i)),
         pl.BlockSpec((ts, tn), lambda i, n, k: (k, n))],
        [jax.ShapeDtypeStruct((kdim, n_out), BF16)],
        [pl.BlockSpec((tm, tn), lambda i, n, k: (i, n))], (tm, tn))[0]


def _mm_wgrad_half(name, a, dy, half, cs=None, init=None):
    s, kdim = a.shape
    ts = _pick(s, MM_TILE_K)
    nk = s // ts
    if cs is not None:
        hr, cols = kdim // 2, cs
        tn = _pick(cs, MM_TILE_N)
        tm = _pick_m(hr, ts, tn, a.dtype, dy.dtype, (BF16,), (BF16,))
        ni, nbj = hr // tm, cs // tn
        grid = (ni, N_CHIPS * nbj, nk)
        a_map = lambda i, n, k, h: (k, h[0] * ni + i)
        o_map = lambda i, n, k, h: (n // nbj, i, n % nbj)
    else:
        hr, cols = kdim // N_CHIPS // 2, dy.shape[1]
        tn = _pick(cols, MM_TILE)
        tm = _pick_m(hr, ts, tn, a.dtype, dy.dtype, (BF16,), (BF16,))
        ni = hr // tm
        grid = (N_CHIPS * ni, cols // tn, nk)
        a_map = lambda i, n, k, h: (k, (i // ni) * 2 * ni + h[0] * ni + i % ni)
        o_map = lambda i, n, k, h: (i // ni, i % ni, n)
    with_init = init is not None

    def body(*refs):
        a_ref, b_ref = refs[1], refs[2]
        init_ref = refs[3] if with_init else None
        o_ref, acc_ref = refs[-2], refs[-1]
        k = pl.program_id(2)
        prod = lax.dot_general(a_ref[...].astype(BF16), b_ref[...].astype(BF16), _DIMS["tn"],
                               preferred_element_type=F32)

        @pl.when(k == 0)
        def _():
            acc_ref[...] = prod + init_ref[...].astype(F32) if with_init else prod

        @pl.when(k > 0)
        def _():
            acc_ref[...] += prod

        @pl.when(k == nk - 1)
        def _():
            o_ref[...] = acc_ref[...].astype(BF16)

    oblk = pl.BlockSpec((None, tm, tn), o_map)
    grid_spec = pltpu.PrefetchScalarGridSpec(
        num_scalar_prefetch=1, grid=grid,
        in_specs=[pl.BlockSpec((ts, tm), a_map), pl.BlockSpec((ts, tn), lambda i, n, k, h: (k, n))]
        + ([oblk] if with_init else []),
        out_specs=oblk, scratch_shapes=[pltpu.VMEM((tm, tn), F32)])
    operands = [jnp.reshape(half, (1,)).astype(jnp.int32), a, dy] + ([init] if with_init else [])
    return pl.pallas_call(
        body, name=name, grid_spec=grid_spec, out_shape=jax.ShapeDtypeStruct((N_CHIPS, hr, cols), BF16),
        compiler_params=_cp(("parallel", "parallel", "arbitrary")))(*operands)


def _mm_wgrad_diag(name, a, dy):
    s, width = a.shape
    nb = width // LANES
    ts = _pick(s, MM_TILE)
    grid = (nb, 1, s // ts)
    return _matmul(
        name, "tn", grid, [a, dy],
        [pl.BlockSpec((ts, LANES), lambda i, n, k: (k, i)),
         pl.BlockSpec((ts, LANES), lambda i, n, k: (k, i))],
        [jax.ShapeDtypeStruct((nb, LANES, LANES), F32)],
        [pl.BlockSpec((None, LANES, LANES), lambda i, n, k: (i, 0, 0))], (LANES, LANES))[0]


def _rowspec(tr, d):
    return pl.BlockSpec((tr, d), lambda i: (i, 0))


def _vecspec(d):
    return pl.BlockSpec((1, d), lambda i: (0, 0))


def _rms(x, g):
    return x * lax.rsqrt(jnp.mean(x * x, axis=-1, keepdims=True) + NORM_EPS) * g


def _cast_into_slot(name, w, layer, chip):
    _, r, c = w.shape
    tr = _pick(r, STREAM_TILE)

    def body(chip_ref, w_ref, o_ref):
        o_ref[...] = w_ref[...].astype(BF16)

    grid_spec = pltpu.PrefetchScalarGridSpec(
        num_scalar_prefetch=1, grid=(r // tr,),
        in_specs=[pl.BlockSpec((None, tr, c), lambda i, chip_ref: (layer, i, 0))],
        out_specs=pl.BlockSpec((None, tr, c), lambda i, chip_ref: (chip_ref[0], i, 0)))
    return pl.pallas_call(
        body, name=name, grid_spec=grid_spec, out_shape=jax.ShapeDtypeStruct((N_CHIPS, r, c), BF16),
        compiler_params=_cp(("parallel",)))(jnp.reshape(chip, (1,)).astype(jnp.int32), w)


def _rms_fwd(name, x, g):
    s, d = x.shape
    tr = _pick(s, ROW_TILE)

    def body(x_ref, g_ref, h_ref):
        h_ref[...] = _rms(x_ref[...], g_ref[...]).astype(BF16)

    return pl.pallas_call(
        body, name=name, grid=(s // tr,), in_specs=[_rowspec(tr, d), _vecspec(d)],
        out_specs=_rowspec(tr, d), out_shape=jax.ShapeDtypeStruct((s, d), BF16),
        compiler_params=_cp(("parallel",)))(x, g)


def _rms_post(name, y, g_post, res, g_next=None):
    s, d = y.shape
    tr = _pick(s, ROW_TILE)
    with_next = g_next is not None

    def body(*refs):
        if with_next:
            y_ref, gp_ref, r_ref, gn_ref, x_ref, h_ref = refs
        else:
            y_ref, gp_ref, r_ref, x_ref = refs
        xn = r_ref[...] + _rms(y_ref[...], gp_ref[...])
        x_ref[...] = xn
        if with_next:
            h_ref[...] = _rms(xn, gn_ref[...]).astype(BF16)

    operands = [y, g_post, res] + ([g_next] if with_next else [])
    in_specs = [_rowspec(tr, d), _vecspec(d), _rowspec(tr, d)] + ([_vecspec(d)] if with_next else [])
    out_shape = [jax.ShapeDtypeStruct((s, d), F32)] + ([jax.ShapeDtypeStruct((s, d), BF16)] if with_next else [])
    out_specs = [_rowspec(tr, d)] + ([_rowspec(tr, d)] if with_next else [])
    return pl.pallas_call(
        body, name=name, grid=(s // tr,), in_specs=in_specs, out_specs=out_specs, out_shape=out_shape,
        compiler_params=_cp(("parallel",)))(*operands)


def _rms_bwd(name, x, g, dy, res=None, out_dtype=F32):
    s, d = x.shape
    tr = _pick(s, ROW_TILE)
    nsteps = s // tr
    with_res = res is not None

    def body(*refs):
        if with_res:
            x_ref, g_ref, dy_ref, r_ref, dx_ref, dg_ref, acc_ref = refs
        else:
            x_ref, g_ref, dy_ref, dx_ref, dg_ref, acc_ref = refs
        i = pl.program_id(0)

        @pl.when(i == 0)
        def _():
            acc_ref[...] = jnp.zeros_like(acc_ref)

        xv = x_ref[...]
        dyv = dy_ref[...].astype(F32)
        r = lax.rsqrt(jnp.mean(xv * xv, axis=-1, keepdims=True) + NORM_EPS)
        xhat = xv * r
        gy = dyv * g_ref[...]
        dx = r * (gy - xhat * jnp.mean(gy * xhat, axis=-1, keepdims=True))
        if with_res:
            dx = dx + r_ref[...]
        dx_ref[...] = dx.astype(dx_ref.dtype)
        acc_ref[...] += jnp.sum((dyv * xhat).reshape(tr // SUBLANES, SUBLANES, d), axis=0)

        @pl.when(i == nsteps - 1)
        def _():
            dg_ref[...] = jnp.broadcast_to(_colsum(acc_ref[...]), (SUBLANES, d))

    operands = [x, g, dy] + ([res] if with_res else [])
    in_specs = [_rowspec(tr, d), _vecspec(d), _rowspec(tr, d)] + ([_rowspec(tr, d)] if with_res else [])
    dx, dg = pl.pallas_call(
        body, name=name, grid=(nsteps,), in_specs=in_specs,
        out_specs=[_rowspec(tr, d), pl.BlockSpec((SUBLANES, d), lambda i: (0, 0))],
        out_shape=[jax.ShapeDtypeStruct((s, d), out_dtype), jax.ShapeDtypeStruct((SUBLANES, d), F32)],
        scratch_shapes=[pltpu.VMEM((SUBLANES, d), F32)],
        compiler_params=_cp(("arbitrary",)))(*operands)
    return dx, dg[0:1]


def _rms_bwd_pair(name, x, g, dy, res, y2, g2):
    s, d = x.shape
    tr = _pick(s, ROW_TILE)
    nsteps = s // tr

    def through(xv, gv, dyv):
        r = lax.rsqrt(jnp.mean(xv * xv, axis=-1, keepdims=True) + NORM_EPS)
        xhat = xv * r
        gy = dyv * gv
        dx = r * (gy - xhat * jnp.mean(gy * xhat, axis=-1, keepdims=True))
        return dx, jnp.sum((dyv * xhat).reshape(tr // SUBLANES, SUBLANES, d), axis=0)

    def body(x_ref, g_ref, dy_ref, r_ref, y2_ref, g2_ref, dx_ref, d2_ref, dg_ref, dg2_ref, acc_ref, acc2_ref):
        i = pl.program_id(0)

        @pl.when(i == 0)
        def _():
            acc_ref[...] = jnp.zeros_like(acc_ref)
            acc2_ref[...] = jnp.zeros_like(acc2_ref)

        dx, part = through(x_ref[...], g_ref[...], dy_ref[...].astype(F32))
        dx = dx + r_ref[...]
        dx_ref[...] = dx
        acc_ref[...] += part
        d2, part2 = through(y2_ref[...], g2_ref[...], dx)
        d2_ref[...] = d2.astype(d2_ref.dtype)
        acc2_ref[...] += part2

        @pl.when(i == nsteps - 1)
        def _():
            dg_ref[...] = jnp.broadcast_to(_colsum(acc_ref[...]), (SUBLANES, d))
            dg2_ref[...] = jnp.broadcast_to(_colsum(acc2_ref[...]), (SUBLANES, d))

    row, vec = _rowspec(tr, d), _vecspec(d)
    gspec = pl.BlockSpec((SUBLANES, d), lambda i: (0, 0))
    dx, d2, dg, dg2 = pl.pallas_call(
        body, name=name, grid=(nsteps,), in_specs=[row, vec, row, row, row, vec],
        out_specs=[row, row, gspec, gspec],
        out_shape=[jax.ShapeDtypeStruct((s, d), F32), jax.ShapeDtypeStruct((s, d), BF16),
                   jax.ShapeDtypeStruct((SUBLANES, d), F32), jax.ShapeDtypeStruct((SUBLANES, d), F32)],
        scratch_shapes=[pltpu.VMEM((SUBLANES, d), F32), pltpu.VMEM((SUBLANES, d), F32)],
        compiler_params=_cp(("arbitrary",)))(x, g, dy, res, y2, g2)
    return dx, dg[0:1], d2, dg2[0:1]


def _last_norm_and_loss(name, y, g, res, target):
    s, d = y.shape
    tr = _pick(s, ROW_TILE)
    nsteps = s // tr

    def body(y_ref, g_ref, r_ref, t_ref, dx_ref, dy_ref, dg_ref, l_ref, acc_ref, lacc_ref):
        i = pl.program_id(0)

        @pl.when(i == 0)
        def _():
            acc_ref[...] = jnp.zeros_like(acc_ref)
            lacc_ref[...] = jnp.zeros_like(lacc_ref)

        yv = y_ref[...]
        gv = g_ref[...]
        r = lax.rsqrt(jnp.mean(yv * yv, axis=-1, keepdims=True) + NORM_EPS)
        yhat = yv * r
        err = r_ref[...] + yhat * gv - t_ref[...]
        dx = err * (1.0 / d)
        dx_ref[...] = dx
        lacc_ref[...] += jnp.sum((err * err).reshape(tr // SUBLANES, SUBLANES, d), axis=0)
        gy = dx * gv
        dy_ref[...] = (r * (gy - yhat * jnp.mean(gy * yhat, axis=-1, keepdims=True))).astype(dy_ref.dtype)
        acc_ref[...] += jnp.sum((dx * yhat).reshape(tr // SUBLANES, SUBLANES, d), axis=0)

        @pl.when(i == nsteps - 1)
        def _():
            dg_ref[...] = jnp.broadcast_to(_colsum(acc_ref[...]), (SUBLANES, d))
            l_ref[...] = jnp.full((SUBLANES, LANES), (0.5 / d) * jnp.sum(lacc_ref[...]), F32)

    dx, dy, dg, l = pl.pallas_call(
        body, name=name, grid=(nsteps,),
        in_specs=[_rowspec(tr, d), _vecspec(d), _rowspec(tr, d), _rowspec(tr, d)],
        out_specs=[_rowspec(tr, d), _rowspec(tr, d), pl.BlockSpec((SUBLANES, d), lambda i: (0, 0)),
                   pl.BlockSpec((SUBLANES, LANES), lambda i: (0, 0))],
        out_shape=[jax.ShapeDtypeStruct((s, d), F32), jax.ShapeDtypeStruct((s, d), BF16),
                   jax.ShapeDtypeStruct((SUBLANES, d), F32), jax.ShapeDtypeStruct((SUBLANES, LANES), F32)],
        scratch_shapes=[pltpu.VMEM((SUBLANES, d), F32), pltpu.VMEM((SUBLANES, d), F32)],
        compiler_params=_cp(("arbitrary",)))(y, g, res, target)
    return dx, dy, dg[0:1], l[0, 0]


def _gates(xr, wa, ba, wx, bx, lam):
    xb = xr.astype(BF16)
    r = _sigmoid(jnp.dot(xb, wa, preferred_element_type=F32) + ba)
    i = _sigmoid(jnp.dot(xb, wx, preferred_element_type=F32) + bx)
    log_a = LRU_C * r * _log_sigmoid(lam)
    a = jnp.exp(log_a)
    m = jnp.sqrt(-_expm1(2.0 * log_a))
    return r, i, a, m


def _mixer_fwd(proj, conv_a, conv_b, bias, wa_blk, ba, wx_blk, bx, lam):
    s, w5 = proj.shape
    w = w5 // 5
    nch = w // LANES
    ts = _pick(s, ROW_TILE)
    nt = s // ts

    def body(p_ref, pp_ref, ca_ref, cb_ref, bias_ref, wa_ref, ba_ref, wx_ref, bx_ref, lam_ref,
             y_ref, h_ref, a_scr, b_scr, hc_scr):
        t = pl.program_id(0)
        first = t == 0
        rows = lax.broadcasted_iota(jnp.int32, (ts, LANES), 0)

        @pl.when(first)
        def _():
            hc_scr[...] = jnp.zeros_like(hc_scr)

        def cur(comp, c):
            return p_ref[:, comp * w + c * LANES:comp * w + (c + 1) * LANES]

        def prev(comp, c):
            v = pp_ref[:, comp * w + c * LANES:comp * w + (c + 1) * LANES]
            return jnp.where(first, 0.0, v)

        for c in range(nch):
            sl = slice(c * LANES, (c + 1) * LANES)
            cx = cur(1, c) * cur(2, c)
            cxp = prev(1, c) * prev(2, c)
            wa3 = ca_ref[:, sl]
            conv = (wa3[2:3] * cx + wa3[1:2] * _shift_down(cx, cxp, 1, rows)
                    + wa3[0:1] * _shift_down(cx, cxp, 2, rows))
            y_ref[:, sl] = (cur(0, c) * conv).astype(BF16)

        for c in range(nch):
            sl = slice(c * LANES, (c + 1) * LANES)
            xb, xbp = cur(4, c), prev(4, c)
            wb4 = cb_ref[:, sl]
            xr = (wb4[3:4] * xb + wb4[2:3] * _shift_down(xb, xbp, 1, rows)
                  + wb4[1:2] * _shift_down(xb, xbp, 2, rows)
                  + wb4[0:1] * _shift_down(xb, xbp, 3, rows) + bias_ref[:, sl])
            _, i, a, m = _gates(xr, wa_ref[c], ba_ref[:, sl], wx_ref[c], bx_ref[:, sl], lam_ref[:, sl])
            a_scr[:, sl] = a
            b_scr[:, sl] = m * i * xr

        def step(r, h):
            h = a_scr[pl.ds(r, 1), :] * h + b_scr[pl.ds(r, 1), :]
            h_ref[pl.ds(r, 1), :] = h
            return h

        hc_scr[0:1, :] = lax.fori_loop(0, ts, step, hc_scr[0:1, :], unroll=8)

        for c in range(nch):
            sl = slice(c * LANES, (c + 1) * LANES)
            gel, _ = _gelu_and_grad(cur(3, c))
            y_ref[:, w + c * LANES:w + (c + 1) * LANES] = (h_ref[:, sl] * gel).astype(BF16)

    vec = lambda n: _whole((n, w))
    return pl.pallas_call(
        body, name="mixer_fwd", grid=(nt,),
        in_specs=[pl.BlockSpec((ts, w5), lambda t: (t, 0)),
                  pl.BlockSpec((SUBLANES, w5), lambda t: (jnp.maximum(t * (ts // SUBLANES) - 1, 0), 0)),
                  vec(3), vec(4), vec(1), _whole(wa_blk.shape), vec(1), _whole(wx_blk.shape), vec(1), vec(1)],
        out_specs=[pl.BlockSpec((ts, 2 * w), lambda t: (t, 0)), pl.BlockSpec((ts, w), lambda t: (t, 0))],
        out_shape=[jax.ShapeDtypeStruct((s, 2 * w), BF16), jax.ShapeDtypeStruct((s, w), F32)],
        scratch_shapes=[pltpu.VMEM((ts, w), F32), pltpu.VMEM((ts, w), F32), pltpu.VMEM((SUBLANES, w), F32)],
        compiler_params=_cp(("arbitrary",)),
    )(proj, proj, conv_a, conv_b, bias, wa_blk, ba, wx_blk, bx, lam)


_SG_CONV_A, _SG_CONV_B, _SG_BIAS, _SG_BA, _SG_BX, _SG_LAM, _SG_ROWS = 0, 3, 7, 8, 9, 10, 16


def _mixer_bwd(proj, hseq, dy, conv_a, conv_b, bias, wa_blk, ba, wx_blk, bx, lam):
    s, w5 = proj.shape
    w = w5 // 5
    nch = w // LANES
    ts = _pick(s, ROW_TILE)
    nt = s // ts
    tpb = ts // SUBLANES

    def body(p_ref, pp_ref, h_ref, hp_ref, dy_ref, ca_ref, cb_ref, bias_ref, wa_ref, ba_ref, wx_ref, bx_ref,
             lam_ref, dp_ref, xr_ref, dpa_ref, dpx_ref, sg_ref,
             a_scr, g_scr, l_scr, x_scr, r_scr, i_scr, m_scr, cl_scr, cdc_scr, cdx_scr):
        pid = pl.program_id(0)
        last = pid == 0
        first = pid == nt - 1
        rows = lax.broadcasted_iota(jnp.int32, (ts, LANES), 0)

        @pl.when(last)
        def _():
            sg_ref[...] = jnp.zeros_like(sg_ref)
            cl_scr[...] = jnp.zeros_like(cl_scr)
            cdc_scr[...] = jnp.zeros_like(cdc_scr)
            cdx_scr[...] = jnp.zeros_like(cdx_scr)

        def cur(comp, c):
            return p_ref[:, comp * w + c * LANES:comp * w + (c + 1) * LANES]

        def prev(comp, c):
            v = pp_ref[:, comp * w + c * LANES:comp * w + (c + 1) * LANES]
            return jnp.where(first, 0.0, v)

        def put(comp, c, v):
            dp_ref[:, comp * w + c * LANES:comp * w + (c + 1) * LANES] = v.astype(dp_ref.dtype)

        def acc(row, sl, v):
            sg_ref[row:row + 1, sl] += _colsum(v)

        for c in range(nch):
            sl = slice(c * LANES, (c + 1) * LANES)
            bg, cg, ax = cur(0, c), cur(1, c), cur(2, c)
            cx = cg * ax
            cxp = prev(1, c) * prev(2, c)
            cx1 = _shift_down(cx, cxp, 1, rows)
            cx2 = _shift_down(cx, cxp, 2, rows)
            wa3 = ca_ref[:, sl]
            conv = wa3[2:3] * cx + wa3[1:2] * cx1 + wa3[0:1] * cx2
            dya = dy_ref[:, sl]
            put(0, c, dya * conv)
            dconv = dya * bg
            nxt = cdc_scr[:, sl]
            dcx = (wa3[2:3] * dconv + wa3[1:2] * _shift_up(dconv, nxt, 1, rows)
                   + wa3[0:1] * _shift_up(dconv, nxt, 2, rows))
            cdc_scr[:, sl] = dconv[0:SUBLANES]
            put(1, c, dcx * ax)
            put(2, c, dcx * cg)
            acc(_SG_CONV_A + 2, sl, dconv * cx)
            acc(_SG_CONV_A + 1, sl, dconv * cx1)
            acc(_SG_CONV_A + 0, sl, dconv * cx2)

        for c in range(nch):
            sl = slice(c * LANES, (c + 1) * LANES)
            xb, xbp = cur(4, c), prev(4, c)
            wb4 = cb_ref[:, sl]
            xr = (wb4[3:4] * xb + wb4[2:3] * _shift_down(xb, xbp, 1, rows)
                  + wb4[1:2] * _shift_down(xb, xbp, 2, rows)
                  + wb4[0:1] * _shift_down(xb, xbp, 3, rows) + bias_ref[:, sl])
            r, i, a, m = _gates(xr, wa_ref[c], ba_ref[:, sl], wx_ref[c], bx_ref[:, sl], lam_ref[:, sl])
            gel, dgel = _gelu_and_grad(cur(3, c))
            dyb = dy_ref[:, w + c * LANES:w + (c + 1) * LANES]
            put(3, c, dyb * h_ref[:, sl] * dgel)
            g_scr[:, sl] = dyb * gel
            a_scr[:, sl] = a
            x_scr[:, sl] = xr
            r_scr[:, sl] = r
            i_scr[:, sl] = i
            m_scr[:, sl] = m

        def step(j, carry):
            r = ts - 1 - j
            lam_t = g_scr[pl.ds(r, 1), :] + carry
            l_scr[pl.ds(r, 1), :] = lam_t
            return a_scr[pl.ds(r, 1), :] * lam_t

        cl_scr[0:1, :] = lax.fori_loop(0, ts, step, cl_scr[0:1, :], unroll=8)

        for c in range(nch):
            sl = slice(c * LANES, (c + 1) * LANES)
            lam_t = l_scr[:, sl]
            hprev = _shift_down(h_ref[:, sl], jnp.where(first, 0.0, hp_ref[:, sl]), 1, rows)
            xr, r, i, m, a = x_scr[:, sl], r_scr[:, sl], i_scr[:, sl], m_scr[:, sl], a_scr[:, sl]
            da = lam_t * hprev
            dm = lam_t * i * xr
            di = lam_t * m * xr
            dxr = lam_t * m * i
            dlog_a = da * a - dm * a * a / m
            lam_p = lam_ref[:, sl]
            dr = dlog_a * (LRU_C * _log_sigmoid(lam_p))
            acc(_SG_LAM, sl, dlog_a * r * (LRU_C * _sigmoid(-lam_p)))
            dpa = dr * r * (1.0 - r)
            dpx = di * i * (1.0 - i)
            dpa_b, dpx_b = dpa.astype(BF16), dpx.astype(BF16)
            dxr = (dxr + lax.dot_general(dpa_b, wa_ref[c], _DIMS["nt"], preferred_element_type=F32)
                   + lax.dot_general(dpx_b, wx_ref[c], _DIMS["nt"], preferred_element_type=F32))
            xr_ref[:, sl] = xr.astype(BF16)
            dpa_ref[:, sl] = dpa_b
            dpx_ref[:, sl] = dpx_b
            acc(_SG_BA, sl, dpa)
            acc(_SG_BX, sl, dpx)
            acc(_SG_BIAS, sl, dxr)
            nxt = cdx_scr[:, sl]
            wb4 = cb_ref[:, sl]
            put(4, c, wb4[3:4] * dxr + wb4[2:3] * _shift_up(dxr, nxt, 1, rows)
                + wb4[1:2] * _shift_up(dxr, nxt, 2, rows) + wb4[0:1] * _shift_up(dxr, nxt, 3, rows))
            cdx_scr[:, sl] = dxr[0:SUBLANES]
            xb, xbp = cur(4, c), prev(4, c)
            acc(_SG_CONV_B + 3, sl, dxr * xb)
            acc(_SG_CONV_B + 2, sl, dxr * _shift_down(xb, xbp, 1, rows))
            acc(_SG_CONV_B + 1, sl, dxr * _shift_down(xb, xbp, 2, rows))
            acc(_SG_CONV_B + 0, sl, dxr * _shift_down(xb, xbp, 3, rows))

    blk = lambda width: pl.BlockSpec((ts, width), lambda p: (nt - 1 - p, 0))
    pre = lambda width: pl.BlockSpec(
        (SUBLANES, width), lambda p: (jnp.maximum((nt - 1 - p) * tpb - 1, 0), 0))
    vec = lambda n: _whole((n, w))
    big = lambda: pltpu.VMEM((ts, w), F32)
    small = lambda: pltpu.VMEM((SUBLANES, w), F32)
    return pl.pallas_call(
        body, name="mixer_bwd", grid=(nt,),
        in_specs=[blk(w5), pre(w5), blk(w), pre(w), blk(2 * w),
                  vec(3), vec(4), vec(1), _whole(wa_blk.shape), vec(1), _whole(wx_blk.shape), vec(1), vec(1)],
        out_specs=[blk(w5), blk(w), blk(w), blk(w), _whole((_SG_ROWS, w))],
        out_shape=[jax.ShapeDtypeStruct((s, w5), BF16), jax.ShapeDtypeStruct((s, w), BF16),
                   jax.ShapeDtypeStruct((s, w), BF16), jax.ShapeDtypeStruct((s, w), BF16),
                   jax.ShapeDtypeStruct((_SG_ROWS, w), F32)],
        scratch_shapes=[big(), big(), big(), big(), big(), big(), big(), small(), small(), small()],
        compiler_params=_cp(("arbitrary",)),
    )(proj, proj, hseq, hseq, dy, conv_a, conv_b, bias, wa_blk, ba, wx_blk, bx, lam)


def _split_dot(v, tri2):
    hi = v.astype(BF16)
    lo = (v - hi.astype(F32)).astype(BF16)
    return jnp.dot(jnp.concatenate([hi, lo], axis=1), tri2, preferred_element_type=F32)


def _tri(cmp):
    r = lax.broadcasted_iota(jnp.int32, (LANES, LANES), 0)
    c = lax.broadcasted_iota(jnp.int32, (LANES, LANES), 1)
    return cmp(r, c).astype(BF16)


def _lane_blocks(v):
    return [v[:, b * LANES:(b + 1) * LANES] for b in range(v.shape[1] // LANES)]


def _last_lane(v):
    return jnp.broadcast_to(v[:, LANES - 1:LANES], v.shape)


def _scores(q, kb, scale):
    return lax.dot_general(q, kb, _DIMS["nt"], preferred_element_type=F32) * scale


def _log_gates(z, diagonal):
    ls = jnp.minimum(z, 0.0) - jnp.log(1.0 + jnp.exp(-jnp.abs(z)))
    ln = ls - z
    valid = None
    if diagonal:
        valid = (lax.broadcasted_iota(jnp.int32, z.shape, 1) < lax.broadcasted_iota(jnp.int32, z.shape, 0))
        ln = jnp.where(valid, ln, 0.0)
    return ls, ln, valid


def _attn_fwd(qkv, heads):
    s = qkv.shape[0]
    dh = LANES
    tq = _pick(s, ATT_TILE)
    nq = s // tq
    nb = tq // LANES
    scale = 1.0 / math.sqrt(dh)

    hp = ATT_FWD_HEADS_PER_STEP
    groups = heads // hp
    wid = hp * dh

    def body(q_ref, k_ref, v_ref, o_ref, tot_ref, acc_scr, car_scr):
        qi = pl.program_id(1)
        acc_scr[...] = jnp.zeros_like(acc_scr)
        car_scr[...] = jnp.zeros_like(car_scr)
        tri = _tri(lambda r, c: r > c)
        tri = jnp.concatenate([tri, tri], axis=0)

        def tile(kt, diagonal):
            k0 = pl.multiple_of(kt * tq, tq)
            heads_cols = [slice(hh * dh, (hh + 1) * dh) for hh in range(hp)]
            zs = [_scores(q_ref[:, cols], k_ref[pl.ds(k0, tq), cols], scale) for cols in heads_cols]
            gates = [_log_gates(z, diagonal) for z in zs]
            sfxs = [_split_dot(jnp.concatenate(_lane_blocks(ln), axis=0), tri) for _, ln, _ in gates]
            for cols, (ls, ln, valid), sfx in zip(heads_cols, gates, sfxs):
                blocks = _lane_blocks(ln)
                car = car_scr[:, cols]
                parts = [None] * nb
                for b in reversed(range(nb)):
                    sb = sfx[b * tq:(b + 1) * tq]
                    parts[b] = sb + car
                    car = car + (sb[:, 0:1] + blocks[b][:, 0:1])
                car_scr[:, cols] = car
                wgt = jnp.exp(ls + jnp.concatenate(parts, axis=1))
                if diagonal:
                    wgt = jnp.where(valid, wgt, 0.0)
                acc_scr[:, cols] += jnp.dot(
                    wgt.astype(BF16), v_ref[pl.ds(k0, tq), cols], preferred_element_type=F32)

        tile(qi, True)

        def step(j, carry):
            tile(qi - 1 - j, False)
            return carry

        lax.fori_loop(0, qi, step, 0)
        o_ref[...] = acc_scr[...].astype(BF16)
        tot_ref[...] = car_scr[...]

    return pl.pallas_call(
        body, name="attn_fwd", grid=(groups, nq),
        in_specs=[pl.BlockSpec((tq, wid), lambda h, i: (i, h)),
                  pl.BlockSpec((s, wid), lambda h, i: (0, groups + h)),
                  pl.BlockSpec((s, wid), lambda h, i: (0, 2 * groups + h))],
        out_specs=[pl.BlockSpec((tq, wid), lambda h, i: (i, h)), pl.BlockSpec((tq, wid), lambda h, i: (i, h))],
        out_shape=[jax.ShapeDtypeStruct((s, heads * dh), BF16), jax.ShapeDtypeStruct((s, heads * dh), F32)],
        scratch_shapes=[pltpu.VMEM((tq, wid), F32), pltpu.VMEM((tq, wid), F32)],
        compiler_params=_cp(("parallel", "arbitrary")),
    )(qkv, qkv, qkv)


def _attn_bwd(qkv, tot, do, heads):
    s = qkv.shape[0]
    dh = LANES
    tq = _pick(s, ATT_TILE)
    nq = s // tq
    nb = tq // LANES
    scale = 1.0 / math.sqrt(dh)

    hp = ATT_HEADS_PER_STEP
    groups = heads // hp
    wid = hp * dh

    def body(q_ref, k_ref, v_ref, tot_ref, do_ref, dq_ref, dk_ref, dv_ref,
             dq_scr, dk_scr, dv_scr, cl_scr, cg_scr):
        qi = pl.program_id(1)

        @pl.when(qi == 0)
        def _():
            dk_scr[...] = jnp.zeros_like(dk_scr)
            dv_scr[...] = jnp.zeros_like(dv_scr)

        dq_scr[...] = jnp.zeros_like(dq_scr)
        cl_scr[...] = jnp.zeros_like(cl_scr)
        cg_scr[...] = jnp.zeros_like(cg_scr)
        tri_le = _tri(lambda r, c: r <= c)
        tri_le = jnp.concatenate([tri_le, tri_le], axis=0)
        tri_lt = _tri(lambda r, c: r < c)

        def tile(kt, diagonal):
            k0 = pl.multiple_of(kt * tq, tq)
            heads_cols = [slice(hh * dh, (hh + 1) * dh) for hh in range(hp)]
            keys = pl.ds(k0, tq)
            zs = [_scores(q_ref[:, cols], k_ref[keys, cols], scale) for cols in heads_cols]
            dws = [lax.dot_general(do_ref[:, cols], v_ref[keys, cols], _DIMS["nt"], preferred_element_type=F32)
                   for cols in heads_cols]
            gates = [_log_gates(z, diagonal) for z in zs]
            pins = [_split_dot(jnp.concatenate(_lane_blocks(ln), axis=0), tri_le) for _, ln, _ in gates]
            wgts, gs = [], []
            for cols, (ls, _, valid), pin, dw in zip(heads_cols, gates, pins, dws):
                total = tot_ref[:, cols]
                cl = cl_scr[:, cols]
                parts = []
                for b in range(nb):
                    pb = pin[b * tq:(b + 1) * tq] + cl
                    parts.append(total - pb)
                    cl = _last_lane(pb)
                cl_scr[:, cols] = cl
                wgt = jnp.exp(ls + jnp.concatenate(parts, axis=1))
                if diagonal:
                    wgt = jnp.where(valid, wgt, 0.0)
                wgts.append(wgt)
                gs.append(wgt * dw)
            pexs = [jnp.dot(jnp.concatenate(_lane_blocks(g), axis=0).astype(BF16), tri_lt,
                            preferred_element_type=F32) for g in gs]
            for cols, wgt in zip(heads_cols, wgts):
                dv_scr[keys, cols] += lax.dot_general(
                    wgt.astype(BF16), do_ref[:, cols], _DIMS["tn"], preferred_element_type=F32)
            for cols, (ls, _, valid), g, pex in zip(heads_cols, gates, gs, pexs):
                gblocks = _lane_blocks(g)
                cg = cg_scr[:, cols]
                parts = []
                for b in range(nb):
                    pb = pex[b * tq:(b + 1) * tq] + cg
                    parts.append(pb)
                    cg = _last_lane(pb + gblocks[b])
                cg_scr[:, cols] = cg
                dz = g - jnp.exp(ls) * (g + jnp.concatenate(parts, axis=1))
                if diagonal:
                    dz = jnp.where(valid, dz, 0.0)
                dz = dz.astype(BF16)
                dq_scr[:, cols] += jnp.dot(dz, k_ref[keys, cols], preferred_element_type=F32)
                dk_scr[keys, cols] += lax.dot_general(
                    dz, q_ref[:, cols], _DIMS["tn"], preferred_element_type=F32)

        def step(j, carry):
            tile(j, False)
            return carry

        lax.fori_loop(0, qi, step, 0)
        tile(qi, True)
        dq_ref[...] = (dq_scr[...] * scale).astype(BF16)

        @pl.when(qi == nq - 1)
        def _():
            dk_ref[...] = (dk_scr[...] * scale).astype(BF16)
            dv_ref[...] = dv_scr[...].astype(BF16)

    qblk = pl.BlockSpec((tq, wid), lambda h, i: (i, h))
    hblk = pl.BlockSpec((s, wid), lambda h, i: (0, h))
    out = jax.ShapeDtypeStruct((s, heads * dh), BF16)
    return pl.pallas_call(
        body, name="attn_bwd", grid=(groups, nq),
        in_specs=[qblk, pl.BlockSpec((s, wid), lambda h, i: (0, groups + h)),
                  pl.BlockSpec((s, wid), lambda h, i: (0, 2 * groups + h)), qblk, qblk],
        out_specs=[qblk, hblk, hblk], out_shape=[out, out, out],
        scratch_shapes=[pltpu.VMEM((tq, wid), F32), pltpu.VMEM((s, wid), F32), pltpu.VMEM((s, wid), F32),
                        pltpu.VMEM((tq, wid), F32), pltpu.VMEM((tq, wid), F32)],
        compiler_params=_cp(("parallel", "arbitrary")),
    )(qkv, qkv, qkv, tot, do)


def _place():
    x, y, c = lax.axis_index("x"), lax.axis_index("y"), lax.axis_index("c")
    chips = [(1 - x, y), (x, 1 - y), (1 - x, 1 - y)]
    return x, y, c, chips


def _remote(src, dst, send_sem, recv_sem, dev):
    return pltpu.make_async_remote_copy(
        src_ref=src, dst_ref=dst, send_sem=send_sem, recv_sem=recv_sem, device_id=dev, device_id_type=MESH)


def _handshake(peers):
    barrier = pltpu.get_barrier_semaphore()
    for dev in peers:
        pl.semaphore_signal(barrier, inc=1, device_id=dev, device_id_type=MESH)
    pl.semaphore_wait(barrier, len(peers))


def _sequencer_kernel(name, n_sems, collective_id):
    return functools.partial(
        pl.kernel, mesh=plsc.ScalarSubcoreMesh(axis_name="seq", num_cores=1), name=name,
        scratch_types=(pltpu.SemaphoreType.DMA,) * n_sems,
        compiler_params=pltpu.CompilerParams(collective_id=collective_id))


def _allgather_async(name, slot_buf, collective_id):
    buf = jax.new_ref(slot_buf, memory_space=pltpu.MemorySpace.HBM)
    hr = slot_buf.shape[1] // 2

    @_sequencer_kernel(name, 12, collective_id)
    def launch(*sems):
        send_sems, recv_sems, fsend_sems, frecv_sems = sems[0:3], sems[3:6], sems[6:9], sems[9:12]
        x, y, c, chips = _place()
        me = 2 * x + y
        sibling = (x, y, 1 - c)
        _handshake([(px, py, c) for px, py in chips] + [sibling])
        mine = buf.at[me, pl.ds(c * hr, hr)]
        firsts = []
        for k, (px, py) in enumerate(chips):
            cp = _remote(mine, mine, send_sems[k], recv_sems[k], (px, py, c))
            cp.start()
            firsts.append(cp)
        passed = []
        for k, (px, py) in enumerate(chips):
            slot = buf.at[2 * px + py, pl.ds(c * hr, hr)]
            _remote(slot, slot, send_sems[k], recv_sems[k], (px, py, c)).wait_recv()
            cp = _remote(slot, slot, fsend_sems[k], frecv_sems[k], sibling)
            cp.start()
            passed.append(cp)
        for k, (px, py) in enumerate(chips):
            slot = buf.at[2 * px + py, pl.ds((1 - c) * hr, hr)]
            _remote(slot, slot, fsend_sems[k], frecv_sems[k], sibling).wait_recv()
        for cp in firsts + passed:
            cp.wait_send()

    launch()
    return buf[...]


def _to_sibling_async(name, slab):
    src = jax.new_ref(slab, memory_space=pltpu.MemorySpace.HBM)
    hr = slab.shape[1] // 2
    got = jax.empty_ref(jax.ShapeDtypeStruct((N_CHIPS, hr, slab.shape[2]), slab.dtype),
                        memory_space=pltpu.MemorySpace.HBM)

    @_sequencer_kernel(name, 2, COLLECTIVE_SIBLING)
    def launch(send_sem, recv_sem):
        x, y, c, _ = _place()
        _handshake([(x, y, 1 - c)])
        _remote(src.at[:, pl.ds((1 - c) * hr, hr), :], got, send_sem, recv_sem, (x, y, 1 - c)).start()
        _remote(got, got, send_sem, recv_sem, (x, y, 1 - c)).wait()

    launch()
    return src[...], got[...]


def _swap_with_sibling_async(name, part):
    src = jax.new_ref(part, memory_space=pltpu.MemorySpace.HBM)
    got = jax.empty_ref(jax.ShapeDtypeStruct(part.shape, part.dtype), memory_space=pltpu.MemorySpace.HBM)

    @_sequencer_kernel(name, 2, COLLECTIVE_SIBLING)
    def launch(send_sem, recv_sem):
        x, y, c, _ = _place()
        _handshake([(x, y, 1 - c)])
        cp = _remote(src, got, send_sem, recv_sem, (x, y, 1 - c))
        cp.start()
        cp.wait()

    launch()
    return got[...]


def _to_chips_async(name, part):
    src = jax.new_ref(part, memory_space=pltpu.MemorySpace.HBM)
    got = jax.empty_ref(jax.ShapeDtypeStruct((3,) + part.shape[1:], part.dtype), memory_space=pltpu.MemorySpace.HBM)

    @_sequencer_kernel(name, 6, COLLECTIVE_CHIPS)
    def launch(*sems):
        send_sems, recv_sems = sems[0:3], sems[3:6]
        x, y, c, chips = _place()
        _handshake([(px, py, c) for px, py in chips])
        cps = []
        for k, (px, py) in enumerate(chips):
            cp = _remote(src.at[2 * px + py], got.at[k], send_sems[k], recv_sems[k], (px, py, c))
            cp.start()
            cps.append(cp)
        for cp in cps:
            cp.wait()

    launch()
    return src[...], got[...]


def _join_sibling_async(name, half_filled):
    buf = jax.new_ref(half_filled, memory_space=pltpu.MemorySpace.HBM)
    hr = half_filled.shape[0] // 2

    @_sequencer_kernel(name, 2, COLLECTIVE_SIBLING)
    def launch(send_sem, recv_sem):
        x, y, c, _ = _place()
        _handshake([(x, y, 1 - c)])
        mine = buf.at[pl.ds(c * hr, hr)]
        other = buf.at[pl.ds((1 - c) * hr, hr)]
        cp = _remote(mine, mine, send_sem, recv_sem, (x, y, 1 - c))
        cp.start()
        _remote(other, other, send_sem, recv_sem, (x, y, 1 - c)).wait_recv()
        cp.wait_send()

    launch()
    return buf[...]


def _allgather_chips_small(name, v):
    r = v.shape[0]

    def body(v_ref, o_ref, send_sems, recv_sems):
        x, y, c, chips = _place()
        me = 2 * x + y
        o_ref[me] = v_ref[...]
        cps = []
        for k, (px, py) in enumerate(chips):
            cp = _remote(v_ref, o_ref.at[me], send_sems.at[k], recv_sems.at[k], (px, py, c))
            cp.start()
            cps.append(cp)
        for k, (px, py) in enumerate(chips):
            slot = o_ref.at[2 * px + py]
            _remote(slot, slot, send_sems.at[k], recv_sems.at[k], (px, py, c)).wait_recv()
        for cp in cps:
            cp.wait_send()

    return pl.pallas_call(
        body, name=name, in_specs=[pl.BlockSpec(memory_space=pltpu.VMEM)],
        out_specs=pl.BlockSpec(memory_space=pltpu.VMEM),
        out_shape=jax.ShapeDtypeStruct((N_CHIPS, r, LANES), F32),
        scratch_shapes=[pltpu.SemaphoreType.DMA((3,)), pltpu.SemaphoreType.DMA((3,))],
    )(v)


def _allreduce_small(name, v):
    r = v.shape[0]
    hr = r // 2
    assert hr % SUBLANES == 0

    def body(v_ref, o_ref, sib_ref, chips_ref, send_sems, recv_sems):
        x, y, c, chips = _place()
        me = 2 * x + y
        sibling = (x, y, 1 - c)
        first = _remote(v_ref, sib_ref, send_sems.at[0], recv_sems.at[0], sibling)
        first.start()
        first.wait()
        mine = pl.ds(pl.multiple_of(c * hr, SUBLANES), hr)
        chips_ref[me] = v_ref[mine, :] + sib_ref[mine, :]
        cps = []
        for k, (px, py) in enumerate(chips):
            cp = _remote(chips_ref.at[me], chips_ref.at[me], send_sems.at[1 + k], recv_sems.at[1 + k], (px, py, c))
            cp.start()
            cps.append(cp)
        for k, (px, py) in enumerate(chips):
            slot = chips_ref.at[2 * px + py]
            _remote(slot, slot, send_sems.at[1 + k], recv_sems.at[1 + k], (px, py, c)).wait_recv()
        total = chips_ref[0]
        for j in range(1, N_CHIPS):
            total = total + chips_ref[j]
        o_ref[mine, :] = total
        last = _remote(o_ref.at[mine], o_ref.at[mine], send_sems.at[4], recv_sems.at[4], sibling)
        last.start()
        other = o_ref.at[pl.ds(pl.multiple_of((1 - c) * hr, SUBLANES), hr)]
        _remote(other, other, send_sems.at[4], recv_sems.at[4], sibling).wait_recv()
        last.wait_send()
        for cp in cps:
            cp.wait_send()

    return pl.pallas_call(
        body, name=name, in_specs=[pl.BlockSpec(memory_space=pltpu.VMEM)],
        out_specs=pl.BlockSpec(memory_space=pltpu.VMEM),
        out_shape=jax.ShapeDtypeStruct((r, LANES), F32),
        scratch_shapes=[pltpu.VMEM((r, LANES), F32), pltpu.VMEM((N_CHIPS, hr, LANES), F32),
                        pltpu.SemaphoreType.DMA((5,)), pltpu.SemaphoreType.DMA((5,))],
    )(v)


def _add_sibling(name, slabs, recv, c):
    _, r, cols = slabs.shape
    hr = r // 2
    tr = _pick(hr, STREAM_TILE)
    nb = hr // tr

    def body(c_ref, a_ref, b_ref, o_ref):
        o_ref[...] = (a_ref[...].astype(F32) + b_ref[...].astype(F32)).astype(BF16)

    grid_spec = pltpu.PrefetchScalarGridSpec(
        num_scalar_prefetch=1, grid=(N_CHIPS, nb),
        in_specs=[pl.BlockSpec((None, tr, cols), lambda j, i, c_ref: (j, c_ref[0] * nb + i, 0)),
                  pl.BlockSpec((None, tr, cols), lambda j, i, c_ref: (j, i, 0))],
        out_specs=pl.BlockSpec((None, tr, cols), lambda j, i, c_ref: (j, i, 0)))
    return pl.pallas_call(
        body, name=name, grid_spec=grid_spec,
        out_shape=jax.ShapeDtypeStruct((N_CHIPS, hr, cols), BF16),
        compiler_params=_cp(("parallel", "parallel")))(jnp.reshape(c, (1,)).astype(jnp.int32), slabs, recv)


def _sum_chips(name, own, recv, chip, c):
    _, hr, cols = recv.shape
    tr = _pick(hr, STREAM_TILE // 2)
    nb = hr // tr

    def body(sc_ref, own_ref, recv_ref, o_ref):
        total = own_ref[...].astype(F32)
        for k in range(3):
            total = total + recv_ref[k].astype(F32)
        o_ref[...] = total

    grid_spec = pltpu.PrefetchScalarGridSpec(
        num_scalar_prefetch=1, grid=(nb,),
        in_specs=[pl.BlockSpec((None, tr, cols), lambda i, sc: (sc[0], i, 0)),
                  pl.BlockSpec((3, tr, cols), lambda i, sc: (0, i, 0))],
        out_specs=pl.BlockSpec((tr, cols), lambda i, sc: (sc[1] * nb + i, 0)))
    return pl.pallas_call(
        body, name=name, grid_spec=grid_spec, out_shape=jax.ShapeDtypeStruct((2 * hr, cols), F32),
        compiler_params=_cp(("parallel",)))(jnp.stack([chip, c]).astype(jnp.int32), own, recv)


def _adamw_math(w, g, m, v):
    m = ADAM_B1 * m + (1.0 - ADAM_B1) * g
    v = ADAM_B2 * v + (1.0 - ADAM_B2) * (g * g)
    m_hat = m / (1.0 - ADAM_B1 ** ADAM_STEP)
    v_hat = v / (1.0 - ADAM_B2 ** ADAM_STEP)
    delta = -ADAM_LR * (m_hat / (jnp.sqrt(v_hat) + ADAM_EPS) + ADAM_WD * w)
    return delta, m, v


def _adamw(name, w, gs, m, v):
    nl, r, cols = w.shape
    tr = _pick(r, ROW_TILE)

    def body(*refs):
        w_ref, m_ref, v_ref = refs[0:3]
        g_refs = refs[3:3 + nl]
        go_ref, d_ref, nm_ref, nv_ref = refs[3 + nl:]
        layer = pl.program_id(0)
        g = g_refs[0][...]
        for j in range(1, nl):
            g = jnp.where(layer == j, g_refs[j][...], g)
        d, nm, nv = _adamw_math(w_ref[...], g, m_ref[...], v_ref[...])
        go_ref[...] = g
        d_ref[...] = d
        nm_ref[...] = nm
        nv_ref[...] = nv

    spec3 = pl.BlockSpec((None, tr, cols), lambda l, i: (l, i, 0))
    gspec = pl.BlockSpec((tr, cols), lambda l, i: (i, 0))
    out = jax.ShapeDtypeStruct((nl, r, cols), F32)
    return pl.pallas_call(
        body, name=name, grid=(nl, r // tr), in_specs=[spec3] * 3 + [gspec] * nl, out_specs=[spec3] * 4,
        out_shape=[out] * 4, compiler_params=_cp(("parallel", "parallel")))(w, m, v, *gs)


def _adamw_small(name, groups):
    n = len(groups)
    flat = [a for grp in groups for a in grp]

    def body(*refs):
        ins, outs = refs[:4 * n], refs[4 * n:]
        for p in range(n):
            w_ref, g_ref, m_ref, v_ref = ins[4 * p:4 * p + 4]
            d, nm, nv = _adamw_math(w_ref[...], g_ref[...], m_ref[...], v_ref[...])
            outs[3 * p][...] = d
            outs[3 * p + 1][...] = nm
            outs[3 * p + 2][...] = nv

    vm = pl.BlockSpec(memory_space=pltpu.VMEM)
    out_shape = [jax.ShapeDtypeStruct(grp[0].shape, F32) for grp in groups for _ in range(3)]
    res = pl.pallas_call(
        body, name=name, in_specs=[vm] * (4 * n), out_specs=[vm] * (3 * n), out_shape=out_shape)(*flat)
    return [tuple(res[3 * p:3 * p + 3]) for p in range(n)]


def _block_diag_pairs(w):
    h, d, _ = w.shape
    z = jnp.zeros((h // 2, d, d), w.dtype)
    top = jnp.concatenate([w[0::2], z], axis=2)
    bot = jnp.concatenate([z, w[1::2]], axis=2)
    return jnp.concatenate([top, bot], axis=1).astype(BF16)


def _diag_pairs_to_heads(g, d):
    a = g[:, :d, :d]
    b = g[:, d:, d:]
    return jnp.stack([a, b], axis=1).reshape(-1, d, d)


def _rows128(a):
    flat = a.reshape(-1, LANES)
    pad = (-flat.shape[0]) % SUBLANES
    if pad:
        flat = jnp.concatenate([flat, jnp.zeros((pad, LANES), flat.dtype)], axis=0)
    return flat


def _unshard_last(g4, shape):
    g4 = g4.reshape((N_CHIPS,) + tuple(shape))
    return jnp.concatenate([g4[j] for j in range(N_CHIPS)], axis=-1)


def kernel(x, norm_gains, hyb_w_in, hyb_conv_a, hyb_conv_b, hyb_conv_b_bias, hyb_rg_w_a, hyb_rg_b_a, hyb_rg_w_x, hyb_rg_b_x, hyb_rg_lambda, hyb_w_out, sb_w_qkv, sb_w_o, mlp_w_up, mlp_w_down, loss_target, m_norm_gains, m_hyb_w_in, m_hyb_conv_a, m_hyb_conv_b, m_hyb_conv_b_bias, m_hyb_rg_w_a, m_hyb_rg_b_a, m_hyb_rg_w_x, m_hyb_rg_b_x, m_hyb_rg_lambda, m_hyb_w_out, m_sb_w_qkv, m_sb_w_o, m_mlp_w_up, m_mlp_w_down, v_norm_gains, v_hyb_w_in, v_hyb_conv_a, v_hyb_conv_b, v_hyb_conv_b_bias, v_hyb_rg_w_a, v_hyb_rg_b_a, v_hyb_rg_w_x, v_hyb_rg_b_x, v_hyb_rg_lambda, v_hyb_w_out, v_sb_w_qkv, v_sb_w_o, v_mlp_w_up, v_mlp_w_down):
    cx_ = lax.axis_index("x")
    cy_ = lax.axis_index("y")
    cc_ = lax.axis_index("c")
    chip = 2 * cx_ + cy_

    x0 = x[0]
    target = loss_target[0]
    s, d = x0.shape
    heads = SB_HEADS
    assert d // heads == LANES
    n_rg, hd = hyb_rg_w_a.shape[1], hyb_rg_w_a.shape[2]
    wmix = n_rg * hd
    assert 2 * hd == LANES

    big = {
        "hyb_w_in": (hyb_w_in, 0), "hyb_w_out": (hyb_w_out, 0), "mlp_w_up0": (mlp_w_up, 0),
        "mlp_w_down0": (mlp_w_down, 0), "sb_w_qkv": (sb_w_qkv, 0), "sb_w_o": (sb_w_o, 0),
        "mlp_w_up1": (mlp_w_up, 1), "mlp_w_down1": (mlp_w_down, 1),
    }
    names = list(big)
    slots = [_cast_into_slot("cast_" + k, big[k][0], big[k][1], chip) for k in names]
    full = {k: _allgather_async("allgather_" + k, slot, cid) for cid, (k, slot) in enumerate(zip(names, slots))}
    rowsharded = lambda k: full[k].reshape(-1, full[k].shape[2])

    ng_s, ca_s, cb_s = norm_gains.reshape(-1, norm_gains.shape[2]), hyb_conv_a[0], hyb_conv_b[0]
    packed = jnp.concatenate([_rows128(ng_s), _rows128(ca_s), _rows128(cb_s)], axis=0)
    gathered = _allgather_chips_small("allgather_small", packed)
    n0 = ng_s.size // LANES
    n1 = n0 + (-n0) % SUBLANES
    m0 = ca_s.size // LANES
    m1 = m0 + (-m0) % SUBLANES
    k0 = cb_s.size // LANES
    gains = _unshard_last(gathered[:, 0:n0], ng_s.shape).reshape(2, 4, 1, d)
    conv_a = _unshard_last(gathered[:, n1:n1 + m0], ca_s.shape)
    conv_b = _unshard_last(gathered[:, n1 + m1:n1 + m1 + k0], cb_s.shape)
    bias, b_a, b_x, lam = hyb_conv_b_bias, hyb_rg_b_a, hyb_rg_b_x, hyb_rg_lambda
    wa_blk = _block_diag_pairs(hyb_rg_w_a[0])
    wx_blk = _block_diag_pairs(hyb_rg_w_x[0])

    relu_sq = lambda acc: (jnp.maximum(acc, 0.0), jnp.square(jnp.maximum(acc, 0.0)))

    h1 = _rms_fwd("rms_pre0", x0, gains[0, 0])
    proj = _mm_fwd_col("proj_in", h1, full["hyb_w_in"])[0]
    ycat, hseq = _mixer_fwd(proj, conv_a, conv_b, bias, wa_blk, b_a, wx_blk, b_x, lam)
    mix0 = _mm_fwd_row("proj_out", ycat, rowsharded("hyb_w_out"))
    x1, h2 = _rms_post("rms_mix0", mix0, gains[0, 1], x0, gains[0, 2])
    u0, a0 = _mm_fwd_col("mlp_up0", h2, full["mlp_w_up0"], (BF16, BF16), relu_sq)
    mlp0 = _mm_fwd_row("mlp_down0", a0, rowsharded("mlp_w_down0"))
    x2, h3 = _rms_post("rms_mlp0", mlp0, gains[0, 3], x1, gains[1, 0])

    qkv = _mm_fwd_col("qkv", h3, full["sb_w_qkv"], (BF16,))[0]
    att, tot = _attn_fwd(qkv, heads)
    mix1 = _mm_fwd_row("attn_out", att, rowsharded("sb_w_o"))
    x3, h4 = _rms_post("rms_mix1", mix1, gains[1, 1], x2, gains[1, 2])
    u1, a1 = _mm_fwd_col("mlp_up1", h4, full["mlp_w_up1"], (BF16, BF16), relu_sq)
    mlp1 = _mm_fwd_row("mlp_down1", a1, rowsharded("mlp_w_down1"))
    dy, dmlp1, dgain_mlp1, loss_local = _last_norm_and_loss("last_norm_loss", mlp1, gains[1, 3], x3, target)
    loss = lax.psum(loss_local, ("x", "y", "c"))

    dgain = [[None] * 4 for _ in range(2)]
    drelu = lambda acc, u: (acc * (2.0 * u.astype(F32)),)
    stage_a, stage_b, gfull = {}, {}, {}

    def tie(main, side):
        return lax.optimization_barrier((main, side))

    def reduce_start(k, slab, main):
        main, slab = tie(main, slab)
        stage_a[k] = _to_sibling_async("grads_to_sibling_" + k, slab)
        return main

    def reduce_to_chips(k, main):
        slab, from_sibling = stage_a.pop(k)
        main, part = tie(main, _add_sibling("grads_add_" + k, slab, from_sibling, cc_))
        stage_b[k] = _to_chips_async("grads_to_chips_" + k, part)
        return main

    def reduce_split_start(k, act, dy, cs, main):
        main, other = tie(main, _mm_wgrad_half(k + "_wgrad_sibling_rows", act, dy, 1 - cc_, cs))
        stage_a[k] = (act, dy, cs, _swap_with_sibling_async("grads_to_sibling_" + k, other))
        return main

    def reduce_split_to_chips(k, main):
        act, dy, cs, from_sibling = stage_a.pop(k)
        main, part = tie(main, _mm_wgrad_half(k + "_wgrad_my_rows", act, dy, cc_, cs, init=from_sibling))
        stage_b[k] = _to_chips_async("grads_to_chips_" + k, part)
        return main

    def after(value, token):
        return tie(value, token)[0]

    def reduce_finish(k, main):
        own, from_chips = stage_b.pop(k)
        main, half = tie(main, _sum_chips("grads_sum_" + k, after(own, main), from_chips, chip, cc_))
        gfull[k] = _join_sibling_async("grads_join_" + k, half)
        return main

    def mlp_bwd(layer, dxo, dmlp, xin, hin, u, a, mix):
        down, up = f"mlp_w_down{layer}", f"mlp_w_up{layer}"
        wd, wu = rowsharded(down), full[up]
        dmlp = reduce_split_start(down, a, dmlp, None, dmlp)
        du = _mm_bwd_row(f"mlp_down{layer}_bwd", dmlp, wd, (BF16,), u, drelu)[0]
        du = reduce_split_start(up, hin, du, wu.shape[2], du)
        du = reduce_split_to_chips(down, du)
        dh = _mm_bwd_col(f"mlp_up{layer}_bwd", du, wu)
        dh = reduce_split_to_chips(up, dh)
        dxm, dgain[layer][2], dmix, dgain[layer][1] = _rms_bwd_pair(
            f"rms_premlp{layer}_mix{layer}_bwd", xin, gains[layer, 2], dh, dxo, mix, gains[layer, 1])
        return dxm, dmix

    dgain[1][3] = dgain_mlp1
    dx3, dmix1 = mlp_bwd(1, dy, dmlp1, x3, h4, u1, a1, mix1)
    dmix1 = reduce_start("sb_w_o", _mm_wgrad_row("attn_out_wgrad", att, dmix1).reshape(N_CHIPS, -1, d), dmix1)
    datt = _mm_bwd_row("attn_out_bwd", dmix1, rowsharded("sb_w_o"), (BF16,))[0]
    dq, dk, dv = _attn_bwd(qkv, tot, datt, heads)
    dqkv = jnp.concatenate([dq, dk, dv], axis=1)
    dqkv = reduce_to_chips("sb_w_o", dqkv)
    dqkv = reduce_finish("mlp_w_down1", dqkv)
    dqkv = reduce_finish("mlp_w_up1", dqkv)
    dqkv = reduce_split_start("sb_w_qkv", h3, dqkv, full["sb_w_qkv"].shape[2], dqkv)
    dh3 = _mm_bwd_col("qkv_bwd", dqkv, full["sb_w_qkv"])
    dh3 = reduce_split_to_chips("sb_w_qkv", dh3)
    dx2, dgain[1][0], dmlp0, dgain[0][3] = _rms_bwd_pair(
        "rms_pre1_mlp0_bwd", x2, gains[1, 0], dh3, dx3, mlp0, gains[0, 3])

    dx1, dmix0 = mlp_bwd(0, dx2, dmlp0, x1, h2, u0, a0, mix0)
    dmix0 = reduce_finish("sb_w_o", dmix0)
    dmix0 = reduce_finish("sb_w_qkv", dmix0)
    dmix0 = reduce_finish("mlp_w_down0", dmix0)
    dmix0 = reduce_start("hyb_w_out", _mm_wgrad_row("proj_out_wgrad", ycat, dmix0).reshape(N_CHIPS, -1, d), dmix0)
    dycat = _mm_bwd_row("proj_out_bwd", dmix0, rowsharded("hyb_w_out"))[0]
    dproj, xr_b, dpa_b, dpx_b, sg = _mixer_bwd(
        proj, hseq, dycat, conv_a, conv_b, bias, wa_blk, b_a, wx_blk, b_x, lam)
    dproj = reduce_finish("mlp_w_up0", dproj)
    dproj = reduce_to_chips("hyb_w_out", dproj)
    dproj = reduce_split_start("hyb_w_in", h1, dproj, full["hyb_w_in"].shape[2], dproj)
    dh1 = _mm_bwd_col("proj_in_bwd", dproj, full["hyb_w_in"])
    dh1 = reduce_split_to_chips("hyb_w_in", dh1)
    dx0, dgain[0][0] = _rms_bwd("rms_pre0_bwd", x0, gains[0, 0], dh1, res=dx1)
    dwa = _diag_pairs_to_heads(_mm_wgrad_diag("rg_w_a_wgrad", xr_b, dpa_b), hd)
    dwx = _diag_pairs_to_heads(_mm_wgrad_diag("rg_w_x_wgrad", xr_b, dpx_b), hd)

    dgains = jnp.concatenate([dgain[l][k] for l in range(2) for k in range(4)], axis=0)
    small_parts = [dgains, sg[_SG_CONV_A:_SG_CONV_A + 3], sg[_SG_CONV_B:_SG_CONV_B + 4], sg[_SG_BIAS:_SG_BIAS + 1],
                   dwa, sg[_SG_BA:_SG_BA + 1], dwx, sg[_SG_BX:_SG_BX + 1], sg[_SG_LAM:_SG_LAM + 1]]
    small_rows = [_rows128(p) for p in small_parts]
    n_small = sum(rws.shape[0] for rws in small_rows)
    tail_pad = [jnp.zeros(((-n_small) % (2 * SUBLANES), LANES), F32)] if n_small % (2 * SUBLANES) else []
    reduced = _allreduce_small("allreduce_small", jnp.concatenate(small_rows + tail_pad, axis=0))
    small_full, off = [], 0
    for p, rws in zip(small_parts, small_rows):
        small_full.append(reduced[off:off + p.size // LANES].reshape(p.shape))
        off += rws.shape[0]
    g_gains, g_ca, g_cb, g_bias, g_wa, g_ba, g_wx, g_bx, g_lam = small_full

    def my_cols(g, width):
        return lax.dynamic_slice_in_dim(g, chip * width, width, axis=g.ndim - 1)

    small = [
        ("norm_gains", norm_gains, my_cols(g_gains, norm_gains.shape[2]).reshape(norm_gains.shape),
         m_norm_gains, v_norm_gains),
        ("hyb_conv_a", hyb_conv_a, my_cols(g_ca, hyb_conv_a.shape[2])[None], m_hyb_conv_a, v_hyb_conv_a),
        ("hyb_conv_b", hyb_conv_b, my_cols(g_cb, hyb_conv_b.shape[2])[None], m_hyb_conv_b, v_hyb_conv_b),
        ("hyb_conv_b_bias", hyb_conv_b_bias, g_bias, m_hyb_conv_b_bias, v_hyb_conv_b_bias),
        ("hyb_rg_w_a", hyb_rg_w_a, g_wa[None], m_hyb_rg_w_a, v_hyb_rg_w_a),
        ("hyb_rg_b_a", hyb_rg_b_a, g_ba, m_hyb_rg_b_a, v_hyb_rg_b_a),
        ("hyb_rg_w_x", hyb_rg_w_x, g_wx[None], m_hyb_rg_w_x, v_hyb_rg_w_x),
        ("hyb_rg_b_x", hyb_rg_b_x, g_bx, m_hyb_rg_b_x, v_hyb_rg_b_x),
        ("hyb_rg_lambda", hyb_rg_lambda, g_lam, m_hyb_rg_lambda, v_hyb_rg_lambda),
    ]
    to2d = lambda a: a.reshape(-1, a.shape[-1])
    small_res = _adamw_small("adamw_small", [tuple(to2d(a) for a in (w, g, m, v)) for _, w, g, m, v in small])
    out = {}
    for (nm, w, g, _, _), (dl, nmom, nvar) in zip(small, small_res):
        out[nm] = (g, dl.reshape(w.shape), nmom.reshape(w.shape), nvar.reshape(w.shape))

    stacked = {
        "mlp_w_down": (mlp_w_down, m_mlp_w_down, v_mlp_w_down, ["mlp_w_down0", "mlp_w_down1"]),
        "mlp_w_up": (mlp_w_up, m_mlp_w_up, v_mlp_w_up, ["mlp_w_up0", "mlp_w_up1"]),
        "sb_w_o": (sb_w_o, m_sb_w_o, v_sb_w_o, ["sb_w_o"]),
        "sb_w_qkv": (sb_w_qkv, m_sb_w_qkv, v_sb_w_qkv, ["sb_w_qkv"]),
        "hyb_w_out": (hyb_w_out, m_hyb_w_out, v_hyb_w_out, ["hyb_w_out"]),
        "hyb_w_in": (hyb_w_in, m_hyb_w_in, v_hyb_w_in, ["hyb_w_in"]),
    }

    def update(k, token):
        w, m, v, parts = stacked[k]
        out[k] = tuple(_adamw("adamw_" + k, w, [after(gfull[p], token) for p in parts], m, v))
        return out[k][1]

    token = small_res[0][0]
    token = update("sb_w_qkv", token)
    token = update("sb_w_o", token)
    token = update("mlp_w_down", token)
    token = reduce_finish("hyb_w_out", token)
    token = update("mlp_w_up", token)
    token = reduce_finish("hyb_w_in", token)
    token = update("hyb_w_out", token)
    update("hyb_w_in", token)

    order = ["norm_gains", "hyb_w_in", "hyb_conv_a", "hyb_conv_b", "hyb_conv_b_bias", "hyb_rg_w_a", "hyb_rg_b_a",
             "hyb_rg_w_x", "hyb_rg_b_x", "hyb_rg_lambda", "hyb_w_out", "sb_w_qkv", "sb_w_o", "mlp_w_up",
             "mlp_w_down"]
    return (loss, dx0[None], *[out[k][0] for k in order], *[out[k][1] for k in order],
            *[out[k][2] for k in order], *[out[k][3] for k in order])
```

```python
import functools
import math

import jax
import jax.numpy as jnp
from jax import lax
from jax.experimental import pallas as pl
from jax.experimental.pallas import tpu as pltpu
from jax.experimental.pallas import tpu_sc as plsc

F32 = jnp.float32
BF16 = jnp.bfloat16
MESH = pl.DeviceIdType.MESH

SB_HEADS = 16
NORM_EPS = 1e-6
LRU_C = 8.0
ADAM_LR = 0.001
ADAM_B1 = 0.9
ADAM_B2 = 0.999
ADAM_EPS = 1e-08
ADAM_WD = 0.01
ADAM_STEP = 10

LANES = 128
SUBLANES = 8
VMEM_LIMIT = 48 * 1024 * 1024
MM_TILE = 1024
MM_VMEM_BUDGET = 40 * 1024 * 1024
MM_TILE_N = 1280
MM_TILE_K = 2048
ROW_TILE = 256
STREAM_TILE = 1024
ATT_TILE = 512
ATT_HEADS_PER_STEP = 2
ATT_FWD_HEADS_PER_STEP = 4
N_CHIPS = 4
COLLECTIVE_SIBLING = 8
COLLECTIVE_CHIPS = 9

_DIMS = {
    "nn": (((1,), (0,)), ((), ())),
    "nt": (((1,), (1,)), ((), ())),
    "tn": (((0,), (0,)), ((), ())),
}


def _cp(sem=None, vmem=VMEM_LIMIT):
    return pltpu.CompilerParams(dimension_semantics=sem, vmem_limit_bytes=vmem)


def _pick(dim, pref):
    t = min(dim, pref)
    while dim % t:
        t -= LANES
    return t


def _whole(shape):
    nd = len(shape)
    return pl.BlockSpec(tuple(shape), lambda *_: (0,) * nd)


def _sigmoid(z):
    return 1.0 / (1.0 + jnp.exp(-z))


def _log_sigmoid(z):
    return jnp.minimum(z, 0.0) - jnp.log(1.0 + jnp.exp(-jnp.abs(z)))


def _expm1(z):
    series = z * (1.0 + z * (0.5 + z * (1.0 / 6.0 + z * (1.0 / 24.0))))
    return jnp.where(jnp.abs(z) < 0.05, series, jnp.exp(z) - 1.0)


_GELU_C = math.sqrt(2.0 / math.pi)


def _gelu_and_grad(g):
    inner = _GELU_C * (g + 0.044715 * g * g * g)
    t = jnp.tanh(inner)
    val = 0.5 * g * (1.0 + t)
    grad = 0.5 * (1.0 + t) + 0.5 * g * (1.0 - t * t) * _GELU_C * (1.0 + 3.0 * 0.044715 * g * g)
    return val, grad


def _shift_down(cur, prev8, k, rows):
    n = cur.shape[0]
    rolled = pltpu.roll(cur, k, 0)
    head = jnp.tile(pltpu.roll(prev8, k, 0), (n // SUBLANES, 1))
    return jnp.where(rows < k, head, rolled)


def _shift_up(cur, next8, k, rows):
    n = cur.shape[0]
    rolled = pltpu.roll(cur, n - k, 0)
    tail = jnp.tile(pltpu.roll(next8, SUBLANES - k, 0), (n // SUBLANES, 1))
    return jnp.where(rows >= n - k, tail, rolled)


def _colsum(v):
    return jnp.sum(v, axis=0, keepdims=True)


def _matmul(name, mode, grid, operands, in_specs, out_shapes, out_specs, acc_shape, epilogue=None):
    nk = grid[2]
    n_in = len(operands)
    dims = _DIMS[mode]

    def finish(acc, extra, outs):
        res = epilogue(acc, *[e[...] for e in extra]) if epilogue is not None else (acc,)
        for o_ref, o in zip(outs, res):
            o_ref[...] = o.astype(o_ref.dtype)

    def product(a_ref, b_ref):
        return lax.dot_general(a_ref[...].astype(BF16), b_ref[...].astype(BF16), dims, preferred_element_type=F32)

    def body_single(*refs):
        finish(product(refs[0], refs[1]), refs[2:n_in], refs[n_in:])

    def body(*refs):
        extra = refs[2:n_in]
        outs = refs[n_in:-1]
        acc_ref = refs[-1]
        k = pl.program_id(2)

        @pl.when(k == 0)
        def _():
            acc_ref[...] = product(refs[0], refs[1])

        @pl.when(k > 0)
        def _():
            acc_ref[...] += product(refs[0], refs[1])

        @pl.when(k == nk - 1)
        def _():
            finish(acc_ref[...], extra, outs)

    return pl.pallas_call(
        body_single if nk == 1 else body, name=name, grid=grid, in_specs=in_specs, out_specs=out_specs,
        out_shape=out_shapes, scratch_shapes=[] if nk == 1 else [pltpu.VMEM(acc_shape, F32)],
        compiler_params=_cp(("parallel", "parallel", "arbitrary")),
    )(*operands)


def _pick_m(m, tk, tn, a_dtype, b_dtype, out_dtypes, extra_dtypes=()):
    size = lambda dt: jnp.dtype(dt).itemsize
    per_row = 2 * tk * size(a_dtype) + tn * (2 * sum(size(dt) for dt in tuple(out_dtypes) + tuple(extra_dtypes)) + 4)
    fixed = 2 * tk * tn * size(b_dtype)
    tm = _pick(m, MM_TILE)
    while tm > LANES and tm * per_row + fixed > MM_VMEM_BUDGET:
        tm = _pick(m, tm // 2)
    return tm


def _mm_fwd_col(name, a, wfull, out_dtypes=(F32,), epilogue=None):
    s, kdim = a.shape
    _, _, cs = wfull.shape
    tk, tn = _pick(kdim, MM_TILE_K), _pick(cs, MM_TILE_N)
    tm = _pick_m(s, tk, tn, a.dtype, wfull.dtype, out_dtypes)
    nbj = cs // tn
    grid = (s // tm, N_CHIPS * nbj, kdim // tk)
    out_shapes = [jax.ShapeDtypeStruct((s, N_CHIPS * cs), dt) for dt in out_dtypes]
    out_specs = [pl.BlockSpec((tm, tn), lambda i, n, k: (i, n)) for _ in out_dtypes]
    return _matmul(
        name, "nn", grid, [a, wfull],
        [pl.BlockSpec((tm, tk), lambda i, n, k: (i, k)),
         pl.BlockSpec((None, tk, tn), lambda i, n, k: (n // nbj, k, n % nbj))],
        out_shapes, out_specs, (tm, tn), epilogue)


def _mm_fwd_row(name, a, w2d, out_dtype=F32):
    s, kdim = a.shape
    _, n_out = w2d.shape
    tk, tn = _pick(kdim, MM_TILE_K), _pick(n_out, MM_TILE)
    tm = _pick_m(s, tk, tn, a.dtype, w2d.dtype, (out_dtype,))
    grid = (s // tm, n_out // tn, kdim // tk)
    return _matmul(
        name, "nn", grid, [a, w2d],
        [pl.BlockSpec((tm, tk), lambda i, n, k: (i, k)),
         pl.BlockSpec((tk, tn), lambda i, n, k: (k, n))],
        [jax.ShapeDtypeStruct((s, n_out), out_dtype)],
        [pl.BlockSpec((tm, tn), lambda i, n, k: (i, n))], (tm, tn))[0]


def _mm_bwd_col(name, dy, wfull, out_dtype=F32):
    s, _ = dy.shape
    _, kdim, cs = wfull.shape
    tn, tk = _pick(kdim, MM_TILE), _pick(cs, MM_TILE_K)
    tm = _pick_m(s, tk, tn, dy.dtype, wfull.dtype, (out_dtype,))
    nbj = cs // tk
    grid = (s // tm, kdim // tn, N_CHIPS * nbj)
    return _matmul(
        name, "nt", grid, [dy, wfull],
        [pl.BlockSpec((tm, tk), lambda i, n, k: (i, k)),
         pl.BlockSpec((None, tn, tk), lambda i, n, k: (k // nbj, n, k % nbj))],
        [jax.ShapeDtypeStruct((s, kdim), out_dtype)],
        [pl.BlockSpec((tm, tn), lambda i, n, k: (i, n))], (tm, tn))[0]


def _mm_bwd_row(name, dy, w2d, out_dtypes=(F32,), extra=None, epilogue=None):
    s, n_in = dy.shape
    kdim, _ = w2d.shape
    tn, tk = _pick(kdim, MM_TILE), _pick(n_in, MM_TILE_K)
    tm = _pick_m(s, tk, tn, dy.dtype, w2d.dtype, out_dtypes, () if extra is None else (extra.dtype,))
    grid = (s // tm, kdim // tn, n_in // tk)
    operands = [dy, w2d]
    in_specs = [pl.BlockSpec((tm, tk), lambda i, n, k: (i, k)),
                pl.BlockSpec((tn, tk), lambda i, n, k: (n, k))]
    if extra is not None:
        operands.append(extra)
        in_specs.append(pl.BlockSpec((tm, tn), lambda i, n, k: (i, n)))
    return _matmul(
        name, "nt", grid, operands, in_specs,
        [jax.ShapeDtypeStruct((s, kdim), dt) for dt in out_dtypes],
        [pl.BlockSpec((tm, tn), lambda i, n, k: (i, n)) for _ in out_dtypes], (tm, tn), epilogue)


def _mm_wgrad_row(name, a, dy):
    s, kdim = a.shape
    _, n_out = dy.shape
    tn, ts = _pick(n_out, MM_TILE), _pick(s, MM_TILE_K)
    tm = _pick_m(kdim, ts, tn, a.dtype, dy.dtype, (BF16,))
    grid = (kdim // tm, n_out // tn, s // ts)
    return _matmul(
        name, "tn", grid, [a, dy],
        [pl.BlockSpec((ts, tm), lambda i, n, k: (k, i)),
         pl.BlockSpec((ts, tn), lambda i, n, k: (k, n))],
        [jax.ShapeDtypeStruct((kdim, n_out), BF16)],
        [pl.BlockSpec((tm, tn), lambda i, n, k: (i, n))], (tm, tn))[0]


def _mm_wgrad_half(name, a, dy, half, cs=None, init=None):
    s, kdim = a.shape
    ts = _pick(s, MM_TILE_K)
    nk = s // ts
    if cs is not None:
        hr, cols = kdim // 2, cs
        tn = _pick(cs, MM_TILE_N)
        tm = _pick_m(hr, ts, tn, a.dtype, dy.dtype, (BF16,), (BF16,))
        ni, nbj = hr // tm, cs // tn
        grid = (ni, N_CHIPS * nbj, nk)
        a_map = lambda i, n, k, h: (k, h[0] * ni + i)
        o_map = lambda i, n, k, h: (n // nbj, i, n % nbj)
    else:
        hr, cols = kdim // N_CHIPS // 2, dy.shape[1]
        tn = _pick(cols, MM_TILE)
        tm = _pick_m(hr, ts, tn, a.dtype, dy.dtype, (BF16,), (BF16,))
        ni = hr // tm
        grid = (N_CHIPS * ni, cols // tn, nk)
        a_map = lambda i, n, k, h: (k, (i // ni) * 2 * ni + h[0] * ni + i % ni)
        o_map = lambda i, n, k, h: (i // ni, i % ni, n)
    with_init = init is not None

    def body(*refs):
        a_ref, b_ref = refs[1], refs[2]
        init_ref = refs[3] if with_init else None
        o_ref, acc_ref = refs[-2], refs[-1]
        k = pl.program_id(2)

        def product():
            return lax.dot_general(a_ref[...].astype(BF16), b_ref[...].astype(BF16), _DIMS["tn"],
                                   preferred_element_type=F32)

        @pl.when(k == 0)
        def _():
            if with_init:
                acc_ref[...] = init_ref[...].astype(F32)
                acc_ref[...] += product()
            else:
                acc_ref[...] = product()

        @pl.when(k > 0)
        def _():
            acc_ref[...] += product()

        @pl.when(k == nk - 1)
        def _():
            o_ref[...] = acc_ref[...].astype(BF16)

    oblk = pl.BlockSpec((None, tm, tn), o_map)
    grid_spec = pltpu.PrefetchScalarGridSpec(
        num_scalar_prefetch=1, grid=grid,
        in_specs=[pl.BlockSpec((ts, tm), a_map), pl.BlockSpec((ts, tn), lambda i, n, k, h: (k, n))]
        + ([oblk] if with_init else []),
        out_specs=oblk, scratch_shapes=[pltpu.VMEM((tm, tn), F32)])
    operands = [jnp.reshape(half, (1,)).astype(jnp.int32), a, dy] + ([init] if with_init else [])
    return pl.pallas_call(
        body, name=name, grid_spec=grid_spec, out_shape=jax.ShapeDtypeStruct((N_CHIPS, hr, cols), BF16),
        compiler_params=_cp(("parallel", "parallel", "arbitrary")))(*operands)


def _mm_wgrad_diag(name, a, dy):
    s, width = a.shape
    nb = width // LANES
    ts = _pick(s, MM_TILE)
    grid = (nb, 1, s // ts)
    return _matmul(
        name, "tn", grid, [a, dy],
        [pl.BlockSpec((ts, LANES), lambda i, n, k: (k, i)),
         pl.BlockSpec((ts, LANES), lambda i, n, k: (k, i))],
        [jax.ShapeDtypeStruct((nb, LANES, LANES), F32)],
        [pl.BlockSpec((None, LANES, LANES), lambda i, n, k: (i, 0, 0))], (LANES, LANES))[0]


def _rowspec(tr, d):
    return pl.BlockSpec((tr, d), lambda i: (i, 0))


def _vecspec(d):
    return pl.BlockSpec((1, d), lambda i: (0, 0))


def _rms(x, g):
    return x * lax.rsqrt(jnp.mean(x * x, axis=-1, keepdims=True) + NORM_EPS) * g


def _cast_into_slot(name, w, layer, chip):
    _, r, c = w.shape
    tr = _pick(r, STREAM_TILE)

    def body(chip_ref, w_ref, o_ref):
        o_ref[...] = w_ref[...].astype(BF16)

    grid_spec = pltpu.PrefetchScalarGridSpec(
        num_scalar_prefetch=1, grid=(r // tr,),
        in_specs=[pl.BlockSpec((None, tr, c), lambda i, chip_ref: (layer, i, 0))],
        out_specs=pl.BlockSpec((None, tr, c), lambda i, chip_ref: (chip_ref[0], i, 0)))
    return pl.pallas_call(
        body, name=name, grid_spec=grid_spec, out_shape=jax.ShapeDtypeStruct((N_CHIPS, r, c), BF16),
        compiler_params=_cp(("parallel",)))(jnp.reshape(chip, (1,)).astype(jnp.int32), w)


def _rms_fwd(name, x, g):
    s, d = x.shape
    tr = _pick(s, ROW_TILE)

    def body(x_ref, g_ref, h_ref):
        h_ref[...] = _rms(x_ref[...], g_ref[...]).astype(BF16)

    return pl.pallas_call(
        body, name=name, grid=(s // tr,), in_specs=[_rowspec(tr, d), _vecspec(d)],
        out_specs=_rowspec(tr, d), out_shape=jax.ShapeDtypeStruct((s, d), BF16),
        compiler_params=_cp(("parallel",)))(x, g)


def _rms_post(name, y, g_post, res, g_next=None):
    s, d = y.shape
    tr = _pick(s, ROW_TILE)
    with_next = g_next is not None

    def body(*refs):
        if with_next:
            y_ref, gp_ref, r_ref, gn_ref, x_ref, h_ref = refs
        else:
            y_ref, gp_ref, r_ref, x_ref = refs
        xn = r_ref[...] + _rms(y_ref[...], gp_ref[...])
        x_ref[...] = xn
        if with_next:
            h_ref[...] = _rms(xn, gn_ref[...]).astype(BF16)

    operands = [y, g_post, res] + ([g_next] if with_next else [])
    in_specs = [_rowspec(tr, d), _vecspec(d), _rowspec(tr, d)] + ([_vecspec(d)] if with_next else [])
    out_shape = [jax.ShapeDtypeStruct((s, d), F32)] + ([jax.ShapeDtypeStruct((s, d), BF16)] if with_next else [])
    out_specs = [_rowspec(tr, d)] + ([_rowspec(tr, d)] if with_next else [])
    return pl.pallas_call(
        body, name=name, grid=(s // tr,), in_specs=in_specs, out_specs=out_specs, out_shape=out_shape,
        compiler_params=_cp(("parallel",)))(*operands)


def _rms_bwd(name, x, g, dy, res=None, out_dtype=F32):
    s, d = x.shape
    tr = _pick(s, ROW_TILE)
    nsteps = s // tr
    with_res = res is not None

    def body(*refs):
        if with_res:
            x_ref, g_ref, dy_ref, r_ref, dx_ref, dg_ref, acc_ref = refs
        else:
            x_ref, g_ref, dy_ref, dx_ref, dg_ref, acc_ref = refs
        i = pl.program_id(0)

        @pl.when(i == 0)
        def _():
            acc_ref[...] = jnp.zeros_like(acc_ref)

        xv = x_ref[...]
        dyv = dy_ref[...].astype(F32)
        r = lax.rsqrt(jnp.mean(xv * xv, axis=-1, keepdims=True) + NORM_EPS)
        xhat = xv * r
        gy = dyv * g_ref[...]
        dx = r * (gy - xhat * jnp.mean(gy * xhat, axis=-1, keepdims=True))
        if with_res:
            dx = dx + r_ref[...]
        dx_ref[...] = dx.astype(dx_ref.dtype)
        acc_ref[...] += jnp.sum((dyv * xhat).reshape(tr // SUBLANES, SUBLANES, d), axis=0)

        @pl.when(i == nsteps - 1)
        def _():
            dg_ref[...] = jnp.broadcast_to(_colsum(acc_ref[...]), (SUBLANES, d))

    operands = [x, g, dy] + ([res] if with_res else [])
    in_specs = [_rowspec(tr, d), _vecspec(d), _rowspec(tr, d)] + ([_rowspec(tr, d)] if with_res else [])
    dx, dg = pl.pallas_call(
        body, name=name, grid=(nsteps,), in_specs=in_specs,
        out_specs=[_rowspec(tr, d), pl.BlockSpec((SUBLANES, d), lambda i: (0, 0))],
        out_shape=[jax.ShapeDtypeStruct((s, d), out_dtype), jax.ShapeDtypeStruct((SUBLANES, d), F32)],
        scratch_shapes=[pltpu.VMEM((SUBLANES, d), F32)],
        compiler_params=_cp(("arbitrary",)))(*operands)
    return dx, dg[0:1]


def _rms_bwd_pair(name, x, g, dy, res, y2, g2):
    s, d = x.shape
    tr = _pick(s, ROW_TILE)
    nsteps = s // tr

    def through(xv, gv, dyv):
        r = lax.rsqrt(jnp.mean(xv * xv, axis=-1, keepdims=True) + NORM_EPS)
        xhat = xv * r
        gy = dyv * gv
        dx = r * (gy - xhat * jnp.mean(gy * xhat, axis=-1, keepdims=True))
        return dx, jnp.sum((dyv * xhat).reshape(tr // SUBLANES, SUBLANES, d), axis=0)

    def body(x_ref, g_ref, dy_ref, r_ref, y2_ref, g2_ref, dx_ref, d2_ref, dg_ref, dg2_ref, acc_ref, acc2_ref):
        i = pl.program_id(0)

        @pl.when(i == 0)
        def _():
            acc_ref[...] = jnp.zeros_like(acc_ref)
            acc2_ref[...] = jnp.zeros_like(acc2_ref)

        dx, part = through(x_ref[...], g_ref[...], dy_ref[...].astype(F32))
        dx = dx + r_ref[...]
        dx_ref[...] = dx
        acc_ref[...] += part
        d2, part2 = through(y2_ref[...], g2_ref[...], dx)
        d2_ref[...] = d2.astype(d2_ref.dtype)
        acc2_ref[...] += part2

        @pl.when(i == nsteps - 1)
        def _():
            dg_ref[...] = jnp.broadcast_to(_colsum(acc_ref[...]), (SUBLANES, d))
            dg2_ref[...] = jnp.broadcast_to(_colsum(acc2_ref[...]), (SUBLANES, d))

    row, vec = _rowspec(tr, d), _vecspec(d)
    gspec = pl.BlockSpec((SUBLANES, d), lambda i: (0, 0))
    dx, d2, dg, dg2 = pl.pallas_call(
        body, name=name, grid=(nsteps,), in_specs=[row, vec, row, row, row, vec],
        out_specs=[row, row, gspec, gspec],
        out_shape=[jax.ShapeDtypeStruct((s, d), F32), jax.ShapeDtypeStruct((s, d), BF16),
                   jax.ShapeDtypeStruct((SUBLANES, d), F32), jax.ShapeDtypeStruct((SUBLANES, d), F32)],
        scratch_shapes=[pltpu.VMEM((SUBLANES, d), F32), pltpu.VMEM((SUBLANES, d), F32)],
        compiler_params=_cp(("arbitrary",)))(x, g, dy, res, y2, g2)
    return dx, dg[0:1], d2, dg2[0:1]


def _last_norm_and_loss(name, y, g, res, target):
    s, d = y.shape
    tr = _pick(s, ROW_TILE)
    nsteps = s // tr

    def body(y_ref, g_ref, r_ref, t_ref, dx_ref, dy_ref, dg_ref, l_ref, acc_ref, lacc_ref):
        i = pl.program_id(0)

        @pl.when(i == 0)
        def _():
            acc_ref[...] = jnp.zeros_like(acc_ref)
            lacc_ref[...] = jnp.zeros_like(lacc_ref)

        yv = y_ref[...]
        gv = g_ref[...]
        r = lax.rsqrt(jnp.mean(yv * yv, axis=-1, keepdims=True) + NORM_EPS)
        yhat = yv * r
        err = r_ref[...] + yhat * gv - t_ref[...]
        dx = err * (1.0 / d)
        dx_ref[...] = dx
        lacc_ref[...] += jnp.sum((err * err).reshape(tr // SUBLANES, SUBLANES, d), axis=0)
        gy = dx * gv
        dy_ref[...] = (r * (gy - yhat * jnp.mean(gy * yhat, axis=-1, keepdims=True))).astype(dy_ref.dtype)
        acc_ref[...] += jnp.sum((dx * yhat).reshape(tr // SUBLANES, SUBLANES, d), axis=0)

        @pl.when(i == nsteps - 1)
        def _():
            dg_ref[...] = jnp.broadcast_to(_colsum(acc_ref[...]), (SUBLANES, d))
            l_ref[...] = jnp.full((SUBLANES, LANES), (0.5 / d) * jnp.sum(lacc_ref[...]), F32)

    dx, dy, dg, l = pl.pallas_call(
        body, name=name, grid=(nsteps,),
        in_specs=[_rowspec(tr, d), _vecspec(d), _rowspec(tr, d), _rowspec(tr, d)],
        out_specs=[_rowspec(tr, d), _rowspec(tr, d), pl.BlockSpec((SUBLANES, d), lambda i: (0, 0)),
                   pl.BlockSpec((SUBLANES, LANES), lambda i: (0, 0))],
        out_shape=[jax.ShapeDtypeStruct((s, d), F32), jax.ShapeDtypeStruct((s, d), BF16),
                   jax.ShapeDtypeStruct((SUBLANES, d), F32), jax.ShapeDtypeStruct((SUBLANES, LANES), F32)],
        scratch_shapes=[pltpu.VMEM((SUBLANES, d), F32), pltpu.VMEM((SUBLANES, d), F32)],
        compiler_params=_cp(("arbitrary",)))(y, g, res, target)
    return dx, dy, dg[0:1], l[0, 0]


def _gates(xr, wa, ba, wx, bx, lam):
    xb = xr.astype(BF16)
    r = _sigmoid(jnp.dot(xb, wa, preferred_element_type=F32) + ba)
    i = _sigmoid(jnp.dot(xb, wx, preferred_element_type=F32) + bx)
    log_a = LRU_C * r * _log_sigmoid(lam)
    a = jnp.exp(log_a)
    m = jnp.sqrt(-_expm1(2.0 * log_a))
    return r, i, a, m


def _mixer_fwd(proj, conv_a, conv_b, bias, wa_blk, ba, wx_blk, bx, lam):
    s, w5 = proj.shape
    w = w5 // 5
    nch = w // LANES
    ts = _pick(s, ROW_TILE)
    nt = s // ts

    def body(p_ref, pp_ref, ca_ref, cb_ref, bias_ref, wa_ref, ba_ref, wx_ref, bx_ref, lam_ref,
             y_ref, h_ref, a_scr, b_scr, hc_scr):
        t = pl.program_id(0)
        first = t == 0
        rows = lax.broadcasted_iota(jnp.int32, (ts, LANES), 0)

        @pl.when(first)
        def _():
            hc_scr[...] = jnp.zeros_like(hc_scr)

        def cur(comp, c):
            return p_ref[:, comp * w + c * LANES:comp * w + (c + 1) * LANES]

        def prev(comp, c):
            v = pp_ref[:, comp * w + c * LANES:comp * w + (c + 1) * LANES]
            return jnp.where(first, 0.0, v)

        for c in range(nch):
            sl = slice(c * LANES, (c + 1) * LANES)
            cx = cur(1, c) * cur(2, c)
            cxp = prev(1, c) * prev(2, c)
            wa3 = ca_ref[:, sl]
            conv = (wa3[2:3] * cx + wa3[1:2] * _shift_down(cx, cxp, 1, rows)
                    + wa3[0:1] * _shift_down(cx, cxp, 2, rows))
            y_ref[:, sl] = (cur(0, c) * conv).astype(BF16)

        for c in range(nch):
            sl = slice(c * LANES, (c + 1) * LANES)
            xb, xbp = cur(4, c), prev(4, c)
            wb4 = cb_ref[:, sl]
            xr = (wb4[3:4] * xb + wb4[2:3] * _shift_down(xb, xbp, 1, rows)
                  + wb4[1:2] * _shift_down(xb, xbp, 2, rows)
                  + wb4[0:1] * _shift_down(xb, xbp, 3, rows) + bias_ref[:, sl])
            _, i, a, m = _gates(xr, wa_ref[c], ba_ref[:, sl], wx_ref[c], bx_ref[:, sl], lam_ref[:, sl])
            a_scr[:, sl] = a
            b_scr[:, sl] = m * i * xr

        def step(r, h):
            h = a_scr[pl.ds(r, 1), :] * h + b_scr[pl.ds(r, 1), :]
            h_ref[pl.ds(r, 1), :] = h
            return h

        hc_scr[0:1, :] = lax.fori_loop(0, ts, step, hc_scr[0:1, :], unroll=8)

        for c in range(nch):
            sl = slice(c * LANES, (c + 1) * LANES)
            gel, _ = _gelu_and_grad(cur(3, c))
            y_ref[:, w + c * LANES:w + (c + 1) * LANES] = (h_ref[:, sl] * gel).astype(BF16)

    vec = lambda n: _whole((n, w))
    return pl.pallas_call(
        body, name="mixer_fwd", grid=(nt,),
        in_specs=[pl.BlockSpec((ts, w5), lambda t: (t, 0)),
                  pl.BlockSpec((SUBLANES, w5), lambda t: (jnp.maximum(t * (ts // SUBLANES) - 1, 0), 0)),
                  vec(3), vec(4), vec(1), _whole(wa_blk.shape), vec(1), _whole(wx_blk.shape), vec(1), vec(1)],
        out_specs=[pl.BlockSpec((ts, 2 * w), lambda t: (t, 0)), pl.BlockSpec((ts, w), lambda t: (t, 0))],
        out_shape=[jax.ShapeDtypeStruct((s, 2 * w), BF16), jax.ShapeDtypeStruct((s, w), F32)],
        scratch_shapes=[pltpu.VMEM((ts, w), F32), pltpu.VMEM((ts, w), F32), pltpu.VMEM((SUBLANES, w), F32)],
        compiler_params=_cp(("arbitrary",)),
    )(proj, proj, conv_a, conv_b, bias, wa_blk, ba, wx_blk, bx, lam)


_SG_CONV_A, _SG_CONV_B, _SG_BIAS, _SG_BA, _SG_BX, _SG_LAM, _SG_ROWS = 0, 3, 7, 8, 9, 10, 16


def _mixer_bwd(proj, hseq, dy, conv_a, conv_b, bias, wa_blk, ba, wx_blk, bx, lam):
    s, w5 = proj.shape
    w = w5 // 5
    nch = w // LANES
    ts = _pick(s, ROW_TILE)
    nt = s // ts
    tpb = ts // SUBLANES

    def body(p_ref, pp_ref, h_ref, hp_ref, dy_ref, ca_ref, cb_ref, bias_ref, wa_ref, ba_ref, wx_ref, bx_ref,
             lam_ref, dp_ref, xr_ref, dpa_ref, dpx_ref, sg_ref,
             a_scr, g_scr, l_scr, x_scr, r_scr, i_scr, m_scr, cl_scr, cdc_scr, cdx_scr):
        pid = pl.program_id(0)
        last = pid == 0
        first = pid == nt - 1
        rows = lax.broadcasted_iota(jnp.int32, (ts, LANES), 0)

        @pl.when(last)
        def _():
            sg_ref[...] = jnp.zeros_like(sg_ref)
            cl_scr[...] = jnp.zeros_like(cl_scr)
            cdc_scr[...] = jnp.zeros_like(cdc_scr)
            cdx_scr[...] = jnp.zeros_like(cdx_scr)

        def cur(comp, c):
            return p_ref[:, comp * w + c * LANES:comp * w + (c + 1) * LANES]

        def prev(comp, c):
            v = pp_ref[:, comp * w + c * LANES:comp * w + (c + 1) * LANES]
            return jnp.where(first, 0.0, v)

        def put(comp, c, v):
            dp_ref[:, comp * w + c * LANES:comp * w + (c + 1) * LANES] = v.astype(dp_ref.dtype)

        def acc(row, sl, v):
            sg_ref[row:row + 1, sl] += _colsum(v)

        for c in range(nch):
            sl = slice(c * LANES, (c + 1) * LANES)
            bg, cg, ax = cur(0, c), cur(1, c), cur(2, c)
            cx = cg * ax
            cxp = prev(1, c) * prev(2, c)
            cx1 = _shift_down(cx, cxp, 1, rows)
            cx2 = _shift_down(cx, cxp, 2, rows)
            wa3 = ca_ref[:, sl]
            conv = wa3[2:3] * cx + wa3[1:2] * cx1 + wa3[0:1] * cx2
            dya = dy_ref[:, sl]
            put(0, c, dya * conv)
            dconv = dya * bg
            nxt = cdc_scr[:, sl]
            dcx = (wa3[2:3] * dconv + wa3[1:2] * _shift_up(dconv, nxt, 1, rows)
                   + wa3[0:1] * _shift_up(dconv, nxt, 2, rows))
            cdc_scr[:, sl] = dconv[0:SUBLANES]
            put(1, c, dcx * ax)
            put(2, c, dcx * cg)
            acc(_SG_CONV_A + 2, sl, dconv * cx)
            acc(_SG_CONV_A + 1, sl, dconv * cx1)
            acc(_SG_CONV_A + 0, sl, dconv * cx2)

        for c in range(nch):
            sl = slice(c * LANES, (c + 1) * LANES)
            xb, xbp = cur(4, c), prev(4, c)
            wb4 = cb_ref[:, sl]
            xr = (wb4[3:4] * xb + wb4[2:3] * _shift_down(xb, xbp, 1, rows)
                  + wb4[1:2] * _shift_down(xb, xbp, 2, rows)
                  + wb4[0:1] * _shift_down(xb, xbp, 3, rows) + bias_ref[:, sl])
            r, i, a, m = _gates(xr, wa_ref[c], ba_ref[:, sl], wx_ref[c], bx_ref[:, sl], lam_ref[:, sl])
            gel, dgel = _gelu_and_grad(cur(3, c))
            dyb = dy_ref[:, w + c * LANES:w + (c + 1) * LANES]
            put(3, c, dyb * h_ref[:, sl] * dgel)
            g_scr[:, sl] = dyb * gel
            a_scr[:, sl] = a
            x_scr[:, sl] = xr
            r_scr[:, sl] = r
            i_scr[:, sl] = i
            m_scr[:, sl] = m

        def step(j, carry):
            r = ts - 1 - j
            lam_t = g_scr[pl.ds(r, 1), :] + carry
            l_scr[pl.ds(r, 1), :] = lam_t
            return a_scr[pl.ds(r, 1), :] * lam_t

        cl_scr[0:1, :] = lax.fori_loop(0, ts, step, cl_scr[0:1, :], unroll=8)

        for c in range(nch):
            sl = slice(c * LANES, (c + 1) * LANES)
            lam_t = l_scr[:, sl]
            hprev = _shift_down(h_ref[:, sl], jnp.where(first, 0.0, hp_ref[:, sl]), 1, rows)
            xr, r, i, m, a = x_scr[:, sl], r_scr[:, sl], i_scr[:, sl], m_scr[:, sl], a_scr[:, sl]
            da = lam_t * hprev
            dm = lam_t * i * xr
            di = lam_t * m * xr
            dxr = lam_t * m * i
            dlog_a = da * a - dm * a * a / m
            lam_p = lam_ref[:, sl]
            dr = dlog_a * (LRU_C * _log_sigmoid(lam_p))
            acc(_SG_LAM, sl, dlog_a * r * (LRU_C * _sigmoid(-lam_p)))
            dpa = dr * r * (1.0 - r)
            dpx = di * i * (1.0 - i)
            dpa_b, dpx_b = dpa.astype(BF16), dpx.astype(BF16)
            dxr = (dxr + lax.dot_general(dpa_b, wa_ref[c], _DIMS["nt"], preferred_element_type=F32)
                   + lax.dot_general(dpx_b, wx_ref[c], _DIMS["nt"], preferred_element_type=F32))
            xr_ref[:, sl] = xr.astype(BF16)
            dpa_ref[:, sl] = dpa_b
            dpx_ref[:, sl] = dpx_b
            acc(_SG_BA, sl, dpa)
            acc(_SG_BX, sl, dpx)
            acc(_SG_BIAS, sl, dxr)
            nxt = cdx_scr[:, sl]
            wb4 = cb_ref[:, sl]
            put(4, c, wb4[3:4] * dxr + wb4[2:3] * _shift_up(dxr, nxt, 1, rows)
                + wb4[1:2] * _shift_up(dxr, nxt, 2, rows) + wb4[0:1] * _shift_up(dxr, nxt, 3, rows))
            cdx_scr[:, sl] = dxr[0:SUBLANES]
            xb, xbp = cur(4, c), prev(4, c)
            acc(_SG_CONV_B + 3, sl, dxr * xb)
            acc(_SG_CONV_B + 2, sl, dxr * _shift_down(xb, xbp, 1, rows))
            acc(_SG_CONV_B + 1, sl, dxr * _shift_down(xb, xbp, 2, rows))
            acc(_SG_CONV_B + 0, sl, dxr * _shift_down(xb, xbp, 3, rows))

    blk = lambda width: pl.BlockSpec((ts, width), lambda p: (nt - 1 - p, 0))
    pre = lambda width: pl.BlockSpec(
        (SUBLANES, width), lambda p: (jnp.maximum((nt - 1 - p) * tpb - 1, 0), 0))
    vec = lambda n: _whole((n, w))
    big = lambda: pltpu.VMEM((ts, w), F32)
    small = lambda: pltpu.VMEM((SUBLANES, w), F32)
    return pl.pallas_call(
        body, name="mixer_bwd", grid=(nt,),
        in_specs=[blk(w5), pre(w5), blk(w), pre(w), blk(2 * w),
                  vec(3), vec(4), vec(1), _whole(wa_blk.shape), vec(1), _whole(wx_blk.shape), vec(1), vec(1)],
        out_specs=[blk(w5), blk(w), blk(w), blk(w), _whole((_SG_ROWS, w))],
        out_shape=[jax.ShapeDtypeStruct((s, w5), BF16), jax.ShapeDtypeStruct((s, w), BF16),
                   jax.ShapeDtypeStruct((s, w), BF16), jax.ShapeDtypeStruct((s, w), BF16),
                   jax.ShapeDtypeStruct((_SG_ROWS, w), F32)],
        scratch_shapes=[big(), big(), big(), big(), big(), big(), big(), small(), small(), small()],
        compiler_params=_cp(("arbitrary",)),
    )(proj, proj, hseq, hseq, dy, conv_a, conv_b, bias, wa_blk, ba, wx_blk, bx, lam)


def _split_dot(v, tri2):
    hi = v.astype(BF16)
    lo = (v - hi.astype(F32)).astype(BF16)
    return jnp.dot(jnp.concatenate([hi, lo], axis=1), tri2, preferred_element_type=F32)


def _tri(cmp):
    r = lax.broadcasted_iota(jnp.int32, (LANES, LANES), 0)
    c = lax.broadcasted_iota(jnp.int32, (LANES, LANES), 1)
    return cmp(r, c).astype(BF16)


def _lane_blocks(v):
    return [v[:, b * LANES:(b + 1) * LANES] for b in range(v.shape[1] // LANES)]


def _last_lane(v):
    return jnp.broadcast_to(v[:, LANES - 1:LANES], v.shape)


def _scores(q, kb, scale):
    return lax.dot_general(q, kb, _DIMS["nt"], preferred_element_type=F32) * scale


def _log_gates(z, diagonal):
    ls = jnp.minimum(z, 0.0) - jnp.log(1.0 + jnp.exp(-jnp.abs(z)))
    ln = ls - z
    valid = None
    if diagonal:
        valid = (lax.broadcasted_iota(jnp.int32, z.shape, 1) < lax.broadcasted_iota(jnp.int32, z.shape, 0))
        ln = jnp.where(valid, ln, 0.0)
    return ls, ln, valid


def _attn_fwd(qkv, heads):
    s = qkv.shape[0]
    dh = LANES
    tq = _pick(s, ATT_TILE)
    nq = s // tq
    nb = tq // LANES
    scale = 1.0 / math.sqrt(dh)

    hp = ATT_FWD_HEADS_PER_STEP
    groups = heads // hp
    wid = hp * dh

    def body(q_ref, k_ref, v_ref, o_ref, tot_ref, acc_scr, car_scr):
        qi = pl.program_id(1)
        acc_scr[...] = jnp.zeros_like(acc_scr)
        car_scr[...] = jnp.zeros_like(car_scr)
        tri = _tri(lambda r, c: r > c)
        tri = jnp.concatenate([tri, tri], axis=0)

        def tile(kt, diagonal):
            k0 = pl.multiple_of(kt * tq, tq)
            heads_cols = [slice(hh * dh, (hh + 1) * dh) for hh in range(hp)]
            zs = [_scores(q_ref[:, cols], k_ref[pl.ds(k0, tq), cols], scale) for cols in heads_cols]
            gates = [_log_gates(z, diagonal) for z in zs]
            sfxs = [_split_dot(jnp.concatenate(_lane_blocks(ln), axis=0), tri) for _, ln, _ in gates]
            for cols, (ls, ln, valid), sfx in zip(heads_cols, gates, sfxs):
                blocks = _lane_blocks(ln)
                car = car_scr[:, cols]
                parts = [None] * nb
                for b in reversed(range(nb)):
                    sb = sfx[b * tq:(b + 1) * tq]
                    parts[b] = sb + car
                    car = car + (sb[:, 0:1] + blocks[b][:, 0:1])
                car_scr[:, cols] = car
                wgt = jnp.exp(ls + jnp.concatenate(parts, axis=1))
                if diagonal:
                    wgt = jnp.where(valid, wgt, 0.0)
                acc_scr[:, cols] += jnp.dot(
                    wgt.astype(BF16), v_ref[pl.ds(k0, tq), cols], preferred_element_type=F32)

        tile(qi, True)

        def step(j, carry):
            tile(qi - 1 - j, False)
            return carry

        lax.fori_loop(0, qi, step, 0)
        o_ref[...] = acc_scr[...].astype(BF16)
        tot_ref[...] = car_scr[...]

    return pl.pallas_call(
        body, name="attn_fwd", grid=(groups, nq),
        in_specs=[pl.BlockSpec((tq, wid), lambda h, i: (i, h)),
                  pl.BlockSpec((s, wid), lambda h, i: (0, groups + h)),
                  pl.BlockSpec((s, wid), lambda h, i: (0, 2 * groups + h))],
        out_specs=[pl.BlockSpec((tq, wid), lambda h, i: (i, h)), pl.BlockSpec((tq, wid), lambda h, i: (i, h))],
        out_shape=[jax.ShapeDtypeStruct((s, heads * dh), BF16), jax.ShapeDtypeStruct((s, heads * dh), F32)],
        scratch_shapes=[pltpu.VMEM((tq, wid), F32), pltpu.VMEM((tq, wid), F32)],
        compiler_params=_cp(("parallel", "arbitrary")),
    )(qkv, qkv, qkv)


def _attn_bwd(qkv, tot, do, heads):
    s = qkv.shape[0]
    dh = LANES
    tq = _pick(s, ATT_TILE)
    nq = s // tq
    nb = tq // LANES
    scale = 1.0 / math.sqrt(dh)

    hp = ATT_HEADS_PER_STEP
    groups = heads // hp
    wid = hp * dh

    def body(q_ref, k_ref, v_ref, tot_ref, do_ref, dq_ref, dk_ref, dv_ref,
             dq_scr, dk_scr, dv_scr, cl_scr, cg_scr):
        qi = pl.program_id(1)

        @pl.when(qi == 0)
        def _():
            dk_scr[...] = jnp.zeros_like(dk_scr)
            dv_scr[...] = jnp.zeros_like(dv_scr)

        dq_scr[...] = jnp.zeros_like(dq_scr)
        cl_scr[...] = jnp.zeros_like(cl_scr)
        cg_scr[...] = jnp.zeros_like(cg_scr)
        tri_le = _tri(lambda r, c: r <= c)
        tri_le = jnp.concatenate([tri_le, tri_le], axis=0)
        tri_lt = _tri(lambda r, c: r < c)

        def tile(kt, diagonal):
            k0 = pl.multiple_of(kt * tq, tq)
            heads_cols = [slice(hh * dh, (hh + 1) * dh) for hh in range(hp)]
            keys = pl.ds(k0, tq)
            zs = [_scores(q_ref[:, cols], k_ref[keys, cols], scale) for cols in heads_cols]
            dws = [lax.dot_general(do_ref[:, cols], v_ref[keys, cols], _DIMS["nt"], preferred_element_type=F32)
                   for cols in heads_cols]
            gates = [_log_gates(z, diagonal) for z in zs]
            pins = [_split_dot(jnp.concatenate(_lane_blocks(ln), axis=0), tri_le) for _, ln, _ in gates]
            wgts, gs = [], []
            for cols, (ls, _, valid), pin, dw in zip(heads_cols, gates, pins, dws):
                total = tot_ref[:, cols]
                cl = cl_scr[:, cols]
                parts = []
                for b in range(nb):
                    pb = pin[b * tq:(b + 1) * tq] + cl
                    parts.append(total - pb)
                    cl = _last_lane(pb)
                cl_scr[:, cols] = cl
                wgt = jnp.exp(ls + jnp.concatenate(parts, axis=1))
                if diagonal:
                    wgt = jnp.where(valid, wgt, 0.0)
                wgts.append(wgt)
                gs.append(wgt * dw)
            pexs = [jnp.dot(jnp.concatenate(_lane_blocks(g), axis=0).astype(BF16), tri_lt,
                            preferred_element_type=F32) for g in gs]
            for cols, wgt in zip(heads_cols, wgts):
                dv_scr[keys, cols] += lax.dot_general(
                    wgt.astype(BF16), do_ref[:, cols], _DIMS["tn"], preferred_element_type=F32)
            for cols, (ls, _, valid), g, pex in zip(heads_cols, gates, gs, pexs):
                gblocks = _lane_blocks(g)
                cg = cg_scr[:, cols]
                parts = []
                for b in range(nb):
                    pb = pex[b * tq:(b + 1) * tq] + cg
                    parts.append(pb)
                    cg = _last_lane(pb + gblocks[b])
                cg_scr[:, cols] = cg
                dz = g - jnp.exp(ls) * (g + jnp.concatenate(parts, axis=1))
                if diagonal:
                    dz = jnp.where(valid, dz, 0.0)
                dz = dz.astype(BF16)
                dq_scr[:, cols] += jnp.dot(dz, k_ref[keys, cols], preferred_element_type=F32)
                dk_scr[keys, cols] += lax.dot_general(
                    dz, q_ref[:, cols], _DIMS["tn"], preferred_element_type=F32)

        def step(j, carry):
            tile(j, False)
            return carry

        lax.fori_loop(0, qi, step, 0)
        tile(qi, True)
        dq_ref[...] = (dq_scr[...] * scale).astype(BF16)

        @pl.when(qi == nq - 1)
        def _():
            dk_ref[...] = (dk_scr[...] * scale).astype(BF16)
            dv_ref[...] = dv_scr[...].astype(BF16)

    qblk = pl.BlockSpec((tq, wid), lambda h, i: (i, h))
    hblk = pl.BlockSpec((s, wid), lambda h, i: (0, h))
    out = jax.ShapeDtypeStruct((s, heads * dh), BF16)
    return pl.pallas_call(
        body, name="attn_bwd", grid=(groups, nq),
        in_specs=[qblk, pl.BlockSpec((s, wid), lambda h, i: (0, groups + h)),
                  pl.BlockSpec((s, wid), lambda h, i: (0, 2 * groups + h)), qblk, qblk],
        out_specs=[qblk, hblk, hblk], out_shape=[out, out, out],
        scratch_shapes=[pltpu.VMEM((tq, wid), F32), pltpu.VMEM((s, wid), F32), pltpu.VMEM((s, wid), F32),
                        pltpu.VMEM((tq, wid), F32), pltpu.VMEM((tq, wid), F32)],
        compiler_params=_cp(("parallel", "arbitrary")),
    )(qkv, qkv, qkv, tot, do)


def _place():
    x, y, c = lax.axis_index("x"), lax.axis_index("y"), lax.axis_index("c")
    chips = [(1 - x, y), (x, 1 - y), (1 - x, 1 - y)]
    return x, y, c, chips


def _remote(src, dst, send_sem, recv_sem, dev):
    return pltpu.make_async_remote_copy(
        src_ref=src, dst_ref=dst, send_sem=send_sem, recv_sem=recv_sem, device_id=dev, device_id_type=MESH)


def _handshake(peers):
    barrier = pltpu.get_barrier_semaphore()
    for dev in peers:
        pl.semaphore_signal(barrier, inc=1, device_id=dev, device_id_type=MESH)
    pl.semaphore_wait(barrier, len(peers))


def _sequencer_kernel(name, n_sems, collective_id):
    return functools.partial(
        pl.kernel, mesh=plsc.ScalarSubcoreMesh(axis_name="seq", num_cores=1), name=name,
        scratch_types=(pltpu.SemaphoreType.DMA,) * n_sems,
        compiler_params=pltpu.CompilerParams(collective_id=collective_id))


def _allgather_async(name, slot_buf, collective_id):
    buf = jax.new_ref(slot_buf, memory_space=pltpu.MemorySpace.HBM)
    hr = slot_buf.shape[1] // 2

    @_sequencer_kernel(name, 12, collective_id)
    def launch(*sems):
        send_sems, recv_sems, fsend_sems, frecv_sems = sems[0:3], sems[3:6], sems[6:9], sems[9:12]
        x, y, c, chips = _place()
        me = 2 * x + y
        sibling = (x, y, 1 - c)
        _handshake([(px, py, c) for px, py in chips] + [sibling])
        mine = buf.at[me, pl.ds(c * hr, hr)]
        firsts = []
        for k, (px, py) in enumerate(chips):
            cp = _remote(mine, mine, send_sems[k], recv_sems[k], (px, py, c))
            cp.start()
            firsts.append(cp)
        passed = []
        for k, (px, py) in enumerate(chips):
            slot = buf.at[2 * px + py, pl.ds(c * hr, hr)]
            _remote(slot, slot, send_sems[k], recv_sems[k], (px, py, c)).wait_recv()
            cp = _remote(slot, slot, fsend_sems[k], frecv_sems[k], sibling)
            cp.start()
            passed.append(cp)
        for k, (px, py) in enumerate(chips):
            slot = buf.at[2 * px + py, pl.ds((1 - c) * hr, hr)]
            _remote(slot, slot, fsend_sems[k], frecv_sems[k], sibling).wait_recv()
        for cp in firsts + passed:
            cp.wait_send()

    launch()
    return buf[...]


def _to_sibling_async(name, slab):
    src = jax.new_ref(slab, memory_space=pltpu.MemorySpace.HBM)
    hr = slab.shape[1] // 2
    got = jax.empty_ref(jax.ShapeDtypeStruct((N_CHIPS, hr, slab.shape[2]), slab.dtype),
                        memory_space=pltpu.MemorySpace.HBM)

    @_sequencer_kernel(name, 2, COLLECTIVE_SIBLING)
    def launch(send_sem, recv_sem):
        x, y, c, _ = _place()
        _handshake([(x, y, 1 - c)])
        _remote(src.at[:, pl.ds((1 - c) * hr, hr), :], got, send_sem, recv_sem, (x, y, 1 - c)).start()
        _remote(got, got, send_sem, recv_sem, (x, y, 1 - c)).wait()

    launch()
    return src[...], got[...]


def _swap_with_sibling_async(name, part):
    src = jax.new_ref(part, memory_space=pltpu.MemorySpace.HBM)
    got = jax.empty_ref(jax.ShapeDtypeStruct(part.shape, part.dtype), memory_space=pltpu.MemorySpace.HBM)

    @_sequencer_kernel(name, 2, COLLECTIVE_SIBLING)
    def launch(send_sem, recv_sem):
        x, y, c, _ = _place()
        _handshake([(x, y, 1 - c)])
        cp = _remote(src, got, send_sem, recv_sem, (x, y, 1 - c))
        cp.start()
        cp.wait()

    launch()
    return got[...]


def _to_chips_async(name, part):
    src = jax.new_ref(part, memory_space=pltpu.MemorySpace.HBM)
    got = jax.empty_ref(jax.ShapeDtypeStruct((3,) + part.shape[1:], part.dtype), memory_space=pltpu.MemorySpace.HBM)

    @_sequencer_kernel(name, 6, COLLECTIVE_CHIPS)
    def launch(*sems):
        send_sems, recv_sems = sems[0:3], sems[3:6]
        x, y, c, chips = _place()
        _handshake([(px, py, c) for px, py in chips])
        cps = []
        for k, (px, py) in enumerate(chips):
            cp = _remote(src.at[2 * px + py], got.at[k], send_sems[k], recv_sems[k], (px, py, c))
            cp.start()
            cps.append(cp)
        for cp in cps:
            cp.wait()

    launch()
    return src[...], got[...]


def _join_sibling_async(name, half_filled):
    buf = jax.new_ref(half_filled, memory_space=pltpu.MemorySpace.HBM)
    hr = half_filled.shape[0] // 2

    @_sequencer_kernel(name, 2, COLLECTIVE_SIBLING)
    def launch(send_sem, recv_sem):
        x, y, c, _ = _place()
        _handshake([(x, y, 1 - c)])
        mine = buf.at[pl.ds(c * hr, hr)]
        other = buf.at[pl.ds((1 - c) * hr, hr)]
        cp = _remote(mine, mine, send_sem, recv_sem, (x, y, 1 - c))
        cp.start()
        _remote(other, other, send_sem, recv_sem, (x, y, 1 - c)).wait_recv()
        cp.wait_send()

    launch()
    return buf[...]


def _allgather_chips_small(name, v):
    r = v.shape[0]

    def body(v_ref, o_ref, send_sems, recv_sems):
        x, y, c, chips = _place()
        me = 2 * x + y
        o_ref[me] = v_ref[...]
        cps = []
        for k, (px, py) in enumerate(chips):
            cp = _remote(v_ref, o_ref.at[me], send_sems.at[k], recv_sems.at[k], (px, py, c))
            cp.start()
            cps.append(cp)
        for k, (px, py) in enumerate(chips):
            slot = o_ref.at[2 * px + py]
            _remote(slot, slot, send_sems.at[k], recv_sems.at[k], (px, py, c)).wait_recv()
        for cp in cps:
            cp.wait_send()

    return pl.pallas_call(
        body, name=name, in_specs=[pl.BlockSpec(memory_space=pltpu.VMEM)],
        out_specs=pl.BlockSpec(memory_space=pltpu.VMEM),
        out_shape=jax.ShapeDtypeStruct((N_CHIPS, r, LANES), F32),
        scratch_shapes=[pltpu.SemaphoreType.DMA((3,)), pltpu.SemaphoreType.DMA((3,))],
    )(v)


def _allreduce_small(name, v):
    r = v.shape[0]
    hr = r // 2
    assert hr % SUBLANES == 0

    def body(v_ref, o_ref, sib_ref, chips_ref, send_sems, recv_sems):
        x, y, c, chips = _place()
        me = 2 * x + y
        sibling = (x, y, 1 - c)
        first = _remote(v_ref, sib_ref, send_sems.at[0], recv_sems.at[0], sibling)
        first.start()
        first.wait()
        mine = pl.ds(pl.multiple_of(c * hr, SUBLANES), hr)
        chips_ref[me] = v_ref[mine, :] + sib_ref[mine, :]
        cps = []
        for k, (px, py) in enumerate(chips):
            cp = _remote(chips_ref.at[me], chips_ref.at[me], send_sems.at[1 + k], recv_sems.at[1 + k], (px, py, c))
            cp.start()
            cps.append(cp)
        for k, (px, py) in enumerate(chips):
            slot = chips_ref.at[2 * px + py]
            _remote(slot, slot, send_sems.at[1 + k], recv_sems.at[1 + k], (px, py, c)).wait_recv()
        total = chips_ref[0]
        for j in range(1, N_CHIPS):
            total = total + chips_ref[j]
        o_ref[mine, :] = total
        last = _remote(o_ref.at[mine], o_ref.at[mine], send_sems.at[4], recv_sems.at[4], sibling)
        last.start()
        other = o_ref.at[pl.ds(pl.multiple_of((1 - c) * hr, SUBLANES), hr)]
        _remote(other, other, send_sems.at[4], recv_sems.at[4], sibling).wait_recv()
        last.wait_send()
        for cp in cps:
            cp.wait_send()

    return pl.pallas_call(
        body, name=name, in_specs=[pl.BlockSpec(memory_space=pltpu.VMEM)],
        out_specs=pl.BlockSpec(memory_space=pltpu.VMEM),
        out_shape=jax.ShapeDtypeStruct((r, LANES), F32),
        scratch_shapes=[pltpu.VMEM((r, LANES), F32), pltpu.VMEM((N_CHIPS, hr, LANES), F32),
                        pltpu.SemaphoreType.DMA((5,)), pltpu.SemaphoreType.DMA((5,))],
    )(v)


def _add_sibling(name, slabs, recv, c):
    _, r, cols = slabs.shape
    hr = r // 2
    tr = _pick(hr, STREAM_TILE)
    nb = hr // tr

    def body(c_ref, a_ref, b_ref, o_ref):
        o_ref[...] = (a_ref[...].astype(F32) + b_ref[...].astype(F32)).astype(BF16)

    grid_spec = pltpu.PrefetchScalarGridSpec(
        num_scalar_prefetch=1, grid=(N_CHIPS, nb),
        in_specs=[pl.BlockSpec((None, tr, cols), lambda j, i, c_ref: (j, c_ref[0] * nb + i, 0)),
                  pl.BlockSpec((None, tr, cols), lambda j, i, c_ref: (j, i, 0))],
        out_specs=pl.BlockSpec((None, tr, cols), lambda j, i, c_ref: (j, i, 0)))
    return pl.pallas_call(
        body, name=name, grid_spec=grid_spec,
        out_shape=jax.ShapeDtypeStruct((N_CHIPS, hr, cols), BF16),
        compiler_params=_cp(("parallel", "parallel")))(jnp.reshape(c, (1,)).astype(jnp.int32), slabs, recv)


def _sum_chips(name, own, recv, chip, c):
    _, hr, cols = recv.shape
    tr = _pick(hr, STREAM_TILE // 2)
    nb = hr // tr

    def body(sc_ref, own_ref, recv_ref, o_ref):
        total = own_ref[...].astype(F32)
        for k in range(3):
            total = total + recv_ref[k].astype(F32)
        o_ref[...] = total

    grid_spec = pltpu.PrefetchScalarGridSpec(
        num_scalar_prefetch=1, grid=(nb,),
        in_specs=[pl.BlockSpec((None, tr, cols), lambda i, sc: (sc[0], i, 0)),
                  pl.BlockSpec((3, tr, cols), lambda i, sc: (0, i, 0))],
        out_specs=pl.BlockSpec((tr, cols), lambda i, sc: (sc[1] * nb + i, 0)))
    return pl.pallas_call(
        body, name=name, grid_spec=grid_spec, out_shape=jax.ShapeDtypeStruct((2 * hr, cols), F32),
        compiler_params=_cp(("parallel",)))(jnp.stack([chip, c]).astype(jnp.int32), own, recv)


def _adamw_math(w, g, m, v):
    m = ADAM_B1 * m + (1.0 - ADAM_B1) * g
    v = ADAM_B2 * v + (1.0 - ADAM_B2) * (g * g)
    m_hat = m / (1.0 - ADAM_B1 ** ADAM_STEP)
    v_hat = v / (1.0 - ADAM_B2 ** ADAM_STEP)
    delta = -ADAM_LR * (m_hat / (jnp.sqrt(v_hat) + ADAM_EPS) + ADAM_WD * w)
    return delta, m, v


def _adamw(name, w, gs, m, v):
    nl, r, cols = w.shape
    tr = _pick(r, ROW_TILE)

    def body(*refs):
        w_ref, m_ref, v_ref = refs[0:3]
        g_refs = refs[3:3 + nl]
        go_ref, d_ref, nm_ref, nv_ref = refs[3 + nl:]
        layer = pl.program_id(0)
        g = g_refs[0][...]
        for j in range(1, nl):
            g = jnp.where(layer == j, g_refs[j][...], g)
        d, nm, nv = _adamw_math(w_ref[...], g, m_ref[...], v_ref[...])
        go_ref[...] = g
        d_ref[...] = d
        nm_ref[...] = nm
        nv_ref[...] = nv

    spec3 = pl.BlockSpec((None, tr, cols), lambda l, i: (l, i, 0))
    gspec = pl.BlockSpec((tr, cols), lambda l, i: (i, 0))
    out = jax.ShapeDtypeStruct((nl, r, cols), F32)
    return pl.pallas_call(
        body, name=name, grid=(nl, r // tr), in_specs=[spec3] * 3 + [gspec] * nl, out_specs=[spec3] * 4,
        out_shape=[out] * 4, compiler_params=_cp(("parallel", "parallel")))(w, m, v, *gs)


def _adamw_small(name, groups):
    n = len(groups)
    flat = [a for grp in groups for a in grp]

    def body(*refs):
        ins, outs = refs[:4 * n], refs[4 * n:]
        for p in range(n):
            w_ref, g_ref, m_ref, v_ref = ins[4 * p:4 * p + 4]
            d, nm, nv = _adamw_math(w_ref[...], g_ref[...], m_ref[...], v_ref[...])
            outs[3 * p][...] = d
            outs[3 * p + 1][...] = nm
            outs[3 * p + 2][...] = nv

    vm = pl.BlockSpec(memory_space=pltpu.VMEM)
    out_shape = [jax.ShapeDtypeStruct(grp[0].shape, F32) for grp in groups for _ in range(3)]
    res = pl.pallas_call(
        body, name=name, in_specs=[vm] * (4 * n), out_specs=[vm] * (3 * n), out_shape=out_shape)(*flat)
    return [tuple(res[3 * p:3 * p + 3]) for p in range(n)]


def _block_diag_pairs(w):
    h, d, _ = w.shape
    z = jnp.zeros((h // 2, d, d), w.dtype)
    top = jnp.concatenate([w[0::2], z], axis=2)
    bot = jnp.concatenate([z, w[1::2]], axis=2)
    return jnp.concatenate([top, bot], axis=1).astype(BF16)


def _diag_pairs_to_heads(g, d):
    a = g[:, :d, :d]
    b = g[:, d:, d:]
    return jnp.stack([a, b], axis=1).reshape(-1, d, d)


def _rows128(a):
    flat = a.reshape(-1, LANES)
    pad = (-flat.shape[0]) % SUBLANES
    if pad:
        flat = jnp.concatenate([flat, jnp.zeros((pad, LANES), flat.dtype)], axis=0)
    return flat


def _unshard_last(g4, shape):
    g4 = g4.reshape((N_CHIPS,) + tuple(shape))
    return jnp.concatenate([g4[j] for j in range(N_CHIPS)], axis=-1)


def kernel(x, norm_gains, hyb_w_in, hyb_conv_a, hyb_conv_b, hyb_conv_b_bias, hyb_rg_w_a, hyb_rg_b_a, hyb_rg_w_x, hyb_rg_b_x, hyb_rg_lambda, hyb_w_out, sb_w_qkv, sb_w_o, mlp_w_up, mlp_w_down, loss_target, m_norm_gains, m_hyb_w_in, m_hyb_conv_a, m_hyb_conv_b, m_hyb_conv_b_bias, m_hyb_rg_w_a, m_hyb_rg_b_a, m_hyb_rg_w_x, m_hyb_rg_b_x, m_hyb_rg_lambda, m_hyb_w_out, m_sb_w_qkv, m_sb_w_o, m_mlp_w_up, m_mlp_w_down, v_norm_gains, v_hyb_w_in, v_hyb_conv_a, v_hyb_conv_b, v_hyb_conv_b_bias, v_hyb_rg_w_a, v_hyb_rg_b_a, v_hyb_rg_w_x, v_hyb_rg_b_x, v_hyb_rg_lambda, v_hyb_w_out, v_sb_w_qkv, v_sb_w_o, v_mlp_w_up, v_mlp_w_down):
    cx_ = lax.axis_index("x")
    cy_ = lax.axis_index("y")
    cc_ = lax.axis_index("c")
    chip = 2 * cx_ + cy_

    x0 = x[0]
    target = loss_target[0]
    s, d = x0.shape
    heads = SB_HEADS
    assert d // heads == LANES
    n_rg, hd = hyb_rg_w_a.shape[1], hyb_rg_w_a.shape[2]
    wmix = n_rg * hd
    assert 2 * hd == LANES

    big = {
        "hyb_w_in": (hyb_w_in, 0), "hyb_w_out": (hyb_w_out, 0), "mlp_w_up0": (mlp_w_up, 0),
        "mlp_w_down0": (mlp_w_down, 0), "sb_w_qkv": (sb_w_qkv, 0), "sb_w_o": (sb_w_o, 0),
        "mlp_w_up1": (mlp_w_up, 1), "mlp_w_down1": (mlp_w_down, 1),
    }
    names = list(big)
    slots = [_cast_into_slot("cast_" + k, big[k][0], big[k][1], chip) for k in names]
    full = {k: _allgather_async("allgather_" + k, slot, cid) for cid, (k, slot) in enumerate(zip(names, slots))}
    rowsharded = lambda k: full[k].reshape(-1, full[k].shape[2])

    ng_s, ca_s, cb_s = norm_gains.reshape(-1, norm_gains.shape[2]), hyb_conv_a[0], hyb_conv_b[0]
    packed = jnp.concatenate([_rows128(ng_s), _rows128(ca_s), _rows128(cb_s)], axis=0)
    gathered = _allgather_chips_small("allgather_small", packed)
    n0 = ng_s.size // LANES
    n1 = n0 + (-n0) % SUBLANES
    m0 = ca_s.size // LANES
    m1 = m0 + (-m0) % SUBLANES
    k0 = cb_s.size // LANES
    gains = _unshard_last(gathered[:, 0:n0], ng_s.shape).reshape(2, 4, 1, d)
    conv_a = _unshard_last(gathered[:, n1:n1 + m0], ca_s.shape)
    conv_b = _unshard_last(gathered[:, n1 + m1:n1 + m1 + k0], cb_s.shape)
    bias, b_a, b_x, lam = hyb_conv_b_bias, hyb_rg_b_a, hyb_rg_b_x, hyb_rg_lambda
    wa_blk = _block_diag_pairs(hyb_rg_w_a[0])
    wx_blk = _block_diag_pairs(hyb_rg_w_x[0])

    relu_sq = lambda acc: (jnp.maximum(acc, 0.0), jnp.square(jnp.maximum(acc, 0.0)))

    h1 = _rms_fwd("rms_pre0", x0, gains[0, 0])
    proj = _mm_fwd_col("proj_in", h1, full["hyb_w_in"])[0]
    ycat, hseq = _mixer_fwd(proj, conv_a, conv_b, bias, wa_blk, b_a, wx_blk, b_x, lam)
    mix0 = _mm_fwd_row("proj_out", ycat, rowsharded("hyb_w_out"))
    x1, h2 = _rms_post("rms_mix0", mix0, gains[0, 1], x0, gains[0, 2])
    u0, a0 = _mm_fwd_col("mlp_up0", h2, full["mlp_w_up0"], (BF16, BF16), relu_sq)
    mlp0 = _mm_fwd_row("mlp_down0", a0, rowsharded("mlp_w_down0"))
    x2, h3 = _rms_post("rms_mlp0", mlp0, gains[0, 3], x1, gains[1, 0])

    qkv = _mm_fwd_col("qkv", h3, full["sb_w_qkv"], (BF16,))[0]
    att, tot = _attn_fwd(qkv, heads)
    mix1 = _mm_fwd_row("attn_out", att, rowsharded("sb_w_o"))
    x3, h4 = _rms_post("rms_mix1", mix1, gains[1, 1], x2, gains[1, 2])
    u1, a1 = _mm_fwd_col("mlp_up1", h4, full["mlp_w_up1"], (BF16, BF16), relu_sq)
    mlp1 = _mm_fwd_row("mlp_down1", a1, rowsharded("mlp_w_down1"))
    dy, dmlp1, dgain_mlp1, loss_local = _last_norm_and_loss("last_norm_loss", mlp1, gains[1, 3], x3, target)
    loss = lax.psum(loss_local, ("x", "y", "c"))

    dgain = [[None] * 4 for _ in range(2)]
    drelu = lambda acc, u: (acc * (2.0 * u.astype(F32)),)
    stage_a, stage_b, gfull = {}, {}, {}

    def tie(main, side):
        return lax.optimization_barrier((main, side))

    def reduce_start(k, slab, main):
        main, slab = tie(main, slab)
        stage_a[k] = _to_sibling_async("grads_to_sibling_" + k, slab)
        return main

    def reduce_to_chips(k, main):
        slab, from_sibling = stage_a.pop(k)
        main, part = tie(main, _add_sibling("grads_add_" + k, slab, from_sibling, cc_))
        stage_b[k] = _to_chips_async("grads_to_chips_" + k, part)
        return main

    def reduce_split_start(k, act, dy, cs, main):
        main, other = tie(main, _mm_wgrad_half(k + "_wgrad_sibling_rows", act, dy, 1 - cc_, cs))
        stage_a[k] = (act, dy, cs, _swap_with_sibling_async("grads_to_sibling_" + k, other))
        return main

    def reduce_split_to_chips(k, main):
        act, dy, cs, from_sibling = stage_a.pop(k)
        main, part = tie(main, _mm_wgrad_half(k + "_wgrad_my_rows", act, dy, cc_, cs, init=from_sibling))
        stage_b[k] = _to_chips_async("grads_to_chips_" + k, part)
        return main

    def after(value, token):
        return tie(value, token)[0]

    def reduce_finish(k, main):
        own, from_chips = stage_b.pop(k)
        main, half = tie(main, _sum_chips("grads_sum_" + k, after(own, main), from_chips, chip, cc_))
        gfull[k] = _join_sibling_async("grads_join_" + k, half)
        return main

    def mlp_bwd(layer, dxo, dmlp, xin, hin, u, a, mix):
        down, up = f"mlp_w_down{layer}", f"mlp_w_up{layer}"
        wd, wu = rowsharded(down), full[up]
        dmlp = reduce_split_start(down, a, dmlp, None, dmlp)
        du = _mm_bwd_row(f"mlp_down{layer}_bwd", dmlp, wd, (BF16,), u, drelu)[0]
        du = reduce_split_start(up, hin, du, wu.shape[2], du)
        du = reduce_split_to_chips(down, du)
        dh = _mm_bwd_col(f"mlp_up{layer}_bwd", du, wu)
        dh = reduce_split_to_chips(up, dh)
        dxm, dgain[layer][2], dmix, dgain[layer][1] = _rms_bwd_pair(
            f"rms_premlp{layer}_mix{layer}_bwd", xin, gains[layer, 2], dh, dxo, mix, gains[layer, 1])
        return dxm, dmix

    dgain[1][3] = dgain_mlp1
    dx3, dmix1 = mlp_bwd(1, dy, dmlp1, x3, h4, u1, a1, mix1)
    dmix1 = reduce_start("sb_w_o", _mm_wgrad_row("attn_out_wgrad", att, dmix1).reshape(N_CHIPS, -1, d), dmix1)
    datt = _mm_bwd_row("attn_out_bwd", dmix1, rowsharded("sb_w_o"), (BF16,))[0]
    dq, dk, dv = _attn_bwd(qkv, tot, datt, heads)
    dqkv = jnp.concatenate([dq, dk, dv], axis=1)
    dqkv = reduce_to_chips("sb_w_o", dqkv)
    dqkv = reduce_finish("mlp_w_down1", dqkv)
    dqkv = reduce_finish("mlp_w_up1", dqkv)
    dqkv = reduce_split_start("sb_w_qkv", h3, dqkv, full["sb_w_qkv"].shape[2], dqkv)
    dh3 = _mm_bwd_col("qkv_bwd", dqkv, full["sb_w_qkv"])
    dh3 = reduce_split_to_chips("sb_w_qkv", dh3)
    dx2, dgain[1][0], dmlp0, dgain[0][3] = _rms_bwd_pair(
        "rms_pre1_mlp0_bwd", x2, gains[1, 0], dh3, dx3, mlp0, gains[0, 3])

    dx1, dmix0 = mlp_bwd(0, dx2, dmlp0, x1, h2, u0, a0, mix0)
    dmix0 = reduce_finish("sb_w_o", dmix0)
    dmix0 = reduce_finish("sb_w_qkv", dmix0)
    dmix0 = reduce_finish("mlp_w_down0", dmix0)
    dmix0 = reduce_start("hyb_w_out", _mm_wgrad_row("proj_out_wgrad", ycat, dmix0).reshape(N_CHIPS, -1, d), dmix0)
    dycat = _mm_bwd_row("proj_out_bwd", dmix0, rowsharded("hyb_w_out"))[0]
    dproj, xr_b, dpa_b, dpx_b, sg = _mixer_bwd(
        proj, hseq, dycat, conv_a, conv_b, bias, wa_blk, b_a, wx_blk, b_x, lam)
    dproj = reduce_finish("mlp_w_up0", dproj)
    dproj = reduce_to_chips("hyb_w_out", dproj)
    dproj = reduce_split_start("hyb_w_in", h1, dproj, full["hyb_w_in"].shape[2], dproj)
    dh1 = _mm_bwd_col("proj_in_bwd", dproj, full["hyb_w_in"])
    dh1 = reduce_split_to_chips("hyb_w_in", dh1)
    dx0, dgain[0][0] = _rms_bwd("rms_pre0_bwd", x0, gains[0, 0], dh1, res=dx1)
    dwa = _diag_pairs_to_heads(_mm_wgrad_diag("rg_w_a_wgrad", xr_b, dpa_b), hd)
    dwx = _diag_pairs_to_heads(_mm_wgrad_diag("rg_w_x_wgrad", xr_b, dpx_b), hd)

    dgains = jnp.concatenate([dgain[l][k] for l in range(2) for k in range(4)], axis=0)
    small_parts = [dgains, sg[_SG_CONV_A:_SG_CONV_A + 3], sg[_SG_CONV_B:_SG_CONV_B + 4], sg[_SG_BIAS:_SG_BIAS + 1],
                   dwa, sg[_SG_BA:_SG_BA + 1], dwx, sg[_SG_BX:_SG_BX + 1], sg[_SG_LAM:_SG_LAM + 1]]
    small_rows = [_rows128(p) for p in small_parts]
    n_small = sum(rws.shape[0] for rws in small_rows)
    tail_pad = [jnp.zeros(((-n_small) % (2 * SUBLANES), LANES), F32)] if n_small % (2 * SUBLANES) else []
    reduced = _allreduce_small("allreduce_small", jnp.concatenate(small_rows + tail_pad, axis=0))
    small_full, off = [], 0
    for p, rws in zip(small_parts, small_rows):
        small_full.append(reduced[off:off + p.size // LANES].reshape(p.shape))
        off += rws.shape[0]
    g_gains, g_ca, g_cb, g_bias, g_wa, g_ba, g_wx, g_bx, g_lam = small_full

    def my_cols(g, width):
        return lax.dynamic_slice_in_dim(g, chip * width, width, axis=g.ndim - 1)

    small = [
        ("norm_gains", norm_gains, my_cols(g_gains, norm_gains.shape[2]).reshape(norm_gains.shape),
         m_norm_gains, v_norm_gains),
        ("hyb_conv_a", hyb_conv_a, my_cols(g_ca, hyb_conv_a.shape[2])[None], m_hyb_conv_a, v_hyb_conv_a),
        ("hyb_conv_b", hyb_conv_b, my_cols(g_cb, hyb_conv_b.shape[2])[None], m_hyb_conv_b, v_hyb_conv_b),
        ("hyb_conv_b_bias", hyb_conv_b_bias, g_bias, m_hyb_conv_b_bias, v_hyb_conv_b_bias),
        ("hyb_rg_w_a", hyb_rg_w_a, g_wa[None], m_hyb_rg_w_a, v_hyb_rg_w_a),
        ("hyb_rg_b_a", hyb_rg_b_a, g_ba, m_hyb_rg_b_a, v_hyb_rg_b_a),
        ("hyb_rg_w_x", hyb_rg_w_x, g_wx[None], m_hyb_rg_w_x, v_hyb_rg_w_x),
        ("hyb_rg_b_x", hyb_rg_b_x, g_bx, m_hyb_rg_b_x, v_hyb_rg_b_x),
        ("hyb_rg_lambda", hyb_rg_lambda, g_lam, m_hyb_rg_lambda, v_hyb_rg_lambda),
    ]
    to2d = lambda a: a.reshape(-1, a.shape[-1])
    small_res = _adamw_small("adamw_small", [tuple(to2d(a) for a in (w, g, m, v)) for _, w, g, m, v in small])
    out = {}
    for (nm, w, g, _, _), (dl, nmom, nvar) in zip(small, small_res):
        out[nm] = (g, dl.reshape(w.shape), nmom.reshape(w.shape), nvar.reshape(w.shape))

    stacked = {
        "mlp_w_down": (mlp_w_down, m_mlp_w_down, v_mlp_w_down, ["mlp_w_down0", "mlp_w_down1"]),
        "mlp_w_up": (mlp_w_up, m_mlp_w_up, v_mlp_w_up, ["mlp_w_up0", "mlp_w_up1"]),
        "sb_w_o": (sb_w_o, m_sb_w_o, v_sb_w_o, ["sb_w_o"]),
        "sb_w_qkv": (sb_w_qkv, m_sb_w_qkv, v_sb_w_qkv, ["sb_w_qkv"]),
        "hyb_w_out": (hyb_w_out, m_hyb_w_out, v_hyb_w_out, ["hyb_w_out"]),
        "hyb_w_in": (hyb_w_in, m_hyb_w_in, v_hyb_w_in, ["hyb_w_in"]),
    }

    def update(k, token):
        w, m, v, parts = stacked[k]
        out[k] = tuple(_adamw("adamw_" + k, w, [after(gfull[p], token) for p in parts], m, v))
        return out[k][1]

    token = small_res[0][0]
    token = update("sb_w_qkv", token)
    token = update("sb_w_o", token)
    token = update("mlp_w_down", token)
    token = reduce_finish("hyb_w_out", token)
    token = update("mlp_w_up", token)
    token = reduce_finish("hyb_w_in", token)
    token = update("hyb_w_out", token)
    update("hyb_w_in", token)

    order = ["norm_gains", "hyb_w_in", "hyb_conv_a", "hyb_conv_b", "hyb_conv_b_bias", "hyb_rg_w_a", "hyb_rg_b_a",
             "hyb_rg_w_x", "hyb_rg_b_x", "hyb_rg_lambda", "hyb_w_out", "sb_w_qkv", "sb_w_o", "mlp_w_up",
             "mlp_w_down"]
    return (loss, dx0[None], *[out[k][0] for k in order], *[out[k][1] for k in order],
            *[out[k][2] for k in order], *[out[k][3] for k in order])
```

```python
import functools
import math

import jax
import jax.numpy as jnp
from jax import lax
from jax.experimental import pallas as pl
from jax.experimental.pallas import tpu as pltpu
from jax.experimental.pallas import tpu_sc as plsc

F32 = jnp.float32
BF16 = jnp.bfloat16
MESH = pl.DeviceIdType.MESH

SB_HEADS = 16
NORM_EPS = 1e-6
LRU_C = 8.0
ADAM_LR = 0.001
ADAM_B1 = 0.9
ADAM_B2 = 0.999
ADAM_EPS = 1e-08
ADAM_WD = 0.01
ADAM_STEP = 10

LANES = 128
SUBLANES = 8
VMEM_LIMIT = 48 * 1024 * 1024
MM_TILE = 1024
MM_VMEM_BUDGET = 40 * 1024 * 1024
MM_TILE_N = 1280
MM_TILE_K = 2048
ROW_TILE = 256
STREAM_TILE = 1024
ATT_TILE = 512
ATT_HEADS_PER_STEP = 2
ATT_FWD_HEADS_PER_STEP = 4
N_CHIPS = 4
COLLECTIVE_SIBLING = 8
COLLECTIVE_CHIPS = 9

_DIMS = {
    "nn": (((1,), (0,)), ((), ())),
    "nt": (((1,), (1,)), ((), ())),
    "tn": (((0,), (0,)), ((), ())),
}


def _cp(sem=None, vmem=VMEM_LIMIT):
    return pltpu.CompilerParams(dimension_semantics=sem, vmem_limit_bytes=vmem)


def _pick(dim, pref):
    t = min(dim, pref)
    while dim % t:
        t -= LANES
    return t


def _whole(shape):
    nd = len(shape)
    return pl.BlockSpec(tuple(shape), lambda *_: (0,) * nd)


def _sigmoid(z):
    return 1.0 / (1.0 + jnp.exp(-z))


def _log_sigmoid(z):
    return jnp.minimum(z, 0.0) - jnp.log(1.0 + jnp.exp(-jnp.abs(z)))


def _expm1(z):
    series = z * (1.0 + z * (0.5 + z * (1.0 / 6.0 + z * (1.0 / 24.0))))
    return jnp.where(jnp.abs(z) < 0.05, series, jnp.exp(z) - 1.0)


_GELU_C = math.sqrt(2.0 / math.pi)


def _gelu_and_grad(g):
    inner = _GELU_C * (g + 0.044715 * g * g * g)
    t = jnp.tanh(inner)
    val = 0.5 * g * (1.0 + t)
    grad = 0.5 * (1.0 + t) + 0.5 * g * (1.0 - t * t) * _GELU_C * (1.0 + 3.0 * 0.044715 * g * g)
    return val, grad


def _shift_down(cur, prev8, k, rows):
    n = cur.shape[0]
    rolled = pltpu.roll(cur, k, 0)
    head = jnp.tile(pltpu.roll(prev8, k, 0), (n // SUBLANES, 1))
    return jnp.where(rows < k, head, rolled)


def _shift_up(cur, next8, k, rows):
    n = cur.shape[0]
    rolled = pltpu.roll(cur, n - k, 0)
    tail = jnp.tile(pltpu.roll(next8, SUBLANES - k, 0), (n // SUBLANES, 1))
    return jnp.where(rows >= n - k, tail, rolled)


def _colsum(v):
    return jnp.sum(v, axis=0, keepdims=True)


def _matmul(name, mode, grid, operands, in_specs, out_shapes, out_specs, acc_shape, epilogue=None):
    nk = grid[2]
    n_in = len(operands)
    dims = _DIMS[mode]

    def finish(acc, extra, outs):
        res = epilogue(acc, *[e[...] for e in extra]) if epilogue is not None else (acc,)
        for o_ref, o in zip(outs, res):
            o_ref[...] = o.astype(o_ref.dtype)

    def product(a_ref, b_ref):
        return lax.dot_general(a_ref[...].astype(BF16), b_ref[...].astype(BF16), dims, preferred_element_type=F32)

    def body_single(*refs):
        finish(product(refs[0], refs[1]), refs[2:n_in], refs[n_in:])

    def body(*refs):
        extra = refs[2:n_in]
        outs = refs[n_in:-1]
        acc_ref = refs[-1]
        k = pl.program_id(2)

        @pl.when(k == 0)
        def _():
            acc_ref[...] = product(refs[0], refs[1])

        @pl.when(k > 0)
        def _():
            acc_ref[...] += product(refs[0], refs[1])

        @pl.when(k == nk - 1)
        def _():
            finish(acc_ref[...], extra, outs)

    return pl.pallas_call(
        body_single if nk == 1 else body, name=name, grid=grid, in_specs=in_specs, out_specs=out_specs,
        out_shape=out_shapes, scratch_shapes=[] if nk == 1 else [pltpu.VMEM(acc_shape, F32)],
        compiler_params=_cp(("parallel", "parallel", "arbitrary")),
    )(*operands)


def _pick_m(m, tk, tn, a_dtype, b_dtype, out_dtypes, extra_dtypes=()):
    size = lambda dt: jnp.dtype(dt).itemsize
    per_row = 2 * tk * size(a_dtype) + tn * (2 * sum(size(dt) for dt in tuple(out_dtypes) + tuple(extra_dtypes)) + 4)
    fixed = 2 * tk * tn * size(b_dtype)
    tm = _pick(m, MM_TILE)
    while tm > LANES and tm * per_row + fixed > MM_VMEM_BUDGET:
        tm = _pick(m, tm // 2)
    return tm


def _mm_fwd_col(name, a, wfull, out_dtypes=(F32,), epilogue=None):
    s, kdim = a.shape
    _, _, cs = wfull.shape
    tk, tn = _pick(kdim, MM_TILE_K), _pick(cs, MM_TILE_N)
    tm = _pick_m(s, tk, tn, a.dtype, wfull.dtype, out_dtypes)
    nbj = cs // tn
    grid = (s // tm, N_CHIPS * nbj, kdim // tk)
    out_shapes = [jax.ShapeDtypeStruct((s, N_CHIPS * cs), dt) for dt in out_dtypes]
    out_specs = [pl.BlockSpec((tm, tn), lambda i, n, k: (i, n)) for _ in out_dtypes]
    return _matmul(
        name, "nn", grid, [a, wfull],
        [pl.BlockSpec((tm, tk), lambda i, n, k: (i, k)),
         pl.BlockSpec((None, tk, tn), lambda i, n, k: (n // nbj, k, n % nbj))],
        out_shapes, out_specs, (tm, tn), epilogue)


def _mm_fwd_row(name, a, w2d, out_dtype=F32):
    s, kdim = a.shape
    _, n_out = w2d.shape
    tk, tn = _pick(kdim, MM_TILE_K), _pick(n_out, MM_TILE)
    tm = _pick_m(s, tk, tn, a.dtype, w2d.dtype, (out_dtype,))
    grid = (s // tm, n_out // tn, kdim // tk)
    return _matmul(
        name, "nn", grid, [a, w2d],
        [pl.BlockSpec((tm, tk), lambda i, n, k: (i, k)),
         pl.BlockSpec((tk, tn), lambda i, n, k: (k, n))],
        [jax.ShapeDtypeStruct((s, n_out), out_dtype)],
        [pl.BlockSpec((tm, tn), lambda i, n, k: (i, n))], (tm, tn))[0]


def _mm_bwd_col(name, dy, wfull, out_dtype=F32):
    s, _ = dy.shape
    _, kdim, cs = wfull.shape
    tn, tk = _pick(kdim, MM_TILE), _pick(cs, MM_TILE_K)
    tm = _pick_m(s, tk, tn, dy.dtype, wfull.dtype, (out_dtype,))
    nbj = cs // tk
    grid = (s // tm, kdim // tn, N_CHIPS * nbj)
    return _matmul(
        name, "nt", grid, [dy, wfull],
        [pl.BlockSpec((tm, tk), lambda i, n, k: (i, k)),
         pl.BlockSpec((None, tn, tk), lambda i, n, k: (k // nbj, n, k % nbj))],
        [jax.ShapeDtypeStruct((s, kdim), out_dtype)],
        [pl.BlockSpec((tm, tn), lambda i, n, k: (i, n))], (tm, tn))[0]


def _mm_bwd_row(name, dy, w2d, out_dtypes=(F32,), extra=None, epilogue=None):
    s, n_in = dy.shape
    kdim, _ = w2d.shape
    tn, tk = _pick(kdim, MM_TILE), _pick(n_in, MM_TILE_K)
    tm = _pick_m(s, tk, tn, dy.dtype, w2d.dtype, out_dtypes, () if extra is None else (extra.dtype,))
    grid = (s // tm, kdim // tn, n_in // tk)
    operands = [dy, w2d]
    in_specs = [pl.BlockSpec((tm, tk), lambda i, n, k: (i, k)),
                pl.BlockSpec((tn, tk), lambda i, n, k: (n, k))]
    if extra is not None:
        operands.append(extra)
        in_specs.append(pl.BlockSpec((tm, tn), lambda i, n, k: (i, n)))
    return _matmul(
        name, "nt", grid, operands, in_specs,
        [jax.ShapeDtypeStruct((s, kdim), dt) for dt in out_dtypes],
        [pl.BlockSpec((tm, tn), lambda i, n, k: (i, n)) for _ in out_dtypes], (tm, tn), epilogue)


def _mm_wgrad_row(name, a, dy):
    s, kdim = a.shape
    _, n_out = dy.shape
    tn, ts = _pick(n_out, MM_TILE), _pick(s, MM_TILE_K)
    tm = _pick_m(kdim, ts, tn, a.dtype, dy.dtype, (BF16,))
    grid = (kdim // tm, n_out // tn, s // ts)
    return _matmul(
        name, "tn", grid, [a, dy],
        [pl.BlockSpec((ts, tm), lambda i, n, k: (k, i)),
         pl.BlockSpec((ts, tn), lambda i, n, k: (k, n))],
        [jax.ShapeDtypeStruct((kdim, n_out), BF16)],
        [pl.BlockSpec((tm, tn), lambda i, n, k: (i, n))], (tm, tn))[0]


def _mm_wgrad_half(name, a, dy, half, cs=None, init=None):
    s, kdim = a.shape
    ts = _pick(s, 2 * MM_TILE_K)
    nk = s // ts
    if cs is not None:
        hr, cols = kdim // 2, cs
        tn = _pick(cs, MM_TILE_N)
        tm = _pick_m(hr, ts, tn, a.dtype, dy.dtype, (BF16,), (BF16,))
        ni, nbj = hr // tm, cs // tn
        grid = (ni, N_CHIPS * nbj, nk)
        a_map = lambda i, n, k, h: (k, h[0] * ni + i)
        o_map = lambda i, n, k, h: (n // nbj, i, n % nbj)
    else:
        hr, cols = kdim // N_CHIPS // 2, dy.shape[1]
        tn = _pick(cols, MM_TILE)
        tm = _pick_m(hr, ts, tn, a.dtype, dy.dtype, (BF16,), (BF16,))
        ni = hr // tm
        grid = (N_CHIPS * ni, cols // tn, nk)
        a_map = lambda i, n, k, h: (k, (i // ni) * 2 * ni + h[0] * ni + i % ni)
        o_map = lambda i, n, k, h: (i // ni, i % ni, n)
    with_init = init is not None

    def body_single(*refs):
        prod = lax.dot_general(refs[1][...].astype(BF16), refs[2][...].astype(BF16), _DIMS["tn"],
                               preferred_element_type=F32)
        if with_init:
            prod = prod + refs[3][...].astype(F32)
        refs[-1][...] = prod.astype(BF16)

    def body(*refs):
        a_ref, b_ref = refs[1], refs[2]
        init_ref = refs[3] if with_init else None
        o_ref, acc_ref = refs[-2], refs[-1]
        k = pl.program_id(2)

        def product():
            return lax.dot_general(a_ref[...].astype(BF16), b_ref[...].astype(BF16), _DIMS["tn"],
                                   preferred_element_type=F32)

        @pl.when(k == 0)
        def _():
            if with_init:
                acc_ref[...] = init_ref[...].astype(F32)
                acc_ref[...] += product()
            else:
                acc_ref[...] = product()

        @pl.when(k > 0)
        def _():
            acc_ref[...] += product()

        @pl.when(k == nk - 1)
        def _():
            o_ref[...] = acc_ref[...].astype(BF16)

    oblk = pl.BlockSpec((None, tm, tn), o_map)
    grid_spec = pltpu.PrefetchScalarGridSpec(
        num_scalar_prefetch=1, grid=grid,
        in_specs=[pl.BlockSpec((ts, tm), a_map), pl.BlockSpec((ts, tn), lambda i, n, k, h: (k, n))]
        + ([oblk] if with_init else []),
        out_specs=oblk, scratch_shapes=[] if nk == 1 else [pltpu.VMEM((tm, tn), F32)])
    operands = [jnp.reshape(half, (1,)).astype(jnp.int32), a, dy] + ([init] if with_init else [])
    return pl.pallas_call(
        body_single if nk == 1 else body, name=name, grid_spec=grid_spec,
        out_shape=jax.ShapeDtypeStruct((N_CHIPS, hr, cols), BF16),
        compiler_params=_cp(("parallel", "parallel", "arbitrary")))(*operands)


def _mm_wgrad_diag(name, a, dy):
    s, width = a.shape
    nb = width // LANES
    ts = _pick(s, MM_TILE)
    grid = (nb, 1, s // ts)
    return _matmul(
        name, "tn", grid, [a, dy],
        [pl.BlockSpec((ts, LANES), lambda i, n, k: (k, i)),
         pl.BlockSpec((ts, LANES), lambda i, n, k: (k, i))],
        [jax.ShapeDtypeStruct((nb, LANES, LANES), F32)],
        [pl.BlockSpec((None, LANES, LANES), lambda i, n, k: (i, 0, 0))], (LANES, LANES))[0]


def _rowspec(tr, d):
    return pl.BlockSpec((tr, d), lambda i: (i, 0))


def _vecspec(d):
    return pl.BlockSpec((1, d), lambda i: (0, 0))


def _rms(x, g):
    return x * lax.rsqrt(jnp.mean(x * x, axis=-1, keepdims=True) + NORM_EPS) * g


def _cast_into_slot(name, w, layer, chip):
    _, r, c = w.shape
    tr = _pick(r, STREAM_TILE)

    def body(chip_ref, w_ref, o_ref):
        o_ref[...] = w_ref[...].astype(BF16)

    grid_spec = pltpu.PrefetchScalarGridSpec(
        num_scalar_prefetch=1, grid=(r // tr,),
        in_specs=[pl.BlockSpec((None, tr, c), lambda i, chip_ref: (layer, i, 0))],
        out_specs=pl.BlockSpec((None, tr, c), lambda i, chip_ref: (chip_ref[0], i, 0)))
    return pl.pallas_call(
        body, name=name, grid_spec=grid_spec, out_shape=jax.ShapeDtypeStruct((N_CHIPS, r, c), BF16),
        compiler_params=_cp(("parallel",)))(jnp.reshape(chip, (1,)).astype(jnp.int32), w)


def _rms_fwd(name, x, g):
    s, d = x.shape
    tr = _pick(s, ROW_TILE)

    def body(x_ref, g_ref, h_ref):
        h_ref[...] = _rms(x_ref[...], g_ref[...]).astype(BF16)

    return pl.pallas_call(
        body, name=name, grid=(s // tr,), in_specs=[_rowspec(tr, d), _vecspec(d)],
        out_specs=_rowspec(tr, d), out_shape=jax.ShapeDtypeStruct((s, d), BF16),
        compiler_params=_cp(("parallel",)))(x, g)


def _rms_post(name, y, g_post, res, g_next=None):
    s, d = y.shape
    tr = _pick(s, ROW_TILE)
    with_next = g_next is not None

    def body(*refs):
        if with_next:
            y_ref, gp_ref, r_ref, gn_ref, x_ref, h_ref = refs
        else:
            y_ref, gp_ref, r_ref, x_ref = refs
        xn = r_ref[...] + _rms(y_ref[...], gp_ref[...])
        x_ref[...] = xn
        if with_next:
            h_ref[...] = _rms(xn, gn_ref[...]).astype(BF16)

    operands = [y, g_post, res] + ([g_next] if with_next else [])
    in_specs = [_rowspec(tr, d), _vecspec(d), _rowspec(tr, d)] + ([_vecspec(d)] if with_next else [])
    out_shape = [jax.ShapeDtypeStruct((s, d), F32)] + ([jax.ShapeDtypeStruct((s, d), BF16)] if with_next else [])
    out_specs = [_rowspec(tr, d)] + ([_rowspec(tr, d)] if with_next else [])
    return pl.pallas_call(
        body, name=name, grid=(s // tr,), in_specs=in_specs, out_specs=out_specs, out_shape=out_shape,
        compiler_params=_cp(("parallel",)))(*operands)


def _rms_bwd(name, x, g, dy, res=None, out_dtype=F32):
    s, d = x.shape
    tr = _pick(s, ROW_TILE)
    nsteps = s // tr
    with_res = res is not None

    def body(*refs):
        if with_res:
            x_ref, g_ref, dy_ref, r_ref, dx_ref, dg_ref, acc_ref = refs
        else:
            x_ref, g_ref, dy_ref, dx_ref, dg_ref, acc_ref = refs
        i = pl.program_id(0)

        @pl.when(i == 0)
        def _():
            acc_ref[...] = jnp.zeros_like(acc_ref)

        xv = x_ref[...]
        dyv = dy_ref[...].astype(F32)
        r = lax.rsqrt(jnp.mean(xv * xv, axis=-1, keepdims=True) + NORM_EPS)
        xhat = xv * r
        gy = dyv * g_ref[...]
        dx = r * (gy - xhat * jnp.mean(gy * xhat, axis=-1, keepdims=True))
        if with_res:
            dx = dx + r_ref[...]
        dx_ref[...] = dx.astype(dx_ref.dtype)
        acc_ref[...] += jnp.sum((dyv * xhat).reshape(tr // SUBLANES, SUBLANES, d), axis=0)

        @pl.when(i == nsteps - 1)
        def _():
            dg_ref[...] = jnp.broadcast_to(_colsum(acc_ref[...]), (SUBLANES, d))

    operands = [x, g, dy] + ([res] if with_res else [])
    in_specs = [_rowspec(tr, d), _vecspec(d), _rowspec(tr, d)] + ([_rowspec(tr, d)] if with_res else [])
    dx, dg = pl.pallas_call(
        body, name=name, grid=(nsteps,), in_specs=in_specs,
        out_specs=[_rowspec(tr, d), pl.BlockSpec((SUBLANES, d), lambda i: (0, 0))],
        out_shape=[jax.ShapeDtypeStruct((s, d), out_dtype), jax.ShapeDtypeStruct((SUBLANES, d), F32)],
        scratch_shapes=[pltpu.VMEM((SUBLANES, d), F32)],
        compiler_params=_cp(("arbitrary",)))(*operands)
    return dx, dg[0:1]


def _rms_bwd_pair(name, x, g, dy, res, y2, g2):
    s, d = x.shape
    tr = _pick(s, ROW_TILE)
    nsteps = s // tr

    def through(xv, gv, dyv):
        r = lax.rsqrt(jnp.mean(xv * xv, axis=-1, keepdims=True) + NORM_EPS)
        xhat = xv * r
        gy = dyv * gv
        dx = r * (gy - xhat * jnp.mean(gy * xhat, axis=-1, keepdims=True))
        return dx, jnp.sum((dyv * xhat).reshape(tr // SUBLANES, SUBLANES, d), axis=0)

    def body(x_ref, g_ref, dy_ref, r_ref, y2_ref, g2_ref, dx_ref, d2_ref, dg_ref, dg2_ref, acc_ref, acc2_ref):
        i = pl.program_id(0)

        @pl.when(i == 0)
        def _():
            acc_ref[...] = jnp.zeros_like(acc_ref)
            acc2_ref[...] = jnp.zeros_like(acc2_ref)

        dx, part = through(x_ref[...], g_ref[...], dy_ref[...].astype(F32))
        dx = dx + r_ref[...]
        dx_ref[...] = dx
        acc_ref[...] += part
        d2, part2 = through(y2_ref[...], g2_ref[...], dx)
        d2_ref[...] = d2.astype(d2_ref.dtype)
        acc2_ref[...] += part2

        @pl.when(i == nsteps - 1)
        def _():
            dg_ref[...] = jnp.broadcast_to(_colsum(acc_ref[...]), (SUBLANES, d))
            dg2_ref[...] = jnp.broadcast_to(_colsum(acc2_ref[...]), (SUBLANES, d))

    row, vec = _rowspec(tr, d), _vecspec(d)
    gspec = pl.BlockSpec((SUBLANES, d), lambda i: (0, 0))
    dx, d2, dg, dg2 = pl.pallas_call(
        body, name=name, grid=(nsteps,), in_specs=[row, vec, row, row, row, vec],
        out_specs=[row, row, gspec, gspec],
        out_shape=[jax.ShapeDtypeStruct((s, d), F32), jax.ShapeDtypeStruct((s, d), BF16),
                   jax.ShapeDtypeStruct((SUBLANES, d), F32), jax.ShapeDtypeStruct((SUBLANES, d), F32)],
        scratch_shapes=[pltpu.VMEM((SUBLANES, d), F32), pltpu.VMEM((SUBLANES, d), F32)],
        compiler_params=_cp(("arbitrary",)))(x, g, dy, res, y2, g2)
    return dx, dg[0:1], d2, dg2[0:1]


def _last_norm_and_loss(name, y, g, res, target):
    s, d = y.shape
    tr = _pick(s, ROW_TILE)
    nsteps = s // tr

    def body(y_ref, g_ref, r_ref, t_ref, dx_ref, dy_ref, dg_ref, l_ref, acc_ref, lacc_ref):
        i = pl.program_id(0)

        @pl.when(i == 0)
        def _():
            acc_ref[...] = jnp.zeros_like(acc_ref)
            lacc_ref[...] = jnp.zeros_like(lacc_ref)

        yv = y_ref[...]
        gv = g_ref[...]
        r = lax.rsqrt(jnp.mean(yv * yv, axis=-1, keepdims=True) + NORM_EPS)
        yhat = yv * r
        err = r_ref[...] + yhat * gv - t_ref[...]
        dx = err * (1.0 / d)
        dx_ref[...] = dx
        lacc_ref[...] += jnp.sum((err * err).reshape(tr // SUBLANES, SUBLANES, d), axis=0)
        gy = dx * gv
        dy_ref[...] = (r * (gy - yhat * jnp.mean(gy * yhat, axis=-1, keepdims=True))).astype(dy_ref.dtype)
        acc_ref[...] += jnp.sum((dx * yhat).reshape(tr // SUBLANES, SUBLANES, d), axis=0)

        @pl.when(i == nsteps - 1)
        def _():
            dg_ref[...] = jnp.broadcast_to(_colsum(acc_ref[...]), (SUBLANES, d))
            l_ref[...] = jnp.full((SUBLANES, LANES), (0.5 / d) * jnp.sum(lacc_ref[...]), F32)

    dx, dy, dg, l = pl.pallas_call(
        body, name=name, grid=(nsteps,),
        in_specs=[_rowspec(tr, d), _vecspec(d), _rowspec(tr, d), _rowspec(tr, d)],
        out_specs=[_rowspec(tr, d), _rowspec(tr, d), pl.BlockSpec((SUBLANES, d), lambda i: (0, 0)),
                   pl.BlockSpec((SUBLANES, LANES), lambda i: (0, 0))],
        out_shape=[jax.ShapeDtypeStruct((s, d), F32), jax.ShapeDtypeStruct((s, d), BF16),
                   jax.ShapeDtypeStruct((SUBLANES, d), F32), jax.ShapeDtypeStruct((SUBLANES, LANES), F32)],
        scratch_shapes=[pltpu.VMEM((SUBLANES, d), F32), pltpu.VMEM((SUBLANES, d), F32)],
        compiler_params=_cp(("arbitrary",)))(y, g, res, target)
    return dx, dy, dg[0:1], l[0, 0]


def _gates(xr, wa, ba, wx, bx, lam):
    xb = xr.astype(BF16)
    r = _sigmoid(jnp.dot(xb, wa, preferred_element_type=F32) + ba)
    i = _sigmoid(jnp.dot(xb, wx, preferred_element_type=F32) + bx)
    log_a = LRU_C * r * _log_sigmoid(lam)
    a = jnp.exp(log_a)
    m = jnp.sqrt(-_expm1(2.0 * log_a))
    return r, i, a, m


def _mixer_fwd(proj, conv_a, conv_b, bias, wa_blk, ba, wx_blk, bx, lam):
    s, w5 = proj.shape
    w = w5 // 5
    nch = w // LANES
    ts = _pick(s, ROW_TILE)
    nt = s // ts

    def body(p_ref, pp_ref, ca_ref, cb_ref, bias_ref, wa_ref, ba_ref, wx_ref, bx_ref, lam_ref,
             y_ref, h_ref, a_scr, b_scr, hc_scr):
        t = pl.program_id(0)
        first = t == 0
        rows = lax.broadcasted_iota(jnp.int32, (ts, LANES), 0)

        @pl.when(first)
        def _():
            hc_scr[...] = jnp.zeros_like(hc_scr)

        def cur(comp, c):
            return p_ref[:, comp * w + c * LANES:comp * w + (c + 1) * LANES]

        def prev(comp, c):
            v = pp_ref[:, comp * w + c * LANES:comp * w + (c + 1) * LANES]
            return jnp.where(first, 0.0, v)

        for c in range(nch):
            sl = slice(c * LANES, (c + 1) * LANES)
            cx = cur(1, c) * cur(2, c)
            cxp = prev(1, c) * prev(2, c)
            wa3 = ca_ref[:, sl]
            conv = (wa3[2:3] * cx + wa3[1:2] * _shift_down(cx, cxp, 1, rows)
                    + wa3[0:1] * _shift_down(cx, cxp, 2, rows))
            y_ref[:, sl] = (cur(0, c) * conv).astype(BF16)

        for c in range(nch):
            sl = slice(c * LANES, (c + 1) * LANES)
            xb, xbp = cur(4, c), prev(4, c)
            wb4 = cb_ref[:, sl]
            xr = (wb4[3:4] * xb + wb4[2:3] * _shift_down(xb, xbp, 1, rows)
                  + wb4[1:2] * _shift_down(xb, xbp, 2, rows)
                  + wb4[0:1] * _shift_down(xb, xbp, 3, rows) + bias_ref[:, sl])
            _, i, a, m = _gates(xr, wa_ref[c], ba_ref[:, sl], wx_ref[c], bx_ref[:, sl], lam_ref[:, sl])
            a_scr[:, sl] = a
            b_scr[:, sl] = m * i * xr

        def step(r, h):
            h = a_scr[pl.ds(r, 1), :] * h + b_scr[pl.ds(r, 1), :]
            h_ref[pl.ds(r, 1), :] = h
            return h

        hc_scr[0:1, :] = lax.fori_loop(0, ts, step, hc_scr[0:1, :], unroll=8)

        for c in range(nch):
            sl = slice(c * LANES, (c + 1) * LANES)
            gel, _ = _gelu_and_grad(cur(3, c))
            y_ref[:, w + c * LANES:w + (c + 1) * LANES] = (h_ref[:, sl] * gel).astype(BF16)

    vec = lambda n: _whole((n, w))
    return pl.pallas_call(
        body, name="mixer_fwd", grid=(nt,),
        in_specs=[pl.BlockSpec((ts, w5), lambda t: (t, 0)),
                  pl.BlockSpec((SUBLANES, w5), lambda t: (jnp.maximum(t * (ts // SUBLANES) - 1, 0), 0)),
                  vec(3), vec(4), vec(1), _whole(wa_blk.shape), vec(1), _whole(wx_blk.shape), vec(1), vec(1)],
        out_specs=[pl.BlockSpec((ts, 2 * w), lambda t: (t, 0)), pl.BlockSpec((ts, w), lambda t: (t, 0))],
        out_shape=[jax.ShapeDtypeStruct((s, 2 * w), BF16), jax.ShapeDtypeStruct((s, w), F32)],
        scratch_shapes=[pltpu.VMEM((ts, w), F32), pltpu.VMEM((ts, w), F32), pltpu.VMEM((SUBLANES, w), F32)],
        compiler_params=_cp(("arbitrary",)),
    )(proj, proj, conv_a, conv_b, bias, wa_blk, ba, wx_blk, bx, lam)


_SG_CONV_A, _SG_CONV_B, _SG_BIAS, _SG_BA, _SG_BX, _SG_LAM, _SG_ROWS = 0, 3, 7, 8, 9, 10, 16


def _mixer_bwd(proj, hseq, dy, conv_a, conv_b, bias, wa_blk, ba, wx_blk, bx, lam):
    s, w5 = proj.shape
    w = w5 // 5
    nch = w // LANES
    ts = _pick(s, ROW_TILE)
    nt = s // ts
    tpb = ts // SUBLANES

    def body(p_ref, pp_ref, h_ref, hp_ref, dy_ref, ca_ref, cb_ref, bias_ref, wa_ref, ba_ref, wx_ref, bx_ref,
             lam_ref, dp_ref, xr_ref, dpa_ref, dpx_ref, sg_ref,
             a_scr, g_scr, l_scr, x_scr, r_scr, i_scr, m_scr, cl_scr, cdc_scr, cdx_scr):
        pid = pl.program_id(0)
        last = pid == 0
        first = pid == nt - 1
        rows = lax.broadcasted_iota(jnp.int32, (ts, LANES), 0)

        @pl.when(last)
        def _():
            sg_ref[...] = jnp.zeros_like(sg_ref)
            cl_scr[...] = jnp.zeros_like(cl_scr)
            cdc_scr[...] = jnp.zeros_like(cdc_scr)
            cdx_scr[...] = jnp.zeros_like(cdx_scr)

        def cur(comp, c):
            return p_ref[:, comp * w + c * LANES:comp * w + (c + 1) * LANES]

        def prev(comp, c):
            v = pp_ref[:, comp * w + c * LANES:comp * w + (c + 1) * LANES]
            return jnp.where(first, 0.0, v)

        def put(comp, c, v):
            dp_ref[:, comp * w + c * LANES:comp * w + (c + 1) * LANES] = v.astype(dp_ref.dtype)

        def acc(row, sl, v):
            sg_ref[row:row + 1, sl] += _colsum(v)

        for c in range(nch):
            sl = slice(c * LANES, (c + 1) * LANES)
            bg, cg, ax = cur(0, c), cur(1, c), cur(2, c)
            cx = cg * ax
            cxp = prev(1, c) * prev(2, c)
            cx1 = _shift_down(cx, cxp, 1, rows)
            cx2 = _shift_down(cx, cxp, 2, rows)
            wa3 = ca_ref[:, sl]
            conv = wa3[2:3] * cx + wa3[1:2] * cx1 + wa3[0:1] * cx2
            dya = dy_ref[:, sl]
            put(0, c, dya * conv)
            dconv = dya * bg
            nxt = cdc_scr[:, sl]
            dcx = (wa3[2:3] * dconv + wa3[1:2] * _shift_up(dconv, nxt, 1, rows)
                   + wa3[0:1] * _shift_up(dconv, nxt, 2, rows))
            cdc_scr[:, sl] = dconv[0:SUBLANES]
            put(1, c, dcx * ax)
            put(2, c, dcx * cg)
            acc(_SG_CONV_A + 2, sl, dconv * cx)
            acc(_SG_CONV_A + 1, sl, dconv * cx1)
            acc(_SG_CONV_A + 0, sl, dconv * cx2)

        for c in range(nch):
            sl = slice(c * LANES, (c + 1) * LANES)
            xb, xbp = cur(4, c), prev(4, c)
            wb4 = cb_ref[:, sl]
            xr = (wb4[3:4] * xb + wb4[2:3] * _shift_down(xb, xbp, 1, rows)
                  + wb4[1:2] * _shift_down(xb, xbp, 2, rows)
                  + wb4[0:1] * _shift_down(xb, xbp, 3, rows) + bias_ref[:, sl])
            r, i, a, m = _gates(xr, wa_ref[c], ba_ref[:, sl], wx_ref[c], bx_ref[:, sl], lam_ref[:, sl])
            gel, dgel = _gelu_and_grad(cur(3, c))
            dyb = dy_ref[:, w + c * LANES:w + (c + 1) * LANES]
            put(3, c, dyb * h_ref[:, sl] * dgel)
            g_scr[:, sl] = dyb * gel
            a_scr[:, sl] = a
            x_scr[:, sl] = xr
            r_scr[:, sl] = r
            i_scr[:, sl] = i
            m_scr[:, sl] = m

        def step(j, carry):
            r = ts - 1 - j
            lam_t = g_scr[pl.ds(r, 1), :] + carry
            l_scr[pl.ds(r, 1), :] = lam_t
            return a_scr[pl.ds(r, 1), :] * lam_t

        cl_scr[0:1, :] = lax.fori_loop(0, ts, step, cl_scr[0:1, :], unroll=8)

        for c in range(nch):
            sl = slice(c * LANES, (c + 1) * LANES)
            lam_t = l_scr[:, sl]
            hprev = _shift_down(h_ref[:, sl], jnp.where(first, 0.0, hp_ref[:, sl]), 1, rows)
            xr, r, i, m, a = x_scr[:, sl], r_scr[:, sl], i_scr[:, sl], m_scr[:, sl], a_scr[:, sl]
            da = lam_t * hprev
            dm = lam_t * i * xr
            di = lam_t * m * xr
            dxr = lam_t * m * i
            dlog_a = da * a - dm * a * a / m
            lam_p = lam_ref[:, sl]
            dr = dlog_a * (LRU_C * _log_sigmoid(lam_p))
            acc(_SG_LAM, sl, dlog_a * r * (LRU_C * _sigmoid(-lam_p)))
            dpa = dr * r * (1.0 - r)
            dpx = di * i * (1.0 - i)
            dpa_b, dpx_b = dpa.astype(BF16), dpx.astype(BF16)
            dxr = (dxr + lax.dot_general(dpa_b, wa_ref[c], _DIMS["nt"], preferred_element_type=F32)
                   + lax.dot_general(dpx_b, wx_ref[c], _DIMS["nt"], preferred_element_type=F32))
            xr_ref[:, sl] = xr.astype(BF16)
            dpa_ref[:, sl] = dpa_b
            dpx_ref[:, sl] = dpx_b
            acc(_SG_BA, sl, dpa)
            acc(_SG_BX, sl, dpx)
            acc(_SG_BIAS, sl, dxr)
            nxt = cdx_scr[:, sl]
            wb4 = cb_ref[:, sl]
            put(4, c, wb4[3:4] * dxr + wb4[2:3] * _shift_up(dxr, nxt, 1, rows)
                + wb4[1:2] * _shift_up(dxr, nxt, 2, rows) + wb4[0:1] * _shift_up(dxr, nxt, 3, rows))
            cdx_scr[:, sl] = dxr[0:SUBLANES]
            xb, xbp = cur(4, c), prev(4, c)
            acc(_SG_CONV_B + 3, sl, dxr * xb)
            acc(_SG_CONV_B + 2, sl, dxr * _shift_down(xb, xbp, 1, rows))
            acc(_SG_CONV_B + 1, sl, dxr * _shift_down(xb, xbp, 2, rows))
            acc(_SG_CONV_B + 0, sl, dxr * _shift_down(xb, xbp, 3, rows))

    blk = lambda width: pl.BlockSpec((ts, width), lambda p: (nt - 1 - p, 0))
    pre = lambda width: pl.BlockSpec(
        (SUBLANES, width), lambda p: (jnp.maximum((nt - 1 - p) * tpb - 1, 0), 0))
    vec = lambda n: _whole((n, w))
    big = lambda: pltpu.VMEM((ts, w), F32)
    small = lambda: pltpu.VMEM((SUBLANES, w), F32)
    return pl.pallas_call(
        body, name="mixer_bwd", grid=(nt,),
        in_specs=[blk(w5), pre(w5), blk(w), pre(w), blk(2 * w),
                  vec(3), vec(4), vec(1), _whole(wa_blk.shape), vec(1), _whole(wx_blk.shape), vec(1), vec(1)],
        out_specs=[blk(w5), blk(w), blk(w), blk(w), _whole((_SG_ROWS, w))],
        out_shape=[jax.ShapeDtypeStruct((s, w5), BF16), jax.ShapeDtypeStruct((s, w), BF16),
                   jax.ShapeDtypeStruct((s, w), BF16), jax.ShapeDtypeStruct((s, w), BF16),
                   jax.ShapeDtypeStruct((_SG_ROWS, w), F32)],
        scratch_shapes=[big(), big(), big(), big(), big(), big(), big(), small(), small(), small()],
        compiler_params=_cp(("arbitrary",)),
    )(proj, proj, hseq, hseq, dy, conv_a, conv_b, bias, wa_blk, ba, wx_blk, bx, lam)


def _split_dot(v, tri2):
    hi = v.astype(BF16)
    lo = (v - hi.astype(F32)).astype(BF16)
    return jnp.dot(jnp.concatenate([hi, lo], axis=1), tri2, preferred_element_type=F32)


def _tri(cmp):
    r = lax.broadcasted_iota(jnp.int32, (LANES, LANES), 0)
    c = lax.broadcasted_iota(jnp.int32, (LANES, LANES), 1)
    return cmp(r, c).astype(BF16)


def _lane_blocks(v):
    return [v[:, b * LANES:(b + 1) * LANES] for b in range(v.shape[1] // LANES)]


def _last_lane(v):
    return jnp.broadcast_to(v[:, LANES - 1:LANES], v.shape)


def _scores(q, kb, scale):
    return lax.dot_general(q, kb, _DIMS["nt"], preferred_element_type=F32) * scale


def _log_gates(z, diagonal):
    ls = jnp.minimum(z, 0.0) - jnp.log(1.0 + jnp.exp(-jnp.abs(z)))
    ln = ls - z
    valid = None
    if diagonal:
        valid = (lax.broadcasted_iota(jnp.int32, z.shape, 1) < lax.broadcasted_iota(jnp.int32, z.shape, 0))
        ln = jnp.where(valid, ln, 0.0)
    return ls, ln, valid


def _attn_fwd(qkv, heads):
    s = qkv.shape[0]
    dh = LANES
    tq = _pick(s, ATT_TILE)
    nq = s // tq
    nb = tq // LANES
    scale = 1.0 / math.sqrt(dh)

    hp = ATT_FWD_HEADS_PER_STEP
    groups = heads // hp
    wid = hp * dh

    def body(q_ref, k_ref, v_ref, o_ref, tot_ref, acc_scr, car_scr):
        qi = pl.program_id(1)
        acc_scr[...] = jnp.zeros_like(acc_scr)
        car_scr[...] = jnp.zeros_like(car_scr)
        tri = _tri(lambda r, c: r > c)
        tri = jnp.concatenate([tri, tri], axis=0)

        def tile(kt, diagonal):
            k0 = pl.multiple_of(kt * tq, tq)
            heads_cols = [slice(hh * dh, (hh + 1) * dh) for hh in range(hp)]
            zs = [_scores(q_ref[:, cols], k_ref[pl.ds(k0, tq), cols], scale) for cols in heads_cols]
            gates = [_log_gates(z, diagonal) for z in zs]
            sfxs = [_split_dot(jnp.concatenate(_lane_blocks(ln), axis=0), tri) for _, ln, _ in gates]
            for cols, (ls, ln, valid), sfx in zip(heads_cols, gates, sfxs):
                blocks = _lane_blocks(ln)
                car = car_scr[:, cols]
                parts = [None] * nb
                for b in reversed(range(nb)):
                    sb = sfx[b * tq:(b + 1) * tq]
                    parts[b] = sb + car
                    car = car + (sb[:, 0:1] + blocks[b][:, 0:1])
                car_scr[:, cols] = car
                wgt = jnp.exp(ls + jnp.concatenate(parts, axis=1))
                if diagonal:
                    wgt = jnp.where(valid, wgt, 0.0)
                acc_scr[:, cols] += jnp.dot(
                    wgt.astype(BF16), v_ref[pl.ds(k0, tq), cols], preferred_element_type=F32)

        tile(qi, True)

        def step(j, carry):
            tile(qi - 1 - j, False)
            return carry

        lax.fori_loop(0, qi, step, 0)
        o_ref[...] = acc_scr[...].astype(BF16)
        tot_ref[...] = car_scr[...]

    return pl.pallas_call(
        body, name="attn_fwd", grid=(groups, nq),
        in_specs=[pl.BlockSpec((tq, wid), lambda h, i: (i, h)),
                  pl.BlockSpec((s, wid), lambda h, i: (0, groups + h)),
                  pl.BlockSpec((s, wid), lambda h, i: (0, 2 * groups + h))],
        out_specs=[pl.BlockSpec((tq, wid), lambda h, i: (i, h)), pl.BlockSpec((tq, wid), lambda h, i: (i, h))],
        out_shape=[jax.ShapeDtypeStruct((s, heads * dh), BF16), jax.ShapeDtypeStruct((s, heads * dh), F32)],
        scratch_shapes=[pltpu.VMEM((tq, wid), F32), pltpu.VMEM((tq, wid), F32)],
        compiler_params=_cp(("parallel", "arbitrary")),
    )(qkv, qkv, qkv)


def _attn_bwd(qkv, tot, do, heads):
    s = qkv.shape[0]
    dh = LANES
    tq = _pick(s, ATT_TILE)
    nq = s // tq
    nb = tq // LANES
    scale = 1.0 / math.sqrt(dh)

    hp = ATT_HEADS_PER_STEP
    groups = heads // hp
    wid = hp * dh

    def body(q_ref, k_ref, v_ref, tot_ref, do_ref, dq_ref, dk_ref, dv_ref,
             dq_scr, dk_scr, dv_scr, cl_scr, cg_scr):
        qi = pl.program_id(1)

        @pl.when(qi == 0)
        def _():
            dk_scr[...] = jnp.zeros_like(dk_scr)
            dv_scr[...] = jnp.zeros_like(dv_scr)

        dq_scr[...] = jnp.zeros_like(dq_scr)
        cl_scr[...] = jnp.zeros_like(cl_scr)
        cg_scr[...] = jnp.zeros_like(cg_scr)
        tri_le = _tri(lambda r, c: r <= c)
        tri_le = jnp.concatenate([tri_le, tri_le], axis=0)
        tri_lt = _tri(lambda r, c: r < c)

        def tile(kt, diagonal):
            k0 = pl.multiple_of(kt * tq, tq)
            heads_cols = [slice(hh * dh, (hh + 1) * dh) for hh in range(hp)]
            keys = pl.ds(k0, tq)
            zs = [_scores(q_ref[:, cols], k_ref[keys, cols], scale) for cols in heads_cols]
            dws = [lax.dot_general(do_ref[:, cols], v_ref[keys, cols], _DIMS["nt"], preferred_element_type=F32)
                   for cols in heads_cols]
            gates = [_log_gates(z, diagonal) for z in zs]
            pins = [_split_dot(jnp.concatenate(_lane_blocks(ln), axis=0), tri_le) for _, ln, _ in gates]
            wgts, gs = [], []
            for cols, (ls, _, valid), pin, dw in zip(heads_cols, gates, pins, dws):
                total = tot_ref[:, cols]
                cl = cl_scr[:, cols]
                parts = []
                for b in range(nb):
                    pb = pin[b * tq:(b + 1) * tq] + cl
                    parts.append(total - pb)
                    cl = _last_lane(pb)
                cl_scr[:, cols] = cl
                wgt = jnp.exp(ls + jnp.concatenate(parts, axis=1))
                if diagonal:
                    wgt = jnp.where(valid, wgt, 0.0)
                wgts.append(wgt)
                gs.append(wgt * dw)
            pexs = [jnp.dot(jnp.concatenate(_lane_blocks(g), axis=0).astype(BF16), tri_lt,
                            preferred_element_type=F32) for g in gs]
            for cols, wgt in zip(heads_cols, wgts):
                dv_scr[keys, cols] += lax.dot_general(
                    wgt.astype(BF16), do_ref[:, cols], _DIMS["tn"], preferred_element_type=F32)
            for cols, (ls, _, valid), g, pex in zip(heads_cols, gates, gs, pexs):
                gblocks = _lane_blocks(g)
                cg = cg_scr[:, cols]
                parts = []
                for b in range(nb):
                    pb = pex[b * tq:(b + 1) * tq] + cg
                    parts.append(pb)
                    cg = _last_lane(pb + gblocks[b])
                cg_scr[:, cols] = cg
                dz = g - jnp.exp(ls) * (g + jnp.concatenate(parts, axis=1))
                if diagonal:
                    dz = jnp.where(valid, dz, 0.0)
                dz = dz.astype(BF16)
                dq_scr[:, cols] += jnp.dot(dz, k_ref[keys, cols], preferred_element_type=F32)
                dk_scr[keys, cols] += lax.dot_general(
                    dz, q_ref[:, cols], _DIMS["tn"], preferred_element_type=F32)

        def step(j, carry):
            tile(j, False)
            return carry

        lax.fori_loop(0, qi, step, 0)
        tile(qi, True)
        dq_ref[...] = (dq_scr[...] * scale).astype(BF16)

        @pl.when(qi == nq - 1)
        def _():
            dk_ref[...] = (dk_scr[...] * scale).astype(BF16)
            dv_ref[...] = dv_scr[...].astype(BF16)

    qblk = pl.BlockSpec((tq, wid), lambda h, i: (i, h))
    hblk = pl.BlockSpec((s, wid), lambda h, i: (0, h))
    out = jax.ShapeDtypeStruct((s, heads * dh), BF16)
    return pl.pallas_call(
        body, name="attn_bwd", grid=(groups, nq),
        in_specs=[qblk, pl.BlockSpec((s, wid), lambda h, i: (0, groups + h)),
                  pl.BlockSpec((s, wid), lambda h, i: (0, 2 * groups + h)), qblk, qblk],
        out_specs=[qblk, hblk, hblk], out_shape=[out, out, out],
        scratch_shapes=[pltpu.VMEM((tq, wid), F32), pltpu.VMEM((s, wid), F32), pltpu.VMEM((s, wid), F32),
                        pltpu.VMEM((tq, wid), F32), pltpu.VMEM((tq, wid), F32)],
        compiler_params=_cp(("parallel", "arbitrary")),
    )(qkv, qkv, qkv, tot, do)


def _place():
    x, y, c = lax.axis_index("x"), lax.axis_index("y"), lax.axis_index("c")
    chips = [(1 - x, y), (x, 1 - y), (1 - x, 1 - y)]
    return x, y, c, chips


def _remote(src, dst, send_sem, recv_sem, dev):
    return pltpu.make_async_remote_copy(
        src_ref=src, dst_ref=dst, send_sem=send_sem, recv_sem=recv_sem, device_id=dev, device_id_type=MESH)


def _handshake(peers):
    barrier = pltpu.get_barrier_semaphore()
    for dev in peers:
        pl.semaphore_signal(barrier, inc=1, device_id=dev, device_id_type=MESH)
    pl.semaphore_wait(barrier, len(peers))


def _sequencer_kernel(name, n_sems, collective_id):
    return functools.partial(
        pl.kernel, mesh=plsc.ScalarSubcoreMesh(axis_name="seq", num_cores=1), name=name,
        scratch_types=(pltpu.SemaphoreType.DMA,) * n_sems,
        compiler_params=pltpu.CompilerParams(collective_id=collective_id))


def _allgather_async(name, slot_buf, collective_id):
    buf = jax.new_ref(slot_buf, memory_space=pltpu.MemorySpace.HBM)
    hr = slot_buf.shape[1] // 2

    @_sequencer_kernel(name, 12, collective_id)
    def launch(*sems):
        send_sems, recv_sems, fsend_sems, frecv_sems = sems[0:3], sems[3:6], sems[6:9], sems[9:12]
        x, y, c, chips = _place()
        me = 2 * x + y
        sibling = (x, y, 1 - c)
        _handshake([(px, py, c) for px, py in chips] + [sibling])
        mine = buf.at[me, pl.ds(c * hr, hr)]
        firsts = []
        for k, (px, py) in enumerate(chips):
            cp = _remote(mine, mine, send_sems[k], recv_sems[k], (px, py, c))
            cp.start()
            firsts.append(cp)
        passed = []
        for k, (px, py) in enumerate(chips):
            slot = buf.at[2 * px + py, pl.ds(c * hr, hr)]
            _remote(slot, slot, send_sems[k], recv_sems[k], (px, py, c)).wait_recv()
            cp = _remote(slot, slot, fsend_sems[k], frecv_sems[k], sibling)
            cp.start()
            passed.append(cp)
        for k, (px, py) in enumerate(chips):
            slot = buf.at[2 * px + py, pl.ds((1 - c) * hr, hr)]
            _remote(slot, slot, fsend_sems[k], frecv_sems[k], sibling).wait_recv()
        for cp in firsts + passed:
            cp.wait_send()

    launch()
    return buf[...]


def _to_sibling_async(name, slab):
    src = jax.new_ref(slab, memory_space=pltpu.MemorySpace.HBM)
    hr = slab.shape[1] // 2
    got = jax.empty_ref(jax.ShapeDtypeStruct((N_CHIPS, hr, slab.shape[2]), slab.dtype),
                        memory_space=pltpu.MemorySpace.HBM)

    @_sequencer_kernel(name, 2, COLLECTIVE_SIBLING)
    def launch(send_sem, recv_sem):
        x, y, c, _ = _place()
        _handshake([(x, y, 1 - c)])
        _remote(src.at[:, pl.ds((1 - c) * hr, hr), :], got, send_sem, recv_sem, (x, y, 1 - c)).start()
        _remote(got, got, send_sem, recv_sem, (x, y, 1 - c)).wait()

    launch()
    return src[...], got[...]


def _swap_with_sibling_async(name, part):
    src = jax.new_ref(part, memory_space=pltpu.MemorySpace.HBM)
    got = jax.empty_ref(jax.ShapeDtypeStruct(part.shape, part.dtype), memory_space=pltpu.MemorySpace.HBM)

    @_sequencer_kernel(name, 2, COLLECTIVE_SIBLING)
    def launch(send_sem, recv_sem):
        x, y, c, _ = _place()
        _handshake([(x, y, 1 - c)])
        cp = _remote(src, got, send_sem, recv_sem, (x, y, 1 - c))
        cp.start()
        cp.wait()

    launch()
    return got[...]


def _to_chips_async(name, part):
    src = jax.new_ref(part, memory_space=pltpu.MemorySpace.HBM)
    got = jax.empty_ref(jax.ShapeDtypeStruct((3,) + part.shape[1:], part.dtype), memory_space=pltpu.MemorySpace.HBM)

    @_sequencer_kernel(name, 6, COLLECTIVE_CHIPS)
    def launch(*sems):
        send_sems, recv_sems = sems[0:3], sems[3:6]
        x, y, c, chips = _place()
        _handshake([(px, py, c) for px, py in chips])
        cps = []
        for k, (px, py) in enumerate(chips):
            cp = _remote(src.at[2 * px + py], got.at[k], send_sems[k], recv_sems[k], (px, py, c))
            cp.start()
            cps.append(cp)
        for cp in cps:
            cp.wait()

    launch()
    return src[...], got[...]


def _join_sibling_async(name, half_filled):
    buf = jax.new_ref(half_filled, memory_space=pltpu.MemorySpace.HBM)
    hr = half_filled.shape[0] // 2

    @_sequencer_kernel(name, 2, COLLECTIVE_SIBLING)
    def launch(send_sem, recv_sem):
        x, y, c, _ = _place()
        _handshake([(x, y, 1 - c)])
        mine = buf.at[pl.ds(c * hr, hr)]
        other = buf.at[pl.ds((1 - c) * hr, hr)]
        cp = _remote(mine, mine, send_sem, recv_sem, (x, y, 1 - c))
        cp.start()
        _remote(other, other, send_sem, recv_sem, (x, y, 1 - c)).wait_recv()
        cp.wait_send()

    launch()
    return buf[...]


def _allgather_chips_small(name, v):
    r = v.shape[0]

    def body(v_ref, o_ref, send_sems, recv_sems):
        x, y, c, chips = _place()
        me = 2 * x + y
        o_ref[me] = v_ref[...]
        cps = []
        for k, (px, py) in enumerate(chips):
            cp = _remote(v_ref, o_ref.at[me], send_sems.at[k], recv_sems.at[k], (px, py, c))
            cp.start()
            cps.append(cp)
        for k, (px, py) in enumerate(chips):
            slot = o_ref.at[2 * px + py]
            _remote(slot, slot, send_sems.at[k], recv_sems.at[k], (px, py, c)).wait_recv()
        for cp in cps:
            cp.wait_send()

    return pl.pallas_call(
        body, name=name, in_specs=[pl.BlockSpec(memory_space=pltpu.VMEM)],
        out_specs=pl.BlockSpec(memory_space=pltpu.VMEM),
        out_shape=jax.ShapeDtypeStruct((N_CHIPS, r, LANES), F32),
        scratch_shapes=[pltpu.SemaphoreType.DMA((3,)), pltpu.SemaphoreType.DMA((3,))],
    )(v)


def _allreduce_small(name, v):
    r = v.shape[0]
    hr = r // 2
    assert hr % SUBLANES == 0

    def body(v_ref, o_ref, sib_ref, chips_ref, send_sems, recv_sems):
        x, y, c, chips = _place()
        me = 2 * x + y
        sibling = (x, y, 1 - c)
        first = _remote(v_ref, sib_ref, send_sems.at[0], recv_sems.at[0], sibling)
        first.start()
        first.wait()
        mine = pl.ds(pl.multiple_of(c * hr, SUBLANES), hr)
        chips_ref[me] = v_ref[mine, :] + sib_ref[mine, :]
        cps = []
        for k, (px, py) in enumerate(chips):
            cp = _remote(chips_ref.at[me], chips_ref.at[me], send_sems.at[1 + k], recv_sems.at[1 + k], (px, py, c))
            cp.start()
            cps.append(cp)
        for k, (px, py) in enumerate(chips):
            slot = chips_ref.at[2 * px + py]
            _remote(slot, slot, send_sems.at[1 + k], recv_sems.at[1 + k], (px, py, c)).wait_recv()
        total = chips_ref[0]
        for j in range(1, N_CHIPS):
            total = total + chips_ref[j]
        o_ref[mine, :] = total
        last = _remote(o_ref.at[mine], o_ref.at[mine], send_sems.at[4], recv_sems.at[4], sibling)
        last.start()
        other = o_ref.at[pl.ds(pl.multiple_of((1 - c) * hr, SUBLANES), hr)]
        _remote(other, other, send_sems.at[4], recv_sems.at[4], sibling).wait_recv()
        last.wait_send()
        for cp in cps:
            cp.wait_send()

    return pl.pallas_call(
        body, name=name, in_specs=[pl.BlockSpec(memory_space=pltpu.VMEM)],
        out_specs=pl.BlockSpec(memory_space=pltpu.VMEM),
        out_shape=jax.ShapeDtypeStruct((r, LANES), F32),
        scratch_shapes=[pltpu.VMEM((r, LANES), F32), pltpu.VMEM((N_CHIPS, hr, LANES), F32),
                        pltpu.SemaphoreType.DMA((5,)), pltpu.SemaphoreType.DMA((5,))],
    )(v)


def _add_sibling(name, slabs, recv, c):
    _, r, cols = slabs.shape
    hr = r // 2
    tr = _pick(hr, STREAM_TILE)
    nb = hr // tr

    def body(c_ref, a_ref, b_ref, o_ref):
        o_ref[...] = (a_ref[...].astype(F32) + b_ref[...].astype(F32)).astype(BF16)

    grid_spec = pltpu.PrefetchScalarGridSpec(
        num_scalar_prefetch=1, grid=(N_CHIPS, nb),
        in_specs=[pl.BlockSpec((None, tr, cols), lambda j, i, c_ref: (j, c_ref[0] * nb + i, 0)),
                  pl.BlockSpec((None, tr, cols), lambda j, i, c_ref: (j, i, 0))],
        out_specs=pl.BlockSpec((None, tr, cols), lambda j, i, c_ref: (j, i, 0)))
    return pl.pallas_call(
        body, name=name, grid_spec=grid_spec,
        out_shape=jax.ShapeDtypeStruct((N_CHIPS, hr, cols), BF16),
        compiler_params=_cp(("parallel", "parallel")))(jnp.reshape(c, (1,)).astype(jnp.int32), slabs, recv)


def _sum_chips(name, own, recv, chip, c):
    _, hr, cols = recv.shape
    tr = _pick(hr, STREAM_TILE // 2)
    nb = hr // tr

    def body(sc_ref, own_ref, recv_ref, o_ref):
        total = own_ref[...].astype(F32)
        for k in range(3):
            total = total + recv_ref[k].astype(F32)
        o_ref[...] = total

    grid_spec = pltpu.PrefetchScalarGridSpec(
        num_scalar_prefetch=1, grid=(nb,),
        in_specs=[pl.BlockSpec((None, tr, cols), lambda i, sc: (sc[0], i, 0)),
                  pl.BlockSpec((3, tr, cols), lambda i, sc: (0, i, 0))],
        out_specs=pl.BlockSpec((tr, cols), lambda i, sc: (sc[1] * nb + i, 0)))
    return pl.pallas_call(
        body, name=name, grid_spec=grid_spec, out_shape=jax.ShapeDtypeStruct((2 * hr, cols), F32),
        compiler_params=_cp(("parallel",)))(jnp.stack([chip, c]).astype(jnp.int32), own, recv)


def _adamw_math(w, g, m, v):
    m = ADAM_B1 * m + (1.0 - ADAM_B1) * g
    v = ADAM_B2 * v + (1.0 - ADAM_B2) * (g * g)
    m_hat = m / (1.0 - ADAM_B1 ** ADAM_STEP)
    v_hat = v / (1.0 - ADAM_B2 ** ADAM_STEP)
    delta = -ADAM_LR * (m_hat / (jnp.sqrt(v_hat) + ADAM_EPS) + ADAM_WD * w)
    return delta, m, v


def _adamw(name, w, gs, m, v):
    nl, r, cols = w.shape
    tr = _pick(r, ROW_TILE)

    def body(*refs):
        w_ref, m_ref, v_ref = refs[0:3]
        g_refs = refs[3:3 + nl]
        go_ref, d_ref, nm_ref, nv_ref = refs[3 + nl:]
        layer = pl.program_id(0)
        g = g_refs[0][...]
        for j in range(1, nl):
            g = jnp.where(layer == j, g_refs[j][...], g)
        d, nm, nv = _adamw_math(w_ref[...], g, m_ref[...], v_ref[...])
        go_ref[...] = g
        d_ref[...] = d
        nm_ref[...] = nm
        nv_ref[...] = nv

    spec3 = pl.BlockSpec((None, tr, cols), lambda l, i: (l, i, 0))
    gspec = pl.BlockSpec((tr, cols), lambda l, i: (i, 0))
    out = jax.ShapeDtypeStruct((nl, r, cols), F32)
    return pl.pallas_call(
        body, name=name, grid=(nl, r // tr), in_specs=[spec3] * 3 + [gspec] * nl, out_specs=[spec3] * 4,
        out_shape=[out] * 4, compiler_params=_cp(("parallel", "parallel")))(w, m, v, *gs)


def _adamw_small(name, groups):
    n = len(groups)
    flat = [a for grp in groups for a in grp]

    def body(*refs):
        ins, outs = refs[:4 * n], refs[4 * n:]
        for p in range(n):
            w_ref, g_ref, m_ref, v_ref = ins[4 * p:4 * p + 4]
            d, nm, nv = _adamw_math(w_ref[...], g_ref[...], m_ref[...], v_ref[...])
            outs[3 * p][...] = d
            outs[3 * p + 1][...] = nm
            outs[3 * p + 2][...] = nv

    vm = pl.BlockSpec(memory_space=pltpu.VMEM)
    out_shape = [jax.ShapeDtypeStruct(grp[0].shape, F32) for grp in groups for _ in range(3)]
    res = pl.pallas_call(
        body, name=name, in_specs=[vm] * (4 * n), out_specs=[vm] * (3 * n), out_shape=out_shape)(*flat)
    return [tuple(res[3 * p:3 * p + 3]) for p in range(n)]


def _block_diag_pairs(w):
    h, d, _ = w.shape
    z = jnp.zeros((h // 2, d, d), w.dtype)
    top = jnp.concatenate([w[0::2], z], axis=2)
    bot = jnp.concatenate([z, w[1::2]], axis=2)
    return jnp.concatenate([top, bot], axis=1).astype(BF16)


def _diag_pairs_to_heads(g, d):
    a = g[:, :d, :d]
    b = g[:, d:, d:]
    return jnp.stack([a, b], axis=1).reshape(-1, d, d)


def _rows128(a):
    flat = a.reshape(-1, LANES)
    pad = (-flat.shape[0]) % SUBLANES
    if pad:
        flat = jnp.concatenate([flat, jnp.zeros((pad, LANES), flat.dtype)], axis=0)
    return flat


def _unshard_last(g4, shape):
    g4 = g4.reshape((N_CHIPS,) + tuple(shape))
    return jnp.concatenate([g4[j] for j in range(N_CHIPS)], axis=-1)


def kernel(x, norm_gains, hyb_w_in, hyb_conv_a, hyb_conv_b, hyb_conv_b_bias, hyb_rg_w_a, hyb_rg_b_a, hyb_rg_w_x, hyb_rg_b_x, hyb_rg_lambda, hyb_w_out, sb_w_qkv, sb_w_o, mlp_w_up, mlp_w_down, loss_target, m_norm_gains, m_hyb_w_in, m_hyb_conv_a, m_hyb_conv_b, m_hyb_conv_b_bias, m_hyb_rg_w_a, m_hyb_rg_b_a, m_hyb_rg_w_x, m_hyb_rg_b_x, m_hyb_rg_lambda, m_hyb_w_out, m_sb_w_qkv, m_sb_w_o, m_mlp_w_up, m_mlp_w_down, v_norm_gains, v_hyb_w_in, v_hyb_conv_a, v_hyb_conv_b, v_hyb_conv_b_bias, v_hyb_rg_w_a, v_hyb_rg_b_a, v_hyb_rg_w_x, v_hyb_rg_b_x, v_hyb_rg_lambda, v_hyb_w_out, v_sb_w_qkv, v_sb_w_o, v_mlp_w_up, v_mlp_w_down):
    cx_ = lax.axis_index("x")
    cy_ = lax.axis_index("y")
    cc_ = lax.axis_index("c")
    chip = 2 * cx_ + cy_

    x0 = x[0]
    target = loss_target[0]
    s, d = x0.shape
    heads = SB_HEADS
    assert d // heads == LANES
    n_rg, hd = hyb_rg_w_a.shape[1], hyb_rg_w_a.shape[2]
    wmix = n_rg * hd
    assert 2 * hd == LANES

    big = {
        "hyb_w_in": (hyb_w_in, 0), "hyb_w_out": (hyb_w_out, 0), "mlp_w_up0": (mlp_w_up, 0),
        "mlp_w_down0": (mlp_w_down, 0), "sb_w_qkv": (sb_w_qkv, 0), "sb_w_o": (sb_w_o, 0),
        "mlp_w_up1": (mlp_w_up, 1), "mlp_w_down1": (mlp_w_down, 1),
    }
    names = list(big)
    slots = [_cast_into_slot("cast_" + k, big[k][0], big[k][1], chip) for k in names]
    full = {k: _allgather_async("allgather_" + k, slot, cid) for cid, (k, slot) in enumerate(zip(names, slots))}
    rowsharded = lambda k: full[k].reshape(-1, full[k].shape[2])

    ng_s, ca_s, cb_s = norm_gains.reshape(-1, norm_gains.shape[2]), hyb_conv_a[0], hyb_conv_b[0]
    packed = jnp.concatenate([_rows128(ng_s), _rows128(ca_s), _rows128(cb_s)], axis=0)
    gathered = _allgather_chips_small("allgather_small", packed)
    n0 = ng_s.size // LANES
    n1 = n0 + (-n0) % SUBLANES
    m0 = ca_s.size // LANES
    m1 = m0 + (-m0) % SUBLANES
    k0 = cb_s.size // LANES
    gains = _unshard_last(gathered[:, 0:n0], ng_s.shape).reshape(2, 4, 1, d)
    conv_a = _unshard_last(gathered[:, n1:n1 + m0], ca_s.shape)
    conv_b = _unshard_last(gathered[:, n1 + m1:n1 + m1 + k0], cb_s.shape)
    bias, b_a, b_x, lam = hyb_conv_b_bias, hyb_rg_b_a, hyb_rg_b_x, hyb_rg_lambda
    wa_blk = _block_diag_pairs(hyb_rg_w_a[0])
    wx_blk = _block_diag_pairs(hyb_rg_w_x[0])

    relu_sq = lambda acc: (jnp.maximum(acc, 0.0), jnp.square(jnp.maximum(acc, 0.0)))

    h1 = _rms_fwd("rms_pre0", x0, gains[0, 0])
    proj = _mm_fwd_col("proj_in", h1, full["hyb_w_in"])[0]
    ycat, hseq = _mixer_fwd(proj, conv_a, conv_b, bias, wa_blk, b_a, wx_blk, b_x, lam)
    mix0 = _mm_fwd_row("proj_out", ycat, rowsharded("hyb_w_out"))
    x1, h2 = _rms_post("rms_mix0", mix0, gains[0, 1], x0, gains[0, 2])
    u0, a0 = _mm_fwd_col("mlp_up0", h2, full["mlp_w_up0"], (BF16, BF16), relu_sq)
    mlp0 = _mm_fwd_row("mlp_down0", a0, rowsharded("mlp_w_down0"))
    x2, h3 = _rms_post("rms_mlp0", mlp0, gains[0, 3], x1, gains[1, 0])

    qkv = _mm_fwd_col("qkv", h3, full["sb_w_qkv"], (BF16,))[0]
    att, tot = _attn_fwd(qkv, heads)
    mix1 = _mm_fwd_row("attn_out", att, rowsharded("sb_w_o"))
    x3, h4 = _rms_post("rms_mix1", mix1, gains[1, 1], x2, gains[1, 2])
    u1, a1 = _mm_fwd_col("mlp_up1", h4, full["mlp_w_up1"], (BF16, BF16), relu_sq)
    mlp1 = _mm_fwd_row("mlp_down1", a1, rowsharded("mlp_w_down1"))
    dy, dmlp1, dgain_mlp1, loss_local = _last_norm_and_loss("last_norm_loss", mlp1, gains[1, 3], x3, target)
    loss = lax.psum(loss_local, ("x", "y", "c"))

    dgain = [[None] * 4 for _ in range(2)]
    drelu = lambda acc, u: (acc * (2.0 * u.astype(F32)),)
    stage_a, stage_b, gfull = {}, {}, {}

    def tie(main, side):
        return lax.optimization_barrier((main, side))

    def reduce_start(k, slab, main):
        main, slab = tie(main, slab)
        stage_a[k] = _to_sibling_async("grads_to_sibling_" + k, slab)
        return main

    def reduce_to_chips(k, main):
        slab, from_sibling = stage_a.pop(k)
        main, part = tie(main, _add_sibling("grads_add_" + k, slab, from_sibling, cc_))
        stage_b[k] = _to_chips_async("grads_to_chips_" + k, part)
        return main

    def reduce_split_start(k, act, dy, cs, main):
        main, other = tie(main, _mm_wgrad_half(k + "_wgrad_sibling_rows", act, dy, 1 - cc_, cs))
        stage_a[k] = (act, dy, cs, _swap_with_sibling_async("grads_to_sibling_" + k, other))
        return main

    def reduce_split_to_chips(k, main):
        act, dy, cs, from_sibling = stage_a.pop(k)
        main, part = tie(main, _mm_wgrad_half(k + "_wgrad_my_rows", act, dy, cc_, cs, init=from_sibling))
        stage_b[k] = _to_chips_async("grads_to_chips_" + k, part)
        return main

    def after(value, token):
        return tie(value, token)[0]

    def reduce_finish(k, main):
        own, from_chips = stage_b.pop(k)
        main, half = tie(main, _sum_chips("grads_sum_" + k, after(own, main), from_chips, chip, cc_))
        gfull[k] = _join_sibling_async("grads_join_" + k, half)
        return main

    def mlp_bwd(layer, dxo, dmlp, xin, hin, u, a, mix):
        down, up = f"mlp_w_down{layer}", f"mlp_w_up{layer}"
        wd, wu = rowsharded(down), full[up]
        dmlp = reduce_split_start(down, a, dmlp, None, dmlp)
        du = _mm_bwd_row(f"mlp_down{layer}_bwd", dmlp, wd, (BF16,), u, drelu)[0]
        du = reduce_split_start(up, hin, du, wu.shape[2], du)
        du = reduce_split_to_chips(down, du)
        dh = _mm_bwd_col(f"mlp_up{layer}_bwd", du, wu)
        dh = reduce_split_to_chips(up, dh)
        dxm, dgain[layer][2], dmix, dgain[layer][1] = _rms_bwd_pair(
            f"rms_premlp{layer}_mix{layer}_bwd", xin, gains[layer, 2], dh, dxo, mix, gains[layer, 1])
        return dxm, dmix

    dgain[1][3] = dgain_mlp1
    dx3, dmix1 = mlp_bwd(1, dy, dmlp1, x3, h4, u1, a1, mix1)
    dmix1 = reduce_start("sb_w_o", _mm_wgrad_row("attn_out_wgrad", att, dmix1).reshape(N_CHIPS, -1, d), dmix1)
    datt = _mm_bwd_row("attn_out_bwd", dmix1, rowsharded("sb_w_o"), (BF16,))[0]
    dq, dk, dv = _attn_bwd(qkv, tot, datt, heads)
    dqkv = jnp.concatenate([dq, dk, dv], axis=1)
    dqkv = reduce_to_chips("sb_w_o", dqkv)
    dqkv = reduce_finish("mlp_w_down1", dqkv)
    dqkv = reduce_finish("mlp_w_up1", dqkv)
    dqkv = reduce_split_start("sb_w_qkv", h3, dqkv, full["sb_w_qkv"].shape[2], dqkv)
    dh3 = _mm_bwd_col("qkv_bwd", dqkv, full["sb_w_qkv"])
    dh3 = reduce_split_to_chips("sb_w_qkv", dh3)
    dx2, dgain[1][0], dmlp0, dgain[0][3] = _rms_bwd_pair(
        "rms_pre1_mlp0_bwd", x2, gains[1, 0], dh3, dx3, mlp0, gains[0, 3])

    dx1, dmix0 = mlp_bwd(0, dx2, dmlp0, x1, h2, u0, a0, mix0)
    dmix0 = reduce_finish("sb_w_o", dmix0)
    dmix0 = reduce_finish("sb_w_qkv", dmix0)
    dmix0 = reduce_finish("mlp_w_down0", dmix0)
    dmix0 = reduce_start("hyb_w_out", _mm_wgrad_row("proj_out_wgrad", ycat, dmix0).reshape(N_CHIPS, -1, d), dmix0)
    dycat = _mm_bwd_row("proj_out_bwd", dmix0, rowsharded("hyb_w_out"))[0]
    dproj, xr_b, dpa_b, dpx_b, sg = _mixer_bwd(
        proj, hseq, dycat, conv_a, conv_b, bias, wa_blk, b_a, wx_blk, b_x, lam)
    dproj = reduce_finish("mlp_w_up0", dproj)
    dproj = reduce_to_chips("hyb_w_out", dproj)
    dproj = reduce_split_start("hyb_w_in", h1, dproj, full["hyb_w_in"].shape[2], dproj)
    dh1 = _mm_bwd_col("proj_in_bwd", dproj, full["hyb_w_in"])
    dh1 = reduce_split_to_chips("hyb_w_in", dh1)
    dx0, dgain[0][0] = _rms_bwd("rms_pre0_bwd", x0, gains[0, 0], dh1, res=dx1)
    dwa = _diag_pairs_to_heads(_mm_wgrad_diag("rg_w_a_wgrad", xr_b, dpa_b), hd)
    dwx = _diag_pairs_to_heads(_mm_wgrad_diag("rg_w_x_wgrad", xr_b, dpx_b), hd)

    dgains = jnp.concatenate([dgain[l][k] for l in range(2) for k in range(4)], axis=0)
    small_parts = [dgains, sg[_SG_CONV_A:_SG_CONV_A + 3], sg[_SG_CONV_B:_SG_CONV_B + 4], sg[_SG_BIAS:_SG_BIAS + 1],
                   dwa, sg[_SG_BA:_SG_BA + 1], dwx, sg[_SG_BX:_SG_BX + 1], sg[_SG_LAM:_SG_LAM + 1]]
    small_rows = [_rows128(p) for p in small_parts]
    n_small = sum(rws.shape[0] for rws in small_rows)
    tail_pad = [jnp.zeros(((-n_small) % (2 * SUBLANES), LANES), F32)] if n_small % (2 * SUBLANES) else []
    reduced = _allreduce_small("allreduce_small", jnp.concatenate(small_rows + tail_pad, axis=0))
    small_full, off = [], 0
    for p, rws in zip(small_parts, small_rows):
        small_full.append(reduced[off:off + p.size // LANES].reshape(p.shape))
        off += rws.shape[0]
    g_gains, g_ca, g_cb, g_bias, g_wa, g_ba, g_wx, g_bx, g_lam = small_full

    def my_cols(g, width):
        return lax.dynamic_slice_in_dim(g, chip * width, width, axis=g.ndim - 1)

    small = [
        ("norm_gains", norm_gains, my_cols(g_gains, norm_gains.shape[2]).reshape(norm_gains.shape),
         m_norm_gains, v_norm_gains),
        ("hyb_conv_a", hyb_conv_a, my_cols(g_ca, hyb_conv_a.shape[2])[None], m_hyb_conv_a, v_hyb_conv_a),
        ("hyb_conv_b", hyb_conv_b, my_cols(g_cb, hyb_conv_b.shape[2])[None], m_hyb_conv_b, v_hyb_conv_b),
        ("hyb_conv_b_bias", hyb_conv_b_bias, g_bias, m_hyb_conv_b_bias, v_hyb_conv_b_bias),
        ("hyb_rg_w_a", hyb_rg_w_a, g_wa[None], m_hyb_rg_w_a, v_hyb_rg_w_a),
        ("hyb_rg_b_a", hyb_rg_b_a, g_ba, m_hyb_rg_b_a, v_hyb_rg_b_a),
        ("hyb_rg_w_x", hyb_rg_w_x, g_wx[None], m_hyb_rg_w_x, v_hyb_rg_w_x),
        ("hyb_rg_b_x", hyb_rg_b_x, g_bx, m_hyb_rg_b_x, v_hyb_rg_b_x),
        ("hyb_rg_lambda", hyb_rg_lambda, g_lam, m_hyb_rg_lambda, v_hyb_rg_lambda),
    ]
    to2d = lambda a: a.reshape(-1, a.shape[-1])
    small_res = _adamw_small("adamw_small", [tuple(to2d(a) for a in (w, g, m, v)) for _, w, g, m, v in small])
    out = {}
    for (nm, w, g, _, _), (dl, nmom, nvar) in zip(small, small_res):
        out[nm] = (g, dl.reshape(w.shape), nmom.reshape(w.shape), nvar.reshape(w.shape))

    stacked = {
        "mlp_w_down": (mlp_w_down, m_mlp_w_down, v_mlp_w_down, ["mlp_w_down0", "mlp_w_down1"]),
        "mlp_w_up": (mlp_w_up, m_mlp_w_up, v_mlp_w_up, ["mlp_w_up0", "mlp_w_up1"]),
        "sb_w_o": (sb_w_o, m_sb_w_o, v_sb_w_o, ["sb_w_o"]),
        "sb_w_qkv": (sb_w_qkv, m_sb_w_qkv, v_sb_w_qkv, ["sb_w_qkv"]),
        "hyb_w_out": (hyb_w_out, m_hyb_w_out, v_hyb_w_out, ["hyb_w_out"]),
        "hyb_w_in": (hyb_w_in, m_hyb_w_in, v_hyb_w_in, ["hyb_w_in"]),
    }

    def update(k, token):
        w, m, v, parts = stacked[k]
        out[k] = tuple(_adamw("adamw_" + k, w, [after(gfull[p], token) for p in parts], m, v))
        return out[k][1]

    token = small_res[0][0]
    token = update("sb_w_qkv", token)
    token = update("sb_w_o", token)
    token = update("mlp_w_down", token)
    token = reduce_finish("hyb_w_out", token)
    token = update("mlp_w_up", token)
    token = reduce_finish("hyb_w_in", token)
    token = update("hyb_w_out", token)
    update("hyb_w_in", token)

    order = ["norm_gains", "hyb_w_in", "hyb_conv_a", "hyb_conv_b", "hyb_conv_b_bias", "hyb_rg_w_a", "hyb_rg_b_a",
             "hyb_rg_w_x", "hyb_rg_b_x", "hyb_rg_lambda", "hyb_w_out", "sb_w_qkv", "sb_w_o", "mlp_w_up",
             "mlp_w_down"]
    return (loss, dx0[None], *[out[k][0] for k in order], *[out[k][1] for k in order],
            *[out[k][2] for k in order], *[out[k][3] for k in order])
```

```python
import functools
import math

import jax
import jax.numpy as jnp
from jax import lax
from jax.experimental import pallas as pl
from jax.experimental.pallas import tpu as pltpu
from jax.experimental.pallas import tpu_sc as plsc

F32 = jnp.float32
BF16 = jnp.bfloat16
MESH = pl.DeviceIdType.MESH

SB_HEADS = 16
NORM_EPS = 1e-6
LRU_C = 8.0
ADAM_LR = 0.001
ADAM_B1 = 0.9
ADAM_B2 = 0.999
ADAM_EPS = 1e-08
ADAM_WD = 0.01
ADAM_STEP = 10

LANES = 128
SUBLANES = 8
VMEM_LIMIT = 48 * 1024 * 1024
MM_TILE = 1024
MM_VMEM_BUDGET = 40 * 1024 * 1024
MM_TILE_N = 1280
MM_TILE_K = 2048
ROW_TILE = 256
STREAM_TILE = 1024
ATT_TILE = 512
ATT_HEADS_PER_STEP = 2
ATT_FWD_HEADS_PER_STEP = 4
N_CHIPS = 4
COLLECTIVE_SIBLING = 8
COLLECTIVE_CHIPS = 9

_DIMS = {
    "nn": (((1,), (0,)), ((), ())),
    "nt": (((1,), (1,)), ((), ())),
    "tn": (((0,), (0,)), ((), ())),
}


def _cp(sem=None, vmem=VMEM_LIMIT):
    return pltpu.CompilerParams(dimension_semantics=sem, vmem_limit_bytes=vmem)


def _pick(dim, pref):
    t = min(dim, pref)
    while dim % t:
        t -= LANES
    return t


def _whole(shape):
    nd = len(shape)
    return pl.BlockSpec(tuple(shape), lambda *_: (0,) * nd)


def _sigmoid(z):
    return 1.0 / (1.0 + jnp.exp(-z))


def _log_sigmoid(z):
    return jnp.minimum(z, 0.0) - jnp.log(1.0 + jnp.exp(-jnp.abs(z)))


def _expm1(z):
    series = z * (1.0 + z * (0.5 + z * (1.0 / 6.0 + z * (1.0 / 24.0))))
    return jnp.where(jnp.abs(z) < 0.05, series, jnp.exp(z) - 1.0)


_GELU_C = math.sqrt(2.0 / math.pi)


def _gelu_and_grad(g):
    inner = _GELU_C * (g + 0.044715 * g * g * g)
    t = jnp.tanh(inner)
    val = 0.5 * g * (1.0 + t)
    grad = 0.5 * (1.0 + t) + 0.5 * g * (1.0 - t * t) * _GELU_C * (1.0 + 3.0 * 0.044715 * g * g)
    return val, grad


def _shift_down(cur, prev8, k, rows):
    n = cur.shape[0]
    rolled = pltpu.roll(cur, k, 0)
    head = jnp.tile(pltpu.roll(prev8, k, 0), (n // SUBLANES, 1))
    return jnp.where(rows < k, head, rolled)


def _shift_up(cur, next8, k, rows):
    n = cur.shape[0]
    rolled = pltpu.roll(cur, n - k, 0)
    tail = jnp.tile(pltpu.roll(next8, SUBLANES - k, 0), (n // SUBLANES, 1))
    return jnp.where(rows >= n - k, tail, rolled)


def _colsum(v):
    return jnp.sum(v, axis=0, keepdims=True)


def _matmul(name, mode, grid, operands, in_specs, out_shapes, out_specs, acc_shape, epilogue=None):
    nk = grid[2]
    n_in = len(operands)
    dims = _DIMS[mode]

    def finish(acc, extra, outs):
        res = epilogue(acc, *[e[...] for e in extra]) if epilogue is not None else (acc,)
        for o_ref, o in zip(outs, res):
            o_ref[...] = o.astype(o_ref.dtype)

    def product(a_ref, b_ref):
        return lax.dot_general(a_ref[...].astype(BF16), b_ref[...].astype(BF16), dims, preferred_element_type=F32)

    def body_single(*refs):
        finish(product(refs[0], refs[1]), refs[2:n_in], refs[n_in:])

    def body(*refs):
        extra = refs[2:n_in]
        outs = refs[n_in:-1]
        acc_ref = refs[-1]
        k = pl.program_id(2)

        @pl.when(k == 0)
        def _():
            acc_ref[...] = product(refs[0], refs[1])

        @pl.when(k > 0)
        def _():
            acc_ref[...] += product(refs[0], refs[1])

        @pl.when(k == nk - 1)
        def _():
            finish(acc_ref[...], extra, outs)

    return pl.pallas_call(
        body_single if nk == 1 else body, name=name, grid=grid, in_specs=in_specs, out_specs=out_specs,
        out_shape=out_shapes, scratch_shapes=[] if nk == 1 else [pltpu.VMEM(acc_shape, F32)],
        compiler_params=_cp(("parallel", "parallel", "arbitrary")),
    )(*operands)


def _pick_m(m, tk, tn, a_dtype, b_dtype, out_dtypes, extra_dtypes=()):
    size = lambda dt: jnp.dtype(dt).itemsize
    per_row = 2 * tk * size(a_dtype) + tn * (2 * sum(size(dt) for dt in tuple(out_dtypes) + tuple(extra_dtypes)) + 4)
    fixed = 2 * tk * tn * size(b_dtype)
    tm = _pick(m, MM_TILE)
    while tm > LANES and tm * per_row + fixed > MM_VMEM_BUDGET:
        tm = _pick(m, tm // 2)
    return tm


def _mm_fwd_col(name, a, wfull, out_dtypes=(F32,), epilogue=None):
    s, kdim = a.shape
    _, _, cs = wfull.shape
    tk, tn = _pick(kdim, MM_TILE_K), _pick(cs, MM_TILE_N)
    tm = _pick_m(s, tk, tn, a.dtype, wfull.dtype, out_dtypes)
    nbj = cs // tn
    grid = (s // tm, N_CHIPS * nbj, kdim // tk)
    out_shapes = [jax.ShapeDtypeStruct((s, N_CHIPS * cs), dt) for dt in out_dtypes]
    out_specs = [pl.BlockSpec((tm, tn), lambda i, n, k: (i, n)) for _ in out_dtypes]
    return _matmul(
        name, "nn", grid, [a, wfull],
        [pl.BlockSpec((tm, tk), lambda i, n, k: (i, k)),
         pl.BlockSpec((None, tk, tn), lambda i, n, k: (n // nbj, k, n % nbj))],
        out_shapes, out_specs, (tm, tn), epilogue)


def _mm_fwd_row(name, a, w2d, out_dtype=F32):
    s, kdim = a.shape
    _, n_out = w2d.shape
    tk, tn = _pick(kdim, MM_TILE_K), _pick(n_out, MM_TILE)
    tm = _pick_m(s, tk, tn, a.dtype, w2d.dtype, (out_dtype,))
    grid = (s // tm, n_out // tn, kdim // tk)
    return _matmul(
        name, "nn", grid, [a, w2d],
        [pl.BlockSpec((tm, tk), lambda i, n, k: (i, k)),
         pl.BlockSpec((tk, tn), lambda i, n, k: (k, n))],
        [jax.ShapeDtypeStruct((s, n_out), out_dtype)],
        [pl.BlockSpec((tm, tn), lambda i, n, k: (i, n))], (tm, tn))[0]


def _mm_bwd_col(name, dy, wfull, out_dtype=BF16):
    s, _ = dy.shape
    _, kdim, cs = wfull.shape
    tn, tk = _pick(kdim, MM_TILE), _pick(cs, MM_TILE_K)
    tm = _pick_m(s, tk, tn, dy.dtype, wfull.dtype, (out_dtype,))
    nbj = cs // tk
    grid = (s // tm, kdim // tn, N_CHIPS * nbj)
    return _matmul(
        name, "nt", grid, [dy, wfull],
        [pl.BlockSpec((tm, tk), lambda i, n, k: (i, k)),
         pl.BlockSpec((None, tn, tk), lambda i, n, k: (k // nbj, n, k % nbj))],
        [jax.ShapeDtypeStruct((s, kdim), out_dtype)],
        [pl.BlockSpec((tm, tn), lambda i, n, k: (i, n))], (tm, tn))[0]


def _mm_bwd_row(name, dy, w2d, out_dtypes=(F32,), extra=None, epilogue=None):
    s, n_in = dy.shape
    kdim, _ = w2d.shape
    tn, tk = _pick(kdim, MM_TILE), _pick(n_in, MM_TILE_K)
    tm = _pick_m(s, tk, tn, dy.dtype, w2d.dtype, out_dtypes, () if extra is None else (extra.dtype,))
    grid = (s // tm, kdim // tn, n_in // tk)
    operands = [dy, w2d]
    in_specs = [pl.BlockSpec((tm, tk), lambda i, n, k: (i, k)),
                pl.BlockSpec((tn, tk), lambda i, n, k: (n, k))]
    if extra is not None:
        operands.append(extra)
        in_specs.append(pl.BlockSpec((tm, tn), lambda i, n, k: (i, n)))
    return _matmul(
        name, "nt", grid, operands, in_specs,
        [jax.ShapeDtypeStruct((s, kdim), dt) for dt in out_dtypes],
        [pl.BlockSpec((tm, tn), lambda i, n, k: (i, n)) for _ in out_dtypes], (tm, tn), epilogue)


def _mm_wgrad_row(name, a, dy):
    s, kdim = a.shape
    _, n_out = dy.shape
    tn, ts = _pick(n_out, MM_TILE), _pick(s, MM_TILE_K)
    tm = _pick_m(kdim, ts, tn, a.dtype, dy.dtype, (BF16,))
    grid = (kdim // tm, n_out // tn, s // ts)
    return _matmul(
        name, "tn", grid, [a, dy],
        [pl.BlockSpec((ts, tm), lambda i, n, k: (k, i)),
         pl.BlockSpec((ts, tn), lambda i, n, k: (k, n))],
        [jax.ShapeDtypeStruct((kdim, n_out), BF16)],
        [pl.BlockSpec((tm, tn), lambda i, n, k: (i, n))], (tm, tn))[0]


def _mm_wgrad_half(name, a, dy, half, cs=None, init=None):
    s, kdim = a.shape
    ts = _pick(s, MM_TILE_K)
    nk = s // ts
    if cs is not None:
        hr, cols = kdim // 2, cs
        tn = _pick(cs, MM_TILE_N)
        tm = _pick_m(hr, ts, tn, a.dtype, dy.dtype, (BF16,), (BF16,))
        ni, nbj = hr // tm, cs // tn
        grid = (ni, N_CHIPS * nbj, nk)
        a_map = lambda i, n, k, h: (k, h[0] * ni + i)
        o_map = lambda i, n, k, h: (n // nbj, i, n % nbj)
    else:
        hr, cols = kdim // N_CHIPS // 2, dy.shape[1]
        tn = _pick(cols, MM_TILE)
        tm = _pick_m(hr, ts, tn, a.dtype, dy.dtype, (BF16,), (BF16,))
        ni = hr // tm
        grid = (N_CHIPS * ni, cols // tn, nk)
        a_map = lambda i, n, k, h: (k, (i // ni) * 2 * ni + h[0] * ni + i % ni)
        o_map = lambda i, n, k, h: (i // ni, i % ni, n)
    with_init = init is not None

    def body(*refs):
        a_ref, b_ref = refs[1], refs[2]
        init_ref = refs[3] if with_init else None
        o_ref, acc_ref = refs[-2], refs[-1]
        k = pl.program_id(2)

        def product():
            return lax.dot_general(a_ref[...].astype(BF16), b_ref[...].astype(BF16), _DIMS["tn"],
                                   preferred_element_type=F32)

        @pl.when(k == 0)
        def _():
            if with_init:
                acc_ref[...] = init_ref[...].astype(F32)
                acc_ref[...] += product()
            else:
                acc_ref[...] = product()

        @pl.when(k > 0)
        def _():
            acc_ref[...] += product()

        @pl.when(k == nk - 1)
        def _():
            o_ref[...] = acc_ref[...].astype(BF16)

    oblk = pl.BlockSpec((None, tm, tn), o_map)
    grid_spec = pltpu.PrefetchScalarGridSpec(
        num_scalar_prefetch=1, grid=grid,
        in_specs=[pl.BlockSpec((ts, tm), a_map), pl.BlockSpec((ts, tn), lambda i, n, k, h: (k, n))]
        + ([oblk] if with_init else []),
        out_specs=oblk, scratch_shapes=[pltpu.VMEM((tm, tn), F32)])
    operands = [jnp.reshape(half, (1,)).astype(jnp.int32), a, dy] + ([init] if with_init else [])
    return pl.pallas_call(
        body, name=name, grid_spec=grid_spec, out_shape=jax.ShapeDtypeStruct((N_CHIPS, hr, cols), BF16),
        compiler_params=_cp(("parallel", "parallel", "arbitrary")))(*operands)


def _mm_wgrad_diag(name, a, dy):
    s, width = a.shape
    nb = width // LANES
    ts = _pick(s, MM_TILE)
    grid = (nb, 1, s // ts)
    return _matmul(
        name, "tn", grid, [a, dy],
        [pl.BlockSpec((ts, LANES), lambda i, n, k: (k, i)),
         pl.BlockSpec((ts, LANES), lambda i, n, k: (k, i))],
        [jax.ShapeDtypeStruct((nb, LANES, LANES), F32)],
        [pl.BlockSpec((None, LANES, LANES), lambda i, n, k: (i, 0, 0))], (LANES, LANES))[0]


def _rowspec(tr, d):
    return pl.BlockSpec((tr, d), lambda i: (i, 0))


def _vecspec(d):
    return pl.BlockSpec((1, d), lambda i: (0, 0))


def _rms(x, g):
    return x * lax.rsqrt(jnp.mean(x * x, axis=-1, keepdims=True) + NORM_EPS) * g


def _cast_into_slot(name, w, layer, chip):
    _, r, c = w.shape
    tr = _pick(r, STREAM_TILE)

    def body(chip_ref, w_ref, o_ref):
        o_ref[...] = w_ref[...].astype(BF16)

    grid_spec = pltpu.PrefetchScalarGridSpec(
        num_scalar_prefetch=1, grid=(r // tr,),
        in_specs=[pl.BlockSpec((None, tr, c), lambda i, chip_ref: (layer, i, 0))],
        out_specs=pl.BlockSpec((None, tr, c), lambda i, chip_ref: (chip_ref[0], i, 0)))
    return pl.pallas_call(
        body, name=name, grid_spec=grid_spec, out_shape=jax.ShapeDtypeStruct((N_CHIPS, r, c), BF16),
        compiler_params=_cp(("parallel",)))(jnp.reshape(chip, (1,)).astype(jnp.int32), w)


def _rms_fwd(name, x, g):
    s, d = x.shape
    tr = _pick(s, ROW_TILE)

    def body(x_ref, g_ref, h_ref):
        h_ref[...] = _rms(x_ref[...], g_ref[...]).astype(BF16)

    return pl.pallas_call(
        body, name=name, grid=(s // tr,), in_specs=[_rowspec(tr, d), _vecspec(d)],
        out_specs=_rowspec(tr, d), out_shape=jax.ShapeDtypeStruct((s, d), BF16),
        compiler_params=_cp(("parallel",)))(x, g)


def _rms_post(name, y, g_post, res, g_next=None):
    s, d = y.shape
    tr = _pick(s, ROW_TILE)
    with_next = g_next is not None

    def body(*refs):
        if with_next:
            y_ref, gp_ref, r_ref, gn_ref, x_ref, h_ref = refs
        else:
            y_ref, gp_ref, r_ref, x_ref = refs
        xn = r_ref[...] + _rms(y_ref[...], gp_ref[...])
        x_ref[...] = xn
        if with_next:
            h_ref[...] = _rms(xn, gn_ref[...]).astype(BF16)

    operands = [y, g_post, res] + ([g_next] if with_next else [])
    in_specs = [_rowspec(tr, d), _vecspec(d), _rowspec(tr, d)] + ([_vecspec(d)] if with_next else [])
    out_shape = [jax.ShapeDtypeStruct((s, d), F32)] + ([jax.ShapeDtypeStruct((s, d), BF16)] if with_next else [])
    out_specs = [_rowspec(tr, d)] + ([_rowspec(tr, d)] if with_next else [])
    return pl.pallas_call(
        body, name=name, grid=(s // tr,), in_specs=in_specs, out_specs=out_specs, out_shape=out_shape,
        compiler_params=_cp(("parallel",)))(*operands)


def _rms_bwd(name, x, g, dy, res=None, out_dtype=F32):
    s, d = x.shape
    tr = _pick(s, ROW_TILE)
    nsteps = s // tr
    with_res = res is not None

    def body(*refs):
        if with_res:
            x_ref, g_ref, dy_ref, r_ref, dx_ref, dg_ref, acc_ref = refs
        else:
            x_ref, g_ref, dy_ref, dx_ref, dg_ref, acc_ref = refs
        i = pl.program_id(0)

        @pl.when(i == 0)
        def _():
            acc_ref[...] = jnp.zeros_like(acc_ref)

        xv = x_ref[...]
        dyv = dy_ref[...].astype(F32)
        r = lax.rsqrt(jnp.mean(xv * xv, axis=-1, keepdims=True) + NORM_EPS)
        xhat = xv * r
        gy = dyv * g_ref[...]
        dx = r * (gy - xhat * jnp.mean(gy * xhat, axis=-1, keepdims=True))
        if with_res:
            dx = dx + r_ref[...]
        dx_ref[...] = dx.astype(dx_ref.dtype)
        acc_ref[...] += jnp.sum((dyv * xhat).reshape(tr // SUBLANES, SUBLANES, d), axis=0)

        @pl.when(i == nsteps - 1)
        def _():
            dg_ref[...] = jnp.broadcast_to(_colsum(acc_ref[...]), (SUBLANES, d))

    operands = [x, g, dy] + ([res] if with_res else [])
    in_specs = [_rowspec(tr, d), _vecspec(d), _rowspec(tr, d)] + ([_rowspec(tr, d)] if with_res else [])
    dx, dg = pl.pallas_call(
        body, name=name, grid=(nsteps,), in_specs=in_specs,
        out_specs=[_rowspec(tr, d), pl.BlockSpec((SUBLANES, d), lambda i: (0, 0))],
        out_shape=[jax.ShapeDtypeStruct((s, d), out_dtype), jax.ShapeDtypeStruct((SUBLANES, d), F32)],
        scratch_shapes=[pltpu.VMEM((SUBLANES, d), F32)],
        compiler_params=_cp(("arbitrary",)))(*operands)
    return dx, dg[0:1]


def _rms_bwd_pair(name, x, g, dy, res, y2, g2):
    s, d = x.shape
    tr = _pick(s, ROW_TILE)
    nsteps = s // tr

    def through(xv, gv, dyv):
        r = lax.rsqrt(jnp.mean(xv * xv, axis=-1, keepdims=True) + NORM_EPS)
        xhat = xv * r
        gy = dyv * gv
        dx = r * (gy - xhat * jnp.mean(gy * xhat, axis=-1, keepdims=True))
        return dx, jnp.sum((dyv * xhat).reshape(tr // SUBLANES, SUBLANES, d), axis=0)

    def body(x_ref, g_ref, dy_ref, r_ref, y2_ref, g2_ref, dx_ref, d2_ref, dg_ref, dg2_ref, acc_ref, acc2_ref):
        i = pl.program_id(0)

        @pl.when(i == 0)
        def _():
            acc_ref[...] = jnp.zeros_like(acc_ref)
            acc2_ref[...] = jnp.zeros_like(acc2_ref)

        dx, part = through(x_ref[...], g_ref[...], dy_ref[...].astype(F32))
        dx = dx + r_ref[...]
        dx_ref[...] = dx
        acc_ref[...] += part
        d2, part2 = through(y2_ref[...], g2_ref[...], dx)
        d2_ref[...] = d2.astype(d2_ref.dtype)
        acc2_ref[...] += part2

        @pl.when(i == nsteps - 1)
        def _():
            dg_ref[...] = jnp.broadcast_to(_colsum(acc_ref[...]), (SUBLANES, d))
            dg2_ref[...] = jnp.broadcast_to(_colsum(acc2_ref[...]), (SUBLANES, d))

    row, vec = _rowspec(tr, d), _vecspec(d)
    gspec = pl.BlockSpec((SUBLANES, d), lambda i: (0, 0))
    dx, d2, dg, dg2 = pl.pallas_call(
        body, name=name, grid=(nsteps,), in_specs=[row, vec, row, row, row, vec],
        out_specs=[row, row, gspec, gspec],
        out_shape=[jax.ShapeDtypeStruct((s, d), F32), jax.ShapeDtypeStruct((s, d), BF16),
                   jax.ShapeDtypeStruct((SUBLANES, d), F32), jax.ShapeDtypeStruct((SUBLANES, d), F32)],
        scratch_shapes=[pltpu.VMEM((SUBLANES, d), F32), pltpu.VMEM((SUBLANES, d), F32)],
        compiler_params=_cp(("arbitrary",)))(x, g, dy, res, y2, g2)
    return dx, dg[0:1], d2, dg2[0:1]


def _last_norm_and_loss(name, y, g, res, target):
    s, d = y.shape
    tr = _pick(s, ROW_TILE)
    nsteps = s // tr

    def body(y_ref, g_ref, r_ref, t_ref, dx_ref, dy_ref, dg_ref, l_ref, acc_ref, lacc_ref):
        i = pl.program_id(0)

        @pl.when(i == 0)
        def _():
            acc_ref[...] = jnp.zeros_like(acc_ref)
            lacc_ref[...] = jnp.zeros_like(lacc_ref)

        yv = y_ref[...]
        gv = g_ref[...]
        r = lax.rsqrt(jnp.mean(yv * yv, axis=-1, keepdims=True) + NORM_EPS)
        yhat = yv * r
        err = r_ref[...] + yhat * gv - t_ref[...]
        dx = err * (1.0 / d)
        dx_ref[...] = dx
        lacc_ref[...] += jnp.sum((err * err).reshape(tr // SUBLANES, SUBLANES, d), axis=0)
        gy = dx * gv
        dy_ref[...] = (r * (gy - yhat * jnp.mean(gy * yhat, axis=-1, keepdims=True))).astype(dy_ref.dtype)
        acc_ref[...] += jnp.sum((dx * yhat).reshape(tr // SUBLANES, SUBLANES, d), axis=0)

        @pl.when(i == nsteps - 1)
        def _():
            dg_ref[...] = jnp.broadcast_to(_colsum(acc_ref[...]), (SUBLANES, d))
            l_ref[...] = jnp.full((SUBLANES, LANES), (0.5 / d) * jnp.sum(lacc_ref[...]), F32)

    dx, dy, dg, l = pl.pallas_call(
        body, name=name, grid=(nsteps,),
        in_specs=[_rowspec(tr, d), _vecspec(d), _rowspec(tr, d), _rowspec(tr, d)],
        out_specs=[_rowspec(tr, d), _rowspec(tr, d), pl.BlockSpec((SUBLANES, d), lambda i: (0, 0)),
                   pl.BlockSpec((SUBLANES, LANES), lambda i: (0, 0))],
        out_shape=[jax.ShapeDtypeStruct((s, d), F32), jax.ShapeDtypeStruct((s, d), BF16),
                   jax.ShapeDtypeStruct((SUBLANES, d), F32), jax.ShapeDtypeStruct((SUBLANES, LANES), F32)],
        scratch_shapes=[pltpu.VMEM((SUBLANES, d), F32), pltpu.VMEM((SUBLANES, d), F32)],
        compiler_params=_cp(("arbitrary",)))(y, g, res, target)
    return dx, dy, dg[0:1], l[0, 0]


def _gates(xr, wa, ba, wx, bx, lam):
    xb = xr.astype(BF16)
    r = _sigmoid(jnp.dot(xb, wa, preferred_element_type=F32) + ba)
    i = _sigmoid(jnp.dot(xb, wx, preferred_element_type=F32) + bx)
    log_a = LRU_C * r * _log_sigmoid(lam)
    a = jnp.exp(log_a)
    m = jnp.sqrt(-_expm1(2.0 * log_a))
    return r, i, a, m


def _mixer_fwd(proj, conv_a, conv_b, bias, wa_blk, ba, wx_blk, bx, lam):
    s, w5 = proj.shape
    w = w5 // 5
    nch = w // LANES
    ts = _pick(s, ROW_TILE)
    nt = s // ts

    def body(p_ref, pp_ref, ca_ref, cb_ref, bias_ref, wa_ref, ba_ref, wx_ref, bx_ref, lam_ref,
             y_ref, h_ref, a_scr, b_scr, hc_scr):
        t = pl.program_id(0)
        first = t == 0
        rows = lax.broadcasted_iota(jnp.int32, (ts, LANES), 0)

        @pl.when(first)
        def _():
            hc_scr[...] = jnp.zeros_like(hc_scr)

        def cur(comp, c):
            return p_ref[:, comp * w + c * LANES:comp * w + (c + 1) * LANES]

        def prev(comp, c):
            v = pp_ref[:, comp * w + c * LANES:comp * w + (c + 1) * LANES]
            return jnp.where(first, 0.0, v)

        for c in range(nch):
            sl = slice(c * LANES, (c + 1) * LANES)
            cx = cur(1, c) * cur(2, c)
            cxp = prev(1, c) * prev(2, c)
            wa3 = ca_ref[:, sl]
            conv = (wa3[2:3] * cx + wa3[1:2] * _shift_down(cx, cxp, 1, rows)
                    + wa3[0:1] * _shift_down(cx, cxp, 2, rows))
            y_ref[:, sl] = (cur(0, c) * conv).astype(BF16)

        for c in range(nch):
            sl = slice(c * LANES, (c + 1) * LANES)
            xb, xbp = cur(4, c), prev(4, c)
            wb4 = cb_ref[:, sl]
            xr = (wb4[3:4] * xb + wb4[2:3] * _shift_down(xb, xbp, 1, rows)
                  + wb4[1:2] * _shift_down(xb, xbp, 2, rows)
                  + wb4[0:1] * _shift_down(xb, xbp, 3, rows) + bias_ref[:, sl])
            _, i, a, m = _gates(xr, wa_ref[c], ba_ref[:, sl], wx_ref[c], bx_ref[:, sl], lam_ref[:, sl])
            a_scr[:, sl] = a
            b_scr[:, sl] = m * i * xr

        def step(r, h):
            h = a_scr[pl.ds(r, 1), :] * h + b_scr[pl.ds(r, 1), :]
            h_ref[pl.ds(r, 1), :] = h
            return h

        hc_scr[0:1, :] = lax.fori_loop(0, ts, step, hc_scr[0:1, :], unroll=8)

        for c in range(nch):
            sl = slice(c * LANES, (c + 1) * LANES)
            gel, _ = _gelu_and_grad(cur(3, c))
            y_ref[:, w + c * LANES:w + (c + 1) * LANES] = (h_ref[:, sl] * gel).astype(BF16)

    vec = lambda n: _whole((n, w))
    return pl.pallas_call(
        body, name="mixer_fwd", grid=(nt,),
        in_specs=[pl.BlockSpec((ts, w5), lambda t: (t, 0)),
                  pl.BlockSpec((SUBLANES, w5), lambda t: (jnp.maximum(t * (ts // SUBLANES) - 1, 0), 0)),
                  vec(3), vec(4), vec(1), _whole(wa_blk.shape), vec(1), _whole(wx_blk.shape), vec(1), vec(1)],
        out_specs=[pl.BlockSpec((ts, 2 * w), lambda t: (t, 0)), pl.BlockSpec((ts, w), lambda t: (t, 0))],
        out_shape=[jax.ShapeDtypeStruct((s, 2 * w), BF16), jax.ShapeDtypeStruct((s, w), F32)],
        scratch_shapes=[pltpu.VMEM((ts, w), F32), pltpu.VMEM((ts, w), F32), pltpu.VMEM((SUBLANES, w), F32)],
        compiler_params=_cp(("arbitrary",)),
    )(proj, proj, conv_a, conv_b, bias, wa_blk, ba, wx_blk, bx, lam)


_SG_CONV_A, _SG_CONV_B, _SG_BIAS, _SG_BA, _SG_BX, _SG_LAM, _SG_ROWS = 0, 3, 7, 8, 9, 10, 16


def _mixer_bwd(proj, hseq, dy, conv_a, conv_b, bias, wa_blk, ba, wx_blk, bx, lam):
    s, w5 = proj.shape
    w = w5 // 5
    nch = w // LANES
    ts = _pick(s, ROW_TILE)
    nt = s // ts
    tpb = ts // SUBLANES

    def body(p_ref, pp_ref, h_ref, hp_ref, dy_ref, ca_ref, cb_ref, bias_ref, wa_ref, ba_ref, wx_ref, bx_ref,
             lam_ref, dp_ref, xr_ref, dpa_ref, dpx_ref, sg_ref,
             a_scr, g_scr, l_scr, x_scr, r_scr, i_scr, m_scr, cl_scr, cdc_scr, cdx_scr):
        pid = pl.program_id(0)
        last = pid == 0
        first = pid == nt - 1
        rows = lax.broadcasted_iota(jnp.int32, (ts, LANES), 0)

        @pl.when(last)
        def _():
            sg_ref[...] = jnp.zeros_like(sg_ref)
            cl_scr[...] = jnp.zeros_like(cl_scr)
            cdc_scr[...] = jnp.zeros_like(cdc_scr)
            cdx_scr[...] = jnp.zeros_like(cdx_scr)

        def cur(comp, c):
            return p_ref[:, comp * w + c * LANES:comp * w + (c + 1) * LANES]

        def prev(comp, c):
            v = pp_ref[:, comp * w + c * LANES:comp * w + (c + 1) * LANES]
            return jnp.where(first, 0.0, v)

        def put(comp, c, v):
            dp_ref[:, comp * w + c * LANES:comp * w + (c + 1) * LANES] = v.astype(dp_ref.dtype)

        def acc(row, sl, v):
            sg_ref[row:row + 1, sl] += _colsum(v)

        for c in range(nch):
            sl = slice(c * LANES, (c + 1) * LANES)
            bg, cg, ax = cur(0, c), cur(1, c), cur(2, c)
            cx = cg * ax
            cxp = prev(1, c) * prev(2, c)
            cx1 = _shift_down(cx, cxp, 1, rows)
            cx2 = _shift_down(cx, cxp, 2, rows)
            wa3 = ca_ref[:, sl]
            conv = wa3[2:3] * cx + wa3[1:2] * cx1 + wa3[0:1] * cx2
            dya = dy_ref[:, sl]
            put(0, c, dya * conv)
            dconv = dya * bg
            nxt = cdc_scr[:, sl]
            dcx = (wa3[2:3] * dconv + wa3[1:2] * _shift_up(dconv, nxt, 1, rows)
                   + wa3[0:1] * _shift_up(dconv, nxt, 2, rows))
            cdc_scr[:, sl] = dconv[0:SUBLANES]
            put(1, c, dcx * ax)
            put(2, c, dcx * cg)
            acc(_SG_CONV_A + 2, sl, dconv * cx)
            acc(_SG_CONV_A + 1, sl, dconv * cx1)
            acc(_SG_CONV_A + 0, sl, dconv * cx2)

        for c in range(nch):
            sl = slice(c * LANES, (c + 1) * LANES)
            xb, xbp = cur(4, c), prev(4, c)
            wb4 = cb_ref[:, sl]
            xr = (wb4[3:4] * xb + wb4[2:3] * _shift_down(xb, xbp, 1, rows)
                  + wb4[1:2] * _shift_down(xb, xbp, 2, rows)
                  + wb4[0:1] * _shift_down(xb, xbp, 3, rows) + bias_ref[:, sl])
            r, i, a, m = _gates(xr, wa_ref[c], ba_ref[:, sl], wx_ref[c], bx_ref[:, sl], lam_ref[:, sl])
            gel, dgel = _gelu_and_grad(cur(3, c))
            dyb = dy_ref[:, w + c * LANES:w + (c + 1) * LANES]
            put(3, c, dyb * h_ref[:, sl] * dgel)
            g_scr[:, sl] = dyb * gel
            a_scr[:, sl] = a
            x_scr[:, sl] = xr
            r_scr[:, sl] = r
            i_scr[:, sl] = i
            m_scr[:, sl] = m

        def step(j, carry):
            r = ts - 1 - j
            lam_t = g_scr[pl.ds(r, 1), :] + carry
            l_scr[pl.ds(r, 1), :] = lam_t
            return a_scr[pl.ds(r, 1), :] * lam_t

        cl_scr[0:1, :] = lax.fori_loop(0, ts, step, cl_scr[0:1, :], unroll=8)

        for c in range(nch):
            sl = slice(c * LANES, (c + 1) * LANES)
            lam_t = l_scr[:, sl]
            hprev = _shift_down(h_ref[:, sl], jnp.where(first, 0.0, hp_ref[:, sl]), 1, rows)
            xr, r, i, m, a = x_scr[:, sl], r_scr[:, sl], i_scr[:, sl], m_scr[:, sl], a_scr[:, sl]
            da = lam_t * hprev
            dm = lam_t * i * xr
            di = lam_t * m * xr
            dxr = lam_t * m * i
            dlog_a = da * a - dm * a * a / m
            lam_p = lam_ref[:, sl]
            dr = dlog_a * (LRU_C * _log_sigmoid(lam_p))
            acc(_SG_LAM, sl, dlog_a * r * (LRU_C * _sigmoid(-lam_p)))
            dpa = dr * r * (1.0 - r)
            dpx = di * i * (1.0 - i)
            dpa_b, dpx_b = dpa.astype(BF16), dpx.astype(BF16)
            dxr = (dxr + lax.dot_general(dpa_b, wa_ref[c], _DIMS["nt"], preferred_element_type=F32)
                   + lax.dot_general(dpx_b, wx_ref[c], _DIMS["nt"], preferred_element_type=F32))
            xr_ref[:, sl] = xr.astype(BF16)
            dpa_ref[:, sl] = dpa_b
            dpx_ref[:, sl] = dpx_b
            acc(_SG_BA, sl, dpa)
            acc(_SG_BX, sl, dpx)
            acc(_SG_BIAS, sl, dxr)
            nxt = cdx_scr[:, sl]
            wb4 = cb_ref[:, sl]
            put(4, c, wb4[3:4] * dxr + wb4[2:3] * _shift_up(dxr, nxt, 1, rows)
                + wb4[1:2] * _shift_up(dxr, nxt, 2, rows) + wb4[0:1] * _shift_up(dxr, nxt, 3, rows))
            cdx_scr[:, sl] = dxr[0:SUBLANES]
            xb, xbp = cur(4, c), prev(4, c)
            acc(_SG_CONV_B + 3, sl, dxr * xb)
            acc(_SG_CONV_B + 2, sl, dxr * _shift_down(xb, xbp, 1, rows))
            acc(_SG_CONV_B + 1, sl, dxr * _shift_down(xb, xbp, 2, rows))
            acc(_SG_CONV_B + 0, sl, dxr * _shift_down(xb, xbp, 3, rows))

    blk = lambda width: pl.BlockSpec((ts, width), lambda p: (nt - 1 - p, 0))
    pre = lambda width: pl.BlockSpec(
        (SUBLANES, width), lambda p: (jnp.maximum((nt - 1 - p) * tpb - 1, 0), 0))
    vec = lambda n: _whole((n, w))
    big = lambda: pltpu.VMEM((ts, w), F32)
    small = lambda: pltpu.VMEM((SUBLANES, w), F32)
    return pl.pallas_call(
        body, name="mixer_bwd", grid=(nt,),
        in_specs=[blk(w5), pre(w5), blk(w), pre(w), blk(2 * w),
                  vec(3), vec(4), vec(1), _whole(wa_blk.shape), vec(1), _whole(wx_blk.shape), vec(1), vec(1)],
        out_specs=[blk(w5), blk(w), blk(w), blk(w), _whole((_SG_ROWS, w))],
        out_shape=[jax.ShapeDtypeStruct((s, w5), BF16), jax.ShapeDtypeStruct((s, w), BF16),
                   jax.ShapeDtypeStruct((s, w), BF16), jax.ShapeDtypeStruct((s, w), BF16),
                   jax.ShapeDtypeStruct((_SG_ROWS, w), F32)],
        scratch_shapes=[big(), big(), big(), big(), big(), big(), big(), small(), small(), small()],
        compiler_params=_cp(("arbitrary",)),
    )(proj, proj, hseq, hseq, dy, conv_a, conv_b, bias, wa_blk, ba, wx_blk, bx, lam)


def _split_dot(v, tri2):
    hi = v.astype(BF16)
    lo = (v - hi.astype(F32)).astype(BF16)
    return jnp.dot(jnp.concatenate([hi, lo], axis=1), tri2, preferred_element_type=F32)


def _tri(cmp):
    r = lax.broadcasted_iota(jnp.int32, (LANES, LANES), 0)
    c = lax.broadcasted_iota(jnp.int32, (LANES, LANES), 1)
    return cmp(r, c).astype(BF16)


def _lane_blocks(v):
    return [v[:, b * LANES:(b + 1) * LANES] for b in range(v.shape[1] // LANES)]


def _last_lane(v):
    return jnp.broadcast_to(v[:, LANES - 1:LANES], v.shape)


def _scores(q, kb, scale):
    return lax.dot_general(q, kb, _DIMS["nt"], preferred_element_type=F32) * scale


def _log_gates(z, diagonal):
    ls = jnp.minimum(z, 0.0) - jnp.log(1.0 + jnp.exp(-jnp.abs(z)))
    ln = ls - z
    valid = None
    if diagonal:
        valid = (lax.broadcasted_iota(jnp.int32, z.shape, 1) < lax.broadcasted_iota(jnp.int32, z.shape, 0))
        ln = jnp.where(valid, ln, 0.0)
    return ls, ln, valid


def _attn_fwd(qkv, heads):
    s = qkv.shape[0]
    dh = LANES
    tq = _pick(s, ATT_TILE)
    nq = s // tq
    nb = tq // LANES
    scale = 1.0 / math.sqrt(dh)

    hp = ATT_FWD_HEADS_PER_STEP
    groups = heads // hp
    wid = hp * dh

    def body(q_ref, k_ref, v_ref, o_ref, tot_ref, acc_scr, car_scr):
        qi = pl.program_id(1)
        acc_scr[...] = jnp.zeros_like(acc_scr)
        car_scr[...] = jnp.zeros_like(car_scr)
        tri = _tri(lambda r, c: r > c)
        tri = jnp.concatenate([tri, tri], axis=0)

        def tile(kt, diagonal):
            k0 = pl.multiple_of(kt * tq, tq)
            heads_cols = [slice(hh * dh, (hh + 1) * dh) for hh in range(hp)]
            zs = [_scores(q_ref[:, cols], k_ref[pl.ds(k0, tq), cols], scale) for cols in heads_cols]
            gates = [_log_gates(z, diagonal) for z in zs]
            sfxs = [_split_dot(jnp.concatenate(_lane_blocks(ln), axis=0), tri) for _, ln, _ in gates]
            for cols, (ls, ln, valid), sfx in zip(heads_cols, gates, sfxs):
                blocks = _lane_blocks(ln)
                car = car_scr[:, cols]
                parts = [None] * nb
                for b in reversed(range(nb)):
                    sb = sfx[b * tq:(b + 1) * tq]
                    parts[b] = sb + car
                    car = car + (sb[:, 0:1] + blocks[b][:, 0:1])
                car_scr[:, cols] = car
                wgt = jnp.exp(ls + jnp.concatenate(parts, axis=1))
                if diagonal:
                    wgt = jnp.where(valid, wgt, 0.0)
                acc_scr[:, cols] += jnp.dot(
                    wgt.astype(BF16), v_ref[pl.ds(k0, tq), cols], preferred_element_type=F32)

        tile(qi, True)

        def step(j, carry):
            tile(qi - 1 - j, False)
            return carry

        lax.fori_loop(0, qi, step, 0)
        o_ref[...] = acc_scr[...].astype(BF16)
        tot_ref[...] = car_scr[...]

    return pl.pallas_call(
        body, name="attn_fwd", grid=(groups, nq),
        in_specs=[pl.BlockSpec((tq, wid), lambda h, i: (i, h)),
                  pl.BlockSpec((s, wid), lambda h, i: (0, groups + h)),
                  pl.BlockSpec((s, wid), lambda h, i: (0, 2 * groups + h))],
        out_specs=[pl.BlockSpec((tq, wid), lambda h, i: (i, h)), pl.BlockSpec((tq, wid), lambda h, i: (i, h))],
        out_shape=[jax.ShapeDtypeStruct((s, heads * dh), BF16), jax.ShapeDtypeStruct((s, heads * dh), F32)],
        scratch_shapes=[pltpu.VMEM((tq, wid), F32), pltpu.VMEM((tq, wid), F32)],
        compiler_params=_cp(("parallel", "arbitrary")),
    )(qkv, qkv, qkv)


def _attn_bwd(qkv, tot, do, heads):
    s = qkv.shape[0]
    dh = LANES
    tq = _pick(s, ATT_TILE)
    nq = s // tq
    nb = tq // LANES
    scale = 1.0 / math.sqrt(dh)

    hp = ATT_HEADS_PER_STEP
    groups = heads // hp
    wid = hp * dh

    def body(q_ref, k_ref, v_ref, tot_ref, do_ref, dq_ref, dk_ref, dv_ref,
             dq_scr, dk_scr, dv_scr, cl_scr, cg_scr):
        qi = pl.program_id(1)

        @pl.when(qi == 0)
        def _():
            dk_scr[...] = jnp.zeros_like(dk_scr)
            dv_scr[...] = jnp.zeros_like(dv_scr)

        dq_scr[...] = jnp.zeros_like(dq_scr)
        cl_scr[...] = jnp.zeros_like(cl_scr)
        cg_scr[...] = jnp.zeros_like(cg_scr)
        tri_le = _tri(lambda r, c: r <= c)
        tri_le = jnp.concatenate([tri_le, tri_le], axis=0)
        tri_lt = _tri(lambda r, c: r < c)

        def tile(kt, diagonal):
            k0 = pl.multiple_of(kt * tq, tq)
            heads_cols = [slice(hh * dh, (hh + 1) * dh) for hh in range(hp)]
            keys = pl.ds(k0, tq)
            zs = [_scores(q_ref[:, cols], k_ref[keys, cols], scale) for cols in heads_cols]
            dws = [lax.dot_general(do_ref[:, cols], v_ref[keys, cols], _DIMS["nt"], preferred_element_type=F32)
                   for cols in heads_cols]
            gates = [_log_gates(z, diagonal) for z in zs]
            pins = [_split_dot(jnp.concatenate(_lane_blocks(ln), axis=0), tri_le) for _, ln, _ in gates]
            wgts, gs = [], []
            for cols, (ls, _, valid), pin, dw in zip(heads_cols, gates, pins, dws):
                total = tot_ref[:, cols]
                cl = cl_scr[:, cols]
                parts = []
                for b in range(nb):
                    pb = pin[b * tq:(b + 1) * tq] + cl
                    parts.append(total - pb)
                    cl = _last_lane(pb)
                cl_scr[:, cols] = cl
                wgt = jnp.exp(ls + jnp.concatenate(parts, axis=1))
                if diagonal:
                    wgt = jnp.where(valid, wgt, 0.0)
                wgts.append(wgt)
                gs.append(wgt * dw)
            pexs = [jnp.dot(jnp.concatenate(_lane_blocks(g), axis=0).astype(BF16), tri_lt,
                            preferred_element_type=F32) for g in gs]
            for cols, wgt in zip(heads_cols, wgts):
                dv_scr[keys, cols] += lax.dot_general(
                    wgt.astype(BF16), do_ref[:, cols], _DIMS["tn"], preferred_element_type=F32)
            for cols, (ls, _, valid), g, pex in zip(heads_cols, gates, gs, pexs):
                gblocks = _lane_blocks(g)
                cg = cg_scr[:, cols]
                parts = []
                for b in range(nb):
                    pb = pex[b * tq:(b + 1) * tq] + cg
                    parts.append(pb)
                    cg = _last_lane(pb + gblocks[b])
                cg_scr[:, cols] = cg
                dz = g - jnp.exp(ls) * (g + jnp.concatenate(parts, axis=1))
                if diagonal:
                    dz = jnp.where(valid, dz, 0.0)
                dz = dz.astype(BF16)
                dq_scr[:, cols] += jnp.dot(dz, k_ref[keys, cols], preferred_element_type=F32)
                dk_scr[keys, cols] += lax.dot_general(
                    dz, q_ref[:, cols], _DIMS["tn"], preferred_element_type=F32)

        def step(j, carry):
            tile(j, False)
            return carry

        lax.fori_loop(0, qi, step, 0)
        tile(qi, True)
        dq_ref[...] = (dq_scr[...] * scale).astype(BF16)

        @pl.when(qi == nq - 1)
        def _():
            dk_ref[...] = (dk_scr[...] * scale).astype(BF16)
            dv_ref[...] = dv_scr[...].astype(BF16)

    qblk = pl.BlockSpec((tq, wid), lambda h, i: (i, h))
    hblk = pl.BlockSpec((s, wid), lambda h, i: (0, h))
    out = jax.ShapeDtypeStruct((s, heads * dh), BF16)
    return pl.pallas_call(
        body, name="attn_bwd", grid=(groups, nq),
        in_specs=[qblk, pl.BlockSpec((s, wid), lambda h, i: (0, groups + h)),
                  pl.BlockSpec((s, wid), lambda h, i: (0, 2 * groups + h)), qblk, qblk],
        out_specs=[qblk, hblk, hblk], out_shape=[out, out, out],
        scratch_shapes=[pltpu.VMEM((tq, wid), F32), pltpu.VMEM((s, wid), F32), pltpu.VMEM((s, wid), F32),
                        pltpu.VMEM((tq, wid), F32), pltpu.VMEM((tq, wid), F32)],
        compiler_params=_cp(("parallel", "arbitrary")),
    )(qkv, qkv, qkv, tot, do)


def _place():
    x, y, c = lax.axis_index("x"), lax.axis_index("y"), lax.axis_index("c")
    chips = [(1 - x, y), (x, 1 - y), (1 - x, 1 - y)]
    return x, y, c, chips


def _remote(src, dst, send_sem, recv_sem, dev):
    return pltpu.make_async_remote_copy(
        src_ref=src, dst_ref=dst, send_sem=send_sem, recv_sem=recv_sem, device_id=dev, device_id_type=MESH)


def _handshake(peers):
    barrier = pltpu.get_barrier_semaphore()
    for dev in peers:
        pl.semaphore_signal(barrier, inc=1, device_id=dev, device_id_type=MESH)
    pl.semaphore_wait(barrier, len(peers))


def _sequencer_kernel(name, n_sems, collective_id):
    return functools.partial(
        pl.kernel, mesh=plsc.ScalarSubcoreMesh(axis_name="seq", num_cores=1), name=name,
        scratch_types=(pltpu.SemaphoreType.DMA,) * n_sems,
        compiler_params=pltpu.CompilerParams(collective_id=collective_id))


def _allgather_async(name, slot_buf, collective_id):
    buf = jax.new_ref(slot_buf, memory_space=pltpu.MemorySpace.HBM)
    hr = slot_buf.shape[1] // 2

    @_sequencer_kernel(name, 12, collective_id)
    def launch(*sems):
        send_sems, recv_sems, fsend_sems, frecv_sems = sems[0:3], sems[3:6], sems[6:9], sems[9:12]
        x, y, c, chips = _place()
        me = 2 * x + y
        sibling = (x, y, 1 - c)
        _handshake([(px, py, c) for px, py in chips] + [sibling])
        mine = buf.at[me, pl.ds(c * hr, hr)]
        firsts = []
        for k, (px, py) in enumerate(chips):
            cp = _remote(mine, mine, send_sems[k], recv_sems[k], (px, py, c))
            cp.start()
            firsts.append(cp)
        passed = []
        for k, (px, py) in enumerate(chips):
            slot = buf.at[2 * px + py, pl.ds(c * hr, hr)]
            _remote(slot, slot, send_sems[k], recv_sems[k], (px, py, c)).wait_recv()
            cp = _remote(slot, slot, fsend_sems[k], frecv_sems[k], sibling)
            cp.start()
            passed.append(cp)
        for k, (px, py) in enumerate(chips):
            slot = buf.at[2 * px + py, pl.ds((1 - c) * hr, hr)]
            _remote(slot, slot, fsend_sems[k], frecv_sems[k], sibling).wait_recv()
        for cp in firsts + passed:
            cp.wait_send()

    launch()
    return buf[...]


def _to_sibling_async(name, slab):
    src = jax.new_ref(slab, memory_space=pltpu.MemorySpace.HBM)
    hr = slab.shape[1] // 2
    got = jax.empty_ref(jax.ShapeDtypeStruct((N_CHIPS, hr, slab.shape[2]), slab.dtype),
                        memory_space=pltpu.MemorySpace.HBM)

    @_sequencer_kernel(name, 2, COLLECTIVE_SIBLING)
    def launch(send_sem, recv_sem):
        x, y, c, _ = _place()
        _handshake([(x, y, 1 - c)])
        _remote(src.at[:, pl.ds((1 - c) * hr, hr), :], got, send_sem, recv_sem, (x, y, 1 - c)).start()
        _remote(got, got, send_sem, recv_sem, (x, y, 1 - c)).wait()

    launch()
    return src[...], got[...]


def _swap_with_sibling_async(name, part):
    src = jax.new_ref(part, memory_space=pltpu.MemorySpace.HBM)
    got = jax.empty_ref(jax.ShapeDtypeStruct(part.shape, part.dtype), memory_space=pltpu.MemorySpace.HBM)

    @_sequencer_kernel(name, 2, COLLECTIVE_SIBLING)
    def launch(send_sem, recv_sem):
        x, y, c, _ = _place()
        _handshake([(x, y, 1 - c)])
        cp = _remote(src, got, send_sem, recv_sem, (x, y, 1 - c))
        cp.start()
        cp.wait()

    launch()
    return got[...]


def _to_chips_async(name, part):
    src = jax.new_ref(part, memory_space=pltpu.MemorySpace.HBM)
    got = jax.empty_ref(jax.ShapeDtypeStruct((3,) + part.shape[1:], part.dtype), memory_space=pltpu.MemorySpace.HBM)

    @_sequencer_kernel(name, 6, COLLECTIVE_CHIPS)
    def launch(*sems):
        send_sems, recv_sems = sems[0:3], sems[3:6]
        x, y, c, chips = _place()
        _handshake([(px, py, c) for px, py in chips])
        cps = []
        for k, (px, py) in enumerate(chips):
            cp = _remote(src.at[2 * px + py], got.at[k], send_sems[k], recv_sems[k], (px, py, c))
            cp.start()
            cps.append(cp)
        for cp in cps:
            cp.wait()

    launch()
    return src[...], got[...]


def _join_sibling_async(name, half_filled):
    buf = jax.new_ref(half_filled, memory_space=pltpu.MemorySpace.HBM)
    hr = half_filled.shape[0] // 2

    @_sequencer_kernel(name, 2, COLLECTIVE_SIBLING)
    def launch(send_sem, recv_sem):
        x, y, c, _ = _place()
        _handshake([(x, y, 1 - c)])
        mine = buf.at[pl.ds(c * hr, hr)]
        other = buf.at[pl.ds((1 - c) * hr, hr)]
        cp = _remote(mine, mine, send_sem, recv_sem, (x, y, 1 - c))
        cp.start()
        _remote(other, other, send_sem, recv_sem, (x, y, 1 - c)).wait_recv()
        cp.wait_send()

    launch()
    return buf[...]


def _allgather_chips_small(name, v):
    r = v.shape[0]

    def body(v_ref, o_ref, send_sems, recv_sems):
        x, y, c, chips = _place()
        me = 2 * x + y
        o_ref[me] = v_ref[...]
        cps = []
        for k, (px, py) in enumerate(chips):
            cp = _remote(v_ref, o_ref.at[me], send_sems.at[k], recv_sems.at[k], (px, py, c))
            cp.start()
            cps.append(cp)
        for k, (px, py) in enumerate(chips):
            slot = o_ref.at[2 * px + py]
            _remote(slot, slot, send_sems.at[k], recv_sems.at[k], (px, py, c)).wait_recv()
        for cp in cps:
            cp.wait_send()

    return pl.pallas_call(
        body, name=name, in_specs=[pl.BlockSpec(memory_space=pltpu.VMEM)],
        out_specs=pl.BlockSpec(memory_space=pltpu.VMEM),
        out_shape=jax.ShapeDtypeStruct((N_CHIPS, r, LANES), F32),
        scratch_shapes=[pltpu.SemaphoreType.DMA((3,)), pltpu.SemaphoreType.DMA((3,))],
    )(v)


def _allreduce_small(name, v):
    r = v.shape[0]
    hr = r // 2
    assert hr % SUBLANES == 0

    def body(v_ref, o_ref, sib_ref, chips_ref, send_sems, recv_sems):
        x, y, c, chips = _place()
        me = 2 * x + y
        sibling = (x, y, 1 - c)
        first = _remote(v_ref, sib_ref, send_sems.at[0], recv_sems.at[0], sibling)
        first.start()
        first.wait()
        mine = pl.ds(pl.multiple_of(c * hr, SUBLANES), hr)
        chips_ref[me] = v_ref[mine, :] + sib_ref[mine, :]
        cps = []
        for k, (px, py) in enumerate(chips):
            cp = _remote(chips_ref.at[me], chips_ref.at[me], send_sems.at[1 + k], recv_sems.at[1 + k], (px, py, c))
            cp.start()
            cps.append(cp)
        for k, (px, py) in enumerate(chips):
            slot = chips_ref.at[2 * px + py]
            _remote(slot, slot, send_sems.at[1 + k], recv_sems.at[1 + k], (px, py, c)).wait_recv()
        total = chips_ref[0]
        for j in range(1, N_CHIPS):
            total = total + chips_ref[j]
        o_ref[mine, :] = total
        last = _remote(o_ref.at[mine], o_ref.at[mine], send_sems.at[4], recv_sems.at[4], sibling)
        last.start()
        other = o_ref.at[pl.ds(pl.multiple_of((1 - c) * hr, SUBLANES), hr)]
        _remote(other, other, send_sems.at[4], recv_sems.at[4], sibling).wait_recv()
        last.wait_send()
        for cp in cps:
            cp.wait_send()

    return pl.pallas_call(
        body, name=name, in_specs=[pl.BlockSpec(memory_space=pltpu.VMEM)],
        out_specs=pl.BlockSpec(memory_space=pltpu.VMEM),
        out_shape=jax.ShapeDtypeStruct((r, LANES), F32),
        scratch_shapes=[pltpu.VMEM((r, LANES), F32), pltpu.VMEM((N_CHIPS, hr, LANES), F32),
                        pltpu.SemaphoreType.DMA((5,)), pltpu.SemaphoreType.DMA((5,))],
    )(v)


def _add_sibling(name, slabs, recv, c):
    _, r, cols = slabs.shape
    hr = r // 2
    tr = _pick(hr, STREAM_TILE)
    nb = hr // tr

    def body(c_ref, a_ref, b_ref, o_ref):
        o_ref[...] = (a_ref[...].astype(F32) + b_ref[...].astype(F32)).astype(BF16)

    grid_spec = pltpu.PrefetchScalarGridSpec(
        num_scalar_prefetch=1, grid=(N_CHIPS, nb),
        in_specs=[pl.BlockSpec((None, tr, cols), lambda j, i, c_ref: (j, c_ref[0] * nb + i, 0)),
                  pl.BlockSpec((None, tr, cols), lambda j, i, c_ref: (j, i, 0))],
        out_specs=pl.BlockSpec((None, tr, cols), lambda j, i, c_ref: (j, i, 0)))
    return pl.pallas_call(
        body, name=name, grid_spec=grid_spec,
        out_shape=jax.ShapeDtypeStruct((N_CHIPS, hr, cols), BF16),
        compiler_params=_cp(("parallel", "parallel")))(jnp.reshape(c, (1,)).astype(jnp.int32), slabs, recv)


def _sum_chips(name, own, recv, chip, c):
    _, hr, cols = recv.shape
    tr = _pick(hr, STREAM_TILE // 2)
    nb = hr // tr

    def body(sc_ref, own_ref, recv_ref, o_ref):
        total = own_ref[...].astype(F32)
        for k in range(3):
            total = total + recv_ref[k].astype(F32)
        o_ref[...] = total

    grid_spec = pltpu.PrefetchScalarGridSpec(
        num_scalar_prefetch=1, grid=(nb,),
        in_specs=[pl.BlockSpec((None, tr, cols), lambda i, sc: (sc[0], i, 0)),
                  pl.BlockSpec((3, tr, cols), lambda i, sc: (0, i, 0))],
        out_specs=pl.BlockSpec((tr, cols), lambda i, sc: (sc[1] * nb + i, 0)))
    return pl.pallas_call(
        body, name=name, grid_spec=grid_spec, out_shape=jax.ShapeDtypeStruct((2 * hr, cols), F32),
        compiler_params=_cp(("parallel",)))(jnp.stack([chip, c]).astype(jnp.int32), own, recv)


def _adamw_math(w, g, m, v):
    m = ADAM_B1 * m + (1.0 - ADAM_B1) * g
    v = ADAM_B2 * v + (1.0 - ADAM_B2) * (g * g)
    m_hat = m / (1.0 - ADAM_B1 ** ADAM_STEP)
    v_hat = v / (1.0 - ADAM_B2 ** ADAM_STEP)
    delta = -ADAM_LR * (m_hat / (jnp.sqrt(v_hat) + ADAM_EPS) + ADAM_WD * w)
    return delta, m, v


def _adamw(name, w, gs, m, v):
    nl, r, cols = w.shape
    tr = _pick(r, ROW_TILE)

    def body(*refs):
        w_ref, m_ref, v_ref = refs[0:3]
        g_refs = refs[3:3 + nl]
        go_ref, d_ref, nm_ref, nv_ref = refs[3 + nl:]
        layer = pl.program_id(0)
        g = g_refs[0][...]
        for j in range(1, nl):
            g = jnp.where(layer == j, g_refs[j][...], g)
        d, nm, nv = _adamw_math(w_ref[...], g, m_ref[...], v_ref[...])
        go_ref[...] = g
        d_ref[...] = d
        nm_ref[...] = nm
        nv_ref[...] = nv

    spec3 = pl.BlockSpec((None, tr, cols), lambda l, i: (l, i, 0))
    gspec = pl.BlockSpec((tr, cols), lambda l, i: (i, 0))
    out = jax.ShapeDtypeStruct((nl, r, cols), F32)
    return pl.pallas_call(
        body, name=name, grid=(nl, r // tr), in_specs=[spec3] * 3 + [gspec] * nl, out_specs=[spec3] * 4,
        out_shape=[out] * 4, compiler_params=_cp(("parallel", "parallel")))(w, m, v, *gs)


def _adamw_small(name, groups):
    n = len(groups)
    flat = [a for grp in groups for a in grp]

    def body(*refs):
        ins, outs = refs[:4 * n], refs[4 * n:]
        for p in range(n):
            w_ref, g_ref, m_ref, v_ref = ins[4 * p:4 * p + 4]
            d, nm, nv = _adamw_math(w_ref[...], g_ref[...], m_ref[...], v_ref[...])
            outs[3 * p][...] = d
            outs[3 * p + 1][...] = nm
            outs[3 * p + 2][...] = nv

    vm = pl.BlockSpec(memory_space=pltpu.VMEM)
    out_shape = [jax.ShapeDtypeStruct(grp[0].shape, F32) for grp in groups for _ in range(3)]
    res = pl.pallas_call(
        body, name=name, in_specs=[vm] * (4 * n), out_specs=[vm] * (3 * n), out_shape=out_shape)(*flat)
    return [tuple(res[3 * p:3 * p + 3]) for p in range(n)]


def _block_diag_pairs(w):
    h, d, _ = w.shape
    z = jnp.zeros((h // 2, d, d), w.dtype)
    top = jnp.concatenate([w[0::2], z], axis=2)
    bot = jnp.concatenate([z, w[1::2]], axis=2)
    return jnp.concatenate([top, bot], axis=1).astype(BF16)


def _diag_pairs_to_heads(g, d):
    a = g[:, :d, :d]
    b = g[:, d:, d:]
    return jnp.stack([a, b], axis=1).reshape(-1, d, d)


def _rows128(a):
    flat = a.reshape(-1, LANES)
    pad = (-flat.shape[0]) % SUBLANES
    if pad:
        flat = jnp.concatenate([flat, jnp.zeros((pad, LANES), flat.dtype)], axis=0)
    return flat


def _unshard_last(g4, shape):
    g4 = g4.reshape((N_CHIPS,) + tuple(shape))
    return jnp.concatenate([g4[j] for j in range(N_CHIPS)], axis=-1)


def kernel(x, norm_gains, hyb_w_in, hyb_conv_a, hyb_conv_b, hyb_conv_b_bias, hyb_rg_w_a, hyb_rg_b_a, hyb_rg_w_x, hyb_rg_b_x, hyb_rg_lambda, hyb_w_out, sb_w_qkv, sb_w_o, mlp_w_up, mlp_w_down, loss_target, m_norm_gains, m_hyb_w_in, m_hyb_conv_a, m_hyb_conv_b, m_hyb_conv_b_bias, m_hyb_rg_w_a, m_hyb_rg_b_a, m_hyb_rg_w_x, m_hyb_rg_b_x, m_hyb_rg_lambda, m_hyb_w_out, m_sb_w_qkv, m_sb_w_o, m_mlp_w_up, m_mlp_w_down, v_norm_gains, v_hyb_w_in, v_hyb_conv_a, v_hyb_conv_b, v_hyb_conv_b_bias, v_hyb_rg_w_a, v_hyb_rg_b_a, v_hyb_rg_w_x, v_hyb_rg_b_x, v_hyb_rg_lambda, v_hyb_w_out, v_sb_w_qkv, v_sb_w_o, v_mlp_w_up, v_mlp_w_down):
    cx_ = lax.axis_index("x")
    cy_ = lax.axis_index("y")
    cc_ = lax.axis_index("c")
    chip = 2 * cx_ + cy_

    x0 = x[0]
    target = loss_target[0]
    s, d = x0.shape
    heads = SB_HEADS
    assert d // heads == LANES
    n_rg, hd = hyb_rg_w_a.shape[1], hyb_rg_w_a.shape[2]
    wmix = n_rg * hd
    assert 2 * hd == LANES

    big = {
        "hyb_w_in": (hyb_w_in, 0), "hyb_w_out": (hyb_w_out, 0), "mlp_w_up0": (mlp_w_up, 0),
        "mlp_w_down0": (mlp_w_down, 0), "sb_w_qkv": (sb_w_qkv, 0), "sb_w_o": (sb_w_o, 0),
        "mlp_w_up1": (mlp_w_up, 1), "mlp_w_down1": (mlp_w_down, 1),
    }
    names = list(big)
    slots = [_cast_into_slot("cast_" + k, big[k][0], big[k][1], chip) for k in names]
    full = {k: _allgather_async("allgather_" + k, slot, cid) for cid, (k, slot) in enumerate(zip(names, slots))}
    rowsharded = lambda k: full[k].reshape(-1, full[k].shape[2])

    ng_s, ca_s, cb_s = norm_gains.reshape(-1, norm_gains.shape[2]), hyb_conv_a[0], hyb_conv_b[0]
    packed = jnp.concatenate([_rows128(ng_s), _rows128(ca_s), _rows128(cb_s)], axis=0)
    gathered = _allgather_chips_small("allgather_small", packed)
    n0 = ng_s.size // LANES
    n1 = n0 + (-n0) % SUBLANES
    m0 = ca_s.size // LANES
    m1 = m0 + (-m0) % SUBLANES
    k0 = cb_s.size // LANES
    gains = _unshard_last(gathered[:, 0:n0], ng_s.shape).reshape(2, 4, 1, d)
    conv_a = _unshard_last(gathered[:, n1:n1 + m0], ca_s.shape)
    conv_b = _unshard_last(gathered[:, n1 + m1:n1 + m1 + k0], cb_s.shape)
    bias, b_a, b_x, lam = hyb_conv_b_bias, hyb_rg_b_a, hyb_rg_b_x, hyb_rg_lambda
    wa_blk = _block_diag_pairs(hyb_rg_w_a[0])
    wx_blk = _block_diag_pairs(hyb_rg_w_x[0])

    relu_sq = lambda acc: (jnp.maximum(acc, 0.0), jnp.square(jnp.maximum(acc, 0.0)))

    h1 = _rms_fwd("rms_pre0", x0, gains[0, 0])
    proj = _mm_fwd_col("proj_in", h1, full["hyb_w_in"])[0]
    ycat, hseq = _mixer_fwd(proj, conv_a, conv_b, bias, wa_blk, b_a, wx_blk, b_x, lam)
    mix0 = _mm_fwd_row("proj_out", ycat, rowsharded("hyb_w_out"))
    x1, h2 = _rms_post("rms_mix0", mix0, gains[0, 1], x0, gains[0, 2])
    u0, a0 = _mm_fwd_col("mlp_up0", h2, full["mlp_w_up0"], (BF16, BF16), relu_sq)
    mlp0 = _mm_fwd_row("mlp_down0", a0, rowsharded("mlp_w_down0"))
    x2, h3 = _rms_post("rms_mlp0", mlp0, gains[0, 3], x1, gains[1, 0])

    qkv = _mm_fwd_col("qkv", h3, full["sb_w_qkv"], (BF16,))[0]
    att, tot = _attn_fwd(qkv, heads)
    mix1 = _mm_fwd_row("attn_out", att, rowsharded("sb_w_o"))
    x3, h4 = _rms_post("rms_mix1", mix1, gains[1, 1], x2, gains[1, 2])
    u1, a1 = _mm_fwd_col("mlp_up1", h4, full["mlp_w_up1"], (BF16, BF16), relu_sq)
    mlp1 = _mm_fwd_row("mlp_down1", a1, rowsharded("mlp_w_down1"))
    dy, dmlp1, dgain_mlp1, loss_local = _last_norm_and_loss("last_norm_loss", mlp1, gains[1, 3], x3, target)
    loss = lax.psum(loss_local, ("x", "y", "c"))

    dgain = [[None] * 4 for _ in range(2)]
    drelu = lambda acc, u: (acc * (2.0 * u.astype(F32)),)
    stage_a, stage_b, gfull = {}, {}, {}

    def tie(main, side):
        return lax.optimization_barrier((main, side))

    def reduce_start(k, slab, main):
        main, slab = tie(main, slab)
        stage_a[k] = _to_sibling_async("grads_to_sibling_" + k, slab)
        return main

    def reduce_to_chips(k, main):
        slab, from_sibling = stage_a.pop(k)
        main, part = tie(main, _add_sibling("grads_add_" + k, slab, from_sibling, cc_))
        stage_b[k] = _to_chips_async("grads_to_chips_" + k, part)
        return main

    def reduce_split_start(k, act, dy, cs, main):
        main, other = tie(main, _mm_wgrad_half(k + "_wgrad_sibling_rows", act, dy, 1 - cc_, cs))
        stage_a[k] = (act, dy, cs, _swap_with_sibling_async("grads_to_sibling_" + k, other))
        return main

    def reduce_split_to_chips(k, main):
        act, dy, cs, from_sibling = stage_a.pop(k)
        main, part = tie(main, _mm_wgrad_half(k + "_wgrad_my_rows", act, dy, cc_, cs, init=from_sibling))
        stage_b[k] = _to_chips_async("grads_to_chips_" + k, part)
        return main

    def after(value, token):
        return tie(value, token)[0]

    def reduce_finish(k, main):
        own, from_chips = stage_b.pop(k)
        main, half = tie(main, _sum_chips("grads_sum_" + k, after(own, main), from_chips, chip, cc_))
        gfull[k] = _join_sibling_async("grads_join_" + k, half)
        return main

    def mlp_bwd(layer, dxo, dmlp, xin, hin, u, a, mix):
        down, up = f"mlp_w_down{layer}", f"mlp_w_up{layer}"
        wd, wu = rowsharded(down), full[up]
        dmlp = reduce_split_start(down, a, dmlp, None, dmlp)
        du = _mm_bwd_row(f"mlp_down{layer}_bwd", dmlp, wd, (BF16,), u, drelu)[0]
        du = reduce_split_start(up, hin, du, wu.shape[2], du)
        du = reduce_split_to_chips(down, du)
        dh = _mm_bwd_col(f"mlp_up{layer}_bwd", du, wu)
        dh = reduce_split_to_chips(up, dh)
        dxm, dgain[layer][2], dmix, dgain[layer][1] = _rms_bwd_pair(
            f"rms_premlp{layer}_mix{layer}_bwd", xin, gains[layer, 2], dh, dxo, mix, gains[layer, 1])
        return dxm, dmix

    dgain[1][3] = dgain_mlp1
    dx3, dmix1 = mlp_bwd(1, dy, dmlp1, x3, h4, u1, a1, mix1)
    dmix1 = reduce_start("sb_w_o", _mm_wgrad_row("attn_out_wgrad", att, dmix1).reshape(N_CHIPS, -1, d), dmix1)
    datt = _mm_bwd_row("attn_out_bwd", dmix1, rowsharded("sb_w_o"), (BF16,))[0]
    dq, dk, dv = _attn_bwd(qkv, tot, datt, heads)
    dqkv = jnp.concatenate([dq, dk, dv], axis=1)
    dqkv = reduce_to_chips("sb_w_o", dqkv)
    dqkv = reduce_finish("mlp_w_down1", dqkv)
    dqkv = reduce_finish("mlp_w_up1", dqkv)
    dqkv = reduce_split_start("sb_w_qkv", h3, dqkv, full["sb_w_qkv"].shape[2], dqkv)
    dh3 = _mm_bwd_col("qkv_bwd", dqkv, full["sb_w_qkv"])
    dh3 = reduce_split_to_chips("sb_w_qkv", dh3)
    dx2, dgain[1][0], dmlp0, dgain[0][3] = _rms_bwd_pair(
        "rms_pre1_mlp0_bwd", x2, gains[1, 0], dh3, dx3, mlp0, gains[0, 3])

    dx1, dmix0 = mlp_bwd(0, dx2, dmlp0, x1, h2, u0, a0, mix0)
    dmix0 = reduce_finish("sb_w_o", dmix0)
    dmix0 = reduce_finish("sb_w_qkv", dmix0)
    dmix0 = reduce_finish("mlp_w_down0", dmix0)
    dmix0 = reduce_start("hyb_w_out", _mm_wgrad_row("proj_out_wgrad", ycat, dmix0).reshape(N_CHIPS, -1, d), dmix0)
    dycat = _mm_bwd_row("proj_out_bwd", dmix0, rowsharded("hyb_w_out"))[0]
    dproj, xr_b, dpa_b, dpx_b, sg = _mixer_bwd(
        proj, hseq, dycat, conv_a, conv_b, bias, wa_blk, b_a, wx_blk, b_x, lam)
    dproj = reduce_finish("mlp_w_up0", dproj)
    dproj = reduce_to_chips("hyb_w_out", dproj)
    dproj = reduce_split_start("hyb_w_in", h1, dproj, full["hyb_w_in"].shape[2], dproj)
    dh1 = _mm_bwd_col("proj_in_bwd", dproj, full["hyb_w_in"])
    dh1 = reduce_split_to_chips("hyb_w_in", dh1)
    dx0, dgain[0][0] = _rms_bwd("rms_pre0_bwd", x0, gains[0, 0], dh1, res=dx1)
    dwa = _diag_pairs_to_heads(_mm_wgrad_diag("rg_w_a_wgrad", xr_b, dpa_b), hd)
    dwx = _diag_pairs_to_heads(_mm_wgrad_diag("rg_w_x_wgrad", xr_b, dpx_b), hd)

    dgains = jnp.concatenate([dgain[l][k] for l in range(2) for k in range(4)], axis=0)
    small_parts = [dgains, sg[_SG_CONV_A:_SG_CONV_A + 3], sg[_SG_CONV_B:_SG_CONV_B + 4], sg[_SG_BIAS:_SG_BIAS + 1],
                   dwa, sg[_SG_BA:_SG_BA + 1], dwx, sg[_SG_BX:_SG_BX + 1], sg[_SG_LAM:_SG_LAM + 1]]
    small_rows = [_rows128(p) for p in small_parts]
    n_small = sum(rws.shape[0] for rws in small_rows)
    tail_pad = [jnp.zeros(((-n_small) % (2 * SUBLANES), LANES), F32)] if n_small % (2 * SUBLANES) else []
    reduced = _allreduce_small("allreduce_small", jnp.concatenate(small_rows + tail_pad, axis=0))
    small_full, off = [], 0
    for p, rws in zip(small_parts, small_rows):
        small_full.append(reduced[off:off + p.size // LANES].reshape(p.shape))
        off += rws.shape[0]
    g_gains, g_ca, g_cb, g_bias, g_wa, g_ba, g_wx, g_bx, g_lam = small_full

    def my_cols(g, width):
        return lax.dynamic_slice_in_dim(g, chip * width, width, axis=g.ndim - 1)

    small = [
        ("norm_gains", norm_gains, my_cols(g_gains, norm_gains.shape[2]).reshape(norm_gains.shape),
         m_norm_gains, v_norm_gains),
        ("hyb_conv_a", hyb_conv_a, my_cols(g_ca, hyb_conv_a.shape[2])[None], m_hyb_conv_a, v_hyb_conv_a),
        ("hyb_conv_b", hyb_conv_b, my_cols(g_cb, hyb_conv_b.shape[2])[None], m_hyb_conv_b, v_hyb_conv_b),
        ("hyb_conv_b_bias", hyb_conv_b_bias, g_bias, m_hyb_conv_b_bias, v_hyb_conv_b_bias),
        ("hyb_rg_w_a", hyb_rg_w_a, g_wa[None], m_hyb_rg_w_a, v_hyb_rg_w_a),
        ("hyb_rg_b_a", hyb_rg_b_a, g_ba, m_hyb_rg_b_a, v_hyb_rg_b_a),
        ("hyb_rg_w_x", hyb_rg_w_x, g_wx[None], m_hyb_rg_w_x, v_hyb_rg_w_x),
        ("hyb_rg_b_x", hyb_rg_b_x, g_bx, m_hyb_rg_b_x, v_hyb_rg_b_x),
        ("hyb_rg_lambda", hyb_rg_lambda, g_lam, m_hyb_rg_lambda, v_hyb_rg_lambda),
    ]
    to2d = lambda a: a.reshape(-1, a.shape[-1])
    small_res = _adamw_small("adamw_small", [tuple(to2d(a) for a in (w, g, m, v)) for _, w, g, m, v in small])
    out = {}
    for (nm, w, g, _, _), (dl, nmom, nvar) in zip(small, small_res):
        out[nm] = (g, dl.reshape(w.shape), nmom.reshape(w.shape), nvar.reshape(w.shape))

    stacked = {
        "mlp_w_down": (mlp_w_down, m_mlp_w_down, v_mlp_w_down, ["mlp_w_down0", "mlp_w_down1"]),
        "mlp_w_up": (mlp_w_up, m_mlp_w_up, v_mlp_w_up, ["mlp_w_up0", "mlp_w_up1"]),
        "sb_w_o": (sb_w_o, m_sb_w_o, v_sb_w_o, ["sb_w_o"]),
        "sb_w_qkv": (sb_w_qkv, m_sb_w_qkv, v_sb_w_qkv, ["sb_w_qkv"]),
        "hyb_w_out": (hyb_w_out, m_hyb_w_out, v_hyb_w_out, ["hyb_w_out"]),
        "hyb_w_in": (hyb_w_in, m_hyb_w_in, v_hyb_w_in, ["hyb_w_in"]),
    }

    def update(k, token):
        w, m, v, parts = stacked[k]
        out[k] = tuple(_adamw("adamw_" + k, w, [after(gfull[p], token) for p in parts], m, v))
        return out[k][1]

    token = small_res[0][0]
    token = update("sb_w_qkv", token)
    token = update("sb_w_o", token)
    token = update("mlp_w_down", token)
    token = reduce_finish("hyb_w_out", token)
    token = update("mlp_w_up", token)
    token = reduce_finish("hyb_w_in", token)
    token = update("hyb_w_out", token)
    update("hyb_w_in", token)

    order = ["norm_gains", "hyb_w_in", "hyb_conv_a", "hyb_conv_b", "hyb_conv_b_bias", "hyb_rg_w_a", "hyb_rg_b_a",
             "hyb_rg_w_x", "hyb_rg_b_x", "hyb_rg_lambda", "hyb_w_out", "sb_w_qkv", "sb_w_o", "mlp_w_up",
             "mlp_w_down"]
    return (loss, dx0[None], *[out[k][0] for k in order], *[out[k][1] for k in order],
            *[out[k][2] for k in order], *[out[k][3] for k in order])
```

```python
import functools
import math

import jax
import jax.numpy as jnp
from jax import lax
from jax.experimental import pallas as pl
from jax.experimental.pallas import tpu as pltpu
from jax.experimental.pallas import tpu_sc as plsc

F32 = jnp.float32
BF16 = jnp.bfloat16
MESH = pl.DeviceIdType.MESH

SB_HEADS = 16
NORM_EPS = 1e-6
LRU_C = 8.0
ADAM_LR = 0.001
ADAM_B1 = 0.9
ADAM_B2 = 0.999
ADAM_EPS = 1e-08
ADAM_WD = 0.01
ADAM_STEP = 10

LANES = 128
SUBLANES = 8
VMEM_LIMIT = 48 * 1024 * 1024
MM_TILE = 1024
MM_VMEM_BUDGET = 40 * 1024 * 1024
MM_TILE_N = 1280
MM_TILE_K = 2048
ROW_TILE = 256
STREAM_TILE = 1024
ATT_TILE = 512
ATT_HEADS_PER_STEP = 2
ATT_FWD_HEADS_PER_STEP = 4
N_CHIPS = 4
COLLECTIVE_SIBLING = 8
COLLECTIVE_CHIPS = 9

_DIMS = {
    "nn": (((1,), (0,)), ((), ())),
    "nt": (((1,), (1,)), ((), ())),
    "tn": (((0,), (0,)), ((), ())),
}


def _cp(sem=None, vmem=VMEM_LIMIT):
    return pltpu.CompilerParams(dimension_semantics=sem, vmem_limit_bytes=vmem)


def _pick(dim, pref):
    t = min(dim, pref)
    while dim % t:
        t -= LANES
    return t


def _whole(shape):
    nd = len(shape)
    return pl.BlockSpec(tuple(shape), lambda *_: (0,) * nd)


def _sigmoid(z):
    return 1.0 / (1.0 + jnp.exp(-z))


def _log_sigmoid(z):
    return jnp.minimum(z, 0.0) - jnp.log(1.0 + jnp.exp(-jnp.abs(z)))


def _expm1(z):
    series = z * (1.0 + z * (0.5 + z * (1.0 / 6.0 + z * (1.0 / 24.0))))
    return jnp.where(jnp.abs(z) < 0.05, series, jnp.exp(z) - 1.0)


_GELU_C = math.sqrt(2.0 / math.pi)


def _gelu_and_grad(g):
    inner = _GELU_C * (g + 0.044715 * g * g * g)
    t = jnp.tanh(inner)
    val = 0.5 * g * (1.0 + t)
    grad = 0.5 * (1.0 + t) + 0.5 * g * (1.0 - t * t) * _GELU_C * (1.0 + 3.0 * 0.044715 * g * g)
    return val, grad


def _shift_down(cur, prev8, k, rows):
    n = cur.shape[0]
    rolled = pltpu.roll(cur, k, 0)
    head = jnp.tile(pltpu.roll(prev8, k, 0), (n // SUBLANES, 1))
    return jnp.where(rows < k, head, rolled)


def _shift_up(cur, next8, k, rows):
    n = cur.shape[0]
    rolled = pltpu.roll(cur, n - k, 0)
    tail = jnp.tile(pltpu.roll(next8, SUBLANES - k, 0), (n // SUBLANES, 1))
    return jnp.where(rows >= n - k, tail, rolled)


def _colsum(v):
    return jnp.sum(v, axis=0, keepdims=True)


def _matmul(name, mode, grid, operands, in_specs, out_shapes, out_specs, acc_shape, epilogue=None):
    nk = grid[2]
    n_in = len(operands)
    dims = _DIMS[mode]

    def finish(acc, extra, outs):
        res = epilogue(acc, *[e[...] for e in extra]) if epilogue is not None else (acc,)
        for o_ref, o in zip(outs, res):
            o_ref[...] = o.astype(o_ref.dtype)

    def product(a_ref, b_ref):
        return lax.dot_general(a_ref[...].astype(BF16), b_ref[...].astype(BF16), dims, preferred_element_type=F32)

    def body_single(*refs):
        finish(product(refs[0], refs[1]), refs[2:n_in], refs[n_in:])

    def body(*refs):
        extra = refs[2:n_in]
        outs = refs[n_in:-1]
        acc_ref = refs[-1]
        k = pl.program_id(2)

        @pl.when(k == 0)
        def _():
            acc_ref[...] = product(refs[0], refs[1])

        @pl.when(k > 0)
        def _():
            acc_ref[...] += product(refs[0], refs[1])

        @pl.when(k == nk - 1)
        def _():
            finish(acc_ref[...], extra, outs)

    return pl.pallas_call(
        body_single if nk == 1 else body, name=name, grid=grid, in_specs=in_specs, out_specs=out_specs,
        out_shape=out_shapes, scratch_shapes=[] if nk == 1 else [pltpu.VMEM(acc_shape, F32)],
        compiler_params=_cp(("parallel", "parallel", "arbitrary")),
    )(*operands)


def _pick_m(m, tk, tn, a_dtype, b_dtype, out_dtypes, extra_dtypes=()):
    size = lambda dt: jnp.dtype(dt).itemsize
    per_row = 2 * tk * size(a_dtype) + tn * (2 * sum(size(dt) for dt in tuple(out_dtypes) + tuple(extra_dtypes)) + 4)
    fixed = 2 * tk * tn * size(b_dtype)
    tm = _pick(m, MM_TILE)
    while tm > LANES and tm * per_row + fixed > MM_VMEM_BUDGET:
        tm = _pick(m, tm // 2)
    return tm


def _mm_fwd_col(name, a, wfull, out_dtypes=(F32,), epilogue=None):
    s, kdim = a.shape
    _, _, cs = wfull.shape
    tk, tn = _pick(kdim, MM_TILE_K), _pick(cs, MM_TILE_N)
    tm = _pick_m(s, tk, tn, a.dtype, wfull.dtype, out_dtypes)
    nbj = cs // tn
    grid = (s // tm, N_CHIPS * nbj, kdim // tk)
    out_shapes = [jax.ShapeDtypeStruct((s, N_CHIPS * cs), dt) for dt in out_dtypes]
    out_specs = [pl.BlockSpec((tm, tn), lambda i, n, k: (i, n)) for _ in out_dtypes]
    return _matmul(
        name, "nn", grid, [a, wfull],
        [pl.BlockSpec((tm, tk), lambda i, n, k: (i, k)),
         pl.BlockSpec((None, tk, tn), lambda i, n, k: (n // nbj, k, n % nbj))],
        out_shapes, out_specs, (tm, tn), epilogue)


def _mm_fwd_row(name, a, w2d, out_dtype=BF16):
    s, kdim = a.shape
    _, n_out = w2d.shape
    tk, tn = _pick(kdim, MM_TILE_K), _pick(n_out, MM_TILE)
    tm = _pick_m(s, tk, tn, a.dtype, w2d.dtype, (out_dtype,))
    grid = (s // tm, n_out // tn, kdim // tk)
    return _matmul(
        name, "nn", grid, [a, w2d],
        [pl.BlockSpec((tm, tk), lambda i, n, k: (i, k)),
         pl.BlockSpec((tk, tn), lambda i, n, k: (k, n))],
        [jax.ShapeDtypeStruct((s, n_out), out_dtype)],
        [pl.BlockSpec((tm, tn), lambda i, n, k: (i, n))], (tm, tn))[0]


def _mm_bwd_col(name, dy, wfull, out_dtype=BF16):
    s, _ = dy.shape
    _, kdim, cs = wfull.shape
    tn, tk = _pick(kdim, MM_TILE), _pick(cs, MM_TILE_K)
    tm = _pick_m(s, tk, tn, dy.dtype, wfull.dtype, (out_dtype,))
    nbj = cs // tk
    grid = (s // tm, kdim // tn, N_CHIPS * nbj)
    return _matmul(
        name, "nt", grid, [dy, wfull],
        [pl.BlockSpec((tm, tk), lambda i, n, k: (i, k)),
         pl.BlockSpec((None, tn, tk), lambda i, n, k: (k // nbj, n, k % nbj))],
        [jax.ShapeDtypeStruct((s, kdim), out_dtype)],
        [pl.BlockSpec((tm, tn), lambda i, n, k: (i, n))], (tm, tn))[0]


def _mm_bwd_row(name, dy, w2d, out_dtypes=(F32,), extra=None, epilogue=None):
    s, n_in = dy.shape
    kdim, _ = w2d.shape
    tn, tk = _pick(kdim, MM_TILE), _pick(n_in, MM_TILE_K)
    tm = _pick_m(s, tk, tn, dy.dtype, w2d.dtype, out_dtypes, () if extra is None else (extra.dtype,))
    grid = (s // tm, kdim // tn, n_in // tk)
    operands = [dy, w2d]
    in_specs = [pl.BlockSpec((tm, tk), lambda i, n, k: (i, k)),
                pl.BlockSpec((tn, tk), lambda i, n, k: (n, k))]
    if extra is not None:
        operands.append(extra)
        in_specs.append(pl.BlockSpec((tm, tn), lambda i, n, k: (i, n)))
    return _matmul(
        name, "nt", grid, operands, in_specs,
        [jax.ShapeDtypeStruct((s, kdim), dt) for dt in out_dtypes],
        [pl.BlockSpec((tm, tn), lambda i, n, k: (i, n)) for _ in out_dtypes], (tm, tn), epilogue)


def _mm_wgrad_row(name, a, dy):
    s, kdim = a.shape
    _, n_out = dy.shape
    tn, ts = _pick(n_out, MM_TILE), _pick(s, MM_TILE_K)
    tm = _pick_m(kdim, ts, tn, a.dtype, dy.dtype, (BF16,))
    grid = (kdim // tm, n_out // tn, s // ts)
    return _matmul(
        name, "tn", grid, [a, dy],
        [pl.BlockSpec((ts, tm), lambda i, n, k: (k, i)),
         pl.BlockSpec((ts, tn), lambda i, n, k: (k, n))],
        [jax.ShapeDtypeStruct((kdim, n_out), BF16)],
        [pl.BlockSpec((tm, tn), lambda i, n, k: (i, n))], (tm, tn))[0]


def _mm_wgrad_half(name, a, dy, half, cs=None, init=None):
    s, kdim = a.shape
    ts = _pick(s, MM_TILE_K)
    nk = s // ts
    if cs is not None:
        hr, cols = kdim // 2, cs
        tn = _pick(cs, MM_TILE_N)
        tm = _pick_m(hr, ts, tn, a.dtype, dy.dtype, (BF16,), (BF16,))
        ni, nbj = hr // tm, cs // tn
        grid = (ni, N_CHIPS * nbj, nk)
        a_map = lambda i, n, k, h: (k, h[0] * ni + i)
        o_map = lambda i, n, k, h: (n // nbj, i, n % nbj)
    else:
        hr, cols = kdim // N_CHIPS // 2, dy.shape[1]
        tn = _pick(cols, MM_TILE)
        tm = _pick_m(hr, ts, tn, a.dtype, dy.dtype, (BF16,), (BF16,))
        ni = hr // tm
        grid = (N_CHIPS * ni, cols // tn, nk)
        a_map = lambda i, n, k, h: (k, (i // ni) * 2 * ni + h[0] * ni + i % ni)
        o_map = lambda i, n, k, h: (i // ni, i % ni, n)
    with_init = init is not None

    def body(*refs):
        a_ref, b_ref = refs[1], refs[2]
        init_ref = refs[3] if with_init else None
        o_ref, acc_ref = refs[-2], refs[-1]
        k = pl.program_id(2)

        def product():
            return lax.dot_general(a_ref[...].astype(BF16), b_ref[...].astype(BF16), _DIMS["tn"],
                                   preferred_element_type=F32)

        @pl.when(k == 0)
        def _():
            if with_init:
                acc_ref[...] = init_ref[...].astype(F32)
                acc_ref[...] += product()
            else:
                acc_ref[...] = product()

        @pl.when(k > 0)
        def _():
            acc_ref[...] += product()

        @pl.when(k == nk - 1)
        def _():
            o_ref[...] = acc_ref[...].astype(BF16)

    oblk = pl.BlockSpec((None, tm, tn), o_map)
    grid_spec = pltpu.PrefetchScalarGridSpec(
        num_scalar_prefetch=1, grid=grid,
        in_specs=[pl.BlockSpec((ts, tm), a_map), pl.BlockSpec((ts, tn), lambda i, n, k, h: (k, n))]
        + ([oblk] if with_init else []),
        out_specs=oblk, scratch_shapes=[pltpu.VMEM((tm, tn), F32)])
    operands = [jnp.reshape(half, (1,)).astype(jnp.int32), a, dy] + ([init] if with_init else [])
    return pl.pallas_call(
        body, name=name, grid_spec=grid_spec, out_shape=jax.ShapeDtypeStruct((N_CHIPS, hr, cols), BF16),
        compiler_params=_cp(("parallel", "parallel", "arbitrary")))(*operands)


def _mm_wgrad_diag(name, a, dy):
    s, width = a.shape
    nb = width // LANES
    ts = _pick(s, MM_TILE)
    grid = (nb, 1, s // ts)
    return _matmul(
        name, "tn", grid, [a, dy],
        [pl.BlockSpec((ts, LANES), lambda i, n, k: (k, i)),
         pl.BlockSpec((ts, LANES), lambda i, n, k: (k, i))],
        [jax.ShapeDtypeStruct((nb, LANES, LANES), F32)],
        [pl.BlockSpec((None, LANES, LANES), lambda i, n, k: (i, 0, 0))], (LANES, LANES))[0]


def _rowspec(tr, d):
    return pl.BlockSpec((tr, d), lambda i: (i, 0))


def _vecspec(d):
    return pl.BlockSpec((1, d), lambda i: (0, 0))


def _rms(x, g):
    return x * lax.rsqrt(jnp.mean(x * x, axis=-1, keepdims=True) + NORM_EPS) * g


def _cast_into_slot(name, w, layer, chip):
    _, r, c = w.shape
    tr = _pick(r, STREAM_TILE)

    def body(chip_ref, w_ref, o_ref):
        o_ref[...] = w_ref[...].astype(BF16)

    grid_spec = pltpu.PrefetchScalarGridSpec(
        num_scalar_prefetch=1, grid=(r // tr,),
        in_specs=[pl.BlockSpec((None, tr, c), lambda i, chip_ref: (layer, i, 0))],
        out_specs=pl.BlockSpec((None, tr, c), lambda i, chip_ref: (chip_ref[0], i, 0)))
    return pl.pallas_call(
        body, name=name, grid_spec=grid_spec, out_shape=jax.ShapeDtypeStruct((N_CHIPS, r, c), BF16),
        compiler_params=_cp(("parallel",)))(jnp.reshape(chip, (1,)).astype(jnp.int32), w)


def _rms_fwd(name, x, g):
    s, d = x.shape
    tr = _pick(s, ROW_TILE)

    def body(x_ref, g_ref, h_ref):
        h_ref[...] = _rms(x_ref[...], g_ref[...]).astype(BF16)

    return pl.pallas_call(
        body, name=name, grid=(s // tr,), in_specs=[_rowspec(tr, d), _vecspec(d)],
        out_specs=_rowspec(tr, d), out_shape=jax.ShapeDtypeStruct((s, d), BF16),
        compiler_params=_cp(("parallel",)))(x, g)


def _rms_post(name, y, g_post, res, g_next=None):
    s, d = y.shape
    tr = _pick(s, ROW_TILE)
    with_next = g_next is not None

    def body(*refs):
        if with_next:
            y_ref, gp_ref, r_ref, gn_ref, x_ref, h_ref = refs
        else:
            y_ref, gp_ref, r_ref, x_ref = refs
        xn = r_ref[...] + _rms(y_ref[...].astype(F32), gp_ref[...])
        x_ref[...] = xn
        if with_next:
            h_ref[...] = _rms(xn, gn_ref[...]).astype(BF16)

    operands = [y, g_post, res] + ([g_next] if with_next else [])
    in_specs = [_rowspec(tr, d), _vecspec(d), _rowspec(tr, d)] + ([_vecspec(d)] if with_next else [])
    out_shape = [jax.ShapeDtypeStruct((s, d), F32)] + ([jax.ShapeDtypeStruct((s, d), BF16)] if with_next else [])
    out_specs = [_rowspec(tr, d)] + ([_rowspec(tr, d)] if with_next else [])
    return pl.pallas_call(
        body, name=name, grid=(s // tr,), in_specs=in_specs, out_specs=out_specs, out_shape=out_shape,
        compiler_params=_cp(("parallel",)))(*operands)


def _rms_bwd(name, x, g, dy, res=None, out_dtype=F32):
    s, d = x.shape
    tr = _pick(s, ROW_TILE)
    nsteps = s // tr
    with_res = res is not None

    def body(*refs):
        if with_res:
            x_ref, g_ref, dy_ref, r_ref, dx_ref, dg_ref, acc_ref = refs
        else:
            x_ref, g_ref, dy_ref, dx_ref, dg_ref, acc_ref = refs
        i = pl.program_id(0)

        @pl.when(i == 0)
        def _():
            acc_ref[...] = jnp.zeros_like(acc_ref)

        xv = x_ref[...]
        dyv = dy_ref[...].astype(F32)
        r = lax.rsqrt(jnp.mean(xv * xv, axis=-1, keepdims=True) + NORM_EPS)
        xhat = xv * r
        gy = dyv * g_ref[...]
        dx = r * (gy - xhat * jnp.mean(gy * xhat, axis=-1, keepdims=True))
        if with_res:
            dx = dx + r_ref[...]
        dx_ref[...] = dx.astype(dx_ref.dtype)
        acc_ref[...] += jnp.sum((dyv * xhat).reshape(tr // SUBLANES, SUBLANES, d), axis=0)

        @pl.when(i == nsteps - 1)
        def _():
            dg_ref[...] = jnp.broadcast_to(_colsum(acc_ref[...]), (SUBLANES, d))

    operands = [x, g, dy] + ([res] if with_res else [])
    in_specs = [_rowspec(tr, d), _vecspec(d), _rowspec(tr, d)] + ([_rowspec(tr, d)] if with_res else [])
    dx, dg = pl.pallas_call(
        body, name=name, grid=(nsteps,), in_specs=in_specs,
        out_specs=[_rowspec(tr, d), pl.BlockSpec((SUBLANES, d), lambda i: (0, 0))],
        out_shape=[jax.ShapeDtypeStruct((s, d), out_dtype), jax.ShapeDtypeStruct((SUBLANES, d), F32)],
        scratch_shapes=[pltpu.VMEM((SUBLANES, d), F32)],
        compiler_params=_cp(("arbitrary",)))(*operands)
    return dx, dg[0:1]


def _rms_bwd_pair(name, x, g, dy, res, y2, g2):
    s, d = x.shape
    tr = _pick(s, ROW_TILE)
    nsteps = s // tr

    def through(xv, gv, dyv):
        r = lax.rsqrt(jnp.mean(xv * xv, axis=-1, keepdims=True) + NORM_EPS)
        xhat = xv * r
        gy = dyv * gv
        dx = r * (gy - xhat * jnp.mean(gy * xhat, axis=-1, keepdims=True))
        return dx, jnp.sum((dyv * xhat).reshape(tr // SUBLANES, SUBLANES, d), axis=0)

    def body(x_ref, g_ref, dy_ref, r_ref, y2_ref, g2_ref, dx_ref, d2_ref, dg_ref, dg2_ref, acc_ref, acc2_ref):
        i = pl.program_id(0)

        @pl.when(i == 0)
        def _():
            acc_ref[...] = jnp.zeros_like(acc_ref)
            acc2_ref[...] = jnp.zeros_like(acc2_ref)

        dx, part = through(x_ref[...], g_ref[...], dy_ref[...].astype(F32))
        dx = dx + r_ref[...]
        dx_ref[...] = dx
        acc_ref[...] += part
        d2, part2 = through(y2_ref[...].astype(F32), g2_ref[...], dx)
        d2_ref[...] = d2.astype(d2_ref.dtype)
        acc2_ref[...] += part2

        @pl.when(i == nsteps - 1)
        def _():
            dg_ref[...] = jnp.broadcast_to(_colsum(acc_ref[...]), (SUBLANES, d))
            dg2_ref[...] = jnp.broadcast_to(_colsum(acc2_ref[...]), (SUBLANES, d))

    row, vec = _rowspec(tr, d), _vecspec(d)
    gspec = pl.BlockSpec((SUBLANES, d), lambda i: (0, 0))
    dx, d2, dg, dg2 = pl.pallas_call(
        body, name=name, grid=(nsteps,), in_specs=[row, vec, row, row, row, vec],
        out_specs=[row, row, gspec, gspec],
        out_shape=[jax.ShapeDtypeStruct((s, d), F32), jax.ShapeDtypeStruct((s, d), BF16),
                   jax.ShapeDtypeStruct((SUBLANES, d), F32), jax.ShapeDtypeStruct((SUBLANES, d), F32)],
        scratch_shapes=[pltpu.VMEM((SUBLANES, d), F32), pltpu.VMEM((SUBLANES, d), F32)],
        compiler_params=_cp(("arbitrary",)))(x, g, dy, res, y2, g2)
    return dx, dg[0:1], d2, dg2[0:1]


def _last_norm_and_loss(name, y, g, res, target):
    s, d = y.shape
    tr = _pick(s, ROW_TILE)
    nsteps = s // tr

    def body(y_ref, g_ref, r_ref, t_ref, dx_ref, dy_ref, dg_ref, l_ref, acc_ref, lacc_ref):
        i = pl.program_id(0)

        @pl.when(i == 0)
        def _():
            acc_ref[...] = jnp.zeros_like(acc_ref)
            lacc_ref[...] = jnp.zeros_like(lacc_ref)

        yv = y_ref[...].astype(F32)
        gv = g_ref[...]
        r = lax.rsqrt(jnp.mean(yv * yv, axis=-1, keepdims=True) + NORM_EPS)
        yhat = yv * r
        err = r_ref[...] + yhat * gv - t_ref[...]
        dx = err * (1.0 / d)
        dx_ref[...] = dx
        lacc_ref[...] += jnp.sum((err * err).reshape(tr // SUBLANES, SUBLANES, d), axis=0)
        gy = dx * gv
        dy_ref[...] = (r * (gy - yhat * jnp.mean(gy * yhat, axis=-1, keepdims=True))).astype(dy_ref.dtype)
        acc_ref[...] += jnp.sum((dx * yhat).reshape(tr // SUBLANES, SUBLANES, d), axis=0)

        @pl.when(i == nsteps - 1)
        def _():
            dg_ref[...] = jnp.broadcast_to(_colsum(acc_ref[...]), (SUBLANES, d))
            l_ref[...] = jnp.full((SUBLANES, LANES), (0.5 / d) * jnp.sum(lacc_ref[...]), F32)

    dx, dy, dg, l = pl.pallas_call(
        body, name=name, grid=(nsteps,),
        in_specs=[_rowspec(tr, d), _vecspec(d), _rowspec(tr, d), _rowspec(tr, d)],
        out_specs=[_rowspec(tr, d), _rowspec(tr, d), pl.BlockSpec((SUBLANES, d), lambda i: (0, 0)),
                   pl.BlockSpec((SUBLANES, LANES), lambda i: (0, 0))],
        out_shape=[jax.ShapeDtypeStruct((s, d), F32), jax.ShapeDtypeStruct((s, d), BF16),
                   jax.ShapeDtypeStruct((SUBLANES, d), F32), jax.ShapeDtypeStruct((SUBLANES, LANES), F32)],
        scratch_shapes=[pltpu.VMEM((SUBLANES, d), F32), pltpu.VMEM((SUBLANES, d), F32)],
        compiler_params=_cp(("arbitrary",)))(y, g, res, target)
    return dx, dy, dg[0:1], l[0, 0]


def _gates(xr, wa, ba, wx, bx, lam):
    xb = xr.astype(BF16)
    r = _sigmoid(jnp.dot(xb, wa, preferred_element_type=F32) + ba)
    i = _sigmoid(jnp.dot(xb, wx, preferred_element_type=F32) + bx)
    log_a = LRU_C * r * _log_sigmoid(lam)
    a = jnp.exp(log_a)
    m = jnp.sqrt(-_expm1(2.0 * log_a))
    return r, i, a, m


def _mixer_fwd(proj, conv_a, conv_b, bias, wa_blk, ba, wx_blk, bx, lam):
    s, w5 = proj.shape
    w = w5 // 5
    nch = w // LANES
    ts = _pick(s, ROW_TILE)
    nt = s // ts

    def body(p_ref, pp_ref, ca_ref, cb_ref, bias_ref, wa_ref, ba_ref, wx_ref, bx_ref, lam_ref,
             y_ref, h_ref, a_scr, b_scr, hc_scr):
        t = pl.program_id(0)
        first = t == 0
        rows = lax.broadcasted_iota(jnp.int32, (ts, LANES), 0)

        @pl.when(first)
        def _():
            hc_scr[...] = jnp.zeros_like(hc_scr)

        def cur(comp, c):
            return p_ref[:, comp * w + c * LANES:comp * w + (c + 1) * LANES]

        def prev(comp, c):
            v = pp_ref[:, comp * w + c * LANES:comp * w + (c + 1) * LANES]
            return jnp.where(first, 0.0, v)

        for c in range(nch):
            sl = slice(c * LANES, (c + 1) * LANES)
            cx = cur(1, c) * cur(2, c)
            cxp = prev(1, c) * prev(2, c)
            wa3 = ca_ref[:, sl]
            conv = (wa3[2:3] * cx + wa3[1:2] * _shift_down(cx, cxp, 1, rows)
                    + wa3[0:1] * _shift_down(cx, cxp, 2, rows))
            y_ref[:, sl] = (cur(0, c) * conv).astype(BF16)

        for c in range(nch):
            sl = slice(c * LANES, (c + 1) * LANES)
            xb, xbp = cur(4, c), prev(4, c)
            wb4 = cb_ref[:, sl]
            xr = (wb4[3:4] * xb + wb4[2:3] * _shift_down(xb, xbp, 1, rows)
                  + wb4[1:2] * _shift_down(xb, xbp, 2, rows)
                  + wb4[0:1] * _shift_down(xb, xbp, 3, rows) + bias_ref[:, sl])
            _, i, a, m = _gates(xr, wa_ref[c], ba_ref[:, sl], wx_ref[c], bx_ref[:, sl], lam_ref[:, sl])
            a_scr[:, sl] = a
            b_scr[:, sl] = m * i * xr

        def step(r, h):
            h = a_scr[pl.ds(r, 1), :] * h + b_scr[pl.ds(r, 1), :]
            h_ref[pl.ds(r, 1), :] = h
            return h

        hc_scr[0:1, :] = lax.fori_loop(0, ts, step, hc_scr[0:1, :], unroll=8)

        for c in range(nch):
            sl = slice(c * LANES, (c + 1) * LANES)
            gel, _ = _gelu_and_grad(cur(3, c))
            y_ref[:, w + c * LANES:w + (c + 1) * LANES] = (h_ref[:, sl] * gel).astype(BF16)

    vec = lambda n: _whole((n, w))
    return pl.pallas_call(
        body, name="mixer_fwd", grid=(nt,),
        in_specs=[pl.BlockSpec((ts, w5), lambda t: (t, 0)),
                  pl.BlockSpec((SUBLANES, w5), lambda t: (jnp.maximum(t * (ts // SUBLANES) - 1, 0), 0)),
                  vec(3), vec(4), vec(1), _whole(wa_blk.shape), vec(1), _whole(wx_blk.shape), vec(1), vec(1)],
        out_specs=[pl.BlockSpec((ts, 2 * w), lambda t: (t, 0)), pl.BlockSpec((ts, w), lambda t: (t, 0))],
        out_shape=[jax.ShapeDtypeStruct((s, 2 * w), BF16), jax.ShapeDtypeStruct((s, w), F32)],
        scratch_shapes=[pltpu.VMEM((ts, w), F32), pltpu.VMEM((ts, w), F32), pltpu.VMEM((SUBLANES, w), F32)],
        compiler_params=_cp(("arbitrary",)),
    )(proj, proj, conv_a, conv_b, bias, wa_blk, ba, wx_blk, bx, lam)


_SG_CONV_A, _SG_CONV_B, _SG_BIAS, _SG_BA, _SG_BX, _SG_LAM, _SG_ROWS = 0, 3, 7, 8, 9, 10, 16


def _mixer_bwd(proj, hseq, dy, conv_a, conv_b, bias, wa_blk, ba, wx_blk, bx, lam):
    s, w5 = proj.shape
    w = w5 // 5
    nch = w // LANES
    ts = _pick(s, ROW_TILE)
    nt = s // ts
    tpb = ts // SUBLANES

    def body(p_ref, pp_ref, h_ref, hp_ref, dy_ref, ca_ref, cb_ref, bias_ref, wa_ref, ba_ref, wx_ref, bx_ref,
             lam_ref, dp_ref, xr_ref, dpa_ref, dpx_ref, sg_ref,
             a_scr, g_scr, l_scr, x_scr, r_scr, i_scr, m_scr, cl_scr, cdc_scr, cdx_scr):
        pid = pl.program_id(0)
        last = pid == 0
        first = pid == nt - 1
        rows = lax.broadcasted_iota(jnp.int32, (ts, LANES), 0)

        @pl.when(last)
        def _():
            sg_ref[...] = jnp.zeros_like(sg_ref)
            cl_scr[...] = jnp.zeros_like(cl_scr)
            cdc_scr[...] = jnp.zeros_like(cdc_scr)
            cdx_scr[...] = jnp.zeros_like(cdx_scr)

        def cur(comp, c):
            return p_ref[:, comp * w + c * LANES:comp * w + (c + 1) * LANES]

        def prev(comp, c):
            v = pp_ref[:, comp * w + c * LANES:comp * w + (c + 1) * LANES]
            return jnp.where(first, 0.0, v)

        def put(comp, c, v):
            dp_ref[:, comp * w + c * LANES:comp * w + (c + 1) * LANES] = v.astype(dp_ref.dtype)

        def acc(row, sl, v):
            sg_ref[row:row + 1, sl] += _colsum(v)

        for c in range(nch):
            sl = slice(c * LANES, (c + 1) * LANES)
            bg, cg, ax = cur(0, c), cur(1, c), cur(2, c)
            cx = cg * ax
            cxp = prev(1, c) * prev(2, c)
            cx1 = _shift_down(cx, cxp, 1, rows)
            cx2 = _shift_down(cx, cxp, 2, rows)
            wa3 = ca_ref[:, sl]
            conv = wa3[2:3] * cx + wa3[1:2] * cx1 + wa3[0:1] * cx2
            dya = dy_ref[:, sl]
            put(0, c, dya * conv)
            dconv = dya * bg
            nxt = cdc_scr[:, sl]
            dcx = (wa3[2:3] * dconv + wa3[1:2] * _shift_up(dconv, nxt, 1, rows)
                   + wa3[0:1] * _shift_up(dconv, nxt, 2, rows))
            cdc_scr[:, sl] = dconv[0:SUBLANES]
            put(1, c, dcx * ax)
            put(2, c, dcx * cg)
            acc(_SG_CONV_A + 2, sl, dconv * cx)
            acc(_SG_CONV_A + 1, sl, dconv * cx1)
            acc(_SG_CONV_A + 0, sl, dconv * cx2)

        for c in range(nch):
            sl = slice(c * LANES, (c + 1) * LANES)
            xb, xbp = cur(4, c), prev(4, c)
            wb4 = cb_ref[:, sl]
            xr = (wb4[3:4] * xb + wb4[2:3] * _shift_down(xb, xbp, 1, rows)
                  + wb4[1:2] * _shift_down(xb, xbp, 2, rows)
                  + wb4[0:1] * _shift_down(xb, xbp, 3, rows) + bias_ref[:, sl])
            r, i, a, m = _gates(xr, wa_ref[c], ba_ref[:, sl], wx_ref[c], bx_ref[:, sl], lam_ref[:, sl])
            gel, dgel = _gelu_and_grad(cur(3, c))
            dyb = dy_ref[:, w + c * LANES:w + (c + 1) * LANES]
            put(3, c, dyb * h_ref[:, sl] * dgel)
            g_scr[:, sl] = dyb * gel
            a_scr[:, sl] = a
            x_scr[:, sl] = xr
            r_scr[:, sl] = r
            i_scr[:, sl] = i
            m_scr[:, sl] = m

        def step(j, carry):
            r = ts - 1 - j
            lam_t = g_scr[pl.ds(r, 1), :] + carry
            l_scr[pl.ds(r, 1), :] = lam_t
            return a_scr[pl.ds(r, 1), :] * lam_t

        cl_scr[0:1, :] = lax.fori_loop(0, ts, step, cl_scr[0:1, :], unroll=8)

        for c in range(nch):
            sl = slice(c * LANES, (c + 1) * LANES)
            lam_t = l_scr[:, sl]
            hprev = _shift_down(h_ref[:, sl], jnp.where(first, 0.0, hp_ref[:, sl]), 1, rows)
            xr, r, i, m, a = x_scr[:, sl], r_scr[:, sl], i_scr[:, sl], m_scr[:, sl], a_scr[:, sl]
            da = lam_t * hprev
            dm = lam_t * i * xr
            di = lam_t * m * xr
            dxr = lam_t * m * i
            dlog_a = da * a - dm * a * a / m
            lam_p = lam_ref[:, sl]
            dr = dlog_a * (LRU_C * _log_sigmoid(lam_p))
            acc(_SG_LAM, sl, dlog_a * r * (LRU_C * _sigmoid(-lam_p)))
            dpa = dr * r * (1.0 - r)
            dpx = di * i * (1.0 - i)
            dpa_b, dpx_b = dpa.astype(BF16), dpx.astype(BF16)
            dxr = (dxr + lax.dot_general(dpa_b, wa_ref[c], _DIMS["nt"], preferred_element_type=F32)
                   + lax.dot_general(dpx_b, wx_ref[c], _DIMS["nt"], preferred_element_type=F32))
            xr_ref[:, sl] = xr.astype(BF16)
            dpa_ref[:, sl] = dpa_b
            dpx_ref[:, sl] = dpx_b
            acc(_SG_BA, sl, dpa)
            acc(_SG_BX, sl, dpx)
            acc(_SG_BIAS, sl, dxr)
            nxt = cdx_scr[:, sl]
            wb4 = cb_ref[:, sl]
            put(4, c, wb4[3:4] * dxr + wb4[2:3] * _shift_up(dxr, nxt, 1, rows)
                + wb4[1:2] * _shift_up(dxr, nxt, 2, rows) + wb4[0:1] * _shift_up(dxr, nxt, 3, rows))
            cdx_scr[:, sl] = dxr[0:SUBLANES]
            xb, xbp = cur(4, c), prev(4, c)
            acc(_SG_CONV_B + 3, sl, dxr * xb)
            acc(_SG_CONV_B + 2, sl, dxr * _shift_down(xb, xbp, 1, rows))
            acc(_SG_CONV_B + 1, sl, dxr * _shift_down(xb, xbp, 2, rows))
            acc(_SG_CONV_B + 0, sl, dxr * _shift_down(xb, xbp, 3, rows))

    blk = lambda width: pl.BlockSpec((ts, width), lambda p: (nt - 1 - p, 0))
    pre = lambda width: pl.BlockSpec(
        (SUBLANES, width), lambda p: (jnp.maximum((nt - 1 - p) * tpb - 1, 0), 0))
    vec = lambda n: _whole((n, w))
    big = lambda: pltpu.VMEM((ts, w), F32)
    small = lambda: pltpu.VMEM((SUBLANES, w), F32)
    return pl.pallas_call(
        body, name="mixer_bwd", grid=(nt,),
        in_specs=[blk(w5), pre(w5), blk(w), pre(w), blk(2 * w),
                  vec(3), vec(4), vec(1), _whole(wa_blk.shape), vec(1), _whole(wx_blk.shape), vec(1), vec(1)],
        out_specs=[blk(w5), blk(w), blk(w), blk(w), _whole((_SG_ROWS, w))],
        out_shape=[jax.ShapeDtypeStruct((s, w5), BF16), jax.ShapeDtypeStruct((s, w), BF16),
                   jax.ShapeDtypeStruct((s, w), BF16), jax.ShapeDtypeStruct((s, w), BF16),
                   jax.ShapeDtypeStruct((_SG_ROWS, w), F32)],
        scratch_shapes=[big(), big(), big(), big(), big(), big(), big(), small(), small(), small()],
        compiler_params=_cp(("arbitrary",)),
    )(proj, proj, hseq, hseq, dy, conv_a, conv_b, bias, wa_blk, ba, wx_blk, bx, lam)


def _split_dot(v, tri2):
    hi = v.astype(BF16)
    lo = (v - hi.astype(F32)).astype(BF16)
    return jnp.dot(jnp.concatenate([hi, lo], axis=1), tri2, preferred_element_type=F32)


def _tri(cmp):
    r = lax.broadcasted_iota(jnp.int32, (LANES, LANES), 0)
    c = lax.broadcasted_iota(jnp.int32, (LANES, LANES), 1)
    return cmp(r, c).astype(BF16)


def _lane_blocks(v):
    return [v[:, b * LANES:(b + 1) * LANES] for b in range(v.shape[1] // LANES)]


def _last_lane(v):
    return jnp.broadcast_to(v[:, LANES - 1:LANES], v.shape)


def _scores(q, kb, scale):
    return lax.dot_general(q, kb, _DIMS["nt"], preferred_element_type=F32) * scale


def _log_gates(z, diagonal):
    ls = jnp.minimum(z, 0.0) - jnp.log(1.0 + jnp.exp(-jnp.abs(z)))
    ln = ls - z
    valid = None
    if diagonal:
        valid = (lax.broadcasted_iota(jnp.int32, z.shape, 1) < lax.broadcasted_iota(jnp.int32, z.shape, 0))
        ln = jnp.where(valid, ln, 0.0)
    return ls, ln, valid


def _attn_fwd(qkv, heads):
    s = qkv.shape[0]
    dh = LANES
    tq = _pick(s, ATT_TILE)
    nq = s // tq
    nb = tq // LANES
    scale = 1.0 / math.sqrt(dh)

    hp = ATT_FWD_HEADS_PER_STEP
    groups = heads // hp
    wid = hp * dh

    def body(q_ref, k_ref, v_ref, o_ref, tot_ref, acc_scr, car_scr):
        qi = pl.program_id(1)
        acc_scr[...] = jnp.zeros_like(acc_scr)
        car_scr[...] = jnp.zeros_like(car_scr)
        tri = _tri(lambda r, c: r > c)
        tri = jnp.concatenate([tri, tri], axis=0)

        def tile(kt, diagonal):
            k0 = pl.multiple_of(kt * tq, tq)
            heads_cols = [slice(hh * dh, (hh + 1) * dh) for hh in range(hp)]
            zs = [_scores(q_ref[:, cols], k_ref[pl.ds(k0, tq), cols], scale) for cols in heads_cols]
            gates = [_log_gates(z, diagonal) for z in zs]
            sfxs = [_split_dot(jnp.concatenate(_lane_blocks(ln), axis=0), tri) for _, ln, _ in gates]
            for cols, (ls, ln, valid), sfx in zip(heads_cols, gates, sfxs):
                blocks = _lane_blocks(ln)
                car = car_scr[:, cols]
                parts = [None] * nb
                for b in reversed(range(nb)):
                    sb = sfx[b * tq:(b + 1) * tq]
                    parts[b] = sb + car
                    car = car + (sb[:, 0:1] + blocks[b][:, 0:1])
                car_scr[:, cols] = car
                wgt = jnp.exp(ls + jnp.concatenate(parts, axis=1))
                if diagonal:
                    wgt = jnp.where(valid, wgt, 0.0)
                acc_scr[:, cols] += jnp.dot(
                    wgt.astype(BF16), v_ref[pl.ds(k0, tq), cols], preferred_element_type=F32)

        tile(qi, True)

        def step(j, carry):
            tile(qi - 1 - j, False)
            return carry

        lax.fori_loop(0, qi, step, 0)
        o_ref[...] = acc_scr[...].astype(BF16)
        tot_ref[...] = car_scr[...]

    return pl.pallas_call(
        body, name="attn_fwd", grid=(groups, nq),
        in_specs=[pl.BlockSpec((tq, wid), lambda h, i: (i, h)),
                  pl.BlockSpec((s, wid), lambda h, i: (0, groups + h)),
                  pl.BlockSpec((s, wid), lambda h, i: (0, 2 * groups + h))],
        out_specs=[pl.BlockSpec((tq, wid), lambda h, i: (i, h)), pl.BlockSpec((tq, wid), lambda h, i: (i, h))],
        out_shape=[jax.ShapeDtypeStruct((s, heads * dh), BF16), jax.ShapeDtypeStruct((s, heads * dh), F32)],
        scratch_shapes=[pltpu.VMEM((tq, wid), F32), pltpu.VMEM((tq, wid), F32)],
        compiler_params=_cp(("parallel", "arbitrary")),
    )(qkv, qkv, qkv)


def _attn_bwd(qkv, tot, do, heads):
    s = qkv.shape[0]
    dh = LANES
    tq = _pick(s, ATT_TILE)
    nq = s // tq
    nb = tq // LANES
    scale = 1.0 / math.sqrt(dh)

    hp = ATT_HEADS_PER_STEP
    groups = heads // hp
    wid = hp * dh

    def body(q_ref, k_ref, v_ref, tot_ref, do_ref, dq_ref, dk_ref, dv_ref,
             dq_scr, dk_scr, dv_scr, cl_scr, cg_scr):
        qi = pl.program_id(1)

        @pl.when(qi == 0)
        def _():
            dk_scr[...] = jnp.zeros_like(dk_scr)
            dv_scr[...] = jnp.zeros_like(dv_scr)

        dq_scr[...] = jnp.zeros_like(dq_scr)
        cl_scr[...] = jnp.zeros_like(cl_scr)
        cg_scr[...] = jnp.zeros_like(cg_scr)
        tri_le = _tri(lambda r, c: r <= c)
        tri_le = jnp.concatenate([tri_le, tri_le], axis=0)
        tri_lt = _tri(lambda r, c: r < c)

        def tile(kt, diagonal):
            k0 = pl.multiple_of(kt * tq, tq)
            heads_cols = [slice(hh * dh, (hh + 1) * dh) for hh in range(hp)]
            keys = pl.ds(k0, tq)
            zs = [_scores(q_ref[:, cols], k_ref[keys, cols], scale) for cols in heads_cols]
            dws = [lax.dot_general(do_ref[:, cols], v_ref[keys, cols], _DIMS["nt"], preferred_element_type=F32)
                   for cols in heads_cols]
            gates = [_log_gates(z, diagonal) for z in zs]
            pins = [_split_dot(jnp.concatenate(_lane_blocks(ln), axis=0), tri_le) for _, ln, _ in gates]
            wgts, gs = [], []
            for cols, (ls, _, valid), pin, dw in zip(heads_cols, gates, pins, dws):
                total = tot_ref[:, cols]
                cl = cl_scr[:, cols]
                parts = []
                for b in range(nb):
                    pb = pin[b * tq:(b + 1) * tq] + cl
                    parts.append(total - pb)
                    cl = _last_lane(pb)
                cl_scr[:, cols] = cl
                wgt = jnp.exp(ls + jnp.concatenate(parts, axis=1))
                if diagonal:
                    wgt = jnp.where(valid, wgt, 0.0)
                wgts.append(wgt)
                gs.append(wgt * dw)
            pexs = [jnp.dot(jnp.concatenate(_lane_blocks(g), axis=0).astype(BF16), tri_lt,
                            preferred_element_type=F32) for g in gs]
            for cols, wgt in zip(heads_cols, wgts):
                dv_scr[keys, cols] += lax.dot_general(
                    wgt.astype(BF16), do_ref[:, cols], _DIMS["tn"], preferred_element_type=F32)
            for cols, (ls, _, valid), g, pex in zip(heads_cols, gates, gs, pexs):
                gblocks = _lane_blocks(g)
                cg = cg_scr[:, cols]
                parts = []
                for b in range(nb):
                    pb = pex[b * tq:(b + 1) * tq] + cg
                    parts.append(pb)
                    cg = _last_lane(pb + gblocks[b])
                cg_scr[:, cols] = cg
                dz = g - jnp.exp(ls) * (g + jnp.concatenate(parts, axis=1))
                if diagonal:
                    dz = jnp.where(valid, dz, 0.0)
                dz = dz.astype(BF16)
                dq_scr[:, cols] += jnp.dot(dz, k_ref[keys, cols], preferred_element_type=F32)
                dk_scr[keys, cols] += lax.dot_general(
                    dz, q_ref[:, cols], _DIMS["tn"], preferred_element_type=F32)

        def step(j, carry):
            tile(j, False)
            return carry

        lax.fori_loop(0, qi, step, 0)
        tile(qi, True)
        dq_ref[...] = (dq_scr[...] * scale).astype(BF16)

        @pl.when(qi == nq - 1)
        def _():
            dk_ref[...] = (dk_scr[...] * scale).astype(BF16)
            dv_ref[...] = dv_scr[...].astype(BF16)

    qblk = pl.BlockSpec((tq, wid), lambda h, i: (i, h))
    hblk = pl.BlockSpec((s, wid), lambda h, i: (0, h))
    out = jax.ShapeDtypeStruct((s, heads * dh), BF16)
    return pl.pallas_call(
        body, name="attn_bwd", grid=(groups, nq),
        in_specs=[qblk, pl.BlockSpec((s, wid), lambda h, i: (0, groups + h)),
                  pl.BlockSpec((s, wid), lambda h, i: (0, 2 * groups + h)), qblk, qblk],
        out_specs=[qblk, hblk, hblk], out_shape=[out, out, out],
        scratch_shapes=[pltpu.VMEM((tq, wid), F32), pltpu.VMEM((s, wid), F32), pltpu.VMEM((s, wid), F32),
                        pltpu.VMEM((tq, wid), F32), pltpu.VMEM((tq, wid), F32)],
        compiler_params=_cp(("parallel", "arbitrary")),
    )(qkv, qkv, qkv, tot, do)


def _place():
    x, y, c = lax.axis_index("x"), lax.axis_index("y"), lax.axis_index("c")
    chips = [(1 - x, y), (x, 1 - y), (1 - x, 1 - y)]
    return x, y, c, chips


def _remote(src, dst, send_sem, recv_sem, dev):
    return pltpu.make_async_remote_copy(
        src_ref=src, dst_ref=dst, send_sem=send_sem, recv_sem=recv_sem, device_id=dev, device_id_type=MESH)


def _handshake(peers):
    barrier = pltpu.get_barrier_semaphore()
    for dev in peers:
        pl.semaphore_signal(barrier, inc=1, device_id=dev, device_id_type=MESH)
    pl.semaphore_wait(barrier, len(peers))


def _sequencer_kernel(name, n_sems, collective_id):
    return functools.partial(
        pl.kernel, mesh=plsc.ScalarSubcoreMesh(axis_name="seq", num_cores=1), name=name,
        scratch_types=(pltpu.SemaphoreType.DMA,) * n_sems,
        compiler_params=pltpu.CompilerParams(collective_id=collective_id))


def _allgather_async(name, slot_buf, collective_id):
    buf = jax.new_ref(slot_buf, memory_space=pltpu.MemorySpace.HBM)
    hr = slot_buf.shape[1] // 2

    @_sequencer_kernel(name, 12, collective_id)
    def launch(*sems):
        send_sems, recv_sems, fsend_sems, frecv_sems = sems[0:3], sems[3:6], sems[6:9], sems[9:12]
        x, y, c, chips = _place()
        me = 2 * x + y
        sibling = (x, y, 1 - c)
        _handshake([(px, py, c) for px, py in chips] + [sibling])
        mine = buf.at[me, pl.ds(c * hr, hr)]
        firsts = []
        for k, (px, py) in enumerate(chips):
            cp = _remote(mine, mine, send_sems[k], recv_sems[k], (px, py, c))
            cp.start()
            firsts.append(cp)
        passed = []
        for k, (px, py) in enumerate(chips):
            slot = buf.at[2 * px + py, pl.ds(c * hr, hr)]
            _remote(slot, slot, send_sems[k], recv_sems[k], (px, py, c)).wait_recv()
            cp = _remote(slot, slot, fsend_sems[k], frecv_sems[k], sibling)
            cp.start()
            passed.append(cp)
        for k, (px, py) in enumerate(chips):
            slot = buf.at[2 * px + py, pl.ds((1 - c) * hr, hr)]
            _remote(slot, slot, fsend_sems[k], frecv_sems[k], sibling).wait_recv()
        for cp in firsts + passed:
            cp.wait_send()

    launch()
    return buf[...]


def _to_sibling_async(name, slab):
    src = jax.new_ref(slab, memory_space=pltpu.MemorySpace.HBM)
    hr = slab.shape[1] // 2
    got = jax.empty_ref(jax.ShapeDtypeStruct((N_CHIPS, hr, slab.shape[2]), slab.dtype),
                        memory_space=pltpu.MemorySpace.HBM)

    @_sequencer_kernel(name, 2, COLLECTIVE_SIBLING)
    def launch(send_sem, recv_sem):
        x, y, c, _ = _place()
        _handshake([(x, y, 1 - c)])
        _remote(src.at[:, pl.ds((1 - c) * hr, hr), :], got, send_sem, recv_sem, (x, y, 1 - c)).start()
        _remote(got, got, send_sem, recv_sem, (x, y, 1 - c)).wait()

    launch()
    return src[...], got[...]


def _swap_with_sibling_async(name, part):
    src = jax.new_ref(part, memory_space=pltpu.MemorySpace.HBM)
    got = jax.empty_ref(jax.ShapeDtypeStruct(part.shape, part.dtype), memory_space=pltpu.MemorySpace.HBM)

    @_sequencer_kernel(name, 2, COLLECTIVE_SIBLING)
    def launch(send_sem, recv_sem):
        x, y, c, _ = _place()
        _handshake([(x, y, 1 - c)])
        cp = _remote(src, got, send_sem, recv_sem, (x, y, 1 - c))
        cp.start()
        cp.wait()

    launch()
    return got[...]


def _to_chips_async(name, part):
    src = jax.new_ref(part, memory_space=pltpu.MemorySpace.HBM)
    got = jax.empty_ref(jax.ShapeDtypeStruct((3,) + part.shape[1:], part.dtype), memory_space=pltpu.MemorySpace.HBM)

    @_sequencer_kernel(name, 6, COLLECTIVE_CHIPS)
    def launch(*sems):
        send_sems, recv_sems = sems[0:3], sems[3:6]
        x, y, c, chips = _place()
        _handshake([(px, py, c) for px, py in chips])
        cps = []
        for k, (px, py) in enumerate(chips):
            cp = _remote(src.at[2 * px + py], got.at[k], send_sems[k], recv_sems[k], (px, py, c))
            cp.start()
            cps.append(cp)
        for cp in cps:
            cp.wait()

    launch()
    return src[...], got[...]


def _join_sibling_async(name, half_filled):
    buf = jax.new_ref(half_filled, memory_space=pltpu.MemorySpace.HBM)
    hr = half_filled.shape[0] // 2

    @_sequencer_kernel(name, 2, COLLECTIVE_SIBLING)
    def launch(send_sem, recv_sem):
        x, y, c, _ = _place()
        _handshake([(x, y, 1 - c)])
        mine = buf.at[pl.ds(c * hr, hr)]
        other = buf.at[pl.ds((1 - c) * hr, hr)]
        cp = _remote(mine, mine, send_sem, recv_sem, (x, y, 1 - c))
        cp.start()
        _remote(other, other, send_sem, recv_sem, (x, y, 1 - c)).wait_recv()
        cp.wait_send()

    launch()
    return buf[...]


def _allgather_chips_small(name, v):
    r = v.shape[0]

    def body(v_ref, o_ref, send_sems, recv_sems):
        x, y, c, chips = _place()
        me = 2 * x + y
        o_ref[me] = v_ref[...]
        cps = []
        for k, (px, py) in enumerate(chips):
            cp = _remote(v_ref, o_ref.at[me], send_sems.at[k], recv_sems.at[k], (px, py, c))
            cp.start()
            cps.append(cp)
        for k, (px, py) in enumerate(chips):
            slot = o_ref.at[2 * px + py]
            _remote(slot, slot, send_sems.at[k], recv_sems.at[k], (px, py, c)).wait_recv()
        for cp in cps:
            cp.wait_send()

    return pl.pallas_call(
        body, name=name, in_specs=[pl.BlockSpec(memory_space=pltpu.VMEM)],
        out_specs=pl.BlockSpec(memory_space=pltpu.VMEM),
        out_shape=jax.ShapeDtypeStruct((N_CHIPS, r, LANES), F32),
        scratch_shapes=[pltpu.SemaphoreType.DMA((3,)), pltpu.SemaphoreType.DMA((3,))],
    )(v)


def _allreduce_small(name, v):
    r = v.shape[0]
    hr = r // 2
    assert hr % SUBLANES == 0

    def body(v_ref, o_ref, sib_ref, chips_ref, send_sems, recv_sems):
        x, y, c, chips = _place()
        me = 2 * x + y
        sibling = (x, y, 1 - c)
        first = _remote(v_ref, sib_ref, send_sems.at[0], recv_sems.at[0], sibling)
        first.start()
        first.wait()
        mine = pl.ds(pl.multiple_of(c * hr, SUBLANES), hr)
        chips_ref[me] = v_ref[mine, :] + sib_ref[mine, :]
        cps = []
        for k, (px, py) in enumerate(chips):
            cp = _remote(chips_ref.at[me], chips_ref.at[me], send_sems.at[1 + k], recv_sems.at[1 + k], (px, py, c))
            cp.start()
            cps.append(cp)
        for k, (px, py) in enumerate(chips):
            slot = chips_ref.at[2 * px + py]
            _remote(slot, slot, send_sems.at[1 + k], recv_sems.at[1 + k], (px, py, c)).wait_recv()
        total = chips_ref[0]
        for j in range(1, N_CHIPS):
            total = total + chips_ref[j]
        o_ref[mine, :] = total
        last = _remote(o_ref.at[mine], o_ref.at[mine], send_sems.at[4], recv_sems.at[4], sibling)
        last.start()
        other = o_ref.at[pl.ds(pl.multiple_of((1 - c) * hr, SUBLANES), hr)]
        _remote(other, other, send_sems.at[4], recv_sems.at[4], sibling).wait_recv()
        last.wait_send()
        for cp in cps:
            cp.wait_send()

    return pl.pallas_call(
        body, name=name, in_specs=[pl.BlockSpec(memory_space=pltpu.VMEM)],
        out_specs=pl.BlockSpec(memory_space=pltpu.VMEM),
        out_shape=jax.ShapeDtypeStruct((r, LANES), F32),
        scratch_shapes=[pltpu.VMEM((r, LANES), F32), pltpu.VMEM((N_CHIPS, hr, LANES), F32),
                        pltpu.SemaphoreType.DMA((5,)), pltpu.SemaphoreType.DMA((5,))],
    )(v)


def _add_sibling(name, slabs, recv, c):
    _, r, cols = slabs.shape
    hr = r // 2
    tr = _pick(hr, STREAM_TILE)
    nb = hr // tr

    def body(c_ref, a_ref, b_ref, o_ref):
        o_ref[...] = (a_ref[...].astype(F32) + b_ref[...].astype(F32)).astype(BF16)

    grid_spec = pltpu.PrefetchScalarGridSpec(
        num_scalar_prefetch=1, grid=(N_CHIPS, nb),
        in_specs=[pl.BlockSpec((None, tr, cols), lambda j, i, c_ref: (j, c_ref[0] * nb + i, 0)),
                  pl.BlockSpec((None, tr, cols), lambda j, i, c_ref: (j, i, 0))],
        out_specs=pl.BlockSpec((None, tr, cols), lambda j, i, c_ref: (j, i, 0)))
    return pl.pallas_call(
        body, name=name, grid_spec=grid_spec,
        out_shape=jax.ShapeDtypeStruct((N_CHIPS, hr, cols), BF16),
        compiler_params=_cp(("parallel", "parallel")))(jnp.reshape(c, (1,)).astype(jnp.int32), slabs, recv)


def _sum_chips(name, own, recv, chip, c):
    _, hr, cols = recv.shape
    tr = _pick(hr, STREAM_TILE // 2)
    nb = hr // tr

    def body(sc_ref, own_ref, recv_ref, o_ref):
        total = own_ref[...].astype(F32)
        for k in range(3):
            total = total + recv_ref[k].astype(F32)
        o_ref[...] = total

    grid_spec = pltpu.PrefetchScalarGridSpec(
        num_scalar_prefetch=1, grid=(nb,),
        in_specs=[pl.BlockSpec((None, tr, cols), lambda i, sc: (sc[0], i, 0)),
                  pl.BlockSpec((3, tr, cols), lambda i, sc: (0, i, 0))],
        out_specs=pl.BlockSpec((tr, cols), lambda i, sc: (sc[1] * nb + i, 0)))
    return pl.pallas_call(
        body, name=name, grid_spec=grid_spec, out_shape=jax.ShapeDtypeStruct((2 * hr, cols), F32),
        compiler_params=_cp(("parallel",)))(jnp.stack([chip, c]).astype(jnp.int32), own, recv)


def _adamw_math(w, g, m, v):
    m = ADAM_B1 * m + (1.0 - ADAM_B1) * g
    v = ADAM_B2 * v + (1.0 - ADAM_B2) * (g * g)
    m_hat = m / (1.0 - ADAM_B1 ** ADAM_STEP)
    v_hat = v / (1.0 - ADAM_B2 ** ADAM_STEP)
    delta = -ADAM_LR * (m_hat / (jnp.sqrt(v_hat) + ADAM_EPS) + ADAM_WD * w)
    return delta, m, v


def _adamw(name, w, gs, m, v):
    nl, r, cols = w.shape
    tr = _pick(r, ROW_TILE)

    def body(*refs):
        w_ref, m_ref, v_ref = refs[0:3]
        g_refs = refs[3:3 + nl]
        go_ref, d_ref, nm_ref, nv_ref = refs[3 + nl:]
        layer = pl.program_id(0)
        g = g_refs[0][...]
        for j in range(1, nl):
            g = jnp.where(layer == j, g_refs[j][...], g)
        d, nm, nv = _adamw_math(w_ref[...], g, m_ref[...], v_ref[...])
        go_ref[...] = g
        d_ref[...] = d
        nm_ref[...] = nm
        nv_ref[...] = nv

    spec3 = pl.BlockSpec((None, tr, cols), lambda l, i: (l, i, 0))
    gspec = pl.BlockSpec((tr, cols), lambda l, i: (i, 0))
    out = jax.ShapeDtypeStruct((nl, r, cols), F32)
    return pl.pallas_call(
        body, name=name, grid=(nl, r // tr), in_specs=[spec3] * 3 + [gspec] * nl, out_specs=[spec3] * 4,
        out_shape=[out] * 4, compiler_params=_cp(("parallel", "parallel")))(w, m, v, *gs)


def _adamw_small(name, groups):
    n = len(groups)
    flat = [a for grp in groups for a in grp]

    def body(*refs):
        ins, outs = refs[:4 * n], refs[4 * n:]
        for p in range(n):
            w_ref, g_ref, m_ref, v_ref = ins[4 * p:4 * p + 4]
            d, nm, nv = _adamw_math(w_ref[...], g_ref[...], m_ref[...], v_ref[...])
            outs[3 * p][...] = d
            outs[3 * p + 1][...] = nm
            outs[3 * p + 2][...] = nv

    vm = pl.BlockSpec(memory_space=pltpu.VMEM)
    out_shape = [jax.ShapeDtypeStruct(grp[0].shape, F32) for grp in groups for _ in range(3)]
    res = pl.pallas_call(
        body, name=name, in_specs=[vm] * (4 * n), out_specs=[vm] * (3 * n), out_shape=out_shape)(*flat)
    return [tuple(res[3 * p:3 * p + 3]) for p in range(n)]


def _block_diag_pairs(w):
    h, d, _ = w.shape
    z = jnp.zeros((h // 2, d, d), w.dtype)
    top = jnp.concatenate([w[0::2], z], axis=2)
    bot = jnp.concatenate([z, w[1::2]], axis=2)
    return jnp.concatenate([top, bot], axis=1).astype(BF16)


def _diag_pairs_to_heads(g, d):
    a = g[:, :d, :d]
    b = g[:, d:, d:]
    return jnp.stack([a, b], axis=1).reshape(-1, d, d)


def _rows128(a):
    flat = a.reshape(-1, LANES)
    pad = (-flat.shape[0]) % SUBLANES
    if pad:
        flat = jnp.concatenate([flat, jnp.zeros((pad, LANES), flat.dtype)], axis=0)
    return flat


def _unshard_last(g4, shape):
    g4 = g4.reshape((N_CHIPS,) + tuple(shape))
    return jnp.concatenate([g4[j] for j in range(N_CHIPS)], axis=-1)


def kernel(x, norm_gains, hyb_w_in, hyb_conv_a, hyb_conv_b, hyb_conv_b_bias, hyb_rg_w_a, hyb_rg_b_a, hyb_rg_w_x, hyb_rg_b_x, hyb_rg_lambda, hyb_w_out, sb_w_qkv, sb_w_o, mlp_w_up, mlp_w_down, loss_target, m_norm_gains, m_hyb_w_in, m_hyb_conv_a, m_hyb_conv_b, m_hyb_conv_b_bias, m_hyb_rg_w_a, m_hyb_rg_b_a, m_hyb_rg_w_x, m_hyb_rg_b_x, m_hyb_rg_lambda, m_hyb_w_out, m_sb_w_qkv, m_sb_w_o, m_mlp_w_up, m_mlp_w_down, v_norm_gains, v_hyb_w_in, v_hyb_conv_a, v_hyb_conv_b, v_hyb_conv_b_bias, v_hyb_rg_w_a, v_hyb_rg_b_a, v_hyb_rg_w_x, v_hyb_rg_b_x, v_hyb_rg_lambda, v_hyb_w_out, v_sb_w_qkv, v_sb_w_o, v_mlp_w_up, v_mlp_w_down):
    cx_ = lax.axis_index("x")
    cy_ = lax.axis_index("y")
    cc_ = lax.axis_index("c")
    chip = 2 * cx_ + cy_

    x0 = x[0]
    target = loss_target[0]
    s, d = x0.shape
    heads = SB_HEADS
    assert d // heads == LANES
    n_rg, hd = hyb_rg_w_a.shape[1], hyb_rg_w_a.shape[2]
    wmix = n_rg * hd
    assert 2 * hd == LANES

    big = {
        "hyb_w_in": (hyb_w_in, 0), "hyb_w_out": (hyb_w_out, 0), "mlp_w_up0": (mlp_w_up, 0),
        "mlp_w_down0": (mlp_w_down, 0), "sb_w_qkv": (sb_w_qkv, 0), "sb_w_o": (sb_w_o, 0),
        "mlp_w_up1": (mlp_w_up, 1), "mlp_w_down1": (mlp_w_down, 1),
    }
    names = list(big)
    slots = [_cast_into_slot("cast_" + k, big[k][0], big[k][1], chip) for k in names]
    full = {k: _allgather_async("allgather_" + k, slot, cid) for cid, (k, slot) in enumerate(zip(names, slots))}
    rowsharded = lambda k: full[k].reshape(-1, full[k].shape[2])

    ng_s, ca_s, cb_s = norm_gains.reshape(-1, norm_gains.shape[2]), hyb_conv_a[0], hyb_conv_b[0]
    packed = jnp.concatenate([_rows128(ng_s), _rows128(ca_s), _rows128(cb_s)], axis=0)
    gathered = _allgather_chips_small("allgather_small", packed)
    n0 = ng_s.size // LANES
    n1 = n0 + (-n0) % SUBLANES
    m0 = ca_s.size // LANES
    m1 = m0 + (-m0) % SUBLANES
    k0 = cb_s.size // LANES
    gains = _unshard_last(gathered[:, 0:n0], ng_s.shape).reshape(2, 4, 1, d)
    conv_a = _unshard_last(gathered[:, n1:n1 + m0], ca_s.shape)
    conv_b = _unshard_last(gathered[:, n1 + m1:n1 + m1 + k0], cb_s.shape)
    bias, b_a, b_x, lam = hyb_conv_b_bias, hyb_rg_b_a, hyb_rg_b_x, hyb_rg_lambda
    wa_blk = _block_diag_pairs(hyb_rg_w_a[0])
    wx_blk = _block_diag_pairs(hyb_rg_w_x[0])

    relu_sq = lambda acc: (jnp.maximum(acc, 0.0), jnp.square(jnp.maximum(acc, 0.0)))

    h1 = _rms_fwd("rms_pre0", x0, gains[0, 0])
    proj = _mm_fwd_col("proj_in", h1, full["hyb_w_in"])[0]
    ycat, hseq = _mixer_fwd(proj, conv_a, conv_b, bias, wa_blk, b_a, wx_blk, b_x, lam)
    mix0 = _mm_fwd_row("proj_out", ycat, rowsharded("hyb_w_out"))
    x1, h2 = _rms_post("rms_mix0", mix0, gains[0, 1], x0, gains[0, 2])
    u0, a0 = _mm_fwd_col("mlp_up0", h2, full["mlp_w_up0"], (BF16, BF16), relu_sq)
    mlp0 = _mm_fwd_row("mlp_down0", a0, rowsharded("mlp_w_down0"))
    x2, h3 = _rms_post("rms_mlp0", mlp0, gains[0, 3], x1, gains[1, 0])

    qkv = _mm_fwd_col("qkv", h3, full["sb_w_qkv"], (BF16,))[0]
    att, tot = _attn_fwd(qkv, heads)
    mix1 = _mm_fwd_row("attn_out", att, rowsharded("sb_w_o"))
    x3, h4 = _rms_post("rms_mix1", mix1, gains[1, 1], x2, gains[1, 2])
    u1, a1 = _mm_fwd_col("mlp_up1", h4, full["mlp_w_up1"], (BF16, BF16), relu_sq)
    mlp1 = _mm_fwd_row("mlp_down1", a1, rowsharded("mlp_w_down1"))
    dy, dmlp1, dgain_mlp1, loss_local = _last_norm_and_loss("last_norm_loss", mlp1, gains[1, 3], x3, target)
    loss = lax.psum(loss_local, ("x", "y", "c"))

    dgain = [[None] * 4 for _ in range(2)]
    drelu = lambda acc, u: (acc * (2.0 * u.astype(F32)),)
    stage_a, stage_b, gfull = {}, {}, {}

    def tie(main, side):
        return lax.optimization_barrier((main, side))

    def reduce_start(k, slab, main):
        main, slab = tie(main, slab)
        stage_a[k] = _to_sibling_async("grads_to_sibling_" + k, slab)
        return main

    def reduce_to_chips(k, main):
        slab, from_sibling = stage_a.pop(k)
        main, part = tie(main, _add_sibling("grads_add_" + k, slab, from_sibling, cc_))
        stage_b[k] = _to_chips_async("grads_to_chips_" + k, part)
        return main

    def reduce_split_start(k, act, dy, cs, main):
        main, other = tie(main, _mm_wgrad_half(k + "_wgrad_sibling_rows", act, dy, 1 - cc_, cs))
        stage_a[k] = (act, dy, cs, _swap_with_sibling_async("grads_to_sibling_" + k, other))
        return main

    def reduce_split_to_chips(k, main):
        act, dy, cs, from_sibling = stage_a.pop(k)
        main, part = tie(main, _mm_wgrad_half(k + "_wgrad_my_rows", act, dy, cc_, cs, init=from_sibling))
        stage_b[k] = _to_chips_async("grads_to_chips_" + k, part)
        return main

    def after(value, token):
        return tie(value, token)[0]

    def reduce_finish(k, main):
        own, from_chips = stage_b.pop(k)
        main, half = tie(main, _sum_chips("grads_sum_" + k, after(own, main), from_chips, chip, cc_))
        gfull[k] = _join_sibling_async("grads_join_" + k, half)
        return main

    def mlp_bwd(layer, dxo, dmlp, xin, hin, u, a, mix):
        down, up = f"mlp_w_down{layer}", f"mlp_w_up{layer}"
        wd, wu = rowsharded(down), full[up]
        dmlp = reduce_split_start(down, a, dmlp, None, dmlp)
        du = _mm_bwd_row(f"mlp_down{layer}_bwd", dmlp, wd, (BF16,), u, drelu)[0]
        du = reduce_split_start(up, hin, du, wu.shape[2], du)
        du = reduce_split_to_chips(down, du)
        dh = _mm_bwd_col(f"mlp_up{layer}_bwd", du, wu)
        dh = reduce_split_to_chips(up, dh)
        dxm, dgain[layer][2], dmix, dgain[layer][1] = _rms_bwd_pair(
            f"rms_premlp{layer}_mix{layer}_bwd", xin, gains[layer, 2], dh, dxo, mix, gains[layer, 1])
        return dxm, dmix

    dgain[1][3] = dgain_mlp1
    dx3, dmix1 = mlp_bwd(1, dy, dmlp1, x3, h4, u1, a1, mix1)
    dmix1 = reduce_start("sb_w_o", _mm_wgrad_row("attn_out_wgrad", att, dmix1).reshape(N_CHIPS, -1, d), dmix1)
    datt = _mm_bwd_row("attn_out_bwd", dmix1, rowsharded("sb_w_o"), (BF16,))[0]
    dq, dk, dv = _attn_bwd(qkv, tot, datt, heads)
    dqkv = jnp.concatenate([dq, dk, dv], axis=1)
    dqkv = reduce_to_chips("sb_w_o", dqkv)
    dqkv = reduce_finish("mlp_w_down1", dqkv)
    dqkv = reduce_finish("mlp_w_up1", dqkv)
    dqkv = reduce_split_start("sb_w_qkv", h3, dqkv, full["sb_w_qkv"].shape[2], dqkv)
    dh3 = _mm_bwd_col("qkv_bwd", dqkv, full["sb_w_qkv"])
    dh3 = reduce_split_to_chips("sb_w_qkv", dh3)
    dx2, dgain[1][0], dmlp0, dgain[0][3] = _rms_bwd_pair(
        "rms_pre1_mlp0_bwd", x2, gains[1, 0], dh3, dx3, mlp0, gains[0, 3])

    dx1, dmix0 = mlp_bwd(0, dx2, dmlp0, x1, h2, u0, a0, mix0)
    dmix0 = reduce_finish("sb_w_o", dmix0)
    dmix0 = reduce_finish("sb_w_qkv", dmix0)
    dmix0 = reduce_finish("mlp_w_down0", dmix0)
    dmix0 = reduce_start("hyb_w_out", _mm_wgrad_row("proj_out_wgrad", ycat, dmix0).reshape(N_CHIPS, -1, d), dmix0)
    dycat = _mm_bwd_row("proj_out_bwd", dmix0, rowsharded("hyb_w_out"))[0]
    dproj, xr_b, dpa_b, dpx_b, sg = _mixer_bwd(
        proj, hseq, dycat, conv_a, conv_b, bias, wa_blk, b_a, wx_blk, b_x, lam)
    dproj = reduce_finish("mlp_w_up0", dproj)
    dproj = reduce_to_chips("hyb_w_out", dproj)
    dproj = reduce_split_start("hyb_w_in", h1, dproj, full["hyb_w_in"].shape[2], dproj)
    dh1 = _mm_bwd_col("proj_in_bwd", dproj, full["hyb_w_in"])
    dh1 = reduce_split_to_chips("hyb_w_in", dh1)
    dx0, dgain[0][0] = _rms_bwd("rms_pre0_bwd", x0, gains[0, 0], dh1, res=dx1)
    dwa = _diag_pairs_to_heads(_mm_wgrad_diag("rg_w_a_wgrad", xr_b, dpa_b), hd)
    dwx = _diag_pairs_to_heads(_mm_wgrad_diag("rg_w_x_wgrad", xr_b, dpx_b), hd)

    dgains = jnp.concatenate([dgain[l][k] for l in range(2) for k in range(4)], axis=0)
    small_parts = [dgains, sg[_SG_CONV_A:_SG_CONV_A + 3], sg[_SG_CONV_B:_SG_CONV_B + 4], sg[_SG_BIAS:_SG_BIAS + 1],
                   dwa, sg[_SG_BA:_SG_BA + 1], dwx, sg[_SG_BX:_SG_BX + 1], sg[_SG_LAM:_SG_LAM + 1]]
    small_rows = [_rows128(p) for p in small_parts]
    n_small = sum(rws.shape[0] for rws in small_rows)
    tail_pad = [jnp.zeros(((-n_small) % (2 * SUBLANES), LANES), F32)] if n_small % (2 * SUBLANES) else []
    reduced = _allreduce_small("allreduce_small", jnp.concatenate(small_rows + tail_pad, axis=0))
    small_full, off = [], 0
    for p, rws in zip(small_parts, small_rows):
        small_full.append(reduced[off:off + p.size // LANES].reshape(p.shape))
        off += rws.shape[0]
    g_gains, g_ca, g_cb, g_bias, g_wa, g_ba, g_wx, g_bx, g_lam = small_full

    def my_cols(g, width):
        return lax.dynamic_slice_in_dim(g, chip * width, width, axis=g.ndim - 1)

    small = [
        ("norm_gains", norm_gains, my_cols(g_gains, norm_gains.shape[2]).reshape(norm_gains.shape),
         m_norm_gains, v_norm_gains),
        ("hyb_conv_a", hyb_conv_a, my_cols(g_ca, hyb_conv_a.shape[2])[None], m_hyb_conv_a, v_hyb_conv_a),
        ("hyb_conv_b", hyb_conv_b, my_cols(g_cb, hyb_conv_b.shape[2])[None], m_hyb_conv_b, v_hyb_conv_b),
        ("hyb_conv_b_bias", hyb_conv_b_bias, g_bias, m_hyb_conv_b_bias, v_hyb_conv_b_bias),
        ("hyb_rg_w_a", hyb_rg_w_a, g_wa[None], m_hyb_rg_w_a, v_hyb_rg_w_a),
        ("hyb_rg_b_a", hyb_rg_b_a, g_ba, m_hyb_rg_b_a, v_hyb_rg_b_a),
        ("hyb_rg_w_x", hyb_rg_w_x, g_wx[None], m_hyb_rg_w_x, v_hyb_rg_w_x),
        ("hyb_rg_b_x", hyb_rg_b_x, g_bx, m_hyb_rg_b_x, v_hyb_rg_b_x),
        ("hyb_rg_lambda", hyb_rg_lambda, g_lam, m_hyb_rg_lambda, v_hyb_rg_lambda),
    ]
    to2d = lambda a: a.reshape(-1, a.shape[-1])
    small_res = _adamw_small("adamw_small", [tuple(to2d(a) for a in (w, g, m, v)) for _, w, g, m, v in small])
    out = {}
    for (nm, w, g, _, _), (dl, nmom, nvar) in zip(small, small_res):
        out[nm] = (g, dl.reshape(w.shape), nmom.reshape(w.shape), nvar.reshape(w.shape))

    stacked = {
        "mlp_w_down": (mlp_w_down, m_mlp_w_down, v_mlp_w_down, ["mlp_w_down0", "mlp_w_down1"]),
        "mlp_w_up": (mlp_w_up, m_mlp_w_up, v_mlp_w_up, ["mlp_w_up0", "mlp_w_up1"]),
        "sb_w_o": (sb_w_o, m_sb_w_o, v_sb_w_o, ["sb_w_o"]),
        "sb_w_qkv": (sb_w_qkv, m_sb_w_qkv, v_sb_w_qkv, ["sb_w_qkv"]),
        "hyb_w_out": (hyb_w_out, m_hyb_w_out, v_hyb_w_out, ["hyb_w_out"]),
        "hyb_w_in": (hyb_w_in, m_hyb_w_in, v_hyb_w_in, ["hyb_w_in"]),
    }

    def update(k, token):
        w, m, v, parts = stacked[k]
        out[k] = tuple(_adamw("adamw_" + k, w, [after(gfull[p], token) for p in parts], m, v))
        return out[k][1]

    token = small_res[0][0]
    token = update("sb_w_qkv", token)
    token = update("sb_w_o", token)
    token = update("mlp_w_down", token)
    token = reduce_finish("hyb_w_out", token)
    token = update("mlp_w_up", token)
    token = reduce_finish("hyb_w_in", token)
    token = update("hyb_w_out", token)
    update("hyb_w_in", token)

    order = ["norm_gains", "hyb_w_in", "hyb_conv_a", "hyb_conv_b", "hyb_conv_b_bias", "hyb_rg_w_a", "hyb_rg_b_a",
             "hyb_rg_w_x", "hyb_rg_b_x", "hyb_rg_lambda", "hyb_w_out", "sb_w_qkv", "sb_w_o", "mlp_w_up",
             "mlp_w_down"]
    return (loss, dx0[None], *[out[k][0] for k in order], *[out[k][1] for k in order],
            *[out[k][2] for k in order], *[out[k][3] for k in order])
```

```python
import functools
import math

import jax
import jax.numpy as jnp
from jax import lax
from jax.experimental import pallas as pl
from jax.experimental.pallas import tpu as pltpu
from jax.experimental.pallas import tpu_sc as plsc

F32 = jnp.float32
BF16 = jnp.bfloat16
MESH = pl.DeviceIdType.MESH

SB_HEADS = 16
NORM_EPS = 1e-6
LRU_C = 8.0
ADAM_LR = 0.001
ADAM_B1 = 0.9
ADAM_B2 = 0.999
ADAM_EPS = 1e-08
ADAM_WD = 0.01
ADAM_STEP = 10

LANES = 128
SUBLANES = 8
VMEM_LIMIT = 48 * 1024 * 1024
MM_TILE = 1024
MM_VMEM_BUDGET = 40 * 1024 * 1024
MM_TILE_N = 1280
MM_TILE_K = 2048
ROW_TILE = 256
STREAM_TILE = 1024
ATT_TILE = 512
ATT_HEADS_PER_STEP = 2
ATT_FWD_HEADS_PER_STEP = 4
N_CHIPS = 4
COLLECTIVE_SIBLING = 8
COLLECTIVE_CHIPS = 9

_DIMS = {
    "nn": (((1,), (0,)), ((), ())),
    "nt": (((1,), (1,)), ((), ())),
    "tn": (((0,), (0,)), ((), ())),
}


def _cp(sem=None, vmem=VMEM_LIMIT):
    return pltpu.CompilerParams(dimension_semantics=sem, vmem_limit_bytes=vmem)


def _pick(dim, pref):
    t = min(dim, pref)
    while dim % t:
        t -= LANES
    return t


def _whole(shape):
    nd = len(shape)
    return pl.BlockSpec(tuple(shape), lambda *_: (0,) * nd)


def _sigmoid(z):
    return 1.0 / (1.0 + jnp.exp(-z))


def _log_sigmoid(z):
    return jnp.minimum(z, 0.0) - jnp.log(1.0 + jnp.exp(-jnp.abs(z)))


def _expm1(z):
    series = z * (1.0 + z * (0.5 + z * (1.0 / 6.0 + z * (1.0 / 24.0))))
    return jnp.where(jnp.abs(z) < 0.05, series, jnp.exp(z) - 1.0)


_GELU_C = math.sqrt(2.0 / math.pi)


def _gelu_and_grad(g):
    inner = _GELU_C * (g + 0.044715 * g * g * g)
    t = jnp.tanh(inner)
    val = 0.5 * g * (1.0 + t)
    grad = 0.5 * (1.0 + t) + 0.5 * g * (1.0 - t * t) * _GELU_C * (1.0 + 3.0 * 0.044715 * g * g)
    return val, grad


def _shift_down(cur, prev8, k, rows):
    n = cur.shape[0]
    rolled = pltpu.roll(cur, k, 0)
    head = jnp.tile(pltpu.roll(prev8, k, 0), (n // SUBLANES, 1))
    return jnp.where(rows < k, head, rolled)


def _shift_up(cur, next8, k, rows):
    n = cur.shape[0]
    rolled = pltpu.roll(cur, n - k, 0)
    tail = jnp.tile(pltpu.roll(next8, SUBLANES - k, 0), (n // SUBLANES, 1))
    return jnp.where(rows >= n - k, tail, rolled)


def _colsum(v):
    return jnp.sum(v, axis=0, keepdims=True)


def _matmul(name, mode, grid, operands, in_specs, out_shapes, out_specs, acc_shape, epilogue=None):
    nk = grid[2]
    n_in = len(operands)
    dims = _DIMS[mode]

    def finish(acc, extra, outs):
        res = epilogue(acc, *[e[...] for e in extra]) if epilogue is not None else (acc,)
        for o_ref, o in zip(outs, res):
            o_ref[...] = o.astype(o_ref.dtype)

    def product(a_ref, b_ref):
        return lax.dot_general(a_ref[...].astype(BF16), b_ref[...].astype(BF16), dims, preferred_element_type=F32)

    def body_single(*refs):
        finish(product(refs[0], refs[1]), refs[2:n_in], refs[n_in:])

    def body(*refs):
        extra = refs[2:n_in]
        outs = refs[n_in:-1]
        acc_ref = refs[-1]
        k = pl.program_id(2)

        @pl.when(k == 0)
        def _():
            acc_ref[...] = product(refs[0], refs[1])

        @pl.when(k > 0)
        def _():
            acc_ref[...] += product(refs[0], refs[1])

        @pl.when(k == nk - 1)
        def _():
            finish(acc_ref[...], extra, outs)

    return pl.pallas_call(
        body_single if nk == 1 else body, name=name, grid=grid, in_specs=in_specs, out_specs=out_specs,
        out_shape=out_shapes, scratch_shapes=[] if nk == 1 else [pltpu.VMEM(acc_shape, F32)],
        compiler_params=_cp(("parallel", "parallel", "arbitrary")),
    )(*operands)


def _pick_m(m, tk, tn, a_dtype, b_dtype, out_dtypes, extra_dtypes=()):
    size = lambda dt: jnp.dtype(dt).itemsize
    per_row = 2 * tk * size(a_dtype) + tn * (2 * sum(size(dt) for dt in tuple(out_dtypes) + tuple(extra_dtypes)) + 4)
    fixed = 2 * tk * tn * size(b_dtype)
    tm = _pick(m, MM_TILE)
    while tm > LANES and tm * per_row + fixed > MM_VMEM_BUDGET:
        tm = _pick(m, tm // 2)
    return tm


def _mm_fwd_col(name, a, wfull, out_dtypes=(F32,), epilogue=None):
    s, kdim = a.shape
    _, _, cs = wfull.shape
    tk, tn = _pick(kdim, MM_TILE_K), _pick(cs, MM_TILE_N)
    tm = _pick_m(s, tk, tn, a.dtype, wfull.dtype, out_dtypes)
    nbj = cs // tn
    grid = (s // tm, N_CHIPS * nbj, kdim // tk)
    out_shapes = [jax.ShapeDtypeStruct((s, N_CHIPS * cs), dt) for dt in out_dtypes]
    out_specs = [pl.BlockSpec((tm, tn), lambda i, n, k: (i, n)) for _ in out_dtypes]
    return _matmul(
        name, "nn", grid, [a, wfull],
        [pl.BlockSpec((tm, tk), lambda i, n, k: (i, k)),
         pl.BlockSpec((None, tk, tn), lambda i, n, k: (n // nbj, k, n % nbj))],
        out_shapes, out_specs, (tm, tn), epilogue)


def _mm_fwd_row(name, a, w2d, out_dtype=BF16):
    s, kdim = a.shape
    _, n_out = w2d.shape
    tk, tn = _pick(kdim, MM_TILE_K), _pick(n_out, MM_TILE)
    tm = _pick_m(s, tk, tn, a.dtype, w2d.dtype, (out_dtype,))
    grid = (s // tm, n_out // tn, kdim // tk)
    return _matmul(
        name, "nn", grid, [a, w2d],
        [pl.BlockSpec((tm, tk), lambda i, n, k: (i, k)),
         pl.BlockSpec((tk, tn), lambda i, n, k: (k, n))],
        [jax.ShapeDtypeStruct((s, n_out), out_dtype)],
        [pl.BlockSpec((tm, tn), lambda i, n, k: (i, n))], (tm, tn))[0]


def _mm_bwd_col(name, dy, wfull, out_dtype=BF16):
    s, _ = dy.shape
    _, kdim, cs = wfull.shape
    tn, tk = _pick(kdim, MM_TILE), _pick(cs, MM_TILE_K)
    tm = _pick_m(s, tk, tn, dy.dtype, wfull.dtype, (out_dtype,))
    nbj = cs // tk
    grid = (s // tm, kdim // tn, N_CHIPS * nbj)
    return _matmul(
        name, "nt", grid, [dy, wfull],
        [pl.BlockSpec((tm, tk), lambda i, n, k: (i, k)),
         pl.BlockSpec((None, tn, tk), lambda i, n, k: (k // nbj, n, k % nbj))],
        [jax.ShapeDtypeStruct((s, kdim), out_dtype)],
        [pl.BlockSpec((tm, tn), lambda i, n, k: (i, n))], (tm, tn))[0]


def _mm_bwd_row(name, dy, w2d, out_dtypes=(F32,), extra=None, epilogue=None):
    s, n_in = dy.shape
    kdim, _ = w2d.shape
    tn, tk = _pick(kdim, MM_TILE), _pick(n_in, MM_TILE_K)
    tm = _pick_m(s, tk, tn, dy.dtype, w2d.dtype, out_dtypes, () if extra is None else (extra.dtype,))
    grid = (s // tm, kdim // tn, n_in // tk)
    operands = [dy, w2d]
    in_specs = [pl.BlockSpec((tm, tk), lambda i, n, k: (i, k)),
                pl.BlockSpec((tn, tk), lambda i, n, k: (n, k))]
    if extra is not None:
        operands.append(extra)
        in_specs.append(pl.BlockSpec((tm, tn), lambda i, n, k: (i, n)))
    return _matmul(
        name, "nt", grid, operands, in_specs,
        [jax.ShapeDtypeStruct((s, kdim), dt) for dt in out_dtypes],
        [pl.BlockSpec((tm, tn), lambda i, n, k: (i, n)) for _ in out_dtypes], (tm, tn), epilogue)


def _mm_wgrad_row(name, a, dy):
    s, kdim = a.shape
    _, n_out = dy.shape
    tn, ts = _pick(n_out, MM_TILE), _pick(s, MM_TILE_K)
    tm = _pick_m(kdim, ts, tn, a.dtype, dy.dtype, (BF16,))
    grid = (kdim // tm, n_out // tn, s // ts)
    return _matmul(
        name, "tn", grid, [a, dy],
        [pl.BlockSpec((ts, tm), lambda i, n, k: (k, i)),
         pl.BlockSpec((ts, tn), lambda i, n, k: (k, n))],
        [jax.ShapeDtypeStruct((kdim, n_out), BF16)],
        [pl.BlockSpec((tm, tn), lambda i, n, k: (i, n))], (tm, tn))[0]


def _mm_wgrad_half(name, a, dy, half, cs=None, init=None):
    s, kdim = a.shape
    ts = _pick(s, MM_TILE_K)
    nk = s // ts
    if cs is not None:
        hr, cols = kdim // 2, cs
        tn = _pick(cs, MM_TILE_N)
        tm = _pick_m(hr, ts, tn, a.dtype, dy.dtype, (BF16,), (BF16,))
        ni, nbj = hr // tm, cs // tn
        grid = (ni, N_CHIPS * nbj, nk)
        a_map = lambda i, n, k, h: (k, h[0] * ni + i)
        o_map = lambda i, n, k, h: (n // nbj, i, n % nbj)
    else:
        hr, cols = kdim // N_CHIPS // 2, dy.shape[1]
        tn = _pick(cols, MM_TILE)
        tm = _pick_m(hr, ts, tn, a.dtype, dy.dtype, (BF16,), (BF16,))
        ni = hr // tm
        grid = (N_CHIPS * ni, cols // tn, nk)
        a_map = lambda i, n, k, h: (k, (i // ni) * 2 * ni + h[0] * ni + i % ni)
        o_map = lambda i, n, k, h: (i // ni, i % ni, n)
    with_init = init is not None

    def body(*refs):
        a_ref, b_ref = refs[1], refs[2]
        init_ref = refs[3] if with_init else None
        o_ref, acc_ref = refs[-2], refs[-1]
        k = pl.program_id(2)

        def product():
            return lax.dot_general(a_ref[...].astype(BF16), b_ref[...].astype(BF16), _DIMS["tn"],
                                   preferred_element_type=F32)

        @pl.when(k == 0)
        def _():
            if with_init:
                acc_ref[...] = init_ref[...].astype(F32)
                acc_ref[...] += product()
            else:
                acc_ref[...] = product()

        @pl.when(k > 0)
        def _():
            acc_ref[...] += product()

        @pl.when(k == nk - 1)
        def _():
            o_ref[...] = acc_ref[...].astype(BF16)

    oblk = pl.BlockSpec((None, tm, tn), o_map)
    grid_spec = pltpu.PrefetchScalarGridSpec(
        num_scalar_prefetch=1, grid=grid,
        in_specs=[pl.BlockSpec((ts, tm), a_map), pl.BlockSpec((ts, tn), lambda i, n, k, h: (k, n))]
        + ([oblk] if with_init else []),
        out_specs=oblk, scratch_shapes=[pltpu.VMEM((tm, tn), F32)])
    operands = [jnp.reshape(half, (1,)).astype(jnp.int32), a, dy] + ([init] if with_init else [])
    return pl.pallas_call(
        body, name=name, grid_spec=grid_spec, out_shape=jax.ShapeDtypeStruct((N_CHIPS, hr, cols), BF16),
        compiler_params=_cp(("parallel", "parallel", "arbitrary")))(*operands)


def _mm_wgrad_diag(name, a, dy):
    s, width = a.shape
    nb = width // LANES
    ts = _pick(s, MM_TILE)
    grid = (nb, 1, s // ts)
    return _matmul(
        name, "tn", grid, [a, dy],
        [pl.BlockSpec((ts, LANES), lambda i, n, k: (k, i)),
         pl.BlockSpec((ts, LANES), lambda i, n, k: (k, i))],
        [jax.ShapeDtypeStruct((nb, LANES, LANES), F32)],
        [pl.BlockSpec((None, LANES, LANES), lambda i, n, k: (i, 0, 0))], (LANES, LANES))[0]


def _rowspec(tr, d):
    return pl.BlockSpec((tr, d), lambda i: (i, 0))


def _vecspec(d):
    return pl.BlockSpec((1, d), lambda i: (0, 0))


def _rms(x, g):
    return x * lax.rsqrt(jnp.mean(x * x, axis=-1, keepdims=True) + NORM_EPS) * g


def _cast_into_slot(name, w, layer, chip):
    _, r, c = w.shape
    tr = _pick(r, STREAM_TILE)

    def body(chip_ref, w_ref, o_ref):
        o_ref[...] = w_ref[...].astype(BF16)

    grid_spec = pltpu.PrefetchScalarGridSpec(
        num_scalar_prefetch=1, grid=(r // tr,),
        in_specs=[pl.BlockSpec((None, tr, c), lambda i, chip_ref: (layer, i, 0))],
        out_specs=pl.BlockSpec((None, tr, c), lambda i, chip_ref: (chip_ref[0], i, 0)))
    return pl.pallas_call(
        body, name=name, grid_spec=grid_spec, out_shape=jax.ShapeDtypeStruct((N_CHIPS, r, c), BF16),
        compiler_params=_cp(("parallel",)))(jnp.reshape(chip, (1,)).astype(jnp.int32), w)


def _rms_fwd(name, x, g):
    s, d = x.shape
    tr = _pick(s, ROW_TILE)

    def body(x_ref, g_ref, h_ref):
        h_ref[...] = _rms(x_ref[...], g_ref[...]).astype(BF16)

    return pl.pallas_call(
        body, name=name, grid=(s // tr,), in_specs=[_rowspec(tr, d), _vecspec(d)],
        out_specs=_rowspec(tr, d), out_shape=jax.ShapeDtypeStruct((s, d), BF16),
        compiler_params=_cp(("parallel",)))(x, g)


def _rms_post(name, y, g_post, res, g_next=None):
    s, d = y.shape
    tr = _pick(s, ROW_TILE)
    with_next = g_next is not None

    def body(*refs):
        if with_next:
            y_ref, gp_ref, r_ref, gn_ref, x_ref, h_ref = refs
        else:
            y_ref, gp_ref, r_ref, x_ref = refs
        xn = r_ref[...] + _rms(y_ref[...].astype(F32), gp_ref[...])
        x_ref[...] = xn
        if with_next:
            h_ref[...] = _rms(xn, gn_ref[...]).astype(BF16)

    operands = [y, g_post, res] + ([g_next] if with_next else [])
    in_specs = [_rowspec(tr, d), _vecspec(d), _rowspec(tr, d)] + ([_vecspec(d)] if with_next else [])
    out_shape = [jax.ShapeDtypeStruct((s, d), F32)] + ([jax.ShapeDtypeStruct((s, d), BF16)] if with_next else [])
    out_specs = [_rowspec(tr, d)] + ([_rowspec(tr, d)] if with_next else [])
    return pl.pallas_call(
        body, name=name, grid=(s // tr,), in_specs=in_specs, out_specs=out_specs, out_shape=out_shape,
        compiler_params=_cp(("parallel",)))(*operands)


def _rms_bwd(name, x, g, dy, res=None, out_dtype=F32):
    s, d = x.shape
    tr = _pick(s, ROW_TILE)
    nsteps = s // tr
    with_res = res is not None

    def body(*refs):
        if with_res:
            x_ref, g_ref, dy_ref, r_ref, dx_ref, dg_ref, acc_ref = refs
        else:
            x_ref, g_ref, dy_ref, dx_ref, dg_ref, acc_ref = refs
        i = pl.program_id(0)

        @pl.when(i == 0)
        def _():
            acc_ref[...] = jnp.zeros_like(acc_ref)

        xv = x_ref[...]
        dyv = dy_ref[...].astype(F32)
        r = lax.rsqrt(jnp.mean(xv * xv, axis=-1, keepdims=True) + NORM_EPS)
        xhat = xv * r
        gy = dyv * g_ref[...]
        dx = r * (gy - xhat * jnp.mean(gy * xhat, axis=-1, keepdims=True))
        if with_res:
            dx = dx + r_ref[...]
        dx_ref[...] = dx.astype(dx_ref.dtype)
        acc_ref[...] += jnp.sum((dyv * xhat).reshape(tr // SUBLANES, SUBLANES, d), axis=0)

        @pl.when(i == nsteps - 1)
        def _():
            dg_ref[...] = jnp.broadcast_to(_colsum(acc_ref[...]), (SUBLANES, d))

    operands = [x, g, dy] + ([res] if with_res else [])
    in_specs = [_rowspec(tr, d), _vecspec(d), _rowspec(tr, d)] + ([_rowspec(tr, d)] if with_res else [])
    dx, dg = pl.pallas_call(
        body, name=name, grid=(nsteps,), in_specs=in_specs,
        out_specs=[_rowspec(tr, d), pl.BlockSpec((SUBLANES, d), lambda i: (0, 0))],
        out_shape=[jax.ShapeDtypeStruct((s, d), out_dtype), jax.ShapeDtypeStruct((SUBLANES, d), F32)],
        scratch_shapes=[pltpu.VMEM((SUBLANES, d), F32)],
        compiler_params=_cp(("arbitrary",)))(*operands)
    return dx, dg[0:1]


def _rms_bwd_pair(name, x, g, dy, res, y2, g2):
    s, d = x.shape
    tr = _pick(s, ROW_TILE)
    nsteps = s // tr

    def through(xv, gv, dyv):
        r = lax.rsqrt(jnp.mean(xv * xv, axis=-1, keepdims=True) + NORM_EPS)
        xhat = xv * r
        gy = dyv * gv
        dx = r * (gy - xhat * jnp.mean(gy * xhat, axis=-1, keepdims=True))
        return dx, jnp.sum((dyv * xhat).reshape(tr // SUBLANES, SUBLANES, d), axis=0)

    def body(x_ref, g_ref, dy_ref, r_ref, y2_ref, g2_ref, dx_ref, d2_ref, dg_ref, dg2_ref, acc_ref, acc2_ref):
        i = pl.program_id(0)

        @pl.when(i == 0)
        def _():
            acc_ref[...] = jnp.zeros_like(acc_ref)
            acc2_ref[...] = jnp.zeros_like(acc2_ref)

        dx, part = through(x_ref[...], g_ref[...], dy_ref[...].astype(F32))
        dx = dx + r_ref[...]
        dx_ref[...] = dx
        acc_ref[...] += part
        d2, part2 = through(y2_ref[...].astype(F32), g2_ref[...], dx)
        d2_ref[...] = d2.astype(d2_ref.dtype)
        acc2_ref[...] += part2

        @pl.when(i == nsteps - 1)
        def _():
            dg_ref[...] = jnp.broadcast_to(_colsum(acc_ref[...]), (SUBLANES, d))
            dg2_ref[...] = jnp.broadcast_to(_colsum(acc2_ref[...]), (SUBLANES, d))

    row, vec = _rowspec(tr, d), _vecspec(d)
    gspec = pl.BlockSpec((SUBLANES, d), lambda i: (0, 0))
    dx, d2, dg, dg2 = pl.pallas_call(
        body, name=name, grid=(nsteps,), in_specs=[row, vec, row, row, row, vec],
        out_specs=[row, row, gspec, gspec],
        out_shape=[jax.ShapeDtypeStruct((s, d), F32), jax.ShapeDtypeStruct((s, d), BF16),
                   jax.ShapeDtypeStruct((SUBLANES, d), F32), jax.ShapeDtypeStruct((SUBLANES, d), F32)],
        scratch_shapes=[pltpu.VMEM((SUBLANES, d), F32), pltpu.VMEM((SUBLANES, d), F32)],
        compiler_params=_cp(("arbitrary",)))(x, g, dy, res, y2, g2)
    return dx, dg[0:1], d2, dg2[0:1]


def _last_norm_and_loss(name, y, g, res, target):
    s, d = y.shape
    tr = _pick(s, ROW_TILE)
    nsteps = s // tr

    def body(y_ref, g_ref, r_ref, t_ref, dx_ref, dy_ref, dg_ref, l_ref, acc_ref, lacc_ref):
        i = pl.program_id(0)

        @pl.when(i == 0)
        def _():
            acc_ref[...] = jnp.zeros_like(acc_ref)
            lacc_ref[...] = jnp.zeros_like(lacc_ref)

        yv = y_ref[...].astype(F32)
        gv = g_ref[...]
        r = lax.rsqrt(jnp.mean(yv * yv, axis=-1, keepdims=True) + NORM_EPS)
        yhat = yv * r
        err = r_ref[...] + yhat * gv - t_ref[...]
        dx = err * (1.0 / d)
        dx_ref[...] = dx
        lacc_ref[...] += jnp.sum((err * err).reshape(tr // SUBLANES, SUBLANES, d), axis=0)
        gy = dx * gv
        dy_ref[...] = (r * (gy - yhat * jnp.mean(gy * yhat, axis=-1, keepdims=True))).astype(dy_ref.dtype)
        acc_ref[...] += jnp.sum((dx * yhat).reshape(tr // SUBLANES, SUBLANES, d), axis=0)

        @pl.when(i == nsteps - 1)
        def _():
            dg_ref[...] = jnp.broadcast_to(_colsum(acc_ref[...]), (SUBLANES, d))
            l_ref[...] = jnp.full((SUBLANES, LANES), (0.5 / d) * jnp.sum(lacc_ref[...]), F32)

    dx, dy, dg, l = pl.pallas_call(
        body, name=name, grid=(nsteps,),
        in_specs=[_rowspec(tr, d), _vecspec(d), _rowspec(tr, d), _rowspec(tr, d)],
        out_specs=[_rowspec(tr, d), _rowspec(tr, d), pl.BlockSpec((SUBLANES, d), lambda i: (0, 0)),
                   pl.BlockSpec((SUBLANES, LANES), lambda i: (0, 0))],
        out_shape=[jax.ShapeDtypeStruct((s, d), F32), jax.ShapeDtypeStruct((s, d), BF16),
                   jax.ShapeDtypeStruct((SUBLANES, d), F32), jax.ShapeDtypeStruct((SUBLANES, LANES), F32)],
        scratch_shapes=[pltpu.VMEM((SUBLANES, d), F32), pltpu.VMEM((SUBLANES, d), F32)],
        compiler_params=_cp(("arbitrary",)))(y, g, res, target)
    return dx, dy, dg[0:1], l[0, 0]


def _gates(xr, wa, ba, wx, bx, lam):
    xb = xr.astype(BF16)
    r = _sigmoid(jnp.dot(xb, wa, preferred_element_type=F32) + ba)
    i = _sigmoid(jnp.dot(xb, wx, preferred_element_type=F32) + bx)
    log_a = LRU_C * r * _log_sigmoid(lam)
    a = jnp.exp(log_a)
    m = jnp.sqrt(-_expm1(2.0 * log_a))
    return r, i, a, m


def _mixer_fwd(proj, conv_a, conv_b, bias, wa_blk, ba, wx_blk, bx, lam):
    s, w5 = proj.shape
    w = w5 // 5
    nch = w // LANES
    ts = _pick(s, ROW_TILE)
    nt = s // ts

    def body(p_ref, pp_ref, ca_ref, cb_ref, bias_ref, wa_ref, ba_ref, wx_ref, bx_ref, lam_ref,
             y_ref, h_ref, a_scr, b_scr, hc_scr):
        t = pl.program_id(0)
        first = t == 0
        rows = lax.broadcasted_iota(jnp.int32, (ts, LANES), 0)

        @pl.when(first)
        def _():
            hc_scr[...] = jnp.zeros_like(hc_scr)

        def cur(comp, c):
            return p_ref[:, comp * w + c * LANES:comp * w + (c + 1) * LANES]

        def prev(comp, c):
            v = pp_ref[:, comp * w + c * LANES:comp * w + (c + 1) * LANES]
            return jnp.where(first, 0.0, v)

        for c in range(nch):
            sl = slice(c * LANES, (c + 1) * LANES)
            cx = cur(1, c) * cur(2, c)
            cxp = prev(1, c) * prev(2, c)
            wa3 = ca_ref[:, sl]
            conv = (wa3[2:3] * cx + wa3[1:2] * _shift_down(cx, cxp, 1, rows)
                    + wa3[0:1] * _shift_down(cx, cxp, 2, rows))
            y_ref[:, sl] = (cur(0, c) * conv).astype(BF16)

        for c in range(nch):
            sl = slice(c * LANES, (c + 1) * LANES)
            xb, xbp = cur(4, c), prev(4, c)
            wb4 = cb_ref[:, sl]
            xr = (wb4[3:4] * xb + wb4[2:3] * _shift_down(xb, xbp, 1, rows)
                  + wb4[1:2] * _shift_down(xb, xbp, 2, rows)
                  + wb4[0:1] * _shift_down(xb, xbp, 3, rows) + bias_ref[:, sl])
            _, i, a, m = _gates(xr, wa_ref[c], ba_ref[:, sl], wx_ref[c], bx_ref[:, sl], lam_ref[:, sl])
            a_scr[:, sl] = a
            b_scr[:, sl] = m * i * xr

        def step(r, h):
            h = a_scr[pl.ds(r, 1), :] * h + b_scr[pl.ds(r, 1), :]
            h_ref[pl.ds(r, 1), :] = h
            return h

        hc_scr[0:1, :] = lax.fori_loop(0, ts, step, hc_scr[0:1, :], unroll=8)

        for c in range(nch):
            sl = slice(c * LANES, (c + 1) * LANES)
            gel, _ = _gelu_and_grad(cur(3, c))
            y_ref[:, w + c * LANES:w + (c + 1) * LANES] = (h_ref[:, sl] * gel).astype(BF16)

    vec = lambda n: _whole((n, w))
    return pl.pallas_call(
        body, name="mixer_fwd", grid=(nt,),
        in_specs=[pl.BlockSpec((ts, w5), lambda t: (t, 0)),
                  pl.BlockSpec((SUBLANES, w5), lambda t: (jnp.maximum(t * (ts // SUBLANES) - 1, 0), 0)),
                  vec(3), vec(4), vec(1), _whole(wa_blk.shape), vec(1), _whole(wx_blk.shape), vec(1), vec(1)],
        out_specs=[pl.BlockSpec((ts, 2 * w), lambda t: (t, 0)), pl.BlockSpec((ts, w), lambda t: (t, 0))],
        out_shape=[jax.ShapeDtypeStruct((s, 2 * w), BF16), jax.ShapeDtypeStruct((s, w), F32)],
        scratch_shapes=[pltpu.VMEM((ts, w), F32), pltpu.VMEM((ts, w), F32), pltpu.VMEM((SUBLANES, w), F32)],
        compiler_params=_cp(("arbitrary",)),
    )(proj, proj, conv_a, conv_b, bias, wa_blk, ba, wx_blk, bx, lam)


_SG_CONV_A, _SG_CONV_B, _SG_BIAS, _SG_BA, _SG_BX, _SG_LAM, _SG_ROWS = 0, 3, 7, 8, 9, 10, 16


def _mixer_bwd(proj, hseq, dy, conv_a, conv_b, bias, wa_blk, ba, wx_blk, bx, lam):
    s, w5 = proj.shape
    w = w5 // 5
    nch = w // LANES
    ts = _pick(s, ROW_TILE)
    nt = s // ts
    tpb = ts // SUBLANES

    def body(p_ref, pp_ref, h_ref, hp_ref, dy_ref, ca_ref, cb_ref, bias_ref, wa_ref, ba_ref, wx_ref, bx_ref,
             lam_ref, dp_ref, xr_ref, dpa_ref, dpx_ref, sg_ref,
             a_scr, g_scr, l_scr, x_scr, r_scr, i_scr, m_scr, cl_scr, cdc_scr, cdx_scr):
        pid = pl.program_id(0)
        last = pid == 0
        first = pid == nt - 1
        rows = lax.broadcasted_iota(jnp.int32, (ts, LANES), 0)

        @pl.when(last)
        def _():
            sg_ref[...] = jnp.zeros_like(sg_ref)
            cl_scr[...] = jnp.zeros_like(cl_scr)
            cdc_scr[...] = jnp.zeros_like(cdc_scr)
            cdx_scr[...] = jnp.zeros_like(cdx_scr)

        def cur(comp, c):
            return p_ref[:, comp * w + c * LANES:comp * w + (c + 1) * LANES]

        def prev(comp, c):
            v = pp_ref[:, comp * w + c * LANES:comp * w + (c + 1) * LANES]
            return jnp.where(first, 0.0, v)

        def put(comp, c, v):
            dp_ref[:, comp * w + c * LANES:comp * w + (c + 1) * LANES] = v.astype(dp_ref.dtype)

        def acc(row, sl, v):
            sg_ref[row:row + 1, sl] += _colsum(v)

        for c in range(nch):
            sl = slice(c * LANES, (c + 1) * LANES)
            bg, cg, ax = cur(0, c), cur(1, c), cur(2, c)
            cx = cg * ax
            cxp = prev(1, c) * prev(2, c)
            cx1 = _shift_down(cx, cxp, 1, rows)
            cx2 = _shift_down(cx, cxp, 2, rows)
            wa3 = ca_ref[:, sl]
            conv = wa3[2:3] * cx + wa3[1:2] * cx1 + wa3[0:1] * cx2
            dya = dy_ref[:, sl]
            put(0, c, dya * conv)
            dconv = dya * bg
            nxt = cdc_scr[:, sl]
            dcx = (wa3[2:3] * dconv + wa3[1:2] * _shift_up(dconv, nxt, 1, rows)
                   + wa3[0:1] * _shift_up(dconv, nxt, 2, rows))
            cdc_scr[:, sl] = dconv[0:SUBLANES]
            put(1, c, dcx * ax)
            put(2, c, dcx * cg)
            acc(_SG_CONV_A + 2, sl, dconv * cx)
            acc(_SG_CONV_A + 1, sl, dconv * cx1)
            acc(_SG_CONV_A + 0, sl, dconv * cx2)

        for c in range(nch):
            sl = slice(c * LANES, (c + 1) * LANES)
            xb, xbp = cur(4, c), prev(4, c)
            wb4 = cb_ref[:, sl]
            xr = (wb4[3:4] * xb + wb4[2:3] * _shift_down(xb, xbp, 1, rows)
                  + wb4[1:2] * _shift_down(xb, xbp, 2, rows)
                  + wb4[0:1] * _shift_down(xb, xbp, 3, rows) + bias_ref[:, sl])
            r, i, a, m = _gates(xr, wa_ref[c], ba_ref[:, sl], wx_ref[c], bx_ref[:, sl], lam_ref[:, sl])
            gel, dgel = _gelu_and_grad(cur(3, c))
            dyb = dy_ref[:, w + c * LANES:w + (c + 1) * LANES]
            put(3, c, dyb * h_ref[:, sl] * dgel)
            g_scr[:, sl] = dyb * gel
            a_scr[:, sl] = a
            x_scr[:, sl] = xr
            r_scr[:, sl] = r
            i_scr[:, sl] = i
            m_scr[:, sl] = m

        def step(j, carry):
            r = ts - 1 - j
            lam_t = g_scr[pl.ds(r, 1), :] + carry
            l_scr[pl.ds(r, 1), :] = lam_t
            return a_scr[pl.ds(r, 1), :] * lam_t

        cl_scr[0:1, :] = lax.fori_loop(0, ts, step, cl_scr[0:1, :], unroll=8)

        for c in range(nch):
            sl = slice(c * LANES, (c + 1) * LANES)
            lam_t = l_scr[:, sl]
            hprev = _shift_down(h_ref[:, sl], jnp.where(first, 0.0, hp_ref[:, sl]), 1, rows)
            xr, r, i, m, a = x_scr[:, sl], r_scr[:, sl], i_scr[:, sl], m_scr[:, sl], a_scr[:, sl]
            da = lam_t * hprev
            dm = lam_t * i * xr
            di = lam_t * m * xr
            dxr = lam_t * m * i
            dlog_a = da * a - dm * a * a / m
            lam_p = lam_ref[:, sl]
            dr = dlog_a * (LRU_C * _log_sigmoid(lam_p))
            acc(_SG_LAM, sl, dlog_a * r * (LRU_C * _sigmoid(-lam_p)))
            dpa = dr * r * (1.0 - r)
            dpx = di * i * (1.0 - i)
            dpa_b, dpx_b = dpa.astype(BF16), dpx.astype(BF16)
            dxr = (dxr + lax.dot_general(dpa_b, wa_ref[c], _DIMS["nt"], preferred_element_type=F32)
                   + lax.dot_general(dpx_b, wx_ref[c], _DIMS["nt"], preferred_element_type=F32))
            xr_ref[:, sl] = xr.astype(BF16)
            dpa_ref[:, sl] = dpa_b
            dpx_ref[:, sl] = dpx_b
            acc(_SG_BA, sl, dpa)
            acc(_SG_BX, sl, dpx)
            acc(_SG_BIAS, sl, dxr)
            nxt = cdx_scr[:, sl]
            wb4 = cb_ref[:, sl]
            put(4, c, wb4[3:4] * dxr + wb4[2:3] * _shift_up(dxr, nxt, 1, rows)
                + wb4[1:2] * _shift_up(dxr, nxt, 2, rows) + wb4[0:1] * _shift_up(dxr, nxt, 3, rows))
            cdx_scr[:, sl] = dxr[0:SUBLANES]
            xb, xbp = cur(4, c), prev(4, c)
            acc(_SG_CONV_B + 3, sl, dxr * xb)
            acc(_SG_CONV_B + 2, sl, dxr * _shift_down(xb, xbp, 1, rows))
            acc(_SG_CONV_B + 1, sl, dxr * _shift_down(xb, xbp, 2, rows))
            acc(_SG_CONV_B + 0, sl, dxr * _shift_down(xb, xbp, 3, rows))

    blk = lambda width: pl.BlockSpec((ts, width), lambda p: (nt - 1 - p, 0))
    pre = lambda width: pl.BlockSpec(
        (SUBLANES, width), lambda p: (jnp.maximum((nt - 1 - p) * tpb - 1, 0), 0))
    vec = lambda n: _whole((n, w))
    big = lambda: pltpu.VMEM((ts, w), F32)
    small = lambda: pltpu.VMEM((SUBLANES, w), F32)
    return pl.pallas_call(
        body, name="mixer_bwd", grid=(nt,),
        in_specs=[blk(w5), pre(w5), blk(w), pre(w), blk(2 * w),
                  vec(3), vec(4), vec(1), _whole(wa_blk.shape), vec(1), _whole(wx_blk.shape), vec(1), vec(1)],
        out_specs=[blk(w5), blk(w), blk(w), blk(w), _whole((_SG_ROWS, w))],
        out_shape=[jax.ShapeDtypeStruct((s, w5), BF16), jax.ShapeDtypeStruct((s, w), BF16),
                   jax.ShapeDtypeStruct((s, w), BF16), jax.ShapeDtypeStruct((s, w), BF16),
                   jax.ShapeDtypeStruct((_SG_ROWS, w), F32)],
        scratch_shapes=[big(), big(), big(), big(), big(), big(), big(), small(), small(), small()],
        compiler_params=_cp(("arbitrary",)),
    )(proj, proj, hseq, hseq, dy, conv_a, conv_b, bias, wa_blk, ba, wx_blk, bx, lam)


def _split_dot(v, tri2):
    hi = v.astype(BF16)
    lo = (v - hi.astype(F32)).astype(BF16)
    return jnp.dot(jnp.concatenate([hi, lo], axis=1), tri2, preferred_element_type=F32)


def _tri(cmp):
    r = lax.broadcasted_iota(jnp.int32, (LANES, LANES), 0)
    c = lax.broadcasted_iota(jnp.int32, (LANES, LANES), 1)
    return cmp(r, c).astype(BF16)


def _lane_blocks(v):
    return [v[:, b * LANES:(b + 1) * LANES] for b in range(v.shape[1] // LANES)]


def _last_lane(v):
    return jnp.broadcast_to(v[:, LANES - 1:LANES], v.shape)


def _scores(q, kb, scale):
    return lax.dot_general(q, kb, _DIMS["nt"], preferred_element_type=F32) * scale


def _log_gates(z, diagonal):
    ls = jnp.minimum(z, 0.0) - jnp.log(1.0 + jnp.exp(-jnp.abs(z)))
    ln = ls - z
    valid = None
    if diagonal:
        valid = (lax.broadcasted_iota(jnp.int32, z.shape, 1) < lax.broadcasted_iota(jnp.int32, z.shape, 0))
        ln = jnp.where(valid, ln, 0.0)
    return ls, ln, valid


def _attn_fwd(qkv, heads):
    s = qkv.shape[0]
    dh = LANES
    tq = _pick(s, ATT_TILE)
    nq = s // tq
    nb = tq // LANES
    scale = 1.0 / math.sqrt(dh)

    hp = ATT_FWD_HEADS_PER_STEP
    groups = heads // hp
    wid = hp * dh

    def body(q_ref, k_ref, v_ref, o_ref, tot_ref, acc_scr, car_scr):
        qi = pl.program_id(1)
        acc_scr[...] = jnp.zeros_like(acc_scr)
        car_scr[...] = jnp.zeros_like(car_scr)
        tri = _tri(lambda r, c: r > c)
        tri = jnp.concatenate([tri, tri], axis=0)

        def tile(kt, diagonal):
            k0 = pl.multiple_of(kt * tq, tq)
            heads_cols = [slice(hh * dh, (hh + 1) * dh) for hh in range(hp)]
            zs = [_scores(q_ref[:, cols], k_ref[pl.ds(k0, tq), cols], scale) for cols in heads_cols]
            gates = [_log_gates(z, diagonal) for z in zs]
            sfxs = [_split_dot(jnp.concatenate(_lane_blocks(ln), axis=0), tri) for _, ln, _ in gates]
            for cols, (ls, ln, valid), sfx in zip(heads_cols, gates, sfxs):
                blocks = _lane_blocks(ln)
                car = car_scr[:, cols]
                parts = [None] * nb
                for b in reversed(range(nb)):
                    sb = sfx[b * tq:(b + 1) * tq]
                    parts[b] = sb + car
                    car = car + (sb[:, 0:1] + blocks[b][:, 0:1])
                car_scr[:, cols] = car
                wgt = jnp.exp(ls + jnp.concatenate(parts, axis=1))
                if diagonal:
                    wgt = jnp.where(valid, wgt, 0.0)
                acc_scr[:, cols] += jnp.dot(
                    wgt.astype(BF16), v_ref[pl.ds(k0, tq), cols], preferred_element_type=F32)

        tile(qi, True)

        def step(j, carry):
            tile(qi - 1 - j, False)
            return carry

        lax.fori_loop(0, qi, step, 0)
        o_ref[...] = acc_scr[...].astype(BF16)
        tot_ref[...] = car_scr[...]

    return pl.pallas_call(
        body, name="attn_fwd", grid=(groups, nq),
        in_specs=[pl.BlockSpec((tq, wid), lambda h, i: (i, h)),
                  pl.BlockSpec((s, wid), lambda h, i: (0, groups + h)),
                  pl.BlockSpec((s, wid), lambda h, i: (0, 2 * groups + h))],
        out_specs=[pl.BlockSpec((tq, wid), lambda h, i: (i, h)), pl.BlockSpec((tq, wid), lambda h, i: (i, h))],
        out_shape=[jax.ShapeDtypeStruct((s, heads * dh), BF16), jax.ShapeDtypeStruct((s, heads * dh), F32)],
        scratch_shapes=[pltpu.VMEM((tq, wid), F32), pltpu.VMEM((tq, wid), F32)],
        compiler_params=_cp(("parallel", "arbitrary")),
    )(qkv, qkv, qkv)


def _attn_bwd(qkv, tot, do, heads):
    s = qkv.shape[0]
    dh = LANES
    tq = _pick(s, ATT_TILE)
    nq = s // tq
    nb = tq // LANES
    scale = 1.0 / math.sqrt(dh)

    hp = ATT_HEADS_PER_STEP
    groups = heads // hp
    wid = hp * dh

    def body(q_ref, k_ref, v_ref, tot_ref, do_ref, dq_ref, dk_ref, dv_ref,
             dq_scr, dk_scr, dv_scr, cl_scr, cg_scr):
        qi = pl.program_id(1)

        @pl.when(qi == 0)
        def _():
            dk_scr[...] = jnp.zeros_like(dk_scr)
            dv_scr[...] = jnp.zeros_like(dv_scr)

        dq_scr[...] = jnp.zeros_like(dq_scr)
        cl_scr[...] = jnp.zeros_like(cl_scr)
        cg_scr[...] = jnp.zeros_like(cg_scr)
        tri_le = _tri(lambda r, c: r <= c)
        tri_le = jnp.concatenate([tri_le, tri_le], axis=0)
        tri_lt = _tri(lambda r, c: r < c)

        def tile(kt, diagonal):
            k0 = pl.multiple_of(kt * tq, tq)
            heads_cols = [slice(hh * dh, (hh + 1) * dh) for hh in range(hp)]
            keys = pl.ds(k0, tq)
            zs = [_scores(q_ref[:, cols], k_ref[keys, cols], scale) for cols in heads_cols]
            dws = [lax.dot_general(do_ref[:, cols], v_ref[keys, cols], _DIMS["nt"], preferred_element_type=F32)
                   for cols in heads_cols]
            gates = [_log_gates(z, diagonal) for z in zs]
            pins = [_split_dot(jnp.concatenate(_lane_blocks(ln), axis=0), tri_le) for _, ln, _ in gates]
            wgts, gs = [], []
            for cols, (ls, _, valid), pin, dw in zip(heads_cols, gates, pins, dws):
                total = tot_ref[:, cols]
                cl = cl_scr[:, cols]
                parts = []
                for b in range(nb):
                    pb = pin[b * tq:(b + 1) * tq] + cl
                    parts.append(total - pb)
                    cl = _last_lane(pb)
                cl_scr[:, cols] = cl
                wgt = jnp.exp(ls + jnp.concatenate(parts, axis=1))
                if diagonal:
                    wgt = jnp.where(valid, wgt, 0.0)
                wgts.append(wgt)
                gs.append(wgt * dw)
            pexs = [jnp.dot(jnp.concatenate(_lane_blocks(g), axis=0).astype(BF16), tri_lt,
                            preferred_element_type=F32) for g in gs]
            for cols, wgt in zip(heads_cols, wgts):
                dv_scr[keys, cols] += lax.dot_general(
                    wgt.astype(BF16), do_ref[:, cols], _DIMS["tn"], preferred_element_type=F32)
            for cols, (ls, _, valid), g, pex in zip(heads_cols, gates, gs, pexs):
                gblocks = _lane_blocks(g)
                cg = cg_scr[:, cols]
                parts = []
                for b in range(nb):
                    pb = pex[b * tq:(b + 1) * tq] + cg
                    parts.append(pb)
                    cg = _last_lane(pb + gblocks[b])
                cg_scr[:, cols] = cg
                dz = g - jnp.exp(ls) * (g + jnp.concatenate(parts, axis=1))
                if diagonal:
                    dz = jnp.where(valid, dz, 0.0)
                dz = dz.astype(BF16)
                dq_scr[:, cols] += jnp.dot(dz, k_ref[keys, cols], preferred_element_type=F32)
                dk_scr[keys, cols] += lax.dot_general(
                    dz, q_ref[:, cols], _DIMS["tn"], preferred_element_type=F32)

        def step(j, carry):
            tile(j, False)
            return carry

        lax.fori_loop(0, qi, step, 0)
        tile(qi, True)
        dq_ref[...] = (dq_scr[...] * scale).astype(BF16)

        @pl.when(qi == nq - 1)
        def _():
            dk_ref[...] = (dk_scr[...] * scale).astype(BF16)
            dv_ref[...] = dv_scr[...].astype(BF16)

    qblk = pl.BlockSpec((tq, wid), lambda h, i: (i, h))
    hblk = pl.BlockSpec((s, wid), lambda h, i: (0, h))
    out = jax.ShapeDtypeStruct((s, heads * dh), BF16)
    return pl.pallas_call(
        body, name="attn_bwd", grid=(groups, nq),
        in_specs=[qblk, pl.BlockSpec((s, wid), lambda h, i: (0, groups + h)),
                  pl.BlockSpec((s, wid), lambda h, i: (0, 2 * groups + h)), qblk, qblk],
        out_specs=[qblk, hblk, hblk], out_shape=[out, out, out],
        scratch_shapes=[pltpu.VMEM((tq, wid), F32), pltpu.VMEM((s, wid), F32), pltpu.VMEM((s, wid), F32),
                        pltpu.VMEM((tq, wid), F32), pltpu.VMEM((tq, wid), F32)],
        compiler_params=_cp(("parallel", "arbitrary")),
    )(qkv, qkv, qkv, tot, do)


def _place():
    x, y, c = lax.axis_index("x"), lax.axis_index("y"), lax.axis_index("c")
    chips = [(1 - x, y), (x, 1 - y), (1 - x, 1 - y)]
    return x, y, c, chips


def _remote(src, dst, send_sem, recv_sem, dev):
    return pltpu.make_async_remote_copy(
        src_ref=src, dst_ref=dst, send_sem=send_sem, recv_sem=recv_sem, device_id=dev, device_id_type=MESH)


def _handshake(peers):
    barrier = pltpu.get_barrier_semaphore()
    for dev in peers:
        pl.semaphore_signal(barrier, inc=1, device_id=dev, device_id_type=MESH)
    pl.semaphore_wait(barrier, len(peers))


def _sequencer_kernel(name, n_sems, collective_id):
    return functools.partial(
        pl.kernel, mesh=plsc.ScalarSubcoreMesh(axis_name="seq", num_cores=1), name=name,
        scratch_types=(pltpu.SemaphoreType.DMA,) * n_sems,
        compiler_params=pltpu.CompilerParams(collective_id=collective_id))


def _allgather_async(name, slot_buf, collective_id):
    buf = jax.new_ref(slot_buf, memory_space=pltpu.MemorySpace.HBM)
    hr = slot_buf.shape[1] // 2

    @_sequencer_kernel(name, 12, collective_id)
    def launch(*sems):
        send_sems, recv_sems, fsend_sems, frecv_sems = sems[0:3], sems[3:6], sems[6:9], sems[9:12]
        x, y, c, chips = _place()
        me = 2 * x + y
        sibling = (x, y, 1 - c)
        _handshake([(px, py, c) for px, py in chips] + [sibling])
        mine = buf.at[me, pl.ds(c * hr, hr)]
        firsts = []
        for k, (px, py) in enumerate(chips):
            cp = _remote(mine, mine, send_sems[k], recv_sems[k], (px, py, c))
            cp.start()
            firsts.append(cp)
        passed = []
        for k, (px, py) in enumerate(chips):
            slot = buf.at[2 * px + py, pl.ds(c * hr, hr)]
            _remote(slot, slot, send_sems[k], recv_sems[k], (px, py, c)).wait_recv()
            cp = _remote(slot, slot, fsend_sems[k], frecv_sems[k], sibling)
            cp.start()
            passed.append(cp)
        for k, (px, py) in enumerate(chips):
            slot = buf.at[2 * px + py, pl.ds((1 - c) * hr, hr)]
            _remote(slot, slot, fsend_sems[k], frecv_sems[k], sibling).wait_recv()
        for cp in firsts + passed:
            cp.wait_send()

    launch()
    return buf[...]


def _to_sibling_async(name, slab):
    src = jax.new_ref(slab, memory_space=pltpu.MemorySpace.HBM)
    hr = slab.shape[1] // 2
    got = jax.empty_ref(jax.ShapeDtypeStruct((N_CHIPS, hr, slab.shape[2]), slab.dtype),
                        memory_space=pltpu.MemorySpace.HBM)

    @_sequencer_kernel(name, 2, COLLECTIVE_SIBLING)
    def launch(send_sem, recv_sem):
        x, y, c, _ = _place()
        _handshake([(x, y, 1 - c)])
        _remote(src.at[:, pl.ds((1 - c) * hr, hr), :], got, send_sem, recv_sem, (x, y, 1 - c)).start()
        _remote(got, got, send_sem, recv_sem, (x, y, 1 - c)).wait()

    launch()
    return src[...], got[...]


def _swap_with_sibling_async(name, part):
    src = jax.new_ref(part, memory_space=pltpu.MemorySpace.HBM)
    got = jax.empty_ref(jax.ShapeDtypeStruct(part.shape, part.dtype), memory_space=pltpu.MemorySpace.HBM)

    @_sequencer_kernel(name, 2, COLLECTIVE_SIBLING)
    def launch(send_sem, recv_sem):
        x, y, c, _ = _place()
        _handshake([(x, y, 1 - c)])
        cp = _remote(src, got, send_sem, recv_sem, (x, y, 1 - c))
        cp.start()
        cp.wait()

    launch()
    return got[...]


def _to_chips_async(name, part):
    src = jax.new_ref(part, memory_space=pltpu.MemorySpace.HBM)
    got = jax.empty_ref(jax.ShapeDtypeStruct((3,) + part.shape[1:], part.dtype), memory_space=pltpu.MemorySpace.HBM)

    @_sequencer_kernel(name, 6, COLLECTIVE_CHIPS)
    def launch(*sems):
        send_sems, recv_sems = sems[0:3], sems[3:6]
        x, y, c, chips = _place()
        _handshake([(px, py, c) for px, py in chips])
        cps = []
        for k, (px, py) in enumerate(chips):
            cp = _remote(src.at[2 * px + py], got.at[k], send_sems[k], recv_sems[k], (px, py, c))
            cp.start()
            cps.append(cp)
        for cp in cps:
            cp.wait()

    launch()
    return src[...], got[...]


def _join_sibling_async(name, half_filled):
    buf = jax.new_ref(half_filled, memory_space=pltpu.MemorySpace.HBM)
    hr = half_filled.shape[0] // 2

    @_sequencer_kernel(name, 2, COLLECTIVE_SIBLING)
    def launch(send_sem, recv_sem):
        x, y, c, _ = _place()
        _handshake([(x, y, 1 - c)])
        mine = buf.at[pl.ds(c * hr, hr)]
        other = buf.at[pl.ds((1 - c) * hr, hr)]
        cp = _remote(mine, mine, send_sem, recv_sem, (x, y, 1 - c))
        cp.start()
        _remote(other, other, send_sem, recv_sem, (x, y, 1 - c)).wait_recv()
        cp.wait_send()

    launch()
    return buf[...]


def _allgather_chips_small(name, v):
    r = v.shape[0]

    def body(v_ref, o_ref, send_sems, recv_sems):
        x, y, c, chips = _place()
        me = 2 * x + y
        o_ref[me] = v_ref[...]
        cps = []
        for k, (px, py) in enumerate(chips):
            cp = _remote(v_ref, o_ref.at[me], send_sems.at[k], recv_sems.at[k], (px, py, c))
            cp.start()
            cps.append(cp)
        for k, (px, py) in enumerate(chips):
            slot = o_ref.at[2 * px + py]
            _remote(slot, slot, send_sems.at[k], recv_sems.at[k], (px, py, c)).wait_recv()
        for cp in cps:
            cp.wait_send()

    return pl.pallas_call(
        body, name=name, in_specs=[pl.BlockSpec(memory_space=pltpu.VMEM)],
        out_specs=pl.BlockSpec(memory_space=pltpu.VMEM),
        out_shape=jax.ShapeDtypeStruct((N_CHIPS, r, LANES), F32),
        scratch_shapes=[pltpu.SemaphoreType.DMA((3,)), pltpu.SemaphoreType.DMA((3,))],
    )(v)


def _allreduce_small(name, v):
    r = v.shape[0]
    hr = r // 2
    assert hr % SUBLANES == 0

    def body(v_ref, o_ref, sib_ref, chips_ref, send_sems, recv_sems):
        x, y, c, chips = _place()
        me = 2 * x + y
        sibling = (x, y, 1 - c)
        first = _remote(v_ref, sib_ref, send_sems.at[0], recv_sems.at[0], sibling)
        first.start()
        first.wait()
        mine = pl.ds(pl.multiple_of(c * hr, SUBLANES), hr)
        chips_ref[me] = v_ref[mine, :] + sib_ref[mine, :]
        cps = []
        for k, (px, py) in enumerate(chips):
            cp = _remote(chips_ref.at[me], chips_ref.at[me], send_sems.at[1 + k], recv_sems.at[1 + k], (px, py, c))
            cp.start()
            cps.append(cp)
        for k, (px, py) in enumerate(chips):
            slot = chips_ref.at[2 * px + py]
            _remote(slot, slot, send_sems.at[1 + k], recv_sems.at[1 + k], (px, py, c)).wait_recv()
        total = chips_ref[0]
        for j in range(1, N_CHIPS):
            total = total + chips_ref[j]
        o_ref[mine, :] = total
        last = _remote(o_ref.at[mine], o_ref.at[mine], send_sems.at[4], recv_sems.at[4], sibling)
        last.start()
        other = o_ref.at[pl.ds(pl.multiple_of((1 - c) * hr, SUBLANES), hr)]
        _remote(other, other, send_sems.at[4], recv_sems.at[4], sibling).wait_recv()
        last.wait_send()
        for cp in cps:
            cp.wait_send()

    return pl.pallas_call(
        body, name=name, in_specs=[pl.BlockSpec(memory_space=pltpu.VMEM)],
        out_specs=pl.BlockSpec(memory_space=pltpu.VMEM),
        out_shape=jax.ShapeDtypeStruct((r, LANES), F32),
        scratch_shapes=[pltpu.VMEM((r, LANES), F32), pltpu.VMEM((N_CHIPS, hr, LANES), F32),
                        pltpu.SemaphoreType.DMA((5,)), pltpu.SemaphoreType.DMA((5,))],
    )(v)


def _add_sibling(name, slabs, recv, c):
    _, r, cols = slabs.shape
    hr = r // 2
    tr = _pick(hr, STREAM_TILE)
    nb = hr // tr

    def body(c_ref, a_ref, b_ref, o_ref):
        o_ref[...] = (a_ref[...].astype(F32) + b_ref[...].astype(F32)).astype(BF16)

    grid_spec = pltpu.PrefetchScalarGridSpec(
        num_scalar_prefetch=1, grid=(N_CHIPS, nb),
        in_specs=[pl.BlockSpec((None, tr, cols), lambda j, i, c_ref: (j, c_ref[0] * nb + i, 0)),
                  pl.BlockSpec((None, tr, cols), lambda j, i, c_ref: (j, i, 0))],
        out_specs=pl.BlockSpec((None, tr, cols), lambda j, i, c_ref: (j, i, 0)))
    return pl.pallas_call(
        body, name=name, grid_spec=grid_spec,
        out_shape=jax.ShapeDtypeStruct((N_CHIPS, hr, cols), BF16),
        compiler_params=_cp(("parallel", "parallel")))(jnp.reshape(c, (1,)).astype(jnp.int32), slabs, recv)


def _sum_chips(name, own, recv, chip, c):
    _, hr, cols = recv.shape
    tr = _pick(hr, STREAM_TILE // 2)
    nb = hr // tr

    def body(sc_ref, own_ref, recv_ref, o_ref):
        total = own_ref[...].astype(F32)
        for k in range(3):
            total = total + recv_ref[k].astype(F32)
        o_ref[...] = total

    grid_spec = pltpu.PrefetchScalarGridSpec(
        num_scalar_prefetch=1, grid=(nb,),
        in_specs=[pl.BlockSpec((None, tr, cols), lambda i, sc: (sc[0], i, 0)),
                  pl.BlockSpec((3, tr, cols), lambda i, sc: (0, i, 0))],
        out_specs=pl.BlockSpec((tr, cols), lambda i, sc: (sc[1] * nb + i, 0)))
    return pl.pallas_call(
        body, name=name, grid_spec=grid_spec, out_shape=jax.ShapeDtypeStruct((2 * hr, cols), F32),
        compiler_params=_cp(("parallel",)))(jnp.stack([chip, c]).astype(jnp.int32), own, recv)


def _adamw_math(w, g, m, v):
    m = ADAM_B1 * m + (1.0 - ADAM_B1) * g
    v = ADAM_B2 * v + (1.0 - ADAM_B2) * (g * g)
    m_hat = m / (1.0 - ADAM_B1 ** ADAM_STEP)
    v_hat = v / (1.0 - ADAM_B2 ** ADAM_STEP)
    delta = -ADAM_LR * (m_hat / (jnp.sqrt(v_hat) + ADAM_EPS) + ADAM_WD * w)
    return delta, m, v


def _adamw(name, w, gs, m, v):
    nl, r, cols = w.shape
    tr = _pick(r, ROW_TILE)

    def body(*refs):
        w_ref, m_ref, v_ref = refs[0:3]
        g_refs = refs[3:3 + nl]
        go_ref, d_ref, nm_ref, nv_ref = refs[3 + nl:]
        layer = pl.program_id(0)
        g = g_refs[0][...]
        for j in range(1, nl):
            g = jnp.where(layer == j, g_refs[j][...], g)
        d, nm, nv = _adamw_math(w_ref[...], g, m_ref[...], v_ref[...])
        go_ref[...] = g
        d_ref[...] = d
        nm_ref[...] = nm
        nv_ref[...] = nv

    spec3 = pl.BlockSpec((None, tr, cols), lambda l, i: (l, i, 0))
    gspec = pl.BlockSpec((tr, cols), lambda l, i: (i, 0))
    out = jax.ShapeDtypeStruct((nl, r, cols), F32)
    return pl.pallas_call(
        body, name=name, grid=(nl, r // tr), in_specs=[spec3] * 3 + [gspec] * nl, out_specs=[spec3] * 4,
        out_shape=[out] * 4, compiler_params=_cp(("parallel", "parallel")))(w, m, v, *gs)


def _adamw_small(name, groups):
    n = len(groups)
    flat = [a for grp in groups for a in grp]

    def body(*refs):
        ins, outs = refs[:4 * n], refs[4 * n:]
        for p in range(n):
            w_ref, g_ref, m_ref, v_ref = ins[4 * p:4 * p + 4]
            d, nm, nv = _adamw_math(w_ref[...], g_ref[...], m_ref[...], v_ref[...])
            outs[3 * p][...] = d
            outs[3 * p + 1][...] = nm
            outs[3 * p + 2][...] = nv

    vm = pl.BlockSpec(memory_space=pltpu.VMEM)
    out_shape = [jax.ShapeDtypeStruct(grp[0].shape, F32) for grp in groups for _ in range(3)]
    res = pl.pallas_call(
        body, name=name, in_specs=[vm] * (4 * n), out_specs=[vm] * (3 * n), out_shape=out_shape)(*flat)
    return [tuple(res[3 * p:3 * p + 3]) for p in range(n)]


def _block_diag_pairs(w):
    h, d, _ = w.shape
    z = jnp.zeros((h // 2, d, d), w.dtype)
    top = jnp.concatenate([w[0::2], z], axis=2)
    bot = jnp.concatenate([z, w[1::2]], axis=2)
    return jnp.concatenate([top, bot], axis=1).astype(BF16)


def _diag_pairs_to_heads(g, d):
    a = g[:, :d, :d]
    b = g[:, d:, d:]
    return jnp.stack([a, b], axis=1).reshape(-1, d, d)


def _rows128(a):
    flat = a.reshape(-1, LANES)
    pad = (-flat.shape[0]) % SUBLANES
    if pad:
        flat = jnp.concatenate([flat, jnp.zeros((pad, LANES), flat.dtype)], axis=0)
    return flat


def _unshard_last(g4, shape):
    g4 = g4.reshape((N_CHIPS,) + tuple(shape))
    return jnp.concatenate([g4[j] for j in range(N_CHIPS)], axis=-1)


def kernel(x, norm_gains, hyb_w_in, hyb_conv_a, hyb_conv_b, hyb_conv_b_bias, hyb_rg_w_a, hyb_rg_b_a, hyb_rg_w_x, hyb_rg_b_x, hyb_rg_lambda, hyb_w_out, sb_w_qkv, sb_w_o, mlp_w_up, mlp_w_down, loss_target, m_norm_gains, m_hyb_w_in, m_hyb_conv_a, m_hyb_conv_b, m_hyb_conv_b_bias, m_hyb_rg_w_a, m_hyb_rg_b_a, m_hyb_rg_w_x, m_hyb_rg_b_x, m_hyb_rg_lambda, m_hyb_w_out, m_sb_w_qkv, m_sb_w_o, m_mlp_w_up, m_mlp_w_down, v_norm_gains, v_hyb_w_in, v_hyb_conv_a, v_hyb_conv_b, v_hyb_conv_b_bias, v_hyb_rg_w_a, v_hyb_rg_b_a, v_hyb_rg_w_x, v_hyb_rg_b_x, v_hyb_rg_lambda, v_hyb_w_out, v_sb_w_qkv, v_sb_w_o, v_mlp_w_up, v_mlp_w_down):
    cx_ = lax.axis_index("x")
    cy_ = lax.axis_index("y")
    cc_ = lax.axis_index("c")
    chip = 2 * cx_ + cy_

    x0 = x[0]
    target = loss_target[0]
    s, d = x0.shape
    heads = SB_HEADS
    assert d // heads == LANES
    n_rg, hd = hyb_rg_w_a.shape[1], hyb_rg_w_a.shape[2]
    wmix = n_rg * hd
    assert 2 * hd == LANES

    big = {
        "hyb_w_in": (hyb_w_in, 0), "hyb_w_out": (hyb_w_out, 0), "mlp_w_up0": (mlp_w_up, 0),
        "mlp_w_down0": (mlp_w_down, 0), "sb_w_qkv": (sb_w_qkv, 0), "sb_w_o": (sb_w_o, 0),
        "mlp_w_up1": (mlp_w_up, 1), "mlp_w_down1": (mlp_w_down, 1),
    }
    names = list(big)

    ng_s, ca_s, cb_s = norm_gains.reshape(-1, norm_gains.shape[2]), hyb_conv_a[0], hyb_conv_b[0]
    packed = jnp.concatenate([_rows128(ng_s), _rows128(ca_s), _rows128(cb_s)], axis=0)
    gathered = _allgather_chips_small("allgather_small", packed)
    chip_then, gathered = lax.optimization_barrier((chip, gathered))

    slots = [_cast_into_slot("cast_" + k, big[k][0], big[k][1], chip_then) for k in names]
    full = {k: _allgather_async("allgather_" + k, slot, cid) for cid, (k, slot) in enumerate(zip(names, slots))}
    rowsharded = lambda k: full[k].reshape(-1, full[k].shape[2])
    n0 = ng_s.size // LANES
    n1 = n0 + (-n0) % SUBLANES
    m0 = ca_s.size // LANES
    m1 = m0 + (-m0) % SUBLANES
    k0 = cb_s.size // LANES
    gains = _unshard_last(gathered[:, 0:n0], ng_s.shape).reshape(2, 4, 1, d)
    conv_a = _unshard_last(gathered[:, n1:n1 + m0], ca_s.shape)
    conv_b = _unshard_last(gathered[:, n1 + m1:n1 + m1 + k0], cb_s.shape)
    bias, b_a, b_x, lam = hyb_conv_b_bias, hyb_rg_b_a, hyb_rg_b_x, hyb_rg_lambda
    wa_blk = _block_diag_pairs(hyb_rg_w_a[0])
    wx_blk = _block_diag_pairs(hyb_rg_w_x[0])

    relu_sq = lambda acc: (jnp.maximum(acc, 0.0), jnp.square(jnp.maximum(acc, 0.0)))

    h1 = _rms_fwd("rms_pre0", x0, gains[0, 0])
    proj = _mm_fwd_col("proj_in", h1, full["hyb_w_in"])[0]
    ycat, hseq = _mixer_fwd(proj, conv_a, conv_b, bias, wa_blk, b_a, wx_blk, b_x, lam)
    mix0 = _mm_fwd_row("proj_out", ycat, rowsharded("hyb_w_out"))
    x1, h2 = _rms_post("rms_mix0", mix0, gains[0, 1], x0, gains[0, 2])
    u0, a0 = _mm_fwd_col("mlp_up0", h2, full["mlp_w_up0"], (BF16, BF16), relu_sq)
    mlp0 = _mm_fwd_row("mlp_down0", a0, rowsharded("mlp_w_down0"))
    x2, h3 = _rms_post("rms_mlp0", mlp0, gains[0, 3], x1, gains[1, 0])

    qkv = _mm_fwd_col("qkv", h3, full["sb_w_qkv"], (BF16,))[0]
    att, tot = _attn_fwd(qkv, heads)
    mix1 = _mm_fwd_row("attn_out", att, rowsharded("sb_w_o"))
    x3, h4 = _rms_post("rms_mix1", mix1, gains[1, 1], x2, gains[1, 2])
    u1, a1 = _mm_fwd_col("mlp_up1", h4, full["mlp_w_up1"], (BF16, BF16), relu_sq)
    mlp1 = _mm_fwd_row("mlp_down1", a1, rowsharded("mlp_w_down1"))
    dy, dmlp1, dgain_mlp1, loss_local = _last_norm_and_loss("last_norm_loss", mlp1, gains[1, 3], x3, target)

    dgain = [[None] * 4 for _ in range(2)]
    drelu = lambda acc, u: (acc * (2.0 * u.astype(F32)),)
    stage_a, stage_b, gfull = {}, {}, {}

    def tie(main, side):
        return lax.optimization_barrier((main, side))

    def reduce_start(k, slab, main):
        main, slab = tie(main, slab)
        stage_a[k] = _to_sibling_async("grads_to_sibling_" + k, slab)
        return main

    def reduce_to_chips(k, main):
        slab, from_sibling = stage_a.pop(k)
        main, part = tie(main, _add_sibling("grads_add_" + k, slab, from_sibling, cc_))
        stage_b[k] = _to_chips_async("grads_to_chips_" + k, part)
        return main

    def reduce_split_start(k, act, dy, cs, main):
        main, other = tie(main, _mm_wgrad_half(k + "_wgrad_sibling_rows", act, dy, 1 - cc_, cs))
        stage_a[k] = (act, dy, cs, _swap_with_sibling_async("grads_to_sibling_" + k, other))
        return main

    def reduce_split_to_chips(k, main):
        act, dy, cs, from_sibling = stage_a.pop(k)
        main, part = tie(main, _mm_wgrad_half(k + "_wgrad_my_rows", act, dy, cc_, cs, init=from_sibling))
        stage_b[k] = _to_chips_async("grads_to_chips_" + k, part)
        return main

    def after(value, token):
        return tie(value, token)[0]

    def reduce_finish(k, main):
        own, from_chips = stage_b.pop(k)
        main, half = tie(main, _sum_chips("grads_sum_" + k, after(own, main), from_chips, chip, cc_))
        gfull[k] = _join_sibling_async("grads_join_" + k, half)
        return main

    def mlp_bwd(layer, dxo, dmlp, xin, hin, u, a, mix):
        down, up = f"mlp_w_down{layer}", f"mlp_w_up{layer}"
        wd, wu = rowsharded(down), full[up]
        dmlp = reduce_split_start(down, a, dmlp, None, dmlp)
        du = _mm_bwd_row(f"mlp_down{layer}_bwd", dmlp, wd, (BF16,), u, drelu)[0]
        du = reduce_split_start(up, hin, du, wu.shape[2], du)
        du = reduce_split_to_chips(down, du)
        dh = _mm_bwd_col(f"mlp_up{layer}_bwd", du, wu)
        dh = reduce_split_to_chips(up, dh)
        dxm, dgain[layer][2], dmix, dgain[layer][1] = _rms_bwd_pair(
            f"rms_premlp{layer}_mix{layer}_bwd", xin, gains[layer, 2], dh, dxo, mix, gains[layer, 1])
        return dxm, dmix

    dgain[1][3] = dgain_mlp1
    dx3, dmix1 = mlp_bwd(1, dy, dmlp1, x3, h4, u1, a1, mix1)
    dmix1 = reduce_start("sb_w_o", _mm_wgrad_row("attn_out_wgrad", att, dmix1).reshape(N_CHIPS, -1, d), dmix1)
    datt = _mm_bwd_row("attn_out_bwd", dmix1, rowsharded("sb_w_o"), (BF16,))[0]
    dq, dk, dv = _attn_bwd(qkv, tot, datt, heads)
    dqkv = jnp.concatenate([dq, dk, dv], axis=1)
    dqkv = reduce_to_chips("sb_w_o", dqkv)
    dqkv = reduce_finish("mlp_w_down1", dqkv)
    dqkv = reduce_finish("mlp_w_up1", dqkv)
    dqkv = reduce_split_start("sb_w_qkv", h3, dqkv, full["sb_w_qkv"].shape[2], dqkv)
    dh3 = _mm_bwd_col("qkv_bwd", dqkv, full["sb_w_qkv"])
    dh3 = reduce_split_to_chips("sb_w_qkv", dh3)
    dx2, dgain[1][0], dmlp0, dgain[0][3] = _rms_bwd_pair(
        "rms_pre1_mlp0_bwd", x2, gains[1, 0], dh3, dx3, mlp0, gains[0, 3])

    dx1, dmix0 = mlp_bwd(0, dx2, dmlp0, x1, h2, u0, a0, mix0)
    dmix0 = reduce_finish("sb_w_o", dmix0)
    dmix0 = reduce_finish("sb_w_qkv", dmix0)
    dmix0 = reduce_finish("mlp_w_down0", dmix0)
    dmix0 = reduce_start("hyb_w_out", _mm_wgrad_row("proj_out_wgrad", ycat, dmix0).reshape(N_CHIPS, -1, d), dmix0)
    dycat = _mm_bwd_row("proj_out_bwd", dmix0, rowsharded("hyb_w_out"))[0]
    dproj, xr_b, dpa_b, dpx_b, sg = _mixer_bwd(
        proj, hseq, dycat, conv_a, conv_b, bias, wa_blk, b_a, wx_blk, b_x, lam)
    dproj = reduce_finish("mlp_w_up0", dproj)
    dproj = reduce_to_chips("hyb_w_out", dproj)
    dproj = reduce_split_start("hyb_w_in", h1, dproj, full["hyb_w_in"].shape[2], dproj)
    dh1 = _mm_bwd_col("proj_in_bwd", dproj, full["hyb_w_in"])
    dh1 = reduce_split_to_chips("hyb_w_in", dh1)
    dx0, dgain[0][0] = _rms_bwd("rms_pre0_bwd", x0, gains[0, 0], dh1, res=dx1)
    dwa = _diag_pairs_to_heads(_mm_wgrad_diag("rg_w_a_wgrad", xr_b, dpa_b), hd)
    dwx = _diag_pairs_to_heads(_mm_wgrad_diag("rg_w_x_wgrad", xr_b, dpx_b), hd)

    dgains = jnp.concatenate([dgain[l][k] for l in range(2) for k in range(4)], axis=0)
    small_parts = [dgains, sg[_SG_CONV_A:_SG_CONV_A + 3], sg[_SG_CONV_B:_SG_CONV_B + 4], sg[_SG_BIAS:_SG_BIAS + 1],
                   dwa, sg[_SG_BA:_SG_BA + 1], dwx, sg[_SG_BX:_SG_BX + 1], sg[_SG_LAM:_SG_LAM + 1],
                   jnp.broadcast_to(loss_local, (1, LANES))]
    small_rows = [_rows128(p) for p in small_parts]
    n_small = sum(rws.shape[0] for rws in small_rows)
    tail_pad = [jnp.zeros(((-n_small) % (2 * SUBLANES), LANES), F32)] if n_small % (2 * SUBLANES) else []
    reduced = _allreduce_small("allreduce_small", jnp.concatenate(small_rows + tail_pad, axis=0))
    small_full, off = [], 0
    for p, rws in zip(small_parts, small_rows):
        small_full.append(reduced[off:off + p.size // LANES].reshape(p.shape))
        off += rws.shape[0]
    g_gains, g_ca, g_cb, g_bias, g_wa, g_ba, g_wx, g_bx, g_lam, loss_row = small_full
    loss = loss_row[0, 0]

    def my_cols(g, width):
        return lax.dynamic_slice_in_dim(g, chip * width, width, axis=g.ndim - 1)

    small = [
        ("norm_gains", norm_gains, my_cols(g_gains, norm_gains.shape[2]).reshape(norm_gains.shape),
         m_norm_gains, v_norm_gains),
        ("hyb_conv_a", hyb_conv_a, my_cols(g_ca, hyb_conv_a.shape[2])[None], m_hyb_conv_a, v_hyb_conv_a),
        ("hyb_conv_b", hyb_conv_b, my_cols(g_cb, hyb_conv_b.shape[2])[None], m_hyb_conv_b, v_hyb_conv_b),
        ("hyb_conv_b_bias", hyb_conv_b_bias, g_bias, m_hyb_conv_b_bias, v_hyb_conv_b_bias),
        ("hyb_rg_w_a", hyb_rg_w_a, g_wa[None], m_hyb_rg_w_a, v_hyb_rg_w_a),
        ("hyb_rg_b_a", hyb_rg_b_a, g_ba, m_hyb_rg_b_a, v_hyb_rg_b_a),
        ("hyb_rg_w_x", hyb_rg_w_x, g_wx[None], m_hyb_rg_w_x, v_hyb_rg_w_x),
        ("hyb_rg_b_x", hyb_rg_b_x, g_bx, m_hyb_rg_b_x, v_hyb_rg_b_x),
        ("hyb_rg_lambda", hyb_rg_lambda, g_lam, m_hyb_rg_lambda, v_hyb_rg_lambda),
    ]
    to2d = lambda a: a.reshape(-1, a.shape[-1])
    small_res = _adamw_small("adamw_small", [tuple(to2d(a) for a in (w, g, m, v)) for _, w, g, m, v in small])
    out = {}
    for (nm, w, g, _, _), (dl, nmom, nvar) in zip(small, small_res):
        out[nm] = (g, dl.reshape(w.shape), nmom.reshape(w.shape), nvar.reshape(w.shape))

    stacked = {
        "mlp_w_down": (mlp_w_down, m_mlp_w_down, v_mlp_w_down, ["mlp_w_down0", "mlp_w_down1"]),
        "mlp_w_up": (mlp_w_up, m_mlp_w_up, v_mlp_w_up, ["mlp_w_up0", "mlp_w_up1"]),
        "sb_w_o": (sb_w_o, m_sb_w_o, v_sb_w_o, ["sb_w_o"]),
        "sb_w_qkv": (sb_w_qkv, m_sb_w_qkv, v_sb_w_qkv, ["sb_w_qkv"]),
        "hyb_w_out": (hyb_w_out, m_hyb_w_out, v_hyb_w_out, ["hyb_w_out"]),
        "hyb_w_in": (hyb_w_in, m_hyb_w_in, v_hyb_w_in, ["hyb_w_in"]),
    }

    def update(k, token):
        w, m, v, parts = stacked[k]
        out[k] = tuple(_adamw("adamw_" + k, w, [after(gfull[p], token) for p in parts], m, v))
        return out[k][1]

    token = small_res[0][0]
    token = update("sb_w_qkv", token)
    token = update("sb_w_o", token)
    token = update("mlp_w_down", token)
    token = reduce_finish("hyb_w_out", token)
    token = update("mlp_w_up", token)
    token = reduce_finish("hyb_w_in", token)
    token = update("hyb_w_out", token)
    update("hyb_w_in", token)

    order = ["norm_gains", "hyb_w_in", "hyb_conv_a", "hyb_conv_b", "hyb_conv_b_bias", "hyb_rg_w_a", "hyb_rg_b_a",
             "hyb_rg_w_x", "hyb_rg_b_x", "hyb_rg_lambda", "hyb_w_out", "sb_w_qkv", "sb_w_o", "mlp_w_up",
             "mlp_w_down"]
    return (loss, dx0[None], *[out[k][0] for k in order], *[out[k][1] for k in order],
            *[out[k][2] for k in order], *[out[k][3] for k in order])
```

```python
import functools
import math

import jax
import jax.numpy as jnp
from jax import lax
from jax.experimental import pallas as pl
from jax.experimental.pallas import tpu as pltpu
from jax.experimental.pallas import tpu_sc as plsc

F32 = jnp.float32
BF16 = jnp.bfloat16
MESH = pl.DeviceIdType.MESH

SB_HEADS = 16
NORM_EPS = 1e-6
LRU_C = 8.0
ADAM_LR = 0.001
ADAM_B1 = 0.9
ADAM_B2 = 0.999
ADAM_EPS = 1e-08
ADAM_WD = 0.01
ADAM_STEP = 10

LANES = 128
SUBLANES = 8
VMEM_LIMIT = 48 * 1024 * 1024
MM_TILE = 1024
MM_VMEM_BUDGET = 40 * 1024 * 1024
MM_TILE_N = 1280
MM_TILE_K = 2048
ROW_TILE = 256
STREAM_TILE = 1024
ATT_TILE = 512
ATT_HEADS_PER_STEP = 2
ATT_FWD_HEADS_PER_STEP = 4
N_CHIPS = 4
COLLECTIVE_SIBLING = 8
COLLECTIVE_CHIPS = 9

_DIMS = {
    "nn": (((1,), (0,)), ((), ())),
    "nt": (((1,), (1,)), ((), ())),
    "tn": (((0,), (0,)), ((), ())),
}


def _cp(sem=None, vmem=VMEM_LIMIT):
    return pltpu.CompilerParams(dimension_semantics=sem, vmem_limit_bytes=vmem)


def _pick(dim, pref):
    t = min(dim, pref)
    while dim % t:
        t -= LANES
    return t


def _whole(shape):
    nd = len(shape)
    return pl.BlockSpec(tuple(shape), lambda *_: (0,) * nd)


def _sigmoid(z):
    return 1.0 / (1.0 + jnp.exp(-z))


def _log_sigmoid(z):
    return jnp.minimum(z, 0.0) - jnp.log(1.0 + jnp.exp(-jnp.abs(z)))


def _expm1(z):
    series = z * (1.0 + z * (0.5 + z * (1.0 / 6.0 + z * (1.0 / 24.0))))
    return jnp.where(jnp.abs(z) < 0.05, series, jnp.exp(z) - 1.0)


_GELU_C = math.sqrt(2.0 / math.pi)


def _gelu_and_grad(g):
    inner = _GELU_C * (g + 0.044715 * g * g * g)
    t = jnp.tanh(inner)
    val = 0.5 * g * (1.0 + t)
    grad = 0.5 * (1.0 + t) + 0.5 * g * (1.0 - t * t) * _GELU_C * (1.0 + 3.0 * 0.044715 * g * g)
    return val, grad


def _shift_down(cur, prev8, k, rows):
    n = cur.shape[0]
    rolled = pltpu.roll(cur, k, 0)
    head = jnp.tile(pltpu.roll(prev8, k, 0), (n // SUBLANES, 1))
    return jnp.where(rows < k, head, rolled)


def _shift_up(cur, next8, k, rows):
    n = cur.shape[0]
    rolled = pltpu.roll(cur, n - k, 0)
    tail = jnp.tile(pltpu.roll(next8, SUBLANES - k, 0), (n // SUBLANES, 1))
    return jnp.where(rows >= n - k, tail, rolled)


def _colsum(v):
    return jnp.sum(v, axis=0, keepdims=True)


def _matmul(name, mode, grid, operands, in_specs, out_shapes, out_specs, acc_shape, epilogue=None):
    nk = grid[2]
    n_in = len(operands)
    dims = _DIMS[mode]

    def finish(acc, extra, outs):
        res = epilogue(acc, *[e[...] for e in extra]) if epilogue is not None else (acc,)
        for o_ref, o in zip(outs, res):
            o_ref[...] = o.astype(o_ref.dtype)

    def product(a_ref, b_ref):
        return lax.dot_general(a_ref[...].astype(BF16), b_ref[...].astype(BF16), dims, preferred_element_type=F32)

    def body_single(*refs):
        finish(product(refs[0], refs[1]), refs[2:n_in], refs[n_in:])

    def body(*refs):
        extra = refs[2:n_in]
        outs = refs[n_in:-1]
        acc_ref = refs[-1]
        k = pl.program_id(2)

        @pl.when(k == 0)
        def _():
            acc_ref[...] = product(refs[0], refs[1])

        @pl.when(k > 0)
        def _():
            acc_ref[...] += product(refs[0], refs[1])

        @pl.when(k == nk - 1)
        def _():
            finish(acc_ref[...], extra, outs)

    return pl.pallas_call(
        body_single if nk == 1 else body, name=name, grid=grid, in_specs=in_specs, out_specs=out_specs,
        out_shape=out_shapes, scratch_shapes=[] if nk == 1 else [pltpu.VMEM(acc_shape, F32)],
        compiler_params=_cp(("parallel", "parallel", "arbitrary")),
    )(*operands)


def _pick_m(m, tk, tn, a_dtype, b_dtype, out_dtypes, extra_dtypes=()):
    size = lambda dt: jnp.dtype(dt).itemsize
    per_row = 2 * tk * size(a_dtype) + tn * (2 * sum(size(dt) for dt in tuple(out_dtypes) + tuple(extra_dtypes)) + 4)
    fixed = 2 * tk * tn * size(b_dtype)
    tm = _pick(m, MM_TILE)
    while tm > LANES and tm * per_row + fixed > MM_VMEM_BUDGET:
        tm = _pick(m, tm // 2)
    return tm


def _mm_fwd_col(name, a, wfull, out_dtypes=(F32,), epilogue=None):
    s, kdim = a.shape
    _, _, cs = wfull.shape
    tk, tn = _pick(kdim, MM_TILE_K), _pick(cs, MM_TILE_N)
    tm = _pick_m(s, tk, tn, a.dtype, wfull.dtype, out_dtypes)
    nbj = cs // tn
    grid = (s // tm, N_CHIPS * nbj, kdim // tk)
    out_shapes = [jax.ShapeDtypeStruct((s, N_CHIPS * cs), dt) for dt in out_dtypes]
    out_specs = [pl.BlockSpec((tm, tn), lambda i, n, k: (i, n)) for _ in out_dtypes]
    return _matmul(
        name, "nn", grid, [a, wfull],
        [pl.BlockSpec((tm, tk), lambda i, n, k: (i, k)),
         pl.BlockSpec((None, tk, tn), lambda i, n, k: (n // nbj, k, n % nbj))],
        out_shapes, out_specs, (tm, tn), epilogue)


def _mm_fwd_row(name, a, w2d, out_dtype=BF16):
    s, kdim = a.shape
    _, n_out = w2d.shape
    tk, tn = _pick(kdim, MM_TILE_K), _pick(n_out, MM_TILE)
    tm = _pick_m(s, tk, tn, a.dtype, w2d.dtype, (out_dtype,))
    grid = (s // tm, n_out // tn, kdim // tk)
    return _matmul(
        name, "nn", grid, [a, w2d],
        [pl.BlockSpec((tm, tk), lambda i, n, k: (i, k)),
         pl.BlockSpec((tk, tn), lambda i, n, k: (k, n))],
        [jax.ShapeDtypeStruct((s, n_out), out_dtype)],
        [pl.BlockSpec((tm, tn), lambda i, n, k: (i, n))], (tm, tn))[0]


def _mm_bwd_col(name, dy, wfull, out_dtype=BF16):
    s, _ = dy.shape
    _, kdim, cs = wfull.shape
    tn, tk = _pick(kdim, MM_TILE), _pick(cs, MM_TILE_K)
    tm = _pick_m(s, tk, tn, dy.dtype, wfull.dtype, (out_dtype,))
    nbj = cs // tk
    grid = (s // tm, kdim // tn, N_CHIPS * nbj)
    return _matmul(
        name, "nt", grid, [dy, wfull],
        [pl.BlockSpec((tm, tk), lambda i, n, k: (i, k)),
         pl.BlockSpec((None, tn, tk), lambda i, n, k: (k // nbj, n, k % nbj))],
        [jax.ShapeDtypeStruct((s, kdim), out_dtype)],
        [pl.BlockSpec((tm, tn), lambda i, n, k: (i, n))], (tm, tn))[0]


def _mm_bwd_row(name, dy, w2d, out_dtypes=(F32,), extra=None, epilogue=None):
    s, n_in = dy.shape
    kdim, _ = w2d.shape
    tn, tk = _pick(kdim, MM_TILE), _pick(n_in, MM_TILE_K)
    tm = _pick_m(s, tk, tn, dy.dtype, w2d.dtype, out_dtypes, () if extra is None else (extra.dtype,))
    grid = (s // tm, kdim // tn, n_in // tk)
    operands = [dy, w2d]
    in_specs = [pl.BlockSpec((tm, tk), lambda i, n, k: (i, k)),
                pl.BlockSpec((tn, tk), lambda i, n, k: (n, k))]
    if extra is not None:
        operands.append(extra)
        in_specs.append(pl.BlockSpec((tm, tn), lambda i, n, k: (i, n)))
    return _matmul(
        name, "nt", grid, operands, in_specs,
        [jax.ShapeDtypeStruct((s, kdim), dt) for dt in out_dtypes],
        [pl.BlockSpec((tm, tn), lambda i, n, k: (i, n)) for _ in out_dtypes], (tm, tn), epilogue)


def _mm_wgrad_row(name, a, dy):
    s, kdim = a.shape
    _, n_out = dy.shape
    tn, ts = _pick(n_out, MM_TILE), _pick(s, MM_TILE_K)
    tm = _pick_m(kdim, ts, tn, a.dtype, dy.dtype, (BF16,))
    grid = (kdim // tm, n_out // tn, s // ts)
    return _matmul(
        name, "tn", grid, [a, dy],
        [pl.BlockSpec((ts, tm), lambda i, n, k: (k, i)),
         pl.BlockSpec((ts, tn), lambda i, n, k: (k, n))],
        [jax.ShapeDtypeStruct((kdim, n_out), BF16)],
        [pl.BlockSpec((tm, tn), lambda i, n, k: (i, n))], (tm, tn))[0]


def _mm_wgrad_half(name, a, dy, half, cs=None, init=None):
    s, kdim = a.shape
    ts = _pick(s, MM_TILE_K)
    nk = s // ts
    if cs is not None:
        hr, cols = kdim // 2, cs
        tn = _pick(cs, MM_TILE_N)
        tm = _pick_m(hr, ts, tn, a.dtype, dy.dtype, (BF16,), (BF16,))
        ni, nbj = hr // tm, cs // tn
        grid = (ni, N_CHIPS * nbj, nk)
        a_map = lambda i, n, k, h: (k, h[0] * ni + i)
        o_map = lambda i, n, k, h: (n // nbj, i, n % nbj)
    else:
        hr, cols = kdim // N_CHIPS // 2, dy.shape[1]
        tn = _pick(cols, MM_TILE)
        tm = _pick_m(hr, ts, tn, a.dtype, dy.dtype, (BF16,), (BF16,))
        ni = hr // tm
        grid = (N_CHIPS * ni, cols // tn, nk)
        a_map = lambda i, n, k, h: (k, (i // ni) * 2 * ni + h[0] * ni + i % ni)
        o_map = lambda i, n, k, h: (i // ni, i % ni, n)
    with_init = init is not None

    def body(*refs):
        a_ref, b_ref = refs[1], refs[2]
        init_ref = refs[3] if with_init else None
        o_ref, acc_ref = refs[-2], refs[-1]
        k = pl.program_id(2)

        def product():
            return lax.dot_general(a_ref[...].astype(BF16), b_ref[...].astype(BF16), _DIMS["tn"],
                                   preferred_element_type=F32)

        @pl.when(k == 0)
        def _():
            if with_init:
                acc_ref[...] = init_ref[...].astype(F32)
                acc_ref[...] += product()
            else:
                acc_ref[...] = product()

        @pl.when(k > 0)
        def _():
            acc_ref[...] += product()

        @pl.when(k == nk - 1)
        def _():
            o_ref[...] = acc_ref[...].astype(BF16)

    oblk = pl.BlockSpec((None, tm, tn), o_map)
    grid_spec = pltpu.PrefetchScalarGridSpec(
        num_scalar_prefetch=1, grid=grid,
        in_specs=[pl.BlockSpec((ts, tm), a_map), pl.BlockSpec((ts, tn), lambda i, n, k, h: (k, n))]
        + ([oblk] if with_init else []),
        out_specs=oblk, scratch_shapes=[pltpu.VMEM((tm, tn), F32)])
    operands = [jnp.reshape(half, (1,)).astype(jnp.int32), a, dy] + ([init] if with_init else [])
    return pl.pallas_call(
        body, name=name, grid_spec=grid_spec, out_shape=jax.ShapeDtypeStruct((N_CHIPS, hr, cols), BF16),
        compiler_params=_cp(("parallel", "parallel", "arbitrary")))(*operands)


def _mm_wgrad_diag(name, a, dy):
    s, width = a.shape
    nb = width // LANES
    ts = _pick(s, MM_TILE)
    grid = (nb, 1, s // ts)
    return _matmul(
        name, "tn", grid, [a, dy],
        [pl.BlockSpec((ts, LANES), lambda i, n, k: (k, i)),
         pl.BlockSpec((ts, LANES), lambda i, n, k: (k, i))],
        [jax.ShapeDtypeStruct((nb, LANES, LANES), F32)],
        [pl.BlockSpec((None, LANES, LANES), lambda i, n, k: (i, 0, 0))], (LANES, LANES))[0]


def _rowspec(tr, d):
    return pl.BlockSpec((tr, d), lambda i: (i, 0))


def _vecspec(d):
    return pl.BlockSpec((1, d), lambda i: (0, 0))


def _rms(x, g):
    return x * lax.rsqrt(jnp.mean(x * x, axis=-1, keepdims=True) + NORM_EPS) * g


def _cast_into_slot(name, w, layer, chip):
    _, r, c = w.shape
    tr = _pick(r, STREAM_TILE)

    def body(chip_ref, w_ref, o_ref):
        o_ref[...] = w_ref[...].astype(BF16)

    grid_spec = pltpu.PrefetchScalarGridSpec(
        num_scalar_prefetch=1, grid=(r // tr,),
        in_specs=[pl.BlockSpec((None, tr, c), lambda i, chip_ref: (layer, i, 0))],
        out_specs=pl.BlockSpec((None, tr, c), lambda i, chip_ref: (chip_ref[0], i, 0)))
    return pl.pallas_call(
        body, name=name, grid_spec=grid_spec, out_shape=jax.ShapeDtypeStruct((N_CHIPS, r, c), BF16),
        compiler_params=_cp(("parallel",)))(jnp.reshape(chip, (1,)).astype(jnp.int32), w)


def _rms_fwd(name, x, g):
    s, d = x.shape
    tr = _pick(s, ROW_TILE)

    def body(x_ref, g_ref, h_ref):
        h_ref[...] = _rms(x_ref[...], g_ref[...]).astype(BF16)

    return pl.pallas_call(
        body, name=name, grid=(s // tr,), in_specs=[_rowspec(tr, d), _vecspec(d)],
        out_specs=_rowspec(tr, d), out_shape=jax.ShapeDtypeStruct((s, d), BF16),
        compiler_params=_cp(("parallel",)))(x, g)


def _rms_post(name, y, g_post, res, g_next=None):
    s, d = y.shape
    tr = _pick(s, ROW_TILE)
    with_next = g_next is not None

    def body(*refs):
        if with_next:
            y_ref, gp_ref, r_ref, gn_ref, x_ref, h_ref = refs
        else:
            y_ref, gp_ref, r_ref, x_ref = refs
        xn = r_ref[...] + _rms(y_ref[...].astype(F32), gp_ref[...])
        x_ref[...] = xn
        if with_next:
            h_ref[...] = _rms(xn, gn_ref[...]).astype(BF16)

    operands = [y, g_post, res] + ([g_next] if with_next else [])
    in_specs = [_rowspec(tr, d), _vecspec(d), _rowspec(tr, d)] + ([_vecspec(d)] if with_next else [])
    out_shape = [jax.ShapeDtypeStruct((s, d), F32)] + ([jax.ShapeDtypeStruct((s, d), BF16)] if with_next else [])
    out_specs = [_rowspec(tr, d)] + ([_rowspec(tr, d)] if with_next else [])
    return pl.pallas_call(
        body, name=name, grid=(s // tr,), in_specs=in_specs, out_specs=out_specs, out_shape=out_shape,
        compiler_params=_cp(("parallel",)))(*operands)


def _rms_bwd(name, x, g, dy, res=None, out_dtype=F32):
    s, d = x.shape
    tr = _pick(s, ROW_TILE)
    nsteps = s // tr
    with_res = res is not None

    def body(*refs):
        if with_res:
            x_ref, g_ref, dy_ref, r_ref, dx_ref, dg_ref, acc_ref = refs
        else:
            x_ref, g_ref, dy_ref, dx_ref, dg_ref, acc_ref = refs
        i = pl.program_id(0)

        @pl.when(i == 0)
        def _():
            acc_ref[...] = jnp.zeros_like(acc_ref)

        xv = x_ref[...]
        dyv = dy_ref[...].astype(F32)
        r = lax.rsqrt(jnp.mean(xv * xv, axis=-1, keepdims=True) + NORM_EPS)
        xhat = xv * r
        gy = dyv * g_ref[...]
        dx = r * (gy - xhat * jnp.mean(gy * xhat, axis=-1, keepdims=True))
        if with_res:
            dx = dx + r_ref[...]
        dx_ref[...] = dx.astype(dx_ref.dtype)
        acc_ref[...] += jnp.sum((dyv * xhat).reshape(tr // SUBLANES, SUBLANES, d), axis=0)

        @pl.when(i == nsteps - 1)
        def _():
            dg_ref[...] = jnp.broadcast_to(_colsum(acc_ref[...]), (SUBLANES, d))

    operands = [x, g, dy] + ([res] if with_res else [])
    in_specs = [_rowspec(tr, d), _vecspec(d), _rowspec(tr, d)] + ([_rowspec(tr, d)] if with_res else [])
    dx, dg = pl.pallas_call(
        body, name=name, grid=(nsteps,), in_specs=in_specs,
        out_specs=[_rowspec(tr, d), pl.BlockSpec((SUBLANES, d), lambda i: (0, 0))],
        out_shape=[jax.ShapeDtypeStruct((s, d), out_dtype), jax.ShapeDtypeStruct((SUBLANES, d), F32)],
        scratch_shapes=[pltpu.VMEM((SUBLANES, d), F32)],
        compiler_params=_cp(("arbitrary",)))(*operands)
    return dx, dg[0:1]


def _rms_bwd_pair(name, x, g, dy, res, y2, g2):
    s, d = x.shape
    tr = _pick(s, ROW_TILE)
    nsteps = s // tr

    def through(xv, gv, dyv):
        r = lax.rsqrt(jnp.mean(xv * xv, axis=-1, keepdims=True) + NORM_EPS)
        xhat = xv * r
        gy = dyv * gv
        dx = r * (gy - xhat * jnp.mean(gy * xhat, axis=-1, keepdims=True))
        return dx, jnp.sum((dyv * xhat).reshape(tr // SUBLANES, SUBLANES, d), axis=0)

    def body(x_ref, g_ref, dy_ref, r_ref, y2_ref, g2_ref, dx_ref, d2_ref, dg_ref, dg2_ref, acc_ref, acc2_ref):
        i = pl.program_id(0)

        @pl.when(i == 0)
        def _():
            acc_ref[...] = jnp.zeros_like(acc_ref)
            acc2_ref[...] = jnp.zeros_like(acc2_ref)

        dx, part = through(x_ref[...], g_ref[...], dy_ref[...].astype(F32))
        dx = dx + r_ref[...]
        dx_ref[...] = dx
        acc_ref[...] += part
        d2, part2 = through(y2_ref[...].astype(F32), g2_ref[...], dx)
        d2_ref[...] = d2.astype(d2_ref.dtype)
        acc2_ref[...] += part2

        @pl.when(i == nsteps - 1)
        def _():
            dg_ref[...] = jnp.broadcast_to(_colsum(acc_ref[...]), (SUBLANES, d))
            dg2_ref[...] = jnp.broadcast_to(_colsum(acc2_ref[...]), (SUBLANES, d))

    row, vec = _rowspec(tr, d), _vecspec(d)
    gspec = pl.BlockSpec((SUBLANES, d), lambda i: (0, 0))
    dx, d2, dg, dg2 = pl.pallas_call(
        body, name=name, grid=(nsteps,), in_specs=[row, vec, row, row, row, vec],
        out_specs=[row, row, gspec, gspec],
        out_shape=[jax.ShapeDtypeStruct((s, d), F32), jax.ShapeDtypeStruct((s, d), BF16),
                   jax.ShapeDtypeStruct((SUBLANES, d), F32), jax.ShapeDtypeStruct((SUBLANES, d), F32)],
        scratch_shapes=[pltpu.VMEM((SUBLANES, d), F32), pltpu.VMEM((SUBLANES, d), F32)],
        compiler_params=_cp(("arbitrary",)))(x, g, dy, res, y2, g2)
    return dx, dg[0:1], d2, dg2[0:1]


def _last_norm_and_loss(name, y, g, res, target):
    s, d = y.shape
    tr = _pick(s, ROW_TILE)
    nsteps = s // tr

    def body(y_ref, g_ref, r_ref, t_ref, dx_ref, dy_ref, dg_ref, l_ref, acc_ref, lacc_ref):
        i = pl.program_id(0)

        @pl.when(i == 0)
        def _():
            acc_ref[...] = jnp.zeros_like(acc_ref)
            lacc_ref[...] = jnp.zeros_like(lacc_ref)

        yv = y_ref[...].astype(F32)
        gv = g_ref[...]
        r = lax.rsqrt(jnp.mean(yv * yv, axis=-1, keepdims=True) + NORM_EPS)
        yhat = yv * r
        err = r_ref[...] + yhat * gv - t_ref[...]
        dx = err * (1.0 / d)
        dx_ref[...] = dx
        lacc_ref[...] += jnp.sum((err * err).reshape(tr // SUBLANES, SUBLANES, d), axis=0)
        gy = dx * gv
        dy_ref[...] = (r * (gy - yhat * jnp.mean(gy * yhat, axis=-1, keepdims=True))).astype(dy_ref.dtype)
        acc_ref[...] += jnp.sum((dx * yhat).reshape(tr // SUBLANES, SUBLANES, d), axis=0)

        @pl.when(i == nsteps - 1)
        def _():
            dg_ref[...] = jnp.broadcast_to(_colsum(acc_ref[...]), (SUBLANES, d))
            l_ref[...] = jnp.full((SUBLANES, LANES), (0.5 / d) * jnp.sum(lacc_ref[...]), F32)

    dx, dy, dg, l = pl.pallas_call(
        body, name=name, grid=(nsteps,),
        in_specs=[_rowspec(tr, d), _vecspec(d), _rowspec(tr, d), _rowspec(tr, d)],
        out_specs=[_rowspec(tr, d), _rowspec(tr, d), pl.BlockSpec((SUBLANES, d), lambda i: (0, 0)),
                   pl.BlockSpec((SUBLANES, LANES), lambda i: (0, 0))],
        out_shape=[jax.ShapeDtypeStruct((s, d), F32), jax.ShapeDtypeStruct((s, d), BF16),
                   jax.ShapeDtypeStruct((SUBLANES, d), F32), jax.ShapeDtypeStruct((SUBLANES, LANES), F32)],
        scratch_shapes=[pltpu.VMEM((SUBLANES, d), F32), pltpu.VMEM((SUBLANES, d), F32)],
        compiler_params=_cp(("arbitrary",)))(y, g, res, target)
    return dx, dy, dg[0:1], l[0, 0]


def _gates(xr, wa, ba, wx, bx, lam):
    xb = xr.astype(BF16)
    r = _sigmoid(jnp.dot(xb, wa, preferred_element_type=F32) + ba)
    i = _sigmoid(jnp.dot(xb, wx, preferred_element_type=F32) + bx)
    log_a = LRU_C * r * _log_sigmoid(lam)
    a = jnp.exp(log_a)
    m = jnp.sqrt(-_expm1(2.0 * log_a))
    return r, i, a, m


def _mixer_fwd(proj, conv_a, conv_b, bias, wa_blk, ba, wx_blk, bx, lam):
    s, w5 = proj.shape
    w = w5 // 5
    nch = w // LANES
    ts = _pick(s, ROW_TILE)
    nt = s // ts

    def body(p_ref, pp_ref, ca_ref, cb_ref, bias_ref, wa_ref, ba_ref, wx_ref, bx_ref, lam_ref,
             y_ref, h_ref, a_scr, b_scr, hc_scr):
        t = pl.program_id(0)
        first = t == 0
        rows = lax.broadcasted_iota(jnp.int32, (ts, LANES), 0)

        @pl.when(first)
        def _():
            hc_scr[...] = jnp.zeros_like(hc_scr)

        def cur(comp, c):
            return p_ref[:, comp * w + c * LANES:comp * w + (c + 1) * LANES]

        def prev(comp, c):
            v = pp_ref[:, comp * w + c * LANES:comp * w + (c + 1) * LANES]
            return jnp.where(first, 0.0, v)

        for c in range(nch):
            sl = slice(c * LANES, (c + 1) * LANES)
            cx = cur(1, c) * cur(2, c)
            cxp = prev(1, c) * prev(2, c)
            wa3 = ca_ref[:, sl]
            conv = (wa3[2:3] * cx + wa3[1:2] * _shift_down(cx, cxp, 1, rows)
                    + wa3[0:1] * _shift_down(cx, cxp, 2, rows))
            y_ref[:, sl] = (cur(0, c) * conv).astype(BF16)

        for c in range(nch):
            sl = slice(c * LANES, (c + 1) * LANES)
            xb, xbp = cur(4, c), prev(4, c)
            wb4 = cb_ref[:, sl]
            xr = (wb4[3:4] * xb + wb4[2:3] * _shift_down(xb, xbp, 1, rows)
                  + wb4[1:2] * _shift_down(xb, xbp, 2, rows)
                  + wb4[0:1] * _shift_down(xb, xbp, 3, rows) + bias_ref[:, sl])
            _, i, a, m = _gates(xr, wa_ref[c], ba_ref[:, sl], wx_ref[c], bx_ref[:, sl], lam_ref[:, sl])
            a_scr[:, sl] = a
            b_scr[:, sl] = m * i * xr

        def step(r, h):
            h = a_scr[pl.ds(r, 1), :] * h + b_scr[pl.ds(r, 1), :]
            h_ref[pl.ds(r, 1), :] = h
            return h

        hc_scr[0:1, :] = lax.fori_loop(0, ts, step, hc_scr[0:1, :], unroll=8)

        for c in range(nch):
            sl = slice(c * LANES, (c + 1) * LANES)
            gel, _ = _gelu_and_grad(cur(3, c))
            y_ref[:, w + c * LANES:w + (c + 1) * LANES] = (h_ref[:, sl] * gel).astype(BF16)

    vec = lambda n: _whole((n, w))
    return pl.pallas_call(
        body, name="mixer_fwd", grid=(nt,),
        in_specs=[pl.BlockSpec((ts, w5), lambda t: (t, 0)),
                  pl.BlockSpec((SUBLANES, w5), lambda t: (jnp.maximum(t * (ts // SUBLANES) - 1, 0), 0)),
                  vec(3), vec(4), vec(1), _whole(wa_blk.shape), vec(1), _whole(wx_blk.shape), vec(1), vec(1)],
        out_specs=[pl.BlockSpec((ts, 2 * w), lambda t: (t, 0)), pl.BlockSpec((ts, w), lambda t: (t, 0))],
        out_shape=[jax.ShapeDtypeStruct((s, 2 * w), BF16), jax.ShapeDtypeStruct((s, w), F32)],
        scratch_shapes=[pltpu.VMEM((ts, w), F32), pltpu.VMEM((ts, w), F32), pltpu.VMEM((SUBLANES, w), F32)],
        compiler_params=_cp(("arbitrary",)),
    )(proj, proj, conv_a, conv_b, bias, wa_blk, ba, wx_blk, bx, lam)


_SG_CONV_A, _SG_CONV_B, _SG_BIAS, _SG_BA, _SG_BX, _SG_LAM, _SG_ROWS = 0, 3, 7, 8, 9, 10, 16


def _mixer_bwd(proj, hseq, dy, conv_a, conv_b, bias, wa_blk, ba, wx_blk, bx, lam):
    s, w5 = proj.shape
    w = w5 // 5
    nch = w // LANES
    ts = _pick(s, ROW_TILE)
    nt = s // ts
    tpb = ts // SUBLANES

    def body(p_ref, pp_ref, h_ref, hp_ref, dy_ref, ca_ref, cb_ref, bias_ref, wa_ref, ba_ref, wx_ref, bx_ref,
             lam_ref, dp_ref, xr_ref, dpa_ref, dpx_ref, sg_ref,
             a_scr, g_scr, l_scr, x_scr, r_scr, i_scr, m_scr, cl_scr, cdc_scr, cdx_scr):
        pid = pl.program_id(0)
        last = pid == 0
        first = pid == nt - 1
        rows = lax.broadcasted_iota(jnp.int32, (ts, LANES), 0)

        @pl.when(last)
        def _():
            sg_ref[...] = jnp.zeros_like(sg_ref)
            cl_scr[...] = jnp.zeros_like(cl_scr)
            cdc_scr[...] = jnp.zeros_like(cdc_scr)
            cdx_scr[...] = jnp.zeros_like(cdx_scr)

        def cur(comp, c):
            return p_ref[:, comp * w + c * LANES:comp * w + (c + 1) * LANES]

        def prev(comp, c):
            v = pp_ref[:, comp * w + c * LANES:comp * w + (c + 1) * LANES]
            return jnp.where(first, 0.0, v)

        def put(comp, c, v):
            dp_ref[:, comp * w + c * LANES:comp * w + (c + 1) * LANES] = v.astype(dp_ref.dtype)

        def acc(row, sl, v):
            sg_ref[row:row + 1, sl] += _colsum(v)

        for c in range(nch):
            sl = slice(c * LANES, (c + 1) * LANES)
            bg, cg, ax = cur(0, c), cur(1, c), cur(2, c)
            cx = cg * ax
            cxp = prev(1, c) * prev(2, c)
            cx1 = _shift_down(cx, cxp, 1, rows)
            cx2 = _shift_down(cx, cxp, 2, rows)
            wa3 = ca_ref[:, sl]
            conv = wa3[2:3] * cx + wa3[1:2] * cx1 + wa3[0:1] * cx2
            dya = dy_ref[:, sl]
            put(0, c, dya * conv)
            dconv = dya * bg
            nxt = cdc_scr[:, sl]
            dcx = (wa3[2:3] * dconv + wa3[1:2] * _shift_up(dconv, nxt, 1, rows)
                   + wa3[0:1] * _shift_up(dconv, nxt, 2, rows))
            cdc_scr[:, sl] = dconv[0:SUBLANES]
            put(1, c, dcx * ax)
            put(2, c, dcx * cg)
            acc(_SG_CONV_A + 2, sl, dconv * cx)
            acc(_SG_CONV_A + 1, sl, dconv * cx1)
            acc(_SG_CONV_A + 0, sl, dconv * cx2)

        for c in range(nch):
            sl = slice(c * LANES, (c + 1) * LANES)
            xb, xbp = cur(4, c), prev(4, c)
            wb4 = cb_ref[:, sl]
            xr = (wb4[3:4] * xb + wb4[2:3] * _shift_down(xb, xbp, 1, rows)
                  + wb4[1:2] * _shift_down(xb, xbp, 2, rows)
                  + wb4[0:1] * _shift_down(xb, xbp, 3, rows) + bias_ref[:, sl])
            r, i, a, m = _gates(xr, wa_ref[c], ba_ref[:, sl], wx_ref[c], bx_ref[:, sl], lam_ref[:, sl])
            gel, dgel = _gelu_and_grad(cur(3, c))
            dyb = dy_ref[:, w + c * LANES:w + (c + 1) * LANES]
            put(3, c, dyb * h_ref[:, sl] * dgel)
            g_scr[:, sl] = dyb * gel
            a_scr[:, sl] = a
            x_scr[:, sl] = xr
            r_scr[:, sl] = r
            i_scr[:, sl] = i
            m_scr[:, sl] = m

        def step(j, carry):
            r = ts - 1 - j
            lam_t = g_scr[pl.ds(r, 1), :] + carry
            l_scr[pl.ds(r, 1), :] = lam_t
            return a_scr[pl.ds(r, 1), :] * lam_t

        cl_scr[0:1, :] = lax.fori_loop(0, ts, step, cl_scr[0:1, :], unroll=8)

        for c in range(nch):
            sl = slice(c * LANES, (c + 1) * LANES)
            lam_t = l_scr[:, sl]
            hprev = _shift_down(h_ref[:, sl], jnp.where(first, 0.0, hp_ref[:, sl]), 1, rows)
            xr, r, i, m, a = x_scr[:, sl], r_scr[:, sl], i_scr[:, sl], m_scr[:, sl], a_scr[:, sl]
            da = lam_t * hprev
            dm = lam_t * i * xr
            di = lam_t * m * xr
            dxr = lam_t * m * i
            dlog_a = da * a - dm * a * a / m
            lam_p = lam_ref[:, sl]
            dr = dlog_a * (LRU_C * _log_sigmoid(lam_p))
            acc(_SG_LAM, sl, dlog_a * r * (LRU_C * _sigmoid(-lam_p)))
            dpa = dr * r * (1.0 - r)
            dpx = di * i * (1.0 - i)
            dpa_b, dpx_b = dpa.astype(BF16), dpx.astype(BF16)
            dxr = (dxr + lax.dot_general(dpa_b, wa_ref[c], _DIMS["nt"], preferred_element_type=F32)
                   + lax.dot_general(dpx_b, wx_ref[c], _DIMS["nt"], preferred_element_type=F32))
            xr_ref[:, sl] = xr.astype(BF16)
            dpa_ref[:, sl] = dpa_b
            dpx_ref[:, sl] = dpx_b
            acc(_SG_BA, sl, dpa)
            acc(_SG_BX, sl, dpx)
            acc(_SG_BIAS, sl, dxr)
            nxt = cdx_scr[:, sl]
            wb4 = cb_ref[:, sl]
            put(4, c, wb4[3:4] * dxr + wb4[2:3] * _shift_up(dxr, nxt, 1, rows)
                + wb4[1:2] * _shift_up(dxr, nxt, 2, rows) + wb4[0:1] * _shift_up(dxr, nxt, 3, rows))
            cdx_scr[:, sl] = dxr[0:SUBLANES]
            xb, xbp = cur(4, c), prev(4, c)
            acc(_SG_CONV_B + 3, sl, dxr * xb)
            acc(_SG_CONV_B + 2, sl, dxr * _shift_down(xb, xbp, 1, rows))
            acc(_SG_CONV_B + 1, sl, dxr * _shift_down(xb, xbp, 2, rows))
            acc(_SG_CONV_B + 0, sl, dxr * _shift_down(xb, xbp, 3, rows))

    blk = lambda width: pl.BlockSpec((ts, width), lambda p: (nt - 1 - p, 0))
    pre = lambda width: pl.BlockSpec(
        (SUBLANES, width), lambda p: (jnp.maximum((nt - 1 - p) * tpb - 1, 0), 0))
    vec = lambda n: _whole((n, w))
    big = lambda: pltpu.VMEM((ts, w), F32)
    small = lambda: pltpu.VMEM((SUBLANES, w), F32)
    return pl.pallas_call(
        body, name="mixer_bwd", grid=(nt,),
        in_specs=[blk(w5), pre(w5), blk(w), pre(w), blk(2 * w),
                  vec(3), vec(4), vec(1), _whole(wa_blk.shape), vec(1), _whole(wx_blk.shape), vec(1), vec(1)],
        out_specs=[blk(w5), blk(w), blk(w), blk(w), _whole((_SG_ROWS, w))],
        out_shape=[jax.ShapeDtypeStruct((s, w5), BF16), jax.ShapeDtypeStruct((s, w), BF16),
                   jax.ShapeDtypeStruct((s, w), BF16), jax.ShapeDtypeStruct((s, w), BF16),
                   jax.ShapeDtypeStruct((_SG_ROWS, w), F32)],
        scratch_shapes=[big(), big(), big(), big(), big(), big(), big(), small(), small(), small()],
        compiler_params=_cp(("arbitrary",)),
    )(proj, proj, hseq, hseq, dy, conv_a, conv_b, bias, wa_blk, ba, wx_blk, bx, lam)


def _split_dot(v, tri2):
    hi = v.astype(BF16)
    lo = (v - hi.astype(F32)).astype(BF16)
    return jnp.dot(jnp.concatenate([hi, lo], axis=1), tri2, preferred_element_type=F32)


def _tri(cmp):
    r = lax.broadcasted_iota(jnp.int32, (LANES, LANES), 0)
    c = lax.broadcasted_iota(jnp.int32, (LANES, LANES), 1)
    return cmp(r, c).astype(BF16)


def _lane_blocks(v):
    return [v[:, b * LANES:(b + 1) * LANES] for b in range(v.shape[1] // LANES)]


def _last_lane(v):
    return jnp.broadcast_to(v[:, LANES - 1:LANES], v.shape)


def _scores(q, kb, scale):
    return lax.dot_general(q, kb, _DIMS["nt"], preferred_element_type=F32) * scale


def _log_gates(z, diagonal):
    ls = jnp.minimum(z, 0.0) - jnp.log(1.0 + jnp.exp(-jnp.abs(z)))
    ln = ls - z
    valid = None
    if diagonal:
        valid = (lax.broadcasted_iota(jnp.int32, z.shape, 1) < lax.broadcasted_iota(jnp.int32, z.shape, 0))
        ln = jnp.where(valid, ln, 0.0)
    return ls, ln, valid


def _attn_fwd(qkv, heads):
    s = qkv.shape[0]
    dh = LANES
    tq = _pick(s, ATT_TILE)
    nq = s // tq
    nb = tq // LANES
    scale = 1.0 / math.sqrt(dh)

    hp = ATT_FWD_HEADS_PER_STEP
    groups = heads // hp
    wid = hp * dh

    def body(q_ref, k_ref, v_ref, o_ref, tot_ref, acc_scr, car_scr):
        qi = pl.program_id(1)
        acc_scr[...] = jnp.zeros_like(acc_scr)
        car_scr[...] = jnp.zeros_like(car_scr)
        tri = _tri(lambda r, c: r > c)
        tri = jnp.concatenate([tri, tri], axis=0)

        def tile(kt, diagonal):
            k0 = pl.multiple_of(kt * tq, tq)
            heads_cols = [slice(hh * dh, (hh + 1) * dh) for hh in range(hp)]
            zs = [_scores(q_ref[:, cols], k_ref[pl.ds(k0, tq), cols], scale) for cols in heads_cols]
            gates = [_log_gates(z, diagonal) for z in zs]
            sfxs = [_split_dot(jnp.concatenate(_lane_blocks(ln), axis=0), tri) for _, ln, _ in gates]
            for cols, (ls, ln, valid), sfx in zip(heads_cols, gates, sfxs):
                blocks = _lane_blocks(ln)
                car = car_scr[:, cols]
                parts = [None] * nb
                for b in reversed(range(nb)):
                    sb = sfx[b * tq:(b + 1) * tq]
                    parts[b] = sb + car
                    car = car + (sb[:, 0:1] + blocks[b][:, 0:1])
                car_scr[:, cols] = car
                wgt = jnp.exp(ls + jnp.concatenate(parts, axis=1))
                if diagonal:
                    wgt = jnp.where(valid, wgt, 0.0)
                acc_scr[:, cols] += jnp.dot(
                    wgt.astype(BF16), v_ref[pl.ds(k0, tq), cols], preferred_element_type=F32)

        tile(qi, True)

        def step(j, carry):
            tile(qi - 1 - j, False)
            return carry

        lax.fori_loop(0, qi, step, 0)
        o_ref[...] = acc_scr[...].astype(BF16)
        tot_ref[...] = car_scr[...]

    return pl.pallas_call(
        body, name="attn_fwd", grid=(groups, nq),
        in_specs=[pl.BlockSpec((tq, wid), lambda h, i: (i, h)),
                  pl.BlockSpec((s, wid), lambda h, i: (0, groups + h)),
                  pl.BlockSpec((s, wid), lambda h, i: (0, 2 * groups + h))],
        out_specs=[pl.BlockSpec((tq, wid), lambda h, i: (i, h)), pl.BlockSpec((tq, wid), lambda h, i: (i, h))],
        out_shape=[jax.ShapeDtypeStruct((s, heads * dh), BF16), jax.ShapeDtypeStruct((s, heads * dh), F32)],
        scratch_shapes=[pltpu.VMEM((tq, wid), F32), pltpu.VMEM((tq, wid), F32)],
        compiler_params=_cp(("parallel", "arbitrary")),
    )(qkv, qkv, qkv)


def _attn_bwd(qkv, tot, do, heads):
    s = qkv.shape[0]
    dh = LANES
    tq = _pick(s, ATT_TILE)
    nq = s // tq
    nb = tq // LANES
    scale = 1.0 / math.sqrt(dh)

    hp = ATT_HEADS_PER_STEP
    groups = heads // hp
    wid = hp * dh

    def body(q_ref, k_ref, v_ref, tot_ref, do_ref, dq_ref, dk_ref, dv_ref,
             dq_scr, dk_scr, dv_scr, cl_scr, cg_scr):
        qi = pl.program_id(1)

        @pl.when(qi == 0)
        def _():
            dk_scr[...] = jnp.zeros_like(dk_scr)
            dv_scr[...] = jnp.zeros_like(dv_scr)

        dq_scr[...] = jnp.zeros_like(dq_scr)
        cl_scr[...] = jnp.zeros_like(cl_scr)
        cg_scr[...] = jnp.zeros_like(cg_scr)
        tri_le = _tri(lambda r, c: r <= c)
        tri_le = jnp.concatenate([tri_le, tri_le], axis=0)
        tri_lt = _tri(lambda r, c: r < c)

        def tile(kt, diagonal):
            k0 = pl.multiple_of(kt * tq, tq)
            heads_cols = [slice(hh * dh, (hh + 1) * dh) for hh in range(hp)]
            keys = pl.ds(k0, tq)
            zs = [_scores(q_ref[:, cols], k_ref[keys, cols], scale) for cols in heads_cols]
            dws = [lax.dot_general(do_ref[:, cols], v_ref[keys, cols], _DIMS["nt"], preferred_element_type=F32)
                   for cols in heads_cols]
            gates = [_log_gates(z, diagonal) for z in zs]
            pins = [_split_dot(jnp.concatenate(_lane_blocks(ln), axis=0), tri_le) for _, ln, _ in gates]
            wgts, gs = [], []
            for cols, (ls, _, valid), pin, dw in zip(heads_cols, gates, pins, dws):
                total = tot_ref[:, cols]
                cl = cl_scr[:, cols]
                parts = []
                for b in range(nb):
                    pb = pin[b * tq:(b + 1) * tq] + cl
                    parts.append(total - pb)
                    cl = _last_lane(pb)
                cl_scr[:, cols] = cl
                wgt = jnp.exp(ls + jnp.concatenate(parts, axis=1))
                if diagonal:
                    wgt = jnp.where(valid, wgt, 0.0)
                wgts.append(wgt)
                gs.append(wgt * dw)
            pexs = [jnp.dot(jnp.concatenate(_lane_blocks(g), axis=0).astype(BF16), tri_lt,
                            preferred_element_type=F32) for g in gs]
            for cols, wgt in zip(heads_cols, wgts):
                dv_scr[keys, cols] += lax.dot_general(
                    wgt.astype(BF16), do_ref[:, cols], _DIMS["tn"], preferred_element_type=F32)
            for cols, (ls, _, valid), g, pex in zip(heads_cols, gates, gs, pexs):
                gblocks = _lane_blocks(g)
                cg = cg_scr[:, cols]
                parts = []
                for b in range(nb):
                    pb = pex[b * tq:(b + 1) * tq] + cg
                    parts.append(pb)
                    cg = _last_lane(pb + gblocks[b])
                cg_scr[:, cols] = cg
                dz = g - jnp.exp(ls) * (g + jnp.concatenate(parts, axis=1))
                if diagonal:
                    dz = jnp.where(valid, dz, 0.0)
                dz = dz.astype(BF16)
                dq_scr[:, cols] += jnp.dot(dz, k_ref[keys, cols], preferred_element_type=F32)
                dk_scr[keys, cols] += lax.dot_general(
                    dz, q_ref[:, cols], _DIMS["tn"], preferred_element_type=F32)

        def step(j, carry):
            tile(j, False)
            return carry

        lax.fori_loop(0, qi, step, 0)
        tile(qi, True)
        dq_ref[...] = (dq_scr[...] * scale).astype(BF16)

        @pl.when(qi == nq - 1)
        def _():
            dk_ref[...] = (dk_scr[...] * scale).astype(BF16)
            dv_ref[...] = dv_scr[...].astype(BF16)

    qblk = pl.BlockSpec((tq, wid), lambda h, i: (i, h))
    hblk = pl.BlockSpec((s, wid), lambda h, i: (0, h))
    out = jax.ShapeDtypeStruct((s, heads * dh), BF16)
    return pl.pallas_call(
        body, name="attn_bwd", grid=(groups, nq),
        in_specs=[qblk, pl.BlockSpec((s, wid), lambda h, i: (0, groups + h)),
                  pl.BlockSpec((s, wid), lambda h, i: (0, 2 * groups + h)), qblk, qblk],
        out_specs=[qblk, hblk, hblk], out_shape=[out, out, out],
        scratch_shapes=[pltpu.VMEM((tq, wid), F32), pltpu.VMEM((s, wid), F32), pltpu.VMEM((s, wid), F32),
                        pltpu.VMEM((tq, wid), F32), pltpu.VMEM((tq, wid), F32)],
        compiler_params=_cp(("parallel", "arbitrary")),
    )(qkv, qkv, qkv, tot, do)


def _place():
    x, y, c = lax.axis_index("x"), lax.axis_index("y"), lax.axis_index("c")
    chips = [(1 - x, y), (x, 1 - y), (1 - x, 1 - y)]
    return x, y, c, chips


def _remote(src, dst, send_sem, recv_sem, dev):
    return pltpu.make_async_remote_copy(
        src_ref=src, dst_ref=dst, send_sem=send_sem, recv_sem=recv_sem, device_id=dev, device_id_type=MESH)


def _handshake(peers):
    barrier = pltpu.get_barrier_semaphore()
    for dev in peers:
        pl.semaphore_signal(barrier, inc=1, device_id=dev, device_id_type=MESH)
    pl.semaphore_wait(barrier, len(peers))


def _sequencer_kernel(name, n_sems, collective_id):
    return functools.partial(
        pl.kernel, mesh=plsc.ScalarSubcoreMesh(axis_name="seq", num_cores=1), name=name,
        scratch_types=(pltpu.SemaphoreType.DMA,) * n_sems,
        compiler_params=pltpu.CompilerParams(collective_id=collective_id))


def _allgather_async(name, slot_buf, collective_id):
    buf = jax.new_ref(slot_buf, memory_space=pltpu.MemorySpace.HBM)
    hr = slot_buf.shape[1] // 2

    @_sequencer_kernel(name, 12, collective_id)
    def launch(*sems):
        send_sems, recv_sems, fsend_sems, frecv_sems = sems[0:3], sems[3:6], sems[6:9], sems[9:12]
        x, y, c, chips = _place()
        me = 2 * x + y
        sibling = (x, y, 1 - c)
        _handshake([(px, py, c) for px, py in chips] + [sibling])
        mine = buf.at[me, pl.ds(c * hr, hr)]
        firsts = []
        for k, (px, py) in enumerate(chips):
            cp = _remote(mine, mine, send_sems[k], recv_sems[k], (px, py, c))
            cp.start()
            firsts.append(cp)
        passed = []
        for k, (px, py) in enumerate(chips):
            slot = buf.at[2 * px + py, pl.ds(c * hr, hr)]
            _remote(slot, slot, send_sems[k], recv_sems[k], (px, py, c)).wait_recv()
            cp = _remote(slot, slot, fsend_sems[k], frecv_sems[k], sibling)
            cp.start()
            passed.append(cp)
        for k, (px, py) in enumerate(chips):
            slot = buf.at[2 * px + py, pl.ds((1 - c) * hr, hr)]
            _remote(slot, slot, fsend_sems[k], frecv_sems[k], sibling).wait_recv()
        for cp in firsts + passed:
            cp.wait_send()

    launch()
    return buf[...]


def _to_sibling_async(name, slab):
    src = jax.new_ref(slab, memory_space=pltpu.MemorySpace.HBM)
    hr = slab.shape[1] // 2
    got = jax.empty_ref(jax.ShapeDtypeStruct((N_CHIPS, hr, slab.shape[2]), slab.dtype),
                        memory_space=pltpu.MemorySpace.HBM)

    @_sequencer_kernel(name, 2, COLLECTIVE_SIBLING)
    def launch(send_sem, recv_sem):
        x, y, c, _ = _place()
        _handshake([(x, y, 1 - c)])
        _remote(src.at[:, pl.ds((1 - c) * hr, hr), :], got, send_sem, recv_sem, (x, y, 1 - c)).start()
        _remote(got, got, send_sem, recv_sem, (x, y, 1 - c)).wait()

    launch()
    return src[...], got[...]


def _swap_with_sibling_async(name, part):
    src = jax.new_ref(part, memory_space=pltpu.MemorySpace.HBM)
    got = jax.empty_ref(jax.ShapeDtypeStruct(part.shape, part.dtype), memory_space=pltpu.MemorySpace.HBM)

    @_sequencer_kernel(name, 2, COLLECTIVE_SIBLING)
    def launch(send_sem, recv_sem):
        x, y, c, _ = _place()
        _handshake([(x, y, 1 - c)])
        cp = _remote(src, got, send_sem, recv_sem, (x, y, 1 - c))
        cp.start()
        cp.wait()

    launch()
    return got[...]


def _to_chips_async(name, part):
    src = jax.new_ref(part, memory_space=pltpu.MemorySpace.HBM)
    got = jax.empty_ref(jax.ShapeDtypeStruct((3,) + part.shape[1:], part.dtype), memory_space=pltpu.MemorySpace.HBM)

    @_sequencer_kernel(name, 6, COLLECTIVE_CHIPS)
    def launch(*sems):
        send_sems, recv_sems = sems[0:3], sems[3:6]
        x, y, c, chips = _place()
        _handshake([(px, py, c) for px, py in chips])
        cps = []
        for k, (px, py) in enumerate(chips):
            cp = _remote(src.at[2 * px + py], got.at[k], send_sems[k], recv_sems[k], (px, py, c))
            cp.start()
            cps.append(cp)
        for cp in cps:
            cp.wait()

    launch()
    return src[...], got[...]


def _join_sibling_async(name, half_filled):
    buf = jax.new_ref(half_filled, memory_space=pltpu.MemorySpace.HBM)
    hr = half_filled.shape[0] // 2

    @_sequencer_kernel(name, 2, COLLECTIVE_SIBLING)
    def launch(send_sem, recv_sem):
        x, y, c, _ = _place()
        _handshake([(x, y, 1 - c)])
        mine = buf.at[pl.ds(c * hr, hr)]
        other = buf.at[pl.ds((1 - c) * hr, hr)]
        cp = _remote(mine, mine, send_sem, recv_sem, (x, y, 1 - c))
        cp.start()
        _remote(other, other, send_sem, recv_sem, (x, y, 1 - c)).wait_recv()
        cp.wait_send()

    launch()
    return buf[...]


def _allgather_chips_small_async(name, v):
    src = jax.new_ref(v, memory_space=pltpu.MemorySpace.HBM)
    got = jax.empty_ref(jax.ShapeDtypeStruct((N_CHIPS,) + v.shape, v.dtype), memory_space=pltpu.MemorySpace.HBM)

    @_sequencer_kernel(name, 6, COLLECTIVE_CHIPS)
    def launch(*sems):
        send_sems, recv_sems = sems[0:3], sems[3:6]
        x, y, c, chips = _place()
        me = 2 * x + y
        _handshake([(px, py, c) for px, py in chips])
        cps = []
        for k, (px, py) in enumerate(chips):
            cp = _remote(src, got.at[me], send_sems[k], recv_sems[k], (px, py, c))
            cp.start()
            cps.append(cp)
        for k, (px, py) in enumerate(chips):
            slot = got.at[2 * px + py]
            _remote(slot, slot, send_sems[k], recv_sems[k], (px, py, c)).wait_recv()
        for cp in cps:
            cp.wait_send()

    launch()
    return got[...]


def _allreduce_small(name, v):
    r = v.shape[0]
    hr = r // 2
    assert hr % SUBLANES == 0

    def body(v_ref, o_ref, sib_ref, chips_ref, send_sems, recv_sems):
        x, y, c, chips = _place()
        me = 2 * x + y
        sibling = (x, y, 1 - c)
        first = _remote(v_ref, sib_ref, send_sems.at[0], recv_sems.at[0], sibling)
        first.start()
        first.wait()
        mine = pl.ds(pl.multiple_of(c * hr, SUBLANES), hr)
        chips_ref[me] = v_ref[mine, :] + sib_ref[mine, :]
        cps = []
        for k, (px, py) in enumerate(chips):
            cp = _remote(chips_ref.at[me], chips_ref.at[me], send_sems.at[1 + k], recv_sems.at[1 + k], (px, py, c))
            cp.start()
            cps.append(cp)
        for k, (px, py) in enumerate(chips):
            slot = chips_ref.at[2 * px + py]
            _remote(slot, slot, send_sems.at[1 + k], recv_sems.at[1 + k], (px, py, c)).wait_recv()
        total = chips_ref[0]
        for j in range(1, N_CHIPS):
            total = total + chips_ref[j]
        o_ref[mine, :] = total
        last = _remote(o_ref.at[mine], o_ref.at[mine], send_sems.at[4], recv_sems.at[4], sibling)
        last.start()
        other = o_ref.at[pl.ds(pl.multiple_of((1 - c) * hr, SUBLANES), hr)]
        _remote(other, other, send_sems.at[4], recv_sems.at[4], sibling).wait_recv()
        last.wait_send()
        for cp in cps:
            cp.wait_send()

    return pl.pallas_call(
        body, name=name, in_specs=[pl.BlockSpec(memory_space=pltpu.VMEM)],
        out_specs=pl.BlockSpec(memory_space=pltpu.VMEM),
        out_shape=jax.ShapeDtypeStruct((r, LANES), F32),
        scratch_shapes=[pltpu.VMEM((r, LANES), F32), pltpu.VMEM((N_CHIPS, hr, LANES), F32),
                        pltpu.SemaphoreType.DMA((5,)), pltpu.SemaphoreType.DMA((5,))],
    )(v)


def _add_sibling(name, slabs, recv, c):
    _, r, cols = slabs.shape
    hr = r // 2
    tr = _pick(hr, STREAM_TILE)
    nb = hr // tr

    def body(c_ref, a_ref, b_ref, o_ref):
        o_ref[...] = (a_ref[...].astype(F32) + b_ref[...].astype(F32)).astype(BF16)

    grid_spec = pltpu.PrefetchScalarGridSpec(
        num_scalar_prefetch=1, grid=(N_CHIPS, nb),
        in_specs=[pl.BlockSpec((None, tr, cols), lambda j, i, c_ref: (j, c_ref[0] * nb + i, 0)),
                  pl.BlockSpec((None, tr, cols), lambda j, i, c_ref: (j, i, 0))],
        out_specs=pl.BlockSpec((None, tr, cols), lambda j, i, c_ref: (j, i, 0)))
    return pl.pallas_call(
        body, name=name, grid_spec=grid_spec,
        out_shape=jax.ShapeDtypeStruct((N_CHIPS, hr, cols), BF16),
        compiler_params=_cp(("parallel", "parallel")))(jnp.reshape(c, (1,)).astype(jnp.int32), slabs, recv)


def _sum_chips(name, own, recv, chip, c):
    _, hr, cols = recv.shape
    tr = _pick(hr, STREAM_TILE // 2)
    nb = hr // tr

    def body(sc_ref, own_ref, recv_ref, o_ref):
        total = own_ref[...].astype(F32)
        for k in range(3):
            total = total + recv_ref[k].astype(F32)
        o_ref[...] = total

    grid_spec = pltpu.PrefetchScalarGridSpec(
        num_scalar_prefetch=1, grid=(nb,),
        in_specs=[pl.BlockSpec((None, tr, cols), lambda i, sc: (sc[0], i, 0)),
                  pl.BlockSpec((3, tr, cols), lambda i, sc: (0, i, 0))],
        out_specs=pl.BlockSpec((tr, cols), lambda i, sc: (sc[1] * nb + i, 0)))
    return pl.pallas_call(
        body, name=name, grid_spec=grid_spec, out_shape=jax.ShapeDtypeStruct((2 * hr, cols), F32),
        compiler_params=_cp(("parallel",)))(jnp.stack([chip, c]).astype(jnp.int32), own, recv)


def _adamw_math(w, g, m, v):
    m = ADAM_B1 * m + (1.0 - ADAM_B1) * g
    v = ADAM_B2 * v + (1.0 - ADAM_B2) * (g * g)
    m_hat = m / (1.0 - ADAM_B1 ** ADAM_STEP)
    v_hat = v / (1.0 - ADAM_B2 ** ADAM_STEP)
    delta = -ADAM_LR * (m_hat / (jnp.sqrt(v_hat) + ADAM_EPS) + ADAM_WD * w)
    return delta, m, v


def _adamw(name, w, gs, m, v):
    nl, r, cols = w.shape
    tr = _pick(r, ROW_TILE)

    def body(*refs):
        w_ref, m_ref, v_ref = refs[0:3]
        g_refs = refs[3:3 + nl]
        go_ref, d_ref, nm_ref, nv_ref = refs[3 + nl:]
        layer = pl.program_id(0)
        g = g_refs[0][...]
        for j in range(1, nl):
            g = jnp.where(layer == j, g_refs[j][...], g)
        d, nm, nv = _adamw_math(w_ref[...], g, m_ref[...], v_ref[...])
        go_ref[...] = g
        d_ref[...] = d
        nm_ref[...] = nm
        nv_ref[...] = nv

    spec3 = pl.BlockSpec((None, tr, cols), lambda l, i: (l, i, 0))
    gspec = pl.BlockSpec((tr, cols), lambda l, i: (i, 0))
    out = jax.ShapeDtypeStruct((nl, r, cols), F32)
    return pl.pallas_call(
        body, name=name, grid=(nl, r // tr), in_specs=[spec3] * 3 + [gspec] * nl, out_specs=[spec3] * 4,
        out_shape=[out] * 4, compiler_params=_cp(("parallel", "parallel")))(w, m, v, *gs)


def _adamw_small(name, groups):
    n = len(groups)
    flat = [a for grp in groups for a in grp]

    def body(*refs):
        ins, outs = refs[:4 * n], refs[4 * n:]
        for p in range(n):
            w_ref, g_ref, m_ref, v_ref = ins[4 * p:4 * p + 4]
            d, nm, nv = _adamw_math(w_ref[...], g_ref[...], m_ref[...], v_ref[...])
            outs[3 * p][...] = d
            outs[3 * p + 1][...] = nm
            outs[3 * p + 2][...] = nv

    vm = pl.BlockSpec(memory_space=pltpu.VMEM)
    out_shape = [jax.ShapeDtypeStruct(grp[0].shape, F32) for grp in groups for _ in range(3)]
    res = pl.pallas_call(
        body, name=name, in_specs=[vm] * (4 * n), out_specs=[vm] * (3 * n), out_shape=out_shape)(*flat)
    return [tuple(res[3 * p:3 * p + 3]) for p in range(n)]


def _block_diag_pairs(w):
    h, d, _ = w.shape
    z = jnp.zeros((h // 2, d, d), w.dtype)
    top = jnp.concatenate([w[0::2], z], axis=2)
    bot = jnp.concatenate([z, w[1::2]], axis=2)
    return jnp.concatenate([top, bot], axis=1).astype(BF16)


def _diag_pairs_to_heads(g, d):
    a = g[:, :d, :d]
    b = g[:, d:, d:]
    return jnp.stack([a, b], axis=1).reshape(-1, d, d)


def _rows128(a):
    flat = a.reshape(-1, LANES)
    pad = (-flat.shape[0]) % SUBLANES
    if pad:
        flat = jnp.concatenate([flat, jnp.zeros((pad, LANES), flat.dtype)], axis=0)
    return flat


def _unshard_last(g4, shape):
    g4 = g4.reshape((N_CHIPS,) + tuple(shape))
    return jnp.concatenate([g4[j] for j in range(N_CHIPS)], axis=-1)


def kernel(x, norm_gains, hyb_w_in, hyb_conv_a, hyb_conv_b, hyb_conv_b_bias, hyb_rg_w_a, hyb_rg_b_a, hyb_rg_w_x, hyb_rg_b_x, hyb_rg_lambda, hyb_w_out, sb_w_qkv, sb_w_o, mlp_w_up, mlp_w_down, loss_target, m_norm_gains, m_hyb_w_in, m_hyb_conv_a, m_hyb_conv_b, m_hyb_conv_b_bias, m_hyb_rg_w_a, m_hyb_rg_b_a, m_hyb_rg_w_x, m_hyb_rg_b_x, m_hyb_rg_lambda, m_hyb_w_out, m_sb_w_qkv, m_sb_w_o, m_mlp_w_up, m_mlp_w_down, v_norm_gains, v_hyb_w_in, v_hyb_conv_a, v_hyb_conv_b, v_hyb_conv_b_bias, v_hyb_rg_w_a, v_hyb_rg_b_a, v_hyb_rg_w_x, v_hyb_rg_b_x, v_hyb_rg_lambda, v_hyb_w_out, v_sb_w_qkv, v_sb_w_o, v_mlp_w_up, v_mlp_w_down):
    cx_ = lax.axis_index("x")
    cy_ = lax.axis_index("y")
    cc_ = lax.axis_index("c")
    chip = 2 * cx_ + cy_

    x0 = x[0]
    target = loss_target[0]
    s, d = x0.shape
    heads = SB_HEADS
    assert d // heads == LANES
    n_rg, hd = hyb_rg_w_a.shape[1], hyb_rg_w_a.shape[2]
    wmix = n_rg * hd
    assert 2 * hd == LANES

    big = {
        "hyb_w_in": (hyb_w_in, 0), "hyb_w_out": (hyb_w_out, 0), "mlp_w_up0": (mlp_w_up, 0),
        "mlp_w_down0": (mlp_w_down, 0), "sb_w_qkv": (sb_w_qkv, 0), "sb_w_o": (sb_w_o, 0),
        "mlp_w_up1": (mlp_w_up, 1), "mlp_w_down1": (mlp_w_down, 1),
    }
    names = list(big)

    ng_s, ca_s, cb_s = norm_gains.reshape(-1, norm_gains.shape[2]), hyb_conv_a[0], hyb_conv_b[0]
    packed = jnp.concatenate([_rows128(ng_s), _rows128(ca_s), _rows128(cb_s)], axis=0)
    from_chips = _allgather_chips_small_async("allgather_small", packed)

    slots = [_cast_into_slot("cast_" + k, big[k][0], big[k][1], chip) for k in names]
    full = {k: _allgather_async("allgather_" + k, slot, cid) for cid, (k, slot) in enumerate(zip(names, slots))}
    rowsharded = lambda k: full[k].reshape(-1, full[k].shape[2])
    is_mine = lax.broadcasted_iota(jnp.int32, (N_CHIPS, 1, 1), 0) == chip
    gathered = jnp.where(is_mine, packed[None], from_chips)
    n0 = ng_s.size // LANES
    n1 = n0 + (-n0) % SUBLANES
    m0 = ca_s.size // LANES
    m1 = m0 + (-m0) % SUBLANES
    k0 = cb_s.size // LANES
    gains = _unshard_last(gathered[:, 0:n0], ng_s.shape).reshape(2, 4, 1, d)
    conv_a = _unshard_last(gathered[:, n1:n1 + m0], ca_s.shape)
    conv_b = _unshard_last(gathered[:, n1 + m1:n1 + m1 + k0], cb_s.shape)
    bias, b_a, b_x, lam = hyb_conv_b_bias, hyb_rg_b_a, hyb_rg_b_x, hyb_rg_lambda
    wa_blk = _block_diag_pairs(hyb_rg_w_a[0])
    wx_blk = _block_diag_pairs(hyb_rg_w_x[0])

    relu_sq = lambda acc: (jnp.maximum(acc, 0.0), jnp.square(jnp.maximum(acc, 0.0)))

    h1 = _rms_fwd("rms_pre0", x0, gains[0, 0])
    proj = _mm_fwd_col("proj_in", h1, full["hyb_w_in"])[0]
    ycat, hseq = _mixer_fwd(proj, conv_a, conv_b, bias, wa_blk, b_a, wx_blk, b_x, lam)
    mix0 = _mm_fwd_row("proj_out", ycat, rowsharded("hyb_w_out"))
    x1, h2 = _rms_post("rms_mix0", mix0, gains[0, 1], x0, gains[0, 2])
    u0, a0 = _mm_fwd_col("mlp_up0", h2, full["mlp_w_up0"], (BF16, BF16), relu_sq)
    mlp0 = _mm_fwd_row("mlp_down0", a0, rowsharded("mlp_w_down0"))
    x2, h3 = _rms_post("rms_mlp0", mlp0, gains[0, 3], x1, gains[1, 0])

    qkv = _mm_fwd_col("qkv", h3, full["sb_w_qkv"], (BF16,))[0]
    att, tot = _attn_fwd(qkv, heads)
    mix1 = _mm_fwd_row("attn_out", att, rowsharded("sb_w_o"))
    x3, h4 = _rms_post("rms_mix1", mix1, gains[1, 1], x2, gains[1, 2])
    u1, a1 = _mm_fwd_col("mlp_up1", h4, full["mlp_w_up1"], (BF16, BF16), relu_sq)
    mlp1 = _mm_fwd_row("mlp_down1", a1, rowsharded("mlp_w_down1"))
    dy, dmlp1, dgain_mlp1, loss_local = _last_norm_and_loss("last_norm_loss", mlp1, gains[1, 3], x3, target)
    loss = lax.psum(loss_local, ("x", "y", "c"))

    dgain = [[None] * 4 for _ in range(2)]
    drelu = lambda acc, u: (acc * (2.0 * u.astype(F32)),)
    stage_a, stage_b, gfull = {}, {}, {}

    def tie(main, side):
        return lax.optimization_barrier((main, side))

    def reduce_start(k, slab, main):
        main, slab = tie(main, slab)
        stage_a[k] = _to_sibling_async("grads_to_sibling_" + k, slab)
        return main

    def reduce_to_chips(k, main):
        slab, from_sibling = stage_a.pop(k)
        main, part = tie(main, _add_sibling("grads_add_" + k, slab, from_sibling, cc_))
        stage_b[k] = _to_chips_async("grads_to_chips_" + k, part)
        return main

    def reduce_split_start(k, act, dy, cs, main):
        main, other = tie(main, _mm_wgrad_half(k + "_wgrad_sibling_rows", act, dy, 1 - cc_, cs))
        stage_a[k] = (act, dy, cs, _swap_with_sibling_async("grads_to_sibling_" + k, other))
        return main

    def reduce_split_to_chips(k, main):
        act, dy, cs, from_sibling = stage_a.pop(k)
        main, part = tie(main, _mm_wgrad_half(k + "_wgrad_my_rows", act, dy, cc_, cs, init=from_sibling))
        stage_b[k] = _to_chips_async("grads_to_chips_" + k, part)
        return main

    def after(value, token):
        return tie(value, token)[0]

    def reduce_finish(k, main):
        own, from_chips = stage_b.pop(k)
        main, half = tie(main, _sum_chips("grads_sum_" + k, after(own, main), from_chips, chip, cc_))
        gfull[k] = _join_sibling_async("grads_join_" + k, half)
        return main

    def mlp_bwd(layer, dxo, dmlp, xin, hin, u, a, mix):
        down, up = f"mlp_w_down{layer}", f"mlp_w_up{layer}"
        wd, wu = rowsharded(down), full[up]
        dmlp = reduce_split_start(down, a, dmlp, None, dmlp)
        du = _mm_bwd_row(f"mlp_down{layer}_bwd", dmlp, wd, (BF16,), u, drelu)[0]
        du = reduce_split_start(up, hin, du, wu.shape[2], du)
        du = reduce_split_to_chips(down, du)
        dh = _mm_bwd_col(f"mlp_up{layer}_bwd", du, wu)
        dh = reduce_split_to_chips(up, dh)
        dxm, dgain[layer][2], dmix, dgain[layer][1] = _rms_bwd_pair(
            f"rms_premlp{layer}_mix{layer}_bwd", xin, gains[layer, 2], dh, dxo, mix, gains[layer, 1])
        return dxm, dmix

    dgain[1][3] = dgain_mlp1
    dx3, dmix1 = mlp_bwd(1, dy, dmlp1, x3, h4, u1, a1, mix1)
    dmix1 = reduce_start("sb_w_o", _mm_wgrad_row("attn_out_wgrad", att, dmix1).reshape(N_CHIPS, -1, d), dmix1)
    datt = _mm_bwd_row("attn_out_bwd", dmix1, rowsharded("sb_w_o"), (BF16,))[0]
    dq, dk, dv = _attn_bwd(qkv, tot, datt, heads)
    dqkv = jnp.concatenate([dq, dk, dv], axis=1)
    dqkv = reduce_to_chips("sb_w_o", dqkv)
    dqkv = reduce_finish("mlp_w_down1", dqkv)
    dqkv = reduce_finish("mlp_w_up1", dqkv)
    dqkv = reduce_split_start("sb_w_qkv", h3, dqkv, full["sb_w_qkv"].shape[2], dqkv)
    dh3 = _mm_bwd_col("qkv_bwd", dqkv, full["sb_w_qkv"])
    dh3 = reduce_split_to_chips("sb_w_qkv", dh3)
    dx2, dgain[1][0], dmlp0, dgain[0][3] = _rms_bwd_pair(
        "rms_pre1_mlp0_bwd", x2, gains[1, 0], dh3, dx3, mlp0, gains[0, 3])

    dx1, dmix0 = mlp_bwd(0, dx2, dmlp0, x1, h2, u0, a0, mix0)
    dmix0 = reduce_finish("sb_w_o", dmix0)
    dmix0 = reduce_finish("sb_w_qkv", dmix0)
    dmix0 = reduce_finish("mlp_w_down0", dmix0)
    dmix0 = reduce_start("hyb_w_out", _mm_wgrad_row("proj_out_wgrad", ycat, dmix0).reshape(N_CHIPS, -1, d), dmix0)
    dycat = _mm_bwd_row("proj_out_bwd", dmix0, rowsharded("hyb_w_out"))[0]
    dproj, xr_b, dpa_b, dpx_b, sg = _mixer_bwd(
        proj, hseq, dycat, conv_a, conv_b, bias, wa_blk, b_a, wx_blk, b_x, lam)
    dproj = reduce_finish("mlp_w_up0", dproj)
    dproj = reduce_to_chips("hyb_w_out", dproj)
    dproj = reduce_split_start("hyb_w_in", h1, dproj, full["hyb_w_in"].shape[2], dproj)
    dh1 = _mm_bwd_col("proj_in_bwd", dproj, full["hyb_w_in"])
    dh1 = reduce_split_to_chips("hyb_w_in", dh1)
    dx0, dgain[0][0] = _rms_bwd("rms_pre0_bwd", x0, gains[0, 0], dh1, res=dx1)
    dwa = _diag_pairs_to_heads(_mm_wgrad_diag("rg_w_a_wgrad", xr_b, dpa_b), hd)
    dwx = _diag_pairs_to_heads(_mm_wgrad_diag("rg_w_x_wgrad", xr_b, dpx_b), hd)

    dgains = jnp.concatenate([dgain[l][k] for l in range(2) for k in range(4)], axis=0)
    small_parts = [dgains, sg[_SG_CONV_A:_SG_CONV_A + 3], sg[_SG_CONV_B:_SG_CONV_B + 4], sg[_SG_BIAS:_SG_BIAS + 1],
                   dwa, sg[_SG_BA:_SG_BA + 1], dwx, sg[_SG_BX:_SG_BX + 1], sg[_SG_LAM:_SG_LAM + 1]]
    small_rows = [_rows128(p) for p in small_parts]
    n_small = sum(rws.shape[0] for rws in small_rows)
    tail_pad = [jnp.zeros(((-n_small) % (2 * SUBLANES), LANES), F32)] if n_small % (2 * SUBLANES) else []
    reduced = _allreduce_small("allreduce_small", jnp.concatenate(small_rows + tail_pad, axis=0))
    small_full, off = [], 0
    for p, rws in zip(small_parts, small_rows):
        small_full.append(reduced[off:off + p.size // LANES].reshape(p.shape))
        off += rws.shape[0]
    g_gains, g_ca, g_cb, g_bias, g_wa, g_ba, g_wx, g_bx, g_lam = small_full

    def my_cols(g, width):
        return lax.dynamic_slice_in_dim(g, chip * width, width, axis=g.ndim - 1)

    small = [
        ("norm_gains", norm_gains, my_cols(g_gains, norm_gains.shape[2]).reshape(norm_gains.shape),
         m_norm_gains, v_norm_gains),
        ("hyb_conv_a", hyb_conv_a, my_cols(g_ca, hyb_conv_a.shape[2])[None], m_hyb_conv_a, v_hyb_conv_a),
        ("hyb_conv_b", hyb_conv_b, my_cols(g_cb, hyb_conv_b.shape[2])[None], m_hyb_conv_b, v_hyb_conv_b),
        ("hyb_conv_b_bias", hyb_conv_b_bias, g_bias, m_hyb_conv_b_bias, v_hyb_conv_b_bias),
        ("hyb_rg_w_a", hyb_rg_w_a, g_wa[None], m_hyb_rg_w_a, v_hyb_rg_w_a),
        ("hyb_rg_b_a", hyb_rg_b_a, g_ba, m_hyb_rg_b_a, v_hyb_rg_b_a),
        ("hyb_rg_w_x", hyb_rg_w_x, g_wx[None], m_hyb_rg_w_x, v_hyb_rg_w_x),
        ("hyb_rg_b_x", hyb_rg_b_x, g_bx, m_hyb_rg_b_x, v_hyb_rg_b_x),
        ("hyb_rg_lambda", hyb_rg_lambda, g_lam, m_hyb_rg_lambda, v_hyb_rg_lambda),
    ]
    to2d = lambda a: a.reshape(-1, a.shape[-1])
    small_res = _adamw_small("adamw_small", [tuple(to2d(a) for a in (w, g, m, v)) for _, w, g, m, v in small])
    out = {}
    for (nm, w, g, _, _), (dl, nmom, nvar) in zip(small, small_res):
        out[nm] = (g, dl.reshape(w.shape), nmom.reshape(w.shape), nvar.reshape(w.shape))

    stacked = {
        "mlp_w_down": (mlp_w_down, m_mlp_w_down, v_mlp_w_down, ["mlp_w_down0", "mlp_w_down1"]),
        "mlp_w_up": (mlp_w_up, m_mlp_w_up, v_mlp_w_up, ["mlp_w_up0", "mlp_w_up1"]),
        "sb_w_o": (sb_w_o, m_sb_w_o, v_sb_w_o, ["sb_w_o"]),
        "sb_w_qkv": (sb_w_qkv, m_sb_w_qkv, v_sb_w_qkv, ["sb_w_qkv"]),
        "hyb_w_out": (hyb_w_out, m_hyb_w_out, v_hyb_w_out, ["hyb_w_out"]),
        "hyb_w_in": (hyb_w_in, m_hyb_w_in, v_hyb_w_in, ["hyb_w_in"]),
    }

    def update(k, token):
        w, m, v, parts = stacked[k]
        out[k] = tuple(_adamw("adamw_" + k, w, [after(gfull[p], token) for p in parts], m, v))
        return out[k][1]

    token = small_res[0][0]
    token = update("sb_w_qkv", token)
    token = update("sb_w_o", token)
    token = update("mlp_w_down", token)
    token = reduce_finish("hyb_w_out", token)
    token = update("mlp_w_up", token)
    token = reduce_finish("hyb_w_in", token)
    token = update("hyb_w_out", token)
    update("hyb_w_in", token)

    order = ["norm_gains", "hyb_w_in", "hyb_conv_a", "hyb_conv_b", "hyb_conv_b_bias", "hyb_rg_w_a", "hyb_rg_b_a",
             "hyb_rg_w_x", "hyb_rg_b_x", "hyb_rg_lambda", "hyb_w_out", "sb_w_qkv", "sb_w_o", "mlp_w_up",
             "mlp_w_down"]
    return (loss, dx0[None], *[out[k][0] for k in order], *[out[k][1] for k in order],
            *[out[k][2] for k in order], *[out[k][3] for k in order])
```

```python
import functools
import math

import jax
import jax.numpy as jnp
from jax import lax
from jax.experimental import pallas as pl
from jax.experimental.pallas import tpu as pltpu
from jax.experimental.pallas import tpu_sc as plsc

F32 = jnp.float32
BF16 = jnp.bfloat16
MESH = pl.DeviceIdType.MESH

SB_HEADS = 16
NORM_EPS = 1e-6
LRU_C = 8.0
ADAM_LR = 0.001
ADAM_B1 = 0.9
ADAM_B2 = 0.999
ADAM_EPS = 1e-08
ADAM_WD = 0.01
ADAM_STEP = 10

LANES = 128
SUBLANES = 8
VMEM_LIMIT = 48 * 1024 * 1024
MM_TILE = 1024
MM_VMEM_BUDGET = 40 * 1024 * 1024
MM_TILE_N = 1280
MM_TILE_K = 2048
ROW_TILE = 256
STREAM_TILE = 1024
ATT_TILE = 512
ATT_HEADS_PER_STEP = 2
ATT_FWD_HEADS_PER_STEP = 4
N_CHIPS = 4
COLLECTIVE_SIBLING = 8
COLLECTIVE_CHIPS = 9

_DIMS = {
    "nn": (((1,), (0,)), ((), ())),
    "nt": (((1,), (1,)), ((), ())),
    "tn": (((0,), (0,)), ((), ())),
}


def _cp(sem=None, vmem=VMEM_LIMIT):
    return pltpu.CompilerParams(dimension_semantics=sem, vmem_limit_bytes=vmem)


def _pick(dim, pref):
    t = min(dim, pref)
    while dim % t:
        t -= LANES
    return t


def _whole(shape):
    nd = len(shape)
    return pl.BlockSpec(tuple(shape), lambda *_: (0,) * nd)


def _sigmoid(z):
    return 1.0 / (1.0 + jnp.exp(-z))


def _log_sigmoid(z):
    return jnp.minimum(z, 0.0) - jnp.log(1.0 + jnp.exp(-jnp.abs(z)))


def _expm1(z):
    series = z * (1.0 + z * (0.5 + z * (1.0 / 6.0 + z * (1.0 / 24.0))))
    return jnp.where(jnp.abs(z) < 0.05, series, jnp.exp(z) - 1.0)


_GELU_C = math.sqrt(2.0 / math.pi)


def _gelu_and_grad(g):
    inner = _GELU_C * (g + 0.044715 * g * g * g)
    t = jnp.tanh(inner)
    val = 0.5 * g * (1.0 + t)
    grad = 0.5 * (1.0 + t) + 0.5 * g * (1.0 - t * t) * _GELU_C * (1.0 + 3.0 * 0.044715 * g * g)
    return val, grad


def _shift_down(cur, prev8, k, rows):
    n = cur.shape[0]
    rolled = pltpu.roll(cur, k, 0)
    head = jnp.tile(pltpu.roll(prev8, k, 0), (n // SUBLANES, 1))
    return jnp.where(rows < k, head, rolled)


def _shift_up(cur, next8, k, rows):
    n = cur.shape[0]
    rolled = pltpu.roll(cur, n - k, 0)
    tail = jnp.tile(pltpu.roll(next8, SUBLANES - k, 0), (n // SUBLANES, 1))
    return jnp.where(rows >= n - k, tail, rolled)


def _colsum(v):
    return jnp.sum(v, axis=0, keepdims=True)


def _matmul(name, mode, grid, operands, in_specs, out_shapes, out_specs, acc_shape, epilogue=None, a_fn=None):
    nk = grid[2]
    n_in = len(operands)
    dims = _DIMS[mode]

    def finish(acc, extra, outs):
        res = epilogue(acc, *[e[...] for e in extra]) if epilogue is not None else (acc,)
        for o_ref, o in zip(outs, res):
            o_ref[...] = o.astype(o_ref.dtype)

    def product(a_ref, b_ref):
        a = a_ref[...].astype(BF16)
        if a_fn is not None:
            a = a_fn(a)
        return lax.dot_general(a, b_ref[...].astype(BF16), dims, preferred_element_type=F32)

    def body_single(*refs):
        finish(product(refs[0], refs[1]), refs[2:n_in], refs[n_in:])

    def body(*refs):
        extra = refs[2:n_in]
        outs = refs[n_in:-1]
        acc_ref = refs[-1]
        k = pl.program_id(2)

        @pl.when(k == 0)
        def _():
            acc_ref[...] = product(refs[0], refs[1])

        @pl.when(k > 0)
        def _():
            acc_ref[...] += product(refs[0], refs[1])

        @pl.when(k == nk - 1)
        def _():
            finish(acc_ref[...], extra, outs)

    return pl.pallas_call(
        body_single if nk == 1 else body, name=name, grid=grid, in_specs=in_specs, out_specs=out_specs,
        out_shape=out_shapes, scratch_shapes=[] if nk == 1 else [pltpu.VMEM(acc_shape, F32)],
        compiler_params=_cp(("parallel", "parallel", "arbitrary")),
    )(*operands)


def _pick_m(m, tk, tn, a_dtype, b_dtype, out_dtypes, extra_dtypes=()):
    size = lambda dt: jnp.dtype(dt).itemsize
    per_row = 2 * tk * size(a_dtype) + tn * (2 * sum(size(dt) for dt in tuple(out_dtypes) + tuple(extra_dtypes)) + 4)
    fixed = 2 * tk * tn * size(b_dtype)
    tm = _pick(m, MM_TILE)
    while tm > LANES and tm * per_row + fixed > MM_VMEM_BUDGET:
        tm = _pick(m, tm // 2)
    return tm


def _mm_fwd_col(name, a, wfull, out_dtypes=(F32,), epilogue=None):
    s, kdim = a.shape
    _, _, cs = wfull.shape
    tk, tn = _pick(kdim, MM_TILE_K), _pick(cs, MM_TILE_N)
    tm = _pick_m(s, tk, tn, a.dtype, wfull.dtype, out_dtypes)
    nbj = cs // tn
    grid = (s // tm, N_CHIPS * nbj, kdim // tk)
    out_shapes = [jax.ShapeDtypeStruct((s, N_CHIPS * cs), dt) for dt in out_dtypes]
    out_specs = [pl.BlockSpec((tm, tn), lambda i, n, k: (i, n)) for _ in out_dtypes]
    return _matmul(
        name, "nn", grid, [a, wfull],
        [pl.BlockSpec((tm, tk), lambda i, n, k: (i, k)),
         pl.BlockSpec((None, tk, tn), lambda i, n, k: (n // nbj, k, n % nbj))],
        out_shapes, out_specs, (tm, tn), epilogue)


def _mm_fwd_row(name, a, w2d, out_dtype=BF16, a_fn=None):
    s, kdim = a.shape
    _, n_out = w2d.shape
    tk, tn = _pick(kdim, MM_TILE_K), _pick(n_out, MM_TILE)
    tm = _pick_m(s, tk, tn, a.dtype, w2d.dtype, (out_dtype,))
    grid = (s // tm, n_out // tn, kdim // tk)
    return _matmul(
        name, "nn", grid, [a, w2d],
        [pl.BlockSpec((tm, tk), lambda i, n, k: (i, k)),
         pl.BlockSpec((tk, tn), lambda i, n, k: (k, n))],
        [jax.ShapeDtypeStruct((s, n_out), out_dtype)],
        [pl.BlockSpec((tm, tn), lambda i, n, k: (i, n))], (tm, tn), a_fn=a_fn)[0]


def _mm_bwd_col(name, dy, wfull, out_dtype=BF16):
    s, _ = dy.shape
    _, kdim, cs = wfull.shape
    tn, tk = _pick(kdim, MM_TILE), _pick(cs, MM_TILE_K)
    tm = _pick_m(s, tk, tn, dy.dtype, wfull.dtype, (out_dtype,))
    nbj = cs // tk
    grid = (s // tm, kdim // tn, N_CHIPS * nbj)
    return _matmul(
        name, "nt", grid, [dy, wfull],
        [pl.BlockSpec((tm, tk), lambda i, n, k: (i, k)),
         pl.BlockSpec((None, tn, tk), lambda i, n, k: (k // nbj, n, k % nbj))],
        [jax.ShapeDtypeStruct((s, kdim), out_dtype)],
        [pl.BlockSpec((tm, tn), lambda i, n, k: (i, n))], (tm, tn))[0]


def _mm_bwd_row(name, dy, w2d, out_dtypes=(F32,), extra=None, epilogue=None):
    s, n_in = dy.shape
    kdim, _ = w2d.shape
    tn, tk = _pick(kdim, MM_TILE), _pick(n_in, MM_TILE_K)
    tm = _pick_m(s, tk, tn, dy.dtype, w2d.dtype, out_dtypes, () if extra is None else (extra.dtype,))
    grid = (s // tm, kdim // tn, n_in // tk)
    operands = [dy, w2d]
    in_specs = [pl.BlockSpec((tm, tk), lambda i, n, k: (i, k)),
                pl.BlockSpec((tn, tk), lambda i, n, k: (n, k))]
    if extra is not None:
        operands.append(extra)
        in_specs.append(pl.BlockSpec((tm, tn), lambda i, n, k: (i, n)))
    return _matmul(
        name, "nt", grid, operands, in_specs,
        [jax.ShapeDtypeStruct((s, kdim), dt) for dt in out_dtypes],
        [pl.BlockSpec((tm, tn), lambda i, n, k: (i, n)) for _ in out_dtypes], (tm, tn), epilogue)


def _mm_wgrad_row(name, a, dy):
    s, kdim = a.shape
    _, n_out = dy.shape
    tn, ts = _pick(n_out, MM_TILE), _pick(s, MM_TILE_K)
    tm = _pick_m(kdim, ts, tn, a.dtype, dy.dtype, (BF16,))
    grid = (kdim // tm, n_out // tn, s // ts)
    return _matmul(
        name, "tn", grid, [a, dy],
        [pl.BlockSpec((ts, tm), lambda i, n, k: (k, i)),
         pl.BlockSpec((ts, tn), lambda i, n, k: (k, n))],
        [jax.ShapeDtypeStruct((kdim, n_out), BF16)],
        [pl.BlockSpec((tm, tn), lambda i, n, k: (i, n))], (tm, tn))[0]


def _mm_wgrad_half(name, a, dy, half, cs=None, init=None, a_fn=None):
    s, kdim = a.shape
    ts = _pick(s, MM_TILE_K)
    nk = s // ts
    if cs is not None:
        hr, cols = kdim // 2, cs
        tn = _pick(cs, MM_TILE_N)
        tm = _pick_m(hr, ts, tn, a.dtype, dy.dtype, (BF16,), (BF16,))
        ni, nbj = hr // tm, cs // tn
        grid = (ni, N_CHIPS * nbj, nk)
        a_map = lambda i, n, k, h: (k, h[0] * ni + i)
        o_map = lambda i, n, k, h: (n // nbj, i, n % nbj)
    else:
        hr, cols = kdim // N_CHIPS // 2, dy.shape[1]
        tn = _pick(cols, MM_TILE)
        tm = _pick_m(hr, ts, tn, a.dtype, dy.dtype, (BF16,), (BF16,))
        ni = hr // tm
        grid = (N_CHIPS * ni, cols // tn, nk)
        a_map = lambda i, n, k, h: (k, (i // ni) * 2 * ni + h[0] * ni + i % ni)
        o_map = lambda i, n, k, h: (i // ni, i % ni, n)
    with_init = init is not None

    def body(*refs):
        a_ref, b_ref = refs[1], refs[2]
        init_ref = refs[3] if with_init else None
        o_ref, acc_ref = refs[-2], refs[-1]
        k = pl.program_id(2)

        def product():
            av = a_ref[...].astype(BF16)
            if a_fn is not None:
                av = a_fn(av)
            return lax.dot_general(av, b_ref[...].astype(BF16), _DIMS["tn"], preferred_element_type=F32)

        @pl.when(k == 0)
        def _():
            if with_init:
                acc_ref[...] = init_ref[...].astype(F32)
                acc_ref[...] += product()
            else:
                acc_ref[...] = product()

        @pl.when(k > 0)
        def _():
            acc_ref[...] += product()

        @pl.when(k == nk - 1)
        def _():
            o_ref[...] = acc_ref[...].astype(BF16)

    oblk = pl.BlockSpec((None, tm, tn), o_map)
    grid_spec = pltpu.PrefetchScalarGridSpec(
        num_scalar_prefetch=1, grid=grid,
        in_specs=[pl.BlockSpec((ts, tm), a_map), pl.BlockSpec((ts, tn), lambda i, n, k, h: (k, n))]
        + ([oblk] if with_init else []),
        out_specs=oblk, scratch_shapes=[pltpu.VMEM((tm, tn), F32)])
    operands = [jnp.reshape(half, (1,)).astype(jnp.int32), a, dy] + ([init] if with_init else [])
    return pl.pallas_call(
        body, name=name, grid_spec=grid_spec, out_shape=jax.ShapeDtypeStruct((N_CHIPS, hr, cols), BF16),
        compiler_params=_cp(("parallel", "parallel", "arbitrary")))(*operands)


def _mm_wgrad_diag(name, a, dy):
    s, width = a.shape
    nb = width // LANES
    ts = _pick(s, MM_TILE)
    grid = (nb, 1, s // ts)
    return _matmul(
        name, "tn", grid, [a, dy],
        [pl.BlockSpec((ts, LANES), lambda i, n, k: (k, i)),
         pl.BlockSpec((ts, LANES), lambda i, n, k: (k, i))],
        [jax.ShapeDtypeStruct((nb, LANES, LANES), F32)],
        [pl.BlockSpec((None, LANES, LANES), lambda i, n, k: (i, 0, 0))], (LANES, LANES))[0]


def _rowspec(tr, d):
    return pl.BlockSpec((tr, d), lambda i: (i, 0))


def _vecspec(d):
    return pl.BlockSpec((1, d), lambda i: (0, 0))


def _rms(x, g):
    return x * lax.rsqrt(jnp.mean(x * x, axis=-1, keepdims=True) + NORM_EPS) * g


def _cast_into_slot(name, w, layer, chip):
    _, r, c = w.shape
    tr = _pick(r, STREAM_TILE)

    def body(chip_ref, w_ref, o_ref):
        o_ref[...] = w_ref[...].astype(BF16)

    grid_spec = pltpu.PrefetchScalarGridSpec(
        num_scalar_prefetch=1, grid=(r // tr,),
        in_specs=[pl.BlockSpec((None, tr, c), lambda i, chip_ref: (layer, i, 0))],
        out_specs=pl.BlockSpec((None, tr, c), lambda i, chip_ref: (chip_ref[0], i, 0)))
    return pl.pallas_call(
        body, name=name, grid_spec=grid_spec, out_shape=jax.ShapeDtypeStruct((N_CHIPS, r, c), BF16),
        compiler_params=_cp(("parallel",)))(jnp.reshape(chip, (1,)).astype(jnp.int32), w)


def _rms_fwd(name, x, g):
    s, d = x.shape
    tr = _pick(s, ROW_TILE)

    def body(x_ref, g_ref, h_ref):
        h_ref[...] = _rms(x_ref[...], g_ref[...]).astype(BF16)

    return pl.pallas_call(
        body, name=name, grid=(s // tr,), in_specs=[_rowspec(tr, d), _vecspec(d)],
        out_specs=_rowspec(tr, d), out_shape=jax.ShapeDtypeStruct((s, d), BF16),
        compiler_params=_cp(("parallel",)))(x, g)


def _rms_post(name, y, g_post, res, g_next=None):
    s, d = y.shape
    tr = _pick(s, ROW_TILE)
    with_next = g_next is not None

    def body(*refs):
        if with_next:
            y_ref, gp_ref, r_ref, gn_ref, x_ref, h_ref = refs
        else:
            y_ref, gp_ref, r_ref, x_ref = refs
        xn = r_ref[...] + _rms(y_ref[...].astype(F32), gp_ref[...])
        x_ref[...] = xn
        if with_next:
            h_ref[...] = _rms(xn, gn_ref[...]).astype(BF16)

    operands = [y, g_post, res] + ([g_next] if with_next else [])
    in_specs = [_rowspec(tr, d), _vecspec(d), _rowspec(tr, d)] + ([_vecspec(d)] if with_next else [])
    out_shape = [jax.ShapeDtypeStruct((s, d), F32)] + ([jax.ShapeDtypeStruct((s, d), BF16)] if with_next else [])
    out_specs = [_rowspec(tr, d)] + ([_rowspec(tr, d)] if with_next else [])
    return pl.pallas_call(
        body, name=name, grid=(s // tr,), in_specs=in_specs, out_specs=out_specs, out_shape=out_shape,
        compiler_params=_cp(("parallel",)))(*operands)


def _rms_bwd(name, x, g, dy, res=None, out_dtype=F32):
    s, d = x.shape
    tr = _pick(s, ROW_TILE)
    nsteps = s // tr
    with_res = res is not None

    def body(*refs):
        if with_res:
            x_ref, g_ref, dy_ref, r_ref, dx_ref, dg_ref, acc_ref = refs
        else:
            x_ref, g_ref, dy_ref, dx_ref, dg_ref, acc_ref = refs
        i = pl.program_id(0)

        @pl.when(i == 0)
        def _():
            acc_ref[...] = jnp.zeros_like(acc_ref)

        xv = x_ref[...]
        dyv = dy_ref[...].astype(F32)
        r = lax.rsqrt(jnp.mean(xv * xv, axis=-1, keepdims=True) + NORM_EPS)
        xhat = xv * r
        gy = dyv * g_ref[...]
        dx = r * (gy - xhat * jnp.mean(gy * xhat, axis=-1, keepdims=True))
        if with_res:
            dx = dx + r_ref[...]
        dx_ref[...] = dx.astype(dx_ref.dtype)
        acc_ref[...] += jnp.sum((dyv * xhat).reshape(tr // SUBLANES, SUBLANES, d), axis=0)

        @pl.when(i == nsteps - 1)
        def _():
            dg_ref[...] = jnp.broadcast_to(_colsum(acc_ref[...]), (SUBLANES, d))

    operands = [x, g, dy] + ([res] if with_res else [])
    in_specs = [_rowspec(tr, d), _vecspec(d), _rowspec(tr, d)] + ([_rowspec(tr, d)] if with_res else [])
    dx, dg = pl.pallas_call(
        body, name=name, grid=(nsteps,), in_specs=in_specs,
        out_specs=[_rowspec(tr, d), pl.BlockSpec((SUBLANES, d), lambda i: (0, 0))],
        out_shape=[jax.ShapeDtypeStruct((s, d), out_dtype), jax.ShapeDtypeStruct((SUBLANES, d), F32)],
        scratch_shapes=[pltpu.VMEM((SUBLANES, d), F32)],
        compiler_params=_cp(("arbitrary",)))(*operands)
    return dx, dg[0:1]


def _rms_bwd_pair(name, x, g, dy, res, y2, g2):
    s, d = x.shape
    tr = _pick(s, ROW_TILE)
    nsteps = s // tr

    def through(xv, gv, dyv):
        r = lax.rsqrt(jnp.mean(xv * xv, axis=-1, keepdims=True) + NORM_EPS)
        xhat = xv * r
        gy = dyv * gv
        dx = r * (gy - xhat * jnp.mean(gy * xhat, axis=-1, keepdims=True))
        return dx, jnp.sum((dyv * xhat).reshape(tr // SUBLANES, SUBLANES, d), axis=0)

    def body(x_ref, g_ref, dy_ref, r_ref, y2_ref, g2_ref, dx_ref, d2_ref, dg_ref, dg2_ref, acc_ref, acc2_ref):
        i = pl.program_id(0)

        @pl.when(i == 0)
        def _():
            acc_ref[...] = jnp.zeros_like(acc_ref)
            acc2_ref[...] = jnp.zeros_like(acc2_ref)

        dx, part = through(x_ref[...], g_ref[...], dy_ref[...].astype(F32))
        dx = dx + r_ref[...]
        dx_ref[...] = dx
        acc_ref[...] += part
        d2, part2 = through(y2_ref[...].astype(F32), g2_ref[...], dx)
        d2_ref[...] = d2.astype(d2_ref.dtype)
        acc2_ref[...] += part2

        @pl.when(i == nsteps - 1)
        def _():
            dg_ref[...] = jnp.broadcast_to(_colsum(acc_ref[...]), (SUBLANES, d))
            dg2_ref[...] = jnp.broadcast_to(_colsum(acc2_ref[...]), (SUBLANES, d))

    row, vec = _rowspec(tr, d), _vecspec(d)
    gspec = pl.BlockSpec((SUBLANES, d), lambda i: (0, 0))
    dx, d2, dg, dg2 = pl.pallas_call(
        body, name=name, grid=(nsteps,), in_specs=[row, vec, row, row, row, vec],
        out_specs=[row, row, gspec, gspec],
        out_shape=[jax.ShapeDtypeStruct((s, d), F32), jax.ShapeDtypeStruct((s, d), BF16),
                   jax.ShapeDtypeStruct((SUBLANES, d), F32), jax.ShapeDtypeStruct((SUBLANES, d), F32)],
        scratch_shapes=[pltpu.VMEM((SUBLANES, d), F32), pltpu.VMEM((SUBLANES, d), F32)],
        compiler_params=_cp(("arbitrary",)))(x, g, dy, res, y2, g2)
    return dx, dg[0:1], d2, dg2[0:1]


def _last_norm_and_loss(name, y, g, res, target):
    s, d = y.shape
    tr = _pick(s, ROW_TILE)
    nsteps = s // tr

    def body(y_ref, g_ref, r_ref, t_ref, dx_ref, dy_ref, dg_ref, l_ref, acc_ref, lacc_ref):
        i = pl.program_id(0)

        @pl.when(i == 0)
        def _():
            acc_ref[...] = jnp.zeros_like(acc_ref)
            lacc_ref[...] = jnp.zeros_like(lacc_ref)

        yv = y_ref[...].astype(F32)
        gv = g_ref[...]
        r = lax.rsqrt(jnp.mean(yv * yv, axis=-1, keepdims=True) + NORM_EPS)
        yhat = yv * r
        err = r_ref[...] + yhat * gv - t_ref[...]
        dx = err * (1.0 / d)
        dx_ref[...] = dx
        lacc_ref[...] += jnp.sum((err * err).reshape(tr // SUBLANES, SUBLANES, d), axis=0)
        gy = dx * gv
        dy_ref[...] = (r * (gy - yhat * jnp.mean(gy * yhat, axis=-1, keepdims=True))).astype(dy_ref.dtype)
        acc_ref[...] += jnp.sum((dx * yhat).reshape(tr // SUBLANES, SUBLANES, d), axis=0)

        @pl.when(i == nsteps - 1)
        def _():
            dg_ref[...] = jnp.broadcast_to(_colsum(acc_ref[...]), (SUBLANES, d))
            l_ref[...] = jnp.full((SUBLANES, LANES), (0.5 / d) * jnp.sum(lacc_ref[...]), F32)

    dx, dy, dg, l = pl.pallas_call(
        body, name=name, grid=(nsteps,),
        in_specs=[_rowspec(tr, d), _vecspec(d), _rowspec(tr, d), _rowspec(tr, d)],
        out_specs=[_rowspec(tr, d), _rowspec(tr, d), pl.BlockSpec((SUBLANES, d), lambda i: (0, 0)),
                   pl.BlockSpec((SUBLANES, LANES), lambda i: (0, 0))],
        out_shape=[jax.ShapeDtypeStruct((s, d), F32), jax.ShapeDtypeStruct((s, d), BF16),
                   jax.ShapeDtypeStruct((SUBLANES, d), F32), jax.ShapeDtypeStruct((SUBLANES, LANES), F32)],
        scratch_shapes=[pltpu.VMEM((SUBLANES, d), F32), pltpu.VMEM((SUBLANES, d), F32)],
        compiler_params=_cp(("arbitrary",)))(y, g, res, target)
    return dx, dy, dg[0:1], l[0, 0]


def _gates(xr, wa, ba, wx, bx, lam):
    xb = xr.astype(BF16)
    r = _sigmoid(jnp.dot(xb, wa, preferred_element_type=F32) + ba)
    i = _sigmoid(jnp.dot(xb, wx, preferred_element_type=F32) + bx)
    log_a = LRU_C * r * _log_sigmoid(lam)
    a = jnp.exp(log_a)
    m = jnp.sqrt(-_expm1(2.0 * log_a))
    return r, i, a, m


def _mixer_fwd(proj, conv_a, conv_b, bias, wa_blk, ba, wx_blk, bx, lam):
    s, w5 = proj.shape
    w = w5 // 5
    nch = w // LANES
    ts = _pick(s, ROW_TILE)
    nt = s // ts

    def body(p_ref, pp_ref, ca_ref, cb_ref, bias_ref, wa_ref, ba_ref, wx_ref, bx_ref, lam_ref,
             y_ref, h_ref, a_scr, b_scr, hc_scr):
        t = pl.program_id(0)
        first = t == 0
        rows = lax.broadcasted_iota(jnp.int32, (ts, LANES), 0)

        @pl.when(first)
        def _():
            hc_scr[...] = jnp.zeros_like(hc_scr)

        def cur(comp, c):
            return p_ref[:, comp * w + c * LANES:comp * w + (c + 1) * LANES]

        def prev(comp, c):
            v = pp_ref[:, comp * w + c * LANES:comp * w + (c + 1) * LANES]
            return jnp.where(first, 0.0, v)

        for c in range(nch):
            sl = slice(c * LANES, (c + 1) * LANES)
            cx = cur(1, c) * cur(2, c)
            cxp = prev(1, c) * prev(2, c)
            wa3 = ca_ref[:, sl]
            conv = (wa3[2:3] * cx + wa3[1:2] * _shift_down(cx, cxp, 1, rows)
                    + wa3[0:1] * _shift_down(cx, cxp, 2, rows))
            y_ref[:, sl] = (cur(0, c) * conv).astype(BF16)

        for c in range(nch):
            sl = slice(c * LANES, (c + 1) * LANES)
            xb, xbp = cur(4, c), prev(4, c)
            wb4 = cb_ref[:, sl]
            xr = (wb4[3:4] * xb + wb4[2:3] * _shift_down(xb, xbp, 1, rows)
                  + wb4[1:2] * _shift_down(xb, xbp, 2, rows)
                  + wb4[0:1] * _shift_down(xb, xbp, 3, rows) + bias_ref[:, sl])
            _, i, a, m = _gates(xr, wa_ref[c], ba_ref[:, sl], wx_ref[c], bx_ref[:, sl], lam_ref[:, sl])
            a_scr[:, sl] = a
            b_scr[:, sl] = m * i * xr

        def step(r, h):
            h = a_scr[pl.ds(r, 1), :] * h + b_scr[pl.ds(r, 1), :]
            h_ref[pl.ds(r, 1), :] = h
            return h

        hc_scr[0:1, :] = lax.fori_loop(0, ts, step, hc_scr[0:1, :], unroll=8)

        for c in range(nch):
            sl = slice(c * LANES, (c + 1) * LANES)
            gel, _ = _gelu_and_grad(cur(3, c))
            y_ref[:, w + c * LANES:w + (c + 1) * LANES] = (h_ref[:, sl] * gel).astype(BF16)

    vec = lambda n: _whole((n, w))
    return pl.pallas_call(
        body, name="mixer_fwd", grid=(nt,),
        in_specs=[pl.BlockSpec((ts, w5), lambda t: (t, 0)),
                  pl.BlockSpec((SUBLANES, w5), lambda t: (jnp.maximum(t * (ts // SUBLANES) - 1, 0), 0)),
                  vec(3), vec(4), vec(1), _whole(wa_blk.shape), vec(1), _whole(wx_blk.shape), vec(1), vec(1)],
        out_specs=[pl.BlockSpec((ts, 2 * w), lambda t: (t, 0)), pl.BlockSpec((ts, w), lambda t: (t, 0))],
        out_shape=[jax.ShapeDtypeStruct((s, 2 * w), BF16), jax.ShapeDtypeStruct((s, w), F32)],
        scratch_shapes=[pltpu.VMEM((ts, w), F32), pltpu.VMEM((ts, w), F32), pltpu.VMEM((SUBLANES, w), F32)],
        compiler_params=_cp(("arbitrary",)),
    )(proj, proj, conv_a, conv_b, bias, wa_blk, ba, wx_blk, bx, lam)


_SG_CONV_A, _SG_CONV_B, _SG_BIAS, _SG_BA, _SG_BX, _SG_LAM, _SG_ROWS = 0, 3, 7, 8, 9, 10, 16


def _mixer_bwd(proj, hseq, dy, conv_a, conv_b, bias, wa_blk, ba, wx_blk, bx, lam):
    s, w5 = proj.shape
    w = w5 // 5
    nch = w // LANES
    ts = _pick(s, ROW_TILE)
    nt = s // ts
    tpb = ts // SUBLANES

    def body(p_ref, pp_ref, h_ref, hp_ref, dy_ref, ca_ref, cb_ref, bias_ref, wa_ref, ba_ref, wx_ref, bx_ref,
             lam_ref, dp_ref, xr_ref, dpa_ref, dpx_ref, sg_ref,
             a_scr, g_scr, l_scr, x_scr, r_scr, i_scr, m_scr, cl_scr, cdc_scr, cdx_scr):
        pid = pl.program_id(0)
        last = pid == 0
        first = pid == nt - 1
        rows = lax.broadcasted_iota(jnp.int32, (ts, LANES), 0)

        @pl.when(last)
        def _():
            sg_ref[...] = jnp.zeros_like(sg_ref)
            cl_scr[...] = jnp.zeros_like(cl_scr)
            cdc_scr[...] = jnp.zeros_like(cdc_scr)
            cdx_scr[...] = jnp.zeros_like(cdx_scr)

        def cur(comp, c):
            return p_ref[:, comp * w + c * LANES:comp * w + (c + 1) * LANES]

        def prev(comp, c):
            v = pp_ref[:, comp * w + c * LANES:comp * w + (c + 1) * LANES]
            return jnp.where(first, 0.0, v)

        def put(comp, c, v):
            dp_ref[:, comp * w + c * LANES:comp * w + (c + 1) * LANES] = v.astype(dp_ref.dtype)

        def acc(row, sl, v):
            sg_ref[row:row + 1, sl] += _colsum(v)

        for c in range(nch):
            sl = slice(c * LANES, (c + 1) * LANES)
            bg, cg, ax = cur(0, c), cur(1, c), cur(2, c)
            cx = cg * ax
            cxp = prev(1, c) * prev(2, c)
            cx1 = _shift_down(cx, cxp, 1, rows)
            cx2 = _shift_down(cx, cxp, 2, rows)
            wa3 = ca_ref[:, sl]
            conv = wa3[2:3] * cx + wa3[1:2] * cx1 + wa3[0:1] * cx2
            dya = dy_ref[:, sl]
            put(0, c, dya * conv)
            dconv = dya * bg
            nxt = cdc_scr[:, sl]
            dcx = (wa3[2:3] * dconv + wa3[1:2] * _shift_up(dconv, nxt, 1, rows)
                   + wa3[0:1] * _shift_up(dconv, nxt, 2, rows))
            cdc_scr[:, sl] = dconv[0:SUBLANES]
            put(1, c, dcx * ax)
            put(2, c, dcx * cg)
            acc(_SG_CONV_A + 2, sl, dconv * cx)
            acc(_SG_CONV_A + 1, sl, dconv * cx1)
            acc(_SG_CONV_A + 0, sl, dconv * cx2)

        for c in range(nch):
            sl = slice(c * LANES, (c + 1) * LANES)
            xb, xbp = cur(4, c), prev(4, c)
            wb4 = cb_ref[:, sl]
            xr = (wb4[3:4] * xb + wb4[2:3] * _shift_down(xb, xbp, 1, rows)
                  + wb4[1:2] * _shift_down(xb, xbp, 2, rows)
                  + wb4[0:1] * _shift_down(xb, xbp, 3, rows) + bias_ref[:, sl])
            r, i, a, m = _gates(xr, wa_ref[c], ba_ref[:, sl], wx_ref[c], bx_ref[:, sl], lam_ref[:, sl])
            gel, dgel = _gelu_and_grad(cur(3, c))
            dyb = dy_ref[:, w + c * LANES:w + (c + 1) * LANES]
            put(3, c, dyb * h_ref[:, sl] * dgel)
            g_scr[:, sl] = dyb * gel
            a_scr[:, sl] = a
            x_scr[:, sl] = xr
            r_scr[:, sl] = r
            i_scr[:, sl] = i
            m_scr[:, sl] = m

        def step(j, carry):
            r = ts - 1 - j
            lam_t = g_scr[pl.ds(r, 1), :] + carry
            l_scr[pl.ds(r, 1), :] = lam_t
            return a_scr[pl.ds(r, 1), :] * lam_t

        cl_scr[0:1, :] = lax.fori_loop(0, ts, step, cl_scr[0:1, :], unroll=8)

        for c in range(nch):
            sl = slice(c * LANES, (c + 1) * LANES)
            lam_t = l_scr[:, sl]
            hprev = _shift_down(h_ref[:, sl], jnp.where(first, 0.0, hp_ref[:, sl]), 1, rows)
            xr, r, i, m, a = x_scr[:, sl], r_scr[:, sl], i_scr[:, sl], m_scr[:, sl], a_scr[:, sl]
            da = lam_t * hprev
            dm = lam_t * i * xr
            di = lam_t * m * xr
            dxr = lam_t * m * i
            dlog_a = da * a - dm * a * a / m
            lam_p = lam_ref[:, sl]
            dr = dlog_a * (LRU_C * _log_sigmoid(lam_p))
            acc(_SG_LAM, sl, dlog_a * r * (LRU_C * _sigmoid(-lam_p)))
            dpa = dr * r * (1.0 - r)
            dpx = di * i * (1.0 - i)
            dpa_b, dpx_b = dpa.astype(BF16), dpx.astype(BF16)
            dxr = (dxr + lax.dot_general(dpa_b, wa_ref[c], _DIMS["nt"], preferred_element_type=F32)
                   + lax.dot_general(dpx_b, wx_ref[c], _DIMS["nt"], preferred_element_type=F32))
            xr_ref[:, sl] = xr.astype(BF16)
            dpa_ref[:, sl] = dpa_b
            dpx_ref[:, sl] = dpx_b
            acc(_SG_BA, sl, dpa)
            acc(_SG_BX, sl, dpx)
            acc(_SG_BIAS, sl, dxr)
            nxt = cdx_scr[:, sl]
            wb4 = cb_ref[:, sl]
            put(4, c, wb4[3:4] * dxr + wb4[2:3] * _shift_up(dxr, nxt, 1, rows)
                + wb4[1:2] * _shift_up(dxr, nxt, 2, rows) + wb4[0:1] * _shift_up(dxr, nxt, 3, rows))
            cdx_scr[:, sl] = dxr[0:SUBLANES]
            xb, xbp = cur(4, c), prev(4, c)
            acc(_SG_CONV_B + 3, sl, dxr * xb)
            acc(_SG_CONV_B + 2, sl, dxr * _shift_down(xb, xbp, 1, rows))
            acc(_SG_CONV_B + 1, sl, dxr * _shift_down(xb, xbp, 2, rows))
            acc(_SG_CONV_B + 0, sl, dxr * _shift_down(xb, xbp, 3, rows))

    blk = lambda width: pl.BlockSpec((ts, width), lambda p: (nt - 1 - p, 0))
    pre = lambda width: pl.BlockSpec(
        (SUBLANES, width), lambda p: (jnp.maximum((nt - 1 - p) * tpb - 1, 0), 0))
    vec = lambda n: _whole((n, w))
    big = lambda: pltpu.VMEM((ts, w), F32)
    small = lambda: pltpu.VMEM((SUBLANES, w), F32)
    return pl.pallas_call(
        body, name="mixer_bwd", grid=(nt,),
        in_specs=[blk(w5), pre(w5), blk(w), pre(w), blk(2 * w),
                  vec(3), vec(4), vec(1), _whole(wa_blk.shape), vec(1), _whole(wx_blk.shape), vec(1), vec(1)],
        out_specs=[blk(w5), blk(w), blk(w), blk(w), _whole((_SG_ROWS, w))],
        out_shape=[jax.ShapeDtypeStruct((s, w5), BF16), jax.ShapeDtypeStruct((s, w), BF16),
                   jax.ShapeDtypeStruct((s, w), BF16), jax.ShapeDtypeStruct((s, w), BF16),
                   jax.ShapeDtypeStruct((_SG_ROWS, w), F32)],
        scratch_shapes=[big(), big(), big(), big(), big(), big(), big(), small(), small(), small()],
        compiler_params=_cp(("arbitrary",)),
    )(proj, proj, hseq, hseq, dy, conv_a, conv_b, bias, wa_blk, ba, wx_blk, bx, lam)


def _split_dot(v, tri2):
    hi = v.astype(BF16)
    lo = (v - hi.astype(F32)).astype(BF16)
    return jnp.dot(jnp.concatenate([hi, lo], axis=1), tri2, preferred_element_type=F32)


def _tri(cmp):
    r = lax.broadcasted_iota(jnp.int32, (LANES, LANES), 0)
    c = lax.broadcasted_iota(jnp.int32, (LANES, LANES), 1)
    return cmp(r, c).astype(BF16)


def _lane_blocks(v):
    return [v[:, b * LANES:(b + 1) * LANES] for b in range(v.shape[1] // LANES)]


def _last_lane(v):
    return jnp.broadcast_to(v[:, LANES - 1:LANES], v.shape)


def _scores(q, kb, scale):
    return lax.dot_general(q, kb, _DIMS["nt"], preferred_element_type=F32) * scale


def _log_gates(z, diagonal):
    ls = jnp.minimum(z, 0.0) - jnp.log(1.0 + jnp.exp(-jnp.abs(z)))
    ln = ls - z
    valid = None
    if diagonal:
        valid = (lax.broadcasted_iota(jnp.int32, z.shape, 1) < lax.broadcasted_iota(jnp.int32, z.shape, 0))
        ln = jnp.where(valid, ln, 0.0)
    return ls, ln, valid


def _attn_fwd(qkv, heads):
    s = qkv.shape[0]
    dh = LANES
    tq = _pick(s, ATT_TILE)
    nq = s // tq
    nb = tq // LANES
    scale = 1.0 / math.sqrt(dh)

    hp = ATT_FWD_HEADS_PER_STEP
    groups = heads // hp
    wid = hp * dh

    def body(q_ref, k_ref, v_ref, o_ref, tot_ref, acc_scr, car_scr):
        qi = pl.program_id(1)
        acc_scr[...] = jnp.zeros_like(acc_scr)
        car_scr[...] = jnp.zeros_like(car_scr)
        tri = _tri(lambda r, c: r > c)
        tri = jnp.concatenate([tri, tri], axis=0)

        def tile(kt, diagonal):
            k0 = pl.multiple_of(kt * tq, tq)
            heads_cols = [slice(hh * dh, (hh + 1) * dh) for hh in range(hp)]
            zs = [_scores(q_ref[:, cols], k_ref[pl.ds(k0, tq), cols], scale) for cols in heads_cols]
            gates = [_log_gates(z, diagonal) for z in zs]
            sfxs = [_split_dot(jnp.concatenate(_lane_blocks(ln), axis=0), tri) for _, ln, _ in gates]
            for cols, (ls, ln, valid), sfx in zip(heads_cols, gates, sfxs):
                blocks = _lane_blocks(ln)
                car = car_scr[:, cols]
                parts = [None] * nb
                for b in reversed(range(nb)):
                    sb = sfx[b * tq:(b + 1) * tq]
                    parts[b] = sb + car
                    car = car + (sb[:, 0:1] + blocks[b][:, 0:1])
                car_scr[:, cols] = car
                wgt = jnp.exp(ls + jnp.concatenate(parts, axis=1))
                if diagonal:
                    wgt = jnp.where(valid, wgt, 0.0)
                acc_scr[:, cols] += jnp.dot(
                    wgt.astype(BF16), v_ref[pl.ds(k0, tq), cols], preferred_element_type=F32)

        tile(qi, True)

        def step(j, carry):
            tile(qi - 1 - j, False)
            return carry

        lax.fori_loop(0, qi, step, 0)
        o_ref[...] = acc_scr[...].astype(BF16)
        tot_ref[...] = car_scr[...]

    return pl.pallas_call(
        body, name="attn_fwd", grid=(groups, nq),
        in_specs=[pl.BlockSpec((tq, wid), lambda h, i: (i, h)),
                  pl.BlockSpec((s, wid), lambda h, i: (0, groups + h)),
                  pl.BlockSpec((s, wid), lambda h, i: (0, 2 * groups + h))],
        out_specs=[pl.BlockSpec((tq, wid), lambda h, i: (i, h)), pl.BlockSpec((tq, wid), lambda h, i: (i, h))],
        out_shape=[jax.ShapeDtypeStruct((s, heads * dh), BF16), jax.ShapeDtypeStruct((s, heads * dh), F32)],
        scratch_shapes=[pltpu.VMEM((tq, wid), F32), pltpu.VMEM((tq, wid), F32)],
        compiler_params=_cp(("parallel", "arbitrary")),
    )(qkv, qkv, qkv)


def _attn_bwd(qkv, tot, do, heads):
    s = qkv.shape[0]
    dh = LANES
    tq = _pick(s, ATT_TILE)
    nq = s // tq
    nb = tq // LANES
    scale = 1.0 / math.sqrt(dh)

    hp = ATT_HEADS_PER_STEP
    groups = heads // hp
    wid = hp * dh

    def body(q_ref, k_ref, v_ref, tot_ref, do_ref, dq_ref, dk_ref, dv_ref,
             dq_scr, dk_scr, dv_scr, cl_scr, cg_scr):
        qi = pl.program_id(1)

        @pl.when(qi == 0)
        def _():
            dk_scr[...] = jnp.zeros_like(dk_scr)
            dv_scr[...] = jnp.zeros_like(dv_scr)

        dq_scr[...] = jnp.zeros_like(dq_scr)
        cl_scr[...] = jnp.zeros_like(cl_scr)
        cg_scr[...] = jnp.zeros_like(cg_scr)
        tri_le = _tri(lambda r, c: r <= c)
        tri_le = jnp.concatenate([tri_le, tri_le], axis=0)
        tri_lt = _tri(lambda r, c: r < c)

        def tile(kt, diagonal):
            k0 = pl.multiple_of(kt * tq, tq)
            heads_cols = [slice(hh * dh, (hh + 1) * dh) for hh in range(hp)]
            keys = pl.ds(k0, tq)
            zs = [_scores(q_ref[:, cols], k_ref[keys, cols], scale) for cols in heads_cols]
            dws = [lax.dot_general(do_ref[:, cols], v_ref[keys, cols], _DIMS["nt"], preferred_element_type=F32)
                   for cols in heads_cols]
            gates = [_log_gates(z, diagonal) for z in zs]
            pins = [_split_dot(jnp.concatenate(_lane_blocks(ln), axis=0), tri_le) for _, ln, _ in gates]
            wgts, gs = [], []
            for cols, (ls, _, valid), pin, dw in zip(heads_cols, gates, pins, dws):
                total = tot_ref[:, cols]
                cl = cl_scr[:, cols]
                parts = []
                for b in range(nb):
                    pb = pin[b * tq:(b + 1) * tq] + cl
                    parts.append(total - pb)
                    cl = _last_lane(pb)
                cl_scr[:, cols] = cl
                wgt = jnp.exp(ls + jnp.concatenate(parts, axis=1))
                if diagonal:
                    wgt = jnp.where(valid, wgt, 0.0)
                wgts.append(wgt)
                gs.append(wgt * dw)
            pexs = [jnp.dot(jnp.concatenate(_lane_blocks(g), axis=0).astype(BF16), tri_lt,
                            preferred_element_type=F32) for g in gs]
            for cols, wgt in zip(heads_cols, wgts):
                dv_scr[keys, cols] += lax.dot_general(
                    wgt.astype(BF16), do_ref[:, cols], _DIMS["tn"], preferred_element_type=F32)
            for cols, (ls, _, valid), g, pex in zip(heads_cols, gates, gs, pexs):
                gblocks = _lane_blocks(g)
                cg = cg_scr[:, cols]
                parts = []
                for b in range(nb):
                    pb = pex[b * tq:(b + 1) * tq] + cg
                    parts.append(pb)
                    cg = _last_lane(pb + gblocks[b])
                cg_scr[:, cols] = cg
                dz = g - jnp.exp(ls) * (g + jnp.concatenate(parts, axis=1))
                if diagonal:
                    dz = jnp.where(valid, dz, 0.0)
                dz = dz.astype(BF16)
                dq_scr[:, cols] += jnp.dot(dz, k_ref[keys, cols], preferred_element_type=F32)
                dk_scr[keys, cols] += lax.dot_general(
                    dz, q_ref[:, cols], _DIMS["tn"], preferred_element_type=F32)

        def step(j, carry):
            tile(j, False)
            return carry

        lax.fori_loop(0, qi, step, 0)
        tile(qi, True)
        dq_ref[...] = (dq_scr[...] * scale).astype(BF16)

        @pl.when(qi == nq - 1)
        def _():
            dk_ref[...] = (dk_scr[...] * scale).astype(BF16)
            dv_ref[...] = dv_scr[...].astype(BF16)

    qblk = pl.BlockSpec((tq, wid), lambda h, i: (i, h))
    hblk = pl.BlockSpec((s, wid), lambda h, i: (0, h))
    out = jax.ShapeDtypeStruct((s, heads * dh), BF16)
    return pl.pallas_call(
        body, name="attn_bwd", grid=(groups, nq),
        in_specs=[qblk, pl.BlockSpec((s, wid), lambda h, i: (0, groups + h)),
                  pl.BlockSpec((s, wid), lambda h, i: (0, 2 * groups + h)), qblk, qblk],
        out_specs=[qblk, hblk, hblk], out_shape=[out, out, out],
        scratch_shapes=[pltpu.VMEM((tq, wid), F32), pltpu.VMEM((s, wid), F32), pltpu.VMEM((s, wid), F32),
                        pltpu.VMEM((tq, wid), F32), pltpu.VMEM((tq, wid), F32)],
        compiler_params=_cp(("parallel", "arbitrary")),
    )(qkv, qkv, qkv, tot, do)


def _place():
    x, y, c = lax.axis_index("x"), lax.axis_index("y"), lax.axis_index("c")
    chips = [(1 - x, y), (x, 1 - y), (1 - x, 1 - y)]
    return x, y, c, chips


def _remote(src, dst, send_sem, recv_sem, dev):
    return pltpu.make_async_remote_copy(
        src_ref=src, dst_ref=dst, send_sem=send_sem, recv_sem=recv_sem, device_id=dev, device_id_type=MESH)


def _handshake(peers):
    barrier = pltpu.get_barrier_semaphore()
    for dev in peers:
        pl.semaphore_signal(barrier, inc=1, device_id=dev, device_id_type=MESH)
    pl.semaphore_wait(barrier, len(peers))


def _sequencer_kernel(name, n_sems, collective_id):
    return functools.partial(
        pl.kernel, mesh=plsc.ScalarSubcoreMesh(axis_name="seq", num_cores=1), name=name,
        scratch_types=(pltpu.SemaphoreType.DMA,) * n_sems,
        compiler_params=pltpu.CompilerParams(collective_id=collective_id))


def _allgather_async(name, slot_buf, collective_id):
    buf = jax.new_ref(slot_buf, memory_space=pltpu.MemorySpace.HBM)
    hr = slot_buf.shape[1] // 2

    @_sequencer_kernel(name, 12, collective_id)
    def launch(*sems):
        send_sems, recv_sems, fsend_sems, frecv_sems = sems[0:3], sems[3:6], sems[6:9], sems[9:12]
        x, y, c, chips = _place()
        me = 2 * x + y
        sibling = (x, y, 1 - c)
        _handshake([(px, py, c) for px, py in chips] + [sibling])
        mine = buf.at[me, pl.ds(c * hr, hr)]
        firsts = []
        for k, (px, py) in enumerate(chips):
            cp = _remote(mine, mine, send_sems[k], recv_sems[k], (px, py, c))
            cp.start()
            firsts.append(cp)
        passed = []
        for k, (px, py) in enumerate(chips):
            slot = buf.at[2 * px + py, pl.ds(c * hr, hr)]
            _remote(slot, slot, send_sems[k], recv_sems[k], (px, py, c)).wait_recv()
            cp = _remote(slot, slot, fsend_sems[k], frecv_sems[k], sibling)
            cp.start()
            passed.append(cp)
        for k, (px, py) in enumerate(chips):
            slot = buf.at[2 * px + py, pl.ds((1 - c) * hr, hr)]
            _remote(slot, slot, fsend_sems[k], frecv_sems[k], sibling).wait_recv()
        for cp in firsts + passed:
            cp.wait_send()

    launch()
    return buf[...]


def _to_sibling_async(name, slab):
    src = jax.new_ref(slab, memory_space=pltpu.MemorySpace.HBM)
    hr = slab.shape[1] // 2
    got = jax.empty_ref(jax.ShapeDtypeStruct((N_CHIPS, hr, slab.shape[2]), slab.dtype),
                        memory_space=pltpu.MemorySpace.HBM)

    @_sequencer_kernel(name, 2, COLLECTIVE_SIBLING)
    def launch(send_sem, recv_sem):
        x, y, c, _ = _place()
        _handshake([(x, y, 1 - c)])
        _remote(src.at[:, pl.ds((1 - c) * hr, hr), :], got, send_sem, recv_sem, (x, y, 1 - c)).start()
        _remote(got, got, send_sem, recv_sem, (x, y, 1 - c)).wait()

    launch()
    return src[...], got[...]


def _swap_with_sibling_async(name, part):
    src = jax.new_ref(part, memory_space=pltpu.MemorySpace.HBM)
    got = jax.empty_ref(jax.ShapeDtypeStruct(part.shape, part.dtype), memory_space=pltpu.MemorySpace.HBM)

    @_sequencer_kernel(name, 2, COLLECTIVE_SIBLING)
    def launch(send_sem, recv_sem):
        x, y, c, _ = _place()
        _handshake([(x, y, 1 - c)])
        cp = _remote(src, got, send_sem, recv_sem, (x, y, 1 - c))
        cp.start()
        cp.wait()

    launch()
    return got[...]


def _to_chips_async(name, part):
    src = jax.new_ref(part, memory_space=pltpu.MemorySpace.HBM)
    got = jax.empty_ref(jax.ShapeDtypeStruct((3,) + part.shape[1:], part.dtype), memory_space=pltpu.MemorySpace.HBM)

    @_sequencer_kernel(name, 6, COLLECTIVE_CHIPS)
    def launch(*sems):
        send_sems, recv_sems = sems[0:3], sems[3:6]
        x, y, c, chips = _place()
        _handshake([(px, py, c) for px, py in chips])
        cps = []
        for k, (px, py) in enumerate(chips):
            cp = _remote(src.at[2 * px + py], got.at[k], send_sems[k], recv_sems[k], (px, py, c))
            cp.start()
            cps.append(cp)
        for cp in cps:
            cp.wait()

    launch()
    return src[...], got[...]


def _join_sibling_async(name, half_filled):
    buf = jax.new_ref(half_filled, memory_space=pltpu.MemorySpace.HBM)
    hr = half_filled.shape[0] // 2

    @_sequencer_kernel(name, 2, COLLECTIVE_SIBLING)
    def launch(send_sem, recv_sem):
        x, y, c, _ = _place()
        _handshake([(x, y, 1 - c)])
        mine = buf.at[pl.ds(c * hr, hr)]
        other = buf.at[pl.ds((1 - c) * hr, hr)]
        cp = _remote(mine, mine, send_sem, recv_sem, (x, y, 1 - c))
        cp.start()
        _remote(other, other, send_sem, recv_sem, (x, y, 1 - c)).wait_recv()
        cp.wait_send()

    launch()
    return buf[...]


def _allgather_chips_small(name, v):
    r = v.shape[0]

    def body(v_ref, o_ref, send_sems, recv_sems):
        x, y, c, chips = _place()
        me = 2 * x + y
        o_ref[me] = v_ref[...]
        cps = []
        for k, (px, py) in enumerate(chips):
            cp = _remote(v_ref, o_ref.at[me], send_sems.at[k], recv_sems.at[k], (px, py, c))
            cp.start()
            cps.append(cp)
        for k, (px, py) in enumerate(chips):
            slot = o_ref.at[2 * px + py]
            _remote(slot, slot, send_sems.at[k], recv_sems.at[k], (px, py, c)).wait_recv()
        for cp in cps:
            cp.wait_send()

    return pl.pallas_call(
        body, name=name, in_specs=[pl.BlockSpec(memory_space=pltpu.VMEM)],
        out_specs=pl.BlockSpec(memory_space=pltpu.VMEM),
        out_shape=jax.ShapeDtypeStruct((N_CHIPS, r, LANES), F32),
        scratch_shapes=[pltpu.SemaphoreType.DMA((3,)), pltpu.SemaphoreType.DMA((3,))],
    )(v)


def _allreduce_small(name, v):
    r = v.shape[0]
    hr = r // 2
    assert hr % SUBLANES == 0

    def body(v_ref, o_ref, sib_ref, chips_ref, send_sems, recv_sems):
        x, y, c, chips = _place()
        me = 2 * x + y
        sibling = (x, y, 1 - c)
        first = _remote(v_ref, sib_ref, send_sems.at[0], recv_sems.at[0], sibling)
        first.start()
        first.wait()
        mine = pl.ds(pl.multiple_of(c * hr, SUBLANES), hr)
        chips_ref[me] = v_ref[mine, :] + sib_ref[mine, :]
        cps = []
        for k, (px, py) in enumerate(chips):
            cp = _remote(chips_ref.at[me], chips_ref.at[me], send_sems.at[1 + k], recv_sems.at[1 + k], (px, py, c))
            cp.start()
            cps.append(cp)
        for k, (px, py) in enumerate(chips):
            slot = chips_ref.at[2 * px + py]
            _remote(slot, slot, send_sems.at[1 + k], recv_sems.at[1 + k], (px, py, c)).wait_recv()
        total = chips_ref[0]
        for j in range(1, N_CHIPS):
            total = total + chips_ref[j]
        o_ref[mine, :] = total
        last = _remote(o_ref.at[mine], o_ref.at[mine], send_sems.at[4], recv_sems.at[4], sibling)
        last.start()
        other = o_ref.at[pl.ds(pl.multiple_of((1 - c) * hr, SUBLANES), hr)]
        _remote(other, other, send_sems.at[4], recv_sems.at[4], sibling).wait_recv()
        last.wait_send()
        for cp in cps:
            cp.wait_send()

    return pl.pallas_call(
        body, name=name, in_specs=[pl.BlockSpec(memory_space=pltpu.VMEM)],
        out_specs=pl.BlockSpec(memory_space=pltpu.VMEM),
        out_shape=jax.ShapeDtypeStruct((r, LANES), F32),
        scratch_shapes=[pltpu.VMEM((r, LANES), F32), pltpu.VMEM((N_CHIPS, hr, LANES), F32),
                        pltpu.SemaphoreType.DMA((5,)), pltpu.SemaphoreType.DMA((5,))],
    )(v)


def _add_sibling(name, slabs, recv, c):
    _, r, cols = slabs.shape
    hr = r // 2
    tr = _pick(hr, STREAM_TILE)
    nb = hr // tr

    def body(c_ref, a_ref, b_ref, o_ref):
        o_ref[...] = (a_ref[...].astype(F32) + b_ref[...].astype(F32)).astype(BF16)

    grid_spec = pltpu.PrefetchScalarGridSpec(
        num_scalar_prefetch=1, grid=(N_CHIPS, nb),
        in_specs=[pl.BlockSpec((None, tr, cols), lambda j, i, c_ref: (j, c_ref[0] * nb + i, 0)),
                  pl.BlockSpec((None, tr, cols), lambda j, i, c_ref: (j, i, 0))],
        out_specs=pl.BlockSpec((None, tr, cols), lambda j, i, c_ref: (j, i, 0)))
    return pl.pallas_call(
        body, name=name, grid_spec=grid_spec,
        out_shape=jax.ShapeDtypeStruct((N_CHIPS, hr, cols), BF16),
        compiler_params=_cp(("parallel", "parallel")))(jnp.reshape(c, (1,)).astype(jnp.int32), slabs, recv)


def _sum_chips(name, own, recv, chip, c):
    _, hr, cols = recv.shape
    tr = _pick(hr, STREAM_TILE // 2)
    nb = hr // tr

    def body(sc_ref, own_ref, recv_ref, o_ref):
        total = own_ref[...].astype(F32)
        for k in range(3):
            total = total + recv_ref[k].astype(F32)
        o_ref[...] = total

    grid_spec = pltpu.PrefetchScalarGridSpec(
        num_scalar_prefetch=1, grid=(nb,),
        in_specs=[pl.BlockSpec((None, tr, cols), lambda i, sc: (sc[0], i, 0)),
                  pl.BlockSpec((3, tr, cols), lambda i, sc: (0, i, 0))],
        out_specs=pl.BlockSpec((tr, cols), lambda i, sc: (sc[1] * nb + i, 0)))
    return pl.pallas_call(
        body, name=name, grid_spec=grid_spec, out_shape=jax.ShapeDtypeStruct((2 * hr, cols), F32),
        compiler_params=_cp(("parallel",)))(jnp.stack([chip, c]).astype(jnp.int32), own, recv)


def _adamw_math(w, g, m, v):
    m = ADAM_B1 * m + (1.0 - ADAM_B1) * g
    v = ADAM_B2 * v + (1.0 - ADAM_B2) * (g * g)
    m_hat = m / (1.0 - ADAM_B1 ** ADAM_STEP)
    v_hat = v / (1.0 - ADAM_B2 ** ADAM_STEP)
    delta = -ADAM_LR * (m_hat / (jnp.sqrt(v_hat) + ADAM_EPS) + ADAM_WD * w)
    return delta, m, v


def _adamw(name, w, gs, m, v):
    nl, r, cols = w.shape
    tr = _pick(r, ROW_TILE)

    def body(*refs):
        w_ref, m_ref, v_ref = refs[0:3]
        g_refs = refs[3:3 + nl]
        go_ref, d_ref, nm_ref, nv_ref = refs[3 + nl:]
        layer = pl.program_id(0)
        g = g_refs[0][...]
        for j in range(1, nl):
            g = jnp.where(layer == j, g_refs[j][...], g)
        d, nm, nv = _adamw_math(w_ref[...], g, m_ref[...], v_ref[...])
        go_ref[...] = g
        d_ref[...] = d
        nm_ref[...] = nm
        nv_ref[...] = nv

    spec3 = pl.BlockSpec((None, tr, cols), lambda l, i: (l, i, 0))
    gspec = pl.BlockSpec((tr, cols), lambda l, i: (i, 0))
    out = jax.ShapeDtypeStruct((nl, r, cols), F32)
    return pl.pallas_call(
        body, name=name, grid=(nl, r // tr), in_specs=[spec3] * 3 + [gspec] * nl, out_specs=[spec3] * 4,
        out_shape=[out] * 4, compiler_params=_cp(("parallel", "parallel")))(w, m, v, *gs)


def _adamw_small(name, groups):
    n = len(groups)
    flat = [a for grp in groups for a in grp]

    def body(*refs):
        ins, outs = refs[:4 * n], refs[4 * n:]
        for p in range(n):
            w_ref, g_ref, m_ref, v_ref = ins[4 * p:4 * p + 4]
            d, nm, nv = _adamw_math(w_ref[...], g_ref[...], m_ref[...], v_ref[...])
            outs[3 * p][...] = d
            outs[3 * p + 1][...] = nm
            outs[3 * p + 2][...] = nv

    vm = pl.BlockSpec(memory_space=pltpu.VMEM)
    out_shape = [jax.ShapeDtypeStruct(grp[0].shape, F32) for grp in groups for _ in range(3)]
    res = pl.pallas_call(
        body, name=name, in_specs=[vm] * (4 * n), out_specs=[vm] * (3 * n), out_shape=out_shape)(*flat)
    return [tuple(res[3 * p:3 * p + 3]) for p in range(n)]


def _block_diag_pairs(w):
    h, d, _ = w.shape
    z = jnp.zeros((h // 2, d, d), w.dtype)
    top = jnp.concatenate([w[0::2], z], axis=2)
    bot = jnp.concatenate([z, w[1::2]], axis=2)
    return jnp.concatenate([top, bot], axis=1).astype(BF16)


def _diag_pairs_to_heads(g, d):
    a = g[:, :d, :d]
    b = g[:, d:, d:]
    return jnp.stack([a, b], axis=1).reshape(-1, d, d)


def _rows128(a):
    flat = a.reshape(-1, LANES)
    pad = (-flat.shape[0]) % SUBLANES
    if pad:
        flat = jnp.concatenate([flat, jnp.zeros((pad, LANES), flat.dtype)], axis=0)
    return flat


def _unshard_last(g4, shape):
    g4 = g4.reshape((N_CHIPS,) + tuple(shape))
    return jnp.concatenate([g4[j] for j in range(N_CHIPS)], axis=-1)


def kernel(x, norm_gains, hyb_w_in, hyb_conv_a, hyb_conv_b, hyb_conv_b_bias, hyb_rg_w_a, hyb_rg_b_a, hyb_rg_w_x, hyb_rg_b_x, hyb_rg_lambda, hyb_w_out, sb_w_qkv, sb_w_o, mlp_w_up, mlp_w_down, loss_target, m_norm_gains, m_hyb_w_in, m_hyb_conv_a, m_hyb_conv_b, m_hyb_conv_b_bias, m_hyb_rg_w_a, m_hyb_rg_b_a, m_hyb_rg_w_x, m_hyb_rg_b_x, m_hyb_rg_lambda, m_hyb_w_out, m_sb_w_qkv, m_sb_w_o, m_mlp_w_up, m_mlp_w_down, v_norm_gains, v_hyb_w_in, v_hyb_conv_a, v_hyb_conv_b, v_hyb_conv_b_bias, v_hyb_rg_w_a, v_hyb_rg_b_a, v_hyb_rg_w_x, v_hyb_rg_b_x, v_hyb_rg_lambda, v_hyb_w_out, v_sb_w_qkv, v_sb_w_o, v_mlp_w_up, v_mlp_w_down):
    cx_ = lax.axis_index("x")
    cy_ = lax.axis_index("y")
    cc_ = lax.axis_index("c")
    chip = 2 * cx_ + cy_

    x0 = x[0]
    target = loss_target[0]
    s, d = x0.shape
    heads = SB_HEADS
    assert d // heads == LANES
    n_rg, hd = hyb_rg_w_a.shape[1], hyb_rg_w_a.shape[2]
    wmix = n_rg * hd
    assert 2 * hd == LANES

    big = {
        "hyb_w_in": (hyb_w_in, 0), "hyb_w_out": (hyb_w_out, 0), "mlp_w_up0": (mlp_w_up, 0),
        "mlp_w_down0": (mlp_w_down, 0), "sb_w_qkv": (sb_w_qkv, 0), "sb_w_o": (sb_w_o, 0),
        "mlp_w_up1": (mlp_w_up, 1), "mlp_w_down1": (mlp_w_down, 1),
    }
    names = list(big)
    slots = [_cast_into_slot("cast_" + k, big[k][0], big[k][1], chip) for k in names]
    full = {k: _allgather_async("allgather_" + k, slot, cid) for cid, (k, slot) in enumerate(zip(names, slots))}
    rowsharded = lambda k: full[k].reshape(-1, full[k].shape[2])

    ng_s, ca_s, cb_s = norm_gains.reshape(-1, norm_gains.shape[2]), hyb_conv_a[0], hyb_conv_b[0]
    packed = jnp.concatenate([_rows128(ng_s), _rows128(ca_s), _rows128(cb_s)], axis=0)
    gathered = _allgather_chips_small("allgather_small", packed)
    n0 = ng_s.size // LANES
    n1 = n0 + (-n0) % SUBLANES
    m0 = ca_s.size // LANES
    m1 = m0 + (-m0) % SUBLANES
    k0 = cb_s.size // LANES
    gains = _unshard_last(gathered[:, 0:n0], ng_s.shape).reshape(2, 4, 1, d)
    conv_a = _unshard_last(gathered[:, n1:n1 + m0], ca_s.shape)
    conv_b = _unshard_last(gathered[:, n1 + m1:n1 + m1 + k0], cb_s.shape)
    bias, b_a, b_x, lam = hyb_conv_b_bias, hyb_rg_b_a, hyb_rg_b_x, hyb_rg_lambda
    wa_blk = _block_diag_pairs(hyb_rg_w_a[0])
    wx_blk = _block_diag_pairs(hyb_rg_w_x[0])

    relu = lambda acc: (jnp.maximum(acc, 0.0),)

    h1 = _rms_fwd("rms_pre0", x0, gains[0, 0])
    proj = _mm_fwd_col("proj_in", h1, full["hyb_w_in"])[0]
    ycat, hseq = _mixer_fwd(proj, conv_a, conv_b, bias, wa_blk, b_a, wx_blk, b_x, lam)
    mix0 = _mm_fwd_row("proj_out", ycat, rowsharded("hyb_w_out"))
    x1, h2 = _rms_post("rms_mix0", mix0, gains[0, 1], x0, gains[0, 2])
    u0 = _mm_fwd_col("mlp_up0", h2, full["mlp_w_up0"], (BF16,), relu)[0]
    mlp0 = _mm_fwd_row("mlp_down0", u0, rowsharded("mlp_w_down0"), a_fn=jnp.square)
    x2, h3 = _rms_post("rms_mlp0", mlp0, gains[0, 3], x1, gains[1, 0])

    qkv = _mm_fwd_col("qkv", h3, full["sb_w_qkv"], (BF16,))[0]
    att, tot = _attn_fwd(qkv, heads)
    mix1 = _mm_fwd_row("attn_out", att, rowsharded("sb_w_o"))
    x3, h4 = _rms_post("rms_mix1", mix1, gains[1, 1], x2, gains[1, 2])
    u1 = _mm_fwd_col("mlp_up1", h4, full["mlp_w_up1"], (BF16,), relu)[0]
    mlp1 = _mm_fwd_row("mlp_down1", u1, rowsharded("mlp_w_down1"), a_fn=jnp.square)
    dy, dmlp1, dgain_mlp1, loss_local = _last_norm_and_loss("last_norm_loss", mlp1, gains[1, 3], x3, target)
    loss = lax.psum(loss_local, ("x", "y", "c"))

    dgain = [[None] * 4 for _ in range(2)]
    drelu = lambda acc, u: (acc * (2.0 * u.astype(F32)),)
    stage_a, stage_b, gfull = {}, {}, {}

    def tie(main, side):
        return lax.optimization_barrier((main, side))

    def reduce_start(k, slab, main):
        main, slab = tie(main, slab)
        stage_a[k] = _to_sibling_async("grads_to_sibling_" + k, slab)
        return main

    def reduce_to_chips(k, main):
        slab, from_sibling = stage_a.pop(k)
        main, part = tie(main, _add_sibling("grads_add_" + k, slab, from_sibling, cc_))
        stage_b[k] = _to_chips_async("grads_to_chips_" + k, part)
        return main

    def reduce_split_start(k, act, dy, cs, main, a_fn=None):
        main, other = tie(main, _mm_wgrad_half(k + "_wgrad_sibling_rows", act, dy, 1 - cc_, cs, a_fn=a_fn))
        stage_a[k] = (act, dy, cs, a_fn, _swap_with_sibling_async("grads_to_sibling_" + k, other))
        return main

    def reduce_split_to_chips(k, main):
        act, dy, cs, a_fn, from_sibling = stage_a.pop(k)
        main, part = tie(main, _mm_wgrad_half(k + "_wgrad_my_rows", act, dy, cc_, cs, init=from_sibling, a_fn=a_fn))
        stage_b[k] = _to_chips_async("grads_to_chips_" + k, part)
        return main

    def after(value, token):
        return tie(value, token)[0]

    def reduce_finish(k, main):
        own, from_chips = stage_b.pop(k)
        main, half = tie(main, _sum_chips("grads_sum_" + k, after(own, main), from_chips, chip, cc_))
        gfull[k] = _join_sibling_async("grads_join_" + k, half)
        return main

    def mlp_bwd(layer, dxo, dmlp, xin, hin, u, mix):
        down, up = f"mlp_w_down{layer}", f"mlp_w_up{layer}"
        wd, wu = rowsharded(down), full[up]
        dmlp = reduce_split_start(down, u, dmlp, None, dmlp, a_fn=jnp.square)
        du = _mm_bwd_row(f"mlp_down{layer}_bwd", dmlp, wd, (BF16,), u, drelu)[0]
        du = reduce_split_start(up, hin, du, wu.shape[2], du)
        du = reduce_split_to_chips(down, du)
        dh = _mm_bwd_col(f"mlp_up{layer}_bwd", du, wu)
        dh = reduce_split_to_chips(up, dh)
        dxm, dgain[layer][2], dmix, dgain[layer][1] = _rms_bwd_pair(
            f"rms_premlp{layer}_mix{layer}_bwd", xin, gains[layer, 2], dh, dxo, mix, gains[layer, 1])
        return dxm, dmix

    dgain[1][3] = dgain_mlp1
    dx3, dmix1 = mlp_bwd(1, dy, dmlp1, x3, h4, u1, mix1)
    dmix1 = reduce_start("sb_w_o", _mm_wgrad_row("attn_out_wgrad", att, dmix1).reshape(N_CHIPS, -1, d), dmix1)
    datt = _mm_bwd_row("attn_out_bwd", dmix1, rowsharded("sb_w_o"), (BF16,))[0]
    dq, dk, dv = _attn_bwd(qkv, tot, datt, heads)
    dqkv = jnp.concatenate([dq, dk, dv], axis=1)
    dqkv = reduce_to_chips("sb_w_o", dqkv)
    dqkv = reduce_finish("mlp_w_down1", dqkv)
    dqkv = reduce_finish("mlp_w_up1", dqkv)
    dqkv = reduce_split_start("sb_w_qkv", h3, dqkv, full["sb_w_qkv"].shape[2], dqkv)
    dh3 = _mm_bwd_col("qkv_bwd", dqkv, full["sb_w_qkv"])
    dh3 = reduce_split_to_chips("sb_w_qkv", dh3)
    dx2, dgain[1][0], dmlp0, dgain[0][3] = _rms_bwd_pair(
        "rms_pre1_mlp0_bwd", x2, gains[1, 0], dh3, dx3, mlp0, gains[0, 3])

    dx1, dmix0 = mlp_bwd(0, dx2, dmlp0, x1, h2, u0, mix0)
    dmix0 = reduce_finish("sb_w_o", dmix0)
    dmix0 = reduce_finish("sb_w_qkv", dmix0)
    dmix0 = reduce_finish("mlp_w_down0", dmix0)
    dmix0 = reduce_start("hyb_w_out", _mm_wgrad_row("proj_out_wgrad", ycat, dmix0).reshape(N_CHIPS, -1, d), dmix0)
    dycat = _mm_bwd_row("proj_out_bwd", dmix0, rowsharded("hyb_w_out"))[0]
    dproj, xr_b, dpa_b, dpx_b, sg = _mixer_bwd(
        proj, hseq, dycat, conv_a, conv_b, bias, wa_blk, b_a, wx_blk, b_x, lam)
    dproj = reduce_finish("mlp_w_up0", dproj)
    dproj = reduce_to_chips("hyb_w_out", dproj)
    dproj = reduce_split_start("hyb_w_in", h1, dproj, full["hyb_w_in"].shape[2], dproj)
    dh1 = _mm_bwd_col("proj_in_bwd", dproj, full["hyb_w_in"])
    dh1 = reduce_split_to_chips("hyb_w_in", dh1)
    dx0, dgain[0][0] = _rms_bwd("rms_pre0_bwd", x0, gains[0, 0], dh1, res=dx1)
    dwa = _diag_pairs_to_heads(_mm_wgrad_diag("rg_w_a_wgrad", xr_b, dpa_b), hd)
    dwx = _diag_pairs_to_heads(_mm_wgrad_diag("rg_w_x_wgrad", xr_b, dpx_b), hd)

    dgains = jnp.concatenate([dgain[l][k] for l in range(2) for k in range(4)], axis=0)
    small_parts = [dgains, sg[_SG_CONV_A:_SG_CONV_A + 3], sg[_SG_CONV_B:_SG_CONV_B + 4], sg[_SG_BIAS:_SG_BIAS + 1],
                   dwa, sg[_SG_BA:_SG_BA + 1], dwx, sg[_SG_BX:_SG_BX + 1], sg[_SG_LAM:_SG_LAM + 1]]
    small_rows = [_rows128(p) for p in small_parts]
    n_small = sum(rws.shape[0] for rws in small_rows)
    tail_pad = [jnp.zeros(((-n_small) % (2 * SUBLANES), LANES), F32)] if n_small % (2 * SUBLANES) else []
    reduced = _allreduce_small("allreduce_small", jnp.concatenate(small_rows + tail_pad, axis=0))
    small_full, off = [], 0
    for p, rws in zip(small_parts, small_rows):
        small_full.append(reduced[off:off + p.size // LANES].reshape(p.shape))
        off += rws.shape[0]
    g_gains, g_ca, g_cb, g_bias, g_wa, g_ba, g_wx, g_bx, g_lam = small_full

    def my_cols(g, width):
        return lax.dynamic_slice_in_dim(g, chip * width, width, axis=g.ndim - 1)

    small = [
        ("norm_gains", norm_gains, my_cols(g_gains, norm_gains.shape[2]).reshape(norm_gains.shape),
         m_norm_gains, v_norm_gains),
        ("hyb_conv_a", hyb_conv_a, my_cols(g_ca, hyb_conv_a.shape[2])[None], m_hyb_conv_a, v_hyb_conv_a),
        ("hyb_conv_b", hyb_conv_b, my_cols(g_cb, hyb_conv_b.shape[2])[None], m_hyb_conv_b, v_hyb_conv_b),
        ("hyb_conv_b_bias", hyb_conv_b_bias, g_bias, m_hyb_conv_b_bias, v_hyb_conv_b_bias),
        ("hyb_rg_w_a", hyb_rg_w_a, g_wa[None], m_hyb_rg_w_a, v_hyb_rg_w_a),
        ("hyb_rg_b_a", hyb_rg_b_a, g_ba, m_hyb_rg_b_a, v_hyb_rg_b_a),
        ("hyb_rg_w_x", hyb_rg_w_x, g_wx[None], m_hyb_rg_w_x, v_hyb_rg_w_x),
        ("hyb_rg_b_x", hyb_rg_b_x, g_bx, m_hyb_rg_b_x, v_hyb_rg_b_x),
        ("hyb_rg_lambda", hyb_rg_lambda, g_lam, m_hyb_rg_lambda, v_hyb_rg_lambda),
    ]
    to2d = lambda a: a.reshape(-1, a.shape[-1])
    small_res = _adamw_small("adamw_small", [tuple(to2d(a) for a in (w, g, m, v)) for _, w, g, m, v in small])
    out = {}
    for (nm, w, g, _, _), (dl, nmom, nvar) in zip(small, small_res):
        out[nm] = (g, dl.reshape(w.shape), nmom.reshape(w.shape), nvar.reshape(w.shape))

    stacked = {
        "mlp_w_down": (mlp_w_down, m_mlp_w_down, v_mlp_w_down, ["mlp_w_down0", "mlp_w_down1"]),
        "mlp_w_up": (mlp_w_up, m_mlp_w_up, v_mlp_w_up, ["mlp_w_up0", "mlp_w_up1"]),
        "sb_w_o": (sb_w_o, m_sb_w_o, v_sb_w_o, ["sb_w_o"]),
        "sb_w_qkv": (sb_w_qkv, m_sb_w_qkv, v_sb_w_qkv, ["sb_w_qkv"]),
        "hyb_w_out": (hyb_w_out, m_hyb_w_out, v_hyb_w_out, ["hyb_w_out"]),
        "hyb_w_in": (hyb_w_in, m_hyb_w_in, v_hyb_w_in, ["hyb_w_in"]),
    }

    def update(k, token):
        w, m, v, parts = stacked[k]
        out[k] = tuple(_adamw("adamw_" + k, w, [after(gfull[p], token) for p in parts], m, v))
        return out[k][1]

    token = small_res[0][0]
    token = update("sb_w_qkv", token)
    token = update("sb_w_o", token)
    token = update("mlp_w_down", token)
    token = reduce_finish("hyb_w_out", token)
    token = update("mlp_w_up", token)
    token = reduce_finish("hyb_w_in", token)
    token = update("hyb_w_out", token)
    update("hyb_w_in", token)

    order = ["norm_gains", "hyb_w_in", "hyb_conv_a", "hyb_conv_b", "hyb_conv_b_bias", "hyb_rg_w_a", "hyb_rg_b_a",
             "hyb_rg_w_x", "hyb_rg_b_x", "hyb_rg_lambda", "hyb_w_out", "sb_w_qkv", "sb_w_o", "mlp_w_up",
             "mlp_w_down"]
    return (loss, dx0[None], *[out[k][0] for k in order], *[out[k][1] for k in order],
            *[out[k][2] for k in order], *[out[k][3] for k in order])
```
